```python
import math
import jax, jax.numpy as jnp
from jax import lax
import numpy as np

D_MODEL = 2048
BATCH = 8
SEQ = 8192
DEPTH = 1

MEM_LEN = 256
D_MIX = D_MODEL
POOL_WIDTH = D_MIX // 4
POOL_WINDOWS = (2, 4, 8, 16)
POOL_GROUPS = len(POOL_WINDOWS)
POOL_GROUP_DIM = POOL_WIDTH // POOL_GROUPS
MLA_V_DIM = 128
MLA_HEADS = (D_MIX // 2) // MLA_V_DIM
MLA_NOPE_DIM = 128
MLA_ROPE_DIM = 64
MLA_QK_DIM = MLA_NOPE_DIM + MLA_ROPE_DIM
Q_LORA_RANK = 512
KV_LORA_RANK = 256
X_HEADS = 4
X_WIDTH = D_MIX // 4
X_HEAD_DIM = X_WIDTH // X_HEADS
D_FF = 5632
CONV_WIDTH = 3
ROPE_THETA = 10000.0
NORM_EPS = 1e-6
Q_BLOCK = 128

IN_COLS = POOL_WIDTH + Q_LORA_RANK + KV_LORA_RANK + MLA_ROPE_DIM + X_WIDTH
IN_SPLITS = (
    POOL_WIDTH,
    POOL_WIDTH + Q_LORA_RANK,
    POOL_WIDTH + Q_LORA_RANK + KV_LORA_RANK,
    POOL_WIDTH + Q_LORA_RANK + KV_LORA_RANK + MLA_ROPE_DIM,
)

kernel_name = "hybrid_pool_mla_memxattn_convglu"


def rms_norm(x, g):
    xf = x.astype(jnp.float32)
    y = xf * lax.rsqrt(jnp.mean(xf * xf, axis=-1, keepdims=True) + NORM_EPS)
    return (y * g.astype(jnp.float32)).astype(x.dtype)


def apply_rope(x, pos):
    half = x.shape[-1] // 2
    inv_freq = 1.0 / (ROPE_THETA ** (jnp.arange(half, dtype=jnp.float32) / half))
    ang = pos.astype(jnp.float32)[:, None] * inv_freq[None, :]
    cos = jnp.cos(ang)[None, :, None, :]
    sin = jnp.sin(ang)[None, :, None, :]
    xf = x.astype(jnp.float32)
    x1, x2 = xf[..., :half], xf[..., half:]
    return jnp.concatenate([x1 * cos - x2 * sin, x2 * cos + x1 * sin], axis=-1).astype(x.dtype)


def pool_mixer(p, w_pool, pool_scale):
    B, S, _ = p.shape
    pf = p.astype(jnp.float32).reshape(B, S, POOL_GROUPS, POOL_GROUP_DIM)
    csum = jnp.cumsum(pf, axis=1)
    t = jnp.arange(S)
    outs = []
    for gi, w in enumerate(POOL_WINDOWS):
        cg = csum[:, :, gi]
        lag = jnp.pad(cg, ((0, 0), (w, 0), (0, 0)))[:, :S]
        cnt = jnp.minimum(t + 1, w).astype(jnp.float32)[None, :, None]
        outs.append((cg - lag) / cnt - pf[:, :, gi])
    d = jnp.stack(outs, axis=2).astype(p.dtype)
    y = jnp.einsum('bsgc,gcd->bsgd', d, w_pool).reshape(B, S, POOL_WIDTH)
    return y * pool_scale


def causal_attention_blocks(q, k, v):
    B, S, H, D = q.shape
    Dv = v.shape[-1]
    nb = S // Q_BLOCK
    scale = 1.0 / math.sqrt(D)
    qb = q.reshape(B, nb, Q_BLOCK, H, D).transpose(1, 0, 2, 3, 4)
    kpos = jnp.arange(S)

    def one_block(args):
        qblk, i = args
        qpos = i * Q_BLOCK + jnp.arange(Q_BLOCK)
        s = jnp.einsum('bqhd,bkhd->bhqk', qblk, k).astype(jnp.float32) * scale
        mask = kpos[None, :] <= qpos[:, None]
        s = jnp.where(mask[None, None], s, -jnp.inf)
        pr = jax.nn.softmax(s, axis=-1).astype(v.dtype)
        return jnp.einsum('bhqk,bkhd->bqhd', pr, v)

    o = lax.map(one_block, (qb, jnp.arange(nb)))
    return o.transpose(1, 0, 2, 3, 4).reshape(B, S, H, Dv)


def mla_mixer(q_lat, kv_lat, k_rope, pos, g_q_lat, w_q_up, g_kv_lat, w_kv_up, g_q_mla, g_k_mla):
    B, S, _ = q_lat.shape
    q = (rms_norm(q_lat, g_q_lat) @ w_q_up).reshape(B, S, MLA_HEADS, MLA_QK_DIM)
    kv = (rms_norm(kv_lat, g_kv_lat) @ w_kv_up).reshape(B, S, MLA_HEADS, MLA_NOPE_DIM + MLA_V_DIM)
    k_nope, v = kv[..., :MLA_NOPE_DIM], kv[..., MLA_NOPE_DIM:]
    k_r = jnp.broadcast_to(k_rope[:, :, None, :], (B, S, MLA_HEADS, MLA_ROPE_DIM))
    k = jnp.concatenate([k_nope, k_r], axis=-1)
    q = rms_norm(q, g_q_mla)
    k = rms_norm(k, g_k_mla)
    q = jnp.concatenate([q[..., :MLA_NOPE_DIM], apply_rope(q[..., MLA_NOPE_DIM:], pos)], axis=-1)
    k = jnp.concatenate([k[..., :MLA_NOPE_DIM], apply_rope(k[..., MLA_NOPE_DIM:], pos)], axis=-1)
    o = causal_attention_blocks(q, k, v)
    return o.reshape(B, S, MLA_HEADS * MLA_V_DIM)


def memory_cross_attention(xq, mem, g_mem, w_mem_kv, g_q_x, g_k_x):
    B, S, _ = xq.shape
    M = mem.shape[1]
    q = rms_norm(xq.reshape(B, S, X_HEADS, X_HEAD_DIM), g_q_x)
    mkv = rms_norm(mem, g_mem) @ w_mem_kv
    k = rms_norm(mkv[..., :X_WIDTH].reshape(B, M, X_HEADS, X_HEAD_DIM), g_k_x)
    v = mkv[..., X_WIDTH:].reshape(B, M, X_HEADS, X_HEAD_DIM)
    s = jnp.einsum('bshd,bmhd->bhsm', q, k).astype(jnp.float32) * (1.0 / math.sqrt(X_HEAD_DIM))
    pr = jax.nn.softmax(s, axis=-1).astype(v.dtype)
    o = jnp.einsum('bhsm,bmhd->bshd', pr, v)
    return o.reshape(B, S, X_WIDTH)


def conv_glu_ffn(h, w_gate, w_up, conv_w, conv_b, w_down):
    S = h.shape[1]
    g = h @ w_gate
    gp = jnp.pad(g, ((0, 0), (CONV_WIDTH - 1, 0), (0, 0)))
    gc = conv_b
    for j in range(CONV_WIDTH):
        gc = gc + conv_w[j] * gp[:, j:j + S]
    return (jax.nn.silu(gc) * (h @ w_up)) @ w_down


def _fwd_setup_inputs(seed: int = 0) -> dict:
    key = jax.random.key(seed)
    ks = jax.random.split(key, 24)
    f32 = jnp.float32
    L = DEPTH

    def nrm(k, shape, scale):
        return jax.random.normal(k, shape, f32) * scale

    def gain(k, n):
        return 1.0 + 0.02 * jax.random.normal(k, (L, n), f32)

    return {
        "x": nrm(ks[0], (BATCH, SEQ, D_MODEL), 1.0),
        "mem": nrm(ks[1], (BATCH, MEM_LEN, D_MODEL), 1.0),
        "g_mix": gain(ks[2], D_MODEL),
        "w_in": nrm(ks[3], (L, D_MODEL, IN_COLS), D_MODEL ** -0.5),
        "g_q_lat": gain(ks[4], Q_LORA_RANK),
        "w_q_up": nrm(ks[5], (L, Q_LORA_RANK, MLA_HEADS * MLA_QK_DIM), Q_LORA_RANK ** -0.5),
        "g_kv_lat": gain(ks[6], KV_LORA_RANK),
        "w_kv_up": nrm(ks[7], (L, KV_LORA_RANK, MLA_HEADS * (MLA_NOPE_DIM + MLA_V_DIM)), KV_LORA_RANK ** -0.5),
        "g_q_mla": gain(ks[8], MLA_QK_DIM),
        "g_k_mla": gain(ks[9], MLA_QK_DIM),
        "w_pool": nrm(ks[10], (L, POOL_GROUPS, POOL_GROUP_DIM, POOL_GROUP_DIM), POOL_GROUP_DIM ** -0.5),
        "pool_scale": 1.0 + 0.1 * jax.random.normal(ks[11], (L, POOL_WIDTH), f32),
        "g_mem": gain(ks[12], D_MODEL),
        "w_mem_kv": nrm(ks[13], (L, D_MODEL, 2 * X_WIDTH), D_MODEL ** -0.5),
        "g_q_x": gain(ks[14], X_HEAD_DIM),
        "g_k_x": gain(ks[15], X_HEAD_DIM),
        "w_o": nrm(ks[16], (L, D_MIX, D_MODEL), D_MIX ** -0.5),
        "g_ffn": gain(ks[17], D_MODEL),
        "w_gate": nrm(ks[18], (L, D_MODEL, D_FF), D_MODEL ** -0.5),
        "w_up": nrm(ks[19], (L, D_MODEL, D_FF), D_MODEL ** -0.5),
        "conv_w": nrm(ks[20], (L, CONV_WIDTH, D_FF), CONV_WIDTH ** -0.5),
        "conv_b": nrm(ks[21], (L, D_FF), 0.01),
        "w_down": nrm(ks[22], (L, D_FF, D_MODEL), D_FF ** -0.5),
    }


def _fwd_reference(x, mem, g_mix, w_in, g_q_lat, w_q_up, g_kv_lat, w_kv_up, g_q_mla, g_k_mla,
              w_pool, pool_scale, g_mem, w_mem_kv, g_q_x, g_k_x, w_o, g_ffn,
              w_gate, w_up, conv_w, conv_b, w_down):
    S = x.shape[1]
    pos = jnp.arange(S)
    for l in range(DEPTH):
        h = rms_norm(x, g_mix[l])
        z = h @ w_in[l]
        z_pool, z_q, z_kv, z_kr, z_mq = jnp.split(z, IN_SPLITS, axis=-1)
        y_pool = pool_mixer(z_pool, w_pool[l], pool_scale[l])
        y_mla = mla_mixer(z_q, z_kv, z_kr, pos, g_q_lat[l], w_q_up[l], g_kv_lat[l],
                          w_kv_up[l], g_q_mla[l], g_k_mla[l])
        y_mem = memory_cross_attention(z_mq, mem, g_mem[l], w_mem_kv[l], g_q_x[l], g_k_x[l])
        x = x + jnp.concatenate([y_pool, y_mla, y_mem], axis=-1) @ w_o[l]
        x = x + conv_glu_ffn(rms_norm(x, g_ffn[l]), w_gate[l], w_up[l], conv_w[l], conv_b[l], w_down[l])
    return x


import jax as _jax
import jax.numpy as _jnp

TWIN_FORMAT = 'train_step'
FWD_PARAMS = ['x', 'mem', 'g_mix', 'w_in', 'g_q_lat', 'w_q_up', 'g_kv_lat', 'w_kv_up', 'g_q_mla', 'g_k_mla', 'w_pool', 'pool_scale', 'g_mem', 'w_mem_kv', 'g_q_x', 'g_k_x', 'w_o', 'g_ffn', 'w_gate', 'w_up', 'conv_w', 'conv_b', 'w_down']
TWIN_WEIGHTS = ['g_mix', 'w_in', 'g_q_lat', 'w_q_up', 'g_kv_lat', 'w_kv_up', 'g_q_mla', 'g_k_mla', 'w_pool', 'pool_scale', 'g_mem', 'w_mem_kv', 'g_q_x', 'g_k_x', 'w_o', 'g_ffn', 'w_gate', 'w_up', 'conv_w', 'conv_b', 'w_down']
TWIN_DIFF_INPUT = 'x'
TWIN_INPUTS = ['x', 'mem', 'g_mix', 'w_in', 'g_q_lat', 'w_q_up', 'g_kv_lat', 'w_kv_up', 'g_q_mla', 'g_k_mla', 'w_pool', 'pool_scale', 'g_mem', 'w_mem_kv', 'g_q_x', 'g_k_x', 'w_o', 'g_ffn', 'w_gate', 'w_up', 'conv_w', 'conv_b', 'w_down', 'loss_target', 'm_g_mix', 'm_w_in', 'm_g_q_lat', 'm_w_q_up', 'm_g_kv_lat', 'm_w_kv_up', 'm_g_q_mla', 'm_g_k_mla', 'm_w_pool', 'm_pool_scale', 'm_g_mem', 'm_w_mem_kv', 'm_g_q_x', 'm_g_k_x', 'm_w_o', 'm_g_ffn', 'm_w_gate', 'm_w_up', 'm_conv_w', 'm_conv_b', 'm_w_down', 'v_g_mix', 'v_w_in', 'v_g_q_lat', 'v_w_q_up', 'v_g_kv_lat', 'v_w_kv_up', 'v_g_q_mla', 'v_g_k_mla', 'v_w_pool', 'v_pool_scale', 'v_g_mem', 'v_w_mem_kv', 'v_g_q_x', 'v_g_k_x', 'v_w_o', 'v_g_ffn', 'v_w_gate', 'v_w_up', 'v_conv_w', 'v_conv_b', 'v_w_down']
TWIN_OUTPUTS = ['loss', 'grad_x', 'grad_g_mix', 'grad_w_in', 'grad_g_q_lat', 'grad_w_q_up', 'grad_g_kv_lat', 'grad_w_kv_up', 'grad_g_q_mla', 'grad_g_k_mla', 'grad_w_pool', 'grad_pool_scale', 'grad_g_mem', 'grad_w_mem_kv', 'grad_g_q_x', 'grad_g_k_x', 'grad_w_o', 'grad_g_ffn', 'grad_w_gate', 'grad_w_up', 'grad_conv_w', 'grad_conv_b', 'grad_w_down', 'delta_g_mix', 'delta_w_in', 'delta_g_q_lat', 'delta_w_q_up', 'delta_g_kv_lat', 'delta_w_kv_up', 'delta_g_q_mla', 'delta_g_k_mla', 'delta_w_pool', 'delta_pool_scale', 'delta_g_mem', 'delta_w_mem_kv', 'delta_g_q_x', 'delta_g_k_x', 'delta_w_o', 'delta_g_ffn', 'delta_w_gate', 'delta_w_up', 'delta_conv_w', 'delta_conv_b', 'delta_w_down', 'new_m_g_mix', 'new_m_w_in', 'new_m_g_q_lat', 'new_m_w_q_up', 'new_m_g_kv_lat', 'new_m_w_kv_up', 'new_m_g_q_mla', 'new_m_g_k_mla', 'new_m_w_pool', 'new_m_pool_scale', 'new_m_g_mem', 'new_m_w_mem_kv', 'new_m_g_q_x', 'new_m_g_k_x', 'new_m_w_o', 'new_m_g_ffn', 'new_m_w_gate', 'new_m_w_up', 'new_m_conv_w', 'new_m_conv_b', 'new_m_w_down', 'new_v_g_mix', 'new_v_w_in', 'new_v_g_q_lat', 'new_v_w_q_up', 'new_v_g_kv_lat', 'new_v_w_kv_up', 'new_v_g_q_mla', 'new_v_g_k_mla', 'new_v_w_pool', 'new_v_pool_scale', 'new_v_g_mem', 'new_v_w_mem_kv', 'new_v_g_q_x', 'new_v_g_k_x', 'new_v_w_o', 'new_v_g_ffn', 'new_v_w_gate', 'new_v_w_up', 'new_v_conv_w', 'new_v_conv_b', 'new_v_w_down']
TWIN_LEAF_KINDS = {'loss': 'loss', 'grad_x': 'grad_x', 'grad_g_mix': 'grad_w', 'grad_w_in': 'grad_w', 'grad_g_q_lat': 'grad_w', 'grad_w_q_up': 'grad_w', 'grad_g_kv_lat': 'grad_w', 'grad_w_kv_up': 'grad_w', 'grad_g_q_mla': 'grad_w', 'grad_g_k_mla': 'grad_w', 'grad_w_pool': 'grad_w', 'grad_pool_scale': 'grad_w', 'grad_g_mem': 'grad_w', 'grad_w_mem_kv': 'grad_w', 'grad_g_q_x': 'grad_w', 'grad_g_k_x': 'grad_w', 'grad_w_o': 'grad_w', 'grad_g_ffn': 'grad_w', 'grad_w_gate': 'grad_w', 'grad_w_up': 'grad_w', 'grad_conv_w': 'grad_w', 'grad_conv_b': 'grad_w', 'grad_w_down': 'grad_w', 'delta_g_mix': 'delta_w', 'delta_w_in': 'delta_w', 'delta_g_q_lat': 'delta_w', 'delta_w_q_up': 'delta_w', 'delta_g_kv_lat': 'delta_w', 'delta_w_kv_up': 'delta_w', 'delta_g_q_mla': 'delta_w', 'delta_g_k_mla': 'delta_w', 'delta_w_pool': 'delta_w', 'delta_pool_scale': 'delta_w', 'delta_g_mem': 'delta_w', 'delta_w_mem_kv': 'delta_w', 'delta_g_q_x': 'delta_w', 'delta_g_k_x': 'delta_w', 'delta_w_o': 'delta_w', 'delta_g_ffn': 'delta_w', 'delta_w_gate': 'delta_w', 'delta_w_up': 'delta_w', 'delta_conv_w': 'delta_w', 'delta_conv_b': 'delta_w', 'delta_w_down': 'delta_w', 'new_m_g_mix': 'new_m', 'new_m_w_in': 'new_m', 'new_m_g_q_lat': 'new_m', 'new_m_w_q_up': 'new_m', 'new_m_g_kv_lat': 'new_m', 'new_m_w_kv_up': 'new_m', 'new_m_g_q_mla': 'new_m', 'new_m_g_k_mla': 'new_m', 'new_m_w_pool': 'new_m', 'new_m_pool_scale': 'new_m', 'new_m_g_mem': 'new_m', 'new_m_w_mem_kv': 'new_m', 'new_m_g_q_x': 'new_m', 'new_m_g_k_x': 'new_m', 'new_m_w_o': 'new_m', 'new_m_g_ffn': 'new_m', 'new_m_w_gate': 'new_m', 'new_m_w_up': 'new_m', 'new_m_conv_w': 'new_m', 'new_m_conv_b': 'new_m', 'new_m_w_down': 'new_m', 'new_v_g_mix': 'new_v', 'new_v_w_in': 'new_v', 'new_v_g_q_lat': 'new_v', 'new_v_w_q_up': 'new_v', 'new_v_g_kv_lat': 'new_v', 'new_v_w_kv_up': 'new_v', 'new_v_g_q_mla': 'new_v', 'new_v_g_k_mla': 'new_v', 'new_v_w_pool': 'new_v', 'new_v_pool_scale': 'new_v', 'new_v_g_mem': 'new_v', 'new_v_w_mem_kv': 'new_v', 'new_v_g_q_x': 'new_v', 'new_v_g_k_x': 'new_v', 'new_v_w_o': 'new_v', 'new_v_g_ffn': 'new_v', 'new_v_w_gate': 'new_v', 'new_v_w_up': 'new_v', 'new_v_conv_w': 'new_v', 'new_v_conv_b': 'new_v', 'new_v_w_down': 'new_v'}


def _forward(args):
    return _fwd_reference(*[args[k] for k in FWD_PARAMS])


def _output_shape():
    def fwd():
        inp = _fwd_setup_inputs(0)
        return _fwd_reference(*[inp[k] for k in FWD_PARAMS])
    out = _jax.eval_shape(fwd)
    return out.shape, out.dtype

N_MICROBATCH = 1
ADAM_LR = 0.001
ADAM_B1 = 0.9
ADAM_B2 = 0.999
ADAM_EPS = 1e-08
ADAM_WD = 0.01
ADAM_STEP = 10
PER_EXAMPLE_BATCH_AXIS = {'x': 0, 'mem': 0, 'loss_target': 0}
SHARED_INPUTS = []
_WEIGHT_DTYPES = {'g_mix': _jnp.float32, 'w_in': _jnp.float32, 'g_q_lat': _jnp.float32, 'w_q_up': _jnp.float32, 'g_kv_lat': _jnp.float32, 'w_kv_up': _jnp.float32, 'g_q_mla': _jnp.float32, 'g_k_mla': _jnp.float32, 'w_pool': _jnp.float32, 'pool_scale': _jnp.float32, 'g_mem': _jnp.float32, 'w_mem_kv': _jnp.float32, 'g_q_x': _jnp.float32, 'g_k_x': _jnp.float32, 'w_o': _jnp.float32, 'g_ffn': _jnp.float32, 'w_gate': _jnp.float32, 'w_up': _jnp.float32, 'conv_w': _jnp.float32, 'conv_b': _jnp.float32, 'w_down': _jnp.float32}
MOMENT_SCALE = {'g_mix': 6.211378e+00, 'w_in': 3.644340e-01, 'g_q_lat': 8.700314e-02, 'w_q_up': 5.080506e-02, 'g_kv_lat': 1.156219e+00, 'w_kv_up': 7.033101e-02, 'g_q_mla': 8.586632e-01, 'g_k_mla': 8.580640e-01, 'w_pool': 2.249326e+00, 'pool_scale': 2.557634e+01, 'g_mem': 9.978951e-02, 'w_mem_kv': 9.005862e-02, 'g_q_x': 1.165099e+00, 'g_k_x': 1.166233e+00, 'w_o': 4.111911e-01, 'g_ffn': 2.569244e+01, 'w_gate': 1.490429e-01, 'w_up': 1.545268e-01, 'conv_w': 2.888220e+00, 'conv_b': 3.480764e+00, 'w_down': 2.233192e-01}


def _to_microbatches(a, axis):
    t = _jnp.moveaxis(a, axis, 0)
    t = t.reshape((N_MICROBATCH, t.shape[0] // N_MICROBATCH) + t.shape[1:])
    return _jnp.moveaxis(t, 1, axis + 1)


def setup_inputs(seed: int = 0) -> dict:
    inp = _fwd_setup_inputs(seed)
    key = _jax.random.fold_in(_jax.random.key(seed), 7919)
    shape, _ = _output_shape()
    out = dict(inp)
    out["loss_target"] = _jax.random.normal(_jax.random.fold_in(key, 0), shape, _jnp.float32)
    for i, name in enumerate(TWIN_WEIGHTS):
        w = inp[name].astype(_jnp.float32)
        if MOMENT_SCALE is None:
            s = _jnp.sqrt(_jnp.mean(_jnp.square(w)) + 1e-30)
        else:
            s = MOMENT_SCALE[name]
        km, kv = _jax.random.split(_jax.random.fold_in(key, i + 1))
        out[name] = w
        out["m_" + name] = s * _jax.random.normal(km, w.shape, _jnp.float32)
        out["v_" + name] = (s * s) * _jax.random.uniform(kv, w.shape, _jnp.float32, 0.5, 1.5)
    if N_MICROBATCH > 1:
        for name, axis in PER_EXAMPLE_BATCH_AXIS.items():
            out[name] = _to_microbatches(out[name], axis)
    return {'x': out['x'], 'mem': out['mem'], 'g_mix': out['g_mix'], 'w_in': out['w_in'], 'g_q_lat': out['g_q_lat'], 'w_q_up': out['w_q_up'], 'g_kv_lat': out['g_kv_lat'], 'w_kv_up': out['w_kv_up'], 'g_q_mla': out['g_q_mla'], 'g_k_mla': out['g_k_mla'], 'w_pool': out['w_pool'], 'pool_scale': out['pool_scale'], 'g_mem': out['g_mem'], 'w_mem_kv': out['w_mem_kv'], 'g_q_x': out['g_q_x'], 'g_k_x': out['g_k_x'], 'w_o': out['w_o'], 'g_ffn': out['g_ffn'], 'w_gate': out['w_gate'], 'w_up': out['w_up'], 'conv_w': out['conv_w'], 'conv_b': out['conv_b'], 'w_down': out['w_down'], 'loss_target': out['loss_target'], 'm_g_mix': out['m_g_mix'], 'm_w_in': out['m_w_in'], 'm_g_q_lat': out['m_g_q_lat'], 'm_w_q_up': out['m_w_q_up'], 'm_g_kv_lat': out['m_g_kv_lat'], 'm_w_kv_up': out['m_w_kv_up'], 'm_g_q_mla': out['m_g_q_mla'], 'm_g_k_mla': out['m_g_k_mla'], 'm_w_pool': out['m_w_pool'], 'm_pool_scale': out['m_pool_scale'], 'm_g_mem': out['m_g_mem'], 'm_w_mem_kv': out['m_w_mem_kv'], 'm_g_q_x': out['m_g_q_x'], 'm_g_k_x': out['m_g_k_x'], 'm_w_o': out['m_w_o'], 'm_g_ffn': out['m_g_ffn'], 'm_w_gate': out['m_w_gate'], 'm_w_up': out['m_w_up'], 'm_conv_w': out['m_conv_w'], 'm_conv_b': out['m_conv_b'], 'm_w_down': out['m_w_down'], 'v_g_mix': out['v_g_mix'], 'v_w_in': out['v_w_in'], 'v_g_q_lat': out['v_g_q_lat'], 'v_w_q_up': out['v_w_q_up'], 'v_g_kv_lat': out['v_g_kv_lat'], 'v_w_kv_up': out['v_w_kv_up'], 'v_g_q_mla': out['v_g_q_mla'], 'v_g_k_mla': out['v_g_k_mla'], 'v_w_pool': out['v_w_pool'], 'v_pool_scale': out['v_pool_scale'], 'v_g_mem': out['v_g_mem'], 'v_w_mem_kv': out['v_w_mem_kv'], 'v_g_q_x': out['v_g_q_x'], 'v_g_k_x': out['v_g_k_x'], 'v_w_o': out['v_w_o'], 'v_g_ffn': out['v_g_ffn'], 'v_w_gate': out['v_w_gate'], 'v_w_up': out['v_w_up'], 'v_conv_w': out['v_conv_w'], 'v_conv_b': out['v_conv_b'], 'v_w_down': out['v_w_down']}


def _loss(weights, diff, rest, loss_target):
    with _jax.named_scope("forward"):
        args = {**rest, TWIN_DIFF_INPUT: diff, **{k: w.astype(_WEIGHT_DTYPES[k]) for k, w in weights.items()}}
        y = _forward(args)
    with _jax.named_scope("loss_head"):
        err = _jnp.square(y.astype(_jnp.float32) - loss_target)
        return 0.5 * _jnp.sum(_jnp.mean(err, axis=-1)) if err.ndim else 0.5 * err


def _adamw(w, g, m, v):
    m = ADAM_B1 * m + (1.0 - ADAM_B1) * g
    v = ADAM_B2 * v + (1.0 - ADAM_B2) * _jnp.square(g)
    m_hat = m / (1.0 - ADAM_B1 ** ADAM_STEP)
    v_hat = v / (1.0 - ADAM_B2 ** ADAM_STEP)
    delta = -ADAM_LR * (m_hat / (_jnp.sqrt(v_hat) + ADAM_EPS) + ADAM_WD * w)
    return delta, m, v


def reference(x, mem, g_mix, w_in, g_q_lat, w_q_up, g_kv_lat, w_kv_up, g_q_mla, g_k_mla, w_pool, pool_scale, g_mem, w_mem_kv, g_q_x, g_k_x, w_o, g_ffn, w_gate, w_up, conv_w, conv_b, w_down, loss_target, m_g_mix, m_w_in, m_g_q_lat, m_w_q_up, m_g_kv_lat, m_w_kv_up, m_g_q_mla, m_g_k_mla, m_w_pool, m_pool_scale, m_g_mem, m_w_mem_kv, m_g_q_x, m_g_k_x, m_w_o, m_g_ffn, m_w_gate, m_w_up, m_conv_w, m_conv_b, m_w_down, v_g_mix, v_w_in, v_g_q_lat, v_w_q_up, v_g_kv_lat, v_w_kv_up, v_g_q_mla, v_g_k_mla, v_w_pool, v_pool_scale, v_g_mem, v_w_mem_kv, v_g_q_x, v_g_k_x, v_w_o, v_g_ffn, v_w_gate, v_w_up, v_conv_w, v_conv_b, v_w_down):
    given = dict(x=x, mem=mem, g_mix=g_mix, w_in=w_in, g_q_lat=g_q_lat, w_q_up=w_q_up, g_kv_lat=g_kv_lat, w_kv_up=w_kv_up, g_q_mla=g_q_mla, g_k_mla=g_k_mla, w_pool=w_pool, pool_scale=pool_scale, g_mem=g_mem, w_mem_kv=w_mem_kv, g_q_x=g_q_x, g_k_x=g_k_x, w_o=w_o, g_ffn=g_ffn, w_gate=w_gate, w_up=w_up, conv_w=conv_w, conv_b=conv_b, w_down=w_down, loss_target=loss_target, m_g_mix=m_g_mix, m_w_in=m_w_in, m_g_q_lat=m_g_q_lat, m_w_q_up=m_w_q_up, m_g_kv_lat=m_g_kv_lat, m_w_kv_up=m_w_kv_up, m_g_q_mla=m_g_q_mla, m_g_k_mla=m_g_k_mla, m_w_pool=m_w_pool, m_pool_scale=m_pool_scale, m_g_mem=m_g_mem, m_w_mem_kv=m_w_mem_kv, m_g_q_x=m_g_q_x, m_g_k_x=m_g_k_x, m_w_o=m_w_o, m_g_ffn=m_g_ffn, m_w_gate=m_w_gate, m_w_up=m_w_up, m_conv_w=m_conv_w, m_conv_b=m_conv_b, m_w_down=m_w_down, v_g_mix=v_g_mix, v_w_in=v_w_in, v_g_q_lat=v_g_q_lat, v_w_q_up=v_w_q_up, v_g_kv_lat=v_g_kv_lat, v_w_kv_up=v_w_kv_up, v_g_q_mla=v_g_q_mla, v_g_k_mla=v_g_k_mla, v_w_pool=v_w_pool, v_pool_scale=v_pool_scale, v_g_mem=v_g_mem, v_w_mem_kv=v_w_mem_kv, v_g_q_x=v_g_q_x, v_g_k_x=v_g_k_x, v_w_o=v_w_o, v_g_ffn=v_g_ffn, v_w_gate=v_w_gate, v_w_up=v_w_up, v_conv_w=v_conv_w, v_conv_b=v_conv_b, v_w_down=v_w_down)
    weights = {n: given[n] for n in TWIN_WEIGHTS}
    shared = {n: given[n] for n in SHARED_INPUTS}
    per_example = {n: given[n] for n in ['x', 'mem']}
    grad_fn = _jax.value_and_grad(_loss, argnums=(0, 1))

    def one_microbatch(ex, loss_target):
        ex = dict(ex)
        diff = ex.pop(TWIN_DIFF_INPUT)
        return grad_fn(weights, diff, {**shared, **ex}, loss_target)

    if N_MICROBATCH == 1:
        loss, (grad_w, grad_x) = one_microbatch(per_example, given["loss_target"])
    else:
        def body(carry, xs):
            loss_sum, grad_sum = carry
            l_k, (gw_k, gx_k) = one_microbatch(xs[0], xs[1])
            with _jax.named_scope("update"):
                return (loss_sum + l_k, _jax.tree.map(_jnp.add, grad_sum, gw_k)), gx_k

        init = (_jnp.zeros((), _jnp.float32), _jax.tree.map(_jnp.zeros_like, weights))
        (loss, grad_w), grad_x = _jax.lax.scan(body, init, (per_example, given["loss_target"]))
    with _jax.named_scope("update"):
        delta_w, new_m, new_v = {}, {}, {}
        for n in TWIN_WEIGHTS:
            delta_w[n], new_m[n], new_v[n] = _adamw(weights[n], grad_w[n], given["m_" + n], given["v_" + n])
    return (loss, grad_x, *[grad_w[n] for n in TWIN_WEIGHTS], *[delta_w[n] for n in TWIN_WEIGHTS],
            *[new_m[n] for n in TWIN_WEIGHTS], *[new_v[n] for n in TWIN_WEIGHTS])
```

```python
import math

import jax
import jax.numpy as jnp
from jax import lax
from jax.experimental import pallas as pl
from jax.experimental.pallas import tpu as pltpu

F32 = jnp.float32
BF16 = jnp.bfloat16

D_MODEL = 2048
D_FF = 5632
POOL_WIDTH = 512
POOL_WINDOWS = (2, 4, 8, 16)
POOL_HALO = 16
MLA_HEADS = 8
MLA_NOPE = 128
MLA_ROPE = 64
MLA_QK = MLA_NOPE + MLA_ROPE
MLA_QK_PAD = 256
MLA_V = 128
Q_LORA = 512
KV_LORA = 256
X_HEADS = 4
X_HEAD_DIM = 128
X_WIDTH = 512
MEM_LEN = 256
ROPE_THETA = 10000.0
NORM_EPS = 1e-6
CONV_HALO = 8
IN_COLS = 1856
Z_COLS = 1920
Z_POOL, Z_Q, Z_MQ, Z_KV, Z_KR = 0, 512, 1024, 1536, 1792
WO_BLOCK_ORDER = (2, 3, 4, 5, 0, 1, 6, 7)

ADAM_LR = 0.001
ADAM_B1 = 0.9
ADAM_B2 = 0.999
ADAM_EPS = 1e-08
ADAM_WD = 0.01
ADAM_STEP = 10

N_DEV = 8
LANES = 128
VMEM_LIMIT = 56 * 1024 * 1024

BIG = ("w_in", "w_q_up", "w_kv_up", "w_mem_kv", "w_o", "w_gate", "w_up", "conv_w", "w_down")
SMALL = ("g_mix", "g_q_lat", "g_kv_lat", "g_q_mla", "g_k_mla", "w_pool", "pool_scale", "g_mem",
         "g_q_x", "g_k_x", "g_ffn", "conv_b")
WEIGHTS = ("g_mix", "w_in", "g_q_lat", "w_q_up", "g_kv_lat", "w_kv_up", "g_q_mla", "g_k_mla", "w_pool",
           "pool_scale", "g_mem", "w_mem_kv", "g_q_x", "g_k_x", "w_o", "g_ffn", "w_gate", "w_up", "conv_w",
           "conv_b", "w_down")


def _tile(n, t):
    if n <= t:
        return n
    for c in range(t - t % LANES, 0, -LANES):
        if n % c == 0:
            return c
    return n


def _params(*sem):
    return pltpu.CompilerParams(dimension_semantics=sem, vmem_limit_bytes=VMEM_LIMIT)


def _full(shape):
    nd = len(shape)
    return pl.BlockSpec(shape, lambda *_: (0,) * nd)


def _rows(tr, w, cb=0):
    return pl.BlockSpec((tr, w), lambda i: (i, cb))


def _rms_hat(x, n):
    r = lax.rsqrt(jnp.sum(x * x, axis=-1, keepdims=True) / n + NORM_EPS)
    return x * r, r


def _rms_bwd(dy, xhat, r, g, n):
    dxh = dy * g
    dx = r * (dxh - xhat * (jnp.sum(dxh * xhat, axis=-1, keepdims=True) / n))
    return dx, jnp.sum(dy * xhat, axis=0, keepdims=True)


def _mm(name, a, b, *, ta=False, tb=False, add=None, out_dtype=F32, tm=512, tn=512, tk=512):
    if ta:
        K, M = a.shape
    else:
        M, K = a.shape
    if tb:
        N, K2 = b.shape
    else:
        K2, N = b.shape
    assert K == K2, (name, a.shape, b.shape)
    tm, tn, tk = _tile(M, tm), _tile(N, tn), _tile(K, tk)
    nk = K // tk
    a_spec = pl.BlockSpec((tk, tm), lambda i, j, k: (k, i)) if ta else pl.BlockSpec((tm, tk), lambda i, j, k: (i, k))
    b_spec = pl.BlockSpec((tn, tk), lambda i, j, k: (j, k)) if tb else pl.BlockSpec((tk, tn), lambda i, j, k: (k, j))
    o_spec = pl.BlockSpec((tm, tn), lambda i, j, k: (i, j))
    dims = (((0,) if ta else (1,), (1,) if tb else (0,)), ((), ()))
    has_add = add is not None

    def body(*refs):
        if has_add:
            a_ref, b_ref, add_ref, o_ref = refs[:4]
            scr = refs[4:]
        else:
            a_ref, b_ref, o_ref = refs[:3]
            add_ref = None
            scr = refs[3:]
        part = lax.dot_general(a_ref[...], b_ref[...], dims, preferred_element_type=F32)
        if nk == 1:
            if has_add:
                part = part + add_ref[...]
            o_ref[...] = part.astype(o_ref.dtype)
            return
        acc = scr[0]
        k = pl.program_id(2)

        @pl.when(k == 0)
        def _():
            acc[...] = part

        @pl.when(k > 0)
        def _():
            acc[...] += part

        @pl.when(k == nk - 1)
        def _():
            r = acc[...]
            if has_add:
                r = r + add_ref[...]
            o_ref[...] = r.astype(o_ref.dtype)

    ins = [a, b] + ([add] if has_add else [])
    in_specs = [a_spec, b_spec] + ([o_spec] if has_add else [])
    return pl.pallas_call(
        body, name=name, grid=(M // tm, N // tn, nk),
        in_specs=in_specs, out_specs=o_spec,
        out_shape=jax.ShapeDtypeStruct((M, N), out_dtype),
        scratch_shapes=[pltpu.VMEM((tm, tn), F32)] if nk > 1 else [],
        compiler_params=_params("parallel", "parallel", "arbitrary"),
    )(*ins)


def _rms_fwd(name, x, g, tr=512):
    S, n = x.shape
    tr = _tile(S, tr)

    def body(x_ref, g_ref, h_ref):
        xhat, _ = _rms_hat(x_ref[...], n)
        h_ref[...] = (xhat * g_ref[...]).astype(BF16)

    return pl.pallas_call(
        body, name=name, grid=(S // tr,),
        in_specs=[_rows(tr, n), _full((1, n))], out_specs=_rows(tr, n),
        out_shape=jax.ShapeDtypeStruct((S, n), BF16),
        compiler_params=_params("parallel"),
    )(x, g)


def _rms_bwd_call(name, x, g, dy, extra, want_bf16, tr=256):
    S, n = x.shape
    tr = _tile(S, tr)

    def body(x_ref, g_ref, dy_ref, ex_ref, *outs):
        xhat, r = _rms_hat(x_ref[...], n)
        dx, dg = _rms_bwd(dy_ref[...], xhat, r, g_ref[...], n)
        dx = dx + ex_ref[...]
        outs[0][...] = dx
        if want_bf16:
            outs[1][...] = dx.astype(BF16)
        dg_ref = outs[-1]

        @pl.when(pl.program_id(0) == 0)
        def _():
            dg_ref[...] = jnp.zeros_like(dg_ref)

        dg_ref[...] += dg

    out_shape = [jax.ShapeDtypeStruct((S, n), F32)]
    out_specs = [_rows(tr, n)]
    if want_bf16:
        out_shape.append(jax.ShapeDtypeStruct((S, n), BF16))
        out_specs.append(_rows(tr, n))
    out_shape.append(jax.ShapeDtypeStruct((1, n), F32))
    out_specs.append(_full((1, n)))
    return pl.pallas_call(
        body, name=name, grid=(S // tr,),
        in_specs=[_rows(tr, n), _full((1, n)), _rows(tr, n), _rows(tr, n)],
        out_specs=out_specs, out_shape=out_shape,
        compiler_params=_params("arbitrary"),
    )(x, g, dy, extra)


def _lat_fwd(z, gq, gkv, tr=512):
    S = z.shape[0]
    tr = _tile(S, tr)

    def body(zq_ref, zkv_ref, gq_ref, gkv_ref, ql_ref, kvl_ref):
        xq, _ = _rms_hat(zq_ref[...], Q_LORA)
        ql_ref[...] = (xq * gq_ref[...]).astype(BF16)
        xkv, _ = _rms_hat(zkv_ref[...], KV_LORA)
        kvl_ref[...] = (xkv * gkv_ref[...]).astype(BF16)

    return pl.pallas_call(
        body, name="lat_fwd", grid=(S // tr,),
        in_specs=[_rows(tr, Q_LORA, Z_Q // Q_LORA), _rows(tr, KV_LORA, Z_KV // KV_LORA),
                  _full((1, Q_LORA)), _full((1, KV_LORA))],
        out_specs=[_rows(tr, Q_LORA), _rows(tr, KV_LORA)],
        out_shape=[jax.ShapeDtypeStruct((S, Q_LORA), BF16), jax.ShapeDtypeStruct((S, KV_LORA), BF16)],
        compiler_params=_params("parallel"),
    )(z, z, gq, gkv)


def _lat_bwd(z, gq, gkv, dql, dkvl, tr=512):
    S = z.shape[0]
    tr = _tile(S, tr)

    def body(zq_ref, zkv_ref, gq_ref, gkv_ref, dql_ref, dkvl_ref, dzq_ref, dzkv_ref, dgq_ref, dgkv_ref):
        @pl.when(pl.program_id(0) == 0)
        def _():
            dgq_ref[...] = jnp.zeros_like(dgq_ref)
            dgkv_ref[...] = jnp.zeros_like(dgkv_ref)

        xq, rq = _rms_hat(zq_ref[...], Q_LORA)
        dx, dg = _rms_bwd(dql_ref[...], xq, rq, gq_ref[...], Q_LORA)
        dzq_ref[...] = dx.astype(BF16)
        dgq_ref[...] += dg
        xkv, rkv = _rms_hat(zkv_ref[...], KV_LORA)
        dx, dg = _rms_bwd(dkvl_ref[...], xkv, rkv, gkv_ref[...], KV_LORA)
        dzkv_ref[...] = dx.astype(BF16)
        dgkv_ref[...] += dg

    return pl.pallas_call(
        body, name="lat_bwd", grid=(S // tr,),
        in_specs=[_rows(tr, Q_LORA, Z_Q // Q_LORA), _rows(tr, KV_LORA, Z_KV // KV_LORA),
                  _full((1, Q_LORA)), _full((1, KV_LORA)), _rows(tr, Q_LORA), _rows(tr, KV_LORA)],
        out_specs=[_rows(tr, Q_LORA), _rows(tr, KV_LORA), _full((1, Q_LORA)), _full((1, KV_LORA))],
        out_shape=[jax.ShapeDtypeStruct((S, Q_LORA), BF16), jax.ShapeDtypeStruct((S, KV_LORA), BF16),
                   jax.ShapeDtypeStruct((1, Q_LORA), F32), jax.ShapeDtypeStruct((1, KV_LORA), F32)],
        compiler_params=_params("arbitrary"),
    )(z, z, gq, gkv, dql, dkvl)


def _pool_d(ext, cur, t0, tr):
    t = t0 + lax.broadcasted_iota(jnp.int32, (tr, 1), 0)
    ds = []
    for gi, w in enumerate(POOL_WINDOWS):
        s = ext[:, gi * LANES:(gi + 1) * LANES]
        sh = 1
        while sh < w:
            s = s + pltpu.roll(s, sh, 0)
            sh *= 2
        cnt = jnp.minimum(t + 1, w).astype(F32)
        ds.append(s[POOL_HALO:] / cnt - cur[:, gi * LANES:(gi + 1) * LANES])
    return ds


def _pool_fwd(z, wp, scale, tr=512):
    S = z.shape[0]
    tr = _tile(S, tr)
    rb = tr // POOL_HALO

    def body(z_ref, zp_ref, wp_ref, sc_ref, y_ref):
        i = pl.program_id(0)
        cur = z_ref[...]
        halo = jnp.where(i == 0, 0.0, zp_ref[...])
        ds = _pool_d(jnp.concatenate([halo, cur], axis=0), cur, i * tr, tr)
        for gi in range(len(POOL_WINDOWS)):
            y = jnp.dot(ds[gi].astype(BF16), wp_ref[gi], preferred_element_type=F32)
            y_ref[:, gi * LANES:(gi + 1) * LANES] = (y * sc_ref[:, gi * LANES:(gi + 1) * LANES]).astype(BF16)

    return pl.pallas_call(
        body, name="pool_fwd", grid=(S // tr,),
        in_specs=[_rows(tr, POOL_WIDTH, Z_POOL // POOL_WIDTH),
                  pl.BlockSpec((POOL_HALO, POOL_WIDTH), lambda i: (jnp.maximum(i * rb - 1, 0), 0)),
                  _full(wp.shape), _full((1, POOL_WIDTH))],
        out_specs=_rows(tr, POOL_WIDTH),
        out_shape=jax.ShapeDtypeStruct((S, POOL_WIDTH), BF16),
        compiler_params=_params("parallel"),
    )(z, z, wp, scale)


def _pool_bwd(z, dyc, wp, scale, dy_cb, tr=512):
    S = z.shape[0]
    tr = _tile(S, tr)
    rb = tr // POOL_HALO
    nhb = S // POOL_HALO
    ng = len(POOL_WINDOWS)

    def body(z_ref, zp_ref, dy_ref, dyn_ref, wp_ref, sc_ref, dz_ref, dwp_ref, dsc_ref):
        i = pl.program_id(0)

        @pl.when(i == 0)
        def _():
            dwp_ref[...] = jnp.zeros_like(dwp_ref)
            dsc_ref[...] = jnp.zeros_like(dsc_ref)

        cur = z_ref[...]
        halo = jnp.where(i == 0, 0.0, zp_ref[...])
        ds = _pool_d(jnp.concatenate([halo, cur], axis=0), cur, i * tr, tr)
        dy_cur = dy_ref[...]
        dy_next = jnp.where(i == pl.num_programs(0) - 1, 0.0, dyn_ref[...])
        dy_ext = jnp.concatenate([dy_cur, dy_next], axis=0)
        n_ext = tr + POOL_HALO
        t_ext = i * tr + lax.broadcasted_iota(jnp.int32, (n_ext, 1), 0)
        for gi, w in enumerate(POOL_WINDOWS):
            cols = slice(gi * LANES, (gi + 1) * LANES)
            d_b = ds[gi].astype(BF16)
            y_pre = jnp.dot(d_b, wp_ref[gi], preferred_element_type=F32)
            dsc_ref[:, cols] += jnp.sum(dy_cur[:, cols] * y_pre, axis=0, keepdims=True)
            dys = (dy_ext[:, cols] * sc_ref[:, cols]).astype(BF16)
            dwp_ref[gi] += lax.dot_general(d_b, dys[:tr], (((0,), (0,)), ((), ())), preferred_element_type=F32)
            dd = lax.dot_general(dys, wp_ref[gi], (((1,), (1,)), ((), ())), preferred_element_type=F32)
            s = dd / jnp.minimum(t_ext + 1, w).astype(F32)
            sh = 1
            while sh < w:
                s = s + pltpu.roll(s, n_ext - sh, 0)
                sh *= 2
            dz_ref[:, cols] = (s[:tr] - dd[:tr]).astype(BF16)

    return pl.pallas_call(
        body, name="pool_bwd", grid=(S // tr,),
        in_specs=[_rows(tr, POOL_WIDTH, Z_POOL // POOL_WIDTH),
                  pl.BlockSpec((POOL_HALO, POOL_WIDTH), lambda i: (jnp.maximum(i * rb - 1, 0), 0)),
                  _rows(tr, POOL_WIDTH, dy_cb),
                  pl.BlockSpec((POOL_HALO, POOL_WIDTH), lambda i: (jnp.minimum((i + 1) * rb, nhb - 1), dy_cb)),
                  _full(wp.shape), _full((1, POOL_WIDTH))],
        out_specs=[_rows(tr, POOL_WIDTH), _full((ng, LANES, LANES)), _full((1, POOL_WIDTH))],
        out_shape=[jax.ShapeDtypeStruct((S, POOL_WIDTH), BF16), jax.ShapeDtypeStruct((ng, LANES, LANES), F32),
                   jax.ShapeDtypeStruct((1, POOL_WIDTH), F32)],
        compiler_params=_params("arbitrary"),
    )(z, z, dyc, dyc, wp, scale)


def _rope_tables(S):
    half = MLA_ROPE // 2
    inv_freq = 1.0 / (ROPE_THETA ** (jnp.arange(half, dtype=F32) / half))
    ang = jnp.arange(S).astype(F32)[:, None] * inv_freq[None, :]
    cos, sin = jnp.cos(ang), jnp.sin(ang)
    z32 = jnp.zeros((S, half), F32)
    z64 = jnp.zeros((S, LANES - MLA_ROPE), F32)
    cos_t = jnp.concatenate([cos, cos, z64], axis=1)
    sin_a = jnp.concatenate([-sin, z32, z64], axis=1)
    sin_b = jnp.concatenate([z32, sin, z64], axis=1)
    return cos_t, sin_a, sin_b


def _rope(x, cos_t, sin_a, sin_b):
    return x * cos_t + pltpu.roll(x, LANES - 32, 1) * sin_a + pltpu.roll(x, 32, 1) * sin_b


def _rope_t(d, cos_t, sin_a, sin_b):
    return d * cos_t + pltpu.roll(d * sin_a, 32, 1) + pltpu.roll(d * sin_b, LANES - 32, 1)


def _mla_prep(q_raw, kv_raw, z, tabs, gq, gk, tr=256):
    S = z.shape[0]
    tr = _tile(S, tr)
    scale = 1.0 / math.sqrt(MLA_QK)
    W = MLA_HEADS * MLA_QK_PAD

    def body(q_ref, kv_ref, kr_ref, c_ref, sa_ref, sb_ref, gq_ref, gk_ref, qo_ref, ko_ref, vo_ref):
        tab = (c_ref[...], sa_ref[...], sb_ref[...])
        kr = kr_ref[...]
        kr_ss = jnp.sum(kr * kr, axis=-1, keepdims=True)
        gqn, gqr = gq_ref[:, :LANES], gq_ref[:, LANES:]
        gkn, gkr = gk_ref[:, :LANES], gk_ref[:, LANES:]
        for h in range(MLA_HEADS):
            c0 = h * MLA_QK_PAD
            qn = q_ref[:, c0:c0 + LANES]
            qr = q_ref[:, c0 + LANES:c0 + 2 * LANES]
            r = lax.rsqrt((jnp.sum(qn * qn, -1, keepdims=True) + jnp.sum(qr * qr, -1, keepdims=True)) / MLA_QK
                          + NORM_EPS)
            qo_ref[h, :, :LANES] = (qn * r * gqn * scale).astype(BF16)
            qo_ref[h, :, LANES:] = (_rope(qr * r * gqr, *tab) * scale).astype(BF16)
            kn = kv_ref[:, c0:c0 + LANES]
            r = lax.rsqrt((jnp.sum(kn * kn, -1, keepdims=True) + kr_ss) / MLA_QK + NORM_EPS)
            ko_ref[h, :, :LANES] = (kn * r * gkn).astype(BF16)
            ko_ref[h, :, LANES:] = _rope(kr * r * gkr, *tab).astype(BF16)
            vo_ref[h] = kv_ref[:, c0 + LANES:c0 + 2 * LANES].astype(BF16)

    hs = lambda w: pl.BlockSpec((MLA_HEADS, tr, w), lambda i: (0, i, 0))
    return pl.pallas_call(
        body, name="mla_prep", grid=(S // tr,),
        in_specs=[_rows(tr, W), _rows(tr, W), _rows(tr, LANES, Z_KR // LANES),
                  _rows(tr, LANES), _rows(tr, LANES), _rows(tr, LANES),
                  _full((1, MLA_QK_PAD)), _full((1, MLA_QK_PAD))],
        out_specs=[hs(MLA_QK_PAD), hs(MLA_QK_PAD), hs(MLA_V)],
        out_shape=[jax.ShapeDtypeStruct((MLA_HEADS, S, MLA_QK_PAD), BF16),
                   jax.ShapeDtypeStruct((MLA_HEADS, S, MLA_QK_PAD), BF16),
                   jax.ShapeDtypeStruct((MLA_HEADS, S, MLA_V), BF16)],
        compiler_params=_params("parallel"),
    )(q_raw, kv_raw, z, *tabs, gq, gk)


def _mla_prep_bwd(dq, dk, dv, q_raw, kv_raw, z, tabs, gq, gk, tr=256):
    S = z.shape[0]
    tr = _tile(S, tr)
    scale = 1.0 / math.sqrt(MLA_QK)
    W = MLA_HEADS * MLA_QK_PAD

    def body(dq_ref, dk_ref, dv_ref, q_ref, kv_ref, kr_ref, c_ref, sa_ref, sb_ref, gq_ref, gk_ref,
             dqr_ref, dkvr_ref, dkr_ref, dgq_ref, dgk_ref):
        @pl.when(pl.program_id(0) == 0)
        def _():
            dgq_ref[...] = jnp.zeros_like(dgq_ref)
            dgk_ref[...] = jnp.zeros_like(dgk_ref)

        tab = (c_ref[...], sa_ref[...], sb_ref[...])
        kr = kr_ref[...]
        kr_ss = jnp.sum(kr * kr, axis=-1, keepdims=True)
        gqn, gqr = gq_ref[:, :LANES], gq_ref[:, LANES:]
        gkn, gkr = gk_ref[:, :LANES], gk_ref[:, LANES:]
        dkr_sum = jnp.zeros((tr, LANES), F32)
        dgq_n = jnp.zeros((1, LANES), F32)
        dgq_r = jnp.zeros((1, LANES), F32)
        dgk_n = jnp.zeros((1, LANES), F32)
        dgk_r = jnp.zeros((1, LANES), F32)

        def head_norm_bwd(xn, xr, r, dyn, dyr, gn, gr):
            hn, hr = xn * r, xr * r
            dxn, dxr = dyn * gn, dyr * gr
            mt = (jnp.sum(dxn * hn, -1, keepdims=True) + jnp.sum(dxr * hr, -1, keepdims=True)) / MLA_QK
            return (r * (dxn - hn * mt), r * (dxr - hr * mt),
                    jnp.sum(dyn * hn, axis=0, keepdims=True), jnp.sum(dyr * hr, axis=0, keepdims=True))

        for h in range(MLA_HEADS):
            c0 = h * MLA_QK_PAD
            qn = q_ref[:, c0:c0 + LANES]
            qr = q_ref[:, c0 + LANES:c0 + 2 * LANES]
            r = lax.rsqrt((jnp.sum(qn * qn, -1, keepdims=True) + jnp.sum(qr * qr, -1, keepdims=True)) / MLA_QK
                          + NORM_EPS)
            dyn = dq_ref[h, :, :LANES] * scale
            dyr = _rope_t(dq_ref[h, :, LANES:] * scale, *tab)
            dxn, dxr, gn_, gr_ = head_norm_bwd(qn, qr, r, dyn, dyr, gqn, gqr)
            dgq_n += gn_
            dgq_r += gr_
            dqr_ref[:, c0:c0 + LANES] = dxn.astype(BF16)
            dqr_ref[:, c0 + LANES:c0 + 2 * LANES] = dxr.astype(BF16)

            kn = kv_ref[:, c0:c0 + LANES]
            r = lax.rsqrt((jnp.sum(kn * kn, -1, keepdims=True) + kr_ss) / MLA_QK + NORM_EPS)
            dyn = dk_ref[h, :, :LANES]
            dyr = _rope_t(dk_ref[h, :, LANES:], *tab)
            dxn, dxr, gn_, gr_ = head_norm_bwd(kn, kr, r, dyn, dyr, gkn, gkr)
            dgk_n += gn_
            dgk_r += gr_
            dkr_sum += dxr
            dkvr_ref[:, c0:c0 + LANES] = dxn.astype(BF16)
            dkvr_ref[:, c0 + LANES:c0 + 2 * LANES] = dv_ref[h].astype(BF16)

        dkr_ref[...] = dkr_sum.astype(BF16)
        dgq_ref[:, :LANES] += dgq_n
        dgq_ref[:, LANES:] += dgq_r
        dgk_ref[:, :LANES] += dgk_n
        dgk_ref[:, LANES:] += dgk_r

    hs = lambda w: pl.BlockSpec((MLA_HEADS, tr, w), lambda i: (0, i, 0))
    return pl.pallas_call(
        body, name="mla_prep_bwd", grid=(S // tr,),
        in_specs=[hs(MLA_QK_PAD), hs(MLA_QK_PAD), hs(MLA_V), _rows(tr, W), _rows(tr, W),
                  _rows(tr, LANES, Z_KR // LANES), _rows(tr, LANES), _rows(tr, LANES), _rows(tr, LANES),
                  _full((1, MLA_QK_PAD)), _full((1, MLA_QK_PAD))],
        out_specs=[_rows(tr, W), _rows(tr, W), _rows(tr, LANES), _full((1, MLA_QK_PAD)), _full((1, MLA_QK_PAD))],
        out_shape=[jax.ShapeDtypeStruct((S, W), BF16), jax.ShapeDtypeStruct((S, W), BF16),
                   jax.ShapeDtypeStruct((S, LANES), BF16),
                   jax.ShapeDtypeStruct((1, MLA_QK_PAD), F32), jax.ShapeDtypeStruct((1, MLA_QK_PAD), F32)],
        compiler_params=_params("arbitrary"),
    )(dq, dk, dv, q_raw, kv_raw, z, *tabs, gq, gk)


_NT = (((1,), (1,)), ((), ()))
_TN = (((0,), (0,)), ((), ()))


def _causal_mask(s, i, j, tq, tk):
    row = i * tq + lax.broadcasted_iota(jnp.int32, (tq, tk), 0)
    col = j * tk + lax.broadcasted_iota(jnp.int32, (tq, tk), 1)
    return jnp.where(col <= row, s, -jnp.inf)


def _flash_fwd(q, k, v, tq=512, tk=512):
    H, S, dq = q.shape
    dv = v.shape[-1]
    tq, tk = _tile(S, tq), _tile(S, tk)
    nq, nk = S // tq, S // tk

    def last_k(i):
        return ((i + 1) * tq - 1) // tk

    def body(q_ref, k_ref, v_ref, o_ref, lse_ref, m_sc, l_sc, acc_sc):
        i, j = pl.program_id(1), pl.program_id(2)

        @pl.when(j == 0)
        def _():
            m_sc[...] = jnp.full_like(m_sc, -jnp.inf)
            l_sc[...] = jnp.zeros_like(l_sc)
            acc_sc[...] = jnp.zeros_like(acc_sc)

        def step(masked):
            s = lax.dot_general(q_ref[...], k_ref[...], _NT, preferred_element_type=F32)
            if masked:
                s = _causal_mask(s, i, j, tq, tk)
            m_prev = m_sc[...]
            m_new = jnp.maximum(m_prev, jnp.max(s, axis=-1, keepdims=True))
            alpha = jnp.exp(m_prev - m_new)
            p = jnp.exp(s - m_new)
            l_sc[...] = alpha * l_sc[...] + jnp.sum(p, axis=-1, keepdims=True)
            acc_sc[...] = alpha * acc_sc[...] + jnp.dot(p.astype(BF16), v_ref[...], preferred_element_type=F32)
            m_sc[...] = m_new

        needed = j * tk <= i * tq + tq - 1
        crosses = j * tk + tk - 1 > i * tq
        pl.when(jnp.logical_and(needed, crosses))(lambda: step(True))
        pl.when(jnp.logical_and(needed, jnp.logical_not(crosses)))(lambda: step(False))

        @pl.when(j == nk - 1)
        def _():
            l = l_sc[...]
            o_ref[...] = (acc_sc[...] / l).astype(BF16)
            lse_ref[...] = jnp.broadcast_to(m_sc[...] + jnp.log(l), (tq, LANES))

    return pl.pallas_call(
        body, name="flash_fwd", grid=(H, nq, nk),
        in_specs=[pl.BlockSpec((None, tq, dq), lambda h, i, j: (h, i, 0)),
                  pl.BlockSpec((None, tk, dq), lambda h, i, j: (h, jnp.minimum(j, last_k(i)), 0)),
                  pl.BlockSpec((None, tk, dv), lambda h, i, j: (h, jnp.minimum(j, last_k(i)), 0))],
        out_specs=[pl.BlockSpec((tq, dv), lambda h, i, j: (i, h)),
                   pl.BlockSpec((None, tq, LANES), lambda h, i, j: (h, i, 0))],
        out_shape=[jax.ShapeDtypeStruct((S, H * dv), BF16), jax.ShapeDtypeStruct((H, S, LANES), F32)],
        scratch_shapes=[pltpu.VMEM((tq, 1), F32), pltpu.VMEM((tq, 1), F32), pltpu.VMEM((tq, dv), F32)],
        compiler_params=_params("parallel", "parallel", "arbitrary"),
    )(q, k, v)


def _attn_bwd_prep(dyc, o, tr=512):
    S = o.shape[0]
    H = o.shape[1] // MLA_V
    tr = _tile(S, tr)

    def body(dy_ref, o_ref, do_ref, dl_ref):
        dy = dy_ref[...]
        do_ref[...] = dy.astype(BF16)
        dl = jnp.sum(dy * o_ref[...].astype(F32), axis=-1, keepdims=True)
        dl_ref[...] = jnp.broadcast_to(dl, (tr, LANES))

    return pl.pallas_call(
        body, name="attn_bwd_prep", grid=(S // tr, H),
        in_specs=[pl.BlockSpec((tr, MLA_V), lambda i, h: (i, h)), pl.BlockSpec((tr, MLA_V), lambda i, h: (i, h))],
        out_specs=[pl.BlockSpec((tr, MLA_V), lambda i, h: (i, h)),
                   pl.BlockSpec((None, tr, LANES), lambda i, h: (h, i, 0))],
        out_shape=[jax.ShapeDtypeStruct((S, H * MLA_V), BF16), jax.ShapeDtypeStruct((H, S, LANES), F32)],
        compiler_params=_params("parallel", "parallel"),
    )(dyc, o)


def _flash_bwd_kv(q, k, v, do, lse, delta, tq=512, tk=512):
    H, S, dq = q.shape
    dv = v.shape[-1]
    tq, tk = _tile(S, tq), _tile(S, tk)
    nq, nk = S // tq, S // tk

    def qi(i, j):
        return jnp.maximum(i, (j * tk) // tq)

    def body(q_ref, k_ref, v_ref, do_ref, lse_ref, dl_ref, dk_ref, dv_ref, dk_sc, dv_sc):
        j, i = pl.program_id(1), pl.program_id(2)

        @pl.when(i == 0)
        def _():
            dk_sc[...] = jnp.zeros_like(dk_sc)
            dv_sc[...] = jnp.zeros_like(dv_sc)

        def step(masked):
            qv = q_ref[...]
            dov = do_ref[...]
            s = lax.dot_general(qv, k_ref[...], _NT, preferred_element_type=F32)
            if masked:
                s = _causal_mask(s, i, j, tq, tk)
            p = jnp.exp(s - lse_ref[:, :1])
            dv_sc[...] += lax.dot_general(p.astype(BF16), dov, _TN, preferred_element_type=F32)
            dp = lax.dot_general(dov, v_ref[...], _NT, preferred_element_type=F32)
            ds = p * (dp - dl_ref[:, :1])
            dk_sc[...] += lax.dot_general(ds.astype(BF16), qv, _TN, preferred_element_type=F32)

        needed = i * tq + tq - 1 >= j * tk
        crosses = j * tk + tk - 1 > i * tq
        pl.when(jnp.logical_and(needed, crosses))(lambda: step(True))
        pl.when(jnp.logical_and(needed, jnp.logical_not(crosses)))(lambda: step(False))

        @pl.when(i == nq - 1)
        def _():
            dk_ref[...] = dk_sc[...]
            dv_ref[...] = dv_sc[...]

    return pl.pallas_call(
        body, name="flash_bwd_kv", grid=(H, nk, nq),
        in_specs=[pl.BlockSpec((None, tq, dq), lambda h, j, i: (h, qi(i, j), 0)),
                  pl.BlockSpec((None, tk, dq), lambda h, j, i: (h, j, 0)),
                  pl.BlockSpec((None, tk, dv), lambda h, j, i: (h, j, 0)),
                  pl.BlockSpec((tq, dv), lambda h, j, i: (qi(i, j), h)),
                  pl.BlockSpec((None, tq, LANES), lambda h, j, i: (h, qi(i, j), 0)),
                  pl.BlockSpec((None, tq, LANES), lambda h, j, i: (h, qi(i, j), 0))],
        out_specs=[pl.BlockSpec((None, tk, dq), lambda h, j, i: (h, j, 0)),
                   pl.BlockSpec((None, tk, dv), lambda h, j, i: (h, j, 0))],
        out_shape=[jax.ShapeDtypeStruct((H, S, dq), F32), jax.ShapeDtypeStruct((H, S, dv), F32)],
        scratch_shapes=[pltpu.VMEM((tk, dq), F32), pltpu.VMEM((tk, dv), F32)],
        compiler_params=_params("parallel", "parallel", "arbitrary"),
    )(q, k, v, do, lse, delta)


def _flash_bwd_q(q, k, v, do, lse, delta, tq=512, tk=512):
    H, S, dq = q.shape
    dv = v.shape[-1]
    tq, tk = _tile(S, tq), _tile(S, tk)
    nq, nk = S // tq, S // tk

    def kj(i, j):
        return jnp.minimum(j, ((i + 1) * tq - 1) // tk)

    def body(q_ref, k_ref, v_ref, do_ref, lse_ref, dl_ref, dq_ref, dq_sc):
        i, j = pl.program_id(1), pl.program_id(2)

        @pl.when(j == 0)
        def _():
            dq_sc[...] = jnp.zeros_like(dq_sc)

        def step(masked):
            kv_ = k_ref[...]
            dov = do_ref[...]
            s = lax.dot_general(q_ref[...], kv_, _NT, preferred_element_type=F32)
            if masked:
                s = _causal_mask(s, i, j, tq, tk)
            p = jnp.exp(s - lse_ref[:, :1])
            dp = lax.dot_general(dov, v_ref[...], _NT, preferred_element_type=F32)
            ds = p * (dp - dl_ref[:, :1])
            dq_sc[...] += jnp.dot(ds.astype(BF16), kv_, preferred_element_type=F32)

        needed = j * tk <= i * tq + tq - 1
        crosses = j * tk + tk - 1 > i * tq
        pl.when(jnp.logical_and(needed, crosses))(lambda: step(True))
        pl.when(jnp.logical_and(needed, jnp.logical_not(crosses)))(lambda: step(False))

        @pl.when(j == nk - 1)
        def _():
            dq_ref[...] = dq_sc[...]

    return pl.pallas_call(
        body, name="flash_bwd_q", grid=(H, nq, nk),
        in_specs=[pl.BlockSpec((None, tq, dq), lambda h, i, j: (h, i, 0)),
                  pl.BlockSpec((None, tk, dq), lambda h, i, j: (h, kj(i, j), 0)),
                  pl.BlockSpec((None, tk, dv), lambda h, i, j: (h, kj(i, j), 0)),
                  pl.BlockSpec((tq, dv), lambda h, i, j: (i, h)),
                  pl.BlockSpec((None, tq, LANES), lambda h, i, j: (h, i, 0)),
                  pl.BlockSpec((None, tq, LANES), lambda h, i, j: (h, i, 0))],
        out_specs=pl.BlockSpec((None, tq, dq), lambda h, i, j: (h, i, 0)),
        out_shape=jax.ShapeDtypeStruct((H, S, dq), F32),
        scratch_shapes=[pltpu.VMEM((tq, dq), F32)],
        compiler_params=_params("parallel", "parallel", "arbitrary"),
    )(q, k, v, do, lse, delta)


def _mem_norm(mem, g):
    n = mem.shape[1]

    def body(m_ref, g_ref, o_ref):
        xhat, _ = _rms_hat(m_ref[...], n)
        o_ref[...] = (xhat * g_ref[...]).astype(BF16)

    return pl.pallas_call(body, name="mem_norm", out_shape=jax.ShapeDtypeStruct(mem.shape, BF16),
                          compiler_params=pltpu.CompilerParams(vmem_limit_bytes=VMEM_LIMIT))(mem, g)


def _memkv_prep(mkv, gk):
    M = mkv.shape[0]

    def body(mkv_ref, g_ref, k_ref, v_ref):
        for h in range(X_HEADS):
            cols = slice(h * X_HEAD_DIM, (h + 1) * X_HEAD_DIM)
            xhat, _ = _rms_hat(mkv_ref[:, cols], X_HEAD_DIM)
            k_ref[:, cols] = (xhat * g_ref[...]).astype(BF16)
        v_ref[...] = mkv_ref[:, X_WIDTH:].astype(BF16)

    return pl.pallas_call(
        body, name="memkv_prep",
        out_shape=[jax.ShapeDtypeStruct((M, X_WIDTH), BF16), jax.ShapeDtypeStruct((M, X_WIDTH), BF16)],
    )(mkv, gk)


def _memkv_bwd(dk, dv, mkv, gk):
    M = mkv.shape[0]

    def body(dk_ref, dv_ref, mkv_ref, g_ref, dmkv_ref, dg_ref):
        dg = jnp.zeros((1, X_HEAD_DIM), F32)
        for h in range(X_HEADS):
            cols = slice(h * X_HEAD_DIM, (h + 1) * X_HEAD_DIM)
            xhat, r = _rms_hat(mkv_ref[:, cols], X_HEAD_DIM)
            dx, dgh = _rms_bwd(dk_ref[:, cols], xhat, r, g_ref[...], X_HEAD_DIM)
            dmkv_ref[:, cols] = dx.astype(BF16)
            dg += dgh
        dmkv_ref[:, X_WIDTH:] = dv_ref[...].astype(BF16)
        dg_ref[...] = dg

    return pl.pallas_call(
        body, name="memkv_bwd",
        out_shape=[jax.ShapeDtypeStruct((M, 2 * X_WIDTH), BF16), jax.ShapeDtypeStruct((1, X_HEAD_DIM), F32)],
    )(dk, dv, mkv, gk)


def _mem_gain_bwd(mem, dmn):
    n = mem.shape[1]

    def body(m_ref, d_ref, dg_ref):
        xhat, _ = _rms_hat(m_ref[...], n)
        dg_ref[...] = jnp.sum(d_ref[...] * xhat, axis=0, keepdims=True)

    return pl.pallas_call(body, name="mem_gain_bwd", out_shape=jax.ShapeDtypeStruct((1, n), F32),
                          compiler_params=pltpu.CompilerParams(vmem_limit_bytes=VMEM_LIMIT))(mem, dmn)


def _memx_scores(zq_ref, g, kx_ref, h):
    cols = slice(h * X_HEAD_DIM, (h + 1) * X_HEAD_DIM)
    xhat, r = _rms_hat(zq_ref[:, cols], X_HEAD_DIM)
    qn = (xhat * g).astype(BF16)
    s = lax.dot_general(qn, kx_ref[:, cols], _NT, preferred_element_type=F32) * (1.0 / math.sqrt(X_HEAD_DIM))
    p = jnp.exp(s - jnp.max(s, axis=-1, keepdims=True))
    return cols, xhat, r, qn, p, jnp.sum(p, axis=-1, keepdims=True)


def _memx_fwd(z, g, kx, vx, tr=512):
    S = z.shape[0]
    tr = _tile(S, tr)
    M = kx.shape[0]

    def body(zq_ref, g_ref, kx_ref, vx_ref, y_ref):
        for h in range(X_HEADS):
            cols, _, _, _, p, l = _memx_scores(zq_ref, g_ref[...], kx_ref, h)
            o = jnp.dot(p.astype(BF16), vx_ref[:, cols], preferred_element_type=F32)
            y_ref[:, cols] = (o / l).astype(BF16)

    return pl.pallas_call(
        body, name="memx_fwd", grid=(S // tr,),
        in_specs=[_rows(tr, X_WIDTH, Z_MQ // X_WIDTH), _full((1, X_HEAD_DIM)), _full((M, X_WIDTH)),
                  _full((M, X_WIDTH))],
        out_specs=_rows(tr, X_WIDTH),
        out_shape=jax.ShapeDtypeStruct((S, X_WIDTH), BF16),
        compiler_params=_params("parallel"),
    )(z, g, kx, vx)


def _memx_bwd(z, g, kx, vx, dyc, dy_cb, tr=512):
    S = z.shape[0]
    tr = _tile(S, tr)
    M = kx.shape[0]
    scale = 1.0 / math.sqrt(X_HEAD_DIM)

    def body(zq_ref, g_ref, kx_ref, vx_ref, dy_ref, dz_ref, dk_ref, dv_ref, dg_ref):
        @pl.when(pl.program_id(0) == 0)
        def _():
            dk_ref[...] = jnp.zeros_like(dk_ref)
            dv_ref[...] = jnp.zeros_like(dv_ref)
            dg_ref[...] = jnp.zeros_like(dg_ref)

        gv = g_ref[...]
        for h in range(X_HEADS):
            cols, xhat, r, qn, p, l = _memx_scores(zq_ref, gv, kx_ref, h)
            p = p / l
            do = dy_ref[:, cols].astype(BF16)
            dv_ref[:, cols] += lax.dot_general(p.astype(BF16), do, _TN, preferred_element_type=F32)
            dp = lax.dot_general(do, vx_ref[:, cols], _NT, preferred_element_type=F32)
            ds = (p * (dp - jnp.sum(dp * p, axis=-1, keepdims=True)) * scale).astype(BF16)
            dqn = jnp.dot(ds, kx_ref[:, cols], preferred_element_type=F32)
            dk_ref[:, cols] += lax.dot_general(ds, qn, _TN, preferred_element_type=F32)
            dx, dg = _rms_bwd(dqn, xhat, r, gv, X_HEAD_DIM)
            dz_ref[:, cols] = dx.astype(BF16)
            dg_ref[...] += dg

    return pl.pallas_call(
        body, name="memx_bwd", grid=(S // tr,),
        in_specs=[_rows(tr, X_WIDTH, Z_MQ // X_WIDTH), _full((1, X_HEAD_DIM)), _full((M, X_WIDTH)),
                  _full((M, X_WIDTH)), _rows(tr, X_WIDTH, dy_cb)],
        out_specs=[_rows(tr, X_WIDTH), _full((M, X_WIDTH)), _full((M, X_WIDTH)), _full((1, X_HEAD_DIM))],
        out_shape=[jax.ShapeDtypeStruct((S, X_WIDTH), BF16), jax.ShapeDtypeStruct((M, X_WIDTH), F32),
                   jax.ShapeDtypeStruct((M, X_WIDTH), F32), jax.ShapeDtypeStruct((1, X_HEAD_DIM), F32)],
        compiler_params=_params("arbitrary"),
    )(z, g, kx, vx, dyc)


def _conv_gc(g_ext, w_ref, b_ref, n_ext):
    g1 = pltpu.roll(g_ext, 1, 0)
    g2 = pltpu.roll(g_ext, 2, 0)
    gc = b_ref[...] + w_ref[0:1, :] * g2
    gc = gc + w_ref[1:2, :] * g1
    gc = gc + w_ref[2:3, :] * g_ext
    return gc, g1, g2


def _convglu_fwd(g, u, cw, cb, tr=512, tc=512):
    S, F = g.shape
    tr, tc = _tile(S, tr), _tile(F, tc)
    rb = tr // CONV_HALO

    def body(g_ref, gp_ref, u_ref, w_ref, b_ref, a_ref):
        i = pl.program_id(1)
        halo = jnp.where(i == 0, 0.0, gp_ref[...])
        g_ext = jnp.concatenate([halo, g_ref[...]], axis=0)
        gc, _, _ = _conv_gc(g_ext, w_ref, b_ref, tr + CONV_HALO)
        gc = gc[CONV_HALO:]
        sig = 1.0 / (1.0 + jnp.exp(-gc))
        a_ref[...] = (gc * sig * u_ref[...]).astype(BF16)

    return pl.pallas_call(
        body, name="convglu_fwd", grid=(F // tc, S // tr),
        in_specs=[pl.BlockSpec((tr, tc), lambda j, i: (i, j)),
                  pl.BlockSpec((CONV_HALO, tc), lambda j, i: (jnp.maximum(i * rb - 1, 0), j)),
                  pl.BlockSpec((tr, tc), lambda j, i: (i, j)),
                  pl.BlockSpec((3, tc), lambda j, i: (0, j)), pl.BlockSpec((1, tc), lambda j, i: (0, j))],
        out_specs=pl.BlockSpec((tr, tc), lambda j, i: (i, j)),
        out_shape=jax.ShapeDtypeStruct((S, F), BF16),
        compiler_params=_params("parallel", "parallel"),
    )(g, g, u, cw, cb)


def _convglu_bwd(g, u, da, cw, cb, tr=512, tc=512):
    S, F = g.shape
    tr, tc = _tile(S, tr), _tile(F, tc)
    rb = tr // CONV_HALO
    nhb = S // CONV_HALO
    H = CONV_HALO

    def body(g_ref, gp_ref, gn_ref, u_ref, un_ref, da_ref, dan_ref, w_ref, b_ref,
             dg_ref, du_ref, dw_ref, db_ref):
        i = pl.program_id(1)
        last = i == pl.num_programs(1) - 1

        @pl.when(i == 0)
        def _():
            dw_ref[...] = jnp.zeros_like(dw_ref)
            db_ref[...] = jnp.zeros_like(db_ref)

        g_prev = jnp.where(i == 0, 0.0, gp_ref[...])
        g_cur = g_ref[...]
        g_ext = jnp.concatenate([g_prev, g_cur, gn_ref[...]], axis=0)
        gc, g1, g2 = _conv_gc(g_ext, w_ref, b_ref, tr + 2 * H)
        gc = gc[H:]
        ux = jnp.concatenate([u_ref[...], un_ref[...]], axis=0)
        dax = jnp.concatenate([da_ref[...], jnp.where(last, 0.0, dan_ref[...])], axis=0)
        sig = 1.0 / (1.0 + jnp.exp(-gc))
        du_ref[...] = (dax[:tr] * (gc[:tr] * sig[:tr])).astype(BF16)
        dgc = dax * ux * (sig * (1.0 + gc * (1.0 - sig)))
        n = tr + H
        d1 = pltpu.roll(dgc, n - 1, 0)[:tr]
        d2 = pltpu.roll(dgc, n - 2, 0)[:tr]
        d0 = dgc[:tr]
        dg_ref[...] = (w_ref[2:3, :] * d0 + w_ref[1:2, :] * d1 + w_ref[0:1, :] * d2).astype(BF16)
        db_ref[...] += jnp.sum(d0, axis=0, keepdims=True)
        dw_ref[0:1, :] += jnp.sum(d0 * g2[H:H + tr], axis=0, keepdims=True)
        dw_ref[1:2, :] += jnp.sum(d0 * g1[H:H + tr], axis=0, keepdims=True)
        dw_ref[2:3, :] += jnp.sum(d0 * g_cur, axis=0, keepdims=True)

    cur = pl.BlockSpec((tr, tc), lambda j, i: (i, j))
    prv = pl.BlockSpec((H, tc), lambda j, i: (jnp.maximum(i * rb - 1, 0), j))
    nxt = pl.BlockSpec((H, tc), lambda j, i: (jnp.minimum((i + 1) * rb, nhb - 1), j))
    return pl.pallas_call(
        body, name="convglu_bwd", grid=(F // tc, S // tr),
        in_specs=[cur, prv, nxt, cur, nxt, cur, nxt,
                  pl.BlockSpec((3, tc), lambda j, i: (0, j)), pl.BlockSpec((1, tc), lambda j, i: (0, j))],
        out_specs=[cur, cur, pl.BlockSpec((3, tc), lambda j, i: (0, j)), pl.BlockSpec((1, tc), lambda j, i: (0, j))],
        out_shape=[jax.ShapeDtypeStruct((S, F), BF16), jax.ShapeDtypeStruct((S, F), BF16),
                   jax.ShapeDtypeStruct((3, F), F32), jax.ShapeDtypeStruct((1, F), F32)],
        compiler_params=_params("parallel", "arbitrary"),
    )(g, g, g, u, u, da, da, cw, cb)


def _loss_head(y, target, tr=256):
    S, n = y.shape
    tr = _tile(S, tr)

    def body(y_ref, t_ref, dy_ref, dyb_ref, loss_ref):
        @pl.when(pl.program_id(0) == 0)
        def _():
            loss_ref[...] = jnp.zeros_like(loss_ref)

        err = y_ref[...] - t_ref[...]
        dy = err / n
        dy_ref[...] = dy
        dyb_ref[...] = dy.astype(BF16)
        per_row = jnp.sum(err * err, axis=-1, keepdims=True) / n
        loss_ref[...] += 0.5 * jnp.sum(per_row)

    return pl.pallas_call(
        body, name="loss_head", grid=(S // tr,),
        in_specs=[_rows(tr, n), _rows(tr, n)],
        out_specs=[_rows(tr, n), _rows(tr, n), _full((8, LANES))],
        out_shape=[jax.ShapeDtypeStruct((S, n), F32), jax.ShapeDtypeStruct((S, n), BF16),
                   jax.ShapeDtypeStruct((8, LANES), F32)],
        compiler_params=_params("arbitrary"),
    )(y, target)


def _local_step(x, mem, target, W):
    S = x.shape[0]
    tabs = _rope_tables(S)
    G = {}

    h = _rms_fwd("rms1_fwd", x, W["g_mix"])
    z = _mm("mm_in", h, W["w_in"], tm=512, tn=Z_COLS, tk=D_MODEL)
    y_pool = _pool_fwd(z, W["w_pool"], W["pool_scale"])
    ql, kvl = _lat_fwd(z, W["g_q_lat"], W["g_kv_lat"])
    q_raw = _mm("mm_q_up", ql, W["w_q_up"], tm=1024, tn=2048, tk=Q_LORA)
    kv_raw = _mm("mm_kv_up", kvl, W["w_kv_up"], tm=1024, tn=2048, tk=KV_LORA)
    q, k, v = _mla_prep(q_raw, kv_raw, z, tabs, W["g_q_mla"], W["g_k_mla"])
    o_mla, lse = _flash_fwd(q, k, v)
    mn = _mem_norm(mem, W["g_mem"])
    mkv = _mm("mm_mem_kv", mn, W["w_mem_kv"], tm=256, tn=1024, tk=D_MODEL)
    kx, vx = _memkv_prep(mkv, W["g_k_x"])
    y_mem = _memx_fwd(z, W["g_q_x"], kx, vx)
    ycat = jnp.concatenate([o_mla, y_pool, y_mem], axis=1)
    x1 = _mm("mm_o", ycat, W["w_o"], add=x, tm=512, tn=2048, tk=D_MODEL)
    h2 = _rms_fwd("rms2_fwd", x1, W["g_ffn"])
    g = _mm("mm_gate", h2, W["w_gate"], tm=1024, tn=512, tk=D_MODEL)
    u = _mm("mm_up", h2, W["w_up"], tm=1024, tn=512, tk=D_MODEL)
    a = _convglu_fwd(g, u, W["conv_w"], W["conv_b"])
    y = _mm("mm_down", a, W["w_down"], add=x1, tm=1024, tn=1024, tk=512)
    dy, dyb, loss_part = _loss_head(y, target)

    da = _mm("mm_down_dx", dyb, W["w_down"], tb=True, tm=1024, tn=512, tk=D_MODEL)
    G["w_down"] = _mm("mm_down_dw", a, dyb, ta=True, tm=1024, tn=1024, tk=512)
    dg, du, G["conv_w"], G["conv_b"] = _convglu_bwd(g, u, da, W["conv_w"], W["conv_b"])
    dh2 = _mm("mm_gate_dx", dg, W["w_gate"], tb=True, tm=1024, tn=1024, tk=512)
    dh2 = _mm("mm_up_dx", du, W["w_up"], tb=True, add=dh2, tm=1024, tn=1024, tk=512)
    G["w_gate"] = _mm("mm_gate_dw", h2, dg, ta=True, tm=1024, tn=1024, tk=512)
    G["w_up"] = _mm("mm_up_dw", h2, du, ta=True, tm=1024, tn=1024, tk=512)
    dx1, dx1b, G["g_ffn"] = _rms_bwd_call("rms2_bwd", x1, W["g_ffn"], dh2, dy, True)

    dyc = _mm("mm_o_dx", dx1b, W["w_o"], tb=True, tm=512, tn=2048, tk=D_MODEL)
    G["w_o"] = _mm("mm_o_dw", ycat, dx1b, ta=True, tm=1024, tn=1024, tk=512)
    dz_pool, G["w_pool"], G["pool_scale"] = _pool_bwd(z, dyc, W["w_pool"], W["pool_scale"], dy_cb=2)
    dz_mq, dkx, dvx, G["g_q_x"] = _memx_bwd(z, W["g_q_x"], kx, vx, dyc, dy_cb=3)
    dmkv, G["g_k_x"] = _memkv_bwd(dkx, dvx, mkv, W["g_k_x"])
    G["w_mem_kv"] = _mm("mm_mem_kv_dw", mn, dmkv, ta=True, tm=1024, tn=1024, tk=MEM_LEN)
    dmn = _mm("mm_mem_kv_dx", dmkv, W["w_mem_kv"], tb=True, tm=256, tn=2048, tk=1024)
    G["g_mem"] = _mem_gain_bwd(mem, dmn)
    do, delta = _attn_bwd_prep(dyc, o_mla)
    dk, dv = _flash_bwd_kv(q, k, v, do, lse, delta)
    dq = _flash_bwd_q(q, k, v, do, lse, delta)
    dq_raw, dkv_raw, dz_kr, G["g_q_mla"], G["g_k_mla"] = _mla_prep_bwd(
        dq, dk, dv, q_raw, kv_raw, z, tabs, W["g_q_mla"], W["g_k_mla"])
    G["w_q_up"] = _mm("mm_q_up_dw", ql, dq_raw, ta=True, tm=512, tn=1024, tk=512)
    dql = _mm("mm_q_up_dx", dq_raw, W["w_q_up"], tb=True, tm=1024, tn=512, tk=2048)
    G["w_kv_up"] = _mm("mm_kv_up_dw", kvl, dkv_raw, ta=True, tm=256, tn=1024, tk=512)
    dkvl = _mm("mm_kv_up_dx", dkv_raw, W["w_kv_up"], tb=True, tm=1024, tn=256, tk=2048)
    dz_q, dz_kv, G["g_q_lat"], G["g_kv_lat"] = _lat_bwd(z, W["g_q_lat"], W["g_kv_lat"], dql, dkvl)
    dz = jnp.concatenate([dz_pool, dz_q, dz_mq, dz_kv, dz_kr], axis=1)
    G["w_in"] = _mm("mm_in_dw", h, dz, ta=True, tm=1024, tn=Z_COLS, tk=512)
    dh = _mm("mm_in_dx", dz, W["w_in"], tb=True, tm=512, tn=2048, tk=Z_COLS)
    grad_x, G["g_mix"] = _rms_bwd_call("rms1_bwd", x, W["g_mix"], dh, dx1, False)
    return loss_part, grad_x, G


_ANY = pl.BlockSpec(memory_space=pl.ANY)
_MESH = pl.DeviceIdType.MESH


def _my_index():
    return 4 * lax.axis_index("x") + 2 * lax.axis_index("y") + lax.axis_index("c")


def _all_gather(p):
    R = p.shape[0]

    def body(x_ref, out_ref, send_sems, recv_sems, local_sem):
        x, y, c = lax.axis_index("x"), lax.axis_index("y"), lax.axis_index("c")
        me, sibling = (x, y, c), (x, y, 1 - c)
        chips = [(1 - x, y), (x, 1 - y), (1 - x, 1 - y)]

        def rows(px, py, pc):
            return out_ref.at[4 * px + 2 * py + pc]

        def copy(k, block, to, src=None):
            return pltpu.make_async_remote_copy(
                src_ref=rows(*block) if src is None else src, dst_ref=rows(*block),
                send_sem=send_sems.at[k], recv_sem=recv_sems.at[k], device_id=to, device_id_type=_MESH)

        mine = pltpu.make_async_copy(x_ref, rows(*me), local_sem)
        mine.start()
        first = [copy(0, me, sibling, src=x_ref)]
        first += [copy(1 + j, me, (*chip, c), src=x_ref) for j, chip in enumerate(chips)]
        for cp in first:
            cp.start()
        passed = [copy(4 + j, (*chip, c), sibling) for j, chip in enumerate(chips)]
        for j, chip in enumerate(chips):
            copy(1 + j, (*chip, c), me).wait_recv()
            passed[j].start()
        copy(0, sibling, me).wait_recv()
        for j, chip in enumerate(chips):
            copy(4 + j, (*chip, 1 - c), me).wait_recv()
        for cp in first + passed:
            cp.wait_send()
        mine.wait()

    return pl.pallas_call(
        body, name="all_gather", out_shape=jax.ShapeDtypeStruct((N_DEV, R, LANES), p.dtype),
        in_specs=[_ANY], out_specs=_ANY,
        scratch_shapes=[pltpu.SemaphoreType.DMA((7,)), pltpu.SemaphoreType.DMA((7,)), pltpu.SemaphoreType.DMA],
    )(p)


def _exchange_all(s):
    R = s.shape[0]

    def body(x_ref, out_ref, send_sems, recv_sems, local_sem):
        x, y, c = lax.axis_index("x"), lax.axis_index("y"), lax.axis_index("c")
        me = 4 * x + 2 * y + c
        mine = pltpu.make_async_copy(x_ref, out_ref.at[me], local_sem)
        mine.start()

        def copy(k):
            px, py, pc = x ^ ((k >> 2) & 1), y ^ ((k >> 1) & 1), c ^ (k & 1)
            return pltpu.make_async_remote_copy(
                src_ref=x_ref, dst_ref=out_ref.at[me], send_sem=send_sems.at[k - 1], recv_sem=recv_sems.at[k - 1],
                device_id=(px, py, pc), device_id_type=_MESH)

        cps = [copy(k) for k in range(1, N_DEV)]
        for cp in cps:
            cp.start()
        for cp in cps:
            cp.wait_recv()
        for cp in cps:
            cp.wait_send()
        mine.wait()

    return pl.pallas_call(
        body, name="exchange_all", out_shape=jax.ShapeDtypeStruct((N_DEV, R, LANES), s.dtype),
        in_specs=[_ANY], out_specs=_ANY,
        scratch_shapes=[pltpu.SemaphoreType.DMA((7,)), pltpu.SemaphoreType.DMA((7,)), pltpu.SemaphoreType.DMA],
    )(s)


def _sum_slots(buf, tr=208):
    n, R, _ = buf.shape
    tr = tr if R % tr == 0 else R

    def body(b_ref, o_ref):
        acc = b_ref[0]
        for d in range(1, n):
            acc = acc + b_ref[d]
        o_ref[...] = acc

    return pl.pallas_call(
        body, name="sum_slots", grid=(R // tr,),
        in_specs=[pl.BlockSpec((n, tr, LANES), lambda i: (0, i, 0))],
        out_specs=pl.BlockSpec((tr, LANES), lambda i: (i, 0)),
        out_shape=jax.ShapeDtypeStruct((R, LANES), buf.dtype),
        compiler_params=_params("parallel"),
    )(buf)


def _rs_to_sibling(gbuf):
    _, R, _ = gbuf.shape

    def body(g_ref, out_ref, send_sem, recv_sem):
        x, y, c = lax.axis_index("x"), lax.axis_index("y"), lax.axis_index("c")
        cp = pltpu.make_async_remote_copy(
            src_ref=g_ref.at[pl.ds(4 * (1 - c), 4)], dst_ref=out_ref, send_sem=send_sem, recv_sem=recv_sem,
            device_id=(x, y, 1 - c), device_id_type=_MESH)
        cp.start()
        cp.wait()

    return pl.pallas_call(
        body, name="rs_to_sibling", out_shape=jax.ShapeDtypeStruct((4, R, LANES), gbuf.dtype),
        in_specs=[_ANY], out_specs=_ANY,
        scratch_shapes=[pltpu.SemaphoreType.DMA, pltpu.SemaphoreType.DMA],
    )(gbuf)


def _rs_add_chip(gbuf, got, c_idx, tr=1024):
    _, R, _ = gbuf.shape

    def body(c_ref, g_ref, r_ref, f_ref, b_ref):
        s = g_ref[...] + r_ref[...]
        f_ref[...] = s
        b_ref[...] = s.astype(BF16)

    blk = lambda f: pl.BlockSpec((None, tr, LANES), f)
    return pl.pallas_call(
        body, name="rs_add_chip",
        grid_spec=pltpu.PrefetchScalarGridSpec(
            num_scalar_prefetch=1, grid=(4, R // tr),
            in_specs=[blk(lambda b, r, c: (4 * c[0] + b, r, 0)), blk(lambda b, r, c: (b, r, 0))],
            out_specs=[blk(lambda b, r, c: (b, r, 0)), blk(lambda b, r, c: (b, r, 0))]),
        out_shape=[jax.ShapeDtypeStruct((4, R, LANES), F32), jax.ShapeDtypeStruct((4, R, LANES), BF16)],
        compiler_params=_params("parallel", "parallel"),
    )(c_idx, gbuf, got)


def _rs_to_chips(sb):
    _, R, _ = sb.shape

    def body(s_ref, out_ref, send_sems, recv_sems):
        x, y, c = lax.axis_index("x"), lax.axis_index("y"), lax.axis_index("c")
        chips = [(1 - x, y), (x, 1 - y), (1 - x, 1 - y)]
        cps = [pltpu.make_async_remote_copy(
            src_ref=s_ref.at[2 * cx + cy], dst_ref=out_ref.at[k], send_sem=send_sems.at[k],
            recv_sem=recv_sems.at[k], device_id=(cx, cy, c), device_id_type=_MESH)
            for k, (cx, cy) in enumerate(chips)]
        for cp in cps:
            cp.start()
        for cp in cps:
            cp.wait_recv()
        for cp in cps:
            cp.wait_send()

    return pl.pallas_call(
        body, name="rs_to_chips", out_shape=jax.ShapeDtypeStruct((3, R, LANES), sb.dtype),
        in_specs=[_ANY], out_specs=_ANY,
        scratch_shapes=[pltpu.SemaphoreType.DMA((3,)), pltpu.SemaphoreType.DMA((3,))],
    )(sb)


def _rs_add_final(sf, got, own_idx, tr=1024):
    _, R, _ = sf.shape

    def body(o_ref, s_ref, r_ref, out_ref):
        acc = s_ref[...]
        for k in range(3):
            acc = acc + r_ref[k].astype(F32)
        out_ref[...] = acc

    return pl.pallas_call(
        body, name="rs_add_final",
        grid_spec=pltpu.PrefetchScalarGridSpec(
            num_scalar_prefetch=1, grid=(R // tr,),
            in_specs=[pl.BlockSpec((None, tr, LANES), lambda r, o: (o[0], r, 0)),
                      pl.BlockSpec((3, tr, LANES), lambda r, o: (0, r, 0))],
            out_specs=pl.BlockSpec((tr, LANES), lambda r, o: (r, 0))),
        out_shape=jax.ShapeDtypeStruct((R, LANES), F32),
        compiler_params=_params("parallel"),
    )(own_idx, sf, got)


def _adamw(name, w, g, m, v, tr=256):
    R, C = w.shape
    tr = tr if R % tr == 0 else R

    def body(w_ref, g_ref, m_ref, v_ref, d_ref, mo_ref, vo_ref):
        gv = g_ref[...]
        m_new = ADAM_B1 * m_ref[...] + (1.0 - ADAM_B1) * gv
        v_new = ADAM_B2 * v_ref[...] + (1.0 - ADAM_B2) * (gv * gv)
        m_hat = m_new / (1.0 - ADAM_B1 ** ADAM_STEP)
        v_hat = v_new / (1.0 - ADAM_B2 ** ADAM_STEP)
        d_ref[...] = -ADAM_LR * (m_hat / (jnp.sqrt(v_hat) + ADAM_EPS) + ADAM_WD * w_ref[...])
        mo_ref[...] = m_new
        vo_ref[...] = v_new

    spec = pl.BlockSpec((tr, C), lambda i: (i, 0))
    sds = jax.ShapeDtypeStruct((R, C), F32)
    return pl.pallas_call(
        body, name=name, grid=(R // tr,), in_specs=[spec] * 4, out_specs=[spec] * 3, out_shape=[sds] * 3,
        compiler_params=_params("parallel"),
    )(w, g, m, v)


def _pad_rows(a, mult):
    r = (-a.shape[0]) % mult
    return a if r == 0 else jnp.concatenate([a, jnp.zeros((r,) + a.shape[1:], a.dtype)], axis=0)


def _as_rows(a, mult):
    flat = a.reshape(-1)
    r = (-flat.shape[0]) % LANES
    if r:
        flat = jnp.concatenate([flat, jnp.zeros((r,), a.dtype)])
    return _pad_rows(flat.reshape(-1, LANES), mult)


def _w_in_to_kernel_cols(w):
    pad = jnp.zeros(w.shape[:-1] + (Z_COLS - IN_COLS,), w.dtype)
    return jnp.concatenate([w[..., :1024], w[..., 1344:1856], w[..., 1024:1344], pad], axis=-1)


def _w_in_from_kernel_cols(w):
    return jnp.concatenate([w[..., :1024], w[..., 1536:1856], w[..., 1024:1536]], axis=-1)


def kernel(x, mem, g_mix, w_in, g_q_lat, w_q_up, g_kv_lat, w_kv_up, g_q_mla, g_k_mla, w_pool, pool_scale, g_mem, w_mem_kv, g_q_x, g_k_x, w_o, g_ffn, w_gate, w_up, conv_w, conv_b, w_down, loss_target, m_g_mix, m_w_in, m_g_q_lat, m_w_q_up, m_g_kv_lat, m_w_kv_up, m_g_q_mla, m_g_k_mla, m_w_pool, m_pool_scale, m_g_mem, m_w_mem_kv, m_g_q_x, m_g_k_x, m_w_o, m_g_ffn, m_w_gate, m_w_up, m_conv_w, m_conv_b, m_w_down, v_g_mix, v_w_in, v_g_q_lat, v_w_q_up, v_g_kv_lat, v_w_kv_up, v_g_q_mla, v_g_k_mla, v_w_pool, v_pool_scale, v_g_mem, v_w_mem_kv, v_g_q_x, v_g_k_x, v_w_o, v_g_ffn, v_w_gate, v_w_up, v_conv_w, v_conv_b, v_w_down):
    given = dict(g_mix=g_mix, w_in=w_in, g_q_lat=g_q_lat, w_q_up=w_q_up, g_kv_lat=g_kv_lat, w_kv_up=w_kv_up,
                 g_q_mla=g_q_mla, g_k_mla=g_k_mla, w_pool=w_pool, pool_scale=pool_scale, g_mem=g_mem,
                 w_mem_kv=w_mem_kv, g_q_x=g_q_x, g_k_x=g_k_x, w_o=w_o, g_ffn=g_ffn, w_gate=w_gate, w_up=w_up,
                 conv_w=conv_w, conv_b=conv_b, w_down=w_down)
    mom_m = dict(g_mix=m_g_mix, w_in=m_w_in, g_q_lat=m_g_q_lat, w_q_up=m_w_q_up, g_kv_lat=m_g_kv_lat,
                 w_kv_up=m_w_kv_up, g_q_mla=m_g_q_mla, g_k_mla=m_g_k_mla, w_pool=m_w_pool,
                 pool_scale=m_pool_scale, g_mem=m_g_mem, w_mem_kv=m_w_mem_kv, g_q_x=m_g_q_x, g_k_x=m_g_k_x,
                 w_o=m_w_o, g_ffn=m_g_ffn, w_gate=m_w_gate, w_up=m_w_up, conv_w=m_conv_w, conv_b=m_conv_b,
                 w_down=m_w_down)
    mom_v = dict(g_mix=v_g_mix, w_in=v_w_in, g_q_lat=v_g_q_lat, w_q_up=v_w_q_up, g_kv_lat=v_g_kv_lat,
                 w_kv_up=v_w_kv_up, g_q_mla=v_g_q_mla, g_k_mla=v_g_k_mla, w_pool=v_w_pool,
                 pool_scale=v_pool_scale, g_mem=v_g_mem, w_mem_kv=v_w_mem_kv, g_q_x=v_g_q_x, g_k_x=v_g_k_x,
                 w_o=v_w_o, g_ffn=v_g_ffn, w_gate=v_w_gate, w_up=v_w_up, conv_w=v_conv_w, conv_b=v_conv_b,
                 w_down=v_w_down)
    drop = lambda a: a[0] if a.ndim > 2 else a
    sh = {n: drop(given[n]) for n in WEIGHTS}
    mom_m = {n: drop(mom_m[n]) for n in WEIGHTS}
    mom_v = {n: drop(mom_v[n]) for n in WEIGHTS}
    ffs = D_FF // N_DEV

    cw_hi = sh["conv_w"].astype(BF16)
    cw_r = sh["conv_w"] - cw_hi.astype(F32)
    cw_mid = cw_r.astype(BF16)
    cw_lo = (cw_r - cw_mid.astype(F32)).astype(BF16)
    parts = [
        _w_in_to_kernel_cols(sh["w_in"]).astype(BF16),
        jnp.pad(sh["w_q_up"], ((0, 0), (0, MLA_QK_PAD - MLA_QK))).astype(BF16),
        sh["w_kv_up"].astype(BF16), sh["w_mem_kv"].astype(BF16), sh["w_o"].astype(BF16),
        sh["w_gate"].astype(BF16), sh["w_up"].astype(BF16), sh["w_down"].astype(BF16),
        jnp.stack([cw_hi, cw_mid, cw_lo]),
    ]
    rows = [_as_rows(p, 16) for p in parts]
    offs = [0]
    for r in rows:
        offs.append(offs[-1] + r.shape[0])
    gathered = _all_gather(jnp.concatenate(rows, axis=0))

    def take(i, shape):
        n = math.prod(shape)
        blk = gathered[:, offs[i]:offs[i] + (n + LANES - 1) // LANES, :]
        return blk.reshape(N_DEV, -1)[:, :n].reshape((N_DEV,) + shape)

    def by_rows(i, shape):
        return take(i, shape).reshape((N_DEV * shape[0],) + shape[1:])

    def by_cols(i, shape):
        return jnp.transpose(take(i, shape), (1, 0, 2)).reshape(shape[0], N_DEV * shape[1])

    w_o_blocks = take(4, (256, D_MODEL))
    W = {
        "w_in": by_rows(0, (256, Z_COLS)),
        "w_q_up": by_cols(1, (Q_LORA, MLA_QK_PAD)),
        "w_kv_up": by_cols(2, (KV_LORA, 256)),
        "w_mem_kv": by_rows(3, (256, 2 * X_WIDTH)),
        "w_o": jnp.concatenate([w_o_blocks[b] for b in WO_BLOCK_ORDER], axis=0),
        "w_gate": by_cols(5, (D_MODEL, ffs)),
        "w_up": by_cols(6, (D_MODEL, ffs)),
        "w_down": by_rows(7, (ffs, D_MODEL)),
    }
    cw = jnp.sum(take(8, (3, 3, ffs)).astype(F32), axis=1)
    W["conv_w"] = jnp.transpose(cw, (1, 0, 2)).reshape(3, D_FF)
    W["w_pool"] = sh["w_pool"].astype(BF16)
    for n in ("g_mix", "g_q_lat", "g_kv_lat", "pool_scale", "g_mem", "g_q_x", "g_k_x", "g_ffn", "conv_b"):
        W[n] = sh[n]
    pad_qk = lambda gv: jnp.pad(gv, ((0, 0), (0, MLA_QK_PAD - MLA_QK)))
    W["g_q_mla"], W["g_k_mla"] = pad_qk(sh["g_q_mla"]), pad_qk(sh["g_k_mla"])

    loss_part, grad_x, G = _local_step(x[0], mem[0], loss_target[0], W)

    def dev_blocks(n):
        g = G[n]
        if n == "w_in":
            return _w_in_from_kernel_cols(g).reshape(N_DEV, 256, IN_COLS)
        if n == "w_q_up":
            return jnp.transpose(g.reshape(Q_LORA, N_DEV, MLA_QK_PAD)[:, :, :MLA_QK], (1, 0, 2))
        if n == "w_kv_up":
            return jnp.transpose(g.reshape(KV_LORA, N_DEV, 256), (1, 0, 2))
        if n == "w_mem_kv":
            return g.reshape(N_DEV, 256, 2 * X_WIDTH)
        if n == "w_o":
            blocks = g.reshape(N_DEV, 256, D_MODEL)
            inv = [WO_BLOCK_ORDER.index(b) for b in range(N_DEV)]
            return jnp.stack([blocks[inv[b]] for b in range(N_DEV)], axis=0)
        if n in ("w_gate", "w_up"):
            return jnp.transpose(g.reshape(D_MODEL, N_DEV, ffs), (1, 0, 2))
        if n == "conv_w":
            return jnp.transpose(g.reshape(3, N_DEV, ffs), (1, 0, 2))
        assert n == "w_down"
        return g.reshape(N_DEV, ffs, D_MODEL)

    per_dev = {n: dev_blocks(n) for n in BIG}
    g_offs = {}
    off = 0
    for n in BIG:
        g_offs[n] = off
        off += -(-math.prod(per_dev[n].shape[1:]) // LANES)
    rs_rows = -(-off // 1024) * 1024

    def dev_rows(d):
        segs = [_as_rows(per_dev[n][d], 1) for n in BIG]
        segs.append(jnp.zeros((rs_rows - off, LANES), F32))
        return jnp.concatenate(segs, axis=0)

    order = [4 * xx + 2 * yy + cc for cc in range(2) for xx in range(2) for yy in range(2)]
    gbuf = jnp.stack([dev_rows(d) for d in order], axis=0)
    ax, ay, ac = lax.axis_index("x"), lax.axis_index("y"), lax.axis_index("c")
    from_sib = _rs_to_sibling(gbuf)
    chip_f32, chip_bf16 = _rs_add_chip(gbuf, from_sib, jnp.reshape(ac, (1,)).astype(jnp.int32))
    from_chips = _rs_to_chips(chip_bf16)
    g_shard = _rs_add_final(chip_f32, from_chips, jnp.reshape(2 * ax + ay, (1,)).astype(jnp.int32))

    def shard_grad(n):
        shape = sh[n].shape
        cnt = math.prod(shape)
        blk = g_shard[g_offs[n]:g_offs[n] + -(-cnt // LANES)]
        return blk.reshape(-1)[:cnt].reshape(shape)

    small = {
        "g_mix": G["g_mix"], "g_q_lat": G["g_q_lat"], "g_kv_lat": G["g_kv_lat"],
        "g_q_mla": G["g_q_mla"][:, :MLA_QK], "g_k_mla": G["g_k_mla"][:, :MLA_QK],
        "w_pool": G["w_pool"], "pool_scale": G["pool_scale"], "g_mem": G["g_mem"],
        "g_q_x": G["g_q_x"], "g_k_x": G["g_k_x"], "g_ffn": G["g_ffn"], "conv_b": G["conv_b"],
    }
    s_offs = {}
    segs = []
    off = 0
    for n in SMALL:
        r = _as_rows(small[n], 1)
        s_offs[n] = off
        off += r.shape[0]
        segs.append(r)
    segs.append(loss_part[:1])
    loss_row = off
    sbuf = _pad_rows(jnp.concatenate(segs, axis=0), 8)
    s_sum = _sum_slots(_exchange_all(sbuf))

    def small_take(buf, n):
        shape = sh[n].shape
        cnt = math.prod(shape)
        return buf[s_offs[n]:s_offs[n] + -(-cnt // LANES)].reshape(-1)[:cnt].reshape(shape)

    loss = s_sum[loss_row, 0]

    grads, deltas, new_m, new_v = {}, {}, {}, {}
    for n in BIG:
        shape = sh[n].shape
        g = shard_grad(n)
        flat = lambda a: a.reshape(-1, shape[-1])
        d, m2, v2 = _adamw("adamw_" + n, flat(sh[n]), flat(g), flat(mom_m[n]), flat(mom_v[n]))
        grads[n], deltas[n], new_m[n], new_v[n] = g, d.reshape(shape), m2.reshape(shape), v2.reshape(shape)

    def pack_small(src):
        rows_ = [_as_rows(src[n], 1) for n in SMALL]
        return _pad_rows(jnp.concatenate(rows_, axis=0), 8)

    n_small = pack_small(sh).shape[0]
    d_s, m_s, v_s = _adamw("adamw_small", pack_small(sh), s_sum[:n_small],
                           pack_small(mom_m), pack_small(mom_v))
    for n in SMALL:
        grads[n] = small_take(s_sum, n)
        deltas[n], new_m[n], new_v[n] = small_take(d_s, n), small_take(m_s, n), small_take(v_s, n)

    lead = lambda n, a: a.reshape(given[n].shape)
    return (loss, grad_x[None],
            *[lead(n, grads[n]) for n in WEIGHTS], *[lead(n, deltas[n]) for n in WEIGHTS],
            *[lead(n, new_m[n]) for n in WEIGHTS], *[lead(n, new_v[n]) for n in WEIGHTS])
```

```python
import math

import jax
import jax.numpy as jnp
from jax import lax
from jax.experimental import pallas as pl
from jax.experimental.pallas import tpu as pltpu

F32 = jnp.float32
BF16 = jnp.bfloat16

D_MODEL = 2048
D_FF = 5632
POOL_WIDTH = 512
POOL_WINDOWS = (2, 4, 8, 16)
POOL_HALO = 16
MLA_HEADS = 8
MLA_NOPE = 128
MLA_ROPE = 64
MLA_QK = MLA_NOPE + MLA_ROPE
MLA_QK_PAD = 256
MLA_V = 128
Q_LORA = 512
KV_LORA = 256
X_HEADS = 4
X_HEAD_DIM = 128
X_WIDTH = 512
MEM_LEN = 256
ROPE_THETA = 10000.0
NORM_EPS = 1e-6
CONV_HALO = 8
IN_COLS = 1856
Z_COLS = 1920
Z_POOL, Z_Q, Z_MQ, Z_KV, Z_KR = 0, 512, 1024, 1536, 1792
WO_BLOCK_ORDER = (2, 3, 4, 5, 0, 1, 6, 7)

ADAM_LR = 0.001
ADAM_B1 = 0.9
ADAM_B2 = 0.999
ADAM_EPS = 1e-08
ADAM_WD = 0.01
ADAM_STEP = 10

N_DEV = 8
LANES = 128
VMEM_LIMIT = 56 * 1024 * 1024

BIG = ("w_in", "w_q_up", "w_kv_up", "w_mem_kv", "w_o", "w_gate", "w_up", "conv_w", "w_down")
SMALL = ("g_mix", "g_q_lat", "g_kv_lat", "g_q_mla", "g_k_mla", "w_pool", "pool_scale", "g_mem",
         "g_q_x", "g_k_x", "g_ffn", "conv_b")
WEIGHTS = ("g_mix", "w_in", "g_q_lat", "w_q_up", "g_kv_lat", "w_kv_up", "g_q_mla", "g_k_mla", "w_pool",
           "pool_scale", "g_mem", "w_mem_kv", "g_q_x", "g_k_x", "w_o", "g_ffn", "w_gate", "w_up", "conv_w",
           "conv_b", "w_down")


def _tile(n, t):
    if n <= t:
        return n
    for c in range(t - t % LANES, 0, -LANES):
        if n % c == 0:
            return c
    return n


def _params(*sem):
    return pltpu.CompilerParams(dimension_semantics=sem, vmem_limit_bytes=VMEM_LIMIT)


def _full(shape):
    nd = len(shape)
    return pl.BlockSpec(shape, lambda *_: (0,) * nd)


def _rows(tr, w, cb=0):
    return pl.BlockSpec((tr, w), lambda i: (i, cb))


def _rms_hat(x, n):
    r = lax.rsqrt(jnp.sum(x * x, axis=-1, keepdims=True) / n + NORM_EPS)
    return x * r, r


def _rms_bwd(dy, xhat, r, g, n):
    dxh = dy * g
    dx = r * (dxh - xhat * (jnp.sum(dxh * xhat, axis=-1, keepdims=True) / n))
    return dx, jnp.sum(dy * xhat, axis=0, keepdims=True)


def _mm(name, a, b, *, ta=False, tb=False, add=None, out_dtype=F32, tm=512, tn=512, tk=512):
    if ta:
        K, M = a.shape
    else:
        M, K = a.shape
    if tb:
        N, K2 = b.shape
    else:
        K2, N = b.shape
    assert K == K2, (name, a.shape, b.shape)
    tm, tn, tk = _tile(M, tm), _tile(N, tn), _tile(K, tk)
    nk = K // tk
    a_spec = pl.BlockSpec((tk, tm), lambda i, j, k: (k, i)) if ta else pl.BlockSpec((tm, tk), lambda i, j, k: (i, k))
    b_spec = pl.BlockSpec((tn, tk), lambda i, j, k: (j, k)) if tb else pl.BlockSpec((tk, tn), lambda i, j, k: (k, j))
    o_spec = pl.BlockSpec((tm, tn), lambda i, j, k: (i, j))
    dims = (((0,) if ta else (1,), (1,) if tb else (0,)), ((), ()))
    has_add = add is not None

    def body(*refs):
        if has_add:
            a_ref, b_ref, add_ref, o_ref = refs[:4]
            scr = refs[4:]
        else:
            a_ref, b_ref, o_ref = refs[:3]
            add_ref = None
            scr = refs[3:]
        part = lax.dot_general(a_ref[...], b_ref[...], dims, preferred_element_type=F32)
        if nk == 1:
            if has_add:
                part = part + add_ref[...]
            o_ref[...] = part.astype(o_ref.dtype)
            return
        acc = scr[0]
        k = pl.program_id(2)

        @pl.when(k == 0)
        def _():
            acc[...] = part

        @pl.when(k > 0)
        def _():
            acc[...] += part

        @pl.when(k == nk - 1)
        def _():
            r = acc[...]
            if has_add:
                r = r + add_ref[...]
            o_ref[...] = r.astype(o_ref.dtype)

    ins = [a, b] + ([add] if has_add else [])
    in_specs = [a_spec, b_spec] + ([o_spec] if has_add else [])
    return pl.pallas_call(
        body, name=name, grid=(M // tm, N // tn, nk),
        in_specs=in_specs, out_specs=o_spec,
        out_shape=jax.ShapeDtypeStruct((M, N), out_dtype),
        scratch_shapes=[pltpu.VMEM((tm, tn), F32)] if nk > 1 else [],
        compiler_params=_params("parallel", "parallel", "arbitrary"),
    )(*ins)


def _rms_fwd(name, x, g, tr=512):
    S, n = x.shape
    tr = _tile(S, tr)

    def body(x_ref, g_ref, h_ref):
        xhat, _ = _rms_hat(x_ref[...], n)
        h_ref[...] = (xhat * g_ref[...]).astype(BF16)

    return pl.pallas_call(
        body, name=name, grid=(S // tr,),
        in_specs=[_rows(tr, n), _full((1, n))], out_specs=_rows(tr, n),
        out_shape=jax.ShapeDtypeStruct((S, n), BF16),
        compiler_params=_params("parallel"),
    )(x, g)


def _rms_bwd_call(name, x, g, dy, extra, want_bf16, tr=256):
    S, n = x.shape
    tr = _tile(S, tr)

    def body(x_ref, g_ref, dy_ref, ex_ref, *outs):
        xhat, r = _rms_hat(x_ref[...], n)
        dx, dg = _rms_bwd(dy_ref[...], xhat, r, g_ref[...], n)
        dx = dx + ex_ref[...]
        outs[0][...] = dx
        if want_bf16:
            outs[1][...] = dx.astype(BF16)
        dg_ref = outs[-1]

        @pl.when(pl.program_id(0) == 0)
        def _():
            dg_ref[...] = jnp.zeros_like(dg_ref)

        dg_ref[...] += dg

    out_shape = [jax.ShapeDtypeStruct((S, n), F32)]
    out_specs = [_rows(tr, n)]
    if want_bf16:
        out_shape.append(jax.ShapeDtypeStruct((S, n), BF16))
        out_specs.append(_rows(tr, n))
    out_shape.append(jax.ShapeDtypeStruct((1, n), F32))
    out_specs.append(_full((1, n)))
    return pl.pallas_call(
        body, name=name, grid=(S // tr,),
        in_specs=[_rows(tr, n), _full((1, n)), _rows(tr, n), _rows(tr, n)],
        out_specs=out_specs, out_shape=out_shape,
        compiler_params=_params("arbitrary"),
    )(x, g, dy, extra)


def _lat_fwd(z, gq, gkv, tr=512):
    S = z.shape[0]
    tr = _tile(S, tr)

    def body(zq_ref, zkv_ref, gq_ref, gkv_ref, ql_ref, kvl_ref):
        xq, _ = _rms_hat(zq_ref[...], Q_LORA)
        ql_ref[...] = (xq * gq_ref[...]).astype(BF16)
        xkv, _ = _rms_hat(zkv_ref[...], KV_LORA)
        kvl_ref[...] = (xkv * gkv_ref[...]).astype(BF16)

    return pl.pallas_call(
        body, name="lat_fwd", grid=(S // tr,),
        in_specs=[_rows(tr, Q_LORA, Z_Q // Q_LORA), _rows(tr, KV_LORA, Z_KV // KV_LORA),
                  _full((1, Q_LORA)), _full((1, KV_LORA))],
        out_specs=[_rows(tr, Q_LORA), _rows(tr, KV_LORA)],
        out_shape=[jax.ShapeDtypeStruct((S, Q_LORA), BF16), jax.ShapeDtypeStruct((S, KV_LORA), BF16)],
        compiler_params=_params("parallel"),
    )(z, z, gq, gkv)


def _lat_bwd(z, gq, gkv, dql, dkvl, tr=512):
    S = z.shape[0]
    tr = _tile(S, tr)

    def body(zq_ref, zkv_ref, gq_ref, gkv_ref, dql_ref, dkvl_ref, dzq_ref, dzkv_ref, dgq_ref, dgkv_ref):
        @pl.when(pl.program_id(0) == 0)
        def _():
            dgq_ref[...] = jnp.zeros_like(dgq_ref)
            dgkv_ref[...] = jnp.zeros_like(dgkv_ref)

        xq, rq = _rms_hat(zq_ref[...], Q_LORA)
        dx, dg = _rms_bwd(dql_ref[...], xq, rq, gq_ref[...], Q_LORA)
        dzq_ref[...] = dx.astype(BF16)
        dgq_ref[...] += dg
        xkv, rkv = _rms_hat(zkv_ref[...], KV_LORA)
        dx, dg = _rms_bwd(dkvl_ref[...], xkv, rkv, gkv_ref[...], KV_LORA)
        dzkv_ref[...] = dx.astype(BF16)
        dgkv_ref[...] += dg

    return pl.pallas_call(
        body, name="lat_bwd", grid=(S // tr,),
        in_specs=[_rows(tr, Q_LORA, Z_Q // Q_LORA), _rows(tr, KV_LORA, Z_KV // KV_LORA),
                  _full((1, Q_LORA)), _full((1, KV_LORA)), _rows(tr, Q_LORA), _rows(tr, KV_LORA)],
        out_specs=[_rows(tr, Q_LORA), _rows(tr, KV_LORA), _full((1, Q_LORA)), _full((1, KV_LORA))],
        out_shape=[jax.ShapeDtypeStruct((S, Q_LORA), BF16), jax.ShapeDtypeStruct((S, KV_LORA), BF16),
                   jax.ShapeDtypeStruct((1, Q_LORA), F32), jax.ShapeDtypeStruct((1, KV_LORA), F32)],
        compiler_params=_params("arbitrary"),
    )(z, z, gq, gkv, dql, dkvl)


def _pool_d(ext, cur, t0, tr):
    t = t0 + lax.broadcasted_iota(jnp.int32, (tr, 1), 0)
    ds = []
    for gi, w in enumerate(POOL_WINDOWS):
        s = ext[:, gi * LANES:(gi + 1) * LANES]
        sh = 1
        while sh < w:
            s = s + pltpu.roll(s, sh, 0)
            sh *= 2
        cnt = jnp.minimum(t + 1, w).astype(F32)
        ds.append(s[POOL_HALO:] / cnt - cur[:, gi * LANES:(gi + 1) * LANES])
    return ds


def _pool_fwd(z, wp, scale, tr=512):
    S = z.shape[0]
    tr = _tile(S, tr)
    rb = tr // POOL_HALO

    def body(z_ref, zp_ref, wp_ref, sc_ref, y_ref):
        i = pl.program_id(0)
        cur = z_ref[...]
        halo = jnp.where(i == 0, 0.0, zp_ref[...])
        ds = _pool_d(jnp.concatenate([halo, cur], axis=0), cur, i * tr, tr)
        for gi in range(len(POOL_WINDOWS)):
            y = jnp.dot(ds[gi].astype(BF16), wp_ref[gi], preferred_element_type=F32)
            y_ref[:, gi * LANES:(gi + 1) * LANES] = (y * sc_ref[:, gi * LANES:(gi + 1) * LANES]).astype(BF16)

    return pl.pallas_call(
        body, name="pool_fwd", grid=(S // tr,),
        in_specs=[_rows(tr, POOL_WIDTH, Z_POOL // POOL_WIDTH),
                  pl.BlockSpec((POOL_HALO, POOL_WIDTH), lambda i: (jnp.maximum(i * rb - 1, 0), 0)),
                  _full(wp.shape), _full((1, POOL_WIDTH))],
        out_specs=_rows(tr, POOL_WIDTH),
        out_shape=jax.ShapeDtypeStruct((S, POOL_WIDTH), BF16),
        compiler_params=_params("parallel"),
    )(z, z, wp, scale)


def _pool_bwd(z, dyc, wp, scale, dy_cb, tr=512):
    S = z.shape[0]
    tr = _tile(S, tr)
    rb = tr // POOL_HALO
    nhb = S // POOL_HALO
    ng = len(POOL_WINDOWS)

    def body(z_ref, zp_ref, dy_ref, dyn_ref, wp_ref, sc_ref, dz_ref, dwp_ref, dsc_ref):
        i = pl.program_id(0)

        @pl.when(i == 0)
        def _():
            dwp_ref[...] = jnp.zeros_like(dwp_ref)
            dsc_ref[...] = jnp.zeros_like(dsc_ref)

        cur = z_ref[...]
        halo = jnp.where(i == 0, 0.0, zp_ref[...])
        ds = _pool_d(jnp.concatenate([halo, cur], axis=0), cur, i * tr, tr)
        dy_cur = dy_ref[...]
        dy_next = jnp.where(i == pl.num_programs(0) - 1, 0.0, dyn_ref[...])
        dy_ext = jnp.concatenate([dy_cur, dy_next], axis=0)
        n_ext = tr + POOL_HALO
        t_ext = i * tr + lax.broadcasted_iota(jnp.int32, (n_ext, 1), 0)
        for gi, w in enumerate(POOL_WINDOWS):
            cols = slice(gi * LANES, (gi + 1) * LANES)
            d_b = ds[gi].astype(BF16)
            y_pre = jnp.dot(d_b, wp_ref[gi], preferred_element_type=F32)
            dsc_ref[:, cols] += jnp.sum(dy_cur[:, cols] * y_pre, axis=0, keepdims=True)
            dys = (dy_ext[:, cols] * sc_ref[:, cols]).astype(BF16)
            dwp_ref[gi] += lax.dot_general(d_b, dys[:tr], (((0,), (0,)), ((), ())), preferred_element_type=F32)
            dd = lax.dot_general(dys, wp_ref[gi], (((1,), (1,)), ((), ())), preferred_element_type=F32)
            s = dd / jnp.minimum(t_ext + 1, w).astype(F32)
            sh = 1
            while sh < w:
                s = s + pltpu.roll(s, n_ext - sh, 0)
                sh *= 2
            dz_ref[:, cols] = (s[:tr] - dd[:tr]).astype(BF16)

    return pl.pallas_call(
        body, name="pool_bwd", grid=(S // tr,),
        in_specs=[_rows(tr, POOL_WIDTH, Z_POOL // POOL_WIDTH),
                  pl.BlockSpec((POOL_HALO, POOL_WIDTH), lambda i: (jnp.maximum(i * rb - 1, 0), 0)),
                  _rows(tr, POOL_WIDTH, dy_cb),
                  pl.BlockSpec((POOL_HALO, POOL_WIDTH), lambda i: (jnp.minimum((i + 1) * rb, nhb - 1), dy_cb)),
                  _full(wp.shape), _full((1, POOL_WIDTH))],
        out_specs=[_rows(tr, POOL_WIDTH), _full((ng, LANES, LANES)), _full((1, POOL_WIDTH))],
        out_shape=[jax.ShapeDtypeStruct((S, POOL_WIDTH), BF16), jax.ShapeDtypeStruct((ng, LANES, LANES), F32),
                   jax.ShapeDtypeStruct((1, POOL_WIDTH), F32)],
        compiler_params=_params("arbitrary"),
    )(z, z, dyc, dyc, wp, scale)


def _rope_tables(S):
    half = MLA_ROPE // 2
    inv_freq = 1.0 / (ROPE_THETA ** (jnp.arange(half, dtype=F32) / half))
    ang = jnp.arange(S).astype(F32)[:, None] * inv_freq[None, :]
    cos, sin = jnp.cos(ang), jnp.sin(ang)
    z32 = jnp.zeros((S, half), F32)
    z64 = jnp.zeros((S, LANES - MLA_ROPE), F32)
    cos_t = jnp.concatenate([cos, cos, z64], axis=1)
    sin_a = jnp.concatenate([-sin, z32, z64], axis=1)
    sin_b = jnp.concatenate([z32, sin, z64], axis=1)
    return cos_t, sin_a, sin_b


def _rope(x, cos_t, sin_a, sin_b):
    return x * cos_t + pltpu.roll(x, LANES - 32, 1) * sin_a + pltpu.roll(x, 32, 1) * sin_b


def _rope_t(d, cos_t, sin_a, sin_b):
    return d * cos_t + pltpu.roll(d * sin_a, 32, 1) + pltpu.roll(d * sin_b, LANES - 32, 1)


def _mla_prep(q_raw, kv_raw, z, tabs, gq, gk, tr=256):
    S = z.shape[0]
    tr = _tile(S, tr)
    scale = 1.0 / math.sqrt(MLA_QK)
    W = MLA_HEADS * MLA_QK_PAD

    def body(q_ref, kv_ref, kr_ref, c_ref, sa_ref, sb_ref, gq_ref, gk_ref, qo_ref, ko_ref, vo_ref):
        tab = (c_ref[...], sa_ref[...], sb_ref[...])
        kr = kr_ref[...]
        kr_ss = jnp.sum(kr * kr, axis=-1, keepdims=True)
        gqn, gqr = gq_ref[:, :LANES], gq_ref[:, LANES:]
        gkn, gkr = gk_ref[:, :LANES], gk_ref[:, LANES:]
        for h in range(MLA_HEADS):
            c0 = h * MLA_QK_PAD
            qn = q_ref[:, c0:c0 + LANES]
            qr = q_ref[:, c0 + LANES:c0 + 2 * LANES]
            r = lax.rsqrt((jnp.sum(qn * qn, -1, keepdims=True) + jnp.sum(qr * qr, -1, keepdims=True)) / MLA_QK
                          + NORM_EPS)
            qo_ref[h, :, :LANES] = (qn * r * gqn * scale).astype(BF16)
            qo_ref[h, :, LANES:] = (_rope(qr * r * gqr, *tab) * scale).astype(BF16)
            kn = kv_ref[:, c0:c0 + LANES]
            r = lax.rsqrt((jnp.sum(kn * kn, -1, keepdims=True) + kr_ss) / MLA_QK + NORM_EPS)
            ko_ref[h, :, :LANES] = (kn * r * gkn).astype(BF16)
            ko_ref[h, :, LANES:] = _rope(kr * r * gkr, *tab).astype(BF16)
            vo_ref[h] = kv_ref[:, c0 + LANES:c0 + 2 * LANES].astype(BF16)

    hs = lambda w: pl.BlockSpec((MLA_HEADS, tr, w), lambda i: (0, i, 0))
    return pl.pallas_call(
        body, name="mla_prep", grid=(S // tr,),
        in_specs=[_rows(tr, W), _rows(tr, W), _rows(tr, LANES, Z_KR // LANES),
                  _rows(tr, LANES), _rows(tr, LANES), _rows(tr, LANES),
                  _full((1, MLA_QK_PAD)), _full((1, MLA_QK_PAD))],
        out_specs=[hs(MLA_QK_PAD), hs(MLA_QK_PAD), hs(MLA_V)],
        out_shape=[jax.ShapeDtypeStruct((MLA_HEADS, S, MLA_QK_PAD), BF16),
                   jax.ShapeDtypeStruct((MLA_HEADS, S, MLA_QK_PAD), BF16),
                   jax.ShapeDtypeStruct((MLA_HEADS, S, MLA_V), BF16)],
        compiler_params=_params("parallel"),
    )(q_raw, kv_raw, z, *tabs, gq, gk)


def _mla_prep_bwd(dq, dk, dv, q_raw, kv_raw, z, tabs, gq, gk, tr=256):
    S = z.shape[0]
    tr = _tile(S, tr)
    scale = 1.0 / math.sqrt(MLA_QK)
    W = MLA_HEADS * MLA_QK_PAD

    def body(dq_ref, dk_ref, dv_ref, q_ref, kv_ref, kr_ref, c_ref, sa_ref, sb_ref, gq_ref, gk_ref,
             dqr_ref, dkvr_ref, dkr_ref, dgq_ref, dgk_ref):
        @pl.when(pl.program_id(0) == 0)
        def _():
            dgq_ref[...] = jnp.zeros_like(dgq_ref)
            dgk_ref[...] = jnp.zeros_like(dgk_ref)

        tab = (c_ref[...], sa_ref[...], sb_ref[...])
        kr = kr_ref[...]
        kr_ss = jnp.sum(kr * kr, axis=-1, keepdims=True)
        gqn, gqr = gq_ref[:, :LANES], gq_ref[:, LANES:]
        gkn, gkr = gk_ref[:, :LANES], gk_ref[:, LANES:]
        dkr_sum = jnp.zeros((tr, LANES), F32)
        dgq_n = jnp.zeros((1, LANES), F32)
        dgq_r = jnp.zeros((1, LANES), F32)
        dgk_n = jnp.zeros((1, LANES), F32)
        dgk_r = jnp.zeros((1, LANES), F32)

        def head_norm_bwd(xn, xr, r, dyn, dyr, gn, gr):
            hn, hr = xn * r, xr * r
            dxn, dxr = dyn * gn, dyr * gr
            mt = (jnp.sum(dxn * hn, -1, keepdims=True) + jnp.sum(dxr * hr, -1, keepdims=True)) / MLA_QK
            return (r * (dxn - hn * mt), r * (dxr - hr * mt),
                    jnp.sum(dyn * hn, axis=0, keepdims=True), jnp.sum(dyr * hr, axis=0, keepdims=True))

        for h in range(MLA_HEADS):
            c0 = h * MLA_QK_PAD
            qn = q_ref[:, c0:c0 + LANES]
            qr = q_ref[:, c0 + LANES:c0 + 2 * LANES]
            r = lax.rsqrt((jnp.sum(qn * qn, -1, keepdims=True) + jnp.sum(qr * qr, -1, keepdims=True)) / MLA_QK
                          + NORM_EPS)
            dyn = dq_ref[h, :, :LANES] * scale
            dyr = _rope_t(dq_ref[h, :, LANES:] * scale, *tab)
            dxn, dxr, gn_, gr_ = head_norm_bwd(qn, qr, r, dyn, dyr, gqn, gqr)
            dgq_n += gn_
            dgq_r += gr_
            dqr_ref[:, c0:c0 + LANES] = dxn.astype(BF16)
            dqr_ref[:, c0 + LANES:c0 + 2 * LANES] = dxr.astype(BF16)

            kn = kv_ref[:, c0:c0 + LANES]
            r = lax.rsqrt((jnp.sum(kn * kn, -1, keepdims=True) + kr_ss) / MLA_QK + NORM_EPS)
            dyn = dk_ref[h, :, :LANES]
            dyr = _rope_t(dk_ref[h, :, LANES:], *tab)
            dxn, dxr, gn_, gr_ = head_norm_bwd(kn, kr, r, dyn, dyr, gkn, gkr)
            dgk_n += gn_
            dgk_r += gr_
            dkr_sum += dxr
            dkvr_ref[:, c0:c0 + LANES] = dxn.astype(BF16)
            dkvr_ref[:, c0 + LANES:c0 + 2 * LANES] = dv_ref[h].astype(BF16)

        dkr_ref[...] = dkr_sum.astype(BF16)
        dgq_ref[:, :LANES] += dgq_n
        dgq_ref[:, LANES:] += dgq_r
        dgk_ref[:, :LANES] += dgk_n
        dgk_ref[:, LANES:] += dgk_r

    hs = lambda w: pl.BlockSpec((MLA_HEADS, tr, w), lambda i: (0, i, 0))
    return pl.pallas_call(
        body, name="mla_prep_bwd", grid=(S // tr,),
        in_specs=[hs(MLA_QK_PAD), hs(MLA_QK_PAD), hs(MLA_V), _rows(tr, W), _rows(tr, W),
                  _rows(tr, LANES, Z_KR // LANES), _rows(tr, LANES), _rows(tr, LANES), _rows(tr, LANES),
                  _full((1, MLA_QK_PAD)), _full((1, MLA_QK_PAD))],
        out_specs=[_rows(tr, W), _rows(tr, W), _rows(tr, LANES), _full((1, MLA_QK_PAD)), _full((1, MLA_QK_PAD))],
        out_shape=[jax.ShapeDtypeStruct((S, W), BF16), jax.ShapeDtypeStruct((S, W), BF16),
                   jax.ShapeDtypeStruct((S, LANES), BF16),
                   jax.ShapeDtypeStruct((1, MLA_QK_PAD), F32), jax.ShapeDtypeStruct((1, MLA_QK_PAD), F32)],
        compiler_params=_params("arbitrary"),
    )(dq, dk, dv, q_raw, kv_raw, z, *tabs, gq, gk)


_NT = (((1,), (1,)), ((), ()))
_TN = (((0,), (0,)), ((), ()))


def _causal_mask(s, i, j, tq, tk):
    row = i * tq + lax.broadcasted_iota(jnp.int32, (tq, tk), 0)
    col = j * tk + lax.broadcasted_iota(jnp.int32, (tq, tk), 1)
    return jnp.where(col <= row, s, -jnp.inf)


def _flash_fwd(q, k, v, tq=1024, tk=1024, rc=1024):
    H, S, dq = q.shape
    dv = v.shape[-1]
    tq, tk = _tile(S, tq), _tile(S, tk)
    assert tq % tk == 0 and tq % rc == 0 and dv == LANES
    nq, nd = S // tq, tq // tk

    def body(q_ref, k_ref, v_ref, o_ref, lse_ref, m_sc, l_sc, acc_sc):
        i = pl.program_id(1)
        m_sc[...] = jnp.full_like(m_sc, -jnp.inf)
        l_sc[...] = jnp.zeros_like(l_sc)
        acc_sc[...] = jnp.zeros_like(acc_sc)

        def step(j, masked):
            off = pl.multiple_of(j * tk, tk)
            kb = k_ref[pl.ds(off, tk), :]
            vb = v_ref[pl.ds(off, tk), :]
            for r in range(tq // rc):
                rows = slice(r * rc, (r + 1) * rc)
                s = lax.dot_general(q_ref[rows, :], kb, _NT, preferred_element_type=F32)
                if masked:
                    row = i * tq + r * rc + lax.broadcasted_iota(jnp.int32, (rc, tk), 0)
                    col = j * tk + lax.broadcasted_iota(jnp.int32, (rc, tk), 1)
                    s = jnp.where(col <= row, s, -jnp.inf)
                parts = [s[:, c * LANES:(c + 1) * LANES] for c in range(tk // LANES)]
                m_cur = parts[0]
                for pt in parts[1:]:
                    m_cur = jnp.maximum(m_cur, pt)
                m_prev = m_sc[rows, :]
                m_new = jnp.maximum(m_prev, jnp.max(m_cur, axis=-1, keepdims=True))
                alpha = jnp.exp(m_prev - m_new)
                ps = [jnp.exp(pt - m_new) for pt in parts]
                l_new = alpha * l_sc[rows, :]
                for pv_ in ps:
                    l_new = l_new + pv_
                l_sc[rows, :] = l_new
                p = jnp.concatenate(ps, axis=1).astype(BF16)
                acc_sc[rows, :] = alpha * acc_sc[rows, :] + jnp.dot(p, vb, preferred_element_type=F32)
                m_sc[rows, :] = m_new

        n_full = i * nd

        def full_step(j, carry):
            step(j, False)
            return carry

        lax.fori_loop(0, n_full, full_step, 0)
        for d in range(nd):
            step(n_full + d, True)
        l = jnp.sum(l_sc[...], axis=-1, keepdims=True)
        o_ref[...] = (acc_sc[...] / l).astype(BF16)
        lse_ref[...] = m_sc[...] + jnp.log(l)

    return pl.pallas_call(
        body, name="flash_fwd", grid=(H, nq),
        in_specs=[pl.BlockSpec((None, tq, dq), lambda h, i: (h, i, 0)),
                  pl.BlockSpec((None, S, dq), lambda h, i: (h, 0, 0)),
                  pl.BlockSpec((None, S, dv), lambda h, i: (h, 0, 0))],
        out_specs=[pl.BlockSpec((tq, dv), lambda h, i: (i, h)),
                   pl.BlockSpec((None, tq, LANES), lambda h, i: (h, i, 0))],
        out_shape=[jax.ShapeDtypeStruct((S, H * dv), BF16), jax.ShapeDtypeStruct((H, S, LANES), F32)],
        scratch_shapes=[pltpu.VMEM((tq, LANES), F32), pltpu.VMEM((tq, LANES), F32), pltpu.VMEM((tq, dv), F32)],
        compiler_params=_params("parallel", "arbitrary"),
    )(q, k, v)


def _attn_bwd_prep(dyc, o, tr=512):
    S = o.shape[0]
    H = o.shape[1] // MLA_V
    tr = _tile(S, tr)

    def body(dy_ref, o_ref, do_ref, dl_ref):
        dy = dy_ref[...]
        do_ref[...] = dy.astype(BF16)
        dl = jnp.sum(dy * o_ref[...].astype(F32), axis=-1, keepdims=True)
        dl_ref[...] = jnp.broadcast_to(dl, (tr, LANES))

    return pl.pallas_call(
        body, name="attn_bwd_prep", grid=(S // tr, H),
        in_specs=[pl.BlockSpec((tr, MLA_V), lambda i, h: (i, h)), pl.BlockSpec((tr, MLA_V), lambda i, h: (i, h))],
        out_specs=[pl.BlockSpec((tr, MLA_V), lambda i, h: (i, h)),
                   pl.BlockSpec((None, tr, LANES), lambda i, h: (h, i, 0))],
        out_shape=[jax.ShapeDtypeStruct((S, H * MLA_V), BF16), jax.ShapeDtypeStruct((H, S, LANES), F32)],
        compiler_params=_params("parallel", "parallel"),
    )(dyc, o)


def _flash_bwd(q, k, v, do, lse, delta, tq=1024, tk=512):
    H, S, dq = q.shape
    dv = v.shape[-1]
    tq, tk = _tile(S, tq), _tile(S, tk)
    assert tq % tk == 0
    nq, nd = S // tq, tq // tk

    def body(q_ref, k_ref, v_ref, do_ref, lse_ref, dl_ref, dq_ref, dk_ref, dv_ref):
        i = pl.program_id(1)

        @pl.when(i == 0)
        def _():
            dk_ref[...] = jnp.zeros_like(dk_ref)
            dv_ref[...] = jnp.zeros_like(dv_ref)

        dq_ref[...] = jnp.zeros_like(dq_ref)

        def step(j, masked):
            off = pl.multiple_of(j * tk, tk)
            qv = q_ref[...]
            dov = do_ref[...]
            kb = k_ref[pl.ds(off, tk), :]
            s = lax.dot_general(qv, kb, _NT, preferred_element_type=F32)
            if masked:
                s = _causal_mask(s, i, j, tq, tk)
            p = jnp.exp(s - lse_ref[:, :1])
            dv_ref[pl.ds(off, tk), :] += lax.dot_general(p.astype(BF16), dov, _TN, preferred_element_type=F32)
            dp = lax.dot_general(dov, v_ref[pl.ds(off, tk), :], _NT, preferred_element_type=F32)
            ds = (p * (dp - dl_ref[:, :1])).astype(BF16)
            dk_ref[pl.ds(off, tk), :] += lax.dot_general(ds, qv, _TN, preferred_element_type=F32)
            dq_ref[...] += jnp.dot(ds, kb, preferred_element_type=F32)

        n_full = i * nd

        def full_step(j, carry):
            step(j, False)
            return carry

        lax.fori_loop(0, n_full, full_step, 0)
        for d in range(nd):
            step(n_full + d, True)

    return pl.pallas_call(
        body, name="flash_bwd", grid=(H, nq),
        in_specs=[pl.BlockSpec((None, tq, dq), lambda h, i: (h, i, 0)),
                  pl.BlockSpec((None, S, dq), lambda h, i: (h, 0, 0)),
                  pl.BlockSpec((None, S, dv), lambda h, i: (h, 0, 0)),
                  pl.BlockSpec((tq, dv), lambda h, i: (i, h)),
                  pl.BlockSpec((None, tq, LANES), lambda h, i: (h, i, 0)),
                  pl.BlockSpec((None, tq, LANES), lambda h, i: (h, i, 0))],
        out_specs=[pl.BlockSpec((None, tq, dq), lambda h, i: (h, i, 0)),
                   pl.BlockSpec((None, S, dq), lambda h, i: (h, 0, 0)),
                   pl.BlockSpec((None, S, dv), lambda h, i: (h, 0, 0))],
        out_shape=[jax.ShapeDtypeStruct((H, S, dq), F32), jax.ShapeDtypeStruct((H, S, dq), F32),
                   jax.ShapeDtypeStruct((H, S, dv), F32)],
        compiler_params=_params("parallel", "arbitrary"),
    )(q, k, v, do, lse, delta)


def _mem_norm(mem, g):
    n = mem.shape[1]

    def body(m_ref, g_ref, o_ref):
        xhat, _ = _rms_hat(m_ref[...], n)
        o_ref[...] = (xhat * g_ref[...]).astype(BF16)

    return pl.pallas_call(body, name="mem_norm", out_shape=jax.ShapeDtypeStruct(mem.shape, BF16),
                          compiler_params=pltpu.CompilerParams(vmem_limit_bytes=VMEM_LIMIT))(mem, g)


def _memkv_prep(mkv, gk):
    M = mkv.shape[0]

    def body(mkv_ref, g_ref, k_ref, v_ref):
        for h in range(X_HEADS):
            cols = slice(h * X_HEAD_DIM, (h + 1) * X_HEAD_DIM)
            xhat, _ = _rms_hat(mkv_ref[:, cols], X_HEAD_DIM)
            k_ref[:, cols] = (xhat * g_ref[...]).astype(BF16)
        v_ref[...] = mkv_ref[:, X_WIDTH:].astype(BF16)

    return pl.pallas_call(
        body, name="memkv_prep",
        out_shape=[jax.ShapeDtypeStruct((M, X_WIDTH), BF16), jax.ShapeDtypeStruct((M, X_WIDTH), BF16)],
    )(mkv, gk)


def _memkv_bwd(dk, dv, mkv, gk):
    M = mkv.shape[0]

    def body(dk_ref, dv_ref, mkv_ref, g_ref, dmkv_ref, dg_ref):
        dg = jnp.zeros((1, X_HEAD_DIM), F32)
        for h in range(X_HEADS):
            cols = slice(h * X_HEAD_DIM, (h + 1) * X_HEAD_DIM)
            xhat, r = _rms_hat(mkv_ref[:, cols], X_HEAD_DIM)
            dx, dgh = _rms_bwd(dk_ref[:, cols], xhat, r, g_ref[...], X_HEAD_DIM)
            dmkv_ref[:, cols] = dx.astype(BF16)
            dg += dgh
        dmkv_ref[:, X_WIDTH:] = dv_ref[...].astype(BF16)
        dg_ref[...] = dg

    return pl.pallas_call(
        body, name="memkv_bwd",
        out_shape=[jax.ShapeDtypeStruct((M, 2 * X_WIDTH), BF16), jax.ShapeDtypeStruct((1, X_HEAD_DIM), F32)],
    )(dk, dv, mkv, gk)


def _mem_gain_bwd(mem, dmn):
    n = mem.shape[1]

    def body(m_ref, d_ref, dg_ref):
        xhat, _ = _rms_hat(m_ref[...], n)
        dg_ref[...] = jnp.sum(d_ref[...] * xhat, axis=0, keepdims=True)

    return pl.pallas_call(body, name="mem_gain_bwd", out_shape=jax.ShapeDtypeStruct((1, n), F32),
                          compiler_params=pltpu.CompilerParams(vmem_limit_bytes=VMEM_LIMIT))(mem, dmn)


def _memx_scores(zq_ref, g, kx_ref, h):
    cols = slice(h * X_HEAD_DIM, (h + 1) * X_HEAD_DIM)
    xhat, r = _rms_hat(zq_ref[:, cols], X_HEAD_DIM)
    qn = (xhat * g).astype(BF16)
    s = lax.dot_general(qn, kx_ref[:, cols], _NT, preferred_element_type=F32) * (1.0 / math.sqrt(X_HEAD_DIM))
    p = jnp.exp(s - jnp.max(s, axis=-1, keepdims=True))
    return cols, xhat, r, qn, p, jnp.sum(p, axis=-1, keepdims=True)


def _memx_fwd(z, g, kx, vx, tr=512):
    S = z.shape[0]
    tr = _tile(S, tr)
    M = kx.shape[0]

    def body(zq_ref, g_ref, kx_ref, vx_ref, y_ref):
        for h in range(X_HEADS):
            cols, _, _, _, p, l = _memx_scores(zq_ref, g_ref[...], kx_ref, h)
            o = jnp.dot(p.astype(BF16), vx_ref[:, cols], preferred_element_type=F32)
            y_ref[:, cols] = (o / l).astype(BF16)

    return pl.pallas_call(
        body, name="memx_fwd", grid=(S // tr,),
        in_specs=[_rows(tr, X_WIDTH, Z_MQ // X_WIDTH), _full((1, X_HEAD_DIM)), _full((M, X_WIDTH)),
                  _full((M, X_WIDTH))],
        out_specs=_rows(tr, X_WIDTH),
        out_shape=jax.ShapeDtypeStruct((S, X_WIDTH), BF16),
        compiler_params=_params("parallel"),
    )(z, g, kx, vx)


def _memx_bwd(z, g, kx, vx, dyc, dy_cb, tr=512):
    S = z.shape[0]
    tr = _tile(S, tr)
    M = kx.shape[0]
    scale = 1.0 / math.sqrt(X_HEAD_DIM)

    def body(zq_ref, g_ref, kx_ref, vx_ref, dy_ref, dz_ref, dk_ref, dv_ref, dg_ref):
        @pl.when(pl.program_id(0) == 0)
        def _():
            dk_ref[...] = jnp.zeros_like(dk_ref)
            dv_ref[...] = jnp.zeros_like(dv_ref)
            dg_ref[...] = jnp.zeros_like(dg_ref)

        gv = g_ref[...]
        for h in range(X_HEADS):
            cols, xhat, r, qn, p, l = _memx_scores(zq_ref, gv, kx_ref, h)
            p = p / l
            do = dy_ref[:, cols].astype(BF16)
            dv_ref[:, cols] += lax.dot_general(p.astype(BF16), do, _TN, preferred_element_type=F32)
            dp = lax.dot_general(do, vx_ref[:, cols], _NT, preferred_element_type=F32)
            ds = (p * (dp - jnp.sum(dp * p, axis=-1, keepdims=True)) * scale).astype(BF16)
            dqn = jnp.dot(ds, kx_ref[:, cols], preferred_element_type=F32)
            dk_ref[:, cols] += lax.dot_general(ds, qn, _TN, preferred_element_type=F32)
            dx, dg = _rms_bwd(dqn, xhat, r, gv, X_HEAD_DIM)
            dz_ref[:, cols] = dx.astype(BF16)
            dg_ref[...] += dg

    return pl.pallas_call(
        body, name="memx_bwd", grid=(S // tr,),
        in_specs=[_rows(tr, X_WIDTH, Z_MQ // X_WIDTH), _full((1, X_HEAD_DIM)), _full((M, X_WIDTH)),
                  _full((M, X_WIDTH)), _rows(tr, X_WIDTH, dy_cb)],
        out_specs=[_rows(tr, X_WIDTH), _full((M, X_WIDTH)), _full((M, X_WIDTH)), _full((1, X_HEAD_DIM))],
        out_shape=[jax.ShapeDtypeStruct((S, X_WIDTH), BF16), jax.ShapeDtypeStruct((M, X_WIDTH), F32),
                   jax.ShapeDtypeStruct((M, X_WIDTH), F32), jax.ShapeDtypeStruct((1, X_HEAD_DIM), F32)],
        compiler_params=_params("arbitrary"),
    )(z, g, kx, vx, dyc)


def _conv_gc(g_ext, w_ref, b_ref, n_ext):
    g1 = pltpu.roll(g_ext, 1, 0)
    g2 = pltpu.roll(g_ext, 2, 0)
    gc = b_ref[...] + w_ref[0:1, :] * g2
    gc = gc + w_ref[1:2, :] * g1
    gc = gc + w_ref[2:3, :] * g_ext
    return gc, g1, g2


def _convglu_fwd(g, u, cw, cb, tr=512, tc=512):
    S, F = g.shape
    tr, tc = _tile(S, tr), _tile(F, tc)
    rb = tr // CONV_HALO

    def body(g_ref, gp_ref, u_ref, w_ref, b_ref, a_ref):
        i = pl.program_id(1)
        halo = jnp.where(i == 0, 0.0, gp_ref[...])
        g_ext = jnp.concatenate([halo, g_ref[...]], axis=0)
        gc, _, _ = _conv_gc(g_ext, w_ref, b_ref, tr + CONV_HALO)
        gc = gc[CONV_HALO:]
        sig = 1.0 / (1.0 + jnp.exp(-gc))
        a_ref[...] = (gc * sig * u_ref[...]).astype(BF16)

    return pl.pallas_call(
        body, name="convglu_fwd", grid=(F // tc, S // tr),
        in_specs=[pl.BlockSpec((tr, tc), lambda j, i: (i, j)),
                  pl.BlockSpec((CONV_HALO, tc), lambda j, i: (jnp.maximum(i * rb - 1, 0), j)),
                  pl.BlockSpec((tr, tc), lambda j, i: (i, j)),
                  pl.BlockSpec((3, tc), lambda j, i: (0, j)), pl.BlockSpec((1, tc), lambda j, i: (0, j))],
        out_specs=pl.BlockSpec((tr, tc), lambda j, i: (i, j)),
        out_shape=jax.ShapeDtypeStruct((S, F), BF16),
        compiler_params=_params("parallel", "parallel"),
    )(g, g, u, cw, cb)


def _convglu_bwd(g, u, da, cw, cb, tr=512, tc=512):
    S, F = g.shape
    tr, tc = _tile(S, tr), _tile(F, tc)
    rb = tr // CONV_HALO
    nhb = S // CONV_HALO
    H = CONV_HALO

    def body(g_ref, gp_ref, gn_ref, u_ref, un_ref, da_ref, dan_ref, w_ref, b_ref,
             dg_ref, du_ref, dw_ref, db_ref):
        i = pl.program_id(1)
        last = i == pl.num_programs(1) - 1

        @pl.when(i == 0)
        def _():
            dw_ref[...] = jnp.zeros_like(dw_ref)
            db_ref[...] = jnp.zeros_like(db_ref)

        g_prev = jnp.where(i == 0, 0.0, gp_ref[...])
        g_cur = g_ref[...]
        g_ext = jnp.concatenate([g_prev, g_cur, gn_ref[...]], axis=0)
        gc, g1, g2 = _conv_gc(g_ext, w_ref, b_ref, tr + 2 * H)
        gc = gc[H:]
        ux = jnp.concatenate([u_ref[...], un_ref[...]], axis=0)
        dax = jnp.concatenate([da_ref[...], jnp.where(last, 0.0, dan_ref[...])], axis=0)
        sig = 1.0 / (1.0 + jnp.exp(-gc))
        du_ref[...] = (dax[:tr] * (gc[:tr] * sig[:tr])).astype(BF16)
        dgc = dax * ux * (sig * (1.0 + gc * (1.0 - sig)))
        n = tr + H
        d1 = pltpu.roll(dgc, n - 1, 0)[:tr]
        d2 = pltpu.roll(dgc, n - 2, 0)[:tr]
        d0 = dgc[:tr]
        dg_ref[...] = (w_ref[2:3, :] * d0 + w_ref[1:2, :] * d1 + w_ref[0:1, :] * d2).astype(BF16)
        db_ref[...] += jnp.sum(d0, axis=0, keepdims=True)
        dw_ref[0:1, :] += jnp.sum(d0 * g2[H:H + tr], axis=0, keepdims=True)
        dw_ref[1:2, :] += jnp.sum(d0 * g1[H:H + tr], axis=0, keepdims=True)
        dw_ref[2:3, :] += jnp.sum(d0 * g_cur, axis=0, keepdims=True)

    cur = pl.BlockSpec((tr, tc), lambda j, i: (i, j))
    prv = pl.BlockSpec((H, tc), lambda j, i: (jnp.maximum(i * rb - 1, 0), j))
    nxt = pl.BlockSpec((H, tc), lambda j, i: (jnp.minimum((i + 1) * rb, nhb - 1), j))
    return pl.pallas_call(
        body, name="convglu_bwd", grid=(F // tc, S // tr),
        in_specs=[cur, prv, nxt, cur, nxt, cur, nxt,
                  pl.BlockSpec((3, tc), lambda j, i: (0, j)), pl.BlockSpec((1, tc), lambda j, i: (0, j))],
        out_specs=[cur, cur, pl.BlockSpec((3, tc), lambda j, i: (0, j)), pl.BlockSpec((1, tc), lambda j, i: (0, j))],
        out_shape=[jax.ShapeDtypeStruct((S, F), BF16), jax.ShapeDtypeStruct((S, F), BF16),
                   jax.ShapeDtypeStruct((3, F), F32), jax.ShapeDtypeStruct((1, F), F32)],
        compiler_params=_params("parallel", "arbitrary"),
    )(g, g, g, u, u, da, da, cw, cb)


def _loss_head(y, target, tr=256):
    S, n = y.shape
    tr = _tile(S, tr)

    def body(y_ref, t_ref, dy_ref, dyb_ref, loss_ref):
        @pl.when(pl.program_id(0) == 0)
        def _():
            loss_ref[...] = jnp.zeros_like(loss_ref)

        err = y_ref[...] - t_ref[...]
        dy = err / n
        dy_ref[...] = dy
        dyb_ref[...] = dy.astype(BF16)
        per_row = jnp.sum(err * err, axis=-1, keepdims=True) / n
        loss_ref[...] += 0.5 * jnp.sum(per_row)

    return pl.pallas_call(
        body, name="loss_head", grid=(S // tr,),
        in_specs=[_rows(tr, n), _rows(tr, n)],
        out_specs=[_rows(tr, n), _rows(tr, n), _full((8, LANES))],
        out_shape=[jax.ShapeDtypeStruct((S, n), F32), jax.ShapeDtypeStruct((S, n), BF16),
                   jax.ShapeDtypeStruct((8, LANES), F32)],
        compiler_params=_params("arbitrary"),
    )(y, target)


def _local_step(x, mem, target, W):
    S = x.shape[0]
    tabs = _rope_tables(S)
    G = {}

    h = _rms_fwd("rms1_fwd", x, W["g_mix"])
    z = _mm("mm_in", h, W["w_in"], tm=512, tn=Z_COLS, tk=D_MODEL)
    y_pool = _pool_fwd(z, W["w_pool"], W["pool_scale"])
    ql, kvl = _lat_fwd(z, W["g_q_lat"], W["g_kv_lat"])
    q_raw = _mm("mm_q_up", ql, W["w_q_up"], tm=1024, tn=2048, tk=Q_LORA)
    kv_raw = _mm("mm_kv_up", kvl, W["w_kv_up"], tm=1024, tn=2048, tk=KV_LORA)
    q, k, v = _mla_prep(q_raw, kv_raw, z, tabs, W["g_q_mla"], W["g_k_mla"])
    o_mla, lse = _flash_fwd(q, k, v)
    mn = _mem_norm(mem, W["g_mem"])
    mkv = _mm("mm_mem_kv", mn, W["w_mem_kv"], tm=256, tn=1024, tk=D_MODEL)
    kx, vx = _memkv_prep(mkv, W["g_k_x"])
    y_mem = _memx_fwd(z, W["g_q_x"], kx, vx)
    ycat = jnp.concatenate([o_mla, y_pool, y_mem], axis=1)
    x1 = _mm("mm_o", ycat, W["w_o"], add=x, tm=512, tn=2048, tk=D_MODEL)
    h2 = _rms_fwd("rms2_fwd", x1, W["g_ffn"])
    g = _mm("mm_gate", h2, W["w_gate"], tm=1024, tn=512, tk=D_MODEL)
    u = _mm("mm_up", h2, W["w_up"], tm=1024, tn=512, tk=D_MODEL)
    a = _convglu_fwd(g, u, W["conv_w"], W["conv_b"])
    y = _mm("mm_down", a, W["w_down"], add=x1, tm=1024, tn=512, tk=D_FF)
    dy, dyb, loss_part = _loss_head(y, target)

    da = _mm("mm_down_dx", dyb, W["w_down"], tb=True, tm=1024, tn=512, tk=D_MODEL)
    G["w_down"] = _mm("mm_down_dw", a, dyb, ta=True, tm=512, tn=2048, tk=2048)
    dg, du, G["conv_w"], G["conv_b"] = _convglu_bwd(g, u, da, W["conv_w"], W["conv_b"])
    dh2 = _mm("mm_gate_dx", dg, W["w_gate"], tb=True, tm=1024, tn=512, tk=D_FF)
    dh2 = _mm("mm_up_dx", du, W["w_up"], tb=True, add=dh2, tm=1024, tn=512, tk=D_FF)
    G["w_gate"] = _mm("mm_gate_dw", h2, dg, ta=True, tm=1024, tn=1408, tk=1024)
    G["w_up"] = _mm("mm_up_dw", h2, du, ta=True, tm=1024, tn=1408, tk=1024)
    dx1, dx1b, G["g_ffn"] = _rms_bwd_call("rms2_bwd", x1, W["g_ffn"], dh2, dy, True)

    dyc = _mm("mm_o_dx", dx1b, W["w_o"], tb=True, tm=512, tn=2048, tk=D_MODEL)
    G["w_o"] = _mm("mm_o_dw", ycat, dx1b, ta=True, tm=1024, tn=1024, tk=2048)
    dz_pool, G["w_pool"], G["pool_scale"] = _pool_bwd(z, dyc, W["w_pool"], W["pool_scale"], dy_cb=2)
    dz_mq, dkx, dvx, G["g_q_x"] = _memx_bwd(z, W["g_q_x"], kx, vx, dyc, dy_cb=3)
    dmkv, G["g_k_x"] = _memkv_bwd(dkx, dvx, mkv, W["g_k_x"])
    G["w_mem_kv"] = _mm("mm_mem_kv_dw", mn, dmkv, ta=True, tm=1024, tn=1024, tk=MEM_LEN)
    dmn = _mm("mm_mem_kv_dx", dmkv, W["w_mem_kv"], tb=True, tm=256, tn=2048, tk=1024)
    G["g_mem"] = _mem_gain_bwd(mem, dmn)
    do, delta = _attn_bwd_prep(dyc, o_mla)
    dq, dk, dv = _flash_bwd(q, k, v, do, lse, delta)
    dq_raw, dkv_raw, dz_kr, G["g_q_mla"], G["g_k_mla"] = _mla_prep_bwd(
        dq, dk, dv, q_raw, kv_raw, z, tabs, W["g_q_mla"], W["g_k_mla"])
    G["w_q_up"] = _mm("mm_q_up_dw", ql, dq_raw, ta=True, tm=512, tn=2048, tk=1024)
    dql = _mm("mm_q_up_dx", dq_raw, W["w_q_up"], tb=True, tm=1024, tn=512, tk=2048)
    G["w_kv_up"] = _mm("mm_kv_up_dw", kvl, dkv_raw, ta=True, tm=256, tn=2048, tk=1024)
    dkvl = _mm("mm_kv_up_dx", dkv_raw, W["w_kv_up"], tb=True, tm=1024, tn=256, tk=2048)
    dz_q, dz_kv, G["g_q_lat"], G["g_kv_lat"] = _lat_bwd(z, W["g_q_lat"], W["g_kv_lat"], dql, dkvl)
    dz = jnp.concatenate([dz_pool, dz_q, dz_mq, dz_kv, dz_kr], axis=1)
    G["w_in"] = _mm("mm_in_dw", h, dz, ta=True, tm=1024, tn=Z_COLS, tk=1024)
    dh = _mm("mm_in_dx", dz, W["w_in"], tb=True, tm=512, tn=2048, tk=Z_COLS)
    grad_x, G["g_mix"] = _rms_bwd_call("rms1_bwd", x, W["g_mix"], dh, dx1, False)
    return loss_part, grad_x, G


_ANY = pl.BlockSpec(memory_space=pl.ANY)
_MESH = pl.DeviceIdType.MESH


def _my_index():
    return 4 * lax.axis_index("x") + 2 * lax.axis_index("y") + lax.axis_index("c")


def _all_gather(p):
    R = p.shape[0]

    def body(x_ref, out_ref, send_sems, recv_sems, local_sem):
        x, y, c = lax.axis_index("x"), lax.axis_index("y"), lax.axis_index("c")
        me, sibling = (x, y, c), (x, y, 1 - c)
        chips = [(1 - x, y), (x, 1 - y), (1 - x, 1 - y)]

        def rows(px, py, pc):
            return out_ref.at[4 * px + 2 * py + pc]

        def copy(k, block, to, src=None):
            return pltpu.make_async_remote_copy(
                src_ref=rows(*block) if src is None else src, dst_ref=rows(*block),
                send_sem=send_sems.at[k], recv_sem=recv_sems.at[k], device_id=to, device_id_type=_MESH)

        mine = pltpu.make_async_copy(x_ref, rows(*me), local_sem)
        mine.start()
        first = [copy(0, me, sibling, src=x_ref)]
        first += [copy(1 + j, me, (*chip, c), src=x_ref) for j, chip in enumerate(chips)]
        for cp in first:
            cp.start()
        passed = [copy(4 + j, (*chip, c), sibling) for j, chip in enumerate(chips)]
        for j, chip in enumerate(chips):
            copy(1 + j, (*chip, c), me).wait_recv()
            passed[j].start()
        copy(0, sibling, me).wait_recv()
        for j, chip in enumerate(chips):
            copy(4 + j, (*chip, 1 - c), me).wait_recv()
        for cp in first + passed:
            cp.wait_send()
        mine.wait()

    return pl.pallas_call(
        body, name="all_gather", out_shape=jax.ShapeDtypeStruct((N_DEV, R, LANES), p.dtype),
        in_specs=[_ANY], out_specs=_ANY,
        scratch_shapes=[pltpu.SemaphoreType.DMA((7,)), pltpu.SemaphoreType.DMA((7,)), pltpu.SemaphoreType.DMA],
    )(p)


def _exchange_all(s):
    R = s.shape[0]

    def body(x_ref, out_ref, send_sems, recv_sems, local_sem):
        x, y, c = lax.axis_index("x"), lax.axis_index("y"), lax.axis_index("c")
        me = 4 * x + 2 * y + c
        mine = pltpu.make_async_copy(x_ref, out_ref.at[me], local_sem)
        mine.start()

        def copy(k):
            px, py, pc = x ^ ((k >> 2) & 1), y ^ ((k >> 1) & 1), c ^ (k & 1)
            return pltpu.make_async_remote_copy(
                src_ref=x_ref, dst_ref=out_ref.at[me], send_sem=send_sems.at[k - 1], recv_sem=recv_sems.at[k - 1],
                device_id=(px, py, pc), device_id_type=_MESH)

        cps = [copy(k) for k in range(1, N_DEV)]
        for cp in cps:
            cp.start()
        for cp in cps:
            cp.wait_recv()
        for cp in cps:
            cp.wait_send()
        mine.wait()

    return pl.pallas_call(
        body, name="exchange_all", out_shape=jax.ShapeDtypeStruct((N_DEV, R, LANES), s.dtype),
        in_specs=[_ANY], out_specs=_ANY,
        scratch_shapes=[pltpu.SemaphoreType.DMA((7,)), pltpu.SemaphoreType.DMA((7,)), pltpu.SemaphoreType.DMA],
    )(s)


def _sum_slots(buf, tr=208):
    n, R, _ = buf.shape
    tr = tr if R % tr == 0 else R

    def body(b_ref, o_ref):
        acc = b_ref[0]
        for d in range(1, n):
            acc = acc + b_ref[d]
        o_ref[...] = acc

    return pl.pallas_call(
        body, name="sum_slots", grid=(R // tr,),
        in_specs=[pl.BlockSpec((n, tr, LANES), lambda i: (0, i, 0))],
        out_specs=pl.BlockSpec((tr, LANES), lambda i: (i, 0)),
        out_shape=jax.ShapeDtypeStruct((R, LANES), buf.dtype),
        compiler_params=_params("parallel"),
    )(buf)


def _rs_to_sibling(gbuf):
    _, R, _ = gbuf.shape

    def body(g_ref, out_ref, send_sem, recv_sem):
        x, y, c = lax.axis_index("x"), lax.axis_index("y"), lax.axis_index("c")
        cp = pltpu.make_async_remote_copy(
            src_ref=g_ref.at[pl.ds(4 * (1 - c), 4)], dst_ref=out_ref, send_sem=send_sem, recv_sem=recv_sem,
            device_id=(x, y, 1 - c), device_id_type=_MESH)
        cp.start()
        cp.wait()

    return pl.pallas_call(
        body, name="rs_to_sibling", out_shape=jax.ShapeDtypeStruct((4, R, LANES), gbuf.dtype),
        in_specs=[_ANY], out_specs=_ANY,
        scratch_shapes=[pltpu.SemaphoreType.DMA, pltpu.SemaphoreType.DMA],
    )(gbuf)


def _rs_add_chip(gbuf, got, c_idx, tr=1024):
    _, R, _ = gbuf.shape

    def body(c_ref, g_ref, r_ref, f_ref, b_ref):
        s = g_ref[...] + r_ref[...]
        f_ref[...] = s
        b_ref[...] = s.astype(BF16)

    blk = lambda f: pl.BlockSpec((None, tr, LANES), f)
    return pl.pallas_call(
        body, name="rs_add_chip",
        grid_spec=pltpu.PrefetchScalarGridSpec(
            num_scalar_prefetch=1, grid=(4, R // tr),
            in_specs=[blk(lambda b, r, c: (4 * c[0] + b, r, 0)), blk(lambda b, r, c: (b, r, 0))],
            out_specs=[blk(lambda b, r, c: (b, r, 0)), blk(lambda b, r, c: (b, r, 0))]),
        out_shape=[jax.ShapeDtypeStruct((4, R, LANES), F32), jax.ShapeDtypeStruct((4, R, LANES), BF16)],
        compiler_params=_params("parallel", "parallel"),
    )(c_idx, gbuf, got)


def _rs_to_chips(sb):
    _, R, _ = sb.shape

    def body(s_ref, out_ref, send_sems, recv_sems):
        x, y, c = lax.axis_index("x"), lax.axis_index("y"), lax.axis_index("c")
        chips = [(1 - x, y), (x, 1 - y), (1 - x, 1 - y)]
        cps = [pltpu.make_async_remote_copy(
            src_ref=s_ref.at[2 * cx + cy], dst_ref=out_ref.at[k], send_sem=send_sems.at[k],
            recv_sem=recv_sems.at[k], device_id=(cx, cy, c), device_id_type=_MESH)
            for k, (cx, cy) in enumerate(chips)]
        for cp in cps:
            cp.start()
        for cp in cps:
            cp.wait_recv()
        for cp in cps:
            cp.wait_send()

    return pl.pallas_call(
        body, name="rs_to_chips", out_shape=jax.ShapeDtypeStruct((3, R, LANES), sb.dtype),
        in_specs=[_ANY], out_specs=_ANY,
        scratch_shapes=[pltpu.SemaphoreType.DMA((3,)), pltpu.SemaphoreType.DMA((3,))],
    )(sb)


def _rs_add_final(sf, got, own_idx, tr=1024):
    _, R, _ = sf.shape

    def body(o_ref, s_ref, r_ref, out_ref):
        acc = s_ref[...]
        for k in range(3):
            acc = acc + r_ref[k].astype(F32)
        out_ref[...] = acc

    return pl.pallas_call(
        body, name="rs_add_final",
        grid_spec=pltpu.PrefetchScalarGridSpec(
            num_scalar_prefetch=1, grid=(R // tr,),
            in_specs=[pl.BlockSpec((None, tr, LANES), lambda r, o: (o[0], r, 0)),
                      pl.BlockSpec((3, tr, LANES), lambda r, o: (0, r, 0))],
            out_specs=pl.BlockSpec((tr, LANES), lambda r, o: (r, 0))),
        out_shape=jax.ShapeDtypeStruct((R, LANES), F32),
        compiler_params=_params("parallel"),
    )(own_idx, sf, got)


def _adamw(name, w, g, m, v, tr=256):
    R, C = w.shape
    tr = tr if R % tr == 0 else R

    def body(w_ref, g_ref, m_ref, v_ref, d_ref, mo_ref, vo_ref):
        gv = g_ref[...]
        m_new = ADAM_B1 * m_ref[...] + (1.0 - ADAM_B1) * gv
        v_new = ADAM_B2 * v_ref[...] + (1.0 - ADAM_B2) * (gv * gv)
        m_hat = m_new / (1.0 - ADAM_B1 ** ADAM_STEP)
        v_hat = v_new / (1.0 - ADAM_B2 ** ADAM_STEP)
        d_ref[...] = -ADAM_LR * (m_hat / (jnp.sqrt(v_hat) + ADAM_EPS) + ADAM_WD * w_ref[...])
        mo_ref[...] = m_new
        vo_ref[...] = v_new

    spec = pl.BlockSpec((tr, C), lambda i: (i, 0))
    sds = jax.ShapeDtypeStruct((R, C), F32)
    return pl.pallas_call(
        body, name=name, grid=(R // tr,), in_specs=[spec] * 4, out_specs=[spec] * 3, out_shape=[sds] * 3,
        compiler_params=_params("parallel"),
    )(w, g, m, v)


def _pad_rows(a, mult):
    r = (-a.shape[0]) % mult
    return a if r == 0 else jnp.concatenate([a, jnp.zeros((r,) + a.shape[1:], a.dtype)], axis=0)


def _as_rows(a, mult):
    flat = a.reshape(-1)
    r = (-flat.shape[0]) % LANES
    if r:
        flat = jnp.concatenate([flat, jnp.zeros((r,), a.dtype)])
    return _pad_rows(flat.reshape(-1, LANES), mult)


def _w_in_to_kernel_cols(w):
    pad = jnp.zeros(w.shape[:-1] + (Z_COLS - IN_COLS,), w.dtype)
    return jnp.concatenate([w[..., :1024], w[..., 1344:1856], w[..., 1024:1344], pad], axis=-1)


def _w_in_from_kernel_cols(w):
    return jnp.concatenate([w[..., :1024], w[..., 1536:1856], w[..., 1024:1536]], axis=-1)


def kernel(x, mem, g_mix, w_in, g_q_lat, w_q_up, g_kv_lat, w_kv_up, g_q_mla, g_k_mla, w_pool, pool_scale, g_mem, w_mem_kv, g_q_x, g_k_x, w_o, g_ffn, w_gate, w_up, conv_w, conv_b, w_down, loss_target, m_g_mix, m_w_in, m_g_q_lat, m_w_q_up, m_g_kv_lat, m_w_kv_up, m_g_q_mla, m_g_k_mla, m_w_pool, m_pool_scale, m_g_mem, m_w_mem_kv, m_g_q_x, m_g_k_x, m_w_o, m_g_ffn, m_w_gate, m_w_up, m_conv_w, m_conv_b, m_w_down, v_g_mix, v_w_in, v_g_q_lat, v_w_q_up, v_g_kv_lat, v_w_kv_up, v_g_q_mla, v_g_k_mla, v_w_pool, v_pool_scale, v_g_mem, v_w_mem_kv, v_g_q_x, v_g_k_x, v_w_o, v_g_ffn, v_w_gate, v_w_up, v_conv_w, v_conv_b, v_w_down):
    given = dict(g_mix=g_mix, w_in=w_in, g_q_lat=g_q_lat, w_q_up=w_q_up, g_kv_lat=g_kv_lat, w_kv_up=w_kv_up,
                 g_q_mla=g_q_mla, g_k_mla=g_k_mla, w_pool=w_pool, pool_scale=pool_scale, g_mem=g_mem,
                 w_mem_kv=w_mem_kv, g_q_x=g_q_x, g_k_x=g_k_x, w_o=w_o, g_ffn=g_ffn, w_gate=w_gate, w_up=w_up,
                 conv_w=conv_w, conv_b=conv_b, w_down=w_down)
    mom_m = dict(g_mix=m_g_mix, w_in=m_w_in, g_q_lat=m_g_q_lat, w_q_up=m_w_q_up, g_kv_lat=m_g_kv_lat,
                 w_kv_up=m_w_kv_up, g_q_mla=m_g_q_mla, g_k_mla=m_g_k_mla, w_pool=m_w_pool,
                 pool_scale=m_pool_scale, g_mem=m_g_mem, w_mem_kv=m_w_mem_kv, g_q_x=m_g_q_x, g_k_x=m_g_k_x,
                 w_o=m_w_o, g_ffn=m_g_ffn, w_gate=m_w_gate, w_up=m_w_up, conv_w=m_conv_w, conv_b=m_conv_b,
                 w_down=m_w_down)
    mom_v = dict(g_mix=v_g_mix, w_in=v_w_in, g_q_lat=v_g_q_lat, w_q_up=v_w_q_up, g_kv_lat=v_g_kv_lat,
                 w_kv_up=v_w_kv_up, g_q_mla=v_g_q_mla, g_k_mla=v_g_k_mla, w_pool=v_w_pool,
                 pool_scale=v_pool_scale, g_mem=v_g_mem, w_mem_kv=v_w_mem_kv, g_q_x=v_g_q_x, g_k_x=v_g_k_x,
                 w_o=v_w_o, g_ffn=v_g_ffn, w_gate=v_w_gate, w_up=v_w_up, conv_w=v_conv_w, conv_b=v_conv_b,
                 w_down=v_w_down)
    drop = lambda a: a[0] if a.ndim > 2 else a
    sh = {n: drop(given[n]) for n in WEIGHTS}
    mom_m = {n: drop(mom_m[n]) for n in WEIGHTS}
    mom_v = {n: drop(mom_v[n]) for n in WEIGHTS}
    ffs = D_FF // N_DEV

    cw_hi = sh["conv_w"].astype(BF16)
    cw_r = sh["conv_w"] - cw_hi.astype(F32)
    cw_mid = cw_r.astype(BF16)
    cw_lo = (cw_r - cw_mid.astype(F32)).astype(BF16)
    parts = [
        _w_in_to_kernel_cols(sh["w_in"]).astype(BF16),
        jnp.pad(sh["w_q_up"], ((0, 0), (0, MLA_QK_PAD - MLA_QK))).astype(BF16),
        sh["w_kv_up"].astype(BF16), sh["w_mem_kv"].astype(BF16), sh["w_o"].astype(BF16),
        sh["w_gate"].astype(BF16), sh["w_up"].astype(BF16), sh["w_down"].astype(BF16),
        jnp.stack([cw_hi, cw_mid, cw_lo]),
    ]
    rows = [_as_rows(p, 16) for p in parts]
    offs = [0]
    for r in rows:
        offs.append(offs[-1] + r.shape[0])
    gathered = _all_gather(jnp.concatenate(rows, axis=0))

    def take(i, shape):
        n = math.prod(shape)
        blk = gathered[:, offs[i]:offs[i] + (n + LANES - 1) // LANES, :]
        return blk.reshape(N_DEV, -1)[:, :n].reshape((N_DEV,) + shape)

    def by_rows(i, shape):
        return take(i, shape).reshape((N_DEV * shape[0],) + shape[1:])

    def by_cols(i, shape):
        return jnp.transpose(take(i, shape), (1, 0, 2)).reshape(shape[0], N_DEV * shape[1])

    w_o_blocks = take(4, (256, D_MODEL))
    W = {
        "w_in": by_rows(0, (256, Z_COLS)),
        "w_q_up": by_cols(1, (Q_LORA, MLA_QK_PAD)),
        "w_kv_up": by_cols(2, (KV_LORA, 256)),
        "w_mem_kv": by_rows(3, (256, 2 * X_WIDTH)),
        "w_o": jnp.concatenate([w_o_blocks[b] for b in WO_BLOCK_ORDER], axis=0),
        "w_gate": by_cols(5, (D_MODEL, ffs)),
        "w_up": by_cols(6, (D_MODEL, ffs)),
        "w_down": by_rows(7, (ffs, D_MODEL)),
    }
    cw = jnp.sum(take(8, (3, 3, ffs)).astype(F32), axis=1)
    W["conv_w"] = jnp.transpose(cw, (1, 0, 2)).reshape(3, D_FF)
    W["w_pool"] = sh["w_pool"].astype(BF16)
    for n in ("g_mix", "g_q_lat", "g_kv_lat", "pool_scale", "g_mem", "g_q_x", "g_k_x", "g_ffn", "conv_b"):
        W[n] = sh[n]
    pad_qk = lambda gv: jnp.pad(gv, ((0, 0), (0, MLA_QK_PAD - MLA_QK)))
    W["g_q_mla"], W["g_k_mla"] = pad_qk(sh["g_q_mla"]), pad_qk(sh["g_k_mla"])

    loss_part, grad_x, G = _local_step(x[0], mem[0], loss_target[0], W)

    def dev_blocks(n):
        g = G[n]
        if n == "w_in":
            return _w_in_from_kernel_cols(g).reshape(N_DEV, 256, IN_COLS)
        if n == "w_q_up":
            return jnp.transpose(g.reshape(Q_LORA, N_DEV, MLA_QK_PAD)[:, :, :MLA_QK], (1, 0, 2))
        if n == "w_kv_up":
            return jnp.transpose(g.reshape(KV_LORA, N_DEV, 256), (1, 0, 2))
        if n == "w_mem_kv":
            return g.reshape(N_DEV, 256, 2 * X_WIDTH)
        if n == "w_o":
            blocks = g.reshape(N_DEV, 256, D_MODEL)
            inv = [WO_BLOCK_ORDER.index(b) for b in range(N_DEV)]
            return jnp.stack([blocks[inv[b]] for b in range(N_DEV)], axis=0)
        if n in ("w_gate", "w_up"):
            return jnp.transpose(g.reshape(D_MODEL, N_DEV, ffs), (1, 0, 2))
        if n == "conv_w":
            return jnp.transpose(g.reshape(3, N_DEV, ffs), (1, 0, 2))
        assert n == "w_down"
        return g.reshape(N_DEV, ffs, D_MODEL)

    per_dev = {n: dev_blocks(n) for n in BIG}
    g_offs = {}
    off = 0
    for n in BIG:
        g_offs[n] = off
        off += -(-math.prod(per_dev[n].shape[1:]) // LANES)
    rs_rows = -(-off // 1024) * 1024

    def dev_rows(d):
        segs = [_as_rows(per_dev[n][d], 1) for n in BIG]
        segs.append(jnp.zeros((rs_rows - off, LANES), F32))
        return jnp.concatenate(segs, axis=0)

    order = [4 * xx + 2 * yy + cc for cc in range(2) for xx in range(2) for yy in range(2)]
    gbuf = jnp.stack([dev_rows(d) for d in order], axis=0)
    ax, ay, ac = lax.axis_index("x"), lax.axis_index("y"), lax.axis_index("c")
    from_sib = _rs_to_sibling(gbuf)
    chip_f32, chip_bf16 = _rs_add_chip(gbuf, from_sib, jnp.reshape(ac, (1,)).astype(jnp.int32))
    from_chips = _rs_to_chips(chip_bf16)
    g_shard = _rs_add_final(chip_f32, from_chips, jnp.reshape(2 * ax + ay, (1,)).astype(jnp.int32))

    def shard_grad(n):
        shape = sh[n].shape
        cnt = math.prod(shape)
        blk = g_shard[g_offs[n]:g_offs[n] + -(-cnt // LANES)]
        return blk.reshape(-1)[:cnt].reshape(shape)

    small = {
        "g_mix": G["g_mix"], "g_q_lat": G["g_q_lat"], "g_kv_lat": G["g_kv_lat"],
        "g_q_mla": G["g_q_mla"][:, :MLA_QK], "g_k_mla": G["g_k_mla"][:, :MLA_QK],
        "w_pool": G["w_pool"], "pool_scale": G["pool_scale"], "g_mem": G["g_mem"],
        "g_q_x": G["g_q_x"], "g_k_x": G["g_k_x"], "g_ffn": G["g_ffn"], "conv_b": G["conv_b"],
    }
    s_offs = {}
    segs = []
    off = 0
    for n in SMALL:
        r = _as_rows(small[n], 1)
        s_offs[n] = off
        off += r.shape[0]
        segs.append(r)
    segs.append(loss_part[:1])
    loss_row = off
    sbuf = _pad_rows(jnp.concatenate(segs, axis=0), 8)
    s_sum = _sum_slots(_exchange_all(sbuf))

    def small_take(buf, n):
        shape = sh[n].shape
        cnt = math.prod(shape)
        return buf[s_offs[n]:s_offs[n] + -(-cnt // LANES)].reshape(-1)[:cnt].reshape(shape)

    loss = s_sum[loss_row, 0]

    grads, deltas, new_m, new_v = {}, {}, {}, {}
    for n in BIG:
        shape = sh[n].shape
        g = shard_grad(n)
        flat = lambda a: a.reshape(-1, shape[-1])
        d, m2, v2 = _adamw("adamw_" + n, flat(sh[n]), flat(g), flat(mom_m[n]), flat(mom_v[n]))
        grads[n], deltas[n], new_m[n], new_v[n] = g, d.reshape(shape), m2.reshape(shape), v2.reshape(shape)

    def pack_small(src):
        rows_ = [_as_rows(src[n], 1) for n in SMALL]
        return _pad_rows(jnp.concatenate(rows_, axis=0), 8)

    n_small = pack_small(sh).shape[0]
    d_s, m_s, v_s = _adamw("adamw_small", pack_small(sh), s_sum[:n_small],
                           pack_small(mom_m), pack_small(mom_v))
    for n in SMALL:
        grads[n] = small_take(s_sum, n)
        deltas[n], new_m[n], new_v[n] = small_take(d_s, n), small_take(m_s, n), small_take(v_s, n)

    lead = lambda n, a: a.reshape(given[n].shape)
    return (loss, grad_x[None],
            *[lead(n, grads[n]) for n in WEIGHTS], *[lead(n, deltas[n]) for n in WEIGHTS],
            *[lead(n, new_m[n]) for n in WEIGHTS], *[lead(n, new_v[n]) for n in WEIGHTS])
```

```python
import math

import jax
import jax.numpy as jnp
from jax import lax
from jax.experimental import pallas as pl
from jax.experimental.pallas import tpu as pltpu

F32 = jnp.float32
BF16 = jnp.bfloat16

D_MODEL = 2048
D_FF = 5632
POOL_WIDTH = 512
POOL_WINDOWS = (2, 4, 8, 16)
POOL_HALO = 16
MLA_HEADS = 8
MLA_NOPE = 128
MLA_ROPE = 64
MLA_QK = MLA_NOPE + MLA_ROPE
MLA_QK_PAD = 256
MLA_V = 128
Q_LORA = 512
KV_LORA = 256
X_HEADS = 4
X_HEAD_DIM = 128
X_WIDTH = 512
MEM_LEN = 256
ROPE_THETA = 10000.0
NORM_EPS = 1e-6
CONV_HALO = 8
IN_COLS = 1856
Z_COLS = 1920
Z_POOL, Z_Q, Z_MQ, Z_KV, Z_KR = 0, 512, 1024, 1536, 1792

ADAM_LR = 0.001
ADAM_B1 = 0.9
ADAM_B2 = 0.999
ADAM_EPS = 1e-08
ADAM_WD = 0.01
ADAM_STEP = 10

N_DEV = 8
LANES = 128
VMEM_LIMIT = 56 * 1024 * 1024

BIG = ("w_in", "w_q_up", "w_kv_up", "w_mem_kv", "w_o", "w_gate", "w_up", "conv_w", "w_down")
SCATTER_EARLY = ("w_down", "w_gate", "w_up", "conv_w", "w_o", "w_mem_kv")
SCATTER_LATE = ("w_in", "w_q_up", "w_kv_up")
FFN_GATHER_KINDS = ("lead", "lead", "rows", "lead")
SMALL = ("g_mix", "g_q_lat", "g_kv_lat", "g_q_mla", "g_k_mla", "w_pool", "pool_scale", "g_mem",
         "g_q_x", "g_k_x", "g_ffn", "conv_b")
WEIGHTS = ("g_mix", "w_in", "g_q_lat", "w_q_up", "g_kv_lat", "w_kv_up", "g_q_mla", "g_k_mla", "w_pool",
           "pool_scale", "g_mem", "w_mem_kv", "g_q_x", "g_k_x", "w_o", "g_ffn", "w_gate", "w_up", "conv_w",
           "conv_b", "w_down")


def _tile(n, t):
    if n <= t:
        return n
    for c in range(t - t % LANES, 0, -LANES):
        if n % c == 0:
            return c
    return n


def _params(*sem):
    return pltpu.CompilerParams(dimension_semantics=sem, vmem_limit_bytes=VMEM_LIMIT)


def _full(shape):
    nd = len(shape)
    return pl.BlockSpec(shape, lambda *_: (0,) * nd)


def _rows(tr, w, cb=0):
    return pl.BlockSpec((tr, w), lambda i: (i, cb))


def _rms_hat(x, n):
    r = lax.rsqrt(jnp.sum(x * x, axis=-1, keepdims=True) / n + NORM_EPS)
    return x * r, r


def _rms_bwd(dy, xhat, r, g, n):
    dxh = dy * g
    dx = r * (dxh - xhat * (jnp.sum(dxh * xhat, axis=-1, keepdims=True) / n))
    return dx, jnp.sum(dy * xhat, axis=0, keepdims=True)


def _mm(name, a, b, *, ta=False, tb=False, add=None, out_dtype=F32, tm=512, tn=512, tk=512):
    if ta:
        K, M = a.shape
    else:
        M, K = a.shape
    if tb:
        N, K2 = b.shape
    else:
        K2, N = b.shape
    assert K == K2, (name, a.shape, b.shape)
    tm, tn, tk = _tile(M, tm), _tile(N, tn), _tile(K, tk)
    nk = K // tk
    a_spec = pl.BlockSpec((tk, tm), lambda i, j, k: (k, i)) if ta else pl.BlockSpec((tm, tk), lambda i, j, k: (i, k))
    b_spec = pl.BlockSpec((tn, tk), lambda i, j, k: (j, k)) if tb else pl.BlockSpec((tk, tn), lambda i, j, k: (k, j))
    o_spec = pl.BlockSpec((tm, tn), lambda i, j, k: (i, j))
    dims = (((0,) if ta else (1,), (1,) if tb else (0,)), ((), ()))
    has_add = add is not None

    def body(*refs):
        if has_add:
            a_ref, b_ref, add_ref, o_ref = refs[:4]
            scr = refs[4:]
        else:
            a_ref, b_ref, o_ref = refs[:3]
            add_ref = None
            scr = refs[3:]
        part = lax.dot_general(a_ref[...], b_ref[...], dims, preferred_element_type=F32)
        if nk == 1:
            if has_add:
                part = part + add_ref[...]
            o_ref[...] = part.astype(o_ref.dtype)
            return
        acc = scr[0]
        k = pl.program_id(2)

        @pl.when(k == 0)
        def _():
            acc[...] = part

        @pl.when(k > 0)
        def _():
            acc[...] += part

        @pl.when(k == nk - 1)
        def _():
            r = acc[...]
            if has_add:
                r = r + add_ref[...]
            o_ref[...] = r.astype(o_ref.dtype)

    ins = [a, b] + ([add] if has_add else [])
    in_specs = [a_spec, b_spec] + ([o_spec] if has_add else [])
    return pl.pallas_call(
        body, name=name, grid=(M // tm, N // tn, nk),
        in_specs=in_specs, out_specs=o_spec,
        out_shape=jax.ShapeDtypeStruct((M, N), out_dtype),
        scratch_shapes=[pltpu.VMEM((tm, tn), F32)] if nk > 1 else [],
        compiler_params=_params("parallel", "parallel", "arbitrary"),
    )(*ins)


def _rms_fwd(name, x, g, tr=512):
    S, n = x.shape
    tr = _tile(S, tr)

    def body(x_ref, g_ref, h_ref):
        xhat, _ = _rms_hat(x_ref[...], n)
        h_ref[...] = (xhat * g_ref[...]).astype(BF16)

    return pl.pallas_call(
        body, name=name, grid=(S // tr,),
        in_specs=[_rows(tr, n), _full((1, n))], out_specs=_rows(tr, n),
        out_shape=jax.ShapeDtypeStruct((S, n), BF16),
        compiler_params=_params("parallel"),
    )(x, g)


def _rms_bwd_call(name, x, g, dy, extra, want_bf16, tr=256):
    S, n = x.shape
    tr = _tile(S, tr)

    def body(x_ref, g_ref, dy_ref, ex_ref, *outs):
        xhat, r = _rms_hat(x_ref[...], n)
        dx, dg = _rms_bwd(dy_ref[...], xhat, r, g_ref[...], n)
        dx = dx + ex_ref[...]
        outs[0][...] = dx
        if want_bf16:
            outs[1][...] = dx.astype(BF16)
        dg_ref = outs[-1]

        @pl.when(pl.program_id(0) == 0)
        def _():
            dg_ref[...] = jnp.zeros_like(dg_ref)

        dg_ref[...] += dg

    out_shape = [jax.ShapeDtypeStruct((S, n), F32)]
    out_specs = [_rows(tr, n)]
    if want_bf16:
        out_shape.append(jax.ShapeDtypeStruct((S, n), BF16))
        out_specs.append(_rows(tr, n))
    out_shape.append(jax.ShapeDtypeStruct((1, n), F32))
    out_specs.append(_full((1, n)))
    return pl.pallas_call(
        body, name=name, grid=(S // tr,),
        in_specs=[_rows(tr, n), _full((1, n)), _rows(tr, n), _rows(tr, n)],
        out_specs=out_specs, out_shape=out_shape,
        compiler_params=_params("arbitrary"),
    )(x, g, dy, extra)


def _lat_fwd(z, gq, gkv, tr=512):
    S = z.shape[0]
    tr = _tile(S, tr)

    def body(zq_ref, zkv_ref, gq_ref, gkv_ref, ql_ref, kvl_ref):
        xq, _ = _rms_hat(zq_ref[...], Q_LORA)
        ql_ref[...] = (xq * gq_ref[...]).astype(BF16)
        xkv, _ = _rms_hat(zkv_ref[...], KV_LORA)
        kvl_ref[...] = (xkv * gkv_ref[...]).astype(BF16)

    return pl.pallas_call(
        body, name="lat_fwd", grid=(S // tr,),
        in_specs=[_rows(tr, Q_LORA, Z_Q // Q_LORA), _rows(tr, KV_LORA, Z_KV // KV_LORA),
                  _full((1, Q_LORA)), _full((1, KV_LORA))],
        out_specs=[_rows(tr, Q_LORA), _rows(tr, KV_LORA)],
        out_shape=[jax.ShapeDtypeStruct((S, Q_LORA), BF16), jax.ShapeDtypeStruct((S, KV_LORA), BF16)],
        compiler_params=_params("parallel"),
    )(z, z, gq, gkv)


def _lat_bwd(z, gq, gkv, dql, dkvl, tr=512):
    S = z.shape[0]
    tr = _tile(S, tr)

    def body(zq_ref, zkv_ref, gq_ref, gkv_ref, dql_ref, dkvl_ref, dzq_ref, dzkv_ref, dgq_ref, dgkv_ref):
        @pl.when(pl.program_id(0) == 0)
        def _():
            dgq_ref[...] = jnp.zeros_like(dgq_ref)
            dgkv_ref[...] = jnp.zeros_like(dgkv_ref)

        xq, rq = _rms_hat(zq_ref[...], Q_LORA)
        dx, dg = _rms_bwd(dql_ref[...], xq, rq, gq_ref[...], Q_LORA)
        dzq_ref[...] = dx.astype(BF16)
        dgq_ref[...] += dg
        xkv, rkv = _rms_hat(zkv_ref[...], KV_LORA)
        dx, dg = _rms_bwd(dkvl_ref[...], xkv, rkv, gkv_ref[...], KV_LORA)
        dzkv_ref[...] = dx.astype(BF16)
        dgkv_ref[...] += dg

    return pl.pallas_call(
        body, name="lat_bwd", grid=(S // tr,),
        in_specs=[_rows(tr, Q_LORA, Z_Q // Q_LORA), _rows(tr, KV_LORA, Z_KV // KV_LORA),
                  _full((1, Q_LORA)), _full((1, KV_LORA)), _rows(tr, Q_LORA), _rows(tr, KV_LORA)],
        out_specs=[_rows(tr, Q_LORA), _rows(tr, KV_LORA), _full((1, Q_LORA)), _full((1, KV_LORA))],
        out_shape=[jax.ShapeDtypeStruct((S, Q_LORA), BF16), jax.ShapeDtypeStruct((S, KV_LORA), BF16),
                   jax.ShapeDtypeStruct((1, Q_LORA), F32), jax.ShapeDtypeStruct((1, KV_LORA), F32)],
        compiler_params=_params("arbitrary"),
    )(z, z, gq, gkv, dql, dkvl)


def _pool_d(ext, cur, t0, tr):
    t = t0 + lax.broadcasted_iota(jnp.int32, (tr, 1), 0)
    ds = []
    for gi, w in enumerate(POOL_WINDOWS):
        s = ext[:, gi * LANES:(gi + 1) * LANES]
        sh = 1
        while sh < w:
            s = s + pltpu.roll(s, sh, 0)
            sh *= 2
        cnt = jnp.minimum(t + 1, w).astype(F32)
        ds.append(s[POOL_HALO:] / cnt - cur[:, gi * LANES:(gi + 1) * LANES])
    return ds


def _pool_fwd(z, wp, scale, tr=512):
    S = z.shape[0]
    tr = _tile(S, tr)
    rb = tr // POOL_HALO

    def body(z_ref, zp_ref, wp_ref, sc_ref, y_ref):
        i = pl.program_id(0)
        cur = z_ref[...]
        halo = jnp.where(i == 0, 0.0, zp_ref[...])
        ds = _pool_d(jnp.concatenate([halo, cur], axis=0), cur, i * tr, tr)
        for gi in range(len(POOL_WINDOWS)):
            y = jnp.dot(ds[gi].astype(BF16), wp_ref[gi], preferred_element_type=F32)
            y_ref[:, gi * LANES:(gi + 1) * LANES] = (y * sc_ref[:, gi * LANES:(gi + 1) * LANES]).astype(BF16)

    return pl.pallas_call(
        body, name="pool_fwd", grid=(S // tr,),
        in_specs=[_rows(tr, POOL_WIDTH, Z_POOL // POOL_WIDTH),
                  pl.BlockSpec((POOL_HALO, POOL_WIDTH), lambda i: (jnp.maximum(i * rb - 1, 0), 0)),
                  _full(wp.shape), _full((1, POOL_WIDTH))],
        out_specs=_rows(tr, POOL_WIDTH),
        out_shape=jax.ShapeDtypeStruct((S, POOL_WIDTH), BF16),
        compiler_params=_params("parallel"),
    )(z, z, wp, scale)


def _pool_bwd(z, dyc, wp, scale, dy_cb, tr=512):
    S = z.shape[0]
    tr = _tile(S, tr)
    rb = tr // POOL_HALO
    nhb = S // POOL_HALO
    ng = len(POOL_WINDOWS)

    def body(z_ref, zp_ref, dy_ref, dyn_ref, wp_ref, sc_ref, dz_ref, dwp_ref, dsc_ref):
        i = pl.program_id(0)

        @pl.when(i == 0)
        def _():
            dwp_ref[...] = jnp.zeros_like(dwp_ref)
            dsc_ref[...] = jnp.zeros_like(dsc_ref)

        cur = z_ref[...]
        halo = jnp.where(i == 0, 0.0, zp_ref[...])
        ds = _pool_d(jnp.concatenate([halo, cur], axis=0), cur, i * tr, tr)
        dy_cur = dy_ref[...]
        dy_next = jnp.where(i == pl.num_programs(0) - 1, 0.0, dyn_ref[...])
        dy_ext = jnp.concatenate([dy_cur, dy_next], axis=0)
        n_ext = tr + POOL_HALO
        t_ext = i * tr + lax.broadcasted_iota(jnp.int32, (n_ext, 1), 0)
        for gi, w in enumerate(POOL_WINDOWS):
            cols = slice(gi * LANES, (gi + 1) * LANES)
            d_b = ds[gi].astype(BF16)
            y_pre = jnp.dot(d_b, wp_ref[gi], preferred_element_type=F32)
            dsc_ref[:, cols] += jnp.sum(dy_cur[:, cols] * y_pre, axis=0, keepdims=True)
            dys = (dy_ext[:, cols] * sc_ref[:, cols]).astype(BF16)
            dwp_ref[gi] += lax.dot_general(d_b, dys[:tr], (((0,), (0,)), ((), ())), preferred_element_type=F32)
            dd = lax.dot_general(dys, wp_ref[gi], (((1,), (1,)), ((), ())), preferred_element_type=F32)
            s = dd / jnp.minimum(t_ext + 1, w).astype(F32)
            sh = 1
            while sh < w:
                s = s + pltpu.roll(s, n_ext - sh, 0)
                sh *= 2
            dz_ref[:, cols] = (s[:tr] - dd[:tr]).astype(BF16)

    return pl.pallas_call(
        body, name="pool_bwd", grid=(S // tr,),
        in_specs=[_rows(tr, POOL_WIDTH, Z_POOL // POOL_WIDTH),
                  pl.BlockSpec((POOL_HALO, POOL_WIDTH), lambda i: (jnp.maximum(i * rb - 1, 0), 0)),
                  _rows(tr, POOL_WIDTH, dy_cb),
                  pl.BlockSpec((POOL_HALO, POOL_WIDTH), lambda i: (jnp.minimum((i + 1) * rb, nhb - 1), dy_cb)),
                  _full(wp.shape), _full((1, POOL_WIDTH))],
        out_specs=[_rows(tr, POOL_WIDTH), _full((ng, LANES, LANES)), _full((1, POOL_WIDTH))],
        out_shape=[jax.ShapeDtypeStruct((S, POOL_WIDTH), BF16), jax.ShapeDtypeStruct((ng, LANES, LANES), F32),
                   jax.ShapeDtypeStruct((1, POOL_WIDTH), F32)],
        compiler_params=_params("arbitrary"),
    )(z, z, dyc, dyc, wp, scale)


def _rope_tables(S):
    half = MLA_ROPE // 2
    inv_freq = 1.0 / (ROPE_THETA ** (jnp.arange(half, dtype=F32) / half))
    ang = jnp.arange(S).astype(F32)[:, None] * inv_freq[None, :]
    cos, sin = jnp.cos(ang), jnp.sin(ang)
    z32 = jnp.zeros((S, half), F32)
    z64 = jnp.zeros((S, LANES - MLA_ROPE), F32)
    cos_t = jnp.concatenate([cos, cos, z64], axis=1)
    sin_a = jnp.concatenate([-sin, z32, z64], axis=1)
    sin_b = jnp.concatenate([z32, sin, z64], axis=1)
    return cos_t, sin_a, sin_b


def _rope(x, cos_t, sin_a, sin_b):
    return x * cos_t + pltpu.roll(x, LANES - 32, 1) * sin_a + pltpu.roll(x, 32, 1) * sin_b


def _rope_t(d, cos_t, sin_a, sin_b):
    return d * cos_t + pltpu.roll(d * sin_a, 32, 1) + pltpu.roll(d * sin_b, LANES - 32, 1)


def _mla_prep(q_raw, kv_raw, z, tabs, gq, gk, tr=256):
    S = z.shape[0]
    tr = _tile(S, tr)
    scale = 1.0 / math.sqrt(MLA_QK)
    W = MLA_HEADS * MLA_QK_PAD

    def body(q_ref, kv_ref, kr_ref, c_ref, sa_ref, sb_ref, gq_ref, gk_ref, qo_ref, ko_ref, vo_ref):
        tab = (c_ref[...], sa_ref[...], sb_ref[...])
        kr = kr_ref[...]
        kr_ss = jnp.sum(kr * kr, axis=-1, keepdims=True)
        gqn, gqr = gq_ref[:, :LANES], gq_ref[:, LANES:]
        gkn, gkr = gk_ref[:, :LANES], gk_ref[:, LANES:]
        for h in range(MLA_HEADS):
            c0 = h * MLA_QK_PAD
            qn = q_ref[:, c0:c0 + LANES]
            qr = q_ref[:, c0 + LANES:c0 + 2 * LANES]
            r = lax.rsqrt((jnp.sum(qn * qn, -1, keepdims=True) + jnp.sum(qr * qr, -1, keepdims=True)) / MLA_QK
                          + NORM_EPS)
            qo_ref[h, :, :LANES] = (qn * r * gqn * scale).astype(BF16)
            qo_ref[h, :, LANES:] = (_rope(qr * r * gqr, *tab) * scale).astype(BF16)
            kn = kv_ref[:, c0:c0 + LANES]
            r = lax.rsqrt((jnp.sum(kn * kn, -1, keepdims=True) + kr_ss) / MLA_QK + NORM_EPS)
            ko_ref[h, :, :LANES] = (kn * r * gkn).astype(BF16)
            ko_ref[h, :, LANES:] = _rope(kr * r * gkr, *tab).astype(BF16)
            vo_ref[h] = kv_ref[:, c0 + LANES:c0 + 2 * LANES].astype(BF16)

    hs = lambda w: pl.BlockSpec((MLA_HEADS, tr, w), lambda i: (0, i, 0))
    return pl.pallas_call(
        body, name="mla_prep", grid=(S // tr,),
        in_specs=[_rows(tr, W), _rows(tr, W), _rows(tr, LANES, Z_KR // LANES),
                  _rows(tr, LANES), _rows(tr, LANES), _rows(tr, LANES),
                  _full((1, MLA_QK_PAD)), _full((1, MLA_QK_PAD))],
        out_specs=[hs(MLA_QK_PAD), hs(MLA_QK_PAD), hs(MLA_V)],
        out_shape=[jax.ShapeDtypeStruct((MLA_HEADS, S, MLA_QK_PAD), BF16),
                   jax.ShapeDtypeStruct((MLA_HEADS, S, MLA_QK_PAD), BF16),
                   jax.ShapeDtypeStruct((MLA_HEADS, S, MLA_V), BF16)],
        compiler_params=_params("parallel"),
    )(q_raw, kv_raw, z, *tabs, gq, gk)


def _mla_prep_bwd(dq, dk, dv, q_raw, kv_raw, z, tabs, gq, gk, tr=256):
    S = z.shape[0]
    tr = _tile(S, tr)
    scale = 1.0 / math.sqrt(MLA_QK)
    W = MLA_HEADS * MLA_QK_PAD

    def body(dq_ref, dk_ref, dv_ref, q_ref, kv_ref, kr_ref, c_ref, sa_ref, sb_ref, gq_ref, gk_ref,
             dqr_ref, dkvr_ref, dkr_ref, dgq_ref, dgk_ref):
        @pl.when(pl.program_id(0) == 0)
        def _():
            dgq_ref[...] = jnp.zeros_like(dgq_ref)
            dgk_ref[...] = jnp.zeros_like(dgk_ref)

        tab = (c_ref[...], sa_ref[...], sb_ref[...])
        kr = kr_ref[...]
        kr_ss = jnp.sum(kr * kr, axis=-1, keepdims=True)
        gqn, gqr = gq_ref[:, :LANES], gq_ref[:, LANES:]
        gkn, gkr = gk_ref[:, :LANES], gk_ref[:, LANES:]
        dkr_sum = jnp.zeros((tr, LANES), F32)
        dgq_n = jnp.zeros((1, LANES), F32)
        dgq_r = jnp.zeros((1, LANES), F32)
        dgk_n = jnp.zeros((1, LANES), F32)
        dgk_r = jnp.zeros((1, LANES), F32)

        def head_norm_bwd(xn, xr, r, dyn, dyr, gn, gr):
            hn, hr = xn * r, xr * r
            dxn, dxr = dyn * gn, dyr * gr
            mt = (jnp.sum(dxn * hn, -1, keepdims=True) + jnp.sum(dxr * hr, -1, keepdims=True)) / MLA_QK
            return (r * (dxn - hn * mt), r * (dxr - hr * mt),
                    jnp.sum(dyn * hn, axis=0, keepdims=True), jnp.sum(dyr * hr, axis=0, keepdims=True))

        for h in range(MLA_HEADS):
            c0 = h * MLA_QK_PAD
            qn = q_ref[:, c0:c0 + LANES]
            qr = q_ref[:, c0 + LANES:c0 + 2 * LANES]
            r = lax.rsqrt((jnp.sum(qn * qn, -1, keepdims=True) + jnp.sum(qr * qr, -1, keepdims=True)) / MLA_QK
                          + NORM_EPS)
            dyn = dq_ref[h, :, :LANES] * scale
            dyr = _rope_t(dq_ref[h, :, LANES:] * scale, *tab)
            dxn, dxr, gn_, gr_ = head_norm_bwd(qn, qr, r, dyn, dyr, gqn, gqr)
            dgq_n += gn_
            dgq_r += gr_
            dqr_ref[:, c0:c0 + LANES] = dxn.astype(BF16)
            dqr_ref[:, c0 + LANES:c0 + 2 * LANES] = dxr.astype(BF16)

            kn = kv_ref[:, c0:c0 + LANES]
            r = lax.rsqrt((jnp.sum(kn * kn, -1, keepdims=True) + kr_ss) / MLA_QK + NORM_EPS)
            dyn = dk_ref[h, :, :LANES]
            dyr = _rope_t(dk_ref[h, :, LANES:], *tab)
            dxn, dxr, gn_, gr_ = head_norm_bwd(kn, kr, r, dyn, dyr, gkn, gkr)
            dgk_n += gn_
            dgk_r += gr_
            dkr_sum += dxr
            dkvr_ref[:, c0:c0 + LANES] = dxn.astype(BF16)
            dkvr_ref[:, c0 + LANES:c0 + 2 * LANES] = dv_ref[h].astype(BF16)

        dkr_ref[...] = dkr_sum.astype(BF16)
        dgq_ref[:, :LANES] += dgq_n
        dgq_ref[:, LANES:] += dgq_r
        dgk_ref[:, :LANES] += dgk_n
        dgk_ref[:, LANES:] += dgk_r

    hs = lambda w: pl.BlockSpec((MLA_HEADS, tr, w), lambda i: (0, i, 0))
    return pl.pallas_call(
        body, name="mla_prep_bwd", grid=(S // tr,),
        in_specs=[hs(MLA_QK_PAD), hs(MLA_QK_PAD), hs(MLA_V), _rows(tr, W), _rows(tr, W),
                  _rows(tr, LANES, Z_KR // LANES), _rows(tr, LANES), _rows(tr, LANES), _rows(tr, LANES),
                  _full((1, MLA_QK_PAD)), _full((1, MLA_QK_PAD))],
        out_specs=[_rows(tr, W), _rows(tr, W), _rows(tr, LANES), _full((1, MLA_QK_PAD)), _full((1, MLA_QK_PAD))],
        out_shape=[jax.ShapeDtypeStruct((S, W), BF16), jax.ShapeDtypeStruct((S, W), BF16),
                   jax.ShapeDtypeStruct((S, LANES), BF16),
                   jax.ShapeDtypeStruct((1, MLA_QK_PAD), F32), jax.ShapeDtypeStruct((1, MLA_QK_PAD), F32)],
        compiler_params=_params("arbitrary"),
    )(dq, dk, dv, q_raw, kv_raw, z, *tabs, gq, gk)


_NT = (((1,), (1,)), ((), ()))
_TN = (((0,), (0,)), ((), ()))


def _causal_mask(s, i, j, tq, tk):
    row = i * tq + lax.broadcasted_iota(jnp.int32, (tq, tk), 0)
    col = j * tk + lax.broadcasted_iota(jnp.int32, (tq, tk), 1)
    return jnp.where(col <= row, s, -jnp.inf)


def _flash_fwd(q, k, v, shards, kinds, tq=1024, tk=1024):
    H, S, dq = q.shape
    dv = v.shape[-1]
    tq, tk = _tile(S, tq), _tile(S, tk)
    assert tq % tk == 0 and dv == LANES
    nq, nd = S // tq, tq // tk
    nw = len(shards)

    def body(q_ref, k_ref, v_ref, *rest):
        x_refs, (o_ref, lse_ref), g_refs = rest[:nw], rest[nw:nw + 2], rest[nw + 2:2 * nw + 2]
        m_sc, l_sc, acc_sc, send_sems, recv_sems, local_sems = rest[2 * nw + 2:]
        h, i = pl.program_id(0), pl.program_id(1)
        gather = _Gather(x_refs, g_refs, kinds, send_sems, recv_sems, local_sems)
        pl.when(jnp.logical_and(h == 0, i == 0))(gather.start)
        pl.when(jnp.logical_and(h == (3 * H) // 4, i == 0))(gather.forward)

        m_sc[...] = jnp.full_like(m_sc, -jnp.inf)
        l_sc[...] = jnp.zeros_like(l_sc)
        acc_sc[...] = jnp.zeros_like(acc_sc)

        def step(j, masked):
            off = pl.multiple_of(j * tk, tk)
            s = lax.dot_general(q_ref[...], k_ref[pl.ds(off, tk), :], _NT, preferred_element_type=F32)
            if masked:
                s = _causal_mask(s, i, j, tq, tk)
            parts = [s[:, c * LANES:(c + 1) * LANES] for c in range(tk // LANES)]
            m_cur = parts[0]
            for pt in parts[1:]:
                m_cur = jnp.maximum(m_cur, pt)
            m_prev = m_sc[...]
            m_new = jnp.maximum(m_prev, jnp.max(m_cur, axis=-1, keepdims=True))
            alpha = jnp.exp(m_prev - m_new)
            ps = [jnp.exp(pt - m_new) for pt in parts]
            l_new = alpha * l_sc[...]
            for pc in ps:
                l_new = l_new + pc
            l_sc[...] = l_new
            p = jnp.concatenate(ps, axis=1).astype(BF16)
            acc_sc[...] = alpha * acc_sc[...] + jnp.dot(p, v_ref[pl.ds(off, tk), :], preferred_element_type=F32)
            m_sc[...] = m_new

        n_full = i * nd

        def full_step(j, carry):
            step(j, False)
            return carry

        lax.fori_loop(0, n_full, full_step, 0)
        for d in range(nd):
            step(n_full + d, True)
        l = jnp.sum(l_sc[...], axis=-1, keepdims=True)
        o_ref[...] = (acc_sc[...] / l).astype(BF16)
        lse_ref[...] = m_sc[...] + jnp.log(l)
        pl.when(jnp.logical_and(h == H - 1, i == nq - 1))(gather.finish)

    return pl.pallas_call(
        body, name="flash_fwd", grid=(H, nq),
        in_specs=[pl.BlockSpec((None, tq, dq), lambda h, i: (h, i, 0)),
                  pl.BlockSpec((None, S, dq), lambda h, i: (h, 0, 0)),
                  pl.BlockSpec((None, S, dv), lambda h, i: (h, 0, 0))] + [_ANY] * nw,
        out_specs=[pl.BlockSpec((tq, dv), lambda h, i: (i, h)),
                   pl.BlockSpec((None, tq, LANES), lambda h, i: (h, i, 0))] + [_ANY] * nw,
        out_shape=[jax.ShapeDtypeStruct((S, H * dv), BF16), jax.ShapeDtypeStruct((H, S, LANES), F32)]
        + [jax.ShapeDtypeStruct(_gather_out_shape(kd, sd.shape), sd.dtype) for kd, sd in zip(kinds, shards)],
        scratch_shapes=[pltpu.VMEM((tq, LANES), F32), pltpu.VMEM((tq, LANES), F32), pltpu.VMEM((tq, dv), F32),
                        pltpu.SemaphoreType.DMA((nw, 7)), pltpu.SemaphoreType.DMA((nw, 7)),
                        pltpu.SemaphoreType.DMA((nw,))],
        compiler_params=_params("arbitrary", "arbitrary"),
    )(q, k, v, *shards)


def _attn_bwd_prep(dyc, o, cb0, tr=512):
    S = o.shape[0]
    H = o.shape[1] // MLA_V
    tr = _tile(S, tr)

    def body(dy_ref, o_ref, do_ref, dl_ref):
        dy = dy_ref[...]
        do_ref[...] = dy.astype(BF16)
        dl = jnp.sum(dy * o_ref[...].astype(F32), axis=-1, keepdims=True)
        dl_ref[...] = jnp.broadcast_to(dl, (tr, LANES))

    return pl.pallas_call(
        body, name="attn_bwd_prep", grid=(S // tr, H),
        in_specs=[pl.BlockSpec((tr, MLA_V), lambda i, h: (i, h + cb0)), pl.BlockSpec((tr, MLA_V), lambda i, h: (i, h))],
        out_specs=[pl.BlockSpec((tr, MLA_V), lambda i, h: (i, h)),
                   pl.BlockSpec((None, tr, LANES), lambda i, h: (h, i, 0))],
        out_shape=[jax.ShapeDtypeStruct((S, H * MLA_V), BF16), jax.ShapeDtypeStruct((H, S, LANES), F32)],
        compiler_params=_params("parallel", "parallel"),
    )(dyc, o)


def _flash_bwd(q, k, v, do, lse, delta, sends, tq=1024, tk=512):
    H, S, dq = q.shape
    dv = v.shape[-1]
    tq, tk = _tile(S, tq), _tile(S, tk)
    assert tq % tk == 0
    nq, nd = S // tq, tq // tk
    nw = len(sends)

    def body(q_ref, k_ref, v_ref, do_ref, lse_ref, dl_ref, *rest):
        s_refs, (dq_ref, dk_ref, dv_ref), r_refs = rest[:nw], rest[nw:nw + 3], rest[nw + 3:2 * nw + 3]
        send_sems, recv_sems = rest[2 * nw + 3:]
        h, i = pl.program_id(0), pl.program_id(1)
        scatter = _Scatter(s_refs, r_refs, send_sems, recv_sems)
        pl.when(jnp.logical_and(h == 0, i == 0))(scatter.start)

        @pl.when(i == 0)
        def _():
            dk_ref[...] = jnp.zeros_like(dk_ref)
            dv_ref[...] = jnp.zeros_like(dv_ref)

        dq_ref[...] = jnp.zeros_like(dq_ref)

        def step(j, masked):
            off = pl.multiple_of(j * tk, tk)
            qv = q_ref[...]
            dov = do_ref[...]
            kb = k_ref[pl.ds(off, tk), :]
            s = lax.dot_general(qv, kb, _NT, preferred_element_type=F32)
            if masked:
                s = _causal_mask(s, i, j, tq, tk)
            p = jnp.exp(s - lse_ref[:, :1])
            dv_ref[pl.ds(off, tk), :] += lax.dot_general(p.astype(BF16), dov, _TN, preferred_element_type=F32)
            dp = lax.dot_general(dov, v_ref[pl.ds(off, tk), :], _NT, preferred_element_type=F32)
            ds = (p * (dp - dl_ref[:, :1])).astype(BF16)
            dk_ref[pl.ds(off, tk), :] += lax.dot_general(ds, qv, _TN, preferred_element_type=F32)
            dq_ref[...] += jnp.dot(ds, kb, preferred_element_type=F32)

        n_full = i * nd

        def full_step(j, carry):
            step(j, False)
            return carry

        lax.fori_loop(0, n_full, full_step, 0)
        for d in range(nd):
            step(n_full + d, True)
        pl.when(jnp.logical_and(h == H - 1, i == nq - 1))(scatter.finish)

    return pl.pallas_call(
        body, name="flash_bwd", grid=(H, nq),
        in_specs=[pl.BlockSpec((None, tq, dq), lambda h, i: (h, i, 0)),
                  pl.BlockSpec((None, S, dq), lambda h, i: (h, 0, 0)),
                  pl.BlockSpec((None, S, dv), lambda h, i: (h, 0, 0)),
                  pl.BlockSpec((tq, dv), lambda h, i: (i, h)),
                  pl.BlockSpec((None, tq, LANES), lambda h, i: (h, i, 0)),
                  pl.BlockSpec((None, tq, LANES), lambda h, i: (h, i, 0))] + [_ANY] * nw,
        out_specs=[pl.BlockSpec((None, tq, dq), lambda h, i: (h, i, 0)),
                   pl.BlockSpec((None, S, dq), lambda h, i: (h, 0, 0)),
                   pl.BlockSpec((None, S, dv), lambda h, i: (h, 0, 0))] + [_ANY] * nw,
        out_shape=[jax.ShapeDtypeStruct((H, S, dq), F32), jax.ShapeDtypeStruct((H, S, dq), F32),
                   jax.ShapeDtypeStruct((H, S, dv), F32)]
        + [jax.ShapeDtypeStruct((N_DEV - 1,) + sd.shape[1:], sd.dtype) for sd in sends],
        scratch_shapes=[pltpu.SemaphoreType.DMA((nw, N_DEV - 1)), pltpu.SemaphoreType.DMA((nw, N_DEV - 1))],
        compiler_params=_params("arbitrary", "arbitrary"),
    )(q, k, v, do, lse, delta, *sends)


def _mem_norm(mem, g):
    n = mem.shape[1]

    def body(m_ref, g_ref, o_ref):
        xhat, _ = _rms_hat(m_ref[...], n)
        o_ref[...] = (xhat * g_ref[...]).astype(BF16)

    return pl.pallas_call(body, name="mem_norm", out_shape=jax.ShapeDtypeStruct(mem.shape, BF16),
                          compiler_params=pltpu.CompilerParams(vmem_limit_bytes=VMEM_LIMIT))(mem, g)


def _memkv_prep(mkv, gk):
    M = mkv.shape[0]

    def body(mkv_ref, g_ref, k_ref, v_ref):
        for h in range(X_HEADS):
            cols = slice(h * X_HEAD_DIM, (h + 1) * X_HEAD_DIM)
            xhat, _ = _rms_hat(mkv_ref[:, cols], X_HEAD_DIM)
            k_ref[:, cols] = (xhat * g_ref[...]).astype(BF16)
        v_ref[...] = mkv_ref[:, X_WIDTH:].astype(BF16)

    return pl.pallas_call(
        body, name="memkv_prep",
        out_shape=[jax.ShapeDtypeStruct((M, X_WIDTH), BF16), jax.ShapeDtypeStruct((M, X_WIDTH), BF16)],
    )(mkv, gk)


def _memkv_bwd(dk, dv, mkv, gk):
    M = mkv.shape[0]

    def body(dk_ref, dv_ref, mkv_ref, g_ref, dmkv_ref, dg_ref):
        dg = jnp.zeros((1, X_HEAD_DIM), F32)
        for h in range(X_HEADS):
            cols = slice(h * X_HEAD_DIM, (h + 1) * X_HEAD_DIM)
            xhat, r = _rms_hat(mkv_ref[:, cols], X_HEAD_DIM)
            dx, dgh = _rms_bwd(dk_ref[:, cols], xhat, r, g_ref[...], X_HEAD_DIM)
            dmkv_ref[:, cols] = dx.astype(BF16)
            dg += dgh
        dmkv_ref[:, X_WIDTH:] = dv_ref[...].astype(BF16)
        dg_ref[...] = dg

    return pl.pallas_call(
        body, name="memkv_bwd",
        out_shape=[jax.ShapeDtypeStruct((M, 2 * X_WIDTH), BF16), jax.ShapeDtypeStruct((1, X_HEAD_DIM), F32)],
    )(dk, dv, mkv, gk)


def _mem_gain_bwd(mem, dmn):
    n = mem.shape[1]

    def body(m_ref, d_ref, dg_ref):
        xhat, _ = _rms_hat(m_ref[...], n)
        dg_ref[...] = jnp.sum(d_ref[...] * xhat, axis=0, keepdims=True)

    return pl.pallas_call(body, name="mem_gain_bwd", out_shape=jax.ShapeDtypeStruct((1, n), F32),
                          compiler_params=pltpu.CompilerParams(vmem_limit_bytes=VMEM_LIMIT))(mem, dmn)


def _memx_scores(zq_ref, g, kx_ref, h):
    cols = slice(h * X_HEAD_DIM, (h + 1) * X_HEAD_DIM)
    xhat, r = _rms_hat(zq_ref[:, cols], X_HEAD_DIM)
    qn = (xhat * g).astype(BF16)
    s = lax.dot_general(qn, kx_ref[:, cols], _NT, preferred_element_type=F32) * (1.0 / math.sqrt(X_HEAD_DIM))
    p = jnp.exp(s - jnp.max(s, axis=-1, keepdims=True))
    return cols, xhat, r, qn, p, jnp.sum(p, axis=-1, keepdims=True)


def _memx_fwd(z, g, kx, vx, tr=512):
    S = z.shape[0]
    tr = _tile(S, tr)
    M = kx.shape[0]

    def body(zq_ref, g_ref, kx_ref, vx_ref, y_ref):
        for h in range(X_HEADS):
            cols, _, _, _, p, l = _memx_scores(zq_ref, g_ref[...], kx_ref, h)
            o = jnp.dot(p.astype(BF16), vx_ref[:, cols], preferred_element_type=F32)
            y_ref[:, cols] = (o / l).astype(BF16)

    return pl.pallas_call(
        body, name="memx_fwd", grid=(S // tr,),
        in_specs=[_rows(tr, X_WIDTH, Z_MQ // X_WIDTH), _full((1, X_HEAD_DIM)), _full((M, X_WIDTH)),
                  _full((M, X_WIDTH))],
        out_specs=_rows(tr, X_WIDTH),
        out_shape=jax.ShapeDtypeStruct((S, X_WIDTH), BF16),
        compiler_params=_params("parallel"),
    )(z, g, kx, vx)


def _memx_bwd(z, g, kx, vx, dyc, dy_cb, tr=512):
    S = z.shape[0]
    tr = _tile(S, tr)
    M = kx.shape[0]
    scale = 1.0 / math.sqrt(X_HEAD_DIM)

    def body(zq_ref, g_ref, kx_ref, vx_ref, dy_ref, dz_ref, dk_ref, dv_ref, dg_ref):
        @pl.when(pl.program_id(0) == 0)
        def _():
            dk_ref[...] = jnp.zeros_like(dk_ref)
            dv_ref[...] = jnp.zeros_like(dv_ref)
            dg_ref[...] = jnp.zeros_like(dg_ref)

        gv = g_ref[...]
        for h in range(X_HEADS):
            cols, xhat, r, qn, p, l = _memx_scores(zq_ref, gv, kx_ref, h)
            p = p / l
            do = dy_ref[:, cols].astype(BF16)
            dv_ref[:, cols] += lax.dot_general(p.astype(BF16), do, _TN, preferred_element_type=F32)
            dp = lax.dot_general(do, vx_ref[:, cols], _NT, preferred_element_type=F32)
            ds = (p * (dp - jnp.sum(dp * p, axis=-1, keepdims=True)) * scale).astype(BF16)
            dqn = jnp.dot(ds, kx_ref[:, cols], preferred_element_type=F32)
            dk_ref[:, cols] += lax.dot_general(ds, qn, _TN, preferred_element_type=F32)
            dx, dg = _rms_bwd(dqn, xhat, r, gv, X_HEAD_DIM)
            dz_ref[:, cols] = dx.astype(BF16)
            dg_ref[...] += dg

    return pl.pallas_call(
        body, name="memx_bwd", grid=(S // tr,),
        in_specs=[_rows(tr, X_WIDTH, Z_MQ // X_WIDTH), _full((1, X_HEAD_DIM)), _full((M, X_WIDTH)),
                  _full((M, X_WIDTH)), _rows(tr, X_WIDTH, dy_cb)],
        out_specs=[_rows(tr, X_WIDTH), _full((M, X_WIDTH)), _full((M, X_WIDTH)), _full((1, X_HEAD_DIM))],
        out_shape=[jax.ShapeDtypeStruct((S, X_WIDTH), BF16), jax.ShapeDtypeStruct((M, X_WIDTH), F32),
                   jax.ShapeDtypeStruct((M, X_WIDTH), F32), jax.ShapeDtypeStruct((1, X_HEAD_DIM), F32)],
        compiler_params=_params("arbitrary"),
    )(z, g, kx, vx, dyc)


def _conv_gc(g_ext, w_ref, b_ref, n_ext):
    g1 = pltpu.roll(g_ext, 1, 0)
    g2 = pltpu.roll(g_ext, 2, 0)
    gc = b_ref[...] + w_ref[0:1, :] * g2
    gc = gc + w_ref[1:2, :] * g1
    gc = gc + w_ref[2:3, :] * g_ext
    return gc, g1, g2


def _convglu_fwd(g, u, cw, cb, tr=512, tc=512):
    S, F = g.shape
    tr, tc = _tile(S, tr), _tile(F, tc)
    rb = tr // CONV_HALO

    def body(g_ref, gp_ref, u_ref, w_ref, b_ref, a_ref):
        i = pl.program_id(1)
        halo = jnp.where(i == 0, 0.0, gp_ref[...])
        g_ext = jnp.concatenate([halo, g_ref[...]], axis=0)
        gc, _, _ = _conv_gc(g_ext, w_ref, b_ref, tr + CONV_HALO)
        gc = gc[CONV_HALO:]
        sig = 1.0 / (1.0 + jnp.exp(-gc))
        a_ref[...] = (gc * sig * u_ref[...]).astype(BF16)

    return pl.pallas_call(
        body, name="convglu_fwd", grid=(F // tc, S // tr),
        in_specs=[pl.BlockSpec((tr, tc), lambda j, i: (i, j)),
                  pl.BlockSpec((CONV_HALO, tc), lambda j, i: (jnp.maximum(i * rb - 1, 0), j)),
                  pl.BlockSpec((tr, tc), lambda j, i: (i, j)),
                  pl.BlockSpec((3, tc), lambda j, i: (0, j)), pl.BlockSpec((1, tc), lambda j, i: (0, j))],
        out_specs=pl.BlockSpec((tr, tc), lambda j, i: (i, j)),
        out_shape=jax.ShapeDtypeStruct((S, F), BF16),
        compiler_params=_params("parallel", "parallel"),
    )(g, g, u, cw, cb)


def _convglu_bwd(g, u, da, cw, cb, tr=512, tc=512):
    S, F = g.shape
    tr, tc = _tile(S, tr), _tile(F, tc)
    rb = tr // CONV_HALO
    nhb = S // CONV_HALO
    H = CONV_HALO

    def body(g_ref, gp_ref, gn_ref, u_ref, un_ref, da_ref, dan_ref, w_ref, b_ref,
             dg_ref, du_ref, dw_ref, db_ref):
        i = pl.program_id(1)
        last = i == pl.num_programs(1) - 1

        @pl.when(i == 0)
        def _():
            dw_ref[...] = jnp.zeros_like(dw_ref)
            db_ref[...] = jnp.zeros_like(db_ref)

        g_prev = jnp.where(i == 0, 0.0, gp_ref[...])
        g_cur = g_ref[...]
        g_ext = jnp.concatenate([g_prev, g_cur, gn_ref[...]], axis=0)
        gc, g1, g2 = _conv_gc(g_ext, w_ref, b_ref, tr + 2 * H)
        gc = gc[H:]
        ux = jnp.concatenate([u_ref[...], un_ref[...]], axis=0)
        dax = jnp.concatenate([da_ref[...], jnp.where(last, 0.0, dan_ref[...])], axis=0)
        sig = 1.0 / (1.0 + jnp.exp(-gc))
        du_ref[...] = (dax[:tr] * (gc[:tr] * sig[:tr])).astype(BF16)
        dgc = dax * ux * (sig * (1.0 + gc * (1.0 - sig)))
        n = tr + H
        d1 = pltpu.roll(dgc, n - 1, 0)[:tr]
        d2 = pltpu.roll(dgc, n - 2, 0)[:tr]
        d0 = dgc[:tr]
        dg_ref[...] = (w_ref[2:3, :] * d0 + w_ref[1:2, :] * d1 + w_ref[0:1, :] * d2).astype(BF16)
        db_ref[...] += jnp.sum(d0, axis=0, keepdims=True)
        dw_ref[0:1, :] += jnp.sum(d0 * g2[H:H + tr], axis=0, keepdims=True)
        dw_ref[1:2, :] += jnp.sum(d0 * g1[H:H + tr], axis=0, keepdims=True)
        dw_ref[2:3, :] += jnp.sum(d0 * g_cur, axis=0, keepdims=True)

    cur = pl.BlockSpec((tr, tc), lambda j, i: (i, j))
    prv = pl.BlockSpec((H, tc), lambda j, i: (jnp.maximum(i * rb - 1, 0), j))
    nxt = pl.BlockSpec((H, tc), lambda j, i: (jnp.minimum((i + 1) * rb, nhb - 1), j))
    return pl.pallas_call(
        body, name="convglu_bwd", grid=(F // tc, S // tr),
        in_specs=[cur, prv, nxt, cur, nxt, cur, nxt,
                  pl.BlockSpec((3, tc), lambda j, i: (0, j)), pl.BlockSpec((1, tc), lambda j, i: (0, j))],
        out_specs=[cur, cur, pl.BlockSpec((3, tc), lambda j, i: (0, j)), pl.BlockSpec((1, tc), lambda j, i: (0, j))],
        out_shape=[jax.ShapeDtypeStruct((S, F), BF16), jax.ShapeDtypeStruct((S, F), BF16),
                   jax.ShapeDtypeStruct((3, F), F32), jax.ShapeDtypeStruct((1, F), F32)],
        compiler_params=_params("parallel", "arbitrary"),
    )(g, g, g, u, u, da, da, cw, cb)


def _loss_head(y, target, tr=256):
    S, n = y.shape
    tr = _tile(S, tr)

    def body(y_ref, t_ref, dy_ref, dyb_ref, loss_ref):
        @pl.when(pl.program_id(0) == 0)
        def _():
            loss_ref[...] = jnp.zeros_like(loss_ref)

        err = y_ref[...] - t_ref[...]
        dy = err / n
        dy_ref[...] = dy
        dyb_ref[...] = dy.astype(BF16)
        per_row = jnp.sum(err * err, axis=-1, keepdims=True) / n
        loss_ref[...] += 0.5 * jnp.sum(per_row)

    return pl.pallas_call(
        body, name="loss_head", grid=(S // tr,),
        in_specs=[_rows(tr, n), _rows(tr, n)],
        out_specs=[_rows(tr, n), _rows(tr, n), _full((8, LANES))],
        out_shape=[jax.ShapeDtypeStruct((S, n), F32), jax.ShapeDtypeStruct((S, n), BF16),
                   jax.ShapeDtypeStruct((8, LANES), F32)],
        compiler_params=_params("arbitrary"),
    )(y, target)


def _local_step(x, mem, target, W, ffn_shards):
    S = x.shape[0]
    tabs = _rope_tables(S)
    W = dict(W)
    G = {}

    h = _rms_fwd("rms1_fwd", x, W["g_mix"])
    z = _mm("mm_in", h, W["w_in"], tm=512, tn=Z_COLS, tk=D_MODEL)
    y_pool = _pool_fwd(z, W["w_pool"], W["pool_scale"])
    ql, kvl = _lat_fwd(z, W["g_q_lat"], W["g_kv_lat"])
    q_raw = _mm("mm_q_up", ql, W["w_q_up"], tm=1024, tn=2048, tk=Q_LORA)
    kv_raw = _mm("mm_kv_up", kvl, W["w_kv_up"], tm=1024, tn=2048, tk=KV_LORA)
    q, k, v = _mla_prep(q_raw, kv_raw, z, tabs, W["g_q_mla"], W["g_k_mla"])
    o_mla, lse, wg3, wu3, W["w_down"], cw3 = _flash_fwd(q, k, v, ffn_shards, FFN_GATHER_KINDS)
    W["w_gate"] = jnp.transpose(wg3, (1, 0, 2)).reshape(D_MODEL, D_FF)
    W["w_up"] = jnp.transpose(wu3, (1, 0, 2)).reshape(D_MODEL, D_FF)
    cw = jnp.sum(cw3.reshape(N_DEV, 3, 3, D_FF // N_DEV).astype(F32), axis=1)
    W["conv_w"] = jnp.transpose(cw, (1, 0, 2)).reshape(3, D_FF)
    mn = _mem_norm(mem, W["g_mem"])
    mkv = _mm("mm_mem_kv", mn, W["w_mem_kv"], tm=256, tn=1024, tk=D_MODEL)
    kx, vx = _memkv_prep(mkv, W["g_k_x"])
    y_mem = _memx_fwd(z, W["g_q_x"], kx, vx)
    ycat = jnp.concatenate([y_pool, o_mla, y_mem], axis=1)
    x1 = _mm("mm_o", ycat, W["w_o"], add=x, tm=512, tn=2048, tk=D_MODEL)
    h2 = _rms_fwd("rms2_fwd", x1, W["g_ffn"])
    g = _mm("mm_gate", h2, W["w_gate"], tm=1024, tn=512, tk=D_MODEL)
    u = _mm("mm_up", h2, W["w_up"], tm=1024, tn=512, tk=D_MODEL)
    a = _convglu_fwd(g, u, W["conv_w"], W["conv_b"])
    y = _mm("mm_down", a, W["w_down"], add=x1, tm=1024, tn=512, tk=D_FF)
    dy, dyb, loss_part = _loss_head(y, target)

    da = _mm("mm_down_dx", dyb, W["w_down"], tb=True, tm=1024, tn=512, tk=D_MODEL)
    G["w_down"] = _mm("mm_down_dw", a, dyb, ta=True, tm=512, tn=2048, tk=2048)
    dg, du, G["conv_w"], G["conv_b"] = _convglu_bwd(g, u, da, W["conv_w"], W["conv_b"])
    dh2 = _mm("mm_gate_dx", dg, W["w_gate"], tb=True, tm=1024, tn=512, tk=D_FF)
    dh2 = _mm("mm_up_dx", du, W["w_up"], tb=True, add=dh2, tm=1024, tn=512, tk=D_FF)
    G["w_gate"] = _mm("mm_gate_dw", h2, dg, ta=True, tm=1024, tn=1408, tk=1024)
    G["w_up"] = _mm("mm_up_dw", h2, du, ta=True, tm=1024, tn=1408, tk=1024)
    dx1, dx1b, G["g_ffn"] = _rms_bwd_call("rms2_bwd", x1, W["g_ffn"], dh2, dy, True)

    dyc = _mm("mm_o_dx", dx1b, W["w_o"], tb=True, tm=512, tn=2048, tk=D_MODEL)
    G["w_o"] = _mm("mm_o_dw", ycat, dx1b, ta=True, tm=1024, tn=1024, tk=2048)
    dz_pool, G["w_pool"], G["pool_scale"] = _pool_bwd(z, dyc, W["w_pool"], W["pool_scale"], dy_cb=0)
    dz_mq, dkx, dvx, G["g_q_x"] = _memx_bwd(z, W["g_q_x"], kx, vx, dyc, dy_cb=3)
    dmkv, G["g_k_x"] = _memkv_bwd(dkx, dvx, mkv, W["g_k_x"])
    G["w_mem_kv"] = _mm("mm_mem_kv_dw", mn, dmkv, ta=True, tm=1024, tn=1024, tk=MEM_LEN)
    dmn = _mm("mm_mem_kv_dx", dmkv, W["w_mem_kv"], tb=True, tm=256, tn=2048, tk=1024)
    G["g_mem"] = _mem_gain_bwd(mem, dmn)
    do, delta = _attn_bwd_prep(dyc, o_mla, POOL_WIDTH // MLA_V)
    sends = [_send_blocks(n, G[n]) for n in SCATTER_EARLY]
    dq, dk, dv, *got_early = _flash_bwd(q, k, v, do, lse, delta, sends)
    dq_raw, dkv_raw, dz_kr, G["g_q_mla"], G["g_k_mla"] = _mla_prep_bwd(
        dq, dk, dv, q_raw, kv_raw, z, tabs, W["g_q_mla"], W["g_k_mla"])
    G["w_q_up"] = _mm("mm_q_up_dw", ql, dq_raw, ta=True, tm=512, tn=2048, tk=1024)
    dql = _mm("mm_q_up_dx", dq_raw, W["w_q_up"], tb=True, tm=1024, tn=512, tk=2048)
    G["w_kv_up"] = _mm("mm_kv_up_dw", kvl, dkv_raw, ta=True, tm=256, tn=2048, tk=1024)
    dkvl = _mm("mm_kv_up_dx", dkv_raw, W["w_kv_up"], tb=True, tm=1024, tn=256, tk=2048)
    dz_q, dz_kv, G["g_q_lat"], G["g_kv_lat"] = _lat_bwd(z, W["g_q_lat"], W["g_kv_lat"], dql, dkvl)
    dz = jnp.concatenate([dz_pool, dz_q, dz_mq, dz_kv, dz_kr], axis=1)
    G["w_in"] = _mm("mm_in_dw", h, dz, ta=True, tm=1024, tn=Z_COLS, tk=1024)
    dh = _mm("mm_in_dx", dz, W["w_in"], tb=True, tm=512, tn=2048, tk=Z_COLS)
    grad_x, G["g_mix"] = _rms_bwd_call("rms1_bwd", x, W["g_mix"], dh, dx1, False)
    return loss_part, grad_x, G, dict(zip(SCATTER_EARLY, got_early))


_ANY = pl.BlockSpec(memory_space=pl.ANY)
_MESH = pl.DeviceIdType.MESH


def _gather_out_shape(kind, shape):
    if kind == "rows":
        return (N_DEV * shape[0],) + tuple(shape[1:])
    if kind == "cols":
        return (shape[0], N_DEV * shape[1])
    return (N_DEV,) + tuple(shape)


def _gather_view(ref, kind, shape, d):
    if kind == "rows":
        return ref.at[pl.ds(pl.multiple_of(d * shape[0], 16), shape[0]), :]
    if kind == "cols":
        return ref.at[:, pl.ds(pl.multiple_of(d * shape[1], LANES), shape[1])]
    return ref.at[d]


class _Gather:
    def __init__(self, x_refs, out_refs, kinds, send_sems, recv_sems, local_sems):
        self.xr, self.outr, self.kinds = x_refs, out_refs, kinds
        self.ss, self.rs, self.ls = send_sems, recv_sems, local_sems
        x, y, c = lax.axis_index("x"), lax.axis_index("y"), lax.axis_index("c")
        self.c = c
        self.me, self.sibling = (x, y, c), (x, y, 1 - c)
        self.chips = [(1 - x, y), (x, 1 - y), (1 - x, 1 - y)]

    def _view(self, w, dev):
        px, py, pc = dev
        return _gather_view(self.outr[w], self.kinds[w], self.xr[w].shape, 4 * px + 2 * py + pc)

    def _copy(self, w, k, block, to, from_shard=False):
        v = self._view(w, block)
        return pltpu.make_async_remote_copy(
            src_ref=self.xr[w] if from_shard else v, dst_ref=v, send_sem=self.ss.at[w, k],
            recv_sem=self.rs.at[w, k], device_id=to, device_id_type=_MESH)

    def _local(self, w):
        return pltpu.make_async_copy(self.xr[w], self._view(w, self.me), self.ls.at[w])

    def start(self):
        for w in range(len(self.xr)):
            self._local(w).start()
            self._copy(w, 0, self.me, self.sibling, True).start()
            for j, chip in enumerate(self.chips):
                self._copy(w, 1 + j, self.me, (*chip, self.c), True).start()

    def forward(self):
        for j, chip in enumerate(self.chips):
            for w in range(len(self.xr)):
                self._copy(w, 1 + j, (*chip, self.c), self.me).wait_recv()
                self._copy(w, 4 + j, (*chip, self.c), self.sibling).start()

    def finish(self):
        for w in range(len(self.xr)):
            self._copy(w, 0, self.sibling, self.me).wait_recv()
            for j, chip in enumerate(self.chips):
                self._copy(w, 4 + j, (*chip, 1 - self.c), self.me).wait_recv()
            self._copy(w, 0, self.me, self.sibling, True).wait_send()
            for j, chip in enumerate(self.chips):
                self._copy(w, 1 + j, self.me, (*chip, self.c), True).wait_send()
                self._copy(w, 4 + j, (*chip, self.c), self.sibling).wait_send()
            self._local(w).wait()


class _Scatter:
    def __init__(self, send_refs, recv_refs, send_sems, recv_sems):
        self.sr, self.rr, self.ss, self.rs = send_refs, recv_refs, send_sems, recv_sems
        self.xyz = lax.axis_index("x"), lax.axis_index("y"), lax.axis_index("c")

    def _copy(self, w, k):
        x, y, c = self.xyz
        px, py, pc = x ^ ((k >> 2) & 1), y ^ ((k >> 1) & 1), c ^ (k & 1)
        return pltpu.make_async_remote_copy(
            src_ref=self.sr[w].at[4 * px + 2 * py + pc], dst_ref=self.rr[w].at[k - 1],
            send_sem=self.ss.at[w, k - 1], recv_sem=self.rs.at[w, k - 1],
            device_id=(px, py, pc), device_id_type=_MESH)

    def _all(self):
        return [self._copy(w, k) for w in range(len(self.sr)) for k in range(1, N_DEV)]

    def start(self):
        for cp in self._all():
            cp.start()

    def finish(self):
        for cp in self._all():
            cp.wait_recv()
        for cp in self._all():
            cp.wait_send()


def _gather_call(shards, kinds):
    nw = len(shards)

    def body(*refs):
        gather = _Gather(refs[:nw], refs[nw:2 * nw], kinds, *refs[2 * nw:])
        gather.start()
        gather.forward()
        gather.finish()

    return pl.pallas_call(
        body, name="gather_mixer_weights",
        out_shape=[jax.ShapeDtypeStruct(_gather_out_shape(kd, sd.shape), sd.dtype) for kd, sd in zip(kinds, shards)],
        in_specs=[_ANY] * nw, out_specs=[_ANY] * nw,
        scratch_shapes=[pltpu.SemaphoreType.DMA((nw, 7)), pltpu.SemaphoreType.DMA((nw, 7)),
                        pltpu.SemaphoreType.DMA((nw,))],
    )(*shards)


def _scatter_late(sends, s):
    nw = len(sends)

    def body(*refs):
        s_refs, x_ref = refs[:nw], refs[nw]
        r_refs, out_ref = refs[nw + 1:2 * nw + 1], refs[2 * nw + 1]
        send_sems, recv_sems, x_send, x_recv, local_sem = refs[2 * nw + 2:]
        x, y, c = lax.axis_index("x"), lax.axis_index("y"), lax.axis_index("c")
        me = 4 * x + 2 * y + c
        scatter = _Scatter(s_refs, r_refs, send_sems, recv_sems)
        scatter.start()
        mine = pltpu.make_async_copy(x_ref, out_ref.at[me], local_sem)
        mine.start()

        def copy(k):
            px, py, pc = x ^ ((k >> 2) & 1), y ^ ((k >> 1) & 1), c ^ (k & 1)
            return pltpu.make_async_remote_copy(
                src_ref=x_ref, dst_ref=out_ref.at[me], send_sem=x_send.at[k - 1], recv_sem=x_recv.at[k - 1],
                device_id=(px, py, pc), device_id_type=_MESH)

        cps = [copy(k) for k in range(1, N_DEV)]
        for cp in cps:
            cp.start()
        for cp in cps:
            cp.wait_recv()
        for cp in cps:
            cp.wait_send()
        mine.wait()
        scatter.finish()

    return pl.pallas_call(
        body, name="scatter_late",
        out_shape=[jax.ShapeDtypeStruct((N_DEV - 1,) + sd.shape[1:], sd.dtype) for sd in sends]
        + [jax.ShapeDtypeStruct((N_DEV,) + s.shape, s.dtype)],
        in_specs=[_ANY] * (nw + 1), out_specs=[_ANY] * (nw + 1),
        scratch_shapes=[pltpu.SemaphoreType.DMA((nw, N_DEV - 1)), pltpu.SemaphoreType.DMA((nw, N_DEV - 1)),
                        pltpu.SemaphoreType.DMA((N_DEV - 1,)), pltpu.SemaphoreType.DMA((N_DEV - 1,)),
                        pltpu.SemaphoreType.DMA],
    )(*sends, s)


def _sum_slots(buf, tr=208):
    n, R, _ = buf.shape
    tr = tr if R % tr == 0 else R

    def body(b_ref, o_ref):
        acc = b_ref[0]
        for d in range(1, n):
            acc = acc + b_ref[d]
        o_ref[...] = acc

    return pl.pallas_call(
        body, name="sum_slots", grid=(R // tr,),
        in_specs=[pl.BlockSpec((n, tr, LANES), lambda i: (0, i, 0))],
        out_specs=pl.BlockSpec((tr, LANES), lambda i: (i, 0)),
        out_shape=jax.ShapeDtypeStruct((R, LANES), buf.dtype),
        compiler_params=_params("parallel"),
    )(buf)


def _adamw(name, w, g, got, m, v, tr=128):
    R, C = w.shape
    tr = max([t for t in range(16, tr + 1, 16) if R % t == 0], default=R) if R > tr else R
    has_got = got is not None

    def body(*refs):
        if has_got:
            w_ref, g_ref, got_ref, m_ref, v_ref, go_ref, d_ref, mo_ref, vo_ref = refs
        else:
            w_ref, g_ref, m_ref, v_ref, go_ref, d_ref, mo_ref, vo_ref = refs
        gv = g_ref[...]
        if has_got:
            for k in range(N_DEV - 1):
                gv = gv + got_ref[k].astype(F32)
        m_new = ADAM_B1 * m_ref[...] + (1.0 - ADAM_B1) * gv
        v_new = ADAM_B2 * v_ref[...] + (1.0 - ADAM_B2) * (gv * gv)
        m_hat = m_new / (1.0 - ADAM_B1 ** ADAM_STEP)
        v_hat = v_new / (1.0 - ADAM_B2 ** ADAM_STEP)
        go_ref[...] = gv
        d_ref[...] = -ADAM_LR * (m_hat / (jnp.sqrt(v_hat) + ADAM_EPS) + ADAM_WD * w_ref[...])
        mo_ref[...] = m_new
        vo_ref[...] = v_new

    spec = pl.BlockSpec((tr, C), lambda i: (i, 0))
    got_spec = [pl.BlockSpec((N_DEV - 1, tr, C), lambda i: (0, i, 0))] if has_got else []
    sds = jax.ShapeDtypeStruct((R, C), F32)
    ins = [w, g] + ([got] if has_got else []) + [m, v]
    return pl.pallas_call(
        body, name=name, grid=(R // tr,), in_specs=[spec] * 2 + got_spec + [spec] * 2, out_specs=[spec] * 4,
        out_shape=[sds] * 4, compiler_params=_params("parallel"),
    )(*ins)


def _pad_rows(a, mult):
    r = (-a.shape[0]) % mult
    return a if r == 0 else jnp.concatenate([a, jnp.zeros((r,) + a.shape[1:], a.dtype)], axis=0)


def _as_rows(a, mult):
    flat = a.reshape(-1)
    r = (-flat.shape[0]) % LANES
    if r:
        flat = jnp.concatenate([flat, jnp.zeros((r,), a.dtype)])
    return _pad_rows(flat.reshape(-1, LANES), mult)


def _w_in_to_kernel_cols(w):
    pad = jnp.zeros(w.shape[:-1] + (Z_COLS - IN_COLS,), w.dtype)
    return jnp.concatenate([w[..., :1024], w[..., 1344:1856], w[..., 1024:1344], pad], axis=-1)


def _w_in_from_kernel_cols(w):
    return jnp.concatenate([w[..., :1024], w[..., 1536:1856], w[..., 1024:1536]], axis=-1)


def _dev_blocks(n, g):
    ffs = D_FF // N_DEV
    if n == "w_in":
        return _w_in_from_kernel_cols(g).reshape(N_DEV, D_MODEL // N_DEV, IN_COLS)
    if n == "w_q_up":
        return jnp.transpose(g.reshape(Q_LORA, N_DEV, MLA_QK_PAD)[:, :, :MLA_QK], (1, 0, 2))
    if n == "w_kv_up":
        return jnp.transpose(g.reshape(KV_LORA, N_DEV, MLA_NOPE + MLA_V), (1, 0, 2))
    if n in ("w_mem_kv", "w_o"):
        return g.reshape(N_DEV, D_MODEL // N_DEV, g.shape[1])
    if n in ("w_gate", "w_up"):
        return jnp.transpose(g.reshape(D_MODEL, N_DEV, ffs), (1, 0, 2))
    if n == "conv_w":
        return jnp.transpose(g.reshape(3, N_DEV, ffs), (1, 0, 2))
    assert n == "w_down"
    return g.reshape(N_DEV, ffs, D_MODEL)


def _send_blocks(n, g):
    blocks = _dev_blocks(n, g)
    return blocks if n == "conv_w" else blocks.astype(BF16)


def kernel(x, mem, g_mix, w_in, g_q_lat, w_q_up, g_kv_lat, w_kv_up, g_q_mla, g_k_mla, w_pool, pool_scale, g_mem, w_mem_kv, g_q_x, g_k_x, w_o, g_ffn, w_gate, w_up, conv_w, conv_b, w_down, loss_target, m_g_mix, m_w_in, m_g_q_lat, m_w_q_up, m_g_kv_lat, m_w_kv_up, m_g_q_mla, m_g_k_mla, m_w_pool, m_pool_scale, m_g_mem, m_w_mem_kv, m_g_q_x, m_g_k_x, m_w_o, m_g_ffn, m_w_gate, m_w_up, m_conv_w, m_conv_b, m_w_down, v_g_mix, v_w_in, v_g_q_lat, v_w_q_up, v_g_kv_lat, v_w_kv_up, v_g_q_mla, v_g_k_mla, v_w_pool, v_pool_scale, v_g_mem, v_w_mem_kv, v_g_q_x, v_g_k_x, v_w_o, v_g_ffn, v_w_gate, v_w_up, v_conv_w, v_conv_b, v_w_down):
    given = dict(g_mix=g_mix, w_in=w_in, g_q_lat=g_q_lat, w_q_up=w_q_up, g_kv_lat=g_kv_lat, w_kv_up=w_kv_up,
                 g_q_mla=g_q_mla, g_k_mla=g_k_mla, w_pool=w_pool, pool_scale=pool_scale, g_mem=g_mem,
                 w_mem_kv=w_mem_kv, g_q_x=g_q_x, g_k_x=g_k_x, w_o=w_o, g_ffn=g_ffn, w_gate=w_gate, w_up=w_up,
                 conv_w=conv_w, conv_b=conv_b, w_down=w_down)
    mom_m = dict(g_mix=m_g_mix, w_in=m_w_in, g_q_lat=m_g_q_lat, w_q_up=m_w_q_up, g_kv_lat=m_g_kv_lat,
                 w_kv_up=m_w_kv_up, g_q_mla=m_g_q_mla, g_k_mla=m_g_k_mla, w_pool=m_w_pool,
                 pool_scale=m_pool_scale, g_mem=m_g_mem, w_mem_kv=m_w_mem_kv, g_q_x=m_g_q_x, g_k_x=m_g_k_x,
                 w_o=m_w_o, g_ffn=m_g_ffn, w_gate=m_w_gate, w_up=m_w_up, conv_w=m_conv_w, conv_b=m_conv_b,
                 w_down=m_w_down)
    mom_v = dict(g_mix=v_g_mix, w_in=v_w_in, g_q_lat=v_g_q_lat, w_q_up=v_w_q_up, g_kv_lat=v_g_kv_lat,
                 w_kv_up=v_w_kv_up, g_q_mla=v_g_q_mla, g_k_mla=v_g_k_mla, w_pool=v_w_pool,
                 pool_scale=v_pool_scale, g_mem=v_g_mem, w_mem_kv=v_w_mem_kv, g_q_x=v_g_q_x, g_k_x=v_g_k_x,
                 w_o=v_w_o, g_ffn=v_g_ffn, w_gate=v_w_gate, w_up=v_w_up, conv_w=v_conv_w, conv_b=v_conv_b,
                 w_down=v_w_down)
    drop = lambda a: a[0] if a.ndim > 2 else a
    sh = {n: drop(given[n]) for n in WEIGHTS}
    mom_m = {n: drop(mom_m[n]) for n in WEIGHTS}
    mom_v = {n: drop(mom_v[n]) for n in WEIGHTS}

    cw_hi = sh["conv_w"].astype(BF16)
    cw_r = sh["conv_w"] - cw_hi.astype(F32)
    cw_mid = cw_r.astype(BF16)
    cw_lo = (cw_r - cw_mid.astype(F32)).astype(BF16)
    mixer_shards = [
        _w_in_to_kernel_cols(sh["w_in"]).astype(BF16),
        jnp.pad(sh["w_q_up"], ((0, 0), (0, MLA_QK_PAD - MLA_QK))).astype(BF16),
        sh["w_kv_up"].astype(BF16), sh["w_mem_kv"].astype(BF16), sh["w_o"].astype(BF16),
    ]
    ffn_shards = [sh["w_gate"].astype(BF16), sh["w_up"].astype(BF16), sh["w_down"].astype(BF16),
                  jnp.concatenate([cw_hi, cw_mid, cw_lo], axis=0)]
    W = dict(zip(("w_in", "w_q_up", "w_kv_up", "w_mem_kv", "w_o"),
                 _gather_call(mixer_shards, ("rows", "cols", "cols", "rows", "rows"))))
    W["w_pool"] = sh["w_pool"].astype(BF16)
    for n in ("g_mix", "g_q_lat", "g_kv_lat", "pool_scale", "g_mem", "g_q_x", "g_k_x", "g_ffn", "conv_b"):
        W[n] = sh[n]
    pad_qk = lambda gv: jnp.pad(gv, ((0, 0), (0, MLA_QK_PAD - MLA_QK)))
    W["g_q_mla"], W["g_k_mla"] = pad_qk(sh["g_q_mla"]), pad_qk(sh["g_k_mla"])

    loss_part, grad_x, G, got = _local_step(x[0], mem[0], loss_target[0], W, ffn_shards)

    small = {
        "g_mix": G["g_mix"], "g_q_lat": G["g_q_lat"], "g_kv_lat": G["g_kv_lat"],
        "g_q_mla": G["g_q_mla"][:, :MLA_QK], "g_k_mla": G["g_k_mla"][:, :MLA_QK],
        "w_pool": G["w_pool"], "pool_scale": G["pool_scale"], "g_mem": G["g_mem"],
        "g_q_x": G["g_q_x"], "g_k_x": G["g_k_x"], "g_ffn": G["g_ffn"], "conv_b": G["conv_b"],
    }
    s_offs = {}
    segs = []
    off = 0
    for n in SMALL:
        r = _as_rows(small[n], 8)
        s_offs[n] = off
        off += r.shape[0]
        segs.append(r)
    segs.append(loss_part)
    loss_row = off
    sbuf = jnp.concatenate(segs, axis=0)
    *got_late, s_all = _scatter_late([_send_blocks(n, G[n]) for n in SCATTER_LATE], sbuf)
    got.update(zip(SCATTER_LATE, got_late))
    s_sum = _sum_slots(s_all)

    def small_take(buf, n):
        shape = sh[n].shape
        cnt = math.prod(shape)
        return buf[s_offs[n]:s_offs[n] + -(-cnt // LANES)].reshape(-1)[:cnt].reshape(shape)

    loss = s_sum[loss_row, 0]

    me = 4 * lax.axis_index("x") + 2 * lax.axis_index("y") + lax.axis_index("c")
    grads, deltas, new_m, new_v = {}, {}, {}, {}
    for n in BIG:
        shape = sh[n].shape
        own = lax.dynamic_index_in_dim(_dev_blocks(n, G[n]), me, 0, keepdims=False)
        flat = lambda a: a.reshape(-1, shape[-1])
        outs = _adamw("adamw_" + n, flat(sh[n]), flat(own), got[n].reshape(N_DEV - 1, -1, shape[-1]),
                      flat(mom_m[n]), flat(mom_v[n]))
        grads[n], deltas[n], new_m[n], new_v[n] = (o.reshape(shape) for o in outs)

    def pack_small(src):
        return jnp.concatenate([_as_rows(src[n], 8) for n in SMALL], axis=0)

    _, d_s, m_s, v_s = _adamw("adamw_small", pack_small(sh), s_sum[:loss_row], None,
                              pack_small(mom_m), pack_small(mom_v))
    for n in SMALL:
        grads[n] = small_take(s_sum, n)
        deltas[n], new_m[n], new_v[n] = small_take(d_s, n), small_take(m_s, n), small_take(v_s, n)

    lead = lambda n, a: a.reshape(given[n].shape)
    return (loss, grad_x[None],
            *[lead(n, grads[n]) for n in WEIGHTS], *[lead(n, deltas[n]) for n in WEIGHTS],
            *[lead(n, new_m[n]) for n in WEIGHTS], *[lead(n, new_v[n]) for n in WEIGHTS])
```

```python
import math

import jax
import jax.numpy as jnp
from jax import lax
from jax.experimental import pallas as pl
from jax.experimental.pallas import tpu as pltpu

F32 = jnp.float32
BF16 = jnp.bfloat16

D_MODEL = 2048
D_FF = 5632
POOL_WIDTH = 512
POOL_WINDOWS = (2, 4, 8, 16)
POOL_HALO = 16
MLA_HEADS = 8
MLA_NOPE = 128
MLA_ROPE = 64
MLA_QK = MLA_NOPE + MLA_ROPE
MLA_QK_PAD = 256
MLA_V = 128
Q_LORA = 512
KV_LORA = 256
X_HEADS = 4
X_HEAD_DIM = 128
X_WIDTH = 512
MEM_LEN = 256
ROPE_THETA = 10000.0
NORM_EPS = 1e-6
CONV_HALO = 16
IN_COLS = 1856
Z_COLS = 1920
Z_POOL, Z_Q, Z_MQ, Z_KV, Z_KR = 0, 512, 1024, 1536, 1792

ADAM_LR = 0.001
ADAM_B1 = 0.9
ADAM_B2 = 0.999
ADAM_EPS = 1e-08
ADAM_WD = 0.01
ADAM_STEP = 10

N_DEV = 8
LANES = 128
VMEM_LIMIT = 56 * 1024 * 1024

BIG = ("w_in", "w_q_up", "w_kv_up", "w_mem_kv", "w_o", "w_gate", "w_up", "conv_w", "w_down")
SCATTER_EARLY = ("w_down", "w_gate", "w_up", "conv_w", "w_o", "w_mem_kv")
SCATTER_LATE = ("w_in", "w_q_up", "w_kv_up")
LATE_GATHER_KINDS = ("lead", "lead", "rows", "lead", "rows", "rows")
SMALL = ("g_mix", "g_q_lat", "g_kv_lat", "g_q_mla", "g_k_mla", "w_pool", "pool_scale", "g_mem",
         "g_q_x", "g_k_x", "g_ffn", "conv_b")
WEIGHTS = ("g_mix", "w_in", "g_q_lat", "w_q_up", "g_kv_lat", "w_kv_up", "g_q_mla", "g_k_mla", "w_pool",
           "pool_scale", "g_mem", "w_mem_kv", "g_q_x", "g_k_x", "w_o", "g_ffn", "w_gate", "w_up", "conv_w",
           "conv_b", "w_down")


def _tile(n, t):
    if n <= t:
        return n
    for c in range(t - t % LANES, 0, -LANES):
        if n % c == 0:
            return c
    return n


def _params(*sem):
    return pltpu.CompilerParams(dimension_semantics=sem, vmem_limit_bytes=VMEM_LIMIT)


def _full(shape):
    nd = len(shape)
    return pl.BlockSpec(shape, lambda *_: (0,) * nd)


def _rows(tr, w, cb=0):
    return pl.BlockSpec((tr, w), lambda i: (i, cb))


def _rms_hat(x, n):
    r = lax.rsqrt(jnp.sum(x * x, axis=-1, keepdims=True) / n + NORM_EPS)
    return x * r, r


def _rms_bwd(dy, xhat, r, g, n):
    dxh = dy * g
    dx = r * (dxh - xhat * (jnp.sum(dxh * xhat, axis=-1, keepdims=True) / n))
    return dx, jnp.sum(dy * xhat, axis=0, keepdims=True)


def _mm(name, a, b, *, ta=False, tb=False, add=None, out_dtype=F32, tm=512, tn=512, tk=512, sends=()):
    if ta:
        K, M = a.shape
    else:
        M, K = a.shape
    if tb:
        N, K2 = b.shape
    else:
        K2, N = b.shape
    assert K == K2, (name, a.shape, b.shape)
    tm, tn, tk = _tile(M, tm), _tile(N, tn), _tile(K, tk)
    nk = K // tk
    grid = (M // tm, N // tn, nk)
    a_spec = pl.BlockSpec((tk, tm), lambda i, j, k: (k, i)) if ta else pl.BlockSpec((tm, tk), lambda i, j, k: (i, k))
    b_spec = pl.BlockSpec((tn, tk), lambda i, j, k: (j, k)) if tb else pl.BlockSpec((tk, tn), lambda i, j, k: (k, j))
    o_spec = pl.BlockSpec((tm, tn), lambda i, j, k: (i, j))
    dims = (((0,) if ta else (1,), (1,) if tb else (0,)), ((), ()))
    has_add = add is not None
    n_in = 3 if has_add else 2
    nw = len(sends)

    def body(*refs):
        a_ref, b_ref = refs[:2]
        add_ref = refs[2] if has_add else None
        s_refs, o_ref, r_refs = refs[n_in:n_in + nw], refs[n_in + nw], refs[n_in + nw + 1:n_in + 2 * nw + 1]
        scr = refs[n_in + 2 * nw + 1:]
        step = (pl.program_id(0) * grid[1] + pl.program_id(1)) * grid[2] + pl.program_id(2)
        if nw:
            scatter = _Scatter(s_refs, r_refs, *scr[-2:])
            pl.when(step == 0)(scatter.start)
        part = lax.dot_general(a_ref[...], b_ref[...], dims, preferred_element_type=F32)
        if nk == 1:
            if has_add:
                part = part + add_ref[...]
            o_ref[...] = part.astype(o_ref.dtype)
        else:
            acc = scr[0]
            k = pl.program_id(2)

            @pl.when(k == 0)
            def _():
                acc[...] = part

            @pl.when(k > 0)
            def _():
                acc[...] += part

            @pl.when(k == nk - 1)
            def _():
                r = acc[...]
                if has_add:
                    r = r + add_ref[...]
                o_ref[...] = r.astype(o_ref.dtype)
        if nw:
            pl.when(step == grid[0] * grid[1] * grid[2] - 1)(scatter.finish)

    ins = [a, b] + ([add] if has_add else []) + list(sends)
    in_specs = [a_spec, b_spec] + ([o_spec] if has_add else []) + [_ANY] * nw
    out_shape = [jax.ShapeDtypeStruct((M, N), out_dtype)]
    out_shape += [jax.ShapeDtypeStruct((N_DEV - 1,) + sd.shape[1:], sd.dtype) for sd in sends]
    scratch = [pltpu.VMEM((tm, tn), F32)] if nk > 1 else []
    if nw:
        scratch += [pltpu.SemaphoreType.DMA((nw, N_DEV - 1)), pltpu.SemaphoreType.DMA((nw, N_DEV - 1))]
    sem = ("arbitrary",) * 3 if nw else ("parallel", "parallel", "arbitrary")
    outs = pl.pallas_call(
        body, name=name, grid=grid, in_specs=in_specs, out_specs=[o_spec] + [_ANY] * nw,
        out_shape=out_shape, scratch_shapes=scratch, compiler_params=_params(*sem),
    )(*ins)
    return outs if nw else outs[0]


def _rms_fwd(name, x, g, tr=512):
    S, n = x.shape
    tr = _tile(S, tr)

    def body(x_ref, g_ref, h_ref):
        xhat, _ = _rms_hat(x_ref[...], n)
        h_ref[...] = (xhat * g_ref[...]).astype(BF16)

    return pl.pallas_call(
        body, name=name, grid=(S // tr,),
        in_specs=[_rows(tr, n), _full((1, n))], out_specs=_rows(tr, n),
        out_shape=jax.ShapeDtypeStruct((S, n), BF16),
        compiler_params=_params("parallel"),
    )(x, g)


def _rms_bwd_call(name, x, g, dy, extra, want_bf16, tr=256):
    S, n = x.shape
    tr = _tile(S, tr)

    def body(x_ref, g_ref, dy_ref, ex_ref, *outs):
        xhat, r = _rms_hat(x_ref[...], n)
        dx, dg = _rms_bwd(dy_ref[...], xhat, r, g_ref[...], n)
        dx = dx + ex_ref[...]
        outs[0][...] = dx
        if want_bf16:
            outs[1][...] = dx.astype(BF16)
        dg_ref = outs[-1]

        @pl.when(pl.program_id(0) == 0)
        def _():
            dg_ref[...] = jnp.zeros_like(dg_ref)

        dg_ref[...] += dg

    out_shape = [jax.ShapeDtypeStruct((S, n), F32)]
    out_specs = [_rows(tr, n)]
    if want_bf16:
        out_shape.append(jax.ShapeDtypeStruct((S, n), BF16))
        out_specs.append(_rows(tr, n))
    out_shape.append(jax.ShapeDtypeStruct((1, n), F32))
    out_specs.append(_full((1, n)))
    return pl.pallas_call(
        body, name=name, grid=(S // tr,),
        in_specs=[_rows(tr, n), _full((1, n)), _rows(tr, n), _rows(tr, n)],
        out_specs=out_specs, out_shape=out_shape,
        compiler_params=_params("arbitrary"),
    )(x, g, dy, extra)


def _lat_fwd(z, gq, gkv, tr=512):
    S = z.shape[0]
    tr = _tile(S, tr)

    def body(zq_ref, zkv_ref, gq_ref, gkv_ref, ql_ref, kvl_ref):
        xq, _ = _rms_hat(zq_ref[...], Q_LORA)
        ql_ref[...] = (xq * gq_ref[...]).astype(BF16)
        xkv, _ = _rms_hat(zkv_ref[...], KV_LORA)
        kvl_ref[...] = (xkv * gkv_ref[...]).astype(BF16)

    return pl.pallas_call(
        body, name="lat_fwd", grid=(S // tr,),
        in_specs=[_rows(tr, Q_LORA, Z_Q // Q_LORA), _rows(tr, KV_LORA, Z_KV // KV_LORA),
                  _full((1, Q_LORA)), _full((1, KV_LORA))],
        out_specs=[_rows(tr, Q_LORA), _rows(tr, KV_LORA)],
        out_shape=[jax.ShapeDtypeStruct((S, Q_LORA), BF16), jax.ShapeDtypeStruct((S, KV_LORA), BF16)],
        compiler_params=_params("parallel"),
    )(z, z, gq, gkv)


def _lat_bwd(z, gq, gkv, dql, dkvl, tr=512):
    S = z.shape[0]
    tr = _tile(S, tr)

    def body(zq_ref, zkv_ref, gq_ref, gkv_ref, dql_ref, dkvl_ref, dzq_ref, dzkv_ref, dgq_ref, dgkv_ref):
        @pl.when(pl.program_id(0) == 0)
        def _():
            dgq_ref[...] = jnp.zeros_like(dgq_ref)
            dgkv_ref[...] = jnp.zeros_like(dgkv_ref)

        xq, rq = _rms_hat(zq_ref[...], Q_LORA)
        dx, dg = _rms_bwd(dql_ref[...], xq, rq, gq_ref[...], Q_LORA)
        dzq_ref[...] = dx.astype(BF16)
        dgq_ref[...] += dg
        xkv, rkv = _rms_hat(zkv_ref[...], KV_LORA)
        dx, dg = _rms_bwd(dkvl_ref[...], xkv, rkv, gkv_ref[...], KV_LORA)
        dzkv_ref[...] = dx.astype(BF16)
        dgkv_ref[...] += dg

    return pl.pallas_call(
        body, name="lat_bwd", grid=(S // tr,),
        in_specs=[_rows(tr, Q_LORA, Z_Q // Q_LORA), _rows(tr, KV_LORA, Z_KV // KV_LORA),
                  _full((1, Q_LORA)), _full((1, KV_LORA)), _rows(tr, Q_LORA), _rows(tr, KV_LORA)],
        out_specs=[_rows(tr, Q_LORA), _rows(tr, KV_LORA), _full((1, Q_LORA)), _full((1, KV_LORA))],
        out_shape=[jax.ShapeDtypeStruct((S, Q_LORA), BF16), jax.ShapeDtypeStruct((S, KV_LORA), BF16),
                   jax.ShapeDtypeStruct((1, Q_LORA), F32), jax.ShapeDtypeStruct((1, KV_LORA), F32)],
        compiler_params=_params("arbitrary"),
    )(z, z, gq, gkv, dql, dkvl)


def _pool_d(ext, cur, t0, tr):
    t = t0 + lax.broadcasted_iota(jnp.int32, (tr, 1), 0)
    ds = []
    for gi, w in enumerate(POOL_WINDOWS):
        s = ext[:, gi * LANES:(gi + 1) * LANES]
        sh = 1
        while sh < w:
            s = s + pltpu.roll(s, sh, 0)
            sh *= 2
        cnt = jnp.minimum(t + 1, w).astype(F32)
        ds.append(s[POOL_HALO:] / cnt - cur[:, gi * LANES:(gi + 1) * LANES])
    return ds


def _pool_fwd(z, wp, scale, tr=512):
    S = z.shape[0]
    tr = _tile(S, tr)
    rb = tr // POOL_HALO

    def body(z_ref, zp_ref, wp_ref, sc_ref, y_ref):
        i = pl.program_id(0)
        cur = z_ref[...]
        halo = jnp.where(i == 0, 0.0, zp_ref[...])
        ds = _pool_d(jnp.concatenate([halo, cur], axis=0), cur, i * tr, tr)
        for gi in range(len(POOL_WINDOWS)):
            y = jnp.dot(ds[gi].astype(BF16), wp_ref[gi], preferred_element_type=F32)
            y_ref[:, gi * LANES:(gi + 1) * LANES] = (y * sc_ref[:, gi * LANES:(gi + 1) * LANES]).astype(BF16)

    return pl.pallas_call(
        body, name="pool_fwd", grid=(S // tr,),
        in_specs=[_rows(tr, POOL_WIDTH, Z_POOL // POOL_WIDTH),
                  pl.BlockSpec((POOL_HALO, POOL_WIDTH), lambda i: (jnp.maximum(i * rb - 1, 0), 0)),
                  _full(wp.shape), _full((1, POOL_WIDTH))],
        out_specs=_rows(tr, POOL_WIDTH),
        out_shape=jax.ShapeDtypeStruct((S, POOL_WIDTH), BF16),
        compiler_params=_params("parallel"),
    )(z, z, wp, scale)


def _pool_bwd(z, dyc, wp, scale, dy_cb, tr=512):
    S = z.shape[0]
    tr = _tile(S, tr)
    rb = tr // POOL_HALO
    nhb = S // POOL_HALO
    ng = len(POOL_WINDOWS)

    def body(z_ref, zp_ref, dy_ref, dyn_ref, wp_ref, sc_ref, dz_ref, dwp_ref, dsc_ref):
        i = pl.program_id(0)

        @pl.when(i == 0)
        def _():
            dwp_ref[...] = jnp.zeros_like(dwp_ref)
            dsc_ref[...] = jnp.zeros_like(dsc_ref)

        cur = z_ref[...]
        halo = jnp.where(i == 0, 0.0, zp_ref[...])
        ds = _pool_d(jnp.concatenate([halo, cur], axis=0), cur, i * tr, tr)
        dy_cur = dy_ref[...]
        dy_next = jnp.where(i == pl.num_programs(0) - 1, 0.0, dyn_ref[...])
        dy_ext = jnp.concatenate([dy_cur, dy_next], axis=0)
        n_ext = tr + POOL_HALO
        t_ext = i * tr + lax.broadcasted_iota(jnp.int32, (n_ext, 1), 0)
        for gi, w in enumerate(POOL_WINDOWS):
            cols = slice(gi * LANES, (gi + 1) * LANES)
            d_b = ds[gi].astype(BF16)
            y_pre = jnp.dot(d_b, wp_ref[gi], preferred_element_type=F32)
            dsc_ref[:, cols] += jnp.sum(dy_cur[:, cols] * y_pre, axis=0, keepdims=True)
            dys = (dy_ext[:, cols] * sc_ref[:, cols]).astype(BF16)
            dwp_ref[gi] += lax.dot_general(d_b, dys[:tr], (((0,), (0,)), ((), ())), preferred_element_type=F32)
            dd = lax.dot_general(dys, wp_ref[gi], (((1,), (1,)), ((), ())), preferred_element_type=F32)
            s = dd / jnp.minimum(t_ext + 1, w).astype(F32)
            sh = 1
            while sh < w:
                s = s + pltpu.roll(s, n_ext - sh, 0)
                sh *= 2
            dz_ref[:, cols] = (s[:tr] - dd[:tr]).astype(BF16)

    return pl.pallas_call(
        body, name="pool_bwd", grid=(S // tr,),
        in_specs=[_rows(tr, POOL_WIDTH, Z_POOL // POOL_WIDTH),
                  pl.BlockSpec((POOL_HALO, POOL_WIDTH), lambda i: (jnp.maximum(i * rb - 1, 0), 0)),
                  _rows(tr, POOL_WIDTH, dy_cb),
                  pl.BlockSpec((POOL_HALO, POOL_WIDTH), lambda i: (jnp.minimum((i + 1) * rb, nhb - 1), dy_cb)),
                  _full(wp.shape), _full((1, POOL_WIDTH))],
        out_specs=[_rows(tr, POOL_WIDTH), _full((ng, LANES, LANES)), _full((1, POOL_WIDTH))],
        out_shape=[jax.ShapeDtypeStruct((S, POOL_WIDTH), BF16), jax.ShapeDtypeStruct((ng, LANES, LANES), F32),
                   jax.ShapeDtypeStruct((1, POOL_WIDTH), F32)],
        compiler_params=_params("arbitrary"),
    )(z, z, dyc, dyc, wp, scale)


def _rope_tables(S):
    half = MLA_ROPE // 2
    inv_freq = 1.0 / (ROPE_THETA ** (jnp.arange(half, dtype=F32) / half))
    ang = jnp.arange(S).astype(F32)[:, None] * inv_freq[None, :]
    cos, sin = jnp.cos(ang), jnp.sin(ang)
    z32 = jnp.zeros((S, half), F32)
    z64 = jnp.zeros((S, LANES - MLA_ROPE), F32)
    cos_t = jnp.concatenate([cos, cos, z64], axis=1)
    sin_a = jnp.concatenate([-sin, z32, z64], axis=1)
    sin_b = jnp.concatenate([z32, sin, z64], axis=1)
    return cos_t, sin_a, sin_b


def _rope(x, cos_t, sin_a, sin_b):
    return x * cos_t + pltpu.roll(x, LANES - 32, 1) * sin_a + pltpu.roll(x, 32, 1) * sin_b


def _rope_t(d, cos_t, sin_a, sin_b):
    return d * cos_t + pltpu.roll(d * sin_a, 32, 1) + pltpu.roll(d * sin_b, LANES - 32, 1)


def _mla_prep(q_raw, kv_raw, z, tabs, gq, gk, tr=256):
    S = z.shape[0]
    tr = _tile(S, tr)
    scale = 1.0 / math.sqrt(MLA_QK)
    W = MLA_HEADS * MLA_QK_PAD

    def body(q_ref, kv_ref, kr_ref, c_ref, sa_ref, sb_ref, gq_ref, gk_ref, qo_ref, ko_ref, vo_ref):
        tab = (c_ref[...], sa_ref[...], sb_ref[...])
        kr = kr_ref[...]
        kr_ss = jnp.sum(kr * kr, axis=-1, keepdims=True)
        gqn, gqr = gq_ref[:, :LANES], gq_ref[:, LANES:]
        gkn, gkr = gk_ref[:, :LANES], gk_ref[:, LANES:]
        for h in range(MLA_HEADS):
            c0 = h * MLA_QK_PAD
            qn = q_ref[:, c0:c0 + LANES]
            qr = q_ref[:, c0 + LANES:c0 + 2 * LANES]
            r = lax.rsqrt((jnp.sum(qn * qn, -1, keepdims=True) + jnp.sum(qr * qr, -1, keepdims=True)) / MLA_QK
                          + NORM_EPS)
            qo_ref[h, :, :LANES] = (qn * r * gqn * scale).astype(BF16)
            qo_ref[h, :, LANES:] = (_rope(qr * r * gqr, *tab) * scale).astype(BF16)
            kn = kv_ref[:, c0:c0 + LANES]
            r = lax.rsqrt((jnp.sum(kn * kn, -1, keepdims=True) + kr_ss) / MLA_QK + NORM_EPS)
            ko_ref[h, :, :LANES] = (kn * r * gkn).astype(BF16)
            ko_ref[h, :, LANES:] = _rope(kr * r * gkr, *tab).astype(BF16)
            vo_ref[h] = kv_ref[:, c0 + LANES:c0 + 2 * LANES].astype(BF16)

    hs = lambda w: pl.BlockSpec((MLA_HEADS, tr, w), lambda i: (0, i, 0))
    return pl.pallas_call(
        body, name="mla_prep", grid=(S // tr,),
        in_specs=[_rows(tr, W), _rows(tr, W), _rows(tr, LANES, Z_KR // LANES),
                  _rows(tr, LANES), _rows(tr, LANES), _rows(tr, LANES),
                  _full((1, MLA_QK_PAD)), _full((1, MLA_QK_PAD))],
        out_specs=[hs(MLA_QK_PAD), hs(MLA_QK_PAD), hs(MLA_V)],
        out_shape=[jax.ShapeDtypeStruct((MLA_HEADS, S, MLA_QK_PAD), BF16),
                   jax.ShapeDtypeStruct((MLA_HEADS, S, MLA_QK_PAD), BF16),
                   jax.ShapeDtypeStruct((MLA_HEADS, S, MLA_V), BF16)],
        compiler_params=_params("parallel"),
    )(q_raw, kv_raw, z, *tabs, gq, gk)


def _mla_prep_bwd(dq, dk, dv, q_raw, kv_raw, z, tabs, gq, gk, tr=256):
    S = z.shape[0]
    tr = _tile(S, tr)
    scale = 1.0 / math.sqrt(MLA_QK)
    W = MLA_HEADS * MLA_QK_PAD

    def body(dq_ref, dk_ref, dv_ref, q_ref, kv_ref, kr_ref, c_ref, sa_ref, sb_ref, gq_ref, gk_ref,
             dqr_ref, dkvr_ref, dkr_ref, dgq_ref, dgk_ref):
        @pl.when(pl.program_id(0) == 0)
        def _():
            dgq_ref[...] = jnp.zeros_like(dgq_ref)
            dgk_ref[...] = jnp.zeros_like(dgk_ref)

        tab = (c_ref[...], sa_ref[...], sb_ref[...])
        kr = kr_ref[...]
        kr_ss = jnp.sum(kr * kr, axis=-1, keepdims=True)
        gqn, gqr = gq_ref[:, :LANES], gq_ref[:, LANES:]
        gkn, gkr = gk_ref[:, :LANES], gk_ref[:, LANES:]
        dkr_sum = jnp.zeros((tr, LANES), F32)
        dgq_n = jnp.zeros((1, LANES), F32)
        dgq_r = jnp.zeros((1, LANES), F32)
        dgk_n = jnp.zeros((1, LANES), F32)
        dgk_r = jnp.zeros((1, LANES), F32)

        def head_norm_bwd(xn, xr, r, dyn, dyr, gn, gr):
            hn, hr = xn * r, xr * r
            dxn, dxr = dyn * gn, dyr * gr
            mt = (jnp.sum(dxn * hn, -1, keepdims=True) + jnp.sum(dxr * hr, -1, keepdims=True)) / MLA_QK
            return (r * (dxn - hn * mt), r * (dxr - hr * mt),
                    jnp.sum(dyn * hn, axis=0, keepdims=True), jnp.sum(dyr * hr, axis=0, keepdims=True))

        for h in range(MLA_HEADS):
            c0 = h * MLA_QK_PAD
            qn = q_ref[:, c0:c0 + LANES]
            qr = q_ref[:, c0 + LANES:c0 + 2 * LANES]
            r = lax.rsqrt((jnp.sum(qn * qn, -1, keepdims=True) + jnp.sum(qr * qr, -1, keepdims=True)) / MLA_QK
                          + NORM_EPS)
            dyn = dq_ref[h, :, :LANES] * scale
            dyr = _rope_t(dq_ref[h, :, LANES:] * scale, *tab)
            dxn, dxr, gn_, gr_ = head_norm_bwd(qn, qr, r, dyn, dyr, gqn, gqr)
            dgq_n += gn_
            dgq_r += gr_
            dqr_ref[:, c0:c0 + LANES] = dxn.astype(BF16)
            dqr_ref[:, c0 + LANES:c0 + 2 * LANES] = dxr.astype(BF16)

            kn = kv_ref[:, c0:c0 + LANES]
            r = lax.rsqrt((jnp.sum(kn * kn, -1, keepdims=True) + kr_ss) / MLA_QK + NORM_EPS)
            dyn = dk_ref[h, :, :LANES]
            dyr = _rope_t(dk_ref[h, :, LANES:], *tab)
            dxn, dxr, gn_, gr_ = head_norm_bwd(kn, kr, r, dyn, dyr, gkn, gkr)
            dgk_n += gn_
            dgk_r += gr_
            dkr_sum += dxr
            dkvr_ref[:, c0:c0 + LANES] = dxn.astype(BF16)
            dkvr_ref[:, c0 + LANES:c0 + 2 * LANES] = dv_ref[h].astype(BF16)

        dkr_ref[...] = dkr_sum.astype(BF16)
        dgq_ref[:, :LANES] += dgq_n
        dgq_ref[:, LANES:] += dgq_r
        dgk_ref[:, :LANES] += dgk_n
        dgk_ref[:, LANES:] += dgk_r

    hs = lambda w: pl.BlockSpec((MLA_HEADS, tr, w), lambda i: (0, i, 0))
    return pl.pallas_call(
        body, name="mla_prep_bwd", grid=(S // tr,),
        in_specs=[hs(MLA_QK_PAD), hs(MLA_QK_PAD), hs(MLA_V), _rows(tr, W), _rows(tr, W),
                  _rows(tr, LANES, Z_KR // LANES), _rows(tr, LANES), _rows(tr, LANES), _rows(tr, LANES),
                  _full((1, MLA_QK_PAD)), _full((1, MLA_QK_PAD))],
        out_specs=[_rows(tr, W), _rows(tr, W), _rows(tr, LANES), _full((1, MLA_QK_PAD)), _full((1, MLA_QK_PAD))],
        out_shape=[jax.ShapeDtypeStruct((S, W), BF16), jax.ShapeDtypeStruct((S, W), BF16),
                   jax.ShapeDtypeStruct((S, LANES), BF16),
                   jax.ShapeDtypeStruct((1, MLA_QK_PAD), F32), jax.ShapeDtypeStruct((1, MLA_QK_PAD), F32)],
        compiler_params=_params("arbitrary"),
    )(dq, dk, dv, q_raw, kv_raw, z, *tabs, gq, gk)


_NT = (((1,), (1,)), ((), ()))
_TN = (((0,), (0,)), ((), ()))


def _causal_mask(s, i, j, tq, tk):
    row = i * tq + lax.broadcasted_iota(jnp.int32, (tq, tk), 0)
    col = j * tk + lax.broadcasted_iota(jnp.int32, (tq, tk), 1)
    return jnp.where(col <= row, s, -jnp.inf)


def _flash_fwd(q, k, v, shards, kinds, tq=1024, tk=1024):
    H, S, dq = q.shape
    dv = v.shape[-1]
    tq, tk = _tile(S, tq), _tile(S, tk)
    assert tq % tk == 0 and dv == LANES
    nq, nd = S // tq, tq // tk
    nw = len(shards)

    def body(q_ref, k_ref, v_ref, *rest):
        x_refs, (o_ref, lse_ref), g_refs = rest[:nw], rest[nw:nw + 2], rest[nw + 2:2 * nw + 2]
        m_sc, l_sc, acc_sc, send_sems, recv_sems, local_sems = rest[2 * nw + 2:]
        h, i = pl.program_id(0), pl.program_id(1)
        gather = _Gather(x_refs, g_refs, kinds, send_sems, recv_sems, local_sems)
        pl.when(jnp.logical_and(h == 0, i == 0))(gather.start)
        pl.when(jnp.logical_and(h == (3 * H) // 4, i == 0))(gather.forward)

        m_sc[...] = jnp.full_like(m_sc, -jnp.inf)
        l_sc[...] = jnp.zeros_like(l_sc)
        acc_sc[...] = jnp.zeros_like(acc_sc)

        def step(j, masked):
            off = pl.multiple_of(j * tk, tk)
            s = lax.dot_general(q_ref[...], k_ref[pl.ds(off, tk), :], _NT, preferred_element_type=F32)
            if masked:
                s = _causal_mask(s, i, j, tq, tk)
            parts = [s[:, c * LANES:(c + 1) * LANES] for c in range(tk // LANES)]
            m_cur = parts[0]
            for pt in parts[1:]:
                m_cur = jnp.maximum(m_cur, pt)
            m_prev = m_sc[...]
            m_new = jnp.maximum(m_prev, jnp.max(m_cur, axis=-1, keepdims=True))
            alpha = jnp.exp(m_prev - m_new)
            ps = [jnp.exp(pt - m_new) for pt in parts]
            l_new = alpha * l_sc[...]
            for pc in ps:
                l_new = l_new + pc
            l_sc[...] = l_new
            p = jnp.concatenate(ps, axis=1).astype(BF16)
            acc_sc[...] = alpha * acc_sc[...] + jnp.dot(p, v_ref[pl.ds(off, tk), :], preferred_element_type=F32)
            m_sc[...] = m_new

        n_full = i * nd

        def full_step(j, carry):
            step(j, False)
            return carry

        lax.fori_loop(0, n_full, full_step, 0)
        for d in range(nd):
            step(n_full + d, True)
        l = jnp.sum(l_sc[...], axis=-1, keepdims=True)
        o_ref[...] = (acc_sc[...] / l).astype(BF16)
        lse_ref[...] = m_sc[...] + jnp.log(l)
        pl.when(jnp.logical_and(h == H - 1, i == nq - 1))(gather.finish)

    return pl.pallas_call(
        body, name="flash_fwd", grid=(H, nq),
        in_specs=[pl.BlockSpec((None, tq, dq), lambda h, i: (h, i, 0)),
                  pl.BlockSpec((None, S, dq), lambda h, i: (h, 0, 0)),
                  pl.BlockSpec((None, S, dv), lambda h, i: (h, 0, 0))] + [_ANY] * nw,
        out_specs=[pl.BlockSpec((tq, dv), lambda h, i: (i, h)),
                   pl.BlockSpec((None, tq, LANES), lambda h, i: (h, i, 0))] + [_ANY] * nw,
        out_shape=[jax.ShapeDtypeStruct((S, H * dv), BF16), jax.ShapeDtypeStruct((H, S, LANES), F32)]
        + [jax.ShapeDtypeStruct(_gather_out_shape(kd, sd.shape), sd.dtype) for kd, sd in zip(kinds, shards)],
        scratch_shapes=[pltpu.VMEM((tq, LANES), F32), pltpu.VMEM((tq, LANES), F32), pltpu.VMEM((tq, dv), F32),
                        pltpu.SemaphoreType.DMA((nw, 7)), pltpu.SemaphoreType.DMA((nw, 7)),
                        pltpu.SemaphoreType.DMA((nw,))],
        compiler_params=_params("arbitrary", "arbitrary"),
    )(q, k, v, *shards)


def _flash_bwd(q, k, v, dyc, cb0, o, lse, sends, tq=1024, tk=512):
    H, S, dq = q.shape
    dv = v.shape[-1]
    tq, tk = _tile(S, tq), _tile(S, tk)
    assert tq % tk == 0
    nq, nd = S // tq, tq // tk
    nw = len(sends)

    def body(q_ref, k_ref, v_ref, dy_ref, o_ref, lse_ref, *rest):
        s_refs, (dq_ref, dk_ref, dv_ref), r_refs = rest[:nw], rest[nw:nw + 3], rest[nw + 3:2 * nw + 3]
        do_sc, dl_sc, send_sems, recv_sems = rest[2 * nw + 3:]
        h, i = pl.program_id(0), pl.program_id(1)
        scatter = _Scatter(s_refs, r_refs, send_sems, recv_sems)
        pl.when(jnp.logical_and(h == 0, i == 0))(scatter.start)

        @pl.when(i == 0)
        def _():
            dk_ref[...] = jnp.zeros_like(dk_ref)
            dv_ref[...] = jnp.zeros_like(dv_ref)

        dq_ref[...] = jnp.zeros_like(dq_ref)
        dy = dy_ref[...]
        do_sc[...] = dy.astype(BF16)
        dl_sc[...] = jnp.broadcast_to(jnp.sum(dy * o_ref[...].astype(F32), axis=-1, keepdims=True), (tq, LANES))

        def step(j, masked):
            off = pl.multiple_of(j * tk, tk)
            qv = q_ref[...]
            dov = do_sc[...]
            kb = k_ref[pl.ds(off, tk), :]
            s = lax.dot_general(qv, kb, _NT, preferred_element_type=F32)
            if masked:
                s = _causal_mask(s, i, j, tq, tk)
            p = jnp.exp(s - lse_ref[:, :1])
            dv_ref[pl.ds(off, tk), :] += lax.dot_general(p.astype(BF16), dov, _TN, preferred_element_type=F32)
            dp = lax.dot_general(dov, v_ref[pl.ds(off, tk), :], _NT, preferred_element_type=F32)
            ds = (p * (dp - dl_sc[:, :1])).astype(BF16)
            dk_ref[pl.ds(off, tk), :] += lax.dot_general(ds, qv, _TN, preferred_element_type=F32)
            dq_ref[...] += jnp.dot(ds, kb, preferred_element_type=F32)

        n_full = i * nd

        def full_step(j, carry):
            step(j, False)
            return carry

        lax.fori_loop(0, n_full, full_step, 0)
        for d in range(nd):
            step(n_full + d, True)
        pl.when(jnp.logical_and(h == H - 1, i == nq - 1))(scatter.finish)

    return pl.pallas_call(
        body, name="flash_bwd", grid=(H, nq),
        in_specs=[pl.BlockSpec((None, tq, dq), lambda h, i: (h, i, 0)),
                  pl.BlockSpec((None, S, dq), lambda h, i: (h, 0, 0)),
                  pl.BlockSpec((None, S, dv), lambda h, i: (h, 0, 0)),
                  pl.BlockSpec((tq, dv), lambda h, i: (i, h + cb0)),
                  pl.BlockSpec((tq, dv), lambda h, i: (i, h)),
                  pl.BlockSpec((None, tq, LANES), lambda h, i: (h, i, 0))] + [_ANY] * nw,
        out_specs=[pl.BlockSpec((None, tq, dq), lambda h, i: (h, i, 0)),
                   pl.BlockSpec((None, S, dq), lambda h, i: (h, 0, 0)),
                   pl.BlockSpec((None, S, dv), lambda h, i: (h, 0, 0))] + [_ANY] * nw,
        out_shape=[jax.ShapeDtypeStruct((H, S, dq), F32), jax.ShapeDtypeStruct((H, S, dq), F32),
                   jax.ShapeDtypeStruct((H, S, dv), F32)]
        + [jax.ShapeDtypeStruct((N_DEV - 1,) + sd.shape[1:], sd.dtype) for sd in sends],
        scratch_shapes=[pltpu.VMEM((tq, dv), BF16), pltpu.VMEM((tq, LANES), F32),
                        pltpu.SemaphoreType.DMA((nw, N_DEV - 1)), pltpu.SemaphoreType.DMA((nw, N_DEV - 1))],
        compiler_params=_params("arbitrary", "arbitrary"),
    )(q, k, v, dyc, o, lse, *sends)


def _mem_norm(mem, g):
    n = mem.shape[1]

    def body(m_ref, g_ref, o_ref):
        xhat, _ = _rms_hat(m_ref[...], n)
        o_ref[...] = (xhat * g_ref[...]).astype(BF16)

    return pl.pallas_call(body, name="mem_norm", out_shape=jax.ShapeDtypeStruct(mem.shape, BF16),
                          compiler_params=pltpu.CompilerParams(vmem_limit_bytes=VMEM_LIMIT))(mem, g)


def _memkv_prep(mkv, gk):
    M = mkv.shape[0]

    def body(mkv_ref, g_ref, k_ref, v_ref):
        for h in range(X_HEADS):
            cols = slice(h * X_HEAD_DIM, (h + 1) * X_HEAD_DIM)
            xhat, _ = _rms_hat(mkv_ref[:, cols], X_HEAD_DIM)
            k_ref[:, cols] = (xhat * g_ref[...]).astype(BF16)
        v_ref[...] = mkv_ref[:, X_WIDTH:].astype(BF16)

    return pl.pallas_call(
        body, name="memkv_prep",
        out_shape=[jax.ShapeDtypeStruct((M, X_WIDTH), BF16), jax.ShapeDtypeStruct((M, X_WIDTH), BF16)],
    )(mkv, gk)


def _memkv_bwd(dk, dv, mkv, gk):
    M = mkv.shape[0]

    def body(dk_ref, dv_ref, mkv_ref, g_ref, dmkv_ref, dg_ref):
        dg = jnp.zeros((1, X_HEAD_DIM), F32)
        for h in range(X_HEADS):
            cols = slice(h * X_HEAD_DIM, (h + 1) * X_HEAD_DIM)
            xhat, r = _rms_hat(mkv_ref[:, cols], X_HEAD_DIM)
            dx, dgh = _rms_bwd(dk_ref[:, cols], xhat, r, g_ref[...], X_HEAD_DIM)
            dmkv_ref[:, cols] = dx.astype(BF16)
            dg += dgh
        dmkv_ref[:, X_WIDTH:] = dv_ref[...].astype(BF16)
        dg_ref[...] = dg

    return pl.pallas_call(
        body, name="memkv_bwd",
        out_shape=[jax.ShapeDtypeStruct((M, 2 * X_WIDTH), BF16), jax.ShapeDtypeStruct((1, X_HEAD_DIM), F32)],
    )(dk, dv, mkv, gk)


def _mem_gain_bwd(mem, dmn):
    n = mem.shape[1]

    def body(m_ref, d_ref, dg_ref):
        xhat, _ = _rms_hat(m_ref[...], n)
        dg_ref[...] = jnp.sum(d_ref[...] * xhat, axis=0, keepdims=True)

    return pl.pallas_call(body, name="mem_gain_bwd", out_shape=jax.ShapeDtypeStruct((1, n), F32),
                          compiler_params=pltpu.CompilerParams(vmem_limit_bytes=VMEM_LIMIT))(mem, dmn)


def _memx_scores(zq_ref, g, kx_ref, h):
    cols = slice(h * X_HEAD_DIM, (h + 1) * X_HEAD_DIM)
    xhat, r = _rms_hat(zq_ref[:, cols], X_HEAD_DIM)
    qn = (xhat * g).astype(BF16)
    s = lax.dot_general(qn, kx_ref[:, cols], _NT, preferred_element_type=F32) * (1.0 / math.sqrt(X_HEAD_DIM))
    p = jnp.exp(s - jnp.max(s, axis=-1, keepdims=True))
    return cols, xhat, r, qn, p, jnp.sum(p, axis=-1, keepdims=True)


def _memx_fwd(z, g, kx, vx, tr=512):
    S = z.shape[0]
    tr = _tile(S, tr)
    M = kx.shape[0]

    def body(zq_ref, g_ref, kx_ref, vx_ref, y_ref):
        for h in range(X_HEADS):
            cols, _, _, _, p, l = _memx_scores(zq_ref, g_ref[...], kx_ref, h)
            o = jnp.dot(p.astype(BF16), vx_ref[:, cols], preferred_element_type=F32)
            y_ref[:, cols] = (o / l).astype(BF16)

    return pl.pallas_call(
        body, name="memx_fwd", grid=(S // tr,),
        in_specs=[_rows(tr, X_WIDTH, Z_MQ // X_WIDTH), _full((1, X_HEAD_DIM)), _full((M, X_WIDTH)),
                  _full((M, X_WIDTH))],
        out_specs=_rows(tr, X_WIDTH),
        out_shape=jax.ShapeDtypeStruct((S, X_WIDTH), BF16),
        compiler_params=_params("parallel"),
    )(z, g, kx, vx)


def _memx_bwd(z, g, kx, vx, dyc, dy_cb, tr=512):
    S = z.shape[0]
    tr = _tile(S, tr)
    M = kx.shape[0]
    scale = 1.0 / math.sqrt(X_HEAD_DIM)

    def body(zq_ref, g_ref, kx_ref, vx_ref, dy_ref, dz_ref, dk_ref, dv_ref, dg_ref):
        @pl.when(pl.program_id(0) == 0)
        def _():
            dk_ref[...] = jnp.zeros_like(dk_ref)
            dv_ref[...] = jnp.zeros_like(dv_ref)
            dg_ref[...] = jnp.zeros_like(dg_ref)

        gv = g_ref[...]
        for h in range(X_HEADS):
            cols, xhat, r, qn, p, l = _memx_scores(zq_ref, gv, kx_ref, h)
            p = p / l
            do = dy_ref[:, cols].astype(BF16)
            dv_ref[:, cols] += lax.dot_general(p.astype(BF16), do, _TN, preferred_element_type=F32)
            dp = lax.dot_general(do, vx_ref[:, cols], _NT, preferred_element_type=F32)
            ds = (p * (dp - jnp.sum(dp * p, axis=-1, keepdims=True)) * scale).astype(BF16)
            dqn = jnp.dot(ds, kx_ref[:, cols], preferred_element_type=F32)
            dk_ref[:, cols] += lax.dot_general(ds, qn, _TN, preferred_element_type=F32)
            dx, dg = _rms_bwd(dqn, xhat, r, gv, X_HEAD_DIM)
            dz_ref[:, cols] = dx.astype(BF16)
            dg_ref[...] += dg

    return pl.pallas_call(
        body, name="memx_bwd", grid=(S // tr,),
        in_specs=[_rows(tr, X_WIDTH, Z_MQ // X_WIDTH), _full((1, X_HEAD_DIM)), _full((M, X_WIDTH)),
                  _full((M, X_WIDTH)), _rows(tr, X_WIDTH, dy_cb)],
        out_specs=[_rows(tr, X_WIDTH), _full((M, X_WIDTH)), _full((M, X_WIDTH)), _full((1, X_HEAD_DIM))],
        out_shape=[jax.ShapeDtypeStruct((S, X_WIDTH), BF16), jax.ShapeDtypeStruct((M, X_WIDTH), F32),
                   jax.ShapeDtypeStruct((M, X_WIDTH), F32), jax.ShapeDtypeStruct((1, X_HEAD_DIM), F32)],
        compiler_params=_params("arbitrary"),
    )(z, g, kx, vx, dyc)


def _conv_gc(g_ext, w_ref, b_ref, n_ext):
    g1 = pltpu.roll(g_ext, 1, 0)
    g2 = pltpu.roll(g_ext, 2, 0)
    gc = b_ref[...] + w_ref[0:1, :] * g2
    gc = gc + w_ref[1:2, :] * g1
    gc = gc + w_ref[2:3, :] * g_ext
    return gc, g1, g2


def _convglu_fwd(g, u, cw, cb, tr=512, tc=512):
    S, F = g.shape
    tr, tc = _tile(S, tr), _tile(F, tc)
    rb = tr // CONV_HALO

    def body(g_ref, gp_ref, u_ref, w_ref, b_ref, a_ref):
        i = pl.program_id(1)
        halo = jnp.where(i == 0, 0.0, gp_ref[...].astype(F32))
        g_ext = jnp.concatenate([halo, g_ref[...].astype(F32)], axis=0)
        gc, _, _ = _conv_gc(g_ext, w_ref, b_ref, tr + CONV_HALO)
        gc = gc[CONV_HALO:]
        sig = 1.0 / (1.0 + jnp.exp(-gc))
        a_ref[...] = (gc * sig * u_ref[...].astype(F32)).astype(BF16)

    return pl.pallas_call(
        body, name="convglu_fwd", grid=(F // tc, S // tr),
        in_specs=[pl.BlockSpec((tr, tc), lambda j, i: (i, j)),
                  pl.BlockSpec((CONV_HALO, tc), lambda j, i: (jnp.maximum(i * rb - 1, 0), j)),
                  pl.BlockSpec((tr, tc), lambda j, i: (i, j)),
                  pl.BlockSpec((3, tc), lambda j, i: (0, j)), pl.BlockSpec((1, tc), lambda j, i: (0, j))],
        out_specs=pl.BlockSpec((tr, tc), lambda j, i: (i, j)),
        out_shape=jax.ShapeDtypeStruct((S, F), BF16),
        compiler_params=_params("parallel", "parallel"),
    )(g, g, u, cw, cb)


def _convglu_bwd(g, u, da, cw, cb, tr=512, tc=512):
    S, F = g.shape
    tr, tc = _tile(S, tr), _tile(F, tc)
    rb = tr // CONV_HALO
    nhb = S // CONV_HALO
    H = CONV_HALO

    def body(g_ref, gp_ref, gn_ref, u_ref, un_ref, da_ref, dan_ref, w_ref, b_ref,
             dg_ref, du_ref, dw_ref, db_ref):
        i = pl.program_id(1)
        last = i == pl.num_programs(1) - 1

        @pl.when(i == 0)
        def _():
            dw_ref[...] = jnp.zeros_like(dw_ref)
            db_ref[...] = jnp.zeros_like(db_ref)

        g_prev = jnp.where(i == 0, 0.0, gp_ref[...].astype(F32))
        g_cur = g_ref[...].astype(F32)
        g_ext = jnp.concatenate([g_prev, g_cur, gn_ref[...].astype(F32)], axis=0)
        gc, g1, g2 = _conv_gc(g_ext, w_ref, b_ref, tr + 2 * H)
        gc = gc[H:]
        ux = jnp.concatenate([u_ref[...], un_ref[...]], axis=0).astype(F32)
        dax = jnp.concatenate([da_ref[...].astype(F32), jnp.where(last, 0.0, dan_ref[...].astype(F32))], axis=0)
        sig = 1.0 / (1.0 + jnp.exp(-gc))
        du_ref[...] = (dax[:tr] * (gc[:tr] * sig[:tr])).astype(BF16)
        dgc = dax * ux * (sig * (1.0 + gc * (1.0 - sig)))
        n = tr + H
        d1 = pltpu.roll(dgc, n - 1, 0)[:tr]
        d2 = pltpu.roll(dgc, n - 2, 0)[:tr]
        d0 = dgc[:tr]
        dg_ref[...] = (w_ref[2:3, :] * d0 + w_ref[1:2, :] * d1 + w_ref[0:1, :] * d2).astype(BF16)
        db_ref[...] += jnp.sum(d0, axis=0, keepdims=True)
        dw_ref[0:1, :] += jnp.sum(d0 * g2[H:H + tr], axis=0, keepdims=True)
        dw_ref[1:2, :] += jnp.sum(d0 * g1[H:H + tr], axis=0, keepdims=True)
        dw_ref[2:3, :] += jnp.sum(d0 * g_cur, axis=0, keepdims=True)

    cur = pl.BlockSpec((tr, tc), lambda j, i: (i, j))
    prv = pl.BlockSpec((H, tc), lambda j, i: (jnp.maximum(i * rb - 1, 0), j))
    nxt = pl.BlockSpec((H, tc), lambda j, i: (jnp.minimum((i + 1) * rb, nhb - 1), j))
    return pl.pallas_call(
        body, name="convglu_bwd", grid=(F // tc, S // tr),
        in_specs=[cur, prv, nxt, cur, nxt, cur, nxt,
                  pl.BlockSpec((3, tc), lambda j, i: (0, j)), pl.BlockSpec((1, tc), lambda j, i: (0, j))],
        out_specs=[cur, cur, pl.BlockSpec((3, tc), lambda j, i: (0, j)), pl.BlockSpec((1, tc), lambda j, i: (0, j))],
        out_shape=[jax.ShapeDtypeStruct((S, F), BF16), jax.ShapeDtypeStruct((S, F), BF16),
                   jax.ShapeDtypeStruct((3, F), F32), jax.ShapeDtypeStruct((1, F), F32)],
        compiler_params=_params("parallel", "arbitrary"),
    )(g, g, g, u, u, da, da, cw, cb)


def _loss_head(y, target, tr=256):
    S, n = y.shape
    tr = _tile(S, tr)

    def body(y_ref, t_ref, dy_ref, dyb_ref, loss_ref):
        @pl.when(pl.program_id(0) == 0)
        def _():
            loss_ref[...] = jnp.zeros_like(loss_ref)

        err = y_ref[...] - t_ref[...]
        dy = err / n
        dy_ref[...] = dy
        dyb_ref[...] = dy.astype(BF16)
        per_row = jnp.sum(err * err, axis=-1, keepdims=True) / n
        loss_ref[...] += 0.5 * jnp.sum(per_row)

    return pl.pallas_call(
        body, name="loss_head", grid=(S // tr,),
        in_specs=[_rows(tr, n), _rows(tr, n)],
        out_specs=[_rows(tr, n), _rows(tr, n), _full((8, LANES))],
        out_shape=[jax.ShapeDtypeStruct((S, n), F32), jax.ShapeDtypeStruct((S, n), BF16),
                   jax.ShapeDtypeStruct((8, LANES), F32)],
        compiler_params=_params("arbitrary"),
    )(y, target)


def _local_step(x, mem, target, W, late_shards):
    S = x.shape[0]
    tabs = _rope_tables(S)
    W = dict(W)
    G = {}

    h = _rms_fwd("rms1_fwd", x, W["g_mix"])
    z = _mm("mm_in", h, W["w_in"], tm=512, tn=Z_COLS, tk=D_MODEL)
    y_pool = _pool_fwd(z, W["w_pool"], W["pool_scale"])
    ql, kvl = _lat_fwd(z, W["g_q_lat"], W["g_kv_lat"])
    q_raw = _mm("mm_q_up", ql, W["w_q_up"], tm=1024, tn=2048, tk=Q_LORA)
    kv_raw = _mm("mm_kv_up", kvl, W["w_kv_up"], tm=1024, tn=2048, tk=KV_LORA)
    q, k, v = _mla_prep(q_raw, kv_raw, z, tabs, W["g_q_mla"], W["g_k_mla"])
    o_mla, lse, wg3, wu3, W["w_down"], cw3, W["w_o"], W["w_mem_kv"] = _flash_fwd(
        q, k, v, late_shards, LATE_GATHER_KINDS)
    W["w_gate"] = jnp.transpose(wg3, (1, 0, 2)).reshape(D_MODEL, D_FF)
    W["w_up"] = jnp.transpose(wu3, (1, 0, 2)).reshape(D_MODEL, D_FF)
    cw = jnp.sum(cw3.reshape(N_DEV, 3, 3, D_FF // N_DEV).astype(F32), axis=1)
    W["conv_w"] = jnp.transpose(cw, (1, 0, 2)).reshape(3, D_FF)
    mn = _mem_norm(mem, W["g_mem"])
    mkv = _mm("mm_mem_kv", mn, W["w_mem_kv"], tm=256, tn=1024, tk=D_MODEL)
    kx, vx = _memkv_prep(mkv, W["g_k_x"])
    y_mem = _memx_fwd(z, W["g_q_x"], kx, vx)
    ycat = jnp.concatenate([y_pool, o_mla, y_mem], axis=1)
    x1 = _mm("mm_o", ycat, W["w_o"], add=x, tm=512, tn=2048, tk=D_MODEL)
    h2 = _rms_fwd("rms2_fwd", x1, W["g_ffn"])
    g = _mm("mm_gate", h2, W["w_gate"], out_dtype=BF16, tm=1024, tn=512, tk=D_MODEL)
    u = _mm("mm_up", h2, W["w_up"], out_dtype=BF16, tm=1024, tn=512, tk=D_MODEL)
    a = _convglu_fwd(g, u, W["conv_w"], W["conv_b"])
    y = _mm("mm_down", a, W["w_down"], add=x1, tm=1024, tn=512, tk=D_FF)
    dy, dyb, loss_part = _loss_head(y, target)

    da = _mm("mm_down_dx", dyb, W["w_down"], tb=True, out_dtype=BF16, tm=1024, tn=512, tk=D_MODEL)
    G["w_down"] = _mm("mm_down_dw", a, dyb, ta=True, tm=512, tn=2048, tk=2048)
    dg, du, G["conv_w"], G["conv_b"] = _convglu_bwd(g, u, da, W["conv_w"], W["conv_b"])
    dh2 = _mm("mm_gate_dx", dg, W["w_gate"], tb=True, tm=1024, tn=512, tk=D_FF)
    dh2 = _mm("mm_up_dx", du, W["w_up"], tb=True, add=dh2, tm=1024, tn=512, tk=D_FF)
    G["w_gate"] = _mm("mm_gate_dw", h2, dg, ta=True, tm=1024, tn=1408, tk=2048)
    G["w_up"] = _mm("mm_up_dw", h2, du, ta=True, tm=1024, tn=1408, tk=2048)
    dx1, dx1b, G["g_ffn"] = _rms_bwd_call("rms2_bwd", x1, W["g_ffn"], dh2, dy, True)

    dyc = _mm("mm_o_dx", dx1b, W["w_o"], tb=True, tm=512, tn=2048, tk=D_MODEL)
    G["w_o"] = _mm("mm_o_dw", ycat, dx1b, ta=True, tm=1024, tn=1024, tk=2048)
    dz_pool, G["w_pool"], G["pool_scale"] = _pool_bwd(z, dyc, W["w_pool"], W["pool_scale"], dy_cb=0)
    dz_mq, dkx, dvx, G["g_q_x"] = _memx_bwd(z, W["g_q_x"], kx, vx, dyc, dy_cb=3)
    dmkv, G["g_k_x"] = _memkv_bwd(dkx, dvx, mkv, W["g_k_x"])
    G["w_mem_kv"] = _mm("mm_mem_kv_dw", mn, dmkv, ta=True, tm=1024, tn=1024, tk=MEM_LEN)
    dmn = _mm("mm_mem_kv_dx", dmkv, W["w_mem_kv"], tb=True, tm=256, tn=2048, tk=1024)
    G["g_mem"] = _mem_gain_bwd(mem, dmn)
    sends = [_send_blocks(n, G[n]) for n in SCATTER_EARLY]
    dq, dk, dv, *got_early = _flash_bwd(q, k, v, dyc, POOL_WIDTH // MLA_V, o_mla, lse, sends)
    dq_raw, dkv_raw, dz_kr, G["g_q_mla"], G["g_k_mla"] = _mla_prep_bwd(
        dq, dk, dv, q_raw, kv_raw, z, tabs, W["g_q_mla"], W["g_k_mla"])
    G["w_q_up"] = _mm("mm_q_up_dw", ql, dq_raw, ta=True, tm=512, tn=2048, tk=1024)
    dql = _mm("mm_q_up_dx", dq_raw, W["w_q_up"], tb=True, tm=1024, tn=512, tk=2048)
    G["w_kv_up"] = _mm("mm_kv_up_dw", kvl, dkv_raw, ta=True, tm=256, tn=2048, tk=1024)
    dkvl = _mm("mm_kv_up_dx", dkv_raw, W["w_kv_up"], tb=True, tm=1024, tn=256, tk=2048)
    dz_q, dz_kv, G["g_q_lat"], G["g_kv_lat"] = _lat_bwd(z, W["g_q_lat"], W["g_kv_lat"], dql, dkvl)
    dz = jnp.concatenate([dz_pool, dz_q, dz_mq, dz_kv, dz_kr], axis=1)
    G["w_in"] = _mm("mm_in_dw", h, dz, ta=True, tm=1024, tn=Z_COLS, tk=1024)
    dh, *got_late = _mm("mm_in_dx", dz, W["w_in"], tb=True, tm=512, tn=2048, tk=Z_COLS,
                        sends=[_send_blocks(n, G[n]) for n in SCATTER_LATE])
    grad_x, G["g_mix"] = _rms_bwd_call("rms1_bwd", x, W["g_mix"], dh, dx1, False)
    return loss_part, grad_x, G, dict(zip(SCATTER_EARLY + SCATTER_LATE, got_early + got_late))


_ANY = pl.BlockSpec(memory_space=pl.ANY)
_MESH = pl.DeviceIdType.MESH


def _gather_out_shape(kind, shape):
    if kind == "rows":
        return (N_DEV * shape[0],) + tuple(shape[1:])
    if kind == "cols":
        return (shape[0], N_DEV * shape[1])
    return (N_DEV,) + tuple(shape)


def _gather_view(ref, kind, shape, d):
    if kind == "rows":
        return ref.at[pl.ds(pl.multiple_of(d * shape[0], 16), shape[0]), :]
    if kind == "cols":
        return ref.at[:, pl.ds(pl.multiple_of(d * shape[1], LANES), shape[1])]
    return ref.at[d]


class _Gather:
    def __init__(self, x_refs, out_refs, kinds, send_sems, recv_sems, local_sems):
        self.xr, self.outr, self.kinds = x_refs, out_refs, kinds
        self.ss, self.rs, self.ls = send_sems, recv_sems, local_sems
        x, y, c = lax.axis_index("x"), lax.axis_index("y"), lax.axis_index("c")
        self.c = c
        self.me, self.sibling = (x, y, c), (x, y, 1 - c)
        self.chips = [(1 - x, y), (x, 1 - y), (1 - x, 1 - y)]

    def _view(self, w, dev):
        px, py, pc = dev
        return _gather_view(self.outr[w], self.kinds[w], self.xr[w].shape, 4 * px + 2 * py + pc)

    def _copy(self, w, k, block, to, from_shard=False):
        v = self._view(w, block)
        return pltpu.make_async_remote_copy(
            src_ref=self.xr[w] if from_shard else v, dst_ref=v, send_sem=self.ss.at[w, k],
            recv_sem=self.rs.at[w, k], device_id=to, device_id_type=_MESH)

    def _local(self, w):
        return pltpu.make_async_copy(self.xr[w], self._view(w, self.me), self.ls.at[w])

    def start(self):
        for w in range(len(self.xr)):
            self._local(w).start()
            self._copy(w, 0, self.me, self.sibling, True).start()
            for j, chip in enumerate(self.chips):
                self._copy(w, 1 + j, self.me, (*chip, self.c), True).start()

    def forward(self):
        for j, chip in enumerate(self.chips):
            for w in range(len(self.xr)):
                self._copy(w, 1 + j, (*chip, self.c), self.me).wait_recv()
                self._copy(w, 4 + j, (*chip, self.c), self.sibling).start()

    def finish(self):
        for w in range(len(self.xr)):
            self._copy(w, 0, self.sibling, self.me).wait_recv()
            for j, chip in enumerate(self.chips):
                self._copy(w, 4 + j, (*chip, 1 - self.c), self.me).wait_recv()
            self._copy(w, 0, self.me, self.sibling, True).wait_send()
            for j, chip in enumerate(self.chips):
                self._copy(w, 1 + j, self.me, (*chip, self.c), True).wait_send()
                self._copy(w, 4 + j, (*chip, self.c), self.sibling).wait_send()
            self._local(w).wait()


class _Scatter:
    def __init__(self, send_refs, recv_refs, send_sems, recv_sems):
        self.sr, self.rr, self.ss, self.rs = send_refs, recv_refs, send_sems, recv_sems
        self.xyz = lax.axis_index("x"), lax.axis_index("y"), lax.axis_index("c")

    def _copy(self, w, k):
        x, y, c = self.xyz
        px, py, pc = x ^ ((k >> 2) & 1), y ^ ((k >> 1) & 1), c ^ (k & 1)
        return pltpu.make_async_remote_copy(
            src_ref=self.sr[w].at[4 * px + 2 * py + pc], dst_ref=self.rr[w].at[k - 1],
            send_sem=self.ss.at[w, k - 1], recv_sem=self.rs.at[w, k - 1],
            device_id=(px, py, pc), device_id_type=_MESH)

    def _all(self):
        return [self._copy(w, k) for w in range(len(self.sr)) for k in range(1, N_DEV)]

    def start(self):
        for cp in self._all():
            cp.start()

    def finish(self):
        for cp in self._all():
            cp.wait_recv()
        for cp in self._all():
            cp.wait_send()


def _gather_call(shards, kinds):
    nw = len(shards)

    def body(*refs):
        gather = _Gather(refs[:nw], refs[nw:2 * nw], kinds, *refs[2 * nw:])
        gather.start()
        gather.forward()
        gather.finish()

    return pl.pallas_call(
        body, name="gather_mixer_weights",
        out_shape=[jax.ShapeDtypeStruct(_gather_out_shape(kd, sd.shape), sd.dtype) for kd, sd in zip(kinds, shards)],
        in_specs=[_ANY] * nw, out_specs=[_ANY] * nw,
        scratch_shapes=[pltpu.SemaphoreType.DMA((nw, 7)), pltpu.SemaphoreType.DMA((nw, 7)),
                        pltpu.SemaphoreType.DMA((nw,))],
    )(*shards)


def _exchange_small(s):
    def body(x_ref, out_ref, x_send, x_recv, local_sem):
        x, y, c = lax.axis_index("x"), lax.axis_index("y"), lax.axis_index("c")
        me = 4 * x + 2 * y + c
        mine = pltpu.make_async_copy(x_ref, out_ref.at[me], local_sem)
        mine.start()

        def copy(k):
            px, py, pc = x ^ ((k >> 2) & 1), y ^ ((k >> 1) & 1), c ^ (k & 1)
            return pltpu.make_async_remote_copy(
                src_ref=x_ref, dst_ref=out_ref.at[me], send_sem=x_send.at[k - 1], recv_sem=x_recv.at[k - 1],
                device_id=(px, py, pc), device_id_type=_MESH)

        cps = [copy(k) for k in range(1, N_DEV)]
        for cp in cps:
            cp.start()
        for cp in cps:
            cp.wait_recv()
        for cp in cps:
            cp.wait_send()
        mine.wait()

    return pl.pallas_call(
        body, name="exchange_small", out_shape=jax.ShapeDtypeStruct((N_DEV,) + s.shape, s.dtype),
        in_specs=[_ANY], out_specs=_ANY,
        scratch_shapes=[pltpu.SemaphoreType.DMA((N_DEV - 1,)), pltpu.SemaphoreType.DMA((N_DEV - 1,)),
                        pltpu.SemaphoreType.DMA],
    )(s)


def _sum_slots(buf, tr=208):
    n, R, _ = buf.shape
    tr = tr if R % tr == 0 else R

    def body(b_ref, o_ref):
        acc = b_ref[0]
        for d in range(1, n):
            acc = acc + b_ref[d]
        o_ref[...] = acc

    return pl.pallas_call(
        body, name="sum_slots", grid=(R // tr,),
        in_specs=[pl.BlockSpec((n, tr, LANES), lambda i: (0, i, 0))],
        out_specs=pl.BlockSpec((tr, LANES), lambda i: (i, 0)),
        out_shape=jax.ShapeDtypeStruct((R, LANES), buf.dtype),
        compiler_params=_params("parallel"),
    )(buf)


def _adamw(name, w, g, got, m, v, tr=128):
    R, C = w.shape
    tr = max([t for t in range(16, tr + 1, 16) if R % t == 0], default=R) if R > tr else R
    has_got = got is not None

    def body(*refs):
        if has_got:
            w_ref, g_ref, got_ref, m_ref, v_ref, go_ref, d_ref, mo_ref, vo_ref = refs
        else:
            w_ref, g_ref, m_ref, v_ref, go_ref, d_ref, mo_ref, vo_ref = refs
        gv = g_ref[...]
        if has_got:
            for k in range(N_DEV - 1):
                gv = gv + got_ref[k].astype(F32)
        m_new = ADAM_B1 * m_ref[...] + (1.0 - ADAM_B1) * gv
        v_new = ADAM_B2 * v_ref[...] + (1.0 - ADAM_B2) * (gv * gv)
        m_hat = m_new / (1.0 - ADAM_B1 ** ADAM_STEP)
        v_hat = v_new / (1.0 - ADAM_B2 ** ADAM_STEP)
        go_ref[...] = gv
        d_ref[...] = -ADAM_LR * (m_hat / (jnp.sqrt(v_hat) + ADAM_EPS) + ADAM_WD * w_ref[...])
        mo_ref[...] = m_new
        vo_ref[...] = v_new

    spec = pl.BlockSpec((tr, C), lambda i: (i, 0))
    got_spec = [pl.BlockSpec((N_DEV - 1, tr, C), lambda i: (0, i, 0))] if has_got else []
    sds = jax.ShapeDtypeStruct((R, C), F32)
    ins = [w, g] + ([got] if has_got else []) + [m, v]
    return pl.pallas_call(
        body, name=name, grid=(R // tr,), in_specs=[spec] * 2 + got_spec + [spec] * 2, out_specs=[spec] * 4,
        out_shape=[sds] * 4, compiler_params=_params("parallel"),
    )(*ins)


def _pad_rows(a, mult):
    r = (-a.shape[0]) % mult
    return a if r == 0 else jnp.concatenate([a, jnp.zeros((r,) + a.shape[1:], a.dtype)], axis=0)


def _as_rows(a, mult):
    flat = a.reshape(-1)
    r = (-flat.shape[0]) % LANES
    if r:
        flat = jnp.concatenate([flat, jnp.zeros((r,), a.dtype)])
    return _pad_rows(flat.reshape(-1, LANES), mult)


def _w_in_to_kernel_cols(w):
    pad = jnp.zeros(w.shape[:-1] + (Z_COLS - IN_COLS,), w.dtype)
    return jnp.concatenate([w[..., :1024], w[..., 1344:1856], w[..., 1024:1344], pad], axis=-1)


def _w_in_from_kernel_cols(w):
    return jnp.concatenate([w[..., :1024], w[..., 1536:1856], w[..., 1024:1536]], axis=-1)


def _dev_blocks(n, g):
    ffs = D_FF // N_DEV
    if n == "w_in":
        return _w_in_from_kernel_cols(g).reshape(N_DEV, D_MODEL // N_DEV, IN_COLS)
    if n == "w_q_up":
        return jnp.transpose(g.reshape(Q_LORA, N_DEV, MLA_QK_PAD)[:, :, :MLA_QK], (1, 0, 2))
    if n == "w_kv_up":
        return jnp.transpose(g.reshape(KV_LORA, N_DEV, MLA_NOPE + MLA_V), (1, 0, 2))
    if n in ("w_mem_kv", "w_o"):
        return g.reshape(N_DEV, D_MODEL // N_DEV, g.shape[1])
    if n in ("w_gate", "w_up"):
        return jnp.transpose(g.reshape(D_MODEL, N_DEV, ffs), (1, 0, 2))
    if n == "conv_w":
        return jnp.transpose(g.reshape(3, N_DEV, ffs), (1, 0, 2))
    assert n == "w_down"
    return g.reshape(N_DEV, ffs, D_MODEL)


def _send_blocks(n, g):
    blocks = _dev_blocks(n, g)
    return blocks if n == "conv_w" else blocks.astype(BF16)


def kernel(x, mem, g_mix, w_in, g_q_lat, w_q_up, g_kv_lat, w_kv_up, g_q_mla, g_k_mla, w_pool, pool_scale, g_mem, w_mem_kv, g_q_x, g_k_x, w_o, g_ffn, w_gate, w_up, conv_w, conv_b, w_down, loss_target, m_g_mix, m_w_in, m_g_q_lat, m_w_q_up, m_g_kv_lat, m_w_kv_up, m_g_q_mla, m_g_k_mla, m_w_pool, m_pool_scale, m_g_mem, m_w_mem_kv, m_g_q_x, m_g_k_x, m_w_o, m_g_ffn, m_w_gate, m_w_up, m_conv_w, m_conv_b, m_w_down, v_g_mix, v_w_in, v_g_q_lat, v_w_q_up, v_g_kv_lat, v_w_kv_up, v_g_q_mla, v_g_k_mla, v_w_pool, v_pool_scale, v_g_mem, v_w_mem_kv, v_g_q_x, v_g_k_x, v_w_o, v_g_ffn, v_w_gate, v_w_up, v_conv_w, v_conv_b, v_w_down):
    given = dict(g_mix=g_mix, w_in=w_in, g_q_lat=g_q_lat, w_q_up=w_q_up, g_kv_lat=g_kv_lat, w_kv_up=w_kv_up,
                 g_q_mla=g_q_mla, g_k_mla=g_k_mla, w_pool=w_pool, pool_scale=pool_scale, g_mem=g_mem,
                 w_mem_kv=w_mem_kv, g_q_x=g_q_x, g_k_x=g_k_x, w_o=w_o, g_ffn=g_ffn, w_gate=w_gate, w_up=w_up,
                 conv_w=conv_w, conv_b=conv_b, w_down=w_down)
    mom_m = dict(g_mix=m_g_mix, w_in=m_w_in, g_q_lat=m_g_q_lat, w_q_up=m_w_q_up, g_kv_lat=m_g_kv_lat,
                 w_kv_up=m_w_kv_up, g_q_mla=m_g_q_mla, g_k_mla=m_g_k_mla, w_pool=m_w_pool,
                 pool_scale=m_pool_scale, g_mem=m_g_mem, w_mem_kv=m_w_mem_kv, g_q_x=m_g_q_x, g_k_x=m_g_k_x,
                 w_o=m_w_o, g_ffn=m_g_ffn, w_gate=m_w_gate, w_up=m_w_up, conv_w=m_conv_w, conv_b=m_conv_b,
                 w_down=m_w_down)
    mom_v = dict(g_mix=v_g_mix, w_in=v_w_in, g_q_lat=v_g_q_lat, w_q_up=v_w_q_up, g_kv_lat=v_g_kv_lat,
                 w_kv_up=v_w_kv_up, g_q_mla=v_g_q_mla, g_k_mla=v_g_k_mla, w_pool=v_w_pool,
                 pool_scale=v_pool_scale, g_mem=v_g_mem, w_mem_kv=v_w_mem_kv, g_q_x=v_g_q_x, g_k_x=v_g_k_x,
                 w_o=v_w_o, g_ffn=v_g_ffn, w_gate=v_w_gate, w_up=v_w_up, conv_w=v_conv_w, conv_b=v_conv_b,
                 w_down=v_w_down)
    drop = lambda a: a[0] if a.ndim > 2 else a
    sh = {n: drop(given[n]) for n in WEIGHTS}
    mom_m = {n: drop(mom_m[n]) for n in WEIGHTS}
    mom_v = {n: drop(mom_v[n]) for n in WEIGHTS}

    cw_hi = sh["conv_w"].astype(BF16)
    cw_r = sh["conv_w"] - cw_hi.astype(F32)
    cw_mid = cw_r.astype(BF16)
    cw_lo = (cw_r - cw_mid.astype(F32)).astype(BF16)
    early_shards = [
        _w_in_to_kernel_cols(sh["w_in"]).astype(BF16),
        jnp.pad(sh["w_q_up"], ((0, 0), (0, MLA_QK_PAD - MLA_QK))).astype(BF16),
        sh["w_kv_up"].astype(BF16),
    ]
    late_shards = [sh["w_gate"].astype(BF16), sh["w_up"].astype(BF16), sh["w_down"].astype(BF16),
                   jnp.concatenate([cw_hi, cw_mid, cw_lo], axis=0), sh["w_o"].astype(BF16),
                   sh["w_mem_kv"].astype(BF16)]
    W = dict(zip(("w_in", "w_q_up", "w_kv_up"), _gather_call(early_shards, ("rows", "cols", "cols"))))
    W["w_pool"] = sh["w_pool"].astype(BF16)
    for n in ("g_mix", "g_q_lat", "g_kv_lat", "pool_scale", "g_mem", "g_q_x", "g_k_x", "g_ffn", "conv_b"):
        W[n] = sh[n]
    pad_qk = lambda gv: jnp.pad(gv, ((0, 0), (0, MLA_QK_PAD - MLA_QK)))
    W["g_q_mla"], W["g_k_mla"] = pad_qk(sh["g_q_mla"]), pad_qk(sh["g_k_mla"])

    loss_part, grad_x, G, got = _local_step(x[0], mem[0], loss_target[0], W, late_shards)

    small = {
        "g_mix": G["g_mix"], "g_q_lat": G["g_q_lat"], "g_kv_lat": G["g_kv_lat"],
        "g_q_mla": G["g_q_mla"][:, :MLA_QK], "g_k_mla": G["g_k_mla"][:, :MLA_QK],
        "w_pool": G["w_pool"], "pool_scale": G["pool_scale"], "g_mem": G["g_mem"],
        "g_q_x": G["g_q_x"], "g_k_x": G["g_k_x"], "g_ffn": G["g_ffn"], "conv_b": G["conv_b"],
    }
    s_offs = {}
    segs = []
    off = 0
    for n in SMALL:
        r = _as_rows(small[n], 8)
        s_offs[n] = off
        off += r.shape[0]
        segs.append(r)
    segs.append(loss_part)
    loss_row = off
    sbuf = jnp.concatenate(segs, axis=0)
    s_sum = _sum_slots(_exchange_small(sbuf))

    def small_take(buf, n):
        shape = sh[n].shape
        cnt = math.prod(shape)
        return buf[s_offs[n]:s_offs[n] + -(-cnt // LANES)].reshape(-1)[:cnt].reshape(shape)

    loss = s_sum[loss_row, 0]

    me = 4 * lax.axis_index("x") + 2 * lax.axis_index("y") + lax.axis_index("c")
    grads, deltas, new_m, new_v = {}, {}, {}, {}
    for n in BIG:
        shape = sh[n].shape
        own = lax.dynamic_index_in_dim(_dev_blocks(n, G[n]), me, 0, keepdims=False)
        flat = lambda a: a.reshape(-1, shape[-1])
        outs = _adamw("adamw_" + n, flat(sh[n]), flat(own), got[n].reshape(N_DEV - 1, -1, shape[-1]),
                      flat(mom_m[n]), flat(mom_v[n]))
        grads[n], deltas[n], new_m[n], new_v[n] = (o.reshape(shape) for o in outs)

    def pack_small(src):
        return jnp.concatenate([_as_rows(src[n], 8) for n in SMALL], axis=0)

    _, d_s, m_s, v_s = _adamw("adamw_small", pack_small(sh), s_sum[:loss_row], None,
                              pack_small(mom_m), pack_small(mom_v))
    for n in SMALL:
        grads[n] = small_take(s_sum, n)
        deltas[n], new_m[n], new_v[n] = small_take(d_s, n), small_take(m_s, n), small_take(v_s, n)

    lead = lambda n, a: a.reshape(given[n].shape)
    return (loss, grad_x[None],
            *[lead(n, grads[n]) for n in WEIGHTS], *[lead(n, deltas[n]) for n in WEIGHTS],
            *[lead(n, new_m[n]) for n in WEIGHTS], *[lead(n, new_v[n]) for n in WEIGHTS])
```

```python
import math

import jax
import jax.numpy as jnp
from jax import lax
from jax.experimental import pallas as pl
from jax.experimental.pallas import tpu as pltpu

F32 = jnp.float32
BF16 = jnp.bfloat16

D_MODEL = 2048
D_FF = 5632
POOL_WIDTH = 512
POOL_WINDOWS = (2, 4, 8, 16)
POOL_HALO = 16
MLA_HEADS = 8
MLA_NOPE = 128
MLA_ROPE = 64
MLA_QK = MLA_NOPE + MLA_ROPE
MLA_QK_PAD = 256
MLA_V = 128
Q_LORA = 512
KV_LORA = 256
X_HEADS = 4
X_HEAD_DIM = 128
X_WIDTH = 512
MEM_LEN = 256
ROPE_THETA = 10000.0
NORM_EPS = 1e-6
CONV_HALO = 16
IN_COLS = 1856
Z_COLS = 1920
Z_POOL, Z_Q, Z_MQ, Z_KV, Z_KR = 0, 512, 1024, 1536, 1792

ADAM_LR = 0.001
ADAM_B1 = 0.9
ADAM_B2 = 0.999
ADAM_EPS = 1e-08
ADAM_WD = 0.01
ADAM_STEP = 10

N_DEV = 8
LANES = 128
VMEM_LIMIT = 56 * 1024 * 1024

BIG = ("w_in", "w_q_up", "w_kv_up", "w_mem_kv", "w_o", "w_gate", "w_up", "conv_w", "w_down")
SCATTER_EARLY = ("w_down", "w_gate", "w_up", "conv_w", "w_o", "w_mem_kv")
SCATTER_LATE = ("w_in", "w_q_up", "w_kv_up")
FFN_GATHER_KINDS = ("lead", "lead", "rows", "lead")
MIXER_GATHER_KINDS = ("rows", "rows")
SMALL = ("g_mix", "g_q_lat", "g_kv_lat", "g_q_mla", "g_k_mla", "w_pool", "pool_scale", "g_mem",
         "g_q_x", "g_k_x", "g_ffn", "conv_b")
WEIGHTS = ("g_mix", "w_in", "g_q_lat", "w_q_up", "g_kv_lat", "w_kv_up", "g_q_mla", "g_k_mla", "w_pool",
           "pool_scale", "g_mem", "w_mem_kv", "g_q_x", "g_k_x", "w_o", "g_ffn", "w_gate", "w_up", "conv_w",
           "conv_b", "w_down")


def _tile(n, t):
    if n <= t:
        return n
    for c in range(t - t % LANES, 0, -LANES):
        if n % c == 0:
            return c
    return n


def _params(*sem):
    return pltpu.CompilerParams(dimension_semantics=sem, vmem_limit_bytes=VMEM_LIMIT)


def _full(shape):
    nd = len(shape)
    return pl.BlockSpec(shape, lambda *_: (0,) * nd)


def _rows(tr, w, cb=0):
    return pl.BlockSpec((tr, w), lambda i: (i, cb))


def _rms_hat(x, n):
    r = lax.rsqrt(jnp.sum(x * x, axis=-1, keepdims=True) / n + NORM_EPS)
    return x * r, r


def _rms_bwd(dy, xhat, r, g, n):
    dxh = dy * g
    dx = r * (dxh - xhat * (jnp.sum(dxh * xhat, axis=-1, keepdims=True) / n))
    return dx, jnp.sum(dy * xhat, axis=0, keepdims=True)


def _mm(name, a, b, *, ta=False, tb=False, add=None, out_dtype=F32, tm=512, tn=512, tk=512, sends=()):
    if ta:
        K, M = a.shape
    else:
        M, K = a.shape
    if tb:
        N, K2 = b.shape
    else:
        K2, N = b.shape
    assert K == K2, (name, a.shape, b.shape)
    tm, tn, tk = _tile(M, tm), _tile(N, tn), _tile(K, tk)
    nk = K // tk
    grid = (M // tm, N // tn, nk)
    a_spec = pl.BlockSpec((tk, tm), lambda i, j, k: (k, i)) if ta else pl.BlockSpec((tm, tk), lambda i, j, k: (i, k))
    b_spec = pl.BlockSpec((tn, tk), lambda i, j, k: (j, k)) if tb else pl.BlockSpec((tk, tn), lambda i, j, k: (k, j))
    o_spec = pl.BlockSpec((tm, tn), lambda i, j, k: (i, j))
    dims = (((0,) if ta else (1,), (1,) if tb else (0,)), ((), ()))
    has_add = add is not None
    n_in = 3 if has_add else 2
    nw = len(sends)

    def body(*refs):
        a_ref, b_ref = refs[:2]
        add_ref = refs[2] if has_add else None
        s_refs, o_ref, r_refs = refs[n_in:n_in + nw], refs[n_in + nw], refs[n_in + nw + 1:n_in + 2 * nw + 1]
        scr = refs[n_in + 2 * nw + 1:]
        step = (pl.program_id(0) * grid[1] + pl.program_id(1)) * grid[2] + pl.program_id(2)
        if nw:
            scatter = _Scatter(s_refs, r_refs, *scr[-2:])
            pl.when(step == 0)(scatter.start)
        part = lax.dot_general(a_ref[...], b_ref[...], dims, preferred_element_type=F32)
        if nk == 1:
            if has_add:
                part = part + add_ref[...]
            o_ref[...] = part.astype(o_ref.dtype)
        else:
            acc = scr[0]
            k = pl.program_id(2)

            @pl.when(k == 0)
            def _():
                acc[...] = part

            @pl.when(k > 0)
            def _():
                acc[...] += part

            @pl.when(k == nk - 1)
            def _():
                r = acc[...]
                if has_add:
                    r = r + add_ref[...]
                o_ref[...] = r.astype(o_ref.dtype)
        if nw:
            pl.when(step == grid[0] * grid[1] * grid[2] - 1)(scatter.finish)

    ins = [a, b] + ([add] if has_add else []) + list(sends)
    in_specs = [a_spec, b_spec] + ([o_spec] if has_add else []) + [_ANY] * nw
    out_shape = [jax.ShapeDtypeStruct((M, N), out_dtype)]
    out_shape += [jax.ShapeDtypeStruct((N_DEV - 1,) + sd.shape[1:], sd.dtype) for sd in sends]
    scratch = [pltpu.VMEM((tm, tn), F32)] if nk > 1 else []
    if nw:
        scratch += [pltpu.SemaphoreType.DMA((nw, N_DEV - 1)), pltpu.SemaphoreType.DMA((nw, N_DEV - 1))]
    sem = ("arbitrary",) * 3 if nw else ("parallel", "parallel", "arbitrary")
    outs = pl.pallas_call(
        body, name=name, grid=grid, in_specs=in_specs, out_specs=[o_spec] + [_ANY] * nw,
        out_shape=out_shape, scratch_shapes=scratch, compiler_params=_params(*sem),
    )(*ins)
    return outs if nw else outs[0]


def _rms_fwd(name, x, g, tr=512):
    S, n = x.shape
    tr = _tile(S, tr)

    def body(x_ref, g_ref, h_ref):
        xhat, _ = _rms_hat(x_ref[...], n)
        h_ref[...] = (xhat * g_ref[...]).astype(BF16)

    return pl.pallas_call(
        body, name=name, grid=(S // tr,),
        in_specs=[_rows(tr, n), _full((1, n))], out_specs=_rows(tr, n),
        out_shape=jax.ShapeDtypeStruct((S, n), BF16),
        compiler_params=_params("parallel"),
    )(x, g)


def _rms_bwd_call(name, x, g, dy, extra, want_bf16, tr=256):
    S, n = x.shape
    tr = _tile(S, tr)

    def body(x_ref, g_ref, dy_ref, ex_ref, *outs):
        xhat, r = _rms_hat(x_ref[...], n)
        dx, dg = _rms_bwd(dy_ref[...], xhat, r, g_ref[...], n)
        dx = dx + ex_ref[...]
        outs[0][...] = dx
        if want_bf16:
            outs[1][...] = dx.astype(BF16)
        dg_ref = outs[-1]

        @pl.when(pl.program_id(0) == 0)
        def _():
            dg_ref[...] = jnp.zeros_like(dg_ref)

        dg_ref[...] += dg

    out_shape = [jax.ShapeDtypeStruct((S, n), F32)]
    out_specs = [_rows(tr, n)]
    if want_bf16:
        out_shape.append(jax.ShapeDtypeStruct((S, n), BF16))
        out_specs.append(_rows(tr, n))
    out_shape.append(jax.ShapeDtypeStruct((1, n), F32))
    out_specs.append(_full((1, n)))
    return pl.pallas_call(
        body, name=name, grid=(S // tr,),
        in_specs=[_rows(tr, n), _full((1, n)), _rows(tr, n), _rows(tr, n)],
        out_specs=out_specs, out_shape=out_shape,
        compiler_params=_params("arbitrary"),
    )(x, g, dy, extra)


def _lat_fwd(z, gq, gkv, tr=512):
    S = z.shape[0]
    tr = _tile(S, tr)

    def body(zq_ref, zkv_ref, gq_ref, gkv_ref, ql_ref, kvl_ref):
        xq, _ = _rms_hat(zq_ref[...], Q_LORA)
        ql_ref[...] = (xq * gq_ref[...]).astype(BF16)
        xkv, _ = _rms_hat(zkv_ref[...], KV_LORA)
        kvl_ref[...] = (xkv * gkv_ref[...]).astype(BF16)

    return pl.pallas_call(
        body, name="lat_fwd", grid=(S // tr,),
        in_specs=[_rows(tr, Q_LORA, Z_Q // Q_LORA), _rows(tr, KV_LORA, Z_KV // KV_LORA),
                  _full((1, Q_LORA)), _full((1, KV_LORA))],
        out_specs=[_rows(tr, Q_LORA), _rows(tr, KV_LORA)],
        out_shape=[jax.ShapeDtypeStruct((S, Q_LORA), BF16), jax.ShapeDtypeStruct((S, KV_LORA), BF16)],
        compiler_params=_params("parallel"),
    )(z, z, gq, gkv)


def _lat_bwd(z, gq, gkv, dql, dkvl, tr=512):
    S = z.shape[0]
    tr = _tile(S, tr)

    def body(zq_ref, zkv_ref, gq_ref, gkv_ref, dql_ref, dkvl_ref, dzq_ref, dzkv_ref, dgq_ref, dgkv_ref):
        @pl.when(pl.program_id(0) == 0)
        def _():
            dgq_ref[...] = jnp.zeros_like(dgq_ref)
            dgkv_ref[...] = jnp.zeros_like(dgkv_ref)

        xq, rq = _rms_hat(zq_ref[...], Q_LORA)
        dx, dg = _rms_bwd(dql_ref[...], xq, rq, gq_ref[...], Q_LORA)
        dzq_ref[...] = dx.astype(BF16)
        dgq_ref[...] += dg
        xkv, rkv = _rms_hat(zkv_ref[...], KV_LORA)
        dx, dg = _rms_bwd(dkvl_ref[...], xkv, rkv, gkv_ref[...], KV_LORA)
        dzkv_ref[...] = dx.astype(BF16)
        dgkv_ref[...] += dg

    return pl.pallas_call(
        body, name="lat_bwd", grid=(S // tr,),
        in_specs=[_rows(tr, Q_LORA, Z_Q // Q_LORA), _rows(tr, KV_LORA, Z_KV // KV_LORA),
                  _full((1, Q_LORA)), _full((1, KV_LORA)), _rows(tr, Q_LORA), _rows(tr, KV_LORA)],
        out_specs=[_rows(tr, Q_LORA), _rows(tr, KV_LORA), _full((1, Q_LORA)), _full((1, KV_LORA))],
        out_shape=[jax.ShapeDtypeStruct((S, Q_LORA), BF16), jax.ShapeDtypeStruct((S, KV_LORA), BF16),
                   jax.ShapeDtypeStruct((1, Q_LORA), F32), jax.ShapeDtypeStruct((1, KV_LORA), F32)],
        compiler_params=_params("arbitrary"),
    )(z, z, gq, gkv, dql, dkvl)


def _pool_d(ext, cur, t0, tr):
    t = t0 + lax.broadcasted_iota(jnp.int32, (tr, 1), 0)
    ds = []
    for gi, w in enumerate(POOL_WINDOWS):
        s = ext[:, gi * LANES:(gi + 1) * LANES]
        sh = 1
        while sh < w:
            s = s + pltpu.roll(s, sh, 0)
            sh *= 2
        cnt = jnp.minimum(t + 1, w).astype(F32)
        ds.append(s[POOL_HALO:] / cnt - cur[:, gi * LANES:(gi + 1) * LANES])
    return ds


def _pool_fwd(z, wp, scale, tr=512):
    S = z.shape[0]
    tr = _tile(S, tr)
    rb = tr // POOL_HALO

    def body(z_ref, zp_ref, wp_ref, sc_ref, y_ref):
        i = pl.program_id(0)
        cur = z_ref[...]
        halo = jnp.where(i == 0, 0.0, zp_ref[...])
        ds = _pool_d(jnp.concatenate([halo, cur], axis=0), cur, i * tr, tr)
        for gi in range(len(POOL_WINDOWS)):
            y = jnp.dot(ds[gi].astype(BF16), wp_ref[gi], preferred_element_type=F32)
            y_ref[:, gi * LANES:(gi + 1) * LANES] = (y * sc_ref[:, gi * LANES:(gi + 1) * LANES]).astype(BF16)

    return pl.pallas_call(
        body, name="pool_fwd", grid=(S // tr,),
        in_specs=[_rows(tr, POOL_WIDTH, Z_POOL // POOL_WIDTH),
                  pl.BlockSpec((POOL_HALO, POOL_WIDTH), lambda i: (jnp.maximum(i * rb - 1, 0), 0)),
                  _full(wp.shape), _full((1, POOL_WIDTH))],
        out_specs=_rows(tr, POOL_WIDTH),
        out_shape=jax.ShapeDtypeStruct((S, POOL_WIDTH), BF16),
        compiler_params=_params("parallel"),
    )(z, z, wp, scale)


def _pool_bwd(z, dyc, wp, scale, dy_cb, tr=512):
    S = z.shape[0]
    tr = _tile(S, tr)
    rb = tr // POOL_HALO
    nhb = S // POOL_HALO
    ng = len(POOL_WINDOWS)

    def body(z_ref, zp_ref, dy_ref, dyn_ref, wp_ref, sc_ref, dz_ref, dwp_ref, dsc_ref):
        i = pl.program_id(0)

        @pl.when(i == 0)
        def _():
            dwp_ref[...] = jnp.zeros_like(dwp_ref)
            dsc_ref[...] = jnp.zeros_like(dsc_ref)

        cur = z_ref[...]
        halo = jnp.where(i == 0, 0.0, zp_ref[...])
        ds = _pool_d(jnp.concatenate([halo, cur], axis=0), cur, i * tr, tr)
        dy_cur = dy_ref[...]
        dy_next = jnp.where(i == pl.num_programs(0) - 1, 0.0, dyn_ref[...])
        dy_ext = jnp.concatenate([dy_cur, dy_next], axis=0)
        n_ext = tr + POOL_HALO
        t_ext = i * tr + lax.broadcasted_iota(jnp.int32, (n_ext, 1), 0)
        for gi, w in enumerate(POOL_WINDOWS):
            cols = slice(gi * LANES, (gi + 1) * LANES)
            d_b = ds[gi].astype(BF16)
            y_pre = jnp.dot(d_b, wp_ref[gi], preferred_element_type=F32)
            dsc_ref[:, cols] += jnp.sum(dy_cur[:, cols] * y_pre, axis=0, keepdims=True)
            dys = (dy_ext[:, cols] * sc_ref[:, cols]).astype(BF16)
            dwp_ref[gi] += lax.dot_general(d_b, dys[:tr], (((0,), (0,)), ((), ())), preferred_element_type=F32)
            dd = lax.dot_general(dys, wp_ref[gi], (((1,), (1,)), ((), ())), preferred_element_type=F32)
            s = dd / jnp.minimum(t_ext + 1, w).astype(F32)
            sh = 1
            while sh < w:
                s = s + pltpu.roll(s, n_ext - sh, 0)
                sh *= 2
            dz_ref[:, cols] = (s[:tr] - dd[:tr]).astype(BF16)

    return pl.pallas_call(
        body, name="pool_bwd", grid=(S // tr,),
        in_specs=[_rows(tr, POOL_WIDTH, Z_POOL // POOL_WIDTH),
                  pl.BlockSpec((POOL_HALO, POOL_WIDTH), lambda i: (jnp.maximum(i * rb - 1, 0), 0)),
                  _rows(tr, POOL_WIDTH, dy_cb),
                  pl.BlockSpec((POOL_HALO, POOL_WIDTH), lambda i: (jnp.minimum((i + 1) * rb, nhb - 1), dy_cb)),
                  _full(wp.shape), _full((1, POOL_WIDTH))],
        out_specs=[_rows(tr, POOL_WIDTH), _full((ng, LANES, LANES)), _full((1, POOL_WIDTH))],
        out_shape=[jax.ShapeDtypeStruct((S, POOL_WIDTH), BF16), jax.ShapeDtypeStruct((ng, LANES, LANES), F32),
                   jax.ShapeDtypeStruct((1, POOL_WIDTH), F32)],
        compiler_params=_params("arbitrary"),
    )(z, z, dyc, dyc, wp, scale)


def _rope_tables(S):
    half = MLA_ROPE // 2
    inv_freq = 1.0 / (ROPE_THETA ** (jnp.arange(half, dtype=F32) / half))
    ang = jnp.arange(S).astype(F32)[:, None] * inv_freq[None, :]
    cos, sin = jnp.cos(ang), jnp.sin(ang)
    z32 = jnp.zeros((S, half), F32)
    z64 = jnp.zeros((S, LANES - MLA_ROPE), F32)
    cos_t = jnp.concatenate([cos, cos, z64], axis=1)
    sin_a = jnp.concatenate([-sin, z32, z64], axis=1)
    sin_b = jnp.concatenate([z32, sin, z64], axis=1)
    return cos_t, sin_a, sin_b


def _rope(x, cos_t, sin_a, sin_b):
    return x * cos_t + pltpu.roll(x, LANES - 32, 1) * sin_a + pltpu.roll(x, 32, 1) * sin_b


def _rope_t(d, cos_t, sin_a, sin_b):
    return d * cos_t + pltpu.roll(d * sin_a, 32, 1) + pltpu.roll(d * sin_b, LANES - 32, 1)


def _mla_prep(q_raw, kv_raw, z, tabs, gq, gk, shards, kinds, tr=256):
    S = z.shape[0]
    tr = _tile(S, tr)
    n_steps = S // tr
    scale = 1.0 / math.sqrt(MLA_QK)
    W = MLA_HEADS * MLA_QK_PAD
    nw = len(shards)

    def body(q_ref, kv_ref, kr_ref, c_ref, sa_ref, sb_ref, gq_ref, gk_ref, *rest):
        x_refs, (qo_ref, ko_ref, vo_ref), g_refs = rest[:nw], rest[nw:nw + 3], rest[nw + 3:2 * nw + 3]
        gather = _Gather(x_refs, g_refs, kinds, *rest[2 * nw + 3:])
        i = pl.program_id(0)
        pl.when(i == 0)(gather.start)
        pl.when(i == n_steps // 2)(gather.forward)
        tab = (c_ref[...], sa_ref[...], sb_ref[...])
        kr = kr_ref[...]
        kr_ss = jnp.sum(kr * kr, axis=-1, keepdims=True)
        gqn, gqr = gq_ref[:, :LANES], gq_ref[:, LANES:]
        gkn, gkr = gk_ref[:, :LANES], gk_ref[:, LANES:]
        for h in range(MLA_HEADS):
            c0 = h * MLA_QK_PAD
            qn = q_ref[:, c0:c0 + LANES]
            qr = q_ref[:, c0 + LANES:c0 + 2 * LANES]
            r = lax.rsqrt((jnp.sum(qn * qn, -1, keepdims=True) + jnp.sum(qr * qr, -1, keepdims=True)) / MLA_QK
                          + NORM_EPS)
            qo_ref[h, :, :LANES] = (qn * r * gqn * scale).astype(BF16)
            qo_ref[h, :, LANES:] = (_rope(qr * r * gqr, *tab) * scale).astype(BF16)
            kn = kv_ref[:, c0:c0 + LANES]
            r = lax.rsqrt((jnp.sum(kn * kn, -1, keepdims=True) + kr_ss) / MLA_QK + NORM_EPS)
            ko_ref[h, :, :LANES] = (kn * r * gkn).astype(BF16)
            ko_ref[h, :, LANES:] = _rope(kr * r * gkr, *tab).astype(BF16)
            vo_ref[h] = kv_ref[:, c0 + LANES:c0 + 2 * LANES].astype(BF16)
        pl.when(i == n_steps - 1)(gather.finish)

    hs = lambda w: pl.BlockSpec((MLA_HEADS, tr, w), lambda i: (0, i, 0))
    return pl.pallas_call(
        body, name="mla_prep", grid=(n_steps,),
        in_specs=[_rows(tr, W), _rows(tr, W), _rows(tr, LANES, Z_KR // LANES),
                  _rows(tr, LANES), _rows(tr, LANES), _rows(tr, LANES),
                  _full((1, MLA_QK_PAD)), _full((1, MLA_QK_PAD))] + [_ANY] * nw,
        out_specs=[hs(MLA_QK_PAD), hs(MLA_QK_PAD), hs(MLA_V)] + [_ANY] * nw,
        out_shape=[jax.ShapeDtypeStruct((MLA_HEADS, S, MLA_QK_PAD), BF16),
                   jax.ShapeDtypeStruct((MLA_HEADS, S, MLA_QK_PAD), BF16),
                   jax.ShapeDtypeStruct((MLA_HEADS, S, MLA_V), BF16)]
        + [jax.ShapeDtypeStruct(_gather_out_shape(kd, sd.shape), sd.dtype) for kd, sd in zip(kinds, shards)],
        scratch_shapes=[pltpu.SemaphoreType.DMA((nw, 7)), pltpu.SemaphoreType.DMA((nw, 7)),
                        pltpu.SemaphoreType.DMA((nw,))],
        compiler_params=_params("arbitrary"),
    )(q_raw, kv_raw, z, *tabs, gq, gk, *shards)


def _mla_prep_bwd(dq, dk, dv, q_raw, kv_raw, z, tabs, gq, gk, tr=256):
    S = z.shape[0]
    tr = _tile(S, tr)
    scale = 1.0 / math.sqrt(MLA_QK)
    W = MLA_HEADS * MLA_QK_PAD

    def body(dq_ref, dk_ref, dv_ref, q_ref, kv_ref, kr_ref, c_ref, sa_ref, sb_ref, gq_ref, gk_ref,
             dqr_ref, dkvr_ref, dkr_ref, dgq_ref, dgk_ref):
        @pl.when(pl.program_id(0) == 0)
        def _():
            dgq_ref[...] = jnp.zeros_like(dgq_ref)
            dgk_ref[...] = jnp.zeros_like(dgk_ref)

        tab = (c_ref[...], sa_ref[...], sb_ref[...])
        kr = kr_ref[...]
        kr_ss = jnp.sum(kr * kr, axis=-1, keepdims=True)
        gqn, gqr = gq_ref[:, :LANES], gq_ref[:, LANES:]
        gkn, gkr = gk_ref[:, :LANES], gk_ref[:, LANES:]
        dkr_sum = jnp.zeros((tr, LANES), F32)
        dgq_n = jnp.zeros((1, LANES), F32)
        dgq_r = jnp.zeros((1, LANES), F32)
        dgk_n = jnp.zeros((1, LANES), F32)
        dgk_r = jnp.zeros((1, LANES), F32)

        def head_norm_bwd(xn, xr, r, dyn, dyr, gn, gr):
            hn, hr = xn * r, xr * r
            dxn, dxr = dyn * gn, dyr * gr
            mt = (jnp.sum(dxn * hn, -1, keepdims=True) + jnp.sum(dxr * hr, -1, keepdims=True)) / MLA_QK
            return (r * (dxn - hn * mt), r * (dxr - hr * mt),
                    jnp.sum(dyn * hn, axis=0, keepdims=True), jnp.sum(dyr * hr, axis=0, keepdims=True))

        for h in range(MLA_HEADS):
            c0 = h * MLA_QK_PAD
            qn = q_ref[:, c0:c0 + LANES]
            qr = q_ref[:, c0 + LANES:c0 + 2 * LANES]
            r = lax.rsqrt((jnp.sum(qn * qn, -1, keepdims=True) + jnp.sum(qr * qr, -1, keepdims=True)) / MLA_QK
                          + NORM_EPS)
            dyn = dq_ref[h, :, :LANES] * scale
            dyr = _rope_t(dq_ref[h, :, LANES:] * scale, *tab)
            dxn, dxr, gn_, gr_ = head_norm_bwd(qn, qr, r, dyn, dyr, gqn, gqr)
            dgq_n += gn_
            dgq_r += gr_
            dqr_ref[:, c0:c0 + LANES] = dxn.astype(BF16)
            dqr_ref[:, c0 + LANES:c0 + 2 * LANES] = dxr.astype(BF16)

            kn = kv_ref[:, c0:c0 + LANES]
            r = lax.rsqrt((jnp.sum(kn * kn, -1, keepdims=True) + kr_ss) / MLA_QK + NORM_EPS)
            dyn = dk_ref[h, :, :LANES]
            dyr = _rope_t(dk_ref[h, :, LANES:], *tab)
            dxn, dxr, gn_, gr_ = head_norm_bwd(kn, kr, r, dyn, dyr, gkn, gkr)
            dgk_n += gn_
            dgk_r += gr_
            dkr_sum += dxr
            dkvr_ref[:, c0:c0 + LANES] = dxn.astype(BF16)
            dkvr_ref[:, c0 + LANES:c0 + 2 * LANES] = dv_ref[h].astype(BF16)

        dkr_ref[...] = dkr_sum.astype(BF16)
        dgq_ref[:, :LANES] += dgq_n
        dgq_ref[:, LANES:] += dgq_r
        dgk_ref[:, :LANES] += dgk_n
        dgk_ref[:, LANES:] += dgk_r

    hs = lambda w: pl.BlockSpec((MLA_HEADS, tr, w), lambda i: (0, i, 0))
    return pl.pallas_call(
        body, name="mla_prep_bwd", grid=(S // tr,),
        in_specs=[hs(MLA_QK_PAD), hs(MLA_QK_PAD), hs(MLA_V), _rows(tr, W), _rows(tr, W),
                  _rows(tr, LANES, Z_KR // LANES), _rows(tr, LANES), _rows(tr, LANES), _rows(tr, LANES),
                  _full((1, MLA_QK_PAD)), _full((1, MLA_QK_PAD))],
        out_specs=[_rows(tr, W), _rows(tr, W), _rows(tr, LANES), _full((1, MLA_QK_PAD)), _full((1, MLA_QK_PAD))],
        out_shape=[jax.ShapeDtypeStruct((S, W), BF16), jax.ShapeDtypeStruct((S, W), BF16),
                   jax.ShapeDtypeStruct((S, LANES), BF16),
                   jax.ShapeDtypeStruct((1, MLA_QK_PAD), F32), jax.ShapeDtypeStruct((1, MLA_QK_PAD), F32)],
        compiler_params=_params("arbitrary"),
    )(dq, dk, dv, q_raw, kv_raw, z, *tabs, gq, gk)


_NT = (((1,), (1,)), ((), ()))
_TN = (((0,), (0,)), ((), ()))


def _flash_fwd(q, k, v, shards, kinds, tq=1024, tk=1024, ts=512):
    H, S, dq = q.shape
    dv = v.shape[-1]
    tq, tk, ts = _tile(S, tq), _tile(S, tk), _tile(S, ts)
    assert tq % tk == 0 and tq % ts == 0 and dv == LANES
    nq, nd = S // tq, tq // tk
    nw = len(shards)

    def body(q_ref, k_ref, v_ref, *rest):
        x_refs, (o_ref, lse_ref), g_refs = rest[:nw], rest[nw:nw + 2], rest[nw + 2:2 * nw + 2]
        m_sc, l_sc, acc_sc, send_sems, recv_sems, local_sems = rest[2 * nw + 2:]
        h, i = pl.program_id(0), pl.program_id(1)
        gather = _Gather(x_refs, g_refs, kinds, send_sems, recv_sems, local_sems)
        pl.when(jnp.logical_and(h == 0, i == 0))(gather.start)
        pl.when(jnp.logical_and(h == (3 * H) // 4, i == 0))(gather.forward)

        m_sc[...] = jnp.full_like(m_sc, -jnp.inf)
        l_sc[...] = jnp.zeros_like(l_sc)
        acc_sc[...] = jnp.zeros_like(acc_sc)

        def step(off, width, r0, masked):
            rows = slice(r0, tq)
            s = lax.dot_general(q_ref[rows, :], k_ref[pl.ds(off, width), :], _NT, preferred_element_type=F32)
            if masked:
                row = i * tq + r0 + lax.broadcasted_iota(jnp.int32, s.shape, 0)
                col = off + lax.broadcasted_iota(jnp.int32, s.shape, 1)
                s = jnp.where(col <= row, s, -jnp.inf)
            parts = [s[:, c * LANES:(c + 1) * LANES] for c in range(width // LANES)]
            m_cur = parts[0]
            for pt in parts[1:]:
                m_cur = jnp.maximum(m_cur, pt)
            m_prev = m_sc[rows, :]
            m_new = jnp.maximum(m_prev, jnp.max(m_cur, axis=-1, keepdims=True))
            alpha = jnp.exp(m_prev - m_new)
            ps = [jnp.exp(pt - m_new) for pt in parts]
            l_new = alpha * l_sc[rows, :]
            for pc in ps:
                l_new = l_new + pc
            l_sc[rows, :] = l_new
            p = jnp.concatenate(ps, axis=1).astype(BF16)
            acc_sc[rows, :] = alpha * acc_sc[rows, :] + jnp.dot(p, v_ref[pl.ds(off, width), :],
                                                                preferred_element_type=F32)
            m_sc[rows, :] = m_new

        def full_step(j, carry):
            step(pl.multiple_of(j * tk, tk), tk, 0, False)
            return carry

        lax.fori_loop(0, i * nd, full_step, 0)
        for d in range(tq // ts):
            step(pl.multiple_of(i * tq + d * ts, ts), ts, d * ts, True)
        l = jnp.sum(l_sc[...], axis=-1, keepdims=True)
        o_ref[...] = (acc_sc[...] / l).astype(BF16)
        lse_ref[...] = m_sc[...] + jnp.log(l)
        pl.when(jnp.logical_and(h == H - 1, i == nq - 1))(gather.finish)

    return pl.pallas_call(
        body, name="flash_fwd", grid=(H, nq),
        in_specs=[pl.BlockSpec((None, tq, dq), lambda h, i: (h, i, 0)),
                  pl.BlockSpec((None, S, dq), lambda h, i: (h, 0, 0)),
                  pl.BlockSpec((None, S, dv), lambda h, i: (h, 0, 0))] + [_ANY] * nw,
        out_specs=[pl.BlockSpec((tq, dv), lambda h, i: (i, h)),
                   pl.BlockSpec((None, tq, LANES), lambda h, i: (h, i, 0))] + [_ANY] * nw,
        out_shape=[jax.ShapeDtypeStruct((S, H * dv), BF16), jax.ShapeDtypeStruct((H, S, LANES), F32)]
        + [jax.ShapeDtypeStruct(_gather_out_shape(kd, sd.shape), sd.dtype) for kd, sd in zip(kinds, shards)],
        scratch_shapes=[pltpu.VMEM((tq, LANES), F32), pltpu.VMEM((tq, LANES), F32), pltpu.VMEM((tq, dv), F32),
                        pltpu.SemaphoreType.DMA((nw, 7)), pltpu.SemaphoreType.DMA((nw, 7)),
                        pltpu.SemaphoreType.DMA((nw,))],
        compiler_params=_params("arbitrary", "arbitrary"),
    )(q, k, v, *shards)


def _flash_bwd(q, k, v, dyc, cb0, o, lse, sends, tq=1024, tk=512):
    H, S, dq = q.shape
    dv = v.shape[-1]
    tq, tk = _tile(S, tq), _tile(S, tk)
    assert tq % tk == 0
    nq, nd = S // tq, tq // tk
    nw = len(sends)

    def body(q_ref, k_ref, v_ref, dy_ref, o_ref, lse_ref, *rest):
        s_refs, (dq_ref, dk_ref, dv_ref), r_refs = rest[:nw], rest[nw:nw + 3], rest[nw + 3:2 * nw + 3]
        do_sc, dl_sc, send_sems, recv_sems = rest[2 * nw + 3:]
        h, i = pl.program_id(0), pl.program_id(1)
        scatter = _Scatter(s_refs, r_refs, send_sems, recv_sems)
        pl.when(jnp.logical_and(h == 0, i == 0))(scatter.start)

        @pl.when(i == 0)
        def _():
            dk_ref[...] = jnp.zeros_like(dk_ref)
            dv_ref[...] = jnp.zeros_like(dv_ref)

        dq_ref[...] = jnp.zeros_like(dq_ref)
        dy = dy_ref[...]
        do_sc[...] = dy.astype(BF16)
        dl_sc[...] = jnp.broadcast_to(jnp.sum(dy * o_ref[...].astype(F32), axis=-1, keepdims=True), (tq, LANES))

        def step(j, r0, masked):
            off = pl.multiple_of(j * tk, tk)
            rows = slice(r0, tq)
            qv = q_ref[rows, :]
            dov = do_sc[rows, :]
            kb = k_ref[pl.ds(off, tk), :]
            s = lax.dot_general(qv, kb, _NT, preferred_element_type=F32)
            if masked:
                row = i * tq + r0 + lax.broadcasted_iota(jnp.int32, s.shape, 0)
                col = off + lax.broadcasted_iota(jnp.int32, s.shape, 1)
                s = jnp.where(col <= row, s, -jnp.inf)
            p = jnp.exp(s - lse_ref[rows, :1])
            dv_ref[pl.ds(off, tk), :] += lax.dot_general(p.astype(BF16), dov, _TN, preferred_element_type=F32)
            dp = lax.dot_general(dov, v_ref[pl.ds(off, tk), :], _NT, preferred_element_type=F32)
            ds = (p * (dp - dl_sc[rows, :1])).astype(BF16)
            dk_ref[pl.ds(off, tk), :] += lax.dot_general(ds, qv, _TN, preferred_element_type=F32)
            dq_ref[rows, :] += jnp.dot(ds, kb, preferred_element_type=F32)

        n_full = i * nd

        def full_step(j, carry):
            step(j, 0, False)
            return carry

        lax.fori_loop(0, n_full, full_step, 0)
        for d in range(nd):
            step(n_full + d, d * tk, True)
        pl.when(jnp.logical_and(h == H - 1, i == nq - 1))(scatter.finish)

    return pl.pallas_call(
        body, name="flash_bwd", grid=(H, nq),
        in_specs=[pl.BlockSpec((None, tq, dq), lambda h, i: (h, i, 0)),
                  pl.BlockSpec((None, S, dq), lambda h, i: (h, 0, 0)),
                  pl.BlockSpec((None, S, dv), lambda h, i: (h, 0, 0)),
                  pl.BlockSpec((tq, dv), lambda h, i: (i, h + cb0)),
                  pl.BlockSpec((tq, dv), lambda h, i: (i, h)),
                  pl.BlockSpec((None, tq, LANES), lambda h, i: (h, i, 0))] + [_ANY] * nw,
        out_specs=[pl.BlockSpec((None, tq, dq), lambda h, i: (h, i, 0)),
                   pl.BlockSpec((None, S, dq), lambda h, i: (h, 0, 0)),
                   pl.BlockSpec((None, S, dv), lambda h, i: (h, 0, 0))] + [_ANY] * nw,
        out_shape=[jax.ShapeDtypeStruct((H, S, dq), F32), jax.ShapeDtypeStruct((H, S, dq), F32),
                   jax.ShapeDtypeStruct((H, S, dv), F32)]
        + [jax.ShapeDtypeStruct((N_DEV - 1,) + sd.shape[1:], sd.dtype) for sd in sends],
        scratch_shapes=[pltpu.VMEM((tq, dv), BF16), pltpu.VMEM((tq, LANES), F32),
                        pltpu.SemaphoreType.DMA((nw, N_DEV - 1)), pltpu.SemaphoreType.DMA((nw, N_DEV - 1))],
        compiler_params=_params("arbitrary", "arbitrary"),
    )(q, k, v, dyc, o, lse, *sends)


def _mem_norm(mem, g):
    n = mem.shape[1]

    def body(m_ref, g_ref, o_ref):
        xhat, _ = _rms_hat(m_ref[...], n)
        o_ref[...] = (xhat * g_ref[...]).astype(BF16)

    return pl.pallas_call(body, name="mem_norm", out_shape=jax.ShapeDtypeStruct(mem.shape, BF16),
                          compiler_params=pltpu.CompilerParams(vmem_limit_bytes=VMEM_LIMIT))(mem, g)


def _memkv_prep(mkv, gk):
    M = mkv.shape[0]

    def body(mkv_ref, g_ref, k_ref, v_ref):
        for h in range(X_HEADS):
            cols = slice(h * X_HEAD_DIM, (h + 1) * X_HEAD_DIM)
            xhat, _ = _rms_hat(mkv_ref[:, cols], X_HEAD_DIM)
            k_ref[:, cols] = (xhat * g_ref[...]).astype(BF16)
        v_ref[...] = mkv_ref[:, X_WIDTH:].astype(BF16)

    return pl.pallas_call(
        body, name="memkv_prep",
        out_shape=[jax.ShapeDtypeStruct((M, X_WIDTH), BF16), jax.ShapeDtypeStruct((M, X_WIDTH), BF16)],
    )(mkv, gk)


def _memkv_bwd(dk, dv, mkv, gk):
    M = mkv.shape[0]

    def body(dk_ref, dv_ref, mkv_ref, g_ref, dmkv_ref, dg_ref):
        dg = jnp.zeros((1, X_HEAD_DIM), F32)
        for h in range(X_HEADS):
            cols = slice(h * X_HEAD_DIM, (h + 1) * X_HEAD_DIM)
            xhat, r = _rms_hat(mkv_ref[:, cols], X_HEAD_DIM)
            dx, dgh = _rms_bwd(dk_ref[:, cols], xhat, r, g_ref[...], X_HEAD_DIM)
            dmkv_ref[:, cols] = dx.astype(BF16)
            dg += dgh
        dmkv_ref[:, X_WIDTH:] = dv_ref[...].astype(BF16)
        dg_ref[...] = dg

    return pl.pallas_call(
        body, name="memkv_bwd",
        out_shape=[jax.ShapeDtypeStruct((M, 2 * X_WIDTH), BF16), jax.ShapeDtypeStruct((1, X_HEAD_DIM), F32)],
    )(dk, dv, mkv, gk)


def _mem_gain_bwd(mem, dmn):
    n = mem.shape[1]

    def body(m_ref, d_ref, dg_ref):
        xhat, _ = _rms_hat(m_ref[...], n)
        dg_ref[...] = jnp.sum(d_ref[...] * xhat, axis=0, keepdims=True)

    return pl.pallas_call(body, name="mem_gain_bwd", out_shape=jax.ShapeDtypeStruct((1, n), F32),
                          compiler_params=pltpu.CompilerParams(vmem_limit_bytes=VMEM_LIMIT))(mem, dmn)


def _memx_scores(zq_ref, g, kx_ref, h):
    cols = slice(h * X_HEAD_DIM, (h + 1) * X_HEAD_DIM)
    xhat, r = _rms_hat(zq_ref[:, cols], X_HEAD_DIM)
    qn = (xhat * g).astype(BF16)
    s = lax.dot_general(qn, kx_ref[:, cols], _NT, preferred_element_type=F32) * (1.0 / math.sqrt(X_HEAD_DIM))
    p = jnp.exp(s - jnp.max(s, axis=-1, keepdims=True))
    return cols, xhat, r, qn, p, jnp.sum(p, axis=-1, keepdims=True)


def _memx_fwd(z, g, kx, vx, tr=512):
    S = z.shape[0]
    tr = _tile(S, tr)
    M = kx.shape[0]

    def body(zq_ref, g_ref, kx_ref, vx_ref, y_ref):
        for h in range(X_HEADS):
            cols, _, _, _, p, l = _memx_scores(zq_ref, g_ref[...], kx_ref, h)
            o = jnp.dot(p.astype(BF16), vx_ref[:, cols], preferred_element_type=F32)
            y_ref[:, cols] = (o / l).astype(BF16)

    return pl.pallas_call(
        body, name="memx_fwd", grid=(S // tr,),
        in_specs=[_rows(tr, X_WIDTH, Z_MQ // X_WIDTH), _full((1, X_HEAD_DIM)), _full((M, X_WIDTH)),
                  _full((M, X_WIDTH))],
        out_specs=_rows(tr, X_WIDTH),
        out_shape=jax.ShapeDtypeStruct((S, X_WIDTH), BF16),
        compiler_params=_params("parallel"),
    )(z, g, kx, vx)


def _memx_bwd(z, g, kx, vx, dyc, dy_cb, tr=512):
    S = z.shape[0]
    tr = _tile(S, tr)
    M = kx.shape[0]
    scale = 1.0 / math.sqrt(X_HEAD_DIM)

    def body(zq_ref, g_ref, kx_ref, vx_ref, dy_ref, dz_ref, dk_ref, dv_ref, dg_ref):
        @pl.when(pl.program_id(0) == 0)
        def _():
            dk_ref[...] = jnp.zeros_like(dk_ref)
            dv_ref[...] = jnp.zeros_like(dv_ref)
            dg_ref[...] = jnp.zeros_like(dg_ref)

        gv = g_ref[...]
        for h in range(X_HEADS):
            cols, xhat, r, qn, p, l = _memx_scores(zq_ref, gv, kx_ref, h)
            p = p / l
            do = dy_ref[:, cols].astype(BF16)
            dv_ref[:, cols] += lax.dot_general(p.astype(BF16), do, _TN, preferred_element_type=F32)
            dp = lax.dot_general(do, vx_ref[:, cols], _NT, preferred_element_type=F32)
            ds = (p * (dp - jnp.sum(dp * p, axis=-1, keepdims=True)) * scale).astype(BF16)
            dqn = jnp.dot(ds, kx_ref[:, cols], preferred_element_type=F32)
            dk_ref[:, cols] += lax.dot_general(ds, qn, _TN, preferred_element_type=F32)
            dx, dg = _rms_bwd(dqn, xhat, r, gv, X_HEAD_DIM)
            dz_ref[:, cols] = dx.astype(BF16)
            dg_ref[...] += dg

    return pl.pallas_call(
        body, name="memx_bwd", grid=(S // tr,),
        in_specs=[_rows(tr, X_WIDTH, Z_MQ // X_WIDTH), _full((1, X_HEAD_DIM)), _full((M, X_WIDTH)),
                  _full((M, X_WIDTH)), _rows(tr, X_WIDTH, dy_cb)],
        out_specs=[_rows(tr, X_WIDTH), _full((M, X_WIDTH)), _full((M, X_WIDTH)), _full((1, X_HEAD_DIM))],
        out_shape=[jax.ShapeDtypeStruct((S, X_WIDTH), BF16), jax.ShapeDtypeStruct((M, X_WIDTH), F32),
                   jax.ShapeDtypeStruct((M, X_WIDTH), F32), jax.ShapeDtypeStruct((1, X_HEAD_DIM), F32)],
        compiler_params=_params("arbitrary"),
    )(z, g, kx, vx, dyc)


def _conv_gc(g_ext, w_ref, b_ref, n_ext):
    g1 = pltpu.roll(g_ext, 1, 0)
    g2 = pltpu.roll(g_ext, 2, 0)
    gc = b_ref[...] + w_ref[0:1, :] * g2
    gc = gc + w_ref[1:2, :] * g1
    gc = gc + w_ref[2:3, :] * g_ext
    return gc, g1, g2


def _convglu_fwd(g, u, cw, cb, tr=512, tc=512):
    S, F = g.shape
    tr, tc = _tile(S, tr), _tile(F, tc)
    rb = tr // CONV_HALO

    def body(g_ref, gp_ref, u_ref, w_ref, b_ref, a_ref):
        i = pl.program_id(1)
        halo = jnp.where(i == 0, 0.0, gp_ref[...].astype(F32))
        g_ext = jnp.concatenate([halo, g_ref[...].astype(F32)], axis=0)
        gc, _, _ = _conv_gc(g_ext, w_ref, b_ref, tr + CONV_HALO)
        gc = gc[CONV_HALO:]
        sig = 1.0 / (1.0 + jnp.exp(-gc))
        a_ref[...] = (gc * sig * u_ref[...].astype(F32)).astype(BF16)

    return pl.pallas_call(
        body, name="convglu_fwd", grid=(F // tc, S // tr),
        in_specs=[pl.BlockSpec((tr, tc), lambda j, i: (i, j)),
                  pl.BlockSpec((CONV_HALO, tc), lambda j, i: (jnp.maximum(i * rb - 1, 0), j)),
                  pl.BlockSpec((tr, tc), lambda j, i: (i, j)),
                  pl.BlockSpec((3, tc), lambda j, i: (0, j)), pl.BlockSpec((1, tc), lambda j, i: (0, j))],
        out_specs=pl.BlockSpec((tr, tc), lambda j, i: (i, j)),
        out_shape=jax.ShapeDtypeStruct((S, F), BF16),
        compiler_params=_params("parallel", "parallel"),
    )(g, g, u, cw, cb)


def _convglu_bwd(g, u, da, cw, cb, tr=512, tc=512):
    S, F = g.shape
    tr, tc = _tile(S, tr), _tile(F, tc)
    rb = tr // CONV_HALO
    nhb = S // CONV_HALO
    H = CONV_HALO

    def body(g_ref, gp_ref, gn_ref, u_ref, un_ref, da_ref, dan_ref, w_ref, b_ref,
             dg_ref, du_ref, dw_ref, db_ref):
        i = pl.program_id(1)
        last = i == pl.num_programs(1) - 1

        @pl.when(i == 0)
        def _():
            dw_ref[...] = jnp.zeros_like(dw_ref)
            db_ref[...] = jnp.zeros_like(db_ref)

        g_prev = jnp.where(i == 0, 0.0, gp_ref[...].astype(F32))
        g_cur = g_ref[...].astype(F32)
        g_ext = jnp.concatenate([g_prev, g_cur, gn_ref[...].astype(F32)], axis=0)
        gc, g1, g2 = _conv_gc(g_ext, w_ref, b_ref, tr + 2 * H)
        gc = gc[H:]
        ux = jnp.concatenate([u_ref[...], un_ref[...]], axis=0).astype(F32)
        dax = jnp.concatenate([da_ref[...].astype(F32), jnp.where(last, 0.0, dan_ref[...].astype(F32))], axis=0)
        sig = 1.0 / (1.0 + jnp.exp(-gc))
        du_ref[...] = (dax[:tr] * (gc[:tr] * sig[:tr])).astype(BF16)
        dgc = dax * ux * (sig * (1.0 + gc * (1.0 - sig)))
        n = tr + H
        d1 = pltpu.roll(dgc, n - 1, 0)[:tr]
        d2 = pltpu.roll(dgc, n - 2, 0)[:tr]
        d0 = dgc[:tr]
        dg_ref[...] = (w_ref[2:3, :] * d0 + w_ref[1:2, :] * d1 + w_ref[0:1, :] * d2).astype(BF16)
        db_ref[...] += jnp.sum(d0, axis=0, keepdims=True)
        dw_ref[0:1, :] += jnp.sum(d0 * g2[H:H + tr], axis=0, keepdims=True)
        dw_ref[1:2, :] += jnp.sum(d0 * g1[H:H + tr], axis=0, keepdims=True)
        dw_ref[2:3, :] += jnp.sum(d0 * g_cur, axis=0, keepdims=True)

    cur = pl.BlockSpec((tr, tc), lambda j, i: (i, j))
    prv = pl.BlockSpec((H, tc), lambda j, i: (jnp.maximum(i * rb - 1, 0), j))
    nxt = pl.BlockSpec((H, tc), lambda j, i: (jnp.minimum((i + 1) * rb, nhb - 1), j))
    return pl.pallas_call(
        body, name="convglu_bwd", grid=(F // tc, S // tr),
        in_specs=[cur, prv, nxt, cur, nxt, cur, nxt,
                  pl.BlockSpec((3, tc), lambda j, i: (0, j)), pl.BlockSpec((1, tc), lambda j, i: (0, j))],
        out_specs=[cur, cur, pl.BlockSpec((3, tc), lambda j, i: (0, j)), pl.BlockSpec((1, tc), lambda j, i: (0, j))],
        out_shape=[jax.ShapeDtypeStruct((S, F), BF16), jax.ShapeDtypeStruct((S, F), BF16),
                   jax.ShapeDtypeStruct((3, F), F32), jax.ShapeDtypeStruct((1, F), F32)],
        compiler_params=_params("parallel", "arbitrary"),
    )(g, g, g, u, u, da, da, cw, cb)


def _mm_down_loss(a, w, x1, target, tm=1024, tn=512):
    S, K = a.shape
    n = w.shape[1]
    tm, tn = _tile(S, tm), _tile(n, tn)

    def body(a_ref, w_ref, x_ref, t_ref, dy_ref, dyb_ref, loss_ref):
        @pl.when(jnp.logical_and(pl.program_id(0) == 0, pl.program_id(1) == 0))
        def _():
            loss_ref[...] = jnp.zeros_like(loss_ref)

        y = jnp.dot(a_ref[...], w_ref[...], preferred_element_type=F32) + x_ref[...]
        err = y - t_ref[...]
        dy = err / n
        dy_ref[...] = dy
        dyb_ref[...] = dy.astype(BF16)
        loss_ref[...] += 0.5 * jnp.sum(jnp.sum(err * err, axis=-1, keepdims=True) / n)

    tile = pl.BlockSpec((tm, tn), lambda i, j: (i, j))
    return pl.pallas_call(
        body, name="mm_down_loss", grid=(S // tm, n // tn),
        in_specs=[pl.BlockSpec((tm, K), lambda i, j: (i, 0)), pl.BlockSpec((K, tn), lambda i, j: (0, j)), tile, tile],
        out_specs=[tile, tile, pl.BlockSpec((8, LANES), lambda i, j: (0, 0))],
        out_shape=[jax.ShapeDtypeStruct((S, n), F32), jax.ShapeDtypeStruct((S, n), BF16),
                   jax.ShapeDtypeStruct((8, LANES), F32)],
        compiler_params=_params("arbitrary", "arbitrary"),
    )(a, w, x1, target)


def _mm_o_rms(ycat, w, x, g, tm=512):
    S, K = ycat.shape
    n = w.shape[1]
    tm = _tile(S, tm)

    def body(y_ref, w_ref, x_ref, g_ref, x1_ref, h2_ref):
        x1 = jnp.dot(y_ref[...], w_ref[...], preferred_element_type=F32) + x_ref[...]
        x1_ref[...] = x1
        xhat, _ = _rms_hat(x1, n)
        h2_ref[...] = (xhat * g_ref[...]).astype(BF16)

    return pl.pallas_call(
        body, name="mm_o_rms", grid=(S // tm,),
        in_specs=[_rows(tm, K), _full((K, n)), _rows(tm, n), _full((1, n))],
        out_specs=[_rows(tm, n), _rows(tm, n)],
        out_shape=[jax.ShapeDtypeStruct((S, n), F32), jax.ShapeDtypeStruct((S, n), BF16)],
        compiler_params=_params("parallel"),
    )(ycat, w, x, g)


def _local_step(x, mem, target, W, late_shards):
    S = x.shape[0]
    tabs = _rope_tables(S)
    W = dict(W)
    G = {}

    h = _rms_fwd("rms1_fwd", x, W["g_mix"])
    z = _mm("mm_in", h, W["w_in"], tm=512, tn=Z_COLS, tk=D_MODEL)
    y_pool = _pool_fwd(z, W["w_pool"], W["pool_scale"])
    ql, kvl = _lat_fwd(z, W["g_q_lat"], W["g_kv_lat"])
    q_raw = _mm("mm_q_up", ql, W["w_q_up"], tm=1024, tn=2048, tk=Q_LORA)
    kv_raw = _mm("mm_kv_up", kvl, W["w_kv_up"], tm=1024, tn=2048, tk=KV_LORA)
    n_ffn = len(FFN_GATHER_KINDS)
    q, k, v, W["w_o"], W["w_mem_kv"] = _mla_prep(q_raw, kv_raw, z, tabs, W["g_q_mla"], W["g_k_mla"],
                                                 late_shards[n_ffn:], MIXER_GATHER_KINDS)
    o_mla, lse, wg3, wu3, W["w_down"], cw3 = _flash_fwd(q, k, v, late_shards[:n_ffn], FFN_GATHER_KINDS)
    W["w_gate"] = jnp.transpose(wg3, (1, 0, 2)).reshape(D_MODEL, D_FF)
    W["w_up"] = jnp.transpose(wu3, (1, 0, 2)).reshape(D_MODEL, D_FF)
    cw = jnp.sum(cw3.reshape(N_DEV, 3, 3, D_FF // N_DEV).astype(F32), axis=1)
    W["conv_w"] = jnp.transpose(cw, (1, 0, 2)).reshape(3, D_FF)
    mn = _mem_norm(mem, W["g_mem"])
    mkv = _mm("mm_mem_kv", mn, W["w_mem_kv"], tm=256, tn=1024, tk=D_MODEL)
    kx, vx = _memkv_prep(mkv, W["g_k_x"])
    y_mem = _memx_fwd(z, W["g_q_x"], kx, vx)
    ycat = jnp.concatenate([y_pool, o_mla, y_mem], axis=1)
    x1, h2 = _mm_o_rms(ycat, W["w_o"], x, W["g_ffn"])
    g = _mm("mm_gate", h2, W["w_gate"], out_dtype=BF16, tm=1024, tn=512, tk=D_MODEL)
    u = _mm("mm_up", h2, W["w_up"], out_dtype=BF16, tm=1024, tn=512, tk=D_MODEL)
    a = _convglu_fwd(g, u, W["conv_w"], W["conv_b"])
    dy, dyb, loss_part = _mm_down_loss(a, W["w_down"], x1, target)

    da = _mm("mm_down_dx", dyb, W["w_down"], tb=True, out_dtype=BF16, tm=1024, tn=512, tk=D_MODEL)
    G["w_down"] = _mm("mm_down_dw", a, dyb, ta=True, tm=512, tn=2048, tk=2048)
    dg, du, G["conv_w"], G["conv_b"] = _convglu_bwd(g, u, da, W["conv_w"], W["conv_b"])
    dh2 = _mm("mm_gate_dx", dg, W["w_gate"], tb=True, tm=1024, tn=512, tk=D_FF)
    dh2 = _mm("mm_up_dx", du, W["w_up"], tb=True, add=dh2, tm=1024, tn=512, tk=D_FF)
    G["w_gate"] = _mm("mm_gate_dw", h2, dg, ta=True, tm=1024, tn=1408, tk=2048)
    G["w_up"] = _mm("mm_up_dw", h2, du, ta=True, tm=1024, tn=1408, tk=2048)
    dx1, dx1b, G["g_ffn"] = _rms_bwd_call("rms2_bwd", x1, W["g_ffn"], dh2, dy, True)

    dyc = _mm("mm_o_dx", dx1b, W["w_o"], tb=True, tm=512, tn=2048, tk=D_MODEL)
    G["w_o"] = _mm("mm_o_dw", ycat, dx1b, ta=True, tm=1024, tn=1024, tk=2048)
    dz_pool, G["w_pool"], G["pool_scale"] = _pool_bwd(z, dyc, W["w_pool"], W["pool_scale"], dy_cb=0)
    dz_mq, dkx, dvx, G["g_q_x"] = _memx_bwd(z, W["g_q_x"], kx, vx, dyc, dy_cb=3)
    dmkv, G["g_k_x"] = _memkv_bwd(dkx, dvx, mkv, W["g_k_x"])
    G["w_mem_kv"] = _mm("mm_mem_kv_dw", mn, dmkv, ta=True, tm=1024, tn=1024, tk=MEM_LEN)
    dmn = _mm("mm_mem_kv_dx", dmkv, W["w_mem_kv"], tb=True, tm=256, tn=2048, tk=1024)
    G["g_mem"] = _mem_gain_bwd(mem, dmn)
    sends = [_send_blocks(n, G[n]) for n in SCATTER_EARLY]
    dq, dk, dv, *got_early = _flash_bwd(q, k, v, dyc, POOL_WIDTH // MLA_V, o_mla, lse, sends)
    dq_raw, dkv_raw, dz_kr, G["g_q_mla"], G["g_k_mla"] = _mla_prep_bwd(
        dq, dk, dv, q_raw, kv_raw, z, tabs, W["g_q_mla"], W["g_k_mla"])
    G["w_q_up"] = _mm("mm_q_up_dw", ql, dq_raw, ta=True, tm=512, tn=2048, tk=1024)
    dql = _mm("mm_q_up_dx", dq_raw, W["w_q_up"], tb=True, tm=1024, tn=512, tk=2048)
    G["w_kv_up"] = _mm("mm_kv_up_dw", kvl, dkv_raw, ta=True, tm=256, tn=2048, tk=1024)
    dkvl = _mm("mm_kv_up_dx", dkv_raw, W["w_kv_up"], tb=True, tm=1024, tn=256, tk=2048)
    dz_q, dz_kv, G["g_q_lat"], G["g_kv_lat"] = _lat_bwd(z, W["g_q_lat"], W["g_kv_lat"], dql, dkvl)
    dz = jnp.concatenate([dz_pool, dz_q, dz_mq, dz_kv, dz_kr], axis=1)
    G["w_in"] = _mm("mm_in_dw", h, dz, ta=True, tm=1024, tn=Z_COLS, tk=1024)
    dh, *got_late = _mm("mm_in_dx", dz, W["w_in"], tb=True, tm=512, tn=2048, tk=Z_COLS,
                        sends=[_send_blocks(n, G[n]) for n in SCATTER_LATE])
    grad_x, G["g_mix"] = _rms_bwd_call("rms1_bwd", x, W["g_mix"], dh, dx1, False)
    return loss_part, grad_x, G, dict(zip(SCATTER_EARLY + SCATTER_LATE, got_early + got_late))


_ANY = pl.BlockSpec(memory_space=pl.ANY)
_MESH = pl.DeviceIdType.MESH


def _gather_out_shape(kind, shape):
    if kind == "rows":
        return (N_DEV * shape[0],) + tuple(shape[1:])
    if kind == "cols":
        return (shape[0], N_DEV * shape[1])
    return (N_DEV,) + tuple(shape)


def _gather_view(ref, kind, shape, d):
    if kind == "rows":
        return ref.at[pl.ds(pl.multiple_of(d * shape[0], 16), shape[0]), :]
    if kind == "cols":
        return ref.at[:, pl.ds(pl.multiple_of(d * shape[1], LANES), shape[1])]
    return ref.at[d]


class _Gather:
    def __init__(self, x_refs, out_refs, kinds, send_sems, recv_sems, local_sems):
        self.xr, self.outr, self.kinds = x_refs, out_refs, kinds
        self.ss, self.rs, self.ls = send_sems, recv_sems, local_sems
        x, y, c = lax.axis_index("x"), lax.axis_index("y"), lax.axis_index("c")
        self.c = c
        self.me, self.sibling = (x, y, c), (x, y, 1 - c)
        self.chips = [(1 - x, y), (x, 1 - y), (1 - x, 1 - y)]

    def _view(self, w, dev):
        px, py, pc = dev
        return _gather_view(self.outr[w], self.kinds[w], self.xr[w].shape, 4 * px + 2 * py + pc)

    def _copy(self, w, k, block, to, from_shard=False):
        v = self._view(w, block)
        return pltpu.make_async_remote_copy(
            src_ref=self.xr[w] if from_shard else v, dst_ref=v, send_sem=self.ss.at[w, k],
            recv_sem=self.rs.at[w, k], device_id=to, device_id_type=_MESH)

    def _local(self, w):
        return pltpu.make_async_copy(self.xr[w], self._view(w, self.me), self.ls.at[w])

    def start(self):
        for w in range(len(self.xr)):
            self._local(w).start()
            self._copy(w, 0, self.me, self.sibling, True).start()
            for j, chip in enumerate(self.chips):
                self._copy(w, 1 + j, self.me, (*chip, self.c), True).start()

    def forward(self):
        for j, chip in enumerate(self.chips):
            for w in range(len(self.xr)):
                self._copy(w, 1 + j, (*chip, self.c), self.me).wait_recv()
                self._copy(w, 4 + j, (*chip, self.c), self.sibling).start()

    def finish(self):
        for w in range(len(self.xr)):
            self._copy(w, 0, self.sibling, self.me).wait_recv()
            for j, chip in enumerate(self.chips):
                self._copy(w, 4 + j, (*chip, 1 - self.c), self.me).wait_recv()
            self._copy(w, 0, self.me, self.sibling, True).wait_send()
            for j, chip in enumerate(self.chips):
                self._copy(w, 1 + j, self.me, (*chip, self.c), True).wait_send()
                self._copy(w, 4 + j, (*chip, self.c), self.sibling).wait_send()
            self._local(w).wait()


class _Scatter:
    def __init__(self, send_refs, recv_refs, send_sems, recv_sems):
        self.sr, self.rr, self.ss, self.rs = send_refs, recv_refs, send_sems, recv_sems
        self.xyz = lax.axis_index("x"), lax.axis_index("y"), lax.axis_index("c")

    def _copy(self, w, k):
        x, y, c = self.xyz
        px, py, pc = x ^ ((k >> 2) & 1), y ^ ((k >> 1) & 1), c ^ (k & 1)
        return pltpu.make_async_remote_copy(
            src_ref=self.sr[w].at[4 * px + 2 * py + pc], dst_ref=self.rr[w].at[k - 1],
            send_sem=self.ss.at[w, k - 1], recv_sem=self.rs.at[w, k - 1],
            device_id=(px, py, pc), device_id_type=_MESH)

    def _all(self):
        return [self._copy(w, k) for w in range(len(self.sr)) for k in range(1, N_DEV)]

    def start(self):
        for cp in self._all():
            cp.start()

    def finish(self):
        for cp in self._all():
            cp.wait_recv()
        for cp in self._all():
            cp.wait_send()


def _gather_call(shards, kinds):
    nw = len(shards)

    def body(*refs):
        gather = _Gather(refs[:nw], refs[nw:2 * nw], kinds, *refs[2 * nw:])
        gather.start()
        gather.forward()
        gather.finish()

    return pl.pallas_call(
        body, name="gather_mixer_weights",
        out_shape=[jax.ShapeDtypeStruct(_gather_out_shape(kd, sd.shape), sd.dtype) for kd, sd in zip(kinds, shards)],
        in_specs=[_ANY] * nw, out_specs=[_ANY] * nw,
        scratch_shapes=[pltpu.SemaphoreType.DMA((nw, 7)), pltpu.SemaphoreType.DMA((nw, 7)),
                        pltpu.SemaphoreType.DMA((nw,))],
    )(*shards)


def _exchange_small(s):
    def body(x_ref, out_ref, x_send, x_recv, local_sem):
        x, y, c = lax.axis_index("x"), lax.axis_index("y"), lax.axis_index("c")
        me = 4 * x + 2 * y + c
        mine = pltpu.make_async_copy(x_ref, out_ref.at[me], local_sem)
        mine.start()

        def copy(k):
            px, py, pc = x ^ ((k >> 2) & 1), y ^ ((k >> 1) & 1), c ^ (k & 1)
            return pltpu.make_async_remote_copy(
                src_ref=x_ref, dst_ref=out_ref.at[me], send_sem=x_send.at[k - 1], recv_sem=x_recv.at[k - 1],
                device_id=(px, py, pc), device_id_type=_MESH)

        cps = [copy(k) for k in range(1, N_DEV)]
        for cp in cps:
            cp.start()
        for cp in cps:
            cp.wait_recv()
        for cp in cps:
            cp.wait_send()
        mine.wait()

    return pl.pallas_call(
        body, name="exchange_small", out_shape=jax.ShapeDtypeStruct((N_DEV,) + s.shape, s.dtype),
        in_specs=[_ANY], out_specs=_ANY,
        scratch_shapes=[pltpu.SemaphoreType.DMA((N_DEV - 1,)), pltpu.SemaphoreType.DMA((N_DEV - 1,)),
                        pltpu.SemaphoreType.DMA],
    )(s)


def _sum_slots(buf, tr=208):
    n, R, _ = buf.shape
    tr = tr if R % tr == 0 else R

    def body(b_ref, o_ref):
        acc = b_ref[0]
        for d in range(1, n):
            acc = acc + b_ref[d]
        o_ref[...] = acc

    return pl.pallas_call(
        body, name="sum_slots", grid=(R // tr,),
        in_specs=[pl.BlockSpec((n, tr, LANES), lambda i: (0, i, 0))],
        out_specs=pl.BlockSpec((tr, LANES), lambda i: (i, 0)),
        out_shape=jax.ShapeDtypeStruct((R, LANES), buf.dtype),
        compiler_params=_params("parallel"),
    )(buf)


def _adamw(name, w, g, got, m, v, tr=128):
    R, C = w.shape
    tr = max([t for t in range(16, tr + 1, 16) if R % t == 0], default=R) if R > tr else R
    has_got = got is not None

    def body(*refs):
        if has_got:
            w_ref, g_ref, got_ref, m_ref, v_ref, go_ref, d_ref, mo_ref, vo_ref = refs
        else:
            w_ref, g_ref, m_ref, v_ref, go_ref, d_ref, mo_ref, vo_ref = refs
        gv = g_ref[...]
        if has_got:
            for k in range(N_DEV - 1):
                gv = gv + got_ref[k].astype(F32)
        m_new = ADAM_B1 * m_ref[...] + (1.0 - ADAM_B1) * gv
        v_new = ADAM_B2 * v_ref[...] + (1.0 - ADAM_B2) * (gv * gv)
        m_hat = m_new / (1.0 - ADAM_B1 ** ADAM_STEP)
        v_hat = v_new / (1.0 - ADAM_B2 ** ADAM_STEP)
        go_ref[...] = gv
        d_ref[...] = -ADAM_LR * (m_hat / (jnp.sqrt(v_hat) + ADAM_EPS) + ADAM_WD * w_ref[...])
        mo_ref[...] = m_new
        vo_ref[...] = v_new

    spec = pl.BlockSpec((tr, C), lambda i: (i, 0))
    got_spec = [pl.BlockSpec((N_DEV - 1, tr, C), lambda i: (0, i, 0))] if has_got else []
    sds = jax.ShapeDtypeStruct((R, C), F32)
    ins = [w, g] + ([got] if has_got else []) + [m, v]
    return pl.pallas_call(
        body, name=name, grid=(R // tr,), in_specs=[spec] * 2 + got_spec + [spec] * 2, out_specs=[spec] * 4,
        out_shape=[sds] * 4, compiler_params=_params("parallel"),
    )(*ins)


def _pad_rows(a, mult):
    r = (-a.shape[0]) % mult
    return a if r == 0 else jnp.concatenate([a, jnp.zeros((r,) + a.shape[1:], a.dtype)], axis=0)


def _as_rows(a, mult):
    flat = a.reshape(-1)
    r = (-flat.shape[0]) % LANES
    if r:
        flat = jnp.concatenate([flat, jnp.zeros((r,), a.dtype)])
    return _pad_rows(flat.reshape(-1, LANES), mult)


def _w_in_to_kernel_cols(w):
    pad = jnp.zeros(w.shape[:-1] + (Z_COLS - IN_COLS,), w.dtype)
    return jnp.concatenate([w[..., :1024], w[..., 1344:1856], w[..., 1024:1344], pad], axis=-1)


def _w_in_from_kernel_cols(w):
    return jnp.concatenate([w[..., :1024], w[..., 1536:1856], w[..., 1024:1536]], axis=-1)


def _dev_blocks(n, g):
    ffs = D_FF // N_DEV
    if n == "w_in":
        return _w_in_from_kernel_cols(g).reshape(N_DEV, D_MODEL // N_DEV, IN_COLS)
    if n == "w_q_up":
        return jnp.transpose(g.reshape(Q_LORA, N_DEV, MLA_QK_PAD)[:, :, :MLA_QK], (1, 0, 2))
    if n == "w_kv_up":
        return jnp.transpose(g.reshape(KV_LORA, N_DEV, MLA_NOPE + MLA_V), (1, 0, 2))
    if n in ("w_mem_kv", "w_o"):
        return g.reshape(N_DEV, D_MODEL // N_DEV, g.shape[1])
    if n in ("w_gate", "w_up"):
        return jnp.transpose(g.reshape(D_MODEL, N_DEV, ffs), (1, 0, 2))
    if n == "conv_w":
        return jnp.transpose(g.reshape(3, N_DEV, ffs), (1, 0, 2))
    assert n == "w_down"
    return g.reshape(N_DEV, ffs, D_MODEL)


def _send_blocks(n, g):
    blocks = _dev_blocks(n, g)
    return blocks if n == "conv_w" else blocks.astype(BF16)


def kernel(x, mem, g_mix, w_in, g_q_lat, w_q_up, g_kv_lat, w_kv_up, g_q_mla, g_k_mla, w_pool, pool_scale, g_mem, w_mem_kv, g_q_x, g_k_x, w_o, g_ffn, w_gate, w_up, conv_w, conv_b, w_down, loss_target, m_g_mix, m_w_in, m_g_q_lat, m_w_q_up, m_g_kv_lat, m_w_kv_up, m_g_q_mla, m_g_k_mla, m_w_pool, m_pool_scale, m_g_mem, m_w_mem_kv, m_g_q_x, m_g_k_x, m_w_o, m_g_ffn, m_w_gate, m_w_up, m_conv_w, m_conv_b, m_w_down, v_g_mix, v_w_in, v_g_q_lat, v_w_q_up, v_g_kv_lat, v_w_kv_up, v_g_q_mla, v_g_k_mla, v_w_pool, v_pool_scale, v_g_mem, v_w_mem_kv, v_g_q_x, v_g_k_x, v_w_o, v_g_ffn, v_w_gate, v_w_up, v_conv_w, v_conv_b, v_w_down):
    given = dict(g_mix=g_mix, w_in=w_in, g_q_lat=g_q_lat, w_q_up=w_q_up, g_kv_lat=g_kv_lat, w_kv_up=w_kv_up,
                 g_q_mla=g_q_mla, g_k_mla=g_k_mla, w_pool=w_pool, pool_scale=pool_scale, g_mem=g_mem,
                 w_mem_kv=w_mem_kv, g_q_x=g_q_x, g_k_x=g_k_x, w_o=w_o, g_ffn=g_ffn, w_gate=w_gate, w_up=w_up,
                 conv_w=conv_w, conv_b=conv_b, w_down=w_down)
    mom_m = dict(g_mix=m_g_mix, w_in=m_w_in, g_q_lat=m_g_q_lat, w_q_up=m_w_q_up, g_kv_lat=m_g_kv_lat,
                 w_kv_up=m_w_kv_up, g_q_mla=m_g_q_mla, g_k_mla=m_g_k_mla, w_pool=m_w_pool,
                 pool_scale=m_pool_scale, g_mem=m_g_mem, w_mem_kv=m_w_mem_kv, g_q_x=m_g_q_x, g_k_x=m_g_k_x,
                 w_o=m_w_o, g_ffn=m_g_ffn, w_gate=m_w_gate, w_up=m_w_up, conv_w=m_conv_w, conv_b=m_conv_b,
                 w_down=m_w_down)
    mom_v = dict(g_mix=v_g_mix, w_in=v_w_in, g_q_lat=v_g_q_lat, w_q_up=v_w_q_up, g_kv_lat=v_g_kv_lat,
                 w_kv_up=v_w_kv_up, g_q_mla=v_g_q_mla, g_k_mla=v_g_k_mla, w_pool=v_w_pool,
                 pool_scale=v_pool_scale, g_mem=v_g_mem, w_mem_kv=v_w_mem_kv, g_q_x=v_g_q_x, g_k_x=v_g_k_x,
                 w_o=v_w_o, g_ffn=v_g_ffn, w_gate=v_w_gate, w_up=v_w_up, conv_w=v_conv_w, conv_b=v_conv_b,
                 w_down=v_w_down)
    drop = lambda a: a[0] if a.ndim > 2 else a
    sh = {n: drop(given[n]) for n in WEIGHTS}
    mom_m = {n: drop(mom_m[n]) for n in WEIGHTS}
    mom_v = {n: drop(mom_v[n]) for n in WEIGHTS}

    cw_hi = sh["conv_w"].astype(BF16)
    cw_r = sh["conv_w"] - cw_hi.astype(F32)
    cw_mid = cw_r.astype(BF16)
    cw_lo = (cw_r - cw_mid.astype(F32)).astype(BF16)
    early_shards = [
        _w_in_to_kernel_cols(sh["w_in"]).astype(BF16),
        jnp.pad(sh["w_q_up"], ((0, 0), (0, MLA_QK_PAD - MLA_QK))).astype(BF16),
        sh["w_kv_up"].astype(BF16),
    ]
    late_shards = [sh["w_gate"].astype(BF16), sh["w_up"].astype(BF16), sh["w_down"].astype(BF16),
                   jnp.concatenate([cw_hi, cw_mid, cw_lo], axis=0), sh["w_o"].astype(BF16),
                   sh["w_mem_kv"].astype(BF16)]
    W = dict(zip(("w_in", "w_q_up", "w_kv_up"), _gather_call(early_shards, ("rows", "cols", "cols"))))
    W["w_pool"] = sh["w_pool"].astype(BF16)
    for n in ("g_mix", "g_q_lat", "g_kv_lat", "pool_scale", "g_mem", "g_q_x", "g_k_x", "g_ffn", "conv_b"):
        W[n] = sh[n]
    pad_qk = lambda gv: jnp.pad(gv, ((0, 0), (0, MLA_QK_PAD - MLA_QK)))
    W["g_q_mla"], W["g_k_mla"] = pad_qk(sh["g_q_mla"]), pad_qk(sh["g_k_mla"])

    loss_part, grad_x, G, got = _local_step(x[0], mem[0], loss_target[0], W, late_shards)

    small = {
        "g_mix": G["g_mix"], "g_q_lat": G["g_q_lat"], "g_kv_lat": G["g_kv_lat"],
        "g_q_mla": G["g_q_mla"][:, :MLA_QK], "g_k_mla": G["g_k_mla"][:, :MLA_QK],
        "w_pool": G["w_pool"], "pool_scale": G["pool_scale"], "g_mem": G["g_mem"],
        "g_q_x": G["g_q_x"], "g_k_x": G["g_k_x"], "g_ffn": G["g_ffn"], "conv_b": G["conv_b"],
    }
    s_offs = {}
    segs = []
    off = 0
    for n in SMALL:
        r = _as_rows(small[n], 8)
        s_offs[n] = off
        off += r.shape[0]
        segs.append(r)
    segs.append(loss_part)
    loss_row = off
    sbuf = jnp.concatenate(segs, axis=0)
    s_sum = _sum_slots(_exchange_small(sbuf))

    def small_take(buf, n):
        shape = sh[n].shape
        cnt = math.prod(shape)
        return buf[s_offs[n]:s_offs[n] + -(-cnt // LANES)].reshape(-1)[:cnt].reshape(shape)

    loss = s_sum[loss_row, 0]

    me = 4 * lax.axis_index("x") + 2 * lax.axis_index("y") + lax.axis_index("c")
    grads, deltas, new_m, new_v = {}, {}, {}, {}
    for n in BIG:
        shape = sh[n].shape
        own = lax.dynamic_index_in_dim(_dev_blocks(n, G[n]), me, 0, keepdims=False)
        flat = lambda a: a.reshape(-1, shape[-1])
        outs = _adamw("adamw_" + n, flat(sh[n]), flat(own), got[n].reshape(N_DEV - 1, -1, shape[-1]),
                      flat(mom_m[n]), flat(mom_v[n]))
        grads[n], deltas[n], new_m[n], new_v[n] = (o.reshape(shape) for o in outs)

    def pack_small(src):
        return jnp.concatenate([_as_rows(src[n], 8) for n in SMALL], axis=0)

    _, d_s, m_s, v_s = _adamw("adamw_small", pack_small(sh), s_sum[:loss_row], None,
                              pack_small(mom_m), pack_small(mom_v))
    for n in SMALL:
        grads[n] = small_take(s_sum, n)
        deltas[n], new_m[n], new_v[n] = small_take(d_s, n), small_take(m_s, n), small_take(v_s, n)

    lead = lambda n, a: a.reshape(given[n].shape)
    return (loss, grad_x[None],
            *[lead(n, grads[n]) for n in WEIGHTS], *[lead(n, deltas[n]) for n in WEIGHTS],
            *[lead(n, new_m[n]) for n in WEIGHTS], *[lead(n, new_v[n]) for n in WEIGHTS])
```

```python
import math

import jax
import jax.numpy as jnp
from jax import lax
from jax.experimental import pallas as pl
from jax.experimental.pallas import tpu as pltpu

F32 = jnp.float32
BF16 = jnp.bfloat16

D_MODEL = 2048
D_FF = 5632
POOL_WIDTH = 512
POOL_WINDOWS = (2, 4, 8, 16)
POOL_HALO = 16
MLA_HEADS = 8
MLA_NOPE = 128
MLA_ROPE = 64
MLA_QK = MLA_NOPE + MLA_ROPE
MLA_QK_PAD = 256
MLA_V = 128
Q_LORA = 512
KV_LORA = 256
X_HEADS = 4
X_HEAD_DIM = 128
X_WIDTH = 512
MEM_LEN = 256
ROPE_THETA = 10000.0
NORM_EPS = 1e-6
CONV_HALO = 16
IN_COLS = 1856
Z_COLS = 1920
Z_POOL, Z_Q, Z_MQ, Z_KV, Z_KR = 0, 512, 1024, 1536, 1792

ADAM_LR = 0.001
ADAM_B1 = 0.9
ADAM_B2 = 0.999
ADAM_EPS = 1e-08
ADAM_WD = 0.01
ADAM_STEP = 10

N_DEV = 8
LANES = 128
VMEM_LIMIT = 56 * 1024 * 1024

BIG = ("w_in", "w_q_up", "w_kv_up", "w_mem_kv", "w_o", "w_gate", "w_up", "conv_w", "w_down")
SCATTER_EARLY = ("w_down", "w_gate", "w_up", "conv_w", "w_o", "w_mem_kv")
SCATTER_LATE = ("w_in", "w_q_up", "w_kv_up")
EARLY_GATHER_KINDS = ("rows", "cols", "cols")
FFN_GATHER_KINDS = ("lead", "lead", "rows", "lead")
MIXER_GATHER_KINDS = ("rows", "rows")
SMALL = ("g_mix", "g_q_lat", "g_kv_lat", "g_q_mla", "g_k_mla", "w_pool", "pool_scale", "g_mem",
         "g_q_x", "g_k_x", "g_ffn", "conv_b")
WEIGHTS = ("g_mix", "w_in", "g_q_lat", "w_q_up", "g_kv_lat", "w_kv_up", "g_q_mla", "g_k_mla", "w_pool",
           "pool_scale", "g_mem", "w_mem_kv", "g_q_x", "g_k_x", "w_o", "g_ffn", "w_gate", "w_up", "conv_w",
           "conv_b", "w_down")


def _tile(n, t):
    if n <= t:
        return n
    for c in range(t - t % LANES, 0, -LANES):
        if n % c == 0:
            return c
    return n


def _params(*sem):
    return pltpu.CompilerParams(dimension_semantics=sem, vmem_limit_bytes=VMEM_LIMIT)


def _full(shape):
    nd = len(shape)
    return pl.BlockSpec(shape, lambda *_: (0,) * nd)


def _rows(tr, w, cb=0):
    return pl.BlockSpec((tr, w), lambda i: (i, cb))


def _rms_hat(x, n):
    r = lax.rsqrt(jnp.sum(x * x, axis=-1, keepdims=True) / n + NORM_EPS)
    return x * r, r


def _rms_bwd(dy, xhat, r, g, n):
    dxh = dy * g
    dx = r * (dxh - xhat * (jnp.sum(dxh * xhat, axis=-1, keepdims=True) / n))
    return dx, jnp.sum(dy * xhat, axis=0, keepdims=True)


def _mm(name, a, b, *, ta=False, tb=False, add=None, out_dtype=F32, tm=512, tn=512, tk=512):
    if ta:
        K, M = a.shape
    else:
        M, K = a.shape
    if tb:
        N, K2 = b.shape
    else:
        K2, N = b.shape
    assert K == K2, (name, a.shape, b.shape)
    tm, tn, tk = _tile(M, tm), _tile(N, tn), _tile(K, tk)
    nk = K // tk
    grid = (M // tm, N // tn, nk)
    a_spec = pl.BlockSpec((tk, tm), lambda i, j, k: (k, i)) if ta else pl.BlockSpec((tm, tk), lambda i, j, k: (i, k))
    b_spec = pl.BlockSpec((tn, tk), lambda i, j, k: (j, k)) if tb else pl.BlockSpec((tk, tn), lambda i, j, k: (k, j))
    o_spec = pl.BlockSpec((tm, tn), lambda i, j, k: (i, j))
    dims = (((0,) if ta else (1,), (1,) if tb else (0,)), ((), ()))
    has_add = add is not None

    def body(*refs):
        a_ref, b_ref = refs[:2]
        add_ref = refs[2] if has_add else None
        o_ref = refs[3] if has_add else refs[2]
        part = lax.dot_general(a_ref[...], b_ref[...], dims, preferred_element_type=F32)
        if nk == 1:
            if has_add:
                part = part + add_ref[...]
            o_ref[...] = part.astype(o_ref.dtype)
            return
        acc = refs[-1]
        k = pl.program_id(2)

        @pl.when(k == 0)
        def _():
            acc[...] = part

        @pl.when(k > 0)
        def _():
            acc[...] += part

        @pl.when(k == nk - 1)
        def _():
            r = acc[...]
            if has_add:
                r = r + add_ref[...]
            o_ref[...] = r.astype(o_ref.dtype)

    ins = [a, b] + ([add] if has_add else [])
    in_specs = [a_spec, b_spec] + ([o_spec] if has_add else [])
    return pl.pallas_call(
        body, name=name, grid=grid, in_specs=in_specs, out_specs=o_spec,
        out_shape=jax.ShapeDtypeStruct((M, N), out_dtype),
        scratch_shapes=[pltpu.VMEM((tm, tn), F32)] if nk > 1 else [],
        compiler_params=_params("parallel", "parallel", "arbitrary"),
    )(*ins)


def _rms_fwd(name, x, g, shards, kinds, tr=512):
    S, n = x.shape
    tr = _tile(S, tr)
    n_steps = S // tr
    nw = len(shards)

    def body(x_ref, g_ref, *rest):
        x_refs, h_ref, g_refs = rest[:nw], rest[nw], rest[nw + 1:2 * nw + 1]
        gather = _Gather(x_refs, g_refs, kinds, *rest[2 * nw + 1:])
        i = pl.program_id(0)
        pl.when(i == 0)(gather.start)
        pl.when(i == n_steps // 2)(gather.forward)
        xhat, _ = _rms_hat(x_ref[...], n)
        h_ref[...] = (xhat * g_ref[...]).astype(BF16)
        pl.when(i == n_steps - 1)(gather.finish)

    return pl.pallas_call(
        body, name=name, grid=(n_steps,),
        in_specs=[_rows(tr, n), _full((1, n))] + [_ANY] * nw, out_specs=[_rows(tr, n)] + [_ANY] * nw,
        out_shape=[jax.ShapeDtypeStruct((S, n), BF16)]
        + [jax.ShapeDtypeStruct(_gather_out_shape(kd, sd.shape), sd.dtype) for kd, sd in zip(kinds, shards)],
        scratch_shapes=[pltpu.SemaphoreType.DMA((nw, 7)), pltpu.SemaphoreType.DMA((nw, 7)),
                        pltpu.SemaphoreType.DMA((nw,))],
        compiler_params=_params("arbitrary"),
    )(x, g, *shards)


def _rms_bwd_call(name, x, g, dy, extra, want_bf16, tr=256):
    S, n = x.shape
    tr = _tile(S, tr)

    def body(x_ref, g_ref, dy_ref, ex_ref, *outs):
        xhat, r = _rms_hat(x_ref[...], n)
        dx, dg = _rms_bwd(dy_ref[...], xhat, r, g_ref[...], n)
        dx = dx + ex_ref[...]
        outs[0][...] = dx
        if want_bf16:
            outs[1][...] = dx.astype(BF16)
        dg_ref = outs[-1]

        @pl.when(pl.program_id(0) == 0)
        def _():
            dg_ref[...] = jnp.zeros_like(dg_ref)

        dg_ref[...] += dg

    out_shape = [jax.ShapeDtypeStruct((S, n), F32)]
    out_specs = [_rows(tr, n)]
    if want_bf16:
        out_shape.append(jax.ShapeDtypeStruct((S, n), BF16))
        out_specs.append(_rows(tr, n))
    out_shape.append(jax.ShapeDtypeStruct((1, n), F32))
    out_specs.append(_full((1, n)))
    return pl.pallas_call(
        body, name=name, grid=(S // tr,),
        in_specs=[_rows(tr, n), _full((1, n)), _rows(tr, n), _rows(tr, n)],
        out_specs=out_specs, out_shape=out_shape,
        compiler_params=_params("arbitrary"),
    )(x, g, dy, extra)


def _lat_fwd(z, gq, gkv, tr=512):
    S = z.shape[0]
    tr = _tile(S, tr)

    def body(zq_ref, zkv_ref, gq_ref, gkv_ref, ql_ref, kvl_ref):
        xq, _ = _rms_hat(zq_ref[...], Q_LORA)
        ql_ref[...] = (xq * gq_ref[...]).astype(BF16)
        xkv, _ = _rms_hat(zkv_ref[...], KV_LORA)
        kvl_ref[...] = (xkv * gkv_ref[...]).astype(BF16)

    return pl.pallas_call(
        body, name="lat_fwd", grid=(S // tr,),
        in_specs=[_rows(tr, Q_LORA, Z_Q // Q_LORA), _rows(tr, KV_LORA, Z_KV // KV_LORA),
                  _full((1, Q_LORA)), _full((1, KV_LORA))],
        out_specs=[_rows(tr, Q_LORA), _rows(tr, KV_LORA)],
        out_shape=[jax.ShapeDtypeStruct((S, Q_LORA), BF16), jax.ShapeDtypeStruct((S, KV_LORA), BF16)],
        compiler_params=_params("parallel"),
    )(z, z, gq, gkv)


def _lat_bwd(z, gq, gkv, dql, dkvl, tr=512):
    S = z.shape[0]
    tr = _tile(S, tr)

    def body(zq_ref, zkv_ref, gq_ref, gkv_ref, dql_ref, dkvl_ref, dzq_ref, dzkv_ref, dgq_ref, dgkv_ref):
        @pl.when(pl.program_id(0) == 0)
        def _():
            dgq_ref[...] = jnp.zeros_like(dgq_ref)
            dgkv_ref[...] = jnp.zeros_like(dgkv_ref)

        xq, rq = _rms_hat(zq_ref[...], Q_LORA)
        dx, dg = _rms_bwd(dql_ref[...], xq, rq, gq_ref[...], Q_LORA)
        dzq_ref[...] = dx.astype(BF16)
        dgq_ref[...] += dg
        xkv, rkv = _rms_hat(zkv_ref[...], KV_LORA)
        dx, dg = _rms_bwd(dkvl_ref[...], xkv, rkv, gkv_ref[...], KV_LORA)
        dzkv_ref[...] = dx.astype(BF16)
        dgkv_ref[...] += dg

    return pl.pallas_call(
        body, name="lat_bwd", grid=(S // tr,),
        in_specs=[_rows(tr, Q_LORA, Z_Q // Q_LORA), _rows(tr, KV_LORA, Z_KV // KV_LORA),
                  _full((1, Q_LORA)), _full((1, KV_LORA)), _rows(tr, Q_LORA), _rows(tr, KV_LORA)],
        out_specs=[_rows(tr, Q_LORA), _rows(tr, KV_LORA), _full((1, Q_LORA)), _full((1, KV_LORA))],
        out_shape=[jax.ShapeDtypeStruct((S, Q_LORA), BF16), jax.ShapeDtypeStruct((S, KV_LORA), BF16),
                   jax.ShapeDtypeStruct((1, Q_LORA), F32), jax.ShapeDtypeStruct((1, KV_LORA), F32)],
        compiler_params=_params("arbitrary"),
    )(z, z, gq, gkv, dql, dkvl)


def _pool_d(ext, cur, t0, tr):
    t = t0 + lax.broadcasted_iota(jnp.int32, (tr, 1), 0)
    ds = []
    for gi, w in enumerate(POOL_WINDOWS):
        s = ext[:, gi * LANES:(gi + 1) * LANES]
        sh = 1
        while sh < w:
            s = s + pltpu.roll(s, sh, 0)
            sh *= 2
        cnt = jnp.minimum(t + 1, w).astype(F32)
        ds.append(s[POOL_HALO:] / cnt - cur[:, gi * LANES:(gi + 1) * LANES])
    return ds


def _pool_fwd(z, wp, scale, tr=512):
    S = z.shape[0]
    tr = _tile(S, tr)
    rb = tr // POOL_HALO

    def body(z_ref, zp_ref, wp_ref, sc_ref, y_ref):
        i = pl.program_id(0)
        cur = z_ref[...]
        halo = jnp.where(i == 0, 0.0, zp_ref[...])
        ds = _pool_d(jnp.concatenate([halo, cur], axis=0), cur, i * tr, tr)
        for gi in range(len(POOL_WINDOWS)):
            y = jnp.dot(ds[gi].astype(BF16), wp_ref[gi], preferred_element_type=F32)
            y_ref[:, gi * LANES:(gi + 1) * LANES] = (y * sc_ref[:, gi * LANES:(gi + 1) * LANES]).astype(BF16)

    return pl.pallas_call(
        body, name="pool_fwd", grid=(S // tr,),
        in_specs=[_rows(tr, POOL_WIDTH, Z_POOL // POOL_WIDTH),
                  pl.BlockSpec((POOL_HALO, POOL_WIDTH), lambda i: (jnp.maximum(i * rb - 1, 0), 0)),
                  _full(wp.shape), _full((1, POOL_WIDTH))],
        out_specs=_rows(tr, POOL_WIDTH),
        out_shape=jax.ShapeDtypeStruct((S, POOL_WIDTH), BF16),
        compiler_params=_params("parallel"),
    )(z, z, wp, scale)


def _pool_bwd(z, dyc, wp, scale, dy_cb, tr=512):
    S = z.shape[0]
    tr = _tile(S, tr)
    rb = tr // POOL_HALO
    nhb = S // POOL_HALO
    ng = len(POOL_WINDOWS)

    def body(z_ref, zp_ref, dy_ref, dyn_ref, wp_ref, sc_ref, dz_ref, dwp_ref, dsc_ref):
        i = pl.program_id(0)

        @pl.when(i == 0)
        def _():
            dwp_ref[...] = jnp.zeros_like(dwp_ref)
            dsc_ref[...] = jnp.zeros_like(dsc_ref)

        cur = z_ref[...]
        halo = jnp.where(i == 0, 0.0, zp_ref[...])
        ds = _pool_d(jnp.concatenate([halo, cur], axis=0), cur, i * tr, tr)
        dy_cur = dy_ref[...]
        dy_next = jnp.where(i == pl.num_programs(0) - 1, 0.0, dyn_ref[...])
        dy_ext = jnp.concatenate([dy_cur, dy_next], axis=0)
        n_ext = tr + POOL_HALO
        t_ext = i * tr + lax.broadcasted_iota(jnp.int32, (n_ext, 1), 0)
        for gi, w in enumerate(POOL_WINDOWS):
            cols = slice(gi * LANES, (gi + 1) * LANES)
            d_b = ds[gi].astype(BF16)
            y_pre = jnp.dot(d_b, wp_ref[gi], preferred_element_type=F32)
            dsc_ref[:, cols] += jnp.sum(dy_cur[:, cols] * y_pre, axis=0, keepdims=True)
            dys = (dy_ext[:, cols] * sc_ref[:, cols]).astype(BF16)
            dwp_ref[gi] += lax.dot_general(d_b, dys[:tr], (((0,), (0,)), ((), ())), preferred_element_type=F32)
            dd = lax.dot_general(dys, wp_ref[gi], (((1,), (1,)), ((), ())), preferred_element_type=F32)
            s = dd / jnp.minimum(t_ext + 1, w).astype(F32)
            sh = 1
            while sh < w:
                s = s + pltpu.roll(s, n_ext - sh, 0)
                sh *= 2
            dz_ref[:, cols] = (s[:tr] - dd[:tr]).astype(BF16)

    return pl.pallas_call(
        body, name="pool_bwd", grid=(S // tr,),
        in_specs=[_rows(tr, POOL_WIDTH, Z_POOL // POOL_WIDTH),
                  pl.BlockSpec((POOL_HALO, POOL_WIDTH), lambda i: (jnp.maximum(i * rb - 1, 0), 0)),
                  _rows(tr, POOL_WIDTH, dy_cb),
                  pl.BlockSpec((POOL_HALO, POOL_WIDTH), lambda i: (jnp.minimum((i + 1) * rb, nhb - 1), dy_cb)),
                  _full(wp.shape), _full((1, POOL_WIDTH))],
        out_specs=[_rows(tr, POOL_WIDTH), _full((ng, LANES, LANES)), _full((1, POOL_WIDTH))],
        out_shape=[jax.ShapeDtypeStruct((S, POOL_WIDTH), BF16), jax.ShapeDtypeStruct((ng, LANES, LANES), F32),
                   jax.ShapeDtypeStruct((1, POOL_WIDTH), F32)],
        compiler_params=_params("arbitrary"),
    )(z, z, dyc, dyc, wp, scale)


def _rope_tables(S):
    half = MLA_ROPE // 2
    inv_freq = 1.0 / (ROPE_THETA ** (jnp.arange(half, dtype=F32) / half))
    ang = jnp.arange(S).astype(F32)[:, None] * inv_freq[None, :]
    cos, sin = jnp.cos(ang), jnp.sin(ang)
    z32 = jnp.zeros((S, half), F32)
    z64 = jnp.zeros((S, LANES - MLA_ROPE), F32)
    cos_t = jnp.concatenate([cos, cos, z64], axis=1)
    sin_a = jnp.concatenate([-sin, z32, z64], axis=1)
    sin_b = jnp.concatenate([z32, sin, z64], axis=1)
    return cos_t, sin_a, sin_b


def _rope(x, cos_t, sin_a, sin_b):
    return x * cos_t + pltpu.roll(x, LANES - 32, 1) * sin_a + pltpu.roll(x, 32, 1) * sin_b


def _rope_t(d, cos_t, sin_a, sin_b):
    return d * cos_t + pltpu.roll(d * sin_a, 32, 1) + pltpu.roll(d * sin_b, LANES - 32, 1)


def _mla_prep(q_raw, kv_raw, z, tabs, gq, gk, shards, kinds, tr=256):
    S = z.shape[0]
    tr = _tile(S, tr)
    n_steps = S // tr
    scale = 1.0 / math.sqrt(MLA_QK)
    W = MLA_HEADS * MLA_QK_PAD
    nw = len(shards)

    def body(q_ref, kv_ref, kr_ref, c_ref, sa_ref, sb_ref, gq_ref, gk_ref, *rest):
        x_refs, (qo_ref, ko_ref, vo_ref), g_refs = rest[:nw], rest[nw:nw + 3], rest[nw + 3:2 * nw + 3]
        gather = _Gather(x_refs, g_refs, kinds, *rest[2 * nw + 3:])
        i = pl.program_id(0)
        pl.when(i == 0)(gather.start)
        pl.when(i == n_steps // 2)(gather.forward)
        tab = (c_ref[...], sa_ref[...], sb_ref[...])
        kr = kr_ref[...]
        kr_ss = jnp.sum(kr * kr, axis=-1, keepdims=True)
        gqn, gqr = gq_ref[:, :LANES], gq_ref[:, LANES:]
        gkn, gkr = gk_ref[:, :LANES], gk_ref[:, LANES:]
        for h in range(MLA_HEADS):
            c0 = h * MLA_QK_PAD
            qn = q_ref[:, c0:c0 + LANES]
            qr = q_ref[:, c0 + LANES:c0 + 2 * LANES]
            r = lax.rsqrt((jnp.sum(qn * qn, -1, keepdims=True) + jnp.sum(qr * qr, -1, keepdims=True)) / MLA_QK
                          + NORM_EPS)
            qo_ref[h, :, :LANES] = (qn * r * gqn * scale).astype(BF16)
            qo_ref[h, :, LANES:] = (_rope(qr * r * gqr, *tab) * scale).astype(BF16)
            kn = kv_ref[:, c0:c0 + LANES]
            r = lax.rsqrt((jnp.sum(kn * kn, -1, keepdims=True) + kr_ss) / MLA_QK + NORM_EPS)
            ko_ref[h, :, :LANES] = (kn * r * gkn).astype(BF16)
            ko_ref[h, :, LANES:] = _rope(kr * r * gkr, *tab).astype(BF16)
            vo_ref[h] = kv_ref[:, c0 + LANES:c0 + 2 * LANES].astype(BF16)
        pl.when(i == n_steps - 1)(gather.finish)

    hs = lambda w: pl.BlockSpec((MLA_HEADS, tr, w), lambda i: (0, i, 0))
    return pl.pallas_call(
        body, name="mla_prep", grid=(n_steps,),
        in_specs=[_rows(tr, W), _rows(tr, W), _rows(tr, LANES, Z_KR // LANES),
                  _rows(tr, LANES), _rows(tr, LANES), _rows(tr, LANES),
                  _full((1, MLA_QK_PAD)), _full((1, MLA_QK_PAD))] + [_ANY] * nw,
        out_specs=[hs(MLA_QK_PAD), hs(MLA_QK_PAD), hs(MLA_V)] + [_ANY] * nw,
        out_shape=[jax.ShapeDtypeStruct((MLA_HEADS, S, MLA_QK_PAD), BF16),
                   jax.ShapeDtypeStruct((MLA_HEADS, S, MLA_QK_PAD), BF16),
                   jax.ShapeDtypeStruct((MLA_HEADS, S, MLA_V), BF16)]
        + [jax.ShapeDtypeStruct(_gather_out_shape(kd, sd.shape), sd.dtype) for kd, sd in zip(kinds, shards)],
        scratch_shapes=[pltpu.SemaphoreType.DMA((nw, 7)), pltpu.SemaphoreType.DMA((nw, 7)),
                        pltpu.SemaphoreType.DMA((nw,))],
        compiler_params=_params("arbitrary"),
    )(q_raw, kv_raw, z, *tabs, gq, gk, *shards)


def _mla_prep_bwd(dq, dk, dv, q_raw, kv_raw, z, tabs, gq, gk, tr=256):
    S = z.shape[0]
    tr = _tile(S, tr)
    scale = 1.0 / math.sqrt(MLA_QK)
    W = MLA_HEADS * MLA_QK_PAD

    def body(dq_ref, dk_ref, dv_ref, q_ref, kv_ref, kr_ref, c_ref, sa_ref, sb_ref, gq_ref, gk_ref,
             dqr_ref, dkvr_ref, dkr_ref, dgq_ref, dgk_ref):
        @pl.when(pl.program_id(0) == 0)
        def _():
            dgq_ref[...] = jnp.zeros_like(dgq_ref)
            dgk_ref[...] = jnp.zeros_like(dgk_ref)

        tab = (c_ref[...], sa_ref[...], sb_ref[...])
        kr = kr_ref[...]
        kr_ss = jnp.sum(kr * kr, axis=-1, keepdims=True)
        gqn, gqr = gq_ref[:, :LANES], gq_ref[:, LANES:]
        gkn, gkr = gk_ref[:, :LANES], gk_ref[:, LANES:]
        dkr_sum = jnp.zeros((tr, LANES), F32)
        dgq_n = jnp.zeros((1, LANES), F32)
        dgq_r = jnp.zeros((1, LANES), F32)
        dgk_n = jnp.zeros((1, LANES), F32)
        dgk_r = jnp.zeros((1, LANES), F32)

        def head_norm_bwd(xn, xr, r, dyn, dyr, gn, gr):
            hn, hr = xn * r, xr * r
            dxn, dxr = dyn * gn, dyr * gr
            mt = (jnp.sum(dxn * hn, -1, keepdims=True) + jnp.sum(dxr * hr, -1, keepdims=True)) / MLA_QK
            return (r * (dxn - hn * mt), r * (dxr - hr * mt),
                    jnp.sum(dyn * hn, axis=0, keepdims=True), jnp.sum(dyr * hr, axis=0, keepdims=True))

        for h in range(MLA_HEADS):
            c0 = h * MLA_QK_PAD
            qn = q_ref[:, c0:c0 + LANES]
            qr = q_ref[:, c0 + LANES:c0 + 2 * LANES]
            r = lax.rsqrt((jnp.sum(qn * qn, -1, keepdims=True) + jnp.sum(qr * qr, -1, keepdims=True)) / MLA_QK
                          + NORM_EPS)
            dyn = dq_ref[h, :, :LANES] * scale
            dyr = _rope_t(dq_ref[h, :, LANES:] * scale, *tab)
            dxn, dxr, gn_, gr_ = head_norm_bwd(qn, qr, r, dyn, dyr, gqn, gqr)
            dgq_n += gn_
            dgq_r += gr_
            dqr_ref[:, c0:c0 + LANES] = dxn.astype(BF16)
            dqr_ref[:, c0 + LANES:c0 + 2 * LANES] = dxr.astype(BF16)

            kn = kv_ref[:, c0:c0 + LANES]
            r = lax.rsqrt((jnp.sum(kn * kn, -1, keepdims=True) + kr_ss) / MLA_QK + NORM_EPS)
            dyn = dk_ref[h, :, :LANES]
            dyr = _rope_t(dk_ref[h, :, LANES:], *tab)
            dxn, dxr, gn_, gr_ = head_norm_bwd(kn, kr, r, dyn, dyr, gkn, gkr)
            dgk_n += gn_
            dgk_r += gr_
            dkr_sum += dxr
            dkvr_ref[:, c0:c0 + LANES] = dxn.astype(BF16)
            dkvr_ref[:, c0 + LANES:c0 + 2 * LANES] = dv_ref[h].astype(BF16)

        dkr_ref[...] = dkr_sum.astype(BF16)
        dgq_ref[:, :LANES] += dgq_n
        dgq_ref[:, LANES:] += dgq_r
        dgk_ref[:, :LANES] += dgk_n
        dgk_ref[:, LANES:] += dgk_r

    hs = lambda w: pl.BlockSpec((MLA_HEADS, tr, w), lambda i: (0, i, 0))
    return pl.pallas_call(
        body, name="mla_prep_bwd", grid=(S // tr,),
        in_specs=[hs(MLA_QK_PAD), hs(MLA_QK_PAD), hs(MLA_V), _rows(tr, W), _rows(tr, W),
                  _rows(tr, LANES, Z_KR // LANES), _rows(tr, LANES), _rows(tr, LANES), _rows(tr, LANES),
                  _full((1, MLA_QK_PAD)), _full((1, MLA_QK_PAD))],
        out_specs=[_rows(tr, W), _rows(tr, W), _rows(tr, LANES), _full((1, MLA_QK_PAD)), _full((1, MLA_QK_PAD))],
        out_shape=[jax.ShapeDtypeStruct((S, W), BF16), jax.ShapeDtypeStruct((S, W), BF16),
                   jax.ShapeDtypeStruct((S, LANES), BF16),
                   jax.ShapeDtypeStruct((1, MLA_QK_PAD), F32), jax.ShapeDtypeStruct((1, MLA_QK_PAD), F32)],
        compiler_params=_params("arbitrary"),
    )(dq, dk, dv, q_raw, kv_raw, z, *tabs, gq, gk)


_NT = (((1,), (1,)), ((), ()))
_TN = (((0,), (0,)), ((), ()))


def _flash_fwd(q, k, v, shards, kinds, tq=1024, tk=1024, ts=512):
    H, S, dq = q.shape
    dv = v.shape[-1]
    tq, tk, ts = _tile(S, tq), _tile(S, tk), _tile(S, ts)
    assert tq % tk == 0 and tq % ts == 0 and dv == LANES
    nq, nd = S // tq, tq // tk
    nw = len(shards)

    def body(q_ref, k_ref, v_ref, *rest):
        x_refs, (o_ref, lse_ref), g_refs = rest[:nw], rest[nw:nw + 2], rest[nw + 2:2 * nw + 2]
        m_sc, l_sc, acc_sc, send_sems, recv_sems, local_sems = rest[2 * nw + 2:]
        h, i = pl.program_id(0), pl.program_id(1)
        gather = _Gather(x_refs, g_refs, kinds, send_sems, recv_sems, local_sems)
        pl.when(jnp.logical_and(h == 0, i == 0))(gather.start)
        pl.when(jnp.logical_and(h == (3 * H) // 4, i == 0))(gather.forward)

        m_sc[...] = jnp.full_like(m_sc, -jnp.inf)
        l_sc[...] = jnp.zeros_like(l_sc)
        acc_sc[...] = jnp.zeros_like(acc_sc)

        def step(off, width, r0, masked):
            rows = slice(r0, tq)
            s = lax.dot_general(q_ref[rows, :], k_ref[pl.ds(off, width), :], _NT, preferred_element_type=F32)
            if masked:
                row = i * tq + r0 + lax.broadcasted_iota(jnp.int32, s.shape, 0)
                col = off + lax.broadcasted_iota(jnp.int32, s.shape, 1)
                s = jnp.where(col <= row, s, -jnp.inf)
            parts = [s[:, c * LANES:(c + 1) * LANES] for c in range(width // LANES)]
            m_cur = parts[0]
            for pt in parts[1:]:
                m_cur = jnp.maximum(m_cur, pt)
            m_prev = m_sc[rows, :]
            m_new = jnp.maximum(m_prev, jnp.max(m_cur, axis=-1, keepdims=True))
            alpha = jnp.exp(m_prev - m_new)
            ps = [jnp.exp(pt - m_new) for pt in parts]
            l_new = alpha * l_sc[rows, :]
            for pc in ps:
                l_new = l_new + pc
            l_sc[rows, :] = l_new
            p = jnp.concatenate(ps, axis=1).astype(BF16)
            acc_sc[rows, :] = alpha * acc_sc[rows, :] + jnp.dot(p, v_ref[pl.ds(off, width), :],
                                                                preferred_element_type=F32)
            m_sc[rows, :] = m_new

        def full_step(j, carry):
            step(pl.multiple_of(j * tk, tk), tk, 0, False)
            return carry

        lax.fori_loop(0, i * nd, full_step, 0)
        for d in range(tq // ts):
            step(pl.multiple_of(i * tq + d * ts, ts), ts, d * ts, True)
        l = jnp.sum(l_sc[...], axis=-1, keepdims=True)
        o_ref[...] = (acc_sc[...] / l).astype(BF16)
        lse_ref[...] = m_sc[...] + jnp.log(l)
        pl.when(jnp.logical_and(h == H - 1, i == nq - 1))(gather.finish)

    return pl.pallas_call(
        body, name="flash_fwd", grid=(H, nq),
        in_specs=[pl.BlockSpec((None, tq, dq), lambda h, i: (h, i, 0)),
                  pl.BlockSpec((None, S, dq), lambda h, i: (h, 0, 0)),
                  pl.BlockSpec((None, S, dv), lambda h, i: (h, 0, 0))] + [_ANY] * nw,
        out_specs=[pl.BlockSpec((tq, dv), lambda h, i: (i, h)),
                   pl.BlockSpec((None, tq, LANES), lambda h, i: (h, i, 0))] + [_ANY] * nw,
        out_shape=[jax.ShapeDtypeStruct((S, H * dv), BF16), jax.ShapeDtypeStruct((H, S, LANES), F32)]
        + [jax.ShapeDtypeStruct(_gather_out_shape(kd, sd.shape), sd.dtype) for kd, sd in zip(kinds, shards)],
        scratch_shapes=[pltpu.VMEM((tq, LANES), F32), pltpu.VMEM((tq, LANES), F32), pltpu.VMEM((tq, dv), F32),
                        pltpu.SemaphoreType.DMA((nw, 7)), pltpu.SemaphoreType.DMA((nw, 7)),
                        pltpu.SemaphoreType.DMA((nw,))],
        compiler_params=_params("arbitrary", "arbitrary"),
    )(q, k, v, *shards)


def _flash_bwd(q, k, v, dyc, cb0, o, lse, sends, tq=1024, tk=1024, ts=512):
    H, S, dq = q.shape
    dv = v.shape[-1]
    tq, tk, ts = _tile(S, tq), _tile(S, tk), _tile(S, ts)
    assert tq % tk == 0 and tq % ts == 0
    nq, nd = S // tq, tq // tk
    nw = len(sends)

    def body(q_ref, k_ref, v_ref, dy_ref, o_ref, lse_ref, *rest):
        s_refs, (dq_ref, dk_ref, dv_ref), r_refs = rest[:nw], rest[nw:nw + 3], rest[nw + 3:2 * nw + 3]
        do_sc, dl_sc, send_sems, recv_sems = rest[2 * nw + 3:]
        h, i = pl.program_id(0), pl.program_id(1)
        scatter = _Scatter(s_refs, r_refs, send_sems, recv_sems)
        pl.when(jnp.logical_and(h == 0, i == 0))(scatter.start)

        @pl.when(i == 0)
        def _():
            dk_ref[...] = jnp.zeros_like(dk_ref)
            dv_ref[...] = jnp.zeros_like(dv_ref)

        dq_ref[...] = jnp.zeros_like(dq_ref)
        dy = dy_ref[...]
        do_sc[...] = dy.astype(BF16)
        dl_sc[...] = jnp.broadcast_to(jnp.sum(dy * o_ref[...].astype(F32), axis=-1, keepdims=True), (tq, LANES))

        def step(off, width, r0, masked):
            rows = slice(r0, tq)
            qv = q_ref[rows, :]
            dov = do_sc[rows, :]
            kb = k_ref[pl.ds(off, width), :]
            s = lax.dot_general(qv, kb, _NT, preferred_element_type=F32)
            if masked:
                row = i * tq + r0 + lax.broadcasted_iota(jnp.int32, s.shape, 0)
                col = off + lax.broadcasted_iota(jnp.int32, s.shape, 1)
                s = jnp.where(col <= row, s, -jnp.inf)
            p = jnp.exp(s - lse_ref[rows, :1])
            dv_ref[pl.ds(off, width), :] += lax.dot_general(p.astype(BF16), dov, _TN, preferred_element_type=F32)
            dp = lax.dot_general(dov, v_ref[pl.ds(off, width), :], _NT, preferred_element_type=F32)
            ds = (p * (dp - dl_sc[rows, :1])).astype(BF16)
            dk_ref[pl.ds(off, width), :] += lax.dot_general(ds, qv, _TN, preferred_element_type=F32)
            dq_ref[rows, :] += jnp.dot(ds, kb, preferred_element_type=F32)

        def full_step(j, carry):
            step(pl.multiple_of(j * tk, tk), tk, 0, False)
            return carry

        lax.fori_loop(0, i * nd, full_step, 0)
        for d in range(tq // ts):
            step(pl.multiple_of(i * tq + d * ts, ts), ts, d * ts, True)
        pl.when(jnp.logical_and(h == H - 1, i == nq - 1))(scatter.finish)

    return pl.pallas_call(
        body, name="flash_bwd", grid=(H, nq),
        in_specs=[pl.BlockSpec((None, tq, dq), lambda h, i: (h, i, 0)),
                  pl.BlockSpec((None, S, dq), lambda h, i: (h, 0, 0), pipeline_mode=pl.Buffered(1)),
                  pl.BlockSpec((None, S, dv), lambda h, i: (h, 0, 0), pipeline_mode=pl.Buffered(1)),
                  pl.BlockSpec((tq, dv), lambda h, i: (i, h + cb0)),
                  pl.BlockSpec((tq, dv), lambda h, i: (i, h)),
                  pl.BlockSpec((None, tq, LANES), lambda h, i: (h, i, 0))] + [_ANY] * nw,
        out_specs=[pl.BlockSpec((None, tq, dq), lambda h, i: (h, i, 0)),
                   pl.BlockSpec((None, S, dq), lambda h, i: (h, 0, 0)),
                   pl.BlockSpec((None, S, dv), lambda h, i: (h, 0, 0))] + [_ANY] * nw,
        out_shape=[jax.ShapeDtypeStruct((H, S, dq), F32), jax.ShapeDtypeStruct((H, S, dq), F32),
                   jax.ShapeDtypeStruct((H, S, dv), F32)]
        + [jax.ShapeDtypeStruct((N_DEV - 1,) + sd.shape[1:], sd.dtype) for sd in sends],
        scratch_shapes=[pltpu.VMEM((tq, dv), BF16), pltpu.VMEM((tq, LANES), F32),
                        pltpu.SemaphoreType.DMA((nw, N_DEV - 1)), pltpu.SemaphoreType.DMA((nw, N_DEV - 1))],
        compiler_params=_params("arbitrary", "arbitrary"),
    )(q, k, v, dyc, o, lse, *sends)


def _mem_norm(mem, g):
    n = mem.shape[1]

    def body(m_ref, g_ref, o_ref):
        xhat, _ = _rms_hat(m_ref[...], n)
        o_ref[...] = (xhat * g_ref[...]).astype(BF16)

    return pl.pallas_call(body, name="mem_norm", out_shape=jax.ShapeDtypeStruct(mem.shape, BF16),
                          compiler_params=pltpu.CompilerParams(vmem_limit_bytes=VMEM_LIMIT))(mem, g)


def _memkv_prep(mkv, gk):
    M = mkv.shape[0]

    def body(mkv_ref, g_ref, k_ref, v_ref):
        for h in range(X_HEADS):
            cols = slice(h * X_HEAD_DIM, (h + 1) * X_HEAD_DIM)
            xhat, _ = _rms_hat(mkv_ref[:, cols], X_HEAD_DIM)
            k_ref[:, cols] = (xhat * g_ref[...]).astype(BF16)
        v_ref[...] = mkv_ref[:, X_WIDTH:].astype(BF16)

    return pl.pallas_call(
        body, name="memkv_prep",
        out_shape=[jax.ShapeDtypeStruct((M, X_WIDTH), BF16), jax.ShapeDtypeStruct((M, X_WIDTH), BF16)],
    )(mkv, gk)


def _memkv_bwd(dk, dv, mkv, gk):
    M = mkv.shape[0]

    def body(dk_ref, dv_ref, mkv_ref, g_ref, dmkv_ref, dg_ref):
        dg = jnp.zeros((1, X_HEAD_DIM), F32)
        for h in range(X_HEADS):
            cols = slice(h * X_HEAD_DIM, (h + 1) * X_HEAD_DIM)
            xhat, r = _rms_hat(mkv_ref[:, cols], X_HEAD_DIM)
            dx, dgh = _rms_bwd(dk_ref[:, cols], xhat, r, g_ref[...], X_HEAD_DIM)
            dmkv_ref[:, cols] = dx.astype(BF16)
            dg += dgh
        dmkv_ref[:, X_WIDTH:] = dv_ref[...].astype(BF16)
        dg_ref[...] = dg

    return pl.pallas_call(
        body, name="memkv_bwd",
        out_shape=[jax.ShapeDtypeStruct((M, 2 * X_WIDTH), BF16), jax.ShapeDtypeStruct((1, X_HEAD_DIM), F32)],
    )(dk, dv, mkv, gk)


def _mem_gain_bwd(mem, dmn):
    n = mem.shape[1]

    def body(m_ref, d_ref, dg_ref):
        xhat, _ = _rms_hat(m_ref[...], n)
        dg_ref[...] = jnp.sum(d_ref[...] * xhat, axis=0, keepdims=True)

    return pl.pallas_call(body, name="mem_gain_bwd", out_shape=jax.ShapeDtypeStruct((1, n), F32),
                          compiler_params=pltpu.CompilerParams(vmem_limit_bytes=VMEM_LIMIT))(mem, dmn)


def _memx_scores(zq_ref, g, kx_ref, h):
    cols = slice(h * X_HEAD_DIM, (h + 1) * X_HEAD_DIM)
    xhat, r = _rms_hat(zq_ref[:, cols], X_HEAD_DIM)
    qn = (xhat * g).astype(BF16)
    s = lax.dot_general(qn, kx_ref[:, cols], _NT, preferred_element_type=F32) * (1.0 / math.sqrt(X_HEAD_DIM))
    p = jnp.exp(s - jnp.max(s, axis=-1, keepdims=True))
    return cols, xhat, r, qn, p, jnp.sum(p, axis=-1, keepdims=True)


def _memx_fwd(z, g, kx, vx, tr=512):
    S = z.shape[0]
    tr = _tile(S, tr)
    M = kx.shape[0]

    def body(zq_ref, g_ref, kx_ref, vx_ref, y_ref):
        for h in range(X_HEADS):
            cols, _, _, _, p, l = _memx_scores(zq_ref, g_ref[...], kx_ref, h)
            o = jnp.dot(p.astype(BF16), vx_ref[:, cols], preferred_element_type=F32)
            y_ref[:, cols] = (o / l).astype(BF16)

    return pl.pallas_call(
        body, name="memx_fwd", grid=(S // tr,),
        in_specs=[_rows(tr, X_WIDTH, Z_MQ // X_WIDTH), _full((1, X_HEAD_DIM)), _full((M, X_WIDTH)),
                  _full((M, X_WIDTH))],
        out_specs=_rows(tr, X_WIDTH),
        out_shape=jax.ShapeDtypeStruct((S, X_WIDTH), BF16),
        compiler_params=_params("parallel"),
    )(z, g, kx, vx)


def _memx_bwd(z, g, kx, vx, dyc, dy_cb, tr=512):
    S = z.shape[0]
    tr = _tile(S, tr)
    M = kx.shape[0]
    scale = 1.0 / math.sqrt(X_HEAD_DIM)

    def body(zq_ref, g_ref, kx_ref, vx_ref, dy_ref, dz_ref, dk_ref, dv_ref, dg_ref):
        @pl.when(pl.program_id(0) == 0)
        def _():
            dk_ref[...] = jnp.zeros_like(dk_ref)
            dv_ref[...] = jnp.zeros_like(dv_ref)
            dg_ref[...] = jnp.zeros_like(dg_ref)

        gv = g_ref[...]
        for h in range(X_HEADS):
            cols, xhat, r, qn, p, l = _memx_scores(zq_ref, gv, kx_ref, h)
            p = p / l
            do = dy_ref[:, cols].astype(BF16)
            dv_ref[:, cols] += lax.dot_general(p.astype(BF16), do, _TN, preferred_element_type=F32)
            dp = lax.dot_general(do, vx_ref[:, cols], _NT, preferred_element_type=F32)
            ds = (p * (dp - jnp.sum(dp * p, axis=-1, keepdims=True)) * scale).astype(BF16)
            dqn = jnp.dot(ds, kx_ref[:, cols], preferred_element_type=F32)
            dk_ref[:, cols] += lax.dot_general(ds, qn, _TN, preferred_element_type=F32)
            dx, dg = _rms_bwd(dqn, xhat, r, gv, X_HEAD_DIM)
            dz_ref[:, cols] = dx.astype(BF16)
            dg_ref[...] += dg

    return pl.pallas_call(
        body, name="memx_bwd", grid=(S // tr,),
        in_specs=[_rows(tr, X_WIDTH, Z_MQ // X_WIDTH), _full((1, X_HEAD_DIM)), _full((M, X_WIDTH)),
                  _full((M, X_WIDTH)), _rows(tr, X_WIDTH, dy_cb)],
        out_specs=[_rows(tr, X_WIDTH), _full((M, X_WIDTH)), _full((M, X_WIDTH)), _full((1, X_HEAD_DIM))],
        out_shape=[jax.ShapeDtypeStruct((S, X_WIDTH), BF16), jax.ShapeDtypeStruct((M, X_WIDTH), F32),
                   jax.ShapeDtypeStruct((M, X_WIDTH), F32), jax.ShapeDtypeStruct((1, X_HEAD_DIM), F32)],
        compiler_params=_params("arbitrary"),
    )(z, g, kx, vx, dyc)


def _conv_gc(g_ext, w_ref, b_ref, n_ext):
    g1 = pltpu.roll(g_ext, 1, 0)
    g2 = pltpu.roll(g_ext, 2, 0)
    gc = b_ref[...] + w_ref[0:1, :] * g2
    gc = gc + w_ref[1:2, :] * g1
    gc = gc + w_ref[2:3, :] * g_ext
    return gc, g1, g2


def _convglu_fwd(g, u, cw, cb, tr=512, tc=512):
    S, F = g.shape
    tr, tc = _tile(S, tr), _tile(F, tc)
    rb = tr // CONV_HALO

    def body(g_ref, gp_ref, u_ref, w_ref, b_ref, a_ref):
        i = pl.program_id(1)
        halo = jnp.where(i == 0, 0.0, gp_ref[...].astype(F32))
        g_ext = jnp.concatenate([halo, g_ref[...].astype(F32)], axis=0)
        gc, _, _ = _conv_gc(g_ext, w_ref, b_ref, tr + CONV_HALO)
        gc = gc[CONV_HALO:]
        sig = 1.0 / (1.0 + jnp.exp(-gc))
        a_ref[...] = (gc * sig * u_ref[...].astype(F32)).astype(BF16)

    return pl.pallas_call(
        body, name="convglu_fwd", grid=(F // tc, S // tr),
        in_specs=[pl.BlockSpec((tr, tc), lambda j, i: (i, j)),
                  pl.BlockSpec((CONV_HALO, tc), lambda j, i: (jnp.maximum(i * rb - 1, 0), j)),
                  pl.BlockSpec((tr, tc), lambda j, i: (i, j)),
                  pl.BlockSpec((3, tc), lambda j, i: (0, j)), pl.BlockSpec((1, tc), lambda j, i: (0, j))],
        out_specs=pl.BlockSpec((tr, tc), lambda j, i: (i, j)),
        out_shape=jax.ShapeDtypeStruct((S, F), BF16),
        compiler_params=_params("parallel", "parallel"),
    )(g, g, u, cw, cb)


def _convglu_bwd(g, u, da, cw, cb, tr=512, tc=512):
    S, F = g.shape
    tr, tc = _tile(S, tr), _tile(F, tc)
    rb = tr // CONV_HALO
    nhb = S // CONV_HALO
    H = CONV_HALO

    def body(g_ref, gp_ref, gn_ref, u_ref, un_ref, da_ref, dan_ref, w_ref, b_ref,
             dg_ref, du_ref, dw_ref, db_ref):
        i = pl.program_id(1)
        last = i == pl.num_programs(1) - 1

        @pl.when(i == 0)
        def _():
            dw_ref[...] = jnp.zeros_like(dw_ref)
            db_ref[...] = jnp.zeros_like(db_ref)

        g_prev = jnp.where(i == 0, 0.0, gp_ref[...].astype(F32))
        g_cur = g_ref[...].astype(F32)
        g_ext = jnp.concatenate([g_prev, g_cur, gn_ref[...].astype(F32)], axis=0)
        gc, g1, g2 = _conv_gc(g_ext, w_ref, b_ref, tr + 2 * H)
        gc = gc[H:]
        ux = jnp.concatenate([u_ref[...], un_ref[...]], axis=0).astype(F32)
        dax = jnp.concatenate([da_ref[...].astype(F32), jnp.where(last, 0.0, dan_ref[...].astype(F32))], axis=0)
        sig = 1.0 / (1.0 + jnp.exp(-gc))
        du_ref[...] = (dax[:tr] * (gc[:tr] * sig[:tr])).astype(BF16)
        dgc = dax * ux * (sig * (1.0 + gc * (1.0 - sig)))
        n = tr + H
        d1 = pltpu.roll(dgc, n - 1, 0)[:tr]
        d2 = pltpu.roll(dgc, n - 2, 0)[:tr]
        d0 = dgc[:tr]
        dg_ref[...] = (w_ref[2:3, :] * d0 + w_ref[1:2, :] * d1 + w_ref[0:1, :] * d2).astype(BF16)
        db_ref[...] += jnp.sum(d0, axis=0, keepdims=True)
        dw_ref[0:1, :] += jnp.sum(d0 * g2[H:H + tr], axis=0, keepdims=True)
        dw_ref[1:2, :] += jnp.sum(d0 * g1[H:H + tr], axis=0, keepdims=True)
        dw_ref[2:3, :] += jnp.sum(d0 * g_cur, axis=0, keepdims=True)

    cur = pl.BlockSpec((tr, tc), lambda j, i: (i, j))
    prv = pl.BlockSpec((H, tc), lambda j, i: (jnp.maximum(i * rb - 1, 0), j))
    nxt = pl.BlockSpec((H, tc), lambda j, i: (jnp.minimum((i + 1) * rb, nhb - 1), j))
    return pl.pallas_call(
        body, name="convglu_bwd", grid=(F // tc, S // tr),
        in_specs=[cur, prv, nxt, cur, nxt, cur, nxt,
                  pl.BlockSpec((3, tc), lambda j, i: (0, j)), pl.BlockSpec((1, tc), lambda j, i: (0, j))],
        out_specs=[cur, cur, pl.BlockSpec((3, tc), lambda j, i: (0, j)), pl.BlockSpec((1, tc), lambda j, i: (0, j))],
        out_shape=[jax.ShapeDtypeStruct((S, F), BF16), jax.ShapeDtypeStruct((S, F), BF16),
                   jax.ShapeDtypeStruct((3, F), F32), jax.ShapeDtypeStruct((1, F), F32)],
        compiler_params=_params("parallel", "arbitrary"),
    )(g, g, g, u, u, da, da, cw, cb)


def _mm_down_loss(a, w, x1, target, tm=1024, tn=512):
    S, K = a.shape
    n = w.shape[1]
    tm, tn = _tile(S, tm), _tile(n, tn)

    def body(a_ref, w_ref, x_ref, t_ref, dy_ref, dyb_ref, loss_ref):
        @pl.when(jnp.logical_and(pl.program_id(0) == 0, pl.program_id(1) == 0))
        def _():
            loss_ref[...] = jnp.zeros_like(loss_ref)

        y = jnp.dot(a_ref[...], w_ref[...], preferred_element_type=F32) + x_ref[...]
        err = y - t_ref[...]
        dy = err / n
        dy_ref[...] = dy
        dyb_ref[...] = dy.astype(BF16)
        loss_ref[...] += 0.5 * jnp.sum(jnp.sum(err * err, axis=-1, keepdims=True) / n)

    tile = pl.BlockSpec((tm, tn), lambda i, j: (i, j))
    return pl.pallas_call(
        body, name="mm_down_loss", grid=(S // tm, n // tn),
        in_specs=[pl.BlockSpec((tm, K), lambda i, j: (i, 0)), pl.BlockSpec((K, tn), lambda i, j: (0, j)), tile, tile],
        out_specs=[tile, tile, pl.BlockSpec((8, LANES), lambda i, j: (0, 0))],
        out_shape=[jax.ShapeDtypeStruct((S, n), F32), jax.ShapeDtypeStruct((S, n), BF16),
                   jax.ShapeDtypeStruct((8, LANES), F32)],
        compiler_params=_params("arbitrary", "arbitrary"),
    )(a, w, x1, target)


def _mm_o_rms(ycat, w, x, g, tm=512):
    S, K = ycat.shape
    n = w.shape[1]
    tm = _tile(S, tm)

    def body(y_ref, w_ref, x_ref, g_ref, x1_ref, h2_ref):
        x1 = jnp.dot(y_ref[...], w_ref[...], preferred_element_type=F32) + x_ref[...]
        x1_ref[...] = x1
        xhat, _ = _rms_hat(x1, n)
        h2_ref[...] = (xhat * g_ref[...]).astype(BF16)

    return pl.pallas_call(
        body, name="mm_o_rms", grid=(S // tm,),
        in_specs=[_rows(tm, K), _full((K, n)), _rows(tm, n), _full((1, n))],
        out_specs=[_rows(tm, n), _rows(tm, n)],
        out_shape=[jax.ShapeDtypeStruct((S, n), F32), jax.ShapeDtypeStruct((S, n), BF16)],
        compiler_params=_params("parallel"),
    )(ycat, w, x, g)


def _mm_in_dx_rms(dz, w, x, g, extra, sends, tm=256):
    S, K = dz.shape
    n = w.shape[0]
    tm = _tile(S, tm)
    n_steps = S // tm
    nw = len(sends)

    def body(dz_ref, w_ref, x_ref, g_ref, ex_ref, *rest):
        s_refs, (dx_ref, dg_ref), r_refs = rest[:nw], rest[nw:nw + 2], rest[nw + 2:2 * nw + 2]
        scatter = _Scatter(s_refs, r_refs, *rest[2 * nw + 2:])
        i = pl.program_id(0)
        pl.when(i == 0)(scatter.start)

        @pl.when(i == 0)
        def _():
            dg_ref[...] = jnp.zeros_like(dg_ref)

        dh = lax.dot_general(dz_ref[...], w_ref[...], _NT, preferred_element_type=F32)
        xhat, r = _rms_hat(x_ref[...], n)
        dx, dg = _rms_bwd(dh, xhat, r, g_ref[...], n)
        dx_ref[...] = dx + ex_ref[...]
        dg_ref[...] += dg
        pl.when(i == n_steps - 1)(scatter.finish)

    return pl.pallas_call(
        body, name="mm_in_dx_rms", grid=(n_steps,),
        in_specs=[_rows(tm, K), _full((n, K)), _rows(tm, n), _full((1, n)), _rows(tm, n)] + [_ANY] * nw,
        out_specs=[_rows(tm, n), _full((1, n))] + [_ANY] * nw,
        out_shape=[jax.ShapeDtypeStruct((S, n), F32), jax.ShapeDtypeStruct((1, n), F32)]
        + [jax.ShapeDtypeStruct((N_DEV - 1,) + sd.shape[1:], sd.dtype) for sd in sends],
        scratch_shapes=[pltpu.SemaphoreType.DMA((nw, N_DEV - 1)), pltpu.SemaphoreType.DMA((nw, N_DEV - 1))],
        compiler_params=_params("arbitrary"),
    )(dz, w, x, g, extra, *sends)


def _local_step(x, mem, target, W, early_shards, late_shards):
    S = x.shape[0]
    tabs = _rope_tables(S)
    W = dict(W)
    G = {}

    h, W["w_in"], W["w_q_up"], W["w_kv_up"] = _rms_fwd("rms1_fwd", x, W["g_mix"], early_shards, EARLY_GATHER_KINDS)
    z = _mm("mm_in", h, W["w_in"], tm=512, tn=Z_COLS, tk=D_MODEL)
    y_pool = _pool_fwd(z, W["w_pool"], W["pool_scale"])
    ql, kvl = _lat_fwd(z, W["g_q_lat"], W["g_kv_lat"])
    q_raw = _mm("mm_q_up", ql, W["w_q_up"], tm=1024, tn=2048, tk=Q_LORA)
    kv_raw = _mm("mm_kv_up", kvl, W["w_kv_up"], tm=1024, tn=2048, tk=KV_LORA)
    n_ffn = len(FFN_GATHER_KINDS)
    q, k, v, W["w_o"], W["w_mem_kv"] = _mla_prep(q_raw, kv_raw, z, tabs, W["g_q_mla"], W["g_k_mla"],
                                                 late_shards[n_ffn:], MIXER_GATHER_KINDS)
    o_mla, lse, wg3, wu3, W["w_down"], cw3 = _flash_fwd(q, k, v, late_shards[:n_ffn], FFN_GATHER_KINDS)
    W["w_gate"] = jnp.transpose(wg3, (1, 0, 2)).reshape(D_MODEL, D_FF)
    W["w_up"] = jnp.transpose(wu3, (1, 0, 2)).reshape(D_MODEL, D_FF)
    cw = jnp.sum(cw3.reshape(N_DEV, 3, 3, D_FF // N_DEV).astype(F32), axis=1)
    W["conv_w"] = jnp.transpose(cw, (1, 0, 2)).reshape(3, D_FF)
    mn = _mem_norm(mem, W["g_mem"])
    mkv = _mm("mm_mem_kv", mn, W["w_mem_kv"], tm=256, tn=1024, tk=D_MODEL)
    kx, vx = _memkv_prep(mkv, W["g_k_x"])
    y_mem = _memx_fwd(z, W["g_q_x"], kx, vx)
    ycat = jnp.concatenate([y_pool, o_mla, y_mem], axis=1)
    x1, h2 = _mm_o_rms(ycat, W["w_o"], x, W["g_ffn"])
    g = _mm("mm_gate", h2, W["w_gate"], out_dtype=BF16, tm=1024, tn=1408, tk=D_MODEL)
    u = _mm("mm_up", h2, W["w_up"], out_dtype=BF16, tm=1024, tn=1408, tk=D_MODEL)
    a = _convglu_fwd(g, u, W["conv_w"], W["conv_b"])
    dy, dyb, loss_part = _mm_down_loss(a, W["w_down"], x1, target)

    da = _mm("mm_down_dx", dyb, W["w_down"], tb=True, out_dtype=BF16, tm=1024, tn=1408, tk=D_MODEL)
    G["w_down"] = _mm("mm_down_dw", a, dyb, ta=True, tm=512, tn=2048, tk=2048)
    dg, du, G["conv_w"], G["conv_b"] = _convglu_bwd(g, u, da, W["conv_w"], W["conv_b"])
    dh2 = _mm("mm_gate_dx", dg, W["w_gate"], tb=True, tm=1024, tn=512, tk=D_FF)
    dh2 = _mm("mm_up_dx", du, W["w_up"], tb=True, add=dh2, tm=1024, tn=512, tk=D_FF)
    G["w_gate"] = _mm("mm_gate_dw", h2, dg, ta=True, tm=1024, tn=1408, tk=2048)
    G["w_up"] = _mm("mm_up_dw", h2, du, ta=True, tm=1024, tn=1408, tk=2048)
    dx1, dx1b, G["g_ffn"] = _rms_bwd_call("rms2_bwd", x1, W["g_ffn"], dh2, dy, True)

    dyc = _mm("mm_o_dx", dx1b, W["w_o"], tb=True, tm=512, tn=2048, tk=D_MODEL)
    G["w_o"] = _mm("mm_o_dw", ycat, dx1b, ta=True, tm=1024, tn=1024, tk=2048)
    dz_pool, G["w_pool"], G["pool_scale"] = _pool_bwd(z, dyc, W["w_pool"], W["pool_scale"], dy_cb=0)
    dz_mq, dkx, dvx, G["g_q_x"] = _memx_bwd(z, W["g_q_x"], kx, vx, dyc, dy_cb=3)
    dmkv, G["g_k_x"] = _memkv_bwd(dkx, dvx, mkv, W["g_k_x"])
    G["w_mem_kv"] = _mm("mm_mem_kv_dw", mn, dmkv, ta=True, tm=1024, tn=1024, tk=MEM_LEN)
    dmn = _mm("mm_mem_kv_dx", dmkv, W["w_mem_kv"], tb=True, tm=256, tn=2048, tk=1024)
    G["g_mem"] = _mem_gain_bwd(mem, dmn)
    sends = [_send_blocks(n, G[n]) for n in SCATTER_EARLY]
    dq, dk, dv, *got_early = _flash_bwd(q, k, v, dyc, POOL_WIDTH // MLA_V, o_mla, lse, sends)
    dq_raw, dkv_raw, dz_kr, G["g_q_mla"], G["g_k_mla"] = _mla_prep_bwd(
        dq, dk, dv, q_raw, kv_raw, z, tabs, W["g_q_mla"], W["g_k_mla"])
    G["w_q_up"] = _mm("mm_q_up_dw", ql, dq_raw, ta=True, tm=512, tn=2048, tk=1024)
    dql = _mm("mm_q_up_dx", dq_raw, W["w_q_up"], tb=True, tm=1024, tn=512, tk=2048)
    G["w_kv_up"] = _mm("mm_kv_up_dw", kvl, dkv_raw, ta=True, tm=256, tn=2048, tk=1024)
    dkvl = _mm("mm_kv_up_dx", dkv_raw, W["w_kv_up"], tb=True, tm=1024, tn=256, tk=2048)
    dz_q, dz_kv, G["g_q_lat"], G["g_kv_lat"] = _lat_bwd(z, W["g_q_lat"], W["g_kv_lat"], dql, dkvl)
    dz = jnp.concatenate([dz_pool, dz_q, dz_mq, dz_kv, dz_kr], axis=1)
    G["w_in"] = _mm("mm_in_dw", h, dz, ta=True, tm=1024, tn=Z_COLS, tk=1024)
    grad_x, G["g_mix"], *got_late = _mm_in_dx_rms(dz, W["w_in"], x, W["g_mix"], dx1,
                                                  [_send_blocks(n, G[n]) for n in SCATTER_LATE])
    return loss_part, grad_x, G, dict(zip(SCATTER_EARLY + SCATTER_LATE, got_early + got_late))


_ANY = pl.BlockSpec(memory_space=pl.ANY)
_MESH = pl.DeviceIdType.MESH


def _gather_out_shape(kind, shape):
    if kind == "rows":
        return (N_DEV * shape[0],) + tuple(shape[1:])
    if kind == "cols":
        return (shape[0], N_DEV * shape[1])
    return (N_DEV,) + tuple(shape)


def _gather_view(ref, kind, shape, d):
    if kind == "rows":
        return ref.at[pl.ds(pl.multiple_of(d * shape[0], 16), shape[0]), :]
    if kind == "cols":
        return ref.at[:, pl.ds(pl.multiple_of(d * shape[1], math.gcd(shape[1], LANES)), shape[1])]
    return ref.at[d]


class _Gather:
    def __init__(self, x_refs, out_refs, kinds, send_sems, recv_sems, local_sems):
        self.xr, self.outr, self.kinds = x_refs, out_refs, kinds
        self.ss, self.rs, self.ls = send_sems, recv_sems, local_sems
        x, y, c = lax.axis_index("x"), lax.axis_index("y"), lax.axis_index("c")
        self.c = c
        self.me, self.sibling = (x, y, c), (x, y, 1 - c)
        self.chips = [(1 - x, y), (x, 1 - y), (1 - x, 1 - y)]

    def _view(self, w, dev):
        px, py, pc = dev
        return _gather_view(self.outr[w], self.kinds[w], self.xr[w].shape, 4 * px + 2 * py + pc)

    def _copy(self, w, k, block, to, from_shard=False):
        v = self._view(w, block)
        return pltpu.make_async_remote_copy(
            src_ref=self.xr[w] if from_shard else v, dst_ref=v, send_sem=self.ss.at[w, k],
            recv_sem=self.rs.at[w, k], device_id=to, device_id_type=_MESH)

    def _local(self, w):
        return pltpu.make_async_copy(self.xr[w], self._view(w, self.me), self.ls.at[w])

    def start(self):
        for w in range(len(self.xr)):
            self._local(w).start()
            self._copy(w, 0, self.me, self.sibling, True).start()
            for j, chip in enumerate(self.chips):
                self._copy(w, 1 + j, self.me, (*chip, self.c), True).start()

    def forward(self):
        for j, chip in enumerate(self.chips):
            for w in range(len(self.xr)):
                self._copy(w, 1 + j, (*chip, self.c), self.me).wait_recv()
                self._copy(w, 4 + j, (*chip, self.c), self.sibling).start()

    def finish(self):
        for w in range(len(self.xr)):
            self._copy(w, 0, self.sibling, self.me).wait_recv()
            for j, chip in enumerate(self.chips):
                self._copy(w, 4 + j, (*chip, 1 - self.c), self.me).wait_recv()
            self._copy(w, 0, self.me, self.sibling, True).wait_send()
            for j, chip in enumerate(self.chips):
                self._copy(w, 1 + j, self.me, (*chip, self.c), True).wait_send()
                self._copy(w, 4 + j, (*chip, self.c), self.sibling).wait_send()
            self._local(w).wait()


class _Scatter:
    def __init__(self, send_refs, recv_refs, send_sems, recv_sems):
        self.sr, self.rr, self.ss, self.rs = send_refs, recv_refs, send_sems, recv_sems
        self.xyz = lax.axis_index("x"), lax.axis_index("y"), lax.axis_index("c")

    def _copy(self, w, k):
        x, y, c = self.xyz
        px, py, pc = x ^ ((k >> 2) & 1), y ^ ((k >> 1) & 1), c ^ (k & 1)
        return pltpu.make_async_remote_copy(
            src_ref=self.sr[w].at[4 * px + 2 * py + pc], dst_ref=self.rr[w].at[k - 1],
            send_sem=self.ss.at[w, k - 1], recv_sem=self.rs.at[w, k - 1],
            device_id=(px, py, pc), device_id_type=_MESH)

    def _all(self):
        return [self._copy(w, k) for w in range(len(self.sr)) for k in range(1, N_DEV)]

    def start(self):
        for cp in self._all():
            cp.start()

    def finish(self):
        for cp in self._all():
            cp.wait_recv()
        for cp in self._all():
            cp.wait_send()


def _exchange_small(s):
    def body(x_ref, out_ref, x_send, x_recv, local_sem):
        x, y, c = lax.axis_index("x"), lax.axis_index("y"), lax.axis_index("c")
        me = 4 * x + 2 * y + c
        mine = pltpu.make_async_copy(x_ref, out_ref.at[me], local_sem)
        mine.start()

        def copy(k):
            px, py, pc = x ^ ((k >> 2) & 1), y ^ ((k >> 1) & 1), c ^ (k & 1)
            return pltpu.make_async_remote_copy(
                src_ref=x_ref, dst_ref=out_ref.at[me], send_sem=x_send.at[k - 1], recv_sem=x_recv.at[k - 1],
                device_id=(px, py, pc), device_id_type=_MESH)

        cps = [copy(k) for k in range(1, N_DEV)]
        for cp in cps:
            cp.start()
        for cp in cps:
            cp.wait_recv()
        for cp in cps:
            cp.wait_send()
        mine.wait()

    return pl.pallas_call(
        body, name="exchange_small", out_shape=jax.ShapeDtypeStruct((N_DEV,) + s.shape, s.dtype),
        in_specs=[_ANY], out_specs=_ANY,
        scratch_shapes=[pltpu.SemaphoreType.DMA((N_DEV - 1,)), pltpu.SemaphoreType.DMA((N_DEV - 1,)),
                        pltpu.SemaphoreType.DMA],
    )(s)


def _sum_slots(buf, tr=208):
    n, R, _ = buf.shape
    tr = tr if R % tr == 0 else R

    def body(b_ref, o_ref):
        acc = b_ref[0]
        for d in range(1, n):
            acc = acc + b_ref[d]
        o_ref[...] = acc

    return pl.pallas_call(
        body, name="sum_slots", grid=(R // tr,),
        in_specs=[pl.BlockSpec((n, tr, LANES), lambda i: (0, i, 0))],
        out_specs=pl.BlockSpec((tr, LANES), lambda i: (i, 0)),
        out_shape=jax.ShapeDtypeStruct((R, LANES), buf.dtype),
        compiler_params=_params("parallel"),
    )(buf)


def _adamw_math(w, g, m, v):
    m_new = ADAM_B1 * m + (1.0 - ADAM_B1) * g
    v_new = ADAM_B2 * v + (1.0 - ADAM_B2) * (g * g)
    m_hat = m_new / (1.0 - ADAM_B1 ** ADAM_STEP)
    v_hat = v_new / (1.0 - ADAM_B2 ** ADAM_STEP)
    return -ADAM_LR * (m_hat / (jnp.sqrt(v_hat) + ADAM_EPS) + ADAM_WD * w), m_new, v_new


def _adamw(name, w, g, got, m, v, tr=128):
    R, C = w.shape
    tr = max([t for t in range(16, tr + 1, 16) if R % t == 0], default=R) if R > tr else R

    def body(w_ref, g_ref, got_ref, m_ref, v_ref, go_ref, d_ref, mo_ref, vo_ref):
        gv = g_ref[...]
        for k in range(N_DEV - 1):
            gv = gv + got_ref[k].astype(F32)
        go_ref[...] = gv
        d_ref[...], mo_ref[...], vo_ref[...] = _adamw_math(w_ref[...], gv, m_ref[...], v_ref[...])

    spec = pl.BlockSpec((tr, C), lambda i: (i, 0))
    got_spec = pl.BlockSpec((N_DEV - 1, tr, C), lambda i: (0, i, 0))
    sds = jax.ShapeDtypeStruct((R, C), F32)
    return pl.pallas_call(
        body, name=name, grid=(R // tr,), in_specs=[spec, spec, got_spec, spec, spec], out_specs=[spec] * 4,
        out_shape=[sds] * 4, compiler_params=_params("parallel"),
    )(w, g, got, m, v)


def _adamw_small(ws, gs, ms, vs):
    n = len(ws)

    def body(*refs):
        ins, outs = refs[:4 * n], refs[4 * n:]
        for p in range(n):
            w_ref, g_ref, m_ref, v_ref = (ins[q * n + p] for q in range(4))
            res = _adamw_math(w_ref[...], g_ref[...], m_ref[...], v_ref[...])
            for q in range(3):
                outs[q * n + p][...] = res[q]

    sds = [jax.ShapeDtypeStruct(w.shape, F32) for w in ws]
    outs = pl.pallas_call(body, name="adamw_small", out_shape=sds * 3)(*ws, *gs, *ms, *vs)
    return outs[:n], outs[n:2 * n], outs[2 * n:]


def _pad_rows(a, mult):
    r = (-a.shape[0]) % mult
    return a if r == 0 else jnp.concatenate([a, jnp.zeros((r,) + a.shape[1:], a.dtype)], axis=0)


def _as_rows(a, mult):
    flat = a.reshape(-1)
    r = (-flat.shape[0]) % LANES
    if r:
        flat = jnp.concatenate([flat, jnp.zeros((r,), a.dtype)])
    return _pad_rows(flat.reshape(-1, LANES), mult)


def _w_in_to_kernel_cols(w):
    pad = jnp.zeros(w.shape[:-1] + (Z_COLS - IN_COLS,), w.dtype)
    return jnp.concatenate([w[..., :1024], w[..., 1344:1856], w[..., 1024:1344], pad], axis=-1)


def _w_in_from_kernel_cols(w):
    return jnp.concatenate([w[..., :1024], w[..., 1536:1856], w[..., 1024:1536]], axis=-1)


def _dev_blocks(n, g):
    ffs = D_FF // N_DEV
    if n == "w_in":
        return _w_in_from_kernel_cols(g).reshape(N_DEV, D_MODEL // N_DEV, IN_COLS)
    if n == "w_q_up":
        return jnp.transpose(g.reshape(Q_LORA, N_DEV, MLA_QK_PAD)[:, :, :MLA_QK], (1, 0, 2))
    if n == "w_kv_up":
        return jnp.transpose(g.reshape(KV_LORA, N_DEV, MLA_NOPE + MLA_V), (1, 0, 2))
    if n in ("w_mem_kv", "w_o"):
        return g.reshape(N_DEV, D_MODEL // N_DEV, g.shape[1])
    if n in ("w_gate", "w_up"):
        return jnp.transpose(g.reshape(D_MODEL, N_DEV, ffs), (1, 0, 2))
    if n == "conv_w":
        return jnp.transpose(g.reshape(3, N_DEV, ffs), (1, 0, 2))
    assert n == "w_down"
    return g.reshape(N_DEV, ffs, D_MODEL)


def _send_blocks(n, g):
    blocks = _dev_blocks(n, g)
    return blocks if n == "conv_w" else blocks.astype(BF16)


def kernel(x, mem, g_mix, w_in, g_q_lat, w_q_up, g_kv_lat, w_kv_up, g_q_mla, g_k_mla, w_pool, pool_scale, g_mem, w_mem_kv, g_q_x, g_k_x, w_o, g_ffn, w_gate, w_up, conv_w, conv_b, w_down, loss_target, m_g_mix, m_w_in, m_g_q_lat, m_w_q_up, m_g_kv_lat, m_w_kv_up, m_g_q_mla, m_g_k_mla, m_w_pool, m_pool_scale, m_g_mem, m_w_mem_kv, m_g_q_x, m_g_k_x, m_w_o, m_g_ffn, m_w_gate, m_w_up, m_conv_w, m_conv_b, m_w_down, v_g_mix, v_w_in, v_g_q_lat, v_w_q_up, v_g_kv_lat, v_w_kv_up, v_g_q_mla, v_g_k_mla, v_w_pool, v_pool_scale, v_g_mem, v_w_mem_kv, v_g_q_x, v_g_k_x, v_w_o, v_g_ffn, v_w_gate, v_w_up, v_conv_w, v_conv_b, v_w_down):
    given = dict(g_mix=g_mix, w_in=w_in, g_q_lat=g_q_lat, w_q_up=w_q_up, g_kv_lat=g_kv_lat, w_kv_up=w_kv_up,
                 g_q_mla=g_q_mla, g_k_mla=g_k_mla, w_pool=w_pool, pool_scale=pool_scale, g_mem=g_mem,
                 w_mem_kv=w_mem_kv, g_q_x=g_q_x, g_k_x=g_k_x, w_o=w_o, g_ffn=g_ffn, w_gate=w_gate, w_up=w_up,
                 conv_w=conv_w, conv_b=conv_b, w_down=w_down)
    mom_m = dict(g_mix=m_g_mix, w_in=m_w_in, g_q_lat=m_g_q_lat, w_q_up=m_w_q_up, g_kv_lat=m_g_kv_lat,
                 w_kv_up=m_w_kv_up, g_q_mla=m_g_q_mla, g_k_mla=m_g_k_mla, w_pool=m_w_pool,
                 pool_scale=m_pool_scale, g_mem=m_g_mem, w_mem_kv=m_w_mem_kv, g_q_x=m_g_q_x, g_k_x=m_g_k_x,
                 w_o=m_w_o, g_ffn=m_g_ffn, w_gate=m_w_gate, w_up=m_w_up, conv_w=m_conv_w, conv_b=m_conv_b,
                 w_down=m_w_down)
    mom_v = dict(g_mix=v_g_mix, w_in=v_w_in, g_q_lat=v_g_q_lat, w_q_up=v_w_q_up, g_kv_lat=v_g_kv_lat,
                 w_kv_up=v_w_kv_up, g_q_mla=v_g_q_mla, g_k_mla=v_g_k_mla, w_pool=v_w_pool,
                 pool_scale=v_pool_scale, g_mem=v_g_mem, w_mem_kv=v_w_mem_kv, g_q_x=v_g_q_x, g_k_x=v_g_k_x,
                 w_o=v_w_o, g_ffn=v_g_ffn, w_gate=v_w_gate, w_up=v_w_up, conv_w=v_conv_w, conv_b=v_conv_b,
                 w_down=v_w_down)
    drop = lambda a: a[0] if a.ndim > 2 else a
    sh = {n: drop(given[n]) for n in WEIGHTS}
    mom_m = {n: drop(mom_m[n]) for n in WEIGHTS}
    mom_v = {n: drop(mom_v[n]) for n in WEIGHTS}

    cw_hi = sh["conv_w"].astype(BF16)
    cw_r = sh["conv_w"] - cw_hi.astype(F32)
    cw_mid = cw_r.astype(BF16)
    cw_lo = (cw_r - cw_mid.astype(F32)).astype(BF16)
    early_shards = [
        _w_in_to_kernel_cols(sh["w_in"]).astype(BF16),
        jnp.pad(sh["w_q_up"], ((0, 0), (0, MLA_QK_PAD - MLA_QK))).astype(BF16),
        sh["w_kv_up"].astype(BF16),
    ]
    late_shards = [sh["w_gate"].astype(BF16), sh["w_up"].astype(BF16), sh["w_down"].astype(BF16),
                   jnp.concatenate([cw_hi, cw_mid, cw_lo], axis=0), sh["w_o"].astype(BF16),
                   sh["w_mem_kv"].astype(BF16)]
    W = {"w_pool": sh["w_pool"].astype(BF16)}
    for n in ("g_mix", "g_q_lat", "g_kv_lat", "pool_scale", "g_mem", "g_q_x", "g_k_x", "g_ffn", "conv_b"):
        W[n] = sh[n]
    pad_qk = lambda gv: jnp.pad(gv, ((0, 0), (0, MLA_QK_PAD - MLA_QK)))
    W["g_q_mla"], W["g_k_mla"] = pad_qk(sh["g_q_mla"]), pad_qk(sh["g_k_mla"])

    loss_part, grad_x, G, got = _local_step(x[0], mem[0], loss_target[0], W, early_shards, late_shards)

    small = {
        "g_mix": G["g_mix"], "g_q_lat": G["g_q_lat"], "g_kv_lat": G["g_kv_lat"],
        "g_q_mla": G["g_q_mla"][:, :MLA_QK], "g_k_mla": G["g_k_mla"][:, :MLA_QK],
        "w_pool": G["w_pool"], "pool_scale": G["pool_scale"], "g_mem": G["g_mem"],
        "g_q_x": G["g_q_x"], "g_k_x": G["g_k_x"], "g_ffn": G["g_ffn"], "conv_b": G["conv_b"],
    }
    s_offs = {}
    segs = []
    off = 0
    for n in SMALL:
        r = _as_rows(small[n], 8)
        s_offs[n] = off
        off += r.shape[0]
        segs.append(r)
    segs.append(loss_part)
    loss_row = off
    sbuf = jnp.concatenate(segs, axis=0)
    s_sum = _sum_slots(_exchange_small(sbuf))

    def small_take(buf, n):
        shape = sh[n].shape
        cnt = math.prod(shape)
        return buf[s_offs[n]:s_offs[n] + -(-cnt // LANES)].reshape(-1)[:cnt].reshape(shape)

    loss = s_sum[loss_row, 0]

    me = 4 * lax.axis_index("x") + 2 * lax.axis_index("y") + lax.axis_index("c")
    grads, deltas, new_m, new_v = {}, {}, {}, {}
    for n in BIG:
        shape = sh[n].shape
        own = lax.dynamic_index_in_dim(_dev_blocks(n, G[n]), me, 0, keepdims=False)
        flat = lambda a: a.reshape(-1, shape[-1])
        outs = _adamw("adamw_" + n, flat(sh[n]), flat(own), got[n].reshape(N_DEV - 1, -1, shape[-1]),
                      flat(mom_m[n]), flat(mom_v[n]))
        grads[n], deltas[n], new_m[n], new_v[n] = (o.reshape(shape) for o in outs)

    for n in SMALL:
        grads[n] = small_take(s_sum, n)
    d_s, m_s, v_s = _adamw_small([sh[n] for n in SMALL], [grads[n] for n in SMALL],
                                 [mom_m[n] for n in SMALL], [mom_v[n] for n in SMALL])
    for j, n in enumerate(SMALL):
        deltas[n], new_m[n], new_v[n] = d_s[j], m_s[j], v_s[j]

    lead = lambda n, a: a.reshape(given[n].shape)
    return (loss, grad_x[None],
            *[lead(n, grads[n]) for n in WEIGHTS], *[lead(n, deltas[n]) for n in WEIGHTS],
            *[lead(n, new_m[n]) for n in WEIGHTS], *[lead(n, new_v[n]) for n in WEIGHTS])
```

```python
import math

import jax
import jax.numpy as jnp
from jax import lax
from jax.experimental import pallas as pl
from jax.experimental.pallas import tpu as pltpu

F32 = jnp.float32
BF16 = jnp.bfloat16

D_MODEL = 2048
D_FF = 5632
POOL_WIDTH = 512
POOL_WINDOWS = (2, 4, 8, 16)
POOL_HALO = 16
MLA_HEADS = 8
MLA_NOPE = 128
MLA_ROPE = 64
MLA_QK = MLA_NOPE + MLA_ROPE
MLA_QK_PAD = 256
MLA_V = 128
Q_LORA = 512
KV_LORA = 256
X_HEADS = 4
X_HEAD_DIM = 128
X_WIDTH = 512
MEM_LEN = 256
ROPE_THETA = 10000.0
NORM_EPS = 1e-6
CONV_HALO = 16
IN_COLS = 1856
Z_COLS = 1920
Z_POOL, Z_Q, Z_MQ, Z_KV, Z_KR = 0, 512, 1024, 1536, 1792

ADAM_LR = 0.001
ADAM_B1 = 0.9
ADAM_B2 = 0.999
ADAM_EPS = 1e-08
ADAM_WD = 0.01
ADAM_STEP = 10

N_DEV = 8
LANES = 128
VMEM_LIMIT = 56 * 1024 * 1024

BIG = ("w_in", "w_q_up", "w_kv_up", "w_mem_kv", "w_o", "w_gate", "w_up", "conv_w", "w_down")
SCATTER_EARLY = ("w_down", "w_gate", "w_up", "conv_w", "w_o", "w_mem_kv")
SCATTER_LATE = ("w_in", "w_q_up", "w_kv_up")
EARLY_GATHER_KINDS = ("rows", "cols", "cols")
FFN_GATHER_KINDS = ("lead", "lead", "rows", "lead")
MIXER_GATHER_KINDS = ("rows", "rows")
SMALL = ("g_mix", "g_q_lat", "g_kv_lat", "g_q_mla", "g_k_mla", "w_pool", "pool_scale", "g_mem",
         "g_q_x", "g_k_x", "g_ffn", "conv_b")
WEIGHTS = ("g_mix", "w_in", "g_q_lat", "w_q_up", "g_kv_lat", "w_kv_up", "g_q_mla", "g_k_mla", "w_pool",
           "pool_scale", "g_mem", "w_mem_kv", "g_q_x", "g_k_x", "w_o", "g_ffn", "w_gate", "w_up", "conv_w",
           "conv_b", "w_down")


def _tile(n, t):
    if n <= t:
        return n
    for c in range(t - t % LANES, 0, -LANES):
        if n % c == 0:
            return c
    return n


def _params(*sem):
    return pltpu.CompilerParams(dimension_semantics=sem, vmem_limit_bytes=VMEM_LIMIT)


def _full(shape):
    nd = len(shape)
    return pl.BlockSpec(shape, lambda *_: (0,) * nd)


def _rows(tr, w, cb=0):
    return pl.BlockSpec((tr, w), lambda i: (i, cb))


def _rms_hat(x, n):
    r = lax.rsqrt(jnp.sum(x * x, axis=-1, keepdims=True) / n + NORM_EPS)
    return x * r, r


def _rms_bwd(dy, xhat, r, g, n):
    dxh = dy * g
    dx = r * (dxh - xhat * (jnp.sum(dxh * xhat, axis=-1, keepdims=True) / n))
    return dx, jnp.sum(dy * xhat, axis=0, keepdims=True)


def _mm(name, a, b, *, ta=False, tb=False, add=None, out_dtype=F32, tm=512, tn=512, tk=512):
    if ta:
        K, M = a.shape
    else:
        M, K = a.shape
    if tb:
        N, K2 = b.shape
    else:
        K2, N = b.shape
    assert K == K2, (name, a.shape, b.shape)
    tm, tn, tk = _tile(M, tm), _tile(N, tn), _tile(K, tk)
    nk = K // tk
    grid = (M // tm, N // tn, nk)
    a_spec = pl.BlockSpec((tk, tm), lambda i, j, k: (k, i)) if ta else pl.BlockSpec((tm, tk), lambda i, j, k: (i, k))
    b_spec = pl.BlockSpec((tn, tk), lambda i, j, k: (j, k)) if tb else pl.BlockSpec((tk, tn), lambda i, j, k: (k, j))
    o_spec = pl.BlockSpec((tm, tn), lambda i, j, k: (i, j))
    dims = (((0,) if ta else (1,), (1,) if tb else (0,)), ((), ()))
    has_add = add is not None

    def body(*refs):
        a_ref, b_ref = refs[:2]
        add_ref = refs[2] if has_add else None
        o_ref = refs[3] if has_add else refs[2]
        part = lax.dot_general(a_ref[...], b_ref[...], dims, preferred_element_type=F32)
        if nk == 1:
            if has_add:
                part = part + add_ref[...]
            o_ref[...] = part.astype(o_ref.dtype)
            return
        acc = refs[-1]
        k = pl.program_id(2)

        @pl.when(k == 0)
        def _():
            acc[...] = part

        @pl.when(k > 0)
        def _():
            acc[...] += part

        @pl.when(k == nk - 1)
        def _():
            r = acc[...]
            if has_add:
                r = r + add_ref[...]
            o_ref[...] = r.astype(o_ref.dtype)

    ins = [a, b] + ([add] if has_add else [])
    in_specs = [a_spec, b_spec] + ([o_spec] if has_add else [])
    return pl.pallas_call(
        body, name=name, grid=grid, in_specs=in_specs, out_specs=o_spec,
        out_shape=jax.ShapeDtypeStruct((M, N), out_dtype),
        scratch_shapes=[pltpu.VMEM((tm, tn), F32)] if nk > 1 else [],
        compiler_params=_params("parallel", "parallel", "arbitrary"),
    )(*ins)


def _rms_fwd(name, x, g, shards, kinds, tr=512):
    S, n = x.shape
    tr = _tile(S, tr)
    n_steps = S // tr
    nw = len(shards)

    def body(x_ref, g_ref, *rest):
        x_refs, h_ref, g_refs = rest[:nw], rest[nw], rest[nw + 1:2 * nw + 1]
        gather = _Gather(x_refs, g_refs, kinds, *rest[2 * nw + 1:])
        i = pl.program_id(0)
        pl.when(i == 0)(gather.start)
        pl.when(i == n_steps // 2)(gather.forward)
        xhat, _ = _rms_hat(x_ref[...], n)
        h_ref[...] = (xhat * g_ref[...]).astype(BF16)
        pl.when(i == n_steps - 1)(gather.finish)

    return pl.pallas_call(
        body, name=name, grid=(n_steps,),
        in_specs=[_rows(tr, n), _full((1, n))] + [_ANY] * nw, out_specs=[_rows(tr, n)] + [_ANY] * nw,
        out_shape=[jax.ShapeDtypeStruct((S, n), BF16)]
        + [jax.ShapeDtypeStruct(_gather_out_shape(kd, sd.shape), sd.dtype) for kd, sd in zip(kinds, shards)],
        scratch_shapes=[pltpu.SemaphoreType.DMA((nw, 7)), pltpu.SemaphoreType.DMA((nw, 7)),
                        pltpu.SemaphoreType.DMA((nw,))],
        compiler_params=_params("arbitrary"),
    )(x, g, *shards)


def _mm_up_dx_rms(du, w, part, x, g, extra, tm=256):
    S, K = du.shape
    n = w.shape[0]
    tm = _tile(S, tm)

    def body(du_ref, w_ref, p_ref, x_ref, g_ref, ex_ref, dx_ref, dxb_ref, dg_ref):
        @pl.when(pl.program_id(0) == 0)
        def _():
            dg_ref[...] = jnp.zeros_like(dg_ref)

        dh = lax.dot_general(du_ref[...], w_ref[...], _NT, preferred_element_type=F32) + p_ref[...]
        xhat, r = _rms_hat(x_ref[...], n)
        dx, dg = _rms_bwd(dh, xhat, r, g_ref[...], n)
        dx = dx + ex_ref[...]
        dx_ref[...] = dx
        dxb_ref[...] = dx.astype(BF16)
        dg_ref[...] += dg

    return pl.pallas_call(
        body, name="mm_up_dx_rms", grid=(S // tm,),
        in_specs=[_rows(tm, K), pl.BlockSpec((n, K), lambda i: (0, 0), pipeline_mode=pl.Buffered(1)),
                  _rows(tm, n), _rows(tm, n), _full((1, n)), _rows(tm, n)],
        out_specs=[_rows(tm, n), _rows(tm, n), _full((1, n))],
        out_shape=[jax.ShapeDtypeStruct((S, n), F32), jax.ShapeDtypeStruct((S, n), BF16),
                   jax.ShapeDtypeStruct((1, n), F32)],
        compiler_params=_params("arbitrary"),
    )(du, w, part, x, g, extra)


def _lat_fwd(z, gq, gkv, tr=512):
    S = z.shape[0]
    tr = _tile(S, tr)

    def body(zq_ref, zkv_ref, gq_ref, gkv_ref, ql_ref, kvl_ref):
        xq, _ = _rms_hat(zq_ref[...], Q_LORA)
        ql_ref[...] = (xq * gq_ref[...]).astype(BF16)
        xkv, _ = _rms_hat(zkv_ref[...], KV_LORA)
        kvl_ref[...] = (xkv * gkv_ref[...]).astype(BF16)

    return pl.pallas_call(
        body, name="lat_fwd", grid=(S // tr,),
        in_specs=[_rows(tr, Q_LORA, Z_Q // Q_LORA), _rows(tr, KV_LORA, Z_KV // KV_LORA),
                  _full((1, Q_LORA)), _full((1, KV_LORA))],
        out_specs=[_rows(tr, Q_LORA), _rows(tr, KV_LORA)],
        out_shape=[jax.ShapeDtypeStruct((S, Q_LORA), BF16), jax.ShapeDtypeStruct((S, KV_LORA), BF16)],
        compiler_params=_params("parallel"),
    )(z, z, gq, gkv)


def _lat_bwd(z, gq, gkv, dql, dkvl, tr=512):
    S = z.shape[0]
    tr = _tile(S, tr)

    def body(zq_ref, zkv_ref, gq_ref, gkv_ref, dql_ref, dkvl_ref, dzq_ref, dzkv_ref, dgq_ref, dgkv_ref):
        @pl.when(pl.program_id(0) == 0)
        def _():
            dgq_ref[...] = jnp.zeros_like(dgq_ref)
            dgkv_ref[...] = jnp.zeros_like(dgkv_ref)

        xq, rq = _rms_hat(zq_ref[...], Q_LORA)
        dx, dg = _rms_bwd(dql_ref[...], xq, rq, gq_ref[...], Q_LORA)
        dzq_ref[...] = dx.astype(BF16)
        dgq_ref[...] += dg
        xkv, rkv = _rms_hat(zkv_ref[...], KV_LORA)
        dx, dg = _rms_bwd(dkvl_ref[...], xkv, rkv, gkv_ref[...], KV_LORA)
        dzkv_ref[...] = dx.astype(BF16)
        dgkv_ref[...] += dg

    return pl.pallas_call(
        body, name="lat_bwd", grid=(S // tr,),
        in_specs=[_rows(tr, Q_LORA, Z_Q // Q_LORA), _rows(tr, KV_LORA, Z_KV // KV_LORA),
                  _full((1, Q_LORA)), _full((1, KV_LORA)), _rows(tr, Q_LORA), _rows(tr, KV_LORA)],
        out_specs=[_rows(tr, Q_LORA), _rows(tr, KV_LORA), _full((1, Q_LORA)), _full((1, KV_LORA))],
        out_shape=[jax.ShapeDtypeStruct((S, Q_LORA), BF16), jax.ShapeDtypeStruct((S, KV_LORA), BF16),
                   jax.ShapeDtypeStruct((1, Q_LORA), F32), jax.ShapeDtypeStruct((1, KV_LORA), F32)],
        compiler_params=_params("arbitrary"),
    )(z, z, gq, gkv, dql, dkvl)


def _pool_d(ext, cur, t0, tr):
    t = t0 + lax.broadcasted_iota(jnp.int32, (tr, 1), 0)
    ds = []
    for gi, w in enumerate(POOL_WINDOWS):
        s = ext[:, gi * LANES:(gi + 1) * LANES]
        sh = 1
        while sh < w:
            s = s + pltpu.roll(s, sh, 0)
            sh *= 2
        cnt = jnp.minimum(t + 1, w).astype(F32)
        ds.append(s[POOL_HALO:] / cnt - cur[:, gi * LANES:(gi + 1) * LANES])
    return ds


def _pool_fwd(z, wp, scale, tr=512):
    S = z.shape[0]
    tr = _tile(S, tr)
    rb = tr // POOL_HALO

    def body(z_ref, zp_ref, wp_ref, sc_ref, y_ref):
        i = pl.program_id(0)
        cur = z_ref[...]
        halo = jnp.where(i == 0, 0.0, zp_ref[...])
        ds = _pool_d(jnp.concatenate([halo, cur], axis=0), cur, i * tr, tr)
        for gi in range(len(POOL_WINDOWS)):
            y = jnp.dot(ds[gi].astype(BF16), wp_ref[gi], preferred_element_type=F32)
            y_ref[:, gi * LANES:(gi + 1) * LANES] = (y * sc_ref[:, gi * LANES:(gi + 1) * LANES]).astype(BF16)

    return pl.pallas_call(
        body, name="pool_fwd", grid=(S // tr,),
        in_specs=[_rows(tr, POOL_WIDTH, Z_POOL // POOL_WIDTH),
                  pl.BlockSpec((POOL_HALO, POOL_WIDTH), lambda i: (jnp.maximum(i * rb - 1, 0), 0)),
                  _full(wp.shape), _full((1, POOL_WIDTH))],
        out_specs=_rows(tr, POOL_WIDTH),
        out_shape=jax.ShapeDtypeStruct((S, POOL_WIDTH), BF16),
        compiler_params=_params("parallel"),
    )(z, z, wp, scale)


def _pool_bwd(z, dyc, wp, scale, dy_cb, tr=512):
    S = z.shape[0]
    tr = _tile(S, tr)
    rb = tr // POOL_HALO
    nhb = S // POOL_HALO
    ng = len(POOL_WINDOWS)

    def body(z_ref, zp_ref, dy_ref, dyn_ref, wp_ref, sc_ref, dz_ref, dwp_ref, dsc_ref):
        i = pl.program_id(0)

        @pl.when(i == 0)
        def _():
            dwp_ref[...] = jnp.zeros_like(dwp_ref)
            dsc_ref[...] = jnp.zeros_like(dsc_ref)

        cur = z_ref[...]
        halo = jnp.where(i == 0, 0.0, zp_ref[...])
        ds = _pool_d(jnp.concatenate([halo, cur], axis=0), cur, i * tr, tr)
        dy_cur = dy_ref[...]
        dy_next = jnp.where(i == pl.num_programs(0) - 1, 0.0, dyn_ref[...])
        dy_ext = jnp.concatenate([dy_cur, dy_next], axis=0)
        n_ext = tr + POOL_HALO
        t_ext = i * tr + lax.broadcasted_iota(jnp.int32, (n_ext, 1), 0)
        for gi, w in enumerate(POOL_WINDOWS):
            cols = slice(gi * LANES, (gi + 1) * LANES)
            d_b = ds[gi].astype(BF16)
            y_pre = jnp.dot(d_b, wp_ref[gi], preferred_element_type=F32)
            dsc_ref[:, cols] += jnp.sum(dy_cur[:, cols] * y_pre, axis=0, keepdims=True)
            dys = (dy_ext[:, cols] * sc_ref[:, cols]).astype(BF16)
            dwp_ref[gi] += lax.dot_general(d_b, dys[:tr], (((0,), (0,)), ((), ())), preferred_element_type=F32)
            dd = lax.dot_general(dys, wp_ref[gi], (((1,), (1,)), ((), ())), preferred_element_type=F32)
            s = dd / jnp.minimum(t_ext + 1, w).astype(F32)
            sh = 1
            while sh < w:
                s = s + pltpu.roll(s, n_ext - sh, 0)
                sh *= 2
            dz_ref[:, cols] = (s[:tr] - dd[:tr]).astype(BF16)

    return pl.pallas_call(
        body, name="pool_bwd", grid=(S // tr,),
        in_specs=[_rows(tr, POOL_WIDTH, Z_POOL // POOL_WIDTH),
                  pl.BlockSpec((POOL_HALO, POOL_WIDTH), lambda i: (jnp.maximum(i * rb - 1, 0), 0)),
                  _rows(tr, POOL_WIDTH, dy_cb),
                  pl.BlockSpec((POOL_HALO, POOL_WIDTH), lambda i: (jnp.minimum((i + 1) * rb, nhb - 1), dy_cb)),
                  _full(wp.shape), _full((1, POOL_WIDTH))],
        out_specs=[_rows(tr, POOL_WIDTH), _full((ng, LANES, LANES)), _full((1, POOL_WIDTH))],
        out_shape=[jax.ShapeDtypeStruct((S, POOL_WIDTH), BF16), jax.ShapeDtypeStruct((ng, LANES, LANES), F32),
                   jax.ShapeDtypeStruct((1, POOL_WIDTH), F32)],
        compiler_params=_params("arbitrary"),
    )(z, z, dyc, dyc, wp, scale)


def _rope_tables(S):
    half = MLA_ROPE // 2
    inv_freq = 1.0 / (ROPE_THETA ** (jnp.arange(half, dtype=F32) / half))
    ang = jnp.arange(S).astype(F32)[:, None] * inv_freq[None, :]
    cos, sin = jnp.cos(ang), jnp.sin(ang)
    z32 = jnp.zeros((S, half), F32)
    z64 = jnp.zeros((S, LANES - MLA_ROPE), F32)
    cos_t = jnp.concatenate([cos, cos, z64], axis=1)
    sin_a = jnp.concatenate([-sin, z32, z64], axis=1)
    sin_b = jnp.concatenate([z32, sin, z64], axis=1)
    return cos_t, sin_a, sin_b


def _rope(x, cos_t, sin_a, sin_b):
    return x * cos_t + pltpu.roll(x, LANES - 32, 1) * sin_a + pltpu.roll(x, 32, 1) * sin_b


def _rope_t(d, cos_t, sin_a, sin_b):
    return d * cos_t + pltpu.roll(d * sin_a, 32, 1) + pltpu.roll(d * sin_b, LANES - 32, 1)


def _mla_prep(q_raw, kv_raw, z, tabs, gq, gk, shards, kinds, tr=256):
    S = z.shape[0]
    tr = _tile(S, tr)
    n_steps = S // tr
    scale = 1.0 / math.sqrt(MLA_QK)
    W = MLA_HEADS * MLA_QK_PAD
    nw = len(shards)

    def body(q_ref, kv_ref, kr_ref, c_ref, sa_ref, sb_ref, gq_ref, gk_ref, *rest):
        x_refs, (qo_ref, ko_ref, vo_ref), g_refs = rest[:nw], rest[nw:nw + 3], rest[nw + 3:2 * nw + 3]
        gather = _Gather(x_refs, g_refs, kinds, *rest[2 * nw + 3:])
        i = pl.program_id(0)
        pl.when(i == 0)(gather.start)
        pl.when(i == n_steps // 2)(gather.forward)
        tab = (c_ref[...], sa_ref[...], sb_ref[...])
        kr = kr_ref[...]
        kr_ss = jnp.sum(kr * kr, axis=-1, keepdims=True)
        gqn, gqr = gq_ref[:, :LANES], gq_ref[:, LANES:]
        gkn, gkr = gk_ref[:, :LANES], gk_ref[:, LANES:]
        for h in range(MLA_HEADS):
            c0 = h * MLA_QK_PAD
            qn = q_ref[:, c0:c0 + LANES].astype(F32)
            qr = q_ref[:, c0 + LANES:c0 + 2 * LANES].astype(F32)
            r = lax.rsqrt((jnp.sum(qn * qn, -1, keepdims=True) + jnp.sum(qr * qr, -1, keepdims=True)) / MLA_QK
                          + NORM_EPS)
            qo_ref[h, :, :LANES] = (qn * r * gqn * scale).astype(BF16)
            qo_ref[h, :, LANES:] = (_rope(qr * r * gqr, *tab) * scale).astype(BF16)
            kn = kv_ref[:, c0:c0 + LANES].astype(F32)
            r = lax.rsqrt((jnp.sum(kn * kn, -1, keepdims=True) + kr_ss) / MLA_QK + NORM_EPS)
            ko_ref[h, :, :LANES] = (kn * r * gkn).astype(BF16)
            ko_ref[h, :, LANES:] = _rope(kr * r * gkr, *tab).astype(BF16)
            vo_ref[h] = kv_ref[:, c0 + LANES:c0 + 2 * LANES].astype(BF16)
        pl.when(i == n_steps - 1)(gather.finish)

    hs = lambda w: pl.BlockSpec((MLA_HEADS, tr, w), lambda i: (0, i, 0))
    return pl.pallas_call(
        body, name="mla_prep", grid=(n_steps,),
        in_specs=[_rows(tr, W), _rows(tr, W), _rows(tr, LANES, Z_KR // LANES),
                  _rows(tr, LANES), _rows(tr, LANES), _rows(tr, LANES),
                  _full((1, MLA_QK_PAD)), _full((1, MLA_QK_PAD))] + [_ANY] * nw,
        out_specs=[hs(MLA_QK_PAD), hs(MLA_QK_PAD), hs(MLA_V)] + [_ANY] * nw,
        out_shape=[jax.ShapeDtypeStruct((MLA_HEADS, S, MLA_QK_PAD), BF16),
                   jax.ShapeDtypeStruct((MLA_HEADS, S, MLA_QK_PAD), BF16),
                   jax.ShapeDtypeStruct((MLA_HEADS, S, MLA_V), BF16)]
        + [jax.ShapeDtypeStruct(_gather_out_shape(kd, sd.shape), sd.dtype) for kd, sd in zip(kinds, shards)],
        scratch_shapes=[pltpu.SemaphoreType.DMA((nw, 7)), pltpu.SemaphoreType.DMA((nw, 7)),
                        pltpu.SemaphoreType.DMA((nw,))],
        compiler_params=_params("arbitrary"),
    )(q_raw, kv_raw, z, *tabs, gq, gk, *shards)


def _mla_prep_bwd(dq, dk, dv, q_raw, kv_raw, z, tabs, gq, gk, tr=256):
    S = z.shape[0]
    tr = _tile(S, tr)
    scale = 1.0 / math.sqrt(MLA_QK)
    W = MLA_HEADS * MLA_QK_PAD

    def body(dq_ref, dk_ref, dv_ref, q_ref, kv_ref, kr_ref, c_ref, sa_ref, sb_ref, gq_ref, gk_ref,
             dqr_ref, dkvr_ref, dkr_ref, dgq_ref, dgk_ref):
        @pl.when(pl.program_id(0) == 0)
        def _():
            dgq_ref[...] = jnp.zeros_like(dgq_ref)
            dgk_ref[...] = jnp.zeros_like(dgk_ref)

        tab = (c_ref[...], sa_ref[...], sb_ref[...])
        kr = kr_ref[...]
        kr_ss = jnp.sum(kr * kr, axis=-1, keepdims=True)
        gqn, gqr = gq_ref[:, :LANES], gq_ref[:, LANES:]
        gkn, gkr = gk_ref[:, :LANES], gk_ref[:, LANES:]
        dkr_sum = jnp.zeros((tr, LANES), F32)
        dgq_n = jnp.zeros((1, LANES), F32)
        dgq_r = jnp.zeros((1, LANES), F32)
        dgk_n = jnp.zeros((1, LANES), F32)
        dgk_r = jnp.zeros((1, LANES), F32)

        def head_norm_bwd(xn, xr, r, dyn, dyr, gn, gr):
            hn, hr = xn * r, xr * r
            dxn, dxr = dyn * gn, dyr * gr
            mt = (jnp.sum(dxn * hn, -1, keepdims=True) + jnp.sum(dxr * hr, -1, keepdims=True)) / MLA_QK
            return (r * (dxn - hn * mt), r * (dxr - hr * mt),
                    jnp.sum(dyn * hn, axis=0, keepdims=True), jnp.sum(dyr * hr, axis=0, keepdims=True))

        for h in range(MLA_HEADS):
            c0 = h * MLA_QK_PAD
            qn = q_ref[:, c0:c0 + LANES].astype(F32)
            qr = q_ref[:, c0 + LANES:c0 + 2 * LANES].astype(F32)
            r = lax.rsqrt((jnp.sum(qn * qn, -1, keepdims=True) + jnp.sum(qr * qr, -1, keepdims=True)) / MLA_QK
                          + NORM_EPS)
            dyn = dq_ref[h, :, :LANES] * scale
            dyr = _rope_t(dq_ref[h, :, LANES:] * scale, *tab)
            dxn, dxr, gn_, gr_ = head_norm_bwd(qn, qr, r, dyn, dyr, gqn, gqr)
            dgq_n += gn_
            dgq_r += gr_
            dqr_ref[:, c0:c0 + LANES] = dxn.astype(BF16)
            dqr_ref[:, c0 + LANES:c0 + 2 * LANES] = dxr.astype(BF16)

            kn = kv_ref[:, c0:c0 + LANES].astype(F32)
            r = lax.rsqrt((jnp.sum(kn * kn, -1, keepdims=True) + kr_ss) / MLA_QK + NORM_EPS)
            dyn = dk_ref[h, :, :LANES]
            dyr = _rope_t(dk_ref[h, :, LANES:], *tab)
            dxn, dxr, gn_, gr_ = head_norm_bwd(kn, kr, r, dyn, dyr, gkn, gkr)
            dgk_n += gn_
            dgk_r += gr_
            dkr_sum += dxr
            dkvr_ref[:, c0:c0 + LANES] = dxn.astype(BF16)
            dkvr_ref[:, c0 + LANES:c0 + 2 * LANES] = dv_ref[h].astype(BF16)

        dkr_ref[...] = dkr_sum.astype(BF16)
        dgq_ref[:, :LANES] += dgq_n
        dgq_ref[:, LANES:] += dgq_r
        dgk_ref[:, :LANES] += dgk_n
        dgk_ref[:, LANES:] += dgk_r

    hs = lambda w: pl.BlockSpec((MLA_HEADS, tr, w), lambda i: (0, i, 0))
    return pl.pallas_call(
        body, name="mla_prep_bwd", grid=(S // tr,),
        in_specs=[hs(MLA_QK_PAD), hs(MLA_QK_PAD), hs(MLA_V), _rows(tr, W), _rows(tr, W),
                  _rows(tr, LANES, Z_KR // LANES), _rows(tr, LANES), _rows(tr, LANES), _rows(tr, LANES),
                  _full((1, MLA_QK_PAD)), _full((1, MLA_QK_PAD))],
        out_specs=[_rows(tr, W), _rows(tr, W), _rows(tr, LANES), _full((1, MLA_QK_PAD)), _full((1, MLA_QK_PAD))],
        out_shape=[jax.ShapeDtypeStruct((S, W), BF16), jax.ShapeDtypeStruct((S, W), BF16),
                   jax.ShapeDtypeStruct((S, LANES), BF16),
                   jax.ShapeDtypeStruct((1, MLA_QK_PAD), F32), jax.ShapeDtypeStruct((1, MLA_QK_PAD), F32)],
        compiler_params=_params("arbitrary"),
    )(dq, dk, dv, q_raw, kv_raw, z, *tabs, gq, gk)


_NT = (((1,), (1,)), ((), ()))
_TN = (((0,), (0,)), ((), ()))


def _flash_fwd(q, k, v, shards, kinds, tq=1024, tk=1024, ts=512):
    H, S, dq = q.shape
    dv = v.shape[-1]
    tq, tk, ts = _tile(S, tq), _tile(S, tk), _tile(S, ts)
    assert tq % tk == 0 and tq % ts == 0 and dv == LANES
    nq, nd = S // tq, tq // tk
    nw = len(shards)

    def body(q_ref, k_ref, v_ref, *rest):
        x_refs, (o_ref, lse_ref), g_refs = rest[:nw], rest[nw:nw + 2], rest[nw + 2:2 * nw + 2]
        m_sc, l_sc, acc_sc, send_sems, recv_sems, local_sems = rest[2 * nw + 2:]
        h, i = pl.program_id(0), pl.program_id(1)
        gather = _Gather(x_refs, g_refs, kinds, send_sems, recv_sems, local_sems)
        pl.when(jnp.logical_and(h == 0, i == 0))(gather.start)
        pl.when(jnp.logical_and(h == (3 * H) // 4, i == 0))(gather.forward)

        m_sc[...] = jnp.full_like(m_sc, -jnp.inf)
        l_sc[...] = jnp.zeros_like(l_sc)
        acc_sc[...] = jnp.zeros_like(acc_sc)

        def step(off, width, r0, masked):
            rows = slice(r0, tq)
            s = lax.dot_general(q_ref[rows, :], k_ref[pl.ds(off, width), :], _NT, preferred_element_type=F32)
            if masked:
                row = i * tq + r0 + lax.broadcasted_iota(jnp.int32, s.shape, 0)
                col = off + lax.broadcasted_iota(jnp.int32, s.shape, 1)
                s = jnp.where(col <= row, s, -jnp.inf)
            parts = [s[:, c * LANES:(c + 1) * LANES] for c in range(width // LANES)]
            m_cur = parts[0]
            for pt in parts[1:]:
                m_cur = jnp.maximum(m_cur, pt)
            m_prev = m_sc[rows, :]
            m_new = jnp.maximum(m_prev, jnp.max(m_cur, axis=-1, keepdims=True))
            alpha = jnp.exp(m_prev - m_new)
            ps = [jnp.exp(pt - m_new) for pt in parts]
            l_new = alpha * l_sc[rows, :]
            for pc in ps:
                l_new = l_new + pc
            l_sc[rows, :] = l_new
            p = jnp.concatenate(ps, axis=1).astype(BF16)
            acc_sc[rows, :] = alpha * acc_sc[rows, :] + jnp.dot(p, v_ref[pl.ds(off, width), :],
                                                                preferred_element_type=F32)
            m_sc[rows, :] = m_new

        def full_step(j, carry):
            step(pl.multiple_of(j * tk, tk), tk, 0, False)
            return carry

        lax.fori_loop(0, i * nd, full_step, 0)
        for d in range(tq // ts):
            step(pl.multiple_of(i * tq + d * ts, ts), ts, d * ts, True)
        l = jnp.sum(l_sc[...], axis=-1, keepdims=True)
        o_ref[...] = (acc_sc[...] / l).astype(BF16)
        lse_ref[...] = m_sc[...] + jnp.log(l)
        pl.when(jnp.logical_and(h == H - 1, i == nq - 1))(gather.finish)

    return pl.pallas_call(
        body, name="flash_fwd", grid=(H, nq),
        in_specs=[pl.BlockSpec((None, tq, dq), lambda h, i: (h, i, 0)),
                  pl.BlockSpec((None, S, dq), lambda h, i: (h, 0, 0)),
                  pl.BlockSpec((None, S, dv), lambda h, i: (h, 0, 0))] + [_ANY] * nw,
        out_specs=[pl.BlockSpec((tq, dv), lambda h, i: (i, h)),
                   pl.BlockSpec((None, tq, LANES), lambda h, i: (h, i, 0))] + [_ANY] * nw,
        out_shape=[jax.ShapeDtypeStruct((S, H * dv), BF16), jax.ShapeDtypeStruct((H, S, LANES), F32)]
        + [jax.ShapeDtypeStruct(_gather_out_shape(kd, sd.shape), sd.dtype) for kd, sd in zip(kinds, shards)],
        scratch_shapes=[pltpu.VMEM((tq, LANES), F32), pltpu.VMEM((tq, LANES), F32), pltpu.VMEM((tq, dv), F32),
                        pltpu.SemaphoreType.DMA((nw, 7)), pltpu.SemaphoreType.DMA((nw, 7)),
                        pltpu.SemaphoreType.DMA((nw,))],
        compiler_params=_params("arbitrary", "arbitrary"),
    )(q, k, v, *shards)


def _flash_bwd(q, k, v, dyc, cb0, o, lse, sends, tq=1024, tk=1024, ts=512):
    H, S, dq = q.shape
    dv = v.shape[-1]
    tq, tk, ts = _tile(S, tq), _tile(S, tk), _tile(S, ts)
    assert tq % tk == 0 and tq % ts == 0
    nq, nd = S // tq, tq // tk
    nw = len(sends)

    def body(q_ref, k_ref, v_ref, dy_ref, o_ref, lse_ref, *rest):
        s_refs, (dq_ref, dk_ref, dv_ref), r_refs = rest[:nw], rest[nw:nw + 3], rest[nw + 3:2 * nw + 3]
        do_sc, dl_sc, send_sems, recv_sems = rest[2 * nw + 3:]
        h, i = pl.program_id(0), pl.program_id(1)
        scatter = _Scatter(s_refs, r_refs, send_sems, recv_sems)
        pl.when(jnp.logical_and(h == 0, i == 0))(scatter.start)

        @pl.when(i == 0)
        def _():
            dk_ref[...] = jnp.zeros_like(dk_ref)
            dv_ref[...] = jnp.zeros_like(dv_ref)

        dq_ref[...] = jnp.zeros_like(dq_ref)
        dy = dy_ref[...]
        do_sc[...] = dy.astype(BF16)
        dl_sc[...] = jnp.broadcast_to(jnp.sum(dy * o_ref[...].astype(F32), axis=-1, keepdims=True), (tq, LANES))

        def step(off, width, r0, masked):
            rows = slice(r0, tq)
            qv = q_ref[rows, :]
            dov = do_sc[rows, :]
            kb = k_ref[pl.ds(off, width), :]
            s = lax.dot_general(qv, kb, _NT, preferred_element_type=F32)
            if masked:
                row = i * tq + r0 + lax.broadcasted_iota(jnp.int32, s.shape, 0)
                col = off + lax.broadcasted_iota(jnp.int32, s.shape, 1)
                s = jnp.where(col <= row, s, -jnp.inf)
            p = jnp.exp(s - lse_ref[rows, :1])
            dv_ref[pl.ds(off, width), :] += lax.dot_general(p.astype(BF16), dov, _TN, preferred_element_type=F32)
            dp = lax.dot_general(dov, v_ref[pl.ds(off, width), :], _NT, preferred_element_type=F32)
            ds = (p * (dp - dl_sc[rows, :1])).astype(BF16)
            dk_ref[pl.ds(off, width), :] += lax.dot_general(ds, qv, _TN, preferred_element_type=F32)
            dq_ref[rows, :] += jnp.dot(ds, kb, preferred_element_type=F32)

        def full_step(j, carry):
            step(pl.multiple_of(j * tk, tk), tk, 0, False)
            return carry

        lax.fori_loop(0, i * nd, full_step, 0)
        for d in range(tq // ts):
            step(pl.multiple_of(i * tq + d * ts, ts), ts, d * ts, True)
        pl.when(jnp.logical_and(h == H - 1, i == nq - 1))(scatter.finish)

    return pl.pallas_call(
        body, name="flash_bwd", grid=(H, nq),
        in_specs=[pl.BlockSpec((None, tq, dq), lambda h, i: (h, i, 0)),
                  pl.BlockSpec((None, S, dq), lambda h, i: (h, 0, 0), pipeline_mode=pl.Buffered(1)),
                  pl.BlockSpec((None, S, dv), lambda h, i: (h, 0, 0), pipeline_mode=pl.Buffered(1)),
                  pl.BlockSpec((tq, dv), lambda h, i: (i, h + cb0)),
                  pl.BlockSpec((tq, dv), lambda h, i: (i, h)),
                  pl.BlockSpec((None, tq, LANES), lambda h, i: (h, i, 0))] + [_ANY] * nw,
        out_specs=[pl.BlockSpec((None, tq, dq), lambda h, i: (h, i, 0)),
                   pl.BlockSpec((None, S, dq), lambda h, i: (h, 0, 0)),
                   pl.BlockSpec((None, S, dv), lambda h, i: (h, 0, 0))] + [_ANY] * nw,
        out_shape=[jax.ShapeDtypeStruct((H, S, dq), F32), jax.ShapeDtypeStruct((H, S, dq), F32),
                   jax.ShapeDtypeStruct((H, S, dv), F32)]
        + [jax.ShapeDtypeStruct((N_DEV - 1,) + sd.shape[1:], sd.dtype) for sd in sends],
        scratch_shapes=[pltpu.VMEM((tq, dv), BF16), pltpu.VMEM((tq, LANES), F32),
                        pltpu.SemaphoreType.DMA((nw, N_DEV - 1)), pltpu.SemaphoreType.DMA((nw, N_DEV - 1))],
        compiler_params=_params("arbitrary", "arbitrary"),
    )(q, k, v, dyc, o, lse, *sends)


def _mem_norm(mem, g):
    n = mem.shape[1]

    def body(m_ref, g_ref, o_ref):
        xhat, _ = _rms_hat(m_ref[...], n)
        o_ref[...] = (xhat * g_ref[...]).astype(BF16)

    return pl.pallas_call(body, name="mem_norm", out_shape=jax.ShapeDtypeStruct(mem.shape, BF16),
                          compiler_params=pltpu.CompilerParams(vmem_limit_bytes=VMEM_LIMIT))(mem, g)


def _memkv_prep(mkv, gk):
    M = mkv.shape[0]

    def body(mkv_ref, g_ref, k_ref, v_ref):
        for h in range(X_HEADS):
            cols = slice(h * X_HEAD_DIM, (h + 1) * X_HEAD_DIM)
            xhat, _ = _rms_hat(mkv_ref[:, cols], X_HEAD_DIM)
            k_ref[:, cols] = (xhat * g_ref[...]).astype(BF16)
        v_ref[...] = mkv_ref[:, X_WIDTH:].astype(BF16)

    return pl.pallas_call(
        body, name="memkv_prep",
        out_shape=[jax.ShapeDtypeStruct((M, X_WIDTH), BF16), jax.ShapeDtypeStruct((M, X_WIDTH), BF16)],
    )(mkv, gk)


def _memkv_bwd(dk, dv, mkv, gk):
    M = mkv.shape[0]

    def body(dk_ref, dv_ref, mkv_ref, g_ref, dmkv_ref, dg_ref):
        dg = jnp.zeros((1, X_HEAD_DIM), F32)
        for h in range(X_HEADS):
            cols = slice(h * X_HEAD_DIM, (h + 1) * X_HEAD_DIM)
            xhat, r = _rms_hat(mkv_ref[:, cols], X_HEAD_DIM)
            dx, dgh = _rms_bwd(dk_ref[:, cols], xhat, r, g_ref[...], X_HEAD_DIM)
            dmkv_ref[:, cols] = dx.astype(BF16)
            dg += dgh
        dmkv_ref[:, X_WIDTH:] = dv_ref[...].astype(BF16)
        dg_ref[...] = dg

    return pl.pallas_call(
        body, name="memkv_bwd",
        out_shape=[jax.ShapeDtypeStruct((M, 2 * X_WIDTH), BF16), jax.ShapeDtypeStruct((1, X_HEAD_DIM), F32)],
    )(dk, dv, mkv, gk)


def _mem_gain_bwd(mem, dmn):
    n = mem.shape[1]

    def body(m_ref, d_ref, dg_ref):
        xhat, _ = _rms_hat(m_ref[...], n)
        dg_ref[...] = jnp.sum(d_ref[...] * xhat, axis=0, keepdims=True)

    return pl.pallas_call(body, name="mem_gain_bwd", out_shape=jax.ShapeDtypeStruct((1, n), F32),
                          compiler_params=pltpu.CompilerParams(vmem_limit_bytes=VMEM_LIMIT))(mem, dmn)


def _memx_scores(zq_ref, g, kx_ref, h):
    cols = slice(h * X_HEAD_DIM, (h + 1) * X_HEAD_DIM)
    xhat, r = _rms_hat(zq_ref[:, cols], X_HEAD_DIM)
    qn = (xhat * g).astype(BF16)
    s = lax.dot_general(qn, kx_ref[:, cols], _NT, preferred_element_type=F32) * (1.0 / math.sqrt(X_HEAD_DIM))
    p = jnp.exp(s - jnp.max(s, axis=-1, keepdims=True))
    return cols, xhat, r, qn, p, jnp.sum(p, axis=-1, keepdims=True)


def _memx_fwd(z, g, kx, vx, tr=512):
    S = z.shape[0]
    tr = _tile(S, tr)
    M = kx.shape[0]

    def body(zq_ref, g_ref, kx_ref, vx_ref, y_ref):
        for h in range(X_HEADS):
            cols, _, _, _, p, l = _memx_scores(zq_ref, g_ref[...], kx_ref, h)
            o = jnp.dot(p.astype(BF16), vx_ref[:, cols], preferred_element_type=F32)
            y_ref[:, cols] = (o / l).astype(BF16)

    return pl.pallas_call(
        body, name="memx_fwd", grid=(S // tr,),
        in_specs=[_rows(tr, X_WIDTH, Z_MQ // X_WIDTH), _full((1, X_HEAD_DIM)), _full((M, X_WIDTH)),
                  _full((M, X_WIDTH))],
        out_specs=_rows(tr, X_WIDTH),
        out_shape=jax.ShapeDtypeStruct((S, X_WIDTH), BF16),
        compiler_params=_params("parallel"),
    )(z, g, kx, vx)


def _memx_bwd(z, g, kx, vx, dyc, dy_cb, tr=512):
    S = z.shape[0]
    tr = _tile(S, tr)
    M = kx.shape[0]
    scale = 1.0 / math.sqrt(X_HEAD_DIM)

    def body(zq_ref, g_ref, kx_ref, vx_ref, dy_ref, dz_ref, dk_ref, dv_ref, dg_ref):
        @pl.when(pl.program_id(0) == 0)
        def _():
            dk_ref[...] = jnp.zeros_like(dk_ref)
            dv_ref[...] = jnp.zeros_like(dv_ref)
            dg_ref[...] = jnp.zeros_like(dg_ref)

        gv = g_ref[...]
        for h in range(X_HEADS):
            cols, xhat, r, qn, p, l = _memx_scores(zq_ref, gv, kx_ref, h)
            p = p / l
            do = dy_ref[:, cols].astype(BF16)
            dv_ref[:, cols] += lax.dot_general(p.astype(BF16), do, _TN, preferred_element_type=F32)
            dp = lax.dot_general(do, vx_ref[:, cols], _NT, preferred_element_type=F32)
            ds = (p * (dp - jnp.sum(dp * p, axis=-1, keepdims=True)) * scale).astype(BF16)
            dqn = jnp.dot(ds, kx_ref[:, cols], preferred_element_type=F32)
            dk_ref[:, cols] += lax.dot_general(ds, qn, _TN, preferred_element_type=F32)
            dx, dg = _rms_bwd(dqn, xhat, r, gv, X_HEAD_DIM)
            dz_ref[:, cols] = dx.astype(BF16)
            dg_ref[...] += dg

    return pl.pallas_call(
        body, name="memx_bwd", grid=(S // tr,),
        in_specs=[_rows(tr, X_WIDTH, Z_MQ // X_WIDTH), _full((1, X_HEAD_DIM)), _full((M, X_WIDTH)),
                  _full((M, X_WIDTH)), _rows(tr, X_WIDTH, dy_cb)],
        out_specs=[_rows(tr, X_WIDTH), _full((M, X_WIDTH)), _full((M, X_WIDTH)), _full((1, X_HEAD_DIM))],
        out_shape=[jax.ShapeDtypeStruct((S, X_WIDTH), BF16), jax.ShapeDtypeStruct((M, X_WIDTH), F32),
                   jax.ShapeDtypeStruct((M, X_WIDTH), F32), jax.ShapeDtypeStruct((1, X_HEAD_DIM), F32)],
        compiler_params=_params("arbitrary"),
    )(z, g, kx, vx, dyc)


def _conv_gc(g_ext, w_ref, b_ref, n_ext):
    g1 = pltpu.roll(g_ext, 1, 0)
    g2 = pltpu.roll(g_ext, 2, 0)
    gc = b_ref[...] + w_ref[0:1, :] * g2
    gc = gc + w_ref[1:2, :] * g1
    gc = gc + w_ref[2:3, :] * g_ext
    return gc, g1, g2


def _convglu_fwd(g, u, cw, cb, tr=512, tc=512):
    S, F = g.shape
    tr, tc = _tile(S, tr), _tile(F, tc)
    rb = tr // CONV_HALO

    def body(g_ref, gp_ref, u_ref, w_ref, b_ref, a_ref):
        i = pl.program_id(1)
        halo = jnp.where(i == 0, 0.0, gp_ref[...].astype(F32))
        g_ext = jnp.concatenate([halo, g_ref[...].astype(F32)], axis=0)
        gc, _, _ = _conv_gc(g_ext, w_ref, b_ref, tr + CONV_HALO)
        gc = gc[CONV_HALO:]
        sig = 1.0 / (1.0 + jnp.exp(-gc))
        a_ref[...] = (gc * sig * u_ref[...].astype(F32)).astype(BF16)

    return pl.pallas_call(
        body, name="convglu_fwd", grid=(F // tc, S // tr),
        in_specs=[pl.BlockSpec((tr, tc), lambda j, i: (i, j)),
                  pl.BlockSpec((CONV_HALO, tc), lambda j, i: (jnp.maximum(i * rb - 1, 0), j)),
                  pl.BlockSpec((tr, tc), lambda j, i: (i, j)),
                  pl.BlockSpec((3, tc), lambda j, i: (0, j)), pl.BlockSpec((1, tc), lambda j, i: (0, j))],
        out_specs=pl.BlockSpec((tr, tc), lambda j, i: (i, j)),
        out_shape=jax.ShapeDtypeStruct((S, F), BF16),
        compiler_params=_params("parallel", "parallel"),
    )(g, g, u, cw, cb)


def _convglu_bwd(g, u, da, cw, cb, tr=512, tc=512):
    S, F = g.shape
    tr, tc = _tile(S, tr), _tile(F, tc)
    rb = tr // CONV_HALO
    nhb = S // CONV_HALO
    H = CONV_HALO

    def body(g_ref, gp_ref, gn_ref, u_ref, un_ref, da_ref, dan_ref, w_ref, b_ref,
             dg_ref, du_ref, dw_ref, db_ref):
        i = pl.program_id(1)
        last = i == pl.num_programs(1) - 1

        @pl.when(i == 0)
        def _():
            dw_ref[...] = jnp.zeros_like(dw_ref)
            db_ref[...] = jnp.zeros_like(db_ref)

        g_prev = jnp.where(i == 0, 0.0, gp_ref[...].astype(F32))
        g_cur = g_ref[...].astype(F32)
        g_ext = jnp.concatenate([g_prev, g_cur, gn_ref[...].astype(F32)], axis=0)
        gc, g1, g2 = _conv_gc(g_ext, w_ref, b_ref, tr + 2 * H)
        gc = gc[H:]
        ux = jnp.concatenate([u_ref[...], un_ref[...]], axis=0).astype(F32)
        dax = jnp.concatenate([da_ref[...].astype(F32), jnp.where(last, 0.0, dan_ref[...].astype(F32))], axis=0)
        sig = 1.0 / (1.0 + jnp.exp(-gc))
        du_ref[...] = (dax[:tr] * (gc[:tr] * sig[:tr])).astype(BF16)
        dgc = dax * ux * (sig * (1.0 + gc * (1.0 - sig)))
        n = tr + H
        d1 = pltpu.roll(dgc, n - 1, 0)[:tr]
        d2 = pltpu.roll(dgc, n - 2, 0)[:tr]
        d0 = dgc[:tr]
        dg_ref[...] = (w_ref[2:3, :] * d0 + w_ref[1:2, :] * d1 + w_ref[0:1, :] * d2).astype(BF16)
        db_ref[...] += jnp.sum(d0, axis=0, keepdims=True)
        dw_ref[0:1, :] += jnp.sum(d0 * g2[H:H + tr], axis=0, keepdims=True)
        dw_ref[1:2, :] += jnp.sum(d0 * g1[H:H + tr], axis=0, keepdims=True)
        dw_ref[2:3, :] += jnp.sum(d0 * g_cur, axis=0, keepdims=True)

    cur = pl.BlockSpec((tr, tc), lambda j, i: (i, j))
    prv = pl.BlockSpec((H, tc), lambda j, i: (jnp.maximum(i * rb - 1, 0), j))
    nxt = pl.BlockSpec((H, tc), lambda j, i: (jnp.minimum((i + 1) * rb, nhb - 1), j))
    return pl.pallas_call(
        body, name="convglu_bwd", grid=(F // tc, S // tr),
        in_specs=[cur, prv, nxt, cur, nxt, cur, nxt,
                  pl.BlockSpec((3, tc), lambda j, i: (0, j)), pl.BlockSpec((1, tc), lambda j, i: (0, j))],
        out_specs=[cur, cur, pl.BlockSpec((3, tc), lambda j, i: (0, j)), pl.BlockSpec((1, tc), lambda j, i: (0, j))],
        out_shape=[jax.ShapeDtypeStruct((S, F), BF16), jax.ShapeDtypeStruct((S, F), BF16),
                   jax.ShapeDtypeStruct((3, F), F32), jax.ShapeDtypeStruct((1, F), F32)],
        compiler_params=_params("parallel", "arbitrary"),
    )(g, g, g, u, u, da, da, cw, cb)


def _mm_down_loss(a, w, x1, target, tm=1024, tn=512):
    S, K = a.shape
    n = w.shape[1]
    tm, tn = _tile(S, tm), _tile(n, tn)

    def body(a_ref, w_ref, x_ref, t_ref, dy_ref, dyb_ref, loss_ref):
        @pl.when(jnp.logical_and(pl.program_id(0) == 0, pl.program_id(1) == 0))
        def _():
            loss_ref[...] = jnp.zeros_like(loss_ref)

        y = jnp.dot(a_ref[...], w_ref[...], preferred_element_type=F32) + x_ref[...]
        err = y - t_ref[...]
        dy = err / n
        dy_ref[...] = dy
        dyb_ref[...] = dy.astype(BF16)
        loss_ref[...] += 0.5 * jnp.sum(jnp.sum(err * err, axis=-1, keepdims=True) / n)

    tile = pl.BlockSpec((tm, tn), lambda i, j: (i, j))
    return pl.pallas_call(
        body, name="mm_down_loss", grid=(S // tm, n // tn),
        in_specs=[pl.BlockSpec((tm, K), lambda i, j: (i, 0)), pl.BlockSpec((K, tn), lambda i, j: (0, j)), tile, tile],
        out_specs=[tile, tile, pl.BlockSpec((8, LANES), lambda i, j: (0, 0))],
        out_shape=[jax.ShapeDtypeStruct((S, n), F32), jax.ShapeDtypeStruct((S, n), BF16),
                   jax.ShapeDtypeStruct((8, LANES), F32)],
        compiler_params=_params("arbitrary", "arbitrary"),
    )(a, w, x1, target)


def _mm_o_rms(ycat, w, x, g, tm=512):
    S, K = ycat.shape
    n = w.shape[1]
    tm = _tile(S, tm)

    def body(y_ref, w_ref, x_ref, g_ref, x1_ref, h2_ref):
        x1 = jnp.dot(y_ref[...], w_ref[...], preferred_element_type=F32) + x_ref[...]
        x1_ref[...] = x1
        xhat, _ = _rms_hat(x1, n)
        h2_ref[...] = (xhat * g_ref[...]).astype(BF16)

    return pl.pallas_call(
        body, name="mm_o_rms", grid=(S // tm,),
        in_specs=[_rows(tm, K), _full((K, n)), _rows(tm, n), _full((1, n))],
        out_specs=[_rows(tm, n), _rows(tm, n)],
        out_shape=[jax.ShapeDtypeStruct((S, n), F32), jax.ShapeDtypeStruct((S, n), BF16)],
        compiler_params=_params("parallel"),
    )(ycat, w, x, g)


def _mm_in_dx_rms(dz, w, x, g, extra, sends, tm=256):
    S, K = dz.shape
    n = w.shape[0]
    tm = _tile(S, tm)
    n_steps = S // tm
    nw = len(sends)

    def body(dz_ref, w_ref, x_ref, g_ref, ex_ref, *rest):
        s_refs, (dx_ref, dg_ref), r_refs = rest[:nw], rest[nw:nw + 2], rest[nw + 2:2 * nw + 2]
        scatter = _Scatter(s_refs, r_refs, *rest[2 * nw + 2:])
        i = pl.program_id(0)
        pl.when(i == 0)(scatter.start)

        @pl.when(i == 0)
        def _():
            dg_ref[...] = jnp.zeros_like(dg_ref)

        dh = lax.dot_general(dz_ref[...], w_ref[...], _NT, preferred_element_type=F32)
        xhat, r = _rms_hat(x_ref[...], n)
        dx, dg = _rms_bwd(dh, xhat, r, g_ref[...], n)
        dx_ref[...] = dx + ex_ref[...]
        dg_ref[...] += dg
        pl.when(i == n_steps - 1)(scatter.finish)

    return pl.pallas_call(
        body, name="mm_in_dx_rms", grid=(n_steps,),
        in_specs=[_rows(tm, K), _full((n, K)), _rows(tm, n), _full((1, n)), _rows(tm, n)] + [_ANY] * nw,
        out_specs=[_rows(tm, n), _full((1, n))] + [_ANY] * nw,
        out_shape=[jax.ShapeDtypeStruct((S, n), F32), jax.ShapeDtypeStruct((1, n), F32)]
        + [jax.ShapeDtypeStruct((N_DEV - 1,) + sd.shape[1:], sd.dtype) for sd in sends],
        scratch_shapes=[pltpu.SemaphoreType.DMA((nw, N_DEV - 1)), pltpu.SemaphoreType.DMA((nw, N_DEV - 1))],
        compiler_params=_params("arbitrary"),
    )(dz, w, x, g, extra, *sends)


def _local_step(x, mem, target, W, early_shards, late_shards):
    S = x.shape[0]
    tabs = _rope_tables(S)
    W = dict(W)
    G = {}

    h, W["w_in"], W["w_q_up"], W["w_kv_up"] = _rms_fwd("rms1_fwd", x, W["g_mix"], early_shards, EARLY_GATHER_KINDS)
    z = _mm("mm_in", h, W["w_in"], tm=512, tn=Z_COLS, tk=D_MODEL)
    y_pool = _pool_fwd(z, W["w_pool"], W["pool_scale"])
    ql, kvl = _lat_fwd(z, W["g_q_lat"], W["g_kv_lat"])
    q_raw = _mm("mm_q_up", ql, W["w_q_up"], out_dtype=BF16, tm=1024, tn=2048, tk=Q_LORA)
    kv_raw = _mm("mm_kv_up", kvl, W["w_kv_up"], out_dtype=BF16, tm=1024, tn=2048, tk=KV_LORA)
    n_ffn = len(FFN_GATHER_KINDS)
    q, k, v, W["w_o"], W["w_mem_kv"] = _mla_prep(q_raw, kv_raw, z, tabs, W["g_q_mla"], W["g_k_mla"],
                                                 late_shards[n_ffn:], MIXER_GATHER_KINDS)
    o_mla, lse, wg3, wu3, W["w_down"], cw3 = _flash_fwd(q, k, v, late_shards[:n_ffn], FFN_GATHER_KINDS)
    W["w_gate"] = jnp.transpose(wg3, (1, 0, 2)).reshape(D_MODEL, D_FF)
    W["w_up"] = jnp.transpose(wu3, (1, 0, 2)).reshape(D_MODEL, D_FF)
    cw = jnp.sum(cw3.reshape(N_DEV, 3, 3, D_FF // N_DEV).astype(F32), axis=1)
    W["conv_w"] = jnp.transpose(cw, (1, 0, 2)).reshape(3, D_FF)
    mn = _mem_norm(mem, W["g_mem"])
    mkv = _mm("mm_mem_kv", mn, W["w_mem_kv"], tm=256, tn=1024, tk=D_MODEL)
    kx, vx = _memkv_prep(mkv, W["g_k_x"])
    y_mem = _memx_fwd(z, W["g_q_x"], kx, vx)
    ycat = jnp.concatenate([y_pool, o_mla, y_mem], axis=1)
    x1, h2 = _mm_o_rms(ycat, W["w_o"], x, W["g_ffn"])
    g = _mm("mm_gate", h2, W["w_gate"], out_dtype=BF16, tm=1024, tn=1408, tk=D_MODEL)
    u = _mm("mm_up", h2, W["w_up"], out_dtype=BF16, tm=1024, tn=1408, tk=D_MODEL)
    a = _convglu_fwd(g, u, W["conv_w"], W["conv_b"])
    dy, dyb, loss_part = _mm_down_loss(a, W["w_down"], x1, target)

    da = _mm("mm_down_dx", dyb, W["w_down"], tb=True, out_dtype=BF16, tm=1024, tn=1408, tk=D_MODEL)
    G["w_down"] = _mm("mm_down_dw", a, dyb, ta=True, tm=512, tn=2048, tk=2048)
    dg, du, G["conv_w"], G["conv_b"] = _convglu_bwd(g, u, da, W["conv_w"], W["conv_b"])
    dh2 = _mm("mm_gate_dx", dg, W["w_gate"], tb=True, tm=1024, tn=512, tk=D_FF)
    G["w_gate"] = _mm("mm_gate_dw", h2, dg, ta=True, tm=1024, tn=1408, tk=2048)
    G["w_up"] = _mm("mm_up_dw", h2, du, ta=True, tm=1024, tn=1408, tk=2048)
    dx1, dx1b, G["g_ffn"] = _mm_up_dx_rms(du, W["w_up"], dh2, x1, W["g_ffn"], dy)

    dyc = _mm("mm_o_dx", dx1b, W["w_o"], tb=True, tm=512, tn=2048, tk=D_MODEL)
    G["w_o"] = _mm("mm_o_dw", ycat, dx1b, ta=True, tm=1024, tn=1024, tk=2048)
    dz_pool, G["w_pool"], G["pool_scale"] = _pool_bwd(z, dyc, W["w_pool"], W["pool_scale"], dy_cb=0)
    dz_mq, dkx, dvx, G["g_q_x"] = _memx_bwd(z, W["g_q_x"], kx, vx, dyc, dy_cb=3)
    dmkv, G["g_k_x"] = _memkv_bwd(dkx, dvx, mkv, W["g_k_x"])
    G["w_mem_kv"] = _mm("mm_mem_kv_dw", mn, dmkv, ta=True, tm=1024, tn=1024, tk=MEM_LEN)
    dmn = _mm("mm_mem_kv_dx", dmkv, W["w_mem_kv"], tb=True, tm=256, tn=2048, tk=1024)
    G["g_mem"] = _mem_gain_bwd(mem, dmn)
    sends = [_send_blocks(n, G[n]) for n in SCATTER_EARLY]
    dq, dk, dv, *got_early = _flash_bwd(q, k, v, dyc, POOL_WIDTH // MLA_V, o_mla, lse, sends)
    dq_raw, dkv_raw, dz_kr, G["g_q_mla"], G["g_k_mla"] = _mla_prep_bwd(
        dq, dk, dv, q_raw, kv_raw, z, tabs, W["g_q_mla"], W["g_k_mla"])
    G["w_q_up"] = _mm("mm_q_up_dw", ql, dq_raw, ta=True, tm=512, tn=2048, tk=1024)
    dql = _mm("mm_q_up_dx", dq_raw, W["w_q_up"], tb=True, tm=1024, tn=512, tk=2048)
    G["w_kv_up"] = _mm("mm_kv_up_dw", kvl, dkv_raw, ta=True, tm=256, tn=2048, tk=1024)
    dkvl = _mm("mm_kv_up_dx", dkv_raw, W["w_kv_up"], tb=True, tm=1024, tn=256, tk=2048)
    dz_q, dz_kv, G["g_q_lat"], G["g_kv_lat"] = _lat_bwd(z, W["g_q_lat"], W["g_kv_lat"], dql, dkvl)
    dz = jnp.concatenate([dz_pool, dz_q, dz_mq, dz_kv, dz_kr], axis=1)
    G["w_in"] = _mm("mm_in_dw", h, dz, ta=True, tm=1024, tn=Z_COLS, tk=1024)
    grad_x, G["g_mix"], *got_late = _mm_in_dx_rms(dz, W["w_in"], x, W["g_mix"], dx1,
                                                  [_send_blocks(n, G[n]) for n in SCATTER_LATE])
    return loss_part, grad_x, G, dict(zip(SCATTER_EARLY + SCATTER_LATE, got_early + got_late))


_ANY = pl.BlockSpec(memory_space=pl.ANY)
_MESH = pl.DeviceIdType.MESH


def _gather_out_shape(kind, shape):
    if kind == "rows":
        return (N_DEV * shape[0],) + tuple(shape[1:])
    if kind == "cols":
        return (shape[0], N_DEV * shape[1])
    return (N_DEV,) + tuple(shape)


def _gather_view(ref, kind, shape, d):
    if kind == "rows":
        return ref.at[pl.ds(pl.multiple_of(d * shape[0], 16), shape[0]), :]
    if kind == "cols":
        return ref.at[:, pl.ds(pl.multiple_of(d * shape[1], math.gcd(shape[1], LANES)), shape[1])]
    return ref.at[d]


class _Gather:
    def __init__(self, x_refs, out_refs, kinds, send_sems, recv_sems, local_sems):
        self.xr, self.outr, self.kinds = x_refs, out_refs, kinds
        self.ss, self.rs, self.ls = send_sems, recv_sems, local_sems
        x, y, c = lax.axis_index("x"), lax.axis_index("y"), lax.axis_index("c")
        self.c = c
        self.me, self.sibling = (x, y, c), (x, y, 1 - c)
        self.chips = [(1 - x, y), (x, 1 - y), (1 - x, 1 - y)]

    def _view(self, w, dev):
        px, py, pc = dev
        return _gather_view(self.outr[w], self.kinds[w], self.xr[w].shape, 4 * px + 2 * py + pc)

    def _copy(self, w, k, block, to, from_shard=False):
        v = self._view(w, block)
        return pltpu.make_async_remote_copy(
            src_ref=self.xr[w] if from_shard else v, dst_ref=v, send_sem=self.ss.at[w, k],
            recv_sem=self.rs.at[w, k], device_id=to, device_id_type=_MESH)

    def _local(self, w):
        return pltpu.make_async_copy(self.xr[w], self._view(w, self.me), self.ls.at[w])

    def start(self):
        for w in range(len(self.xr)):
            self._local(w).start()
            self._copy(w, 0, self.me, self.sibling, True).start()
            for j, chip in enumerate(self.chips):
                self._copy(w, 1 + j, self.me, (*chip, self.c), True).start()

    def forward(self):
        for j, chip in enumerate(self.chips):
            for w in range(len(self.xr)):
                self._copy(w, 1 + j, (*chip, self.c), self.me).wait_recv()
                self._copy(w, 4 + j, (*chip, self.c), self.sibling).start()

    def finish(self):
        for w in range(len(self.xr)):
            self._copy(w, 0, self.sibling, self.me).wait_recv()
            for j, chip in enumerate(self.chips):
                self._copy(w, 4 + j, (*chip, 1 - self.c), self.me).wait_recv()
            self._copy(w, 0, self.me, self.sibling, True).wait_send()
            for j, chip in enumerate(self.chips):
                self._copy(w, 1 + j, self.me, (*chip, self.c), True).wait_send()
                self._copy(w, 4 + j, (*chip, self.c), self.sibling).wait_send()
            self._local(w).wait()


class _Scatter:
    def __init__(self, send_refs, recv_refs, send_sems, recv_sems):
        self.sr, self.rr, self.ss, self.rs = send_refs, recv_refs, send_sems, recv_sems
        self.xyz = lax.axis_index("x"), lax.axis_index("y"), lax.axis_index("c")

    def _copy(self, w, k):
        x, y, c = self.xyz
        px, py, pc = x ^ ((k >> 2) & 1), y ^ ((k >> 1) & 1), c ^ (k & 1)
        return pltpu.make_async_remote_copy(
            src_ref=self.sr[w].at[4 * px + 2 * py + pc], dst_ref=self.rr[w].at[k - 1],
            send_sem=self.ss.at[w, k - 1], recv_sem=self.rs.at[w, k - 1],
            device_id=(px, py, pc), device_id_type=_MESH)

    def _all(self):
        return [self._copy(w, k) for w in range(len(self.sr)) for k in range(1, N_DEV)]

    def start(self):
        for cp in self._all():
            cp.start()

    def finish(self):
        for cp in self._all():
            cp.wait_recv()
        for cp in self._all():
            cp.wait_send()


def _exchange_small(s):
    def body(x_ref, out_ref, x_send, x_recv, local_sem):
        x, y, c = lax.axis_index("x"), lax.axis_index("y"), lax.axis_index("c")
        me = 4 * x + 2 * y + c
        mine = pltpu.make_async_copy(x_ref, out_ref.at[me], local_sem)
        mine.start()

        def copy(k):
            px, py, pc = x ^ ((k >> 2) & 1), y ^ ((k >> 1) & 1), c ^ (k & 1)
            return pltpu.make_async_remote_copy(
                src_ref=x_ref, dst_ref=out_ref.at[me], send_sem=x_send.at[k - 1], recv_sem=x_recv.at[k - 1],
                device_id=(px, py, pc), device_id_type=_MESH)

        cps = [copy(k) for k in range(1, N_DEV)]
        for cp in cps:
            cp.start()
        for cp in cps:
            cp.wait_recv()
        for cp in cps:
            cp.wait_send()
        mine.wait()

    return pl.pallas_call(
        body, name="exchange_small", out_shape=jax.ShapeDtypeStruct((N_DEV,) + s.shape, s.dtype),
        in_specs=[_ANY], out_specs=_ANY,
        scratch_shapes=[pltpu.SemaphoreType.DMA((N_DEV - 1,)), pltpu.SemaphoreType.DMA((N_DEV - 1,)),
                        pltpu.SemaphoreType.DMA],
    )(s)


def _sum_slots(buf, tr=208):
    n, R, _ = buf.shape
    tr = tr if R % tr == 0 else R

    def body(b_ref, o_ref):
        acc = b_ref[0]
        for d in range(1, n):
            acc = acc + b_ref[d]
        o_ref[...] = acc

    return pl.pallas_call(
        body, name="sum_slots", grid=(R // tr,),
        in_specs=[pl.BlockSpec((n, tr, LANES), lambda i: (0, i, 0))],
        out_specs=pl.BlockSpec((tr, LANES), lambda i: (i, 0)),
        out_shape=jax.ShapeDtypeStruct((R, LANES), buf.dtype),
        compiler_params=_params("parallel"),
    )(buf)


def _adamw_math(w, g, m, v):
    m_new = ADAM_B1 * m + (1.0 - ADAM_B1) * g
    v_new = ADAM_B2 * v + (1.0 - ADAM_B2) * (g * g)
    m_hat = m_new / (1.0 - ADAM_B1 ** ADAM_STEP)
    v_hat = v_new / (1.0 - ADAM_B2 ** ADAM_STEP)
    return -ADAM_LR * (m_hat / (jnp.sqrt(v_hat) + ADAM_EPS) + ADAM_WD * w), m_new, v_new


def _adamw(name, w, g, got, m, v, tr=128):
    R, C = w.shape
    tr = max([t for t in range(16, tr + 1, 16) if R % t == 0], default=R) if R > tr else R

    def body(w_ref, g_ref, got_ref, m_ref, v_ref, go_ref, d_ref, mo_ref, vo_ref):
        gv = g_ref[...]
        for k in range(N_DEV - 1):
            gv = gv + got_ref[k].astype(F32)
        go_ref[...] = gv
        d_ref[...], mo_ref[...], vo_ref[...] = _adamw_math(w_ref[...], gv, m_ref[...], v_ref[...])

    spec = pl.BlockSpec((tr, C), lambda i: (i, 0))
    got_spec = pl.BlockSpec((N_DEV - 1, tr, C), lambda i: (0, i, 0))
    sds = jax.ShapeDtypeStruct((R, C), F32)
    return pl.pallas_call(
        body, name=name, grid=(R // tr,), in_specs=[spec, spec, got_spec, spec, spec], out_specs=[spec] * 4,
        out_shape=[sds] * 4, compiler_params=_params("parallel"),
    )(w, g, got, m, v)


def _adamw_small(ws, gs, ms, vs):
    n = len(ws)

    def body(*refs):
        ins, outs = refs[:4 * n], refs[4 * n:]
        for p in range(n):
            w_ref, g_ref, m_ref, v_ref = (ins[q * n + p] for q in range(4))
            res = _adamw_math(w_ref[...], g_ref[...], m_ref[...], v_ref[...])
            for q in range(3):
                outs[q * n + p][...] = res[q]

    sds = [jax.ShapeDtypeStruct(w.shape, F32) for w in ws]
    outs = pl.pallas_call(body, name="adamw_small", out_shape=sds * 3)(*ws, *gs, *ms, *vs)
    return outs[:n], outs[n:2 * n], outs[2 * n:]


def _pad_rows(a, mult):
    r = (-a.shape[0]) % mult
    return a if r == 0 else jnp.concatenate([a, jnp.zeros((r,) + a.shape[1:], a.dtype)], axis=0)


def _as_rows(a, mult):
    flat = a.reshape(-1)
    r = (-flat.shape[0]) % LANES
    if r:
        flat = jnp.concatenate([flat, jnp.zeros((r,), a.dtype)])
    return _pad_rows(flat.reshape(-1, LANES), mult)


def _w_in_to_kernel_cols(w):
    pad = jnp.zeros(w.shape[:-1] + (Z_COLS - IN_COLS,), w.dtype)
    return jnp.concatenate([w[..., :1024], w[..., 1344:1856], w[..., 1024:1344], pad], axis=-1)


def _w_in_from_kernel_cols(w):
    return jnp.concatenate([w[..., :1024], w[..., 1536:1856], w[..., 1024:1536]], axis=-1)


def _dev_blocks(n, g):
    ffs = D_FF // N_DEV
    if n == "w_in":
        return _w_in_from_kernel_cols(g).reshape(N_DEV, D_MODEL // N_DEV, IN_COLS)
    if n == "w_q_up":
        return jnp.transpose(g.reshape(Q_LORA, N_DEV, MLA_QK_PAD)[:, :, :MLA_QK], (1, 0, 2))
    if n == "w_kv_up":
        return jnp.transpose(g.reshape(KV_LORA, N_DEV, MLA_NOPE + MLA_V), (1, 0, 2))
    if n in ("w_mem_kv", "w_o"):
        return g.reshape(N_DEV, D_MODEL // N_DEV, g.shape[1])
    if n in ("w_gate", "w_up"):
        return jnp.transpose(g.reshape(D_MODEL, N_DEV, ffs), (1, 0, 2))
    if n == "conv_w":
        return jnp.transpose(g.reshape(3, N_DEV, ffs), (1, 0, 2))
    assert n == "w_down"
    return g.reshape(N_DEV, ffs, D_MODEL)


def _send_blocks(n, g):
    blocks = _dev_blocks(n, g)
    return blocks if n == "conv_w" else blocks.astype(BF16)


def kernel(x, mem, g_mix, w_in, g_q_lat, w_q_up, g_kv_lat, w_kv_up, g_q_mla, g_k_mla, w_pool, pool_scale, g_mem, w_mem_kv, g_q_x, g_k_x, w_o, g_ffn, w_gate, w_up, conv_w, conv_b, w_down, loss_target, m_g_mix, m_w_in, m_g_q_lat, m_w_q_up, m_g_kv_lat, m_w_kv_up, m_g_q_mla, m_g_k_mla, m_w_pool, m_pool_scale, m_g_mem, m_w_mem_kv, m_g_q_x, m_g_k_x, m_w_o, m_g_ffn, m_w_gate, m_w_up, m_conv_w, m_conv_b, m_w_down, v_g_mix, v_w_in, v_g_q_lat, v_w_q_up, v_g_kv_lat, v_w_kv_up, v_g_q_mla, v_g_k_mla, v_w_pool, v_pool_scale, v_g_mem, v_w_mem_kv, v_g_q_x, v_g_k_x, v_w_o, v_g_ffn, v_w_gate, v_w_up, v_conv_w, v_conv_b, v_w_down):
    given = dict(g_mix=g_mix, w_in=w_in, g_q_lat=g_q_lat, w_q_up=w_q_up, g_kv_lat=g_kv_lat, w_kv_up=w_kv_up,
                 g_q_mla=g_q_mla, g_k_mla=g_k_mla, w_pool=w_pool, pool_scale=pool_scale, g_mem=g_mem,
                 w_mem_kv=w_mem_kv, g_q_x=g_q_x, g_k_x=g_k_x, w_o=w_o, g_ffn=g_ffn, w_gate=w_gate, w_up=w_up,
                 conv_w=conv_w, conv_b=conv_b, w_down=w_down)
    mom_m = dict(g_mix=m_g_mix, w_in=m_w_in, g_q_lat=m_g_q_lat, w_q_up=m_w_q_up, g_kv_lat=m_g_kv_lat,
                 w_kv_up=m_w_kv_up, g_q_mla=m_g_q_mla, g_k_mla=m_g_k_mla, w_pool=m_w_pool,
                 pool_scale=m_pool_scale, g_mem=m_g_mem, w_mem_kv=m_w_mem_kv, g_q_x=m_g_q_x, g_k_x=m_g_k_x,
                 w_o=m_w_o, g_ffn=m_g_ffn, w_gate=m_w_gate, w_up=m_w_up, conv_w=m_conv_w, conv_b=m_conv_b,
                 w_down=m_w_down)
    mom_v = dict(g_mix=v_g_mix, w_in=v_w_in, g_q_lat=v_g_q_lat, w_q_up=v_w_q_up, g_kv_lat=v_g_kv_lat,
                 w_kv_up=v_w_kv_up, g_q_mla=v_g_q_mla, g_k_mla=v_g_k_mla, w_pool=v_w_pool,
                 pool_scale=v_pool_scale, g_mem=v_g_mem, w_mem_kv=v_w_mem_kv, g_q_x=v_g_q_x, g_k_x=v_g_k_x,
                 w_o=v_w_o, g_ffn=v_g_ffn, w_gate=v_w_gate, w_up=v_w_up, conv_w=v_conv_w, conv_b=v_conv_b,
                 w_down=v_w_down)
    drop = lambda a: a[0] if a.ndim > 2 else a
    sh = {n: drop(given[n]) for n in WEIGHTS}
    mom_m = {n: drop(mom_m[n]) for n in WEIGHTS}
    mom_v = {n: drop(mom_v[n]) for n in WEIGHTS}

    cw_hi = sh["conv_w"].astype(BF16)
    cw_r = sh["conv_w"] - cw_hi.astype(F32)
    cw_mid = cw_r.astype(BF16)
    cw_lo = (cw_r - cw_mid.astype(F32)).astype(BF16)
    early_shards = [
        _w_in_to_kernel_cols(sh["w_in"]).astype(BF16),
        jnp.pad(sh["w_q_up"], ((0, 0), (0, MLA_QK_PAD - MLA_QK))).astype(BF16),
        sh["w_kv_up"].astype(BF16),
    ]
    late_shards = [sh["w_gate"].astype(BF16), sh["w_up"].astype(BF16), sh["w_down"].astype(BF16),
                   jnp.concatenate([cw_hi, cw_mid, cw_lo], axis=0), sh["w_o"].astype(BF16),
                   sh["w_mem_kv"].astype(BF16)]
    W = {"w_pool": sh["w_pool"].astype(BF16)}
    for n in ("g_mix", "g_q_lat", "g_kv_lat", "pool_scale", "g_mem", "g_q_x", "g_k_x", "g_ffn", "conv_b"):
        W[n] = sh[n]
    pad_qk = lambda gv: jnp.pad(gv, ((0, 0), (0, MLA_QK_PAD - MLA_QK)))
    W["g_q_mla"], W["g_k_mla"] = pad_qk(sh["g_q_mla"]), pad_qk(sh["g_k_mla"])

    loss_part, grad_x, G, got = _local_step(x[0], mem[0], loss_target[0], W, early_shards, late_shards)

    small = {
        "g_mix": G["g_mix"], "g_q_lat": G["g_q_lat"], "g_kv_lat": G["g_kv_lat"],
        "g_q_mla": G["g_q_mla"][:, :MLA_QK], "g_k_mla": G["g_k_mla"][:, :MLA_QK],
        "w_pool": G["w_pool"], "pool_scale": G["pool_scale"], "g_mem": G["g_mem"],
        "g_q_x": G["g_q_x"], "g_k_x": G["g_k_x"], "g_ffn": G["g_ffn"], "conv_b": G["conv_b"],
    }
    s_offs = {}
    segs = []
    off = 0
    for n in SMALL:
        r = _as_rows(small[n], 8)
        s_offs[n] = off
        off += r.shape[0]
        segs.append(r)
    segs.append(loss_part)
    loss_row = off
    sbuf = jnp.concatenate(segs, axis=0)
    s_sum = _sum_slots(_exchange_small(sbuf))

    def small_take(buf, n):
        shape = sh[n].shape
        cnt = math.prod(shape)
        return buf[s_offs[n]:s_offs[n] + -(-cnt // LANES)].reshape(-1)[:cnt].reshape(shape)

    loss = s_sum[loss_row, 0]

    me = 4 * lax.axis_index("x") + 2 * lax.axis_index("y") + lax.axis_index("c")
    grads, deltas, new_m, new_v = {}, {}, {}, {}
    for n in BIG:
        shape = sh[n].shape
        own = lax.dynamic_index_in_dim(_dev_blocks(n, G[n]), me, 0, keepdims=False)
        flat = lambda a: a.reshape(-1, shape[-1])
        outs = _adamw("adamw_" + n, flat(sh[n]), flat(own), got[n].reshape(N_DEV - 1, -1, shape[-1]),
                      flat(mom_m[n]), flat(mom_v[n]))
        grads[n], deltas[n], new_m[n], new_v[n] = (o.reshape(shape) for o in outs)

    for n in SMALL:
        grads[n] = small_take(s_sum, n)
    d_s, m_s, v_s = _adamw_small([sh[n] for n in SMALL], [grads[n] for n in SMALL],
                                 [mom_m[n] for n in SMALL], [mom_v[n] for n in SMALL])
    for j, n in enumerate(SMALL):
        deltas[n], new_m[n], new_v[n] = d_s[j], m_s[j], v_s[j]

    lead = lambda n, a: a.reshape(given[n].shape)
    return (loss, grad_x[None],
            *[lead(n, grads[n]) for n in WEIGHTS], *[lead(n, deltas[n]) for n in WEIGHTS],
            *[lead(n, new_m[n]) for n in WEIGHTS], *[lead(n, new_v[n]) for n in WEIGHTS])
```

```python
import math

import jax
import jax.numpy as jnp
from jax import lax
from jax.experimental import pallas as pl
from jax.experimental.pallas import tpu as pltpu

F32 = jnp.float32
BF16 = jnp.bfloat16

D_MODEL = 2048
D_FF = 5632
POOL_WIDTH = 512
POOL_WINDOWS = (2, 4, 8, 16)
POOL_HALO = 16
MLA_HEADS = 8
MLA_NOPE = 128
MLA_ROPE = 64
MLA_QK = MLA_NOPE + MLA_ROPE
MLA_QK_PAD = 256
MLA_V = 128
Q_LORA = 512
KV_LORA = 256
X_HEADS = 4
X_HEAD_DIM = 128
X_WIDTH = 512
MEM_LEN = 256
ROPE_THETA = 10000.0
NORM_EPS = 1e-6
CONV_HALO = 16
IN_COLS = 1856
Z_COLS = 1920
Z_POOL, Z_Q, Z_MQ, Z_KV, Z_KR = 0, 512, 1024, 1536, 1792

ADAM_LR = 0.001
ADAM_B1 = 0.9
ADAM_B2 = 0.999
ADAM_EPS = 1e-08
ADAM_WD = 0.01
ADAM_STEP = 10

N_DEV = 8
LANES = 128
VMEM_LIMIT = 56 * 1024 * 1024

BIG = ("w_in", "w_q_up", "w_kv_up", "w_mem_kv", "w_o", "w_gate", "w_up", "conv_w", "w_down")
SCATTER_EARLY = ("w_down", "w_gate", "w_up", "conv_w", "w_o", "w_mem_kv")
SCATTER_LATE = ("w_in", "w_q_up", "w_kv_up")
EARLY_GATHER_KINDS = ("rows", "cols", "cols")
FFN_GATHER_KINDS = ("lead", "lead", "rows", "lead")
MIXER_GATHER_KINDS = ("rows", "rows")
SMALL = ("g_mix", "g_q_lat", "g_kv_lat", "g_q_mla", "g_k_mla", "w_pool", "pool_scale", "g_mem",
         "g_q_x", "g_k_x", "g_ffn", "conv_b")
WEIGHTS = ("g_mix", "w_in", "g_q_lat", "w_q_up", "g_kv_lat", "w_kv_up", "g_q_mla", "g_k_mla", "w_pool",
           "pool_scale", "g_mem", "w_mem_kv", "g_q_x", "g_k_x", "w_o", "g_ffn", "w_gate", "w_up", "conv_w",
           "conv_b", "w_down")


def _tile(n, t):
    if n <= t:
        return n
    for c in range(t - t % LANES, 0, -LANES):
        if n % c == 0:
            return c
    return n


def _params(*sem):
    return pltpu.CompilerParams(dimension_semantics=sem, vmem_limit_bytes=VMEM_LIMIT)


def _full(shape):
    nd = len(shape)
    return pl.BlockSpec(shape, lambda *_: (0,) * nd)


def _rows(tr, w, cb=0):
    return pl.BlockSpec((tr, w), lambda i: (i, cb))


def _rms_hat(x, n):
    r = lax.rsqrt(jnp.sum(x * x, axis=-1, keepdims=True) / n + NORM_EPS)
    return x * r, r


def _rms_bwd(dy, xhat, r, g, n):
    dxh = dy * g
    dx = r * (dxh - xhat * (jnp.sum(dxh * xhat, axis=-1, keepdims=True) / n))
    return dx, jnp.sum(dy * xhat, axis=0, keepdims=True)


def _mm(name, a, b, *, ta=False, tb=False, add=None, out_dtype=F32, also_bf16=False, tm=512, tn=512, tk=512):
    if ta:
        K, M = a.shape
    else:
        M, K = a.shape
    if tb:
        N, K2 = b.shape
    else:
        K2, N = b.shape
    assert K == K2, (name, a.shape, b.shape)
    tm, tn, tk = _tile(M, tm), _tile(N, tn), _tile(K, tk)
    nk = K // tk
    grid = (M // tm, N // tn, nk)
    a_spec = pl.BlockSpec((tk, tm), lambda i, j, k: (k, i)) if ta else pl.BlockSpec((tm, tk), lambda i, j, k: (i, k))
    b_spec = pl.BlockSpec((tn, tk), lambda i, j, k: (j, k)) if tb else pl.BlockSpec((tk, tn), lambda i, j, k: (k, j))
    o_spec = pl.BlockSpec((tm, tn), lambda i, j, k: (i, j))
    dims = (((0,) if ta else (1,), (1,) if tb else (0,)), ((), ()))
    has_add = add is not None

    n_in = 3 if has_add else 2
    n_out = 2 if also_bf16 else 1

    def body(*refs):
        a_ref, b_ref = refs[:2]
        add_ref = refs[2] if has_add else None
        o_refs = refs[n_in:n_in + n_out]

        def write(r):
            if has_add:
                r = r + add_ref[...]
            for o_ref in o_refs:
                o_ref[...] = r.astype(o_ref.dtype)

        part = lax.dot_general(a_ref[...], b_ref[...], dims, preferred_element_type=F32)
        if nk == 1:
            write(part)
            return
        acc = refs[-1]
        k = pl.program_id(2)

        @pl.when(k == 0)
        def _():
            acc[...] = part

        @pl.when(k > 0)
        def _():
            acc[...] += part

        @pl.when(k == nk - 1)
        def _():
            write(acc[...])

    ins = [a, b] + ([add] if has_add else [])
    in_specs = [a_spec, b_spec] + ([o_spec] if has_add else [])
    out_shape = [jax.ShapeDtypeStruct((M, N), out_dtype)] + ([jax.ShapeDtypeStruct((M, N), BF16)] if also_bf16 else [])
    outs = pl.pallas_call(
        body, name=name, grid=grid, in_specs=in_specs, out_specs=[o_spec] * n_out, out_shape=out_shape,
        scratch_shapes=[pltpu.VMEM((tm, tn), F32)] if nk > 1 else [],
        compiler_params=_params("parallel", "parallel", "arbitrary"),
    )(*ins)
    return outs if also_bf16 else outs[0]


def _rms_fwd(name, x, g, shards, kinds, tr=512):
    S, n = x.shape
    tr = _tile(S, tr)
    n_steps = S // tr
    nw = len(shards)

    def body(x_ref, g_ref, *rest):
        x_refs, h_ref, g_refs = rest[:nw], rest[nw], rest[nw + 1:2 * nw + 1]
        gather = _Gather(x_refs, g_refs, kinds, *rest[2 * nw + 1:])
        i = pl.program_id(0)
        pl.when(i == 0)(gather.start)
        pl.when(i == n_steps // 2)(gather.forward)
        xhat, _ = _rms_hat(x_ref[...], n)
        h_ref[...] = (xhat * g_ref[...]).astype(BF16)
        pl.when(i == n_steps - 1)(gather.finish)

    return pl.pallas_call(
        body, name=name, grid=(n_steps,),
        in_specs=[_rows(tr, n), _full((1, n))] + [_ANY] * nw, out_specs=[_rows(tr, n)] + [_ANY] * nw,
        out_shape=[jax.ShapeDtypeStruct((S, n), BF16)]
        + [jax.ShapeDtypeStruct(_gather_out_shape(kd, sd.shape), sd.dtype) for kd, sd in zip(kinds, shards)],
        scratch_shapes=[pltpu.SemaphoreType.DMA((nw, 7)), pltpu.SemaphoreType.DMA((nw, 7)),
                        pltpu.SemaphoreType.DMA((nw,))],
        compiler_params=_params("arbitrary"),
    )(x, g, *shards)


def _mm_up_dx_rms(du, w, part, x, g, extra, tm=256):
    S, K = du.shape
    n = w.shape[0]
    tm = _tile(S, tm)

    def body(du_ref, w_ref, p_ref, x_ref, g_ref, ex_ref, dx_ref, dxb_ref, dg_ref):
        @pl.when(pl.program_id(0) == 0)
        def _():
            dg_ref[...] = jnp.zeros_like(dg_ref)

        dh = lax.dot_general(du_ref[...], w_ref[...], _NT, preferred_element_type=F32) + p_ref[...]
        xhat, r = _rms_hat(x_ref[...], n)
        dx, dg = _rms_bwd(dh, xhat, r, g_ref[...], n)
        dx = dx + ex_ref[...]
        dx_ref[...] = dx
        dxb_ref[...] = dx.astype(BF16)
        dg_ref[...] += dg

    return pl.pallas_call(
        body, name="mm_up_dx_rms", grid=(S // tm,),
        in_specs=[_rows(tm, K), pl.BlockSpec((n, K), lambda i: (0, 0), pipeline_mode=pl.Buffered(1)),
                  _rows(tm, n), _rows(tm, n), _full((1, n)), _rows(tm, n)],
        out_specs=[_rows(tm, n), _rows(tm, n), _full((1, n))],
        out_shape=[jax.ShapeDtypeStruct((S, n), F32), jax.ShapeDtypeStruct((S, n), BF16),
                   jax.ShapeDtypeStruct((1, n), F32)],
        compiler_params=_params("arbitrary"),
    )(du, w, part, x, g, extra)


def _lat_fwd(z, gq, gkv, tr=512):
    S = z.shape[0]
    tr = _tile(S, tr)

    def body(zq_ref, zkv_ref, gq_ref, gkv_ref, ql_ref, kvl_ref):
        xq, _ = _rms_hat(zq_ref[...], Q_LORA)
        ql_ref[...] = (xq * gq_ref[...]).astype(BF16)
        xkv, _ = _rms_hat(zkv_ref[...], KV_LORA)
        kvl_ref[...] = (xkv * gkv_ref[...]).astype(BF16)

    return pl.pallas_call(
        body, name="lat_fwd", grid=(S // tr,),
        in_specs=[_rows(tr, Q_LORA, Z_Q // Q_LORA), _rows(tr, KV_LORA, Z_KV // KV_LORA),
                  _full((1, Q_LORA)), _full((1, KV_LORA))],
        out_specs=[_rows(tr, Q_LORA), _rows(tr, KV_LORA)],
        out_shape=[jax.ShapeDtypeStruct((S, Q_LORA), BF16), jax.ShapeDtypeStruct((S, KV_LORA), BF16)],
        compiler_params=_params("parallel"),
    )(z, z, gq, gkv)


def _lat_bwd(z, gq, gkv, dql, dkvl, tr=512):
    S = z.shape[0]
    tr = _tile(S, tr)

    def body(zq_ref, zkv_ref, gq_ref, gkv_ref, dql_ref, dkvl_ref, dzq_ref, dzkv_ref, dgq_ref, dgkv_ref):
        @pl.when(pl.program_id(0) == 0)
        def _():
            dgq_ref[...] = jnp.zeros_like(dgq_ref)
            dgkv_ref[...] = jnp.zeros_like(dgkv_ref)

        xq, rq = _rms_hat(zq_ref[...], Q_LORA)
        dx, dg = _rms_bwd(dql_ref[...], xq, rq, gq_ref[...], Q_LORA)
        dzq_ref[...] = dx.astype(BF16)
        dgq_ref[...] += dg
        xkv, rkv = _rms_hat(zkv_ref[...], KV_LORA)
        dx, dg = _rms_bwd(dkvl_ref[...], xkv, rkv, gkv_ref[...], KV_LORA)
        dzkv_ref[...] = dx.astype(BF16)
        dgkv_ref[...] += dg

    return pl.pallas_call(
        body, name="lat_bwd", grid=(S // tr,),
        in_specs=[_rows(tr, Q_LORA, Z_Q // Q_LORA), _rows(tr, KV_LORA, Z_KV // KV_LORA),
                  _full((1, Q_LORA)), _full((1, KV_LORA)), _rows(tr, Q_LORA), _rows(tr, KV_LORA)],
        out_specs=[_rows(tr, Q_LORA), _rows(tr, KV_LORA), _full((1, Q_LORA)), _full((1, KV_LORA))],
        out_shape=[jax.ShapeDtypeStruct((S, Q_LORA), BF16), jax.ShapeDtypeStruct((S, KV_LORA), BF16),
                   jax.ShapeDtypeStruct((1, Q_LORA), F32), jax.ShapeDtypeStruct((1, KV_LORA), F32)],
        compiler_params=_params("arbitrary"),
    )(z, z, gq, gkv, dql, dkvl)


def _pool_d(ext, cur, t0, tr):
    t = t0 + lax.broadcasted_iota(jnp.int32, (tr, 1), 0)
    ds = []
    for gi, w in enumerate(POOL_WINDOWS):
        s = ext[:, gi * LANES:(gi + 1) * LANES]
        sh = 1
        while sh < w:
            s = s + pltpu.roll(s, sh, 0)
            sh *= 2
        cnt = jnp.minimum(t + 1, w).astype(F32)
        ds.append(s[POOL_HALO:] / cnt - cur[:, gi * LANES:(gi + 1) * LANES])
    return ds


def _pool_fwd(z, wp, scale, tr=512):
    S = z.shape[0]
    tr = _tile(S, tr)
    rb = tr // POOL_HALO

    def body(z_ref, zp_ref, wp_ref, sc_ref, y_ref):
        i = pl.program_id(0)
        cur = z_ref[...]
        halo = jnp.where(i == 0, 0.0, zp_ref[...])
        ds = _pool_d(jnp.concatenate([halo, cur], axis=0), cur, i * tr, tr)
        for gi in range(len(POOL_WINDOWS)):
            y = jnp.dot(ds[gi].astype(BF16), wp_ref[gi], preferred_element_type=F32)
            y_ref[:, gi * LANES:(gi + 1) * LANES] = (y * sc_ref[:, gi * LANES:(gi + 1) * LANES]).astype(BF16)

    return pl.pallas_call(
        body, name="pool_fwd", grid=(S // tr,),
        in_specs=[_rows(tr, POOL_WIDTH, Z_POOL // POOL_WIDTH),
                  pl.BlockSpec((POOL_HALO, POOL_WIDTH), lambda i: (jnp.maximum(i * rb - 1, 0), 0)),
                  _full(wp.shape), _full((1, POOL_WIDTH))],
        out_specs=_rows(tr, POOL_WIDTH),
        out_shape=jax.ShapeDtypeStruct((S, POOL_WIDTH), BF16),
        compiler_params=_params("parallel"),
    )(z, z, wp, scale)


def _pool_bwd(z, dyc, wp, scale, dy_cb, tr=512):
    S = z.shape[0]
    tr = _tile(S, tr)
    rb = tr // POOL_HALO
    nhb = S // POOL_HALO
    ng = len(POOL_WINDOWS)

    def body(z_ref, zp_ref, dy_ref, dyn_ref, wp_ref, sc_ref, dz_ref, dwp_ref, dsc_ref):
        i = pl.program_id(0)

        @pl.when(i == 0)
        def _():
            dwp_ref[...] = jnp.zeros_like(dwp_ref)
            dsc_ref[...] = jnp.zeros_like(dsc_ref)

        cur = z_ref[...]
        halo = jnp.where(i == 0, 0.0, zp_ref[...])
        ds = _pool_d(jnp.concatenate([halo, cur], axis=0), cur, i * tr, tr)
        dy_cur = dy_ref[...]
        dy_next = jnp.where(i == pl.num_programs(0) - 1, 0.0, dyn_ref[...])
        dy_ext = jnp.concatenate([dy_cur, dy_next], axis=0)
        n_ext = tr + POOL_HALO
        t_ext = i * tr + lax.broadcasted_iota(jnp.int32, (n_ext, 1), 0)
        for gi, w in enumerate(POOL_WINDOWS):
            cols = slice(gi * LANES, (gi + 1) * LANES)
            d_b = ds[gi].astype(BF16)
            y_pre = jnp.dot(d_b, wp_ref[gi], preferred_element_type=F32)
            dsc_ref[:, cols] += jnp.sum(dy_cur[:, cols] * y_pre, axis=0, keepdims=True)
            dys = (dy_ext[:, cols] * sc_ref[:, cols]).astype(BF16)
            dwp_ref[gi] += lax.dot_general(d_b, dys[:tr], (((0,), (0,)), ((), ())), preferred_element_type=F32)
            dd = lax.dot_general(dys, wp_ref[gi], (((1,), (1,)), ((), ())), preferred_element_type=F32)
            s = dd / jnp.minimum(t_ext + 1, w).astype(F32)
            sh = 1
            while sh < w:
                s = s + pltpu.roll(s, n_ext - sh, 0)
                sh *= 2
            dz_ref[:, cols] = (s[:tr] - dd[:tr]).astype(BF16)

    return pl.pallas_call(
        body, name="pool_bwd", grid=(S // tr,),
        in_specs=[_rows(tr, POOL_WIDTH, Z_POOL // POOL_WIDTH),
                  pl.BlockSpec((POOL_HALO, POOL_WIDTH), lambda i: (jnp.maximum(i * rb - 1, 0), 0)),
                  _rows(tr, POOL_WIDTH, dy_cb),
                  pl.BlockSpec((POOL_HALO, POOL_WIDTH), lambda i: (jnp.minimum((i + 1) * rb, nhb - 1), dy_cb)),
                  _full(wp.shape), _full((1, POOL_WIDTH))],
        out_specs=[_rows(tr, POOL_WIDTH), _full((ng, LANES, LANES)), _full((1, POOL_WIDTH))],
        out_shape=[jax.ShapeDtypeStruct((S, POOL_WIDTH), BF16), jax.ShapeDtypeStruct((ng, LANES, LANES), F32),
                   jax.ShapeDtypeStruct((1, POOL_WIDTH), F32)],
        compiler_params=_params("arbitrary"),
    )(z, z, dyc, dyc, wp, scale)


def _rope_tables(S):
    half = MLA_ROPE // 2
    inv_freq = 1.0 / (ROPE_THETA ** (jnp.arange(half, dtype=F32) / half))
    ang = jnp.arange(S).astype(F32)[:, None] * inv_freq[None, :]
    cos, sin = jnp.cos(ang), jnp.sin(ang)
    z32 = jnp.zeros((S, half), F32)
    z64 = jnp.zeros((S, LANES - MLA_ROPE), F32)
    cos_t = jnp.concatenate([cos, cos, z64], axis=1)
    sin_a = jnp.concatenate([-sin, z32, z64], axis=1)
    sin_b = jnp.concatenate([z32, sin, z64], axis=1)
    return cos_t, sin_a, sin_b


def _rope(x, cos_t, sin_a, sin_b):
    return x * cos_t + pltpu.roll(x, LANES - 32, 1) * sin_a + pltpu.roll(x, 32, 1) * sin_b


def _rope_t(d, cos_t, sin_a, sin_b):
    return d * cos_t + pltpu.roll(d * sin_a, 32, 1) + pltpu.roll(d * sin_b, LANES - 32, 1)


def _mla_prep(q_raw, kv_raw, z, tabs, gq, gk, shards, kinds, tr=256):
    S = z.shape[0]
    tr = _tile(S, tr)
    n_steps = S // tr
    scale = 1.0 / math.sqrt(MLA_QK)
    W = MLA_HEADS * MLA_QK_PAD
    nw = len(shards)

    def body(q_ref, kv_ref, kr_ref, c_ref, sa_ref, sb_ref, gq_ref, gk_ref, *rest):
        x_refs, (qo_ref, ko_ref, vo_ref), g_refs = rest[:nw], rest[nw:nw + 3], rest[nw + 3:2 * nw + 3]
        gather = _Gather(x_refs, g_refs, kinds, *rest[2 * nw + 3:])
        i = pl.program_id(0)
        pl.when(i == 0)(gather.start)
        pl.when(i == n_steps // 2)(gather.forward)
        tab = (c_ref[...], sa_ref[...], sb_ref[...])
        kr = kr_ref[...]
        kr_ss = jnp.sum(kr * kr, axis=-1, keepdims=True)
        gqn, gqr = gq_ref[:, :LANES], gq_ref[:, LANES:]
        gkn, gkr = gk_ref[:, :LANES], gk_ref[:, LANES:]
        for h in range(MLA_HEADS):
            c0 = h * MLA_QK_PAD
            qn = q_ref[:, c0:c0 + LANES].astype(F32)
            qr = q_ref[:, c0 + LANES:c0 + 2 * LANES].astype(F32)
            r = lax.rsqrt((jnp.sum(qn * qn, -1, keepdims=True) + jnp.sum(qr * qr, -1, keepdims=True)) / MLA_QK
                          + NORM_EPS)
            qo_ref[h, :, :LANES] = (qn * r * gqn * scale).astype(BF16)
            qo_ref[h, :, LANES:] = (_rope(qr * r * gqr, *tab) * scale).astype(BF16)
            kn = kv_ref[:, c0:c0 + LANES].astype(F32)
            r = lax.rsqrt((jnp.sum(kn * kn, -1, keepdims=True) + kr_ss) / MLA_QK + NORM_EPS)
            ko_ref[h, :, :LANES] = (kn * r * gkn).astype(BF16)
            ko_ref[h, :, LANES:] = _rope(kr * r * gkr, *tab).astype(BF16)
            vo_ref[h] = kv_ref[:, c0 + LANES:c0 + 2 * LANES].astype(BF16)
        pl.when(i == n_steps - 1)(gather.finish)

    hs = lambda w: pl.BlockSpec((MLA_HEADS, tr, w), lambda i: (0, i, 0))
    return pl.pallas_call(
        body, name="mla_prep", grid=(n_steps,),
        in_specs=[_rows(tr, W), _rows(tr, W), _rows(tr, LANES, Z_KR // LANES),
                  _rows(tr, LANES), _rows(tr, LANES), _rows(tr, LANES),
                  _full((1, MLA_QK_PAD)), _full((1, MLA_QK_PAD))] + [_ANY] * nw,
        out_specs=[hs(MLA_QK_PAD), hs(MLA_QK_PAD), hs(MLA_V)] + [_ANY] * nw,
        out_shape=[jax.ShapeDtypeStruct((MLA_HEADS, S, MLA_QK_PAD), BF16),
                   jax.ShapeDtypeStruct((MLA_HEADS, S, MLA_QK_PAD), BF16),
                   jax.ShapeDtypeStruct((MLA_HEADS, S, MLA_V), BF16)]
        + [jax.ShapeDtypeStruct(_gather_out_shape(kd, sd.shape), sd.dtype) for kd, sd in zip(kinds, shards)],
        scratch_shapes=[pltpu.SemaphoreType.DMA((nw, 7)), pltpu.SemaphoreType.DMA((nw, 7)),
                        pltpu.SemaphoreType.DMA((nw,))],
        compiler_params=_params("arbitrary"),
    )(q_raw, kv_raw, z, *tabs, gq, gk, *shards)


def _mla_prep_bwd(dq, dk, dv, q_raw, kv_raw, z, tabs, gq, gk, tr=256):
    S = z.shape[0]
    tr = _tile(S, tr)
    scale = 1.0 / math.sqrt(MLA_QK)
    W = MLA_HEADS * MLA_QK_PAD

    def body(dq_ref, dk_ref, dv_ref, q_ref, kv_ref, kr_ref, c_ref, sa_ref, sb_ref, gq_ref, gk_ref,
             dqr_ref, dkvr_ref, dkr_ref, dgq_ref, dgk_ref):
        @pl.when(pl.program_id(0) == 0)
        def _():
            dgq_ref[...] = jnp.zeros_like(dgq_ref)
            dgk_ref[...] = jnp.zeros_like(dgk_ref)

        tab = (c_ref[...], sa_ref[...], sb_ref[...])
        kr = kr_ref[...]
        kr_ss = jnp.sum(kr * kr, axis=-1, keepdims=True)
        gqn, gqr = gq_ref[:, :LANES], gq_ref[:, LANES:]
        gkn, gkr = gk_ref[:, :LANES], gk_ref[:, LANES:]
        dkr_sum = jnp.zeros((tr, LANES), F32)
        dgq_n = jnp.zeros((1, LANES), F32)
        dgq_r = jnp.zeros((1, LANES), F32)
        dgk_n = jnp.zeros((1, LANES), F32)
        dgk_r = jnp.zeros((1, LANES), F32)

        def head_norm_bwd(xn, xr, r, dyn, dyr, gn, gr):
            hn, hr = xn * r, xr * r
            dxn, dxr = dyn * gn, dyr * gr
            mt = (jnp.sum(dxn * hn, -1, keepdims=True) + jnp.sum(dxr * hr, -1, keepdims=True)) / MLA_QK
            return (r * (dxn - hn * mt), r * (dxr - hr * mt),
                    jnp.sum(dyn * hn, axis=0, keepdims=True), jnp.sum(dyr * hr, axis=0, keepdims=True))

        for h in range(MLA_HEADS):
            c0 = h * MLA_QK_PAD
            qn = q_ref[:, c0:c0 + LANES].astype(F32)
            qr = q_ref[:, c0 + LANES:c0 + 2 * LANES].astype(F32)
            r = lax.rsqrt((jnp.sum(qn * qn, -1, keepdims=True) + jnp.sum(qr * qr, -1, keepdims=True)) / MLA_QK
                          + NORM_EPS)
            dyn = dq_ref[h, :, :LANES] * scale
            dyr = _rope_t(dq_ref[h, :, LANES:] * scale, *tab)
            dxn, dxr, gn_, gr_ = head_norm_bwd(qn, qr, r, dyn, dyr, gqn, gqr)
            dgq_n += gn_
            dgq_r += gr_
            dqr_ref[:, c0:c0 + LANES] = dxn.astype(BF16)
            dqr_ref[:, c0 + LANES:c0 + 2 * LANES] = dxr.astype(BF16)

            kn = kv_ref[:, c0:c0 + LANES].astype(F32)
            r = lax.rsqrt((jnp.sum(kn * kn, -1, keepdims=True) + kr_ss) / MLA_QK + NORM_EPS)
            dyn = dk_ref[h, :, :LANES]
            dyr = _rope_t(dk_ref[h, :, LANES:], *tab)
            dxn, dxr, gn_, gr_ = head_norm_bwd(kn, kr, r, dyn, dyr, gkn, gkr)
            dgk_n += gn_
            dgk_r += gr_
            dkr_sum += dxr
            dkvr_ref[:, c0:c0 + LANES] = dxn.astype(BF16)
            dkvr_ref[:, c0 + LANES:c0 + 2 * LANES] = dv_ref[h].astype(BF16)

        dkr_ref[...] = dkr_sum.astype(BF16)
        dgq_ref[:, :LANES] += dgq_n
        dgq_ref[:, LANES:] += dgq_r
        dgk_ref[:, :LANES] += dgk_n
        dgk_ref[:, LANES:] += dgk_r

    hs = lambda w: pl.BlockSpec((MLA_HEADS, tr, w), lambda i: (0, i, 0))
    return pl.pallas_call(
        body, name="mla_prep_bwd", grid=(S // tr,),
        in_specs=[hs(MLA_QK_PAD), hs(MLA_QK_PAD), hs(MLA_V), _rows(tr, W), _rows(tr, W),
                  _rows(tr, LANES, Z_KR // LANES), _rows(tr, LANES), _rows(tr, LANES), _rows(tr, LANES),
                  _full((1, MLA_QK_PAD)), _full((1, MLA_QK_PAD))],
        out_specs=[_rows(tr, W), _rows(tr, W), _rows(tr, LANES), _full((1, MLA_QK_PAD)), _full((1, MLA_QK_PAD))],
        out_shape=[jax.ShapeDtypeStruct((S, W), BF16), jax.ShapeDtypeStruct((S, W), BF16),
                   jax.ShapeDtypeStruct((S, LANES), BF16),
                   jax.ShapeDtypeStruct((1, MLA_QK_PAD), F32), jax.ShapeDtypeStruct((1, MLA_QK_PAD), F32)],
        compiler_params=_params("arbitrary"),
    )(dq, dk, dv, q_raw, kv_raw, z, *tabs, gq, gk)


_NT = (((1,), (1,)), ((), ()))
_TN = (((0,), (0,)), ((), ()))


def _flash_fwd(q, k, v, shards, kinds, tq=1024, tk=1024, ts=512):
    H, S, dq = q.shape
    dv = v.shape[-1]
    tq, tk, ts = _tile(S, tq), _tile(S, tk), _tile(S, ts)
    assert tq % tk == 0 and tq % ts == 0 and dv == LANES
    nq, nd = S // tq, tq // tk
    nw = len(shards)

    def body(q_ref, k_ref, v_ref, *rest):
        x_refs, (o_ref, lse_ref), g_refs = rest[:nw], rest[nw:nw + 2], rest[nw + 2:2 * nw + 2]
        m_sc, l_sc, acc_sc, send_sems, recv_sems, local_sems = rest[2 * nw + 2:]
        h, i = pl.program_id(0), pl.program_id(1)
        gather = _Gather(x_refs, g_refs, kinds, send_sems, recv_sems, local_sems)
        pl.when(jnp.logical_and(h == 0, i == 0))(gather.start)
        pl.when(jnp.logical_and(h == (3 * H) // 4, i == 0))(gather.forward)

        m_sc[...] = jnp.full_like(m_sc, -jnp.inf)
        l_sc[...] = jnp.zeros_like(l_sc)
        acc_sc[...] = jnp.zeros_like(acc_sc)

        def step(off, width, r0, masked):
            rows = slice(r0, tq)
            s = lax.dot_general(q_ref[rows, :], k_ref[pl.ds(off, width), :], _NT, preferred_element_type=F32)
            if masked:
                row = i * tq + r0 + lax.broadcasted_iota(jnp.int32, s.shape, 0)
                col = off + lax.broadcasted_iota(jnp.int32, s.shape, 1)
                s = jnp.where(col <= row, s, -jnp.inf)
            parts = [s[:, c * LANES:(c + 1) * LANES] for c in range(width // LANES)]
            m_cur = parts[0]
            for pt in parts[1:]:
                m_cur = jnp.maximum(m_cur, pt)
            m_prev = m_sc[rows, :]
            m_new = jnp.maximum(m_prev, jnp.max(m_cur, axis=-1, keepdims=True))
            alpha = jnp.exp(m_prev - m_new)
            ps = [jnp.exp(pt - m_new) for pt in parts]
            l_new = alpha * l_sc[rows, :]
            for pc in ps:
                l_new = l_new + pc
            l_sc[rows, :] = l_new
            p = jnp.concatenate(ps, axis=1).astype(BF16)
            acc_sc[rows, :] = alpha * acc_sc[rows, :] + jnp.dot(p, v_ref[pl.ds(off, width), :],
                                                                preferred_element_type=F32)
            m_sc[rows, :] = m_new

        def full_step(j, carry):
            step(pl.multiple_of(j * tk, tk), tk, 0, False)
            return carry

        lax.fori_loop(0, i * nd, full_step, 0)
        for d in range(tq // ts):
            step(pl.multiple_of(i * tq + d * ts, ts), ts, d * ts, True)
        l = jnp.sum(l_sc[...], axis=-1, keepdims=True)
        o_ref[...] = (acc_sc[...] / l).astype(BF16)
        lse_ref[...] = m_sc[...] + jnp.log(l)
        pl.when(jnp.logical_and(h == H - 1, i == nq - 1))(gather.finish)

    return pl.pallas_call(
        body, name="flash_fwd", grid=(H, nq),
        in_specs=[pl.BlockSpec((None, tq, dq), lambda h, i: (h, i, 0)),
                  pl.BlockSpec((None, S, dq), lambda h, i: (h, 0, 0)),
                  pl.BlockSpec((None, S, dv), lambda h, i: (h, 0, 0))] + [_ANY] * nw,
        out_specs=[pl.BlockSpec((tq, dv), lambda h, i: (i, h)),
                   pl.BlockSpec((None, tq, LANES), lambda h, i: (h, i, 0))] + [_ANY] * nw,
        out_shape=[jax.ShapeDtypeStruct((S, H * dv), BF16), jax.ShapeDtypeStruct((H, S, LANES), F32)]
        + [jax.ShapeDtypeStruct(_gather_out_shape(kd, sd.shape), sd.dtype) for kd, sd in zip(kinds, shards)],
        scratch_shapes=[pltpu.VMEM((tq, LANES), F32), pltpu.VMEM((tq, LANES), F32), pltpu.VMEM((tq, dv), F32),
                        pltpu.SemaphoreType.DMA((nw, 7)), pltpu.SemaphoreType.DMA((nw, 7)),
                        pltpu.SemaphoreType.DMA((nw,))],
        compiler_params=_params("arbitrary", "arbitrary"),
    )(q, k, v, *shards)


def _flash_bwd(q, k, v, dyc, cb0, o, lse, sends, tq=1024, tk=1024, ts=512):
    H, S, dq = q.shape
    dv = v.shape[-1]
    tq, tk, ts = _tile(S, tq), _tile(S, tk), _tile(S, ts)
    assert tq % tk == 0 and tq % ts == 0
    nq, nd = S // tq, tq // tk
    nw = len(sends)

    def body(q_ref, k_ref, v_ref, dy_ref, o_ref, lse_ref, *rest):
        s_refs, (dq_ref, dk_ref, dv_ref), r_refs = rest[:nw], rest[nw:nw + 3], rest[nw + 3:2 * nw + 3]
        do_sc, dl_sc, send_sems, recv_sems = rest[2 * nw + 3:]
        h, i = pl.program_id(0), pl.program_id(1)
        scatter = _Scatter(s_refs, r_refs, send_sems, recv_sems)
        pl.when(jnp.logical_and(h == 0, i == 0))(scatter.start)

        @pl.when(i == 0)
        def _():
            dk_ref[...] = jnp.zeros_like(dk_ref)
            dv_ref[...] = jnp.zeros_like(dv_ref)

        dq_ref[...] = jnp.zeros_like(dq_ref)
        dy = dy_ref[...]
        do_sc[...] = dy.astype(BF16)
        dl_sc[...] = jnp.broadcast_to(jnp.sum(dy * o_ref[...].astype(F32), axis=-1, keepdims=True), (tq, LANES))

        def step(off, width, r0, masked):
            rows = slice(r0, tq)
            qv = q_ref[rows, :]
            dov = do_sc[rows, :]
            kb = k_ref[pl.ds(off, width), :]
            s = lax.dot_general(qv, kb, _NT, preferred_element_type=F32)
            if masked:
                row = i * tq + r0 + lax.broadcasted_iota(jnp.int32, s.shape, 0)
                col = off + lax.broadcasted_iota(jnp.int32, s.shape, 1)
                s = jnp.where(col <= row, s, -jnp.inf)
            p = jnp.exp(s - lse_ref[rows, :1])
            dv_ref[pl.ds(off, width), :] += lax.dot_general(p.astype(BF16), dov, _TN, preferred_element_type=F32)
            dp = lax.dot_general(dov, v_ref[pl.ds(off, width), :], _NT, preferred_element_type=F32)
            ds = (p * (dp - dl_sc[rows, :1])).astype(BF16)
            dk_ref[pl.ds(off, width), :] += lax.dot_general(ds, qv, _TN, preferred_element_type=F32)
            dq_ref[rows, :] += jnp.dot(ds, kb, preferred_element_type=F32)

        def full_step(j, carry):
            step(pl.multiple_of(j * tk, tk), tk, 0, False)
            return carry

        lax.fori_loop(0, i * nd, full_step, 0)
        for d in range(tq // ts):
            step(pl.multiple_of(i * tq + d * ts, ts), ts, d * ts, True)
        pl.when(jnp.logical_and(h == H - 1, i == nq - 1))(scatter.finish)

    return pl.pallas_call(
        body, name="flash_bwd", grid=(H, nq),
        in_specs=[pl.BlockSpec((None, tq, dq), lambda h, i: (h, i, 0)),
                  pl.BlockSpec((None, S, dq), lambda h, i: (h, 0, 0), pipeline_mode=pl.Buffered(1)),
                  pl.BlockSpec((None, S, dv), lambda h, i: (h, 0, 0), pipeline_mode=pl.Buffered(1)),
                  pl.BlockSpec((tq, dv), lambda h, i: (i, h + cb0)),
                  pl.BlockSpec((tq, dv), lambda h, i: (i, h)),
                  pl.BlockSpec((None, tq, LANES), lambda h, i: (h, i, 0))] + [_ANY] * nw,
        out_specs=[pl.BlockSpec((None, tq, dq), lambda h, i: (h, i, 0)),
                   pl.BlockSpec((None, S, dq), lambda h, i: (h, 0, 0)),
                   pl.BlockSpec((None, S, dv), lambda h, i: (h, 0, 0))] + [_ANY] * nw,
        out_shape=[jax.ShapeDtypeStruct((H, S, dq), F32), jax.ShapeDtypeStruct((H, S, dq), F32),
                   jax.ShapeDtypeStruct((H, S, dv), F32)]
        + [jax.ShapeDtypeStruct((N_DEV - 1,) + sd.shape[1:], sd.dtype) for sd in sends],
        scratch_shapes=[pltpu.VMEM((tq, dv), BF16), pltpu.VMEM((tq, LANES), F32),
                        pltpu.SemaphoreType.DMA((nw, N_DEV - 1)), pltpu.SemaphoreType.DMA((nw, N_DEV - 1))],
        compiler_params=_params("arbitrary", "arbitrary"),
    )(q, k, v, dyc, o, lse, *sends)


def _mem_norm(mem, g):
    n = mem.shape[1]

    def body(m_ref, g_ref, o_ref):
        xhat, _ = _rms_hat(m_ref[...], n)
        o_ref[...] = (xhat * g_ref[...]).astype(BF16)

    return pl.pallas_call(body, name="mem_norm", out_shape=jax.ShapeDtypeStruct(mem.shape, BF16),
                          compiler_params=pltpu.CompilerParams(vmem_limit_bytes=VMEM_LIMIT))(mem, g)


def _memkv_prep(mkv, gk):
    M = mkv.shape[0]

    def body(mkv_ref, g_ref, k_ref, v_ref):
        for h in range(X_HEADS):
            cols = slice(h * X_HEAD_DIM, (h + 1) * X_HEAD_DIM)
            xhat, _ = _rms_hat(mkv_ref[:, cols], X_HEAD_DIM)
            k_ref[:, cols] = (xhat * g_ref[...]).astype(BF16)
        v_ref[...] = mkv_ref[:, X_WIDTH:].astype(BF16)

    return pl.pallas_call(
        body, name="memkv_prep",
        out_shape=[jax.ShapeDtypeStruct((M, X_WIDTH), BF16), jax.ShapeDtypeStruct((M, X_WIDTH), BF16)],
    )(mkv, gk)


def _memkv_bwd(dk, dv, mkv, gk):
    M = mkv.shape[0]

    def body(dk_ref, dv_ref, mkv_ref, g_ref, dmkv_ref, dg_ref):
        dg = jnp.zeros((1, X_HEAD_DIM), F32)
        for h in range(X_HEADS):
            cols = slice(h * X_HEAD_DIM, (h + 1) * X_HEAD_DIM)
            xhat, r = _rms_hat(mkv_ref[:, cols], X_HEAD_DIM)
            dx, dgh = _rms_bwd(dk_ref[:, cols], xhat, r, g_ref[...], X_HEAD_DIM)
            dmkv_ref[:, cols] = dx.astype(BF16)
            dg += dgh
        dmkv_ref[:, X_WIDTH:] = dv_ref[...].astype(BF16)
        dg_ref[...] = dg

    return pl.pallas_call(
        body, name="memkv_bwd",
        out_shape=[jax.ShapeDtypeStruct((M, 2 * X_WIDTH), BF16), jax.ShapeDtypeStruct((1, X_HEAD_DIM), F32)],
    )(dk, dv, mkv, gk)


def _mem_gain_bwd(mem, dmn):
    n = mem.shape[1]

    def body(m_ref, d_ref, dg_ref):
        xhat, _ = _rms_hat(m_ref[...], n)
        dg_ref[...] = jnp.sum(d_ref[...] * xhat, axis=0, keepdims=True)

    return pl.pallas_call(body, name="mem_gain_bwd", out_shape=jax.ShapeDtypeStruct((1, n), F32),
                          compiler_params=pltpu.CompilerParams(vmem_limit_bytes=VMEM_LIMIT))(mem, dmn)


def _memx_scores(zq_ref, g, kx_ref, h):
    cols = slice(h * X_HEAD_DIM, (h + 1) * X_HEAD_DIM)
    xhat, r = _rms_hat(zq_ref[:, cols], X_HEAD_DIM)
    qn = (xhat * g).astype(BF16)
    s = lax.dot_general(qn, kx_ref[:, cols], _NT, preferred_element_type=F32) * (1.0 / math.sqrt(X_HEAD_DIM))
    p = jnp.exp(s - jnp.max(s, axis=-1, keepdims=True))
    return cols, xhat, r, qn, p, jnp.sum(p, axis=-1, keepdims=True)


def _memx_fwd(z, g, kx, vx, tr=512):
    S = z.shape[0]
    tr = _tile(S, tr)
    M = kx.shape[0]

    def body(zq_ref, g_ref, kx_ref, vx_ref, y_ref):
        for h in range(X_HEADS):
            cols, _, _, _, p, l = _memx_scores(zq_ref, g_ref[...], kx_ref, h)
            o = jnp.dot(p.astype(BF16), vx_ref[:, cols], preferred_element_type=F32)
            y_ref[:, cols] = (o / l).astype(BF16)

    return pl.pallas_call(
        body, name="memx_fwd", grid=(S // tr,),
        in_specs=[_rows(tr, X_WIDTH, Z_MQ // X_WIDTH), _full((1, X_HEAD_DIM)), _full((M, X_WIDTH)),
                  _full((M, X_WIDTH))],
        out_specs=_rows(tr, X_WIDTH),
        out_shape=jax.ShapeDtypeStruct((S, X_WIDTH), BF16),
        compiler_params=_params("parallel"),
    )(z, g, kx, vx)


def _memx_bwd(z, g, kx, vx, dyc, dy_cb, tr=512):
    S = z.shape[0]
    tr = _tile(S, tr)
    M = kx.shape[0]
    scale = 1.0 / math.sqrt(X_HEAD_DIM)

    def body(zq_ref, g_ref, kx_ref, vx_ref, dy_ref, dz_ref, dk_ref, dv_ref, dg_ref):
        @pl.when(pl.program_id(0) == 0)
        def _():
            dk_ref[...] = jnp.zeros_like(dk_ref)
            dv_ref[...] = jnp.zeros_like(dv_ref)
            dg_ref[...] = jnp.zeros_like(dg_ref)

        gv = g_ref[...]
        for h in range(X_HEADS):
            cols, xhat, r, qn, p, l = _memx_scores(zq_ref, gv, kx_ref, h)
            p = p / l
            do = dy_ref[:, cols].astype(BF16)
            dv_ref[:, cols] += lax.dot_general(p.astype(BF16), do, _TN, preferred_element_type=F32)
            dp = lax.dot_general(do, vx_ref[:, cols], _NT, preferred_element_type=F32)
            ds = (p * (dp - jnp.sum(dp * p, axis=-1, keepdims=True)) * scale).astype(BF16)
            dqn = jnp.dot(ds, kx_ref[:, cols], preferred_element_type=F32)
            dk_ref[:, cols] += lax.dot_general(ds, qn, _TN, preferred_element_type=F32)
            dx, dg = _rms_bwd(dqn, xhat, r, gv, X_HEAD_DIM)
            dz_ref[:, cols] = dx.astype(BF16)
            dg_ref[...] += dg

    return pl.pallas_call(
        body, name="memx_bwd", grid=(S // tr,),
        in_specs=[_rows(tr, X_WIDTH, Z_MQ // X_WIDTH), _full((1, X_HEAD_DIM)), _full((M, X_WIDTH)),
                  _full((M, X_WIDTH)), _rows(tr, X_WIDTH, dy_cb)],
        out_specs=[_rows(tr, X_WIDTH), _full((M, X_WIDTH)), _full((M, X_WIDTH)), _full((1, X_HEAD_DIM))],
        out_shape=[jax.ShapeDtypeStruct((S, X_WIDTH), BF16), jax.ShapeDtypeStruct((M, X_WIDTH), F32),
                   jax.ShapeDtypeStruct((M, X_WIDTH), F32), jax.ShapeDtypeStruct((1, X_HEAD_DIM), F32)],
        compiler_params=_params("arbitrary"),
    )(z, g, kx, vx, dyc)


def _conv_gc(g_ext, w_ref, b_ref, n_ext):
    g1 = pltpu.roll(g_ext, 1, 0)
    g2 = pltpu.roll(g_ext, 2, 0)
    gc = b_ref[...] + w_ref[0:1, :] * g2
    gc = gc + w_ref[1:2, :] * g1
    gc = gc + w_ref[2:3, :] * g_ext
    return gc, g1, g2


def _convglu_fwd(g, u, cw, cb, tr=512, tc=512):
    S, F = g.shape
    tr, tc = _tile(S, tr), _tile(F, tc)
    rb = tr // CONV_HALO

    def body(g_ref, gp_ref, u_ref, w_ref, b_ref, a_ref):
        i = pl.program_id(1)
        halo = jnp.where(i == 0, 0.0, gp_ref[...].astype(F32))
        g_ext = jnp.concatenate([halo, g_ref[...].astype(F32)], axis=0)
        gc, _, _ = _conv_gc(g_ext, w_ref, b_ref, tr + CONV_HALO)
        gc = gc[CONV_HALO:]
        sig = 1.0 / (1.0 + jnp.exp(-gc))
        a_ref[...] = (gc * sig * u_ref[...].astype(F32)).astype(BF16)

    return pl.pallas_call(
        body, name="convglu_fwd", grid=(F // tc, S // tr),
        in_specs=[pl.BlockSpec((tr, tc), lambda j, i: (i, j)),
                  pl.BlockSpec((CONV_HALO, tc), lambda j, i: (jnp.maximum(i * rb - 1, 0), j)),
                  pl.BlockSpec((tr, tc), lambda j, i: (i, j)),
                  pl.BlockSpec((3, tc), lambda j, i: (0, j)), pl.BlockSpec((1, tc), lambda j, i: (0, j))],
        out_specs=pl.BlockSpec((tr, tc), lambda j, i: (i, j)),
        out_shape=jax.ShapeDtypeStruct((S, F), BF16),
        compiler_params=_params("parallel", "parallel"),
    )(g, g, u, cw, cb)


def _convglu_bwd(g, u, da, cw, cb, tr=512, tc=512):
    S, F = g.shape
    tr, tc = _tile(S, tr), _tile(F, tc)
    rb = tr // CONV_HALO
    nhb = S // CONV_HALO
    H = CONV_HALO

    def body(g_ref, gp_ref, gn_ref, u_ref, un_ref, da_ref, dan_ref, w_ref, b_ref,
             dg_ref, du_ref, dw_ref, db_ref):
        i = pl.program_id(1)
        last = i == pl.num_programs(1) - 1

        @pl.when(i == 0)
        def _():
            dw_ref[...] = jnp.zeros_like(dw_ref)
            db_ref[...] = jnp.zeros_like(db_ref)

        g_prev = jnp.where(i == 0, 0.0, gp_ref[...].astype(F32))
        g_cur = g_ref[...].astype(F32)
        g_ext = jnp.concatenate([g_prev, g_cur, gn_ref[...].astype(F32)], axis=0)
        gc, g1, g2 = _conv_gc(g_ext, w_ref, b_ref, tr + 2 * H)
        gc = gc[H:]
        ux = jnp.concatenate([u_ref[...], un_ref[...]], axis=0).astype(F32)
        dax = jnp.concatenate([da_ref[...].astype(F32), jnp.where(last, 0.0, dan_ref[...].astype(F32))], axis=0)
        sig = 1.0 / (1.0 + jnp.exp(-gc))
        du_ref[...] = (dax[:tr] * (gc[:tr] * sig[:tr])).astype(BF16)
        dgc = dax * ux * (sig * (1.0 + gc * (1.0 - sig)))
        n = tr + H
        d1 = pltpu.roll(dgc, n - 1, 0)[:tr]
        d2 = pltpu.roll(dgc, n - 2, 0)[:tr]
        d0 = dgc[:tr]
        dg_ref[...] = (w_ref[2:3, :] * d0 + w_ref[1:2, :] * d1 + w_ref[0:1, :] * d2).astype(BF16)
        db_ref[...] += jnp.sum(d0, axis=0, keepdims=True)
        dw_ref[0:1, :] += jnp.sum(d0 * g2[H:H + tr], axis=0, keepdims=True)
        dw_ref[1:2, :] += jnp.sum(d0 * g1[H:H + tr], axis=0, keepdims=True)
        dw_ref[2:3, :] += jnp.sum(d0 * g_cur, axis=0, keepdims=True)

    cur = pl.BlockSpec((tr, tc), lambda j, i: (i, j))
    prv = pl.BlockSpec((H, tc), lambda j, i: (jnp.maximum(i * rb - 1, 0), j))
    nxt = pl.BlockSpec((H, tc), lambda j, i: (jnp.minimum((i + 1) * rb, nhb - 1), j))
    return pl.pallas_call(
        body, name="convglu_bwd", grid=(F // tc, S // tr),
        in_specs=[cur, prv, nxt, cur, nxt, cur, nxt,
                  pl.BlockSpec((3, tc), lambda j, i: (0, j)), pl.BlockSpec((1, tc), lambda j, i: (0, j))],
        out_specs=[cur, cur, pl.BlockSpec((3, tc), lambda j, i: (0, j)), pl.BlockSpec((1, tc), lambda j, i: (0, j))],
        out_shape=[jax.ShapeDtypeStruct((S, F), BF16), jax.ShapeDtypeStruct((S, F), BF16),
                   jax.ShapeDtypeStruct((3, F), F32), jax.ShapeDtypeStruct((1, F), F32)],
        compiler_params=_params("parallel", "arbitrary"),
    )(g, g, g, u, u, da, da, cw, cb)


def _mm_down_loss(a, w, x1, target, tm=1024, tn=512):
    S, K = a.shape
    n = w.shape[1]
    tm, tn = _tile(S, tm), _tile(n, tn)

    def body(a_ref, w_ref, x_ref, t_ref, dy_ref, dyb_ref, loss_ref):
        @pl.when(jnp.logical_and(pl.program_id(0) == 0, pl.program_id(1) == 0))
        def _():
            loss_ref[...] = jnp.zeros_like(loss_ref)

        y = jnp.dot(a_ref[...], w_ref[...], preferred_element_type=F32) + x_ref[...]
        err = y - t_ref[...]
        dy = err / n
        dy_ref[...] = dy
        dyb_ref[...] = dy.astype(BF16)
        loss_ref[...] += 0.5 * jnp.sum(jnp.sum(err * err, axis=-1, keepdims=True) / n)

    tile = pl.BlockSpec((tm, tn), lambda i, j: (i, j))
    return pl.pallas_call(
        body, name="mm_down_loss", grid=(S // tm, n // tn),
        in_specs=[pl.BlockSpec((tm, K), lambda i, j: (i, 0)), pl.BlockSpec((K, tn), lambda i, j: (0, j)), tile, tile],
        out_specs=[tile, tile, pl.BlockSpec((8, LANES), lambda i, j: (0, 0))],
        out_shape=[jax.ShapeDtypeStruct((S, n), F32), jax.ShapeDtypeStruct((S, n), BF16),
                   jax.ShapeDtypeStruct((8, LANES), F32)],
        compiler_params=_params("arbitrary", "arbitrary"),
    )(a, w, x1, target)


def _mm_o_rms(ycat, w, x, g, tm=512):
    S, K = ycat.shape
    n = w.shape[1]
    tm = _tile(S, tm)

    def body(y_ref, w_ref, x_ref, g_ref, x1_ref, h2_ref):
        x1 = jnp.dot(y_ref[...], w_ref[...], preferred_element_type=F32) + x_ref[...]
        x1_ref[...] = x1
        xhat, _ = _rms_hat(x1, n)
        h2_ref[...] = (xhat * g_ref[...]).astype(BF16)

    return pl.pallas_call(
        body, name="mm_o_rms", grid=(S // tm,),
        in_specs=[_rows(tm, K), _full((K, n)), _rows(tm, n), _full((1, n))],
        out_specs=[_rows(tm, n), _rows(tm, n)],
        out_shape=[jax.ShapeDtypeStruct((S, n), F32), jax.ShapeDtypeStruct((S, n), BF16)],
        compiler_params=_params("parallel"),
    )(ycat, w, x, g)


IN_DX_TILE = 256


def _mm_in_dx_rms(name, dz, w, x, g, extra, tiles, sends=(), carry=None):
    S, K = dz.shape
    n = w.shape[0]
    tm = _tile(S, IN_DX_TILE)
    t0, n_steps = tiles[0], tiles[1] - tiles[0]
    nw = len(sends)
    nc = 2 if carry is not None else 0

    def body(dz_ref, w_ref, x_ref, g_ref, ex_ref, *rest):
        dg_in = rest[1] if nc else None
        rest = rest[nc:]
        s_refs, (dx_ref, dg_ref), r_refs = rest[:nw], rest[nw:nw + 2], rest[nw + 2:2 * nw + 2]
        i = pl.program_id(0)
        if nw:
            scatter = _Scatter(s_refs, r_refs, *rest[2 * nw + 2:])
            pl.when(i == 0)(scatter.start)

        @pl.when(i == 0)
        def _():
            dg_ref[...] = dg_in[...] if nc else jnp.zeros_like(dg_ref)

        dh = lax.dot_general(dz_ref[...], w_ref[...], _NT, preferred_element_type=F32)
        xhat, r = _rms_hat(x_ref[...], n)
        dx, dg = _rms_bwd(dh, xhat, r, g_ref[...], n)
        dx_ref[...] = dx + ex_ref[...]
        dg_ref[...] += dg
        if nw:
            pl.when(i == n_steps - 1)(scatter.finish)

    rows = lambda width: pl.BlockSpec((tm, width), lambda i: (i + t0, 0))
    scratch = [pltpu.SemaphoreType.DMA((nw, N_DEV - 1)), pltpu.SemaphoreType.DMA((nw, N_DEV - 1))] if nw else []
    return pl.pallas_call(
        body, name=name, grid=(n_steps,),
        in_specs=[rows(K), _full((n, K)), rows(n), _full((1, n)), rows(n)]
        + ([_ANY, _full((1, n))] if nc else []) + [_ANY] * nw,
        out_specs=[rows(n), _full((1, n))] + [_ANY] * nw,
        out_shape=[jax.ShapeDtypeStruct((S, n), F32), jax.ShapeDtypeStruct((1, n), F32)]
        + [jax.ShapeDtypeStruct((N_DEV - 1,) + sd.shape[1:], sd.dtype) for sd in sends],
        scratch_shapes=scratch, input_output_aliases={5: 0} if nc else {},
        compiler_params=_params("arbitrary"),
    )(dz, w, x, g, extra, *(carry or ()), *sends)


def _local_step(x, mem, target, W, early_shards, late_shards):
    S = x.shape[0]
    tabs = _rope_tables(S)
    W = dict(W)
    G, Gb = {}, {}

    h, W["w_in"], W["w_q_up"], W["w_kv_up"] = _rms_fwd("rms1_fwd", x, W["g_mix"], early_shards, EARLY_GATHER_KINDS)
    z = _mm("mm_in", h, W["w_in"], tm=512, tn=Z_COLS, tk=D_MODEL)
    y_pool = _pool_fwd(z, W["w_pool"], W["pool_scale"])
    ql, kvl = _lat_fwd(z, W["g_q_lat"], W["g_kv_lat"])
    q_raw = _mm("mm_q_up", ql, W["w_q_up"], out_dtype=BF16, tm=1024, tn=2048, tk=Q_LORA)
    kv_raw = _mm("mm_kv_up", kvl, W["w_kv_up"], out_dtype=BF16, tm=1024, tn=2048, tk=KV_LORA)
    n_ffn = len(FFN_GATHER_KINDS)
    q, k, v, W["w_o"], W["w_mem_kv"] = _mla_prep(q_raw, kv_raw, z, tabs, W["g_q_mla"], W["g_k_mla"],
                                                 late_shards[n_ffn:], MIXER_GATHER_KINDS)
    o_mla, lse, wg3, wu3, W["w_down"], cw3 = _flash_fwd(q, k, v, late_shards[:n_ffn], FFN_GATHER_KINDS)
    W["w_gate"] = jnp.transpose(wg3, (1, 0, 2)).reshape(D_MODEL, D_FF)
    W["w_up"] = jnp.transpose(wu3, (1, 0, 2)).reshape(D_MODEL, D_FF)
    cw = jnp.sum(cw3.reshape(N_DEV, 3, 3, D_FF // N_DEV).astype(F32), axis=1)
    W["conv_w"] = jnp.transpose(cw, (1, 0, 2)).reshape(3, D_FF)
    mn = _mem_norm(mem, W["g_mem"])
    mkv = _mm("mm_mem_kv", mn, W["w_mem_kv"], tm=256, tn=1024, tk=D_MODEL)
    kx, vx = _memkv_prep(mkv, W["g_k_x"])
    y_mem = _memx_fwd(z, W["g_q_x"], kx, vx)
    ycat = jnp.concatenate([y_pool, o_mla, y_mem], axis=1)
    x1, h2 = _mm_o_rms(ycat, W["w_o"], x, W["g_ffn"])
    g = _mm("mm_gate", h2, W["w_gate"], out_dtype=BF16, tm=1024, tn=1408, tk=D_MODEL)
    u = _mm("mm_up", h2, W["w_up"], out_dtype=BF16, tm=1024, tn=1408, tk=D_MODEL)
    a = _convglu_fwd(g, u, W["conv_w"], W["conv_b"])
    dy, dyb, loss_part = _mm_down_loss(a, W["w_down"], x1, target)

    da = _mm("mm_down_dx", dyb, W["w_down"], tb=True, out_dtype=BF16, tm=1024, tn=1408, tk=D_MODEL)
    G["w_down"], Gb["w_down"] = _mm("mm_down_dw", a, dyb, ta=True, also_bf16=True, tm=512, tn=2048, tk=2048)
    dg, du, G["conv_w"], G["conv_b"] = _convglu_bwd(g, u, da, W["conv_w"], W["conv_b"])
    dh2 = _mm("mm_gate_dx", dg, W["w_gate"], tb=True, tm=1024, tn=512, tk=D_FF)
    G["w_gate"] = _mm("mm_gate_dw", h2, dg, ta=True, tm=1024, tn=1408, tk=2048)
    G["w_up"] = _mm("mm_up_dw", h2, du, ta=True, tm=1024, tn=1408, tk=2048)
    dx1, dx1b, G["g_ffn"] = _mm_up_dx_rms(du, W["w_up"], dh2, x1, W["g_ffn"], dy)

    dyc = _mm("mm_o_dx", dx1b, W["w_o"], tb=True, tm=512, tn=2048, tk=D_MODEL)
    G["w_o"], Gb["w_o"] = _mm("mm_o_dw", ycat, dx1b, ta=True, also_bf16=True, tm=1024, tn=1024, tk=2048)
    dz_pool, G["w_pool"], G["pool_scale"] = _pool_bwd(z, dyc, W["w_pool"], W["pool_scale"], dy_cb=0)
    dz_mq, dkx, dvx, G["g_q_x"] = _memx_bwd(z, W["g_q_x"], kx, vx, dyc, dy_cb=3)
    dmkv, G["g_k_x"] = _memkv_bwd(dkx, dvx, mkv, W["g_k_x"])
    G["w_mem_kv"], Gb["w_mem_kv"] = _mm("mm_mem_kv_dw", mn, dmkv, ta=True, also_bf16=True, tm=1024, tn=1024,
                                        tk=MEM_LEN)
    dmn = _mm("mm_mem_kv_dx", dmkv, W["w_mem_kv"], tb=True, tm=256, tn=2048, tk=1024)
    G["g_mem"] = _mem_gain_bwd(mem, dmn)
    sends = [_send_blocks(n, Gb.get(n, G[n])) for n in SCATTER_EARLY]
    dq, dk, dv, *got_early = _flash_bwd(q, k, v, dyc, POOL_WIDTH // MLA_V, o_mla, lse, sends)
    dq_raw, dkv_raw, dz_kr, G["g_q_mla"], G["g_k_mla"] = _mla_prep_bwd(
        dq, dk, dv, q_raw, kv_raw, z, tabs, W["g_q_mla"], W["g_k_mla"])
    G["w_q_up"] = _mm("mm_q_up_dw", ql, dq_raw, ta=True, tm=512, tn=2048, tk=1024)
    dql = _mm("mm_q_up_dx", dq_raw, W["w_q_up"], tb=True, tm=1024, tn=512, tk=2048)
    G["w_kv_up"] = _mm("mm_kv_up_dw", kvl, dkv_raw, ta=True, tm=256, tn=2048, tk=1024)
    dkvl = _mm("mm_kv_up_dx", dkv_raw, W["w_kv_up"], tb=True, tm=1024, tn=256, tk=2048)
    dz_q, dz_kv, G["g_q_lat"], G["g_kv_lat"] = _lat_bwd(z, W["g_q_lat"], W["g_kv_lat"], dql, dkvl)
    dz = jnp.concatenate([dz_pool, dz_q, dz_mq, dz_kv, dz_kr], axis=1)
    G["w_in"] = _mm("mm_in_dw", h, dz, ta=True, tm=1024, tn=Z_COLS, tk=1024)
    n_tiles = S // _tile(S, IN_DX_TILE)
    cut = (3 * n_tiles) // 4
    part_x, part_g, *got_late = _mm_in_dx_rms("mm_in_dx_rms", dz, W["w_in"], x, W["g_mix"], dx1, (0, cut),
                                              sends=[_send_blocks(n, G[n]) for n in SCATTER_LATE])
    grad_x, G["g_mix"] = _mm_in_dx_rms("mm_in_dx_rms_rest", dz, W["w_in"], x, W["g_mix"], dx1, (cut, n_tiles),
                                       carry=(part_x, part_g))
    return loss_part, grad_x, G, dict(zip(SCATTER_EARLY + SCATTER_LATE, got_early + got_late))


_ANY = pl.BlockSpec(memory_space=pl.ANY)
_MESH = pl.DeviceIdType.MESH


def _gather_out_shape(kind, shape):
    if kind == "rows":
        return (N_DEV * shape[0],) + tuple(shape[1:])
    if kind == "cols":
        return (shape[0], N_DEV * shape[1])
    return (N_DEV,) + tuple(shape)


def _gather_view(ref, kind, shape, d):
    if kind == "rows":
        return ref.at[pl.ds(pl.multiple_of(d * shape[0], 16), shape[0]), :]
    if kind == "cols":
        return ref.at[:, pl.ds(pl.multiple_of(d * shape[1], math.gcd(shape[1], LANES)), shape[1])]
    return ref.at[d]


class _Gather:
    def __init__(self, x_refs, out_refs, kinds, send_sems, recv_sems, local_sems):
        self.xr, self.outr, self.kinds = x_refs, out_refs, kinds
        self.ss, self.rs, self.ls = send_sems, recv_sems, local_sems
        x, y, c = lax.axis_index("x"), lax.axis_index("y"), lax.axis_index("c")
        self.c = c
        self.me, self.sibling = (x, y, c), (x, y, 1 - c)
        self.chips = [(1 - x, y), (x, 1 - y), (1 - x, 1 - y)]

    def _view(self, w, dev):
        px, py, pc = dev
        return _gather_view(self.outr[w], self.kinds[w], self.xr[w].shape, 4 * px + 2 * py + pc)

    def _copy(self, w, k, block, to, from_shard=False):
        v = self._view(w, block)
        return pltpu.make_async_remote_copy(
            src_ref=self.xr[w] if from_shard else v, dst_ref=v, send_sem=self.ss.at[w, k],
            recv_sem=self.rs.at[w, k], device_id=to, device_id_type=_MESH)

    def _local(self, w):
        return pltpu.make_async_copy(self.xr[w], self._view(w, self.me), self.ls.at[w])

    def start(self):
        for w in range(len(self.xr)):
            self._local(w).start()
            self._copy(w, 0, self.me, self.sibling, True).start()
            for j, chip in enumerate(self.chips):
                self._copy(w, 1 + j, self.me, (*chip, self.c), True).start()

    def forward(self):
        for j, chip in enumerate(self.chips):
            for w in range(len(self.xr)):
                self._copy(w, 1 + j, (*chip, self.c), self.me).wait_recv()
                self._copy(w, 4 + j, (*chip, self.c), self.sibling).start()

    def finish(self):
        for w in range(len(self.xr)):
            self._copy(w, 0, self.sibling, self.me).wait_recv()
            for j, chip in enumerate(self.chips):
                self._copy(w, 4 + j, (*chip, 1 - self.c), self.me).wait_recv()
            self._copy(w, 0, self.me, self.sibling, True).wait_send()
            for j, chip in enumerate(self.chips):
                self._copy(w, 1 + j, self.me, (*chip, self.c), True).wait_send()
                self._copy(w, 4 + j, (*chip, self.c), self.sibling).wait_send()
            self._local(w).wait()


class _Scatter:
    def __init__(self, send_refs, recv_refs, send_sems, recv_sems):
        self.sr, self.rr, self.ss, self.rs = send_refs, recv_refs, send_sems, recv_sems
        self.xyz = lax.axis_index("x"), lax.axis_index("y"), lax.axis_index("c")

    def _copy(self, w, k):
        x, y, c = self.xyz
        px, py, pc = x ^ ((k >> 2) & 1), y ^ ((k >> 1) & 1), c ^ (k & 1)
        return pltpu.make_async_remote_copy(
            src_ref=self.sr[w].at[4 * px + 2 * py + pc], dst_ref=self.rr[w].at[k - 1],
            send_sem=self.ss.at[w, k - 1], recv_sem=self.rs.at[w, k - 1],
            device_id=(px, py, pc), device_id_type=_MESH)

    def _all(self):
        return [self._copy(w, k) for w in range(len(self.sr)) for k in range(1, N_DEV)]

    def start(self):
        for cp in self._all():
            cp.start()

    def finish(self):
        for cp in self._all():
            cp.wait_recv()
        for cp in self._all():
            cp.wait_send()


def _exchange_small(s):
    def body(x_ref, out_ref, x_send, x_recv, local_sem):
        x, y, c = lax.axis_index("x"), lax.axis_index("y"), lax.axis_index("c")
        me = 4 * x + 2 * y + c
        mine = pltpu.make_async_copy(x_ref, out_ref.at[me], local_sem)
        mine.start()

        def copy(k):
            px, py, pc = x ^ ((k >> 2) & 1), y ^ ((k >> 1) & 1), c ^ (k & 1)
            return pltpu.make_async_remote_copy(
                src_ref=x_ref, dst_ref=out_ref.at[me], send_sem=x_send.at[k - 1], recv_sem=x_recv.at[k - 1],
                device_id=(px, py, pc), device_id_type=_MESH)

        cps = [copy(k) for k in range(1, N_DEV)]
        for cp in cps:
            cp.start()
        for cp in cps:
            cp.wait_recv()
        for cp in cps:
            cp.wait_send()
        mine.wait()

    return pl.pallas_call(
        body, name="exchange_small", out_shape=jax.ShapeDtypeStruct((N_DEV,) + s.shape, s.dtype),
        in_specs=[_ANY], out_specs=_ANY,
        scratch_shapes=[pltpu.SemaphoreType.DMA((N_DEV - 1,)), pltpu.SemaphoreType.DMA((N_DEV - 1,)),
                        pltpu.SemaphoreType.DMA],
    )(s)


def _sum_slots(buf, tr=208):
    n, R, _ = buf.shape
    tr = tr if R % tr == 0 else R

    def body(b_ref, o_ref):
        acc = b_ref[0]
        for d in range(1, n):
            acc = acc + b_ref[d]
        o_ref[...] = acc

    return pl.pallas_call(
        body, name="sum_slots", grid=(R // tr,),
        in_specs=[pl.BlockSpec((n, tr, LANES), lambda i: (0, i, 0))],
        out_specs=pl.BlockSpec((tr, LANES), lambda i: (i, 0)),
        out_shape=jax.ShapeDtypeStruct((R, LANES), buf.dtype),
        compiler_params=_params("parallel"),
    )(buf)


def _adamw_math(w, g, m, v):
    m_new = ADAM_B1 * m + (1.0 - ADAM_B1) * g
    v_new = ADAM_B2 * v + (1.0 - ADAM_B2) * (g * g)
    m_hat = m_new / (1.0 - ADAM_B1 ** ADAM_STEP)
    v_hat = v_new / (1.0 - ADAM_B2 ** ADAM_STEP)
    return -ADAM_LR * (m_hat / (jnp.sqrt(v_hat) + ADAM_EPS) + ADAM_WD * w), m_new, v_new


def _adamw(name, w, g, got, m, v, tr=128):
    R, C = w.shape
    tr = max([t for t in range(16, tr + 1, 16) if R % t == 0], default=R) if R > tr else R

    def body(w_ref, g_ref, got_ref, m_ref, v_ref, go_ref, d_ref, mo_ref, vo_ref):
        gv = g_ref[...]
        for k in range(N_DEV - 1):
            gv = gv + got_ref[k].astype(F32)
        go_ref[...] = gv
        d_ref[...], mo_ref[...], vo_ref[...] = _adamw_math(w_ref[...], gv, m_ref[...], v_ref[...])

    spec = pl.BlockSpec((tr, C), lambda i: (i, 0))
    got_spec = pl.BlockSpec((N_DEV - 1, tr, C), lambda i: (0, i, 0))
    sds = jax.ShapeDtypeStruct((R, C), F32)
    return pl.pallas_call(
        body, name=name, grid=(R // tr,), in_specs=[spec, spec, got_spec, spec, spec], out_specs=[spec] * 4,
        out_shape=[sds] * 4, compiler_params=_params("parallel"),
    )(w, g, got, m, v)


def _adamw_small(ws, gs, ms, vs):
    n = len(ws)

    def body(*refs):
        ins, outs = refs[:4 * n], refs[4 * n:]
        for p in range(n):
            w_ref, g_ref, m_ref, v_ref = (ins[q * n + p] for q in range(4))
            res = _adamw_math(w_ref[...], g_ref[...], m_ref[...], v_ref[...])
            for q in range(3):
                outs[q * n + p][...] = res[q]

    sds = [jax.ShapeDtypeStruct(w.shape, F32) for w in ws]
    outs = pl.pallas_call(body, name="adamw_small", out_shape=sds * 3)(*ws, *gs, *ms, *vs)
    return outs[:n], outs[n:2 * n], outs[2 * n:]


def _pad_rows(a, mult):
    r = (-a.shape[0]) % mult
    return a if r == 0 else jnp.concatenate([a, jnp.zeros((r,) + a.shape[1:], a.dtype)], axis=0)


def _as_rows(a, mult):
    flat = a.reshape(-1)
    r = (-flat.shape[0]) % LANES
    if r:
        flat = jnp.concatenate([flat, jnp.zeros((r,), a.dtype)])
    return _pad_rows(flat.reshape(-1, LANES), mult)


def _w_in_to_kernel_cols(w):
    pad = jnp.zeros(w.shape[:-1] + (Z_COLS - IN_COLS,), w.dtype)
    return jnp.concatenate([w[..., :1024], w[..., 1344:1856], w[..., 1024:1344], pad], axis=-1)


def _w_in_from_kernel_cols(w):
    return jnp.concatenate([w[..., :1024], w[..., 1536:1856], w[..., 1024:1536]], axis=-1)


def _dev_blocks(n, g):
    ffs = D_FF // N_DEV
    if n == "w_in":
        return _w_in_from_kernel_cols(g).reshape(N_DEV, D_MODEL // N_DEV, IN_COLS)
    if n == "w_q_up":
        return jnp.transpose(g.reshape(Q_LORA, N_DEV, MLA_QK_PAD)[:, :, :MLA_QK], (1, 0, 2))
    if n == "w_kv_up":
        return jnp.transpose(g.reshape(KV_LORA, N_DEV, MLA_NOPE + MLA_V), (1, 0, 2))
    if n in ("w_mem_kv", "w_o"):
        return g.reshape(N_DEV, D_MODEL // N_DEV, g.shape[1])
    if n in ("w_gate", "w_up"):
        return jnp.transpose(g.reshape(D_MODEL, N_DEV, ffs), (1, 0, 2))
    if n == "conv_w":
        return jnp.transpose(g.reshape(3, N_DEV, ffs), (1, 0, 2))
    assert n == "w_down"
    return g.reshape(N_DEV, ffs, D_MODEL)


def _send_blocks(n, g):
    blocks = _dev_blocks(n, g)
    return blocks if n == "conv_w" else blocks.astype(BF16)


def kernel(x, mem, g_mix, w_in, g_q_lat, w_q_up, g_kv_lat, w_kv_up, g_q_mla, g_k_mla, w_pool, pool_scale, g_mem, w_mem_kv, g_q_x, g_k_x, w_o, g_ffn, w_gate, w_up, conv_w, conv_b, w_down, loss_target, m_g_mix, m_w_in, m_g_q_lat, m_w_q_up, m_g_kv_lat, m_w_kv_up, m_g_q_mla, m_g_k_mla, m_w_pool, m_pool_scale, m_g_mem, m_w_mem_kv, m_g_q_x, m_g_k_x, m_w_o, m_g_ffn, m_w_gate, m_w_up, m_conv_w, m_conv_b, m_w_down, v_g_mix, v_w_in, v_g_q_lat, v_w_q_up, v_g_kv_lat, v_w_kv_up, v_g_q_mla, v_g_k_mla, v_w_pool, v_pool_scale, v_g_mem, v_w_mem_kv, v_g_q_x, v_g_k_x, v_w_o, v_g_ffn, v_w_gate, v_w_up, v_conv_w, v_conv_b, v_w_down):
    given = dict(g_mix=g_mix, w_in=w_in, g_q_lat=g_q_lat, w_q_up=w_q_up, g_kv_lat=g_kv_lat, w_kv_up=w_kv_up,
                 g_q_mla=g_q_mla, g_k_mla=g_k_mla, w_pool=w_pool, pool_scale=pool_scale, g_mem=g_mem,
                 w_mem_kv=w_mem_kv, g_q_x=g_q_x, g_k_x=g_k_x, w_o=w_o, g_ffn=g_ffn, w_gate=w_gate, w_up=w_up,
                 conv_w=conv_w, conv_b=conv_b, w_down=w_down)
    mom_m = dict(g_mix=m_g_mix, w_in=m_w_in, g_q_lat=m_g_q_lat, w_q_up=m_w_q_up, g_kv_lat=m_g_kv_lat,
                 w_kv_up=m_w_kv_up, g_q_mla=m_g_q_mla, g_k_mla=m_g_k_mla, w_pool=m_w_pool,
                 pool_scale=m_pool_scale, g_mem=m_g_mem, w_mem_kv=m_w_mem_kv, g_q_x=m_g_q_x, g_k_x=m_g_k_x,
                 w_o=m_w_o, g_ffn=m_g_ffn, w_gate=m_w_gate, w_up=m_w_up, conv_w=m_conv_w, conv_b=m_conv_b,
                 w_down=m_w_down)
    mom_v = dict(g_mix=v_g_mix, w_in=v_w_in, g_q_lat=v_g_q_lat, w_q_up=v_w_q_up, g_kv_lat=v_g_kv_lat,
                 w_kv_up=v_w_kv_up, g_q_mla=v_g_q_mla, g_k_mla=v_g_k_mla, w_pool=v_w_pool,
                 pool_scale=v_pool_scale, g_mem=v_g_mem, w_mem_kv=v_w_mem_kv, g_q_x=v_g_q_x, g_k_x=v_g_k_x,
                 w_o=v_w_o, g_ffn=v_g_ffn, w_gate=v_w_gate, w_up=v_w_up, conv_w=v_conv_w, conv_b=v_conv_b,
                 w_down=v_w_down)
    drop = lambda a: a[0] if a.ndim > 2 else a
    sh = {n: drop(given[n]) for n in WEIGHTS}
    mom_m = {n: drop(mom_m[n]) for n in WEIGHTS}
    mom_v = {n: drop(mom_v[n]) for n in WEIGHTS}

    cw_hi = sh["conv_w"].astype(BF16)
    cw_r = sh["conv_w"] - cw_hi.astype(F32)
    cw_mid = cw_r.astype(BF16)
    cw_lo = (cw_r - cw_mid.astype(F32)).astype(BF16)
    early_shards = [
        _w_in_to_kernel_cols(sh["w_in"]).astype(BF16),
        jnp.pad(sh["w_q_up"], ((0, 0), (0, MLA_QK_PAD - MLA_QK))).astype(BF16),
        sh["w_kv_up"].astype(BF16),
    ]
    late_shards = [sh["w_gate"].astype(BF16), sh["w_up"].astype(BF16), sh["w_down"].astype(BF16),
                   jnp.concatenate([cw_hi, cw_mid, cw_lo], axis=0), sh["w_o"].astype(BF16),
                   sh["w_mem_kv"].astype(BF16)]
    W = {"w_pool": sh["w_pool"].astype(BF16)}
    for n in ("g_mix", "g_q_lat", "g_kv_lat", "pool_scale", "g_mem", "g_q_x", "g_k_x", "g_ffn", "conv_b"):
        W[n] = sh[n]
    pad_qk = lambda gv: jnp.pad(gv, ((0, 0), (0, MLA_QK_PAD - MLA_QK)))
    W["g_q_mla"], W["g_k_mla"] = pad_qk(sh["g_q_mla"]), pad_qk(sh["g_k_mla"])

    loss_part, grad_x, G, got = _local_step(x[0], mem[0], loss_target[0], W, early_shards, late_shards)

    small = {
        "g_mix": G["g_mix"], "g_q_lat": G["g_q_lat"], "g_kv_lat": G["g_kv_lat"],
        "g_q_mla": G["g_q_mla"][:, :MLA_QK], "g_k_mla": G["g_k_mla"][:, :MLA_QK],
        "w_pool": G["w_pool"], "pool_scale": G["pool_scale"], "g_mem": G["g_mem"],
        "g_q_x": G["g_q_x"], "g_k_x": G["g_k_x"], "g_ffn": G["g_ffn"], "conv_b": G["conv_b"],
    }
    s_offs = {}
    segs = []
    off = 0
    for n in SMALL:
        r = _as_rows(small[n], 8)
        s_offs[n] = off
        off += r.shape[0]
        segs.append(r)
    segs.append(loss_part)
    loss_row = off
    sbuf = jnp.concatenate(segs, axis=0)
    s_sum = _sum_slots(_exchange_small(sbuf))

    def small_take(buf, n):
        shape = sh[n].shape
        cnt = math.prod(shape)
        return buf[s_offs[n]:s_offs[n] + -(-cnt // LANES)].reshape(-1)[:cnt].reshape(shape)

    loss = s_sum[loss_row, 0]

    me = 4 * lax.axis_index("x") + 2 * lax.axis_index("y") + lax.axis_index("c")
    grads, deltas, new_m, new_v = {}, {}, {}, {}
    for n in BIG:
        shape = sh[n].shape
        own = lax.dynamic_index_in_dim(_dev_blocks(n, G[n]), me, 0, keepdims=False)
        flat = lambda a: a.reshape(-1, shape[-1])
        outs = _adamw("adamw_" + n, flat(sh[n]), flat(own), got[n].reshape(N_DEV - 1, -1, shape[-1]),
                      flat(mom_m[n]), flat(mom_v[n]))
        grads[n], deltas[n], new_m[n], new_v[n] = (o.reshape(shape) for o in outs)

    for n in SMALL:
        grads[n] = small_take(s_sum, n)
    d_s, m_s, v_s = _adamw_small([sh[n] for n in SMALL], [grads[n] for n in SMALL],
                                 [mom_m[n] for n in SMALL], [mom_v[n] for n in SMALL])
    for j, n in enumerate(SMALL):
        deltas[n], new_m[n], new_v[n] = d_s[j], m_s[j], v_s[j]

    lead = lambda n, a: a.reshape(given[n].shape)
    return (loss, grad_x[None],
            *[lead(n, grads[n]) for n in WEIGHTS], *[lead(n, deltas[n]) for n in WEIGHTS],
            *[lead(n, new_m[n]) for n in WEIGHTS], *[lead(n, new_v[n]) for n in WEIGHTS])
```

```python
import math

import jax
import jax.numpy as jnp
from jax import lax
from jax.experimental import pallas as pl
from jax.experimental.pallas import tpu as pltpu

F32 = jnp.float32
BF16 = jnp.bfloat16

D_MODEL = 2048
D_FF = 5632
POOL_WIDTH = 512
POOL_WINDOWS = (2, 4, 8, 16)
POOL_HALO = 16
MLA_HEADS = 8
MLA_NOPE = 128
MLA_ROPE = 64
MLA_QK = MLA_NOPE + MLA_ROPE
MLA_QK_PAD = 256
MLA_V = 128
Q_LORA = 512
KV_LORA = 256
X_HEADS = 4
X_HEAD_DIM = 128
X_WIDTH = 512
MEM_LEN = 256
ROPE_THETA = 10000.0
NORM_EPS = 1e-6
CONV_HALO = 16
IN_COLS = 1856
Z_COLS = 1920
Z_POOL, Z_Q, Z_MQ, Z_KV, Z_KR = 0, 512, 1024, 1536, 1792

ADAM_LR = 0.001
ADAM_B1 = 0.9
ADAM_B2 = 0.999
ADAM_EPS = 1e-08
ADAM_WD = 0.01
ADAM_STEP = 10

N_DEV = 8
LANES = 128
VMEM_LIMIT = 56 * 1024 * 1024

BIG = ("w_in", "w_q_up", "w_kv_up", "w_mem_kv", "w_o", "w_gate", "w_up", "conv_w", "w_down")
SCATTER_EARLY = ("w_down", "w_gate", "w_up", "conv_w", "w_o", "w_mem_kv")
SCATTER_LATE = ("w_in", "w_q_up", "w_kv_up")
EARLY_GATHER_KINDS = ("rows", "cols", "cols")
FFN_GATHER_KINDS = ("lead", "lead", "rows", "lead")
MIXER_GATHER_KINDS = ("rows", "rows")
SMALL = ("g_mix", "g_q_lat", "g_kv_lat", "g_q_mla", "g_k_mla", "w_pool", "pool_scale", "g_mem",
         "g_q_x", "g_k_x", "g_ffn", "conv_b")
WEIGHTS = ("g_mix", "w_in", "g_q_lat", "w_q_up", "g_kv_lat", "w_kv_up", "g_q_mla", "g_k_mla", "w_pool",
           "pool_scale", "g_mem", "w_mem_kv", "g_q_x", "g_k_x", "w_o", "g_ffn", "w_gate", "w_up", "conv_w",
           "conv_b", "w_down")


def _tile(n, t):
    if n <= t:
        return n
    for c in range(t - t % LANES, 0, -LANES):
        if n % c == 0:
            return c
    return n


def _params(*sem):
    return pltpu.CompilerParams(dimension_semantics=sem, vmem_limit_bytes=VMEM_LIMIT)


def _full(shape):
    nd = len(shape)
    return pl.BlockSpec(shape, lambda *_: (0,) * nd)


def _rows(tr, w, cb=0):
    return pl.BlockSpec((tr, w), lambda i: (i, cb))


def _rms_hat(x, n):
    r = lax.rsqrt(jnp.sum(x * x, axis=-1, keepdims=True) / n + NORM_EPS)
    return x * r, r


def _rms_bwd(dy, xhat, r, g, n):
    dxh = dy * g
    dx = r * (dxh - xhat * (jnp.sum(dxh * xhat, axis=-1, keepdims=True) / n))
    return dx, jnp.sum(dy * xhat, axis=0, keepdims=True)


def _mm(name, a, b, *, ta=False, tb=False, add=None, out_dtype=F32, also_bf16=False, tm=512, tn=512, tk=512):
    if ta:
        K, M = a.shape
    else:
        M, K = a.shape
    if tb:
        N, K2 = b.shape
    else:
        K2, N = b.shape
    assert K == K2, (name, a.shape, b.shape)
    tm, tn, tk = _tile(M, tm), _tile(N, tn), _tile(K, tk)
    nk = K // tk
    grid = (M // tm, N // tn, nk)
    a_spec = pl.BlockSpec((tk, tm), lambda i, j, k: (k, i)) if ta else pl.BlockSpec((tm, tk), lambda i, j, k: (i, k))
    b_spec = pl.BlockSpec((tn, tk), lambda i, j, k: (j, k)) if tb else pl.BlockSpec((tk, tn), lambda i, j, k: (k, j))
    o_spec = pl.BlockSpec((tm, tn), lambda i, j, k: (i, j))
    dims = (((0,) if ta else (1,), (1,) if tb else (0,)), ((), ()))
    has_add = add is not None

    n_in = 3 if has_add else 2
    n_out = 2 if also_bf16 else 1

    def body(*refs):
        a_ref, b_ref = refs[:2]
        add_ref = refs[2] if has_add else None
        o_refs = refs[n_in:n_in + n_out]

        def write(r):
            if has_add:
                r = r + add_ref[...]
            for o_ref in o_refs:
                o_ref[...] = r.astype(o_ref.dtype)

        part = lax.dot_general(a_ref[...], b_ref[...], dims, preferred_element_type=F32)
        if nk == 1:
            write(part)
            return
        acc = refs[-1]
        k = pl.program_id(2)

        @pl.when(k == 0)
        def _():
            acc[...] = part

        @pl.when(k > 0)
        def _():
            acc[...] += part

        @pl.when(k == nk - 1)
        def _():
            write(acc[...])

    ins = [a, b] + ([add] if has_add else [])
    in_specs = [a_spec, b_spec] + ([o_spec] if has_add else [])
    out_shape = [jax.ShapeDtypeStruct((M, N), out_dtype)] + ([jax.ShapeDtypeStruct((M, N), BF16)] if also_bf16 else [])
    outs = pl.pallas_call(
        body, name=name, grid=grid, in_specs=in_specs, out_specs=[o_spec] * n_out, out_shape=out_shape,
        scratch_shapes=[pltpu.VMEM((tm, tn), F32)] if nk > 1 else [],
        compiler_params=_params("parallel", "parallel", "arbitrary"),
    )(*ins)
    return outs if also_bf16 else outs[0]


def _rms_fwd(name, x, g, shards, kinds, tr=512):
    S, n = x.shape
    tr = _tile(S, tr)
    n_steps = S // tr
    nw = len(shards)

    def body(x_ref, g_ref, *rest):
        x_refs, h_ref, g_refs = rest[:nw], rest[nw], rest[nw + 1:2 * nw + 1]
        gather = _Gather(x_refs, g_refs, kinds, *rest[2 * nw + 1:])
        i = pl.program_id(0)
        pl.when(i == 0)(gather.start)
        pl.when(i == n_steps // 2)(gather.forward)
        xhat, _ = _rms_hat(x_ref[...], n)
        h_ref[...] = (xhat * g_ref[...]).astype(BF16)
        pl.when(i == n_steps - 1)(gather.finish)

    return pl.pallas_call(
        body, name=name, grid=(n_steps,),
        in_specs=[_rows(tr, n), _full((1, n))] + [_ANY] * nw, out_specs=[_rows(tr, n)] + [_ANY] * nw,
        out_shape=[jax.ShapeDtypeStruct((S, n), BF16)]
        + [jax.ShapeDtypeStruct(_gather_out_shape(kd, sd.shape), sd.dtype) for kd, sd in zip(kinds, shards)],
        scratch_shapes=[pltpu.SemaphoreType.DMA((nw, 7)), pltpu.SemaphoreType.DMA((nw, 7)),
                        pltpu.SemaphoreType.DMA((nw,))],
        compiler_params=_params("arbitrary"),
    )(x, g, *shards)


def _mm_up_dx_rms(du, w, part, x, g, extra, tm=256):
    S, K = du.shape
    n = w.shape[0]
    tm = _tile(S, tm)

    def body(du_ref, w_ref, p_ref, x_ref, g_ref, ex_ref, dx_ref, dxb_ref, dg_ref):
        @pl.when(pl.program_id(0) == 0)
        def _():
            dg_ref[...] = jnp.zeros_like(dg_ref)

        dh = lax.dot_general(du_ref[...], w_ref[...], _NT, preferred_element_type=F32) + p_ref[...]
        xhat, r = _rms_hat(x_ref[...], n)
        dx, dg = _rms_bwd(dh, xhat, r, g_ref[...], n)
        dx = dx + ex_ref[...]
        dx_ref[...] = dx
        dxb_ref[...] = dx.astype(BF16)
        dg_ref[...] += dg

    return pl.pallas_call(
        body, name="mm_up_dx_rms", grid=(S // tm,),
        in_specs=[_rows(tm, K), pl.BlockSpec((n, K), lambda i: (0, 0), pipeline_mode=pl.Buffered(1)),
                  _rows(tm, n), _rows(tm, n), _full((1, n)), _rows(tm, n)],
        out_specs=[_rows(tm, n), _rows(tm, n), _full((1, n))],
        out_shape=[jax.ShapeDtypeStruct((S, n), F32), jax.ShapeDtypeStruct((S, n), BF16),
                   jax.ShapeDtypeStruct((1, n), F32)],
        compiler_params=_params("arbitrary"),
    )(du, w, part, x, g, extra)


def _lat_fwd(z, gq, gkv, tr=512):
    S = z.shape[0]
    tr = _tile(S, tr)

    def body(zq_ref, zkv_ref, gq_ref, gkv_ref, ql_ref, kvl_ref):
        xq, _ = _rms_hat(zq_ref[...], Q_LORA)
        ql_ref[...] = (xq * gq_ref[...]).astype(BF16)
        xkv, _ = _rms_hat(zkv_ref[...], KV_LORA)
        kvl_ref[...] = (xkv * gkv_ref[...]).astype(BF16)

    return pl.pallas_call(
        body, name="lat_fwd", grid=(S // tr,),
        in_specs=[_rows(tr, Q_LORA, Z_Q // Q_LORA), _rows(tr, KV_LORA, Z_KV // KV_LORA),
                  _full((1, Q_LORA)), _full((1, KV_LORA))],
        out_specs=[_rows(tr, Q_LORA), _rows(tr, KV_LORA)],
        out_shape=[jax.ShapeDtypeStruct((S, Q_LORA), BF16), jax.ShapeDtypeStruct((S, KV_LORA), BF16)],
        compiler_params=_params("parallel"),
    )(z, z, gq, gkv)


def _lat_bwd(z, gq, gkv, dql, dkvl, tr=512):
    S = z.shape[0]
    tr = _tile(S, tr)

    def body(zq_ref, zkv_ref, gq_ref, gkv_ref, dql_ref, dkvl_ref, dzq_ref, dzkv_ref, dgq_ref, dgkv_ref):
        @pl.when(pl.program_id(0) == 0)
        def _():
            dgq_ref[...] = jnp.zeros_like(dgq_ref)
            dgkv_ref[...] = jnp.zeros_like(dgkv_ref)

        xq, rq = _rms_hat(zq_ref[...], Q_LORA)
        dx, dg = _rms_bwd(dql_ref[...], xq, rq, gq_ref[...], Q_LORA)
        dzq_ref[...] = dx.astype(BF16)
        dgq_ref[...] += dg
        xkv, rkv = _rms_hat(zkv_ref[...], KV_LORA)
        dx, dg = _rms_bwd(dkvl_ref[...], xkv, rkv, gkv_ref[...], KV_LORA)
        dzkv_ref[...] = dx.astype(BF16)
        dgkv_ref[...] += dg

    return pl.pallas_call(
        body, name="lat_bwd", grid=(S // tr,),
        in_specs=[_rows(tr, Q_LORA, Z_Q // Q_LORA), _rows(tr, KV_LORA, Z_KV // KV_LORA),
                  _full((1, Q_LORA)), _full((1, KV_LORA)), _rows(tr, Q_LORA), _rows(tr, KV_LORA)],
        out_specs=[_rows(tr, Q_LORA), _rows(tr, KV_LORA), _full((1, Q_LORA)), _full((1, KV_LORA))],
        out_shape=[jax.ShapeDtypeStruct((S, Q_LORA), BF16), jax.ShapeDtypeStruct((S, KV_LORA), BF16),
                   jax.ShapeDtypeStruct((1, Q_LORA), F32), jax.ShapeDtypeStruct((1, KV_LORA), F32)],
        compiler_params=_params("arbitrary"),
    )(z, z, gq, gkv, dql, dkvl)


def _pool_d(ext, cur, t0, tr):
    t = t0 + lax.broadcasted_iota(jnp.int32, (tr, 1), 0)
    ds = []
    for gi, w in enumerate(POOL_WINDOWS):
        s = ext[:, gi * LANES:(gi + 1) * LANES]
        sh = 1
        while sh < w:
            s = s + pltpu.roll(s, sh, 0)
            sh *= 2
        cnt = jnp.minimum(t + 1, w).astype(F32)
        ds.append(s[POOL_HALO:] / cnt - cur[:, gi * LANES:(gi + 1) * LANES])
    return ds


def _pool_fwd(z, wp, scale, tr=512):
    S = z.shape[0]
    tr = _tile(S, tr)
    rb = tr // POOL_HALO

    def body(z_ref, zp_ref, wp_ref, sc_ref, y_ref):
        i = pl.program_id(0)
        cur = z_ref[...]
        halo = jnp.where(i == 0, 0.0, zp_ref[...])
        ds = _pool_d(jnp.concatenate([halo, cur], axis=0), cur, i * tr, tr)
        for gi in range(len(POOL_WINDOWS)):
            y = jnp.dot(ds[gi].astype(BF16), wp_ref[gi], preferred_element_type=F32)
            y_ref[:, gi * LANES:(gi + 1) * LANES] = (y * sc_ref[:, gi * LANES:(gi + 1) * LANES]).astype(BF16)

    return pl.pallas_call(
        body, name="pool_fwd", grid=(S // tr,),
        in_specs=[_rows(tr, POOL_WIDTH, Z_POOL // POOL_WIDTH),
                  pl.BlockSpec((POOL_HALO, POOL_WIDTH), lambda i: (jnp.maximum(i * rb - 1, 0), 0)),
                  _full(wp.shape), _full((1, POOL_WIDTH))],
        out_specs=_rows(tr, POOL_WIDTH),
        out_shape=jax.ShapeDtypeStruct((S, POOL_WIDTH), BF16),
        compiler_params=_params("parallel"),
    )(z, z, wp, scale)


def _pool_bwd(z, dyc, wp, scale, dy_cb, tr=512):
    S = z.shape[0]
    tr = _tile(S, tr)
    rb = tr // POOL_HALO
    nhb = S // POOL_HALO
    ng = len(POOL_WINDOWS)

    def body(z_ref, zp_ref, dy_ref, dyn_ref, wp_ref, sc_ref, dz_ref, dwp_ref, dsc_ref):
        i = pl.program_id(0)

        @pl.when(i == 0)
        def _():
            dwp_ref[...] = jnp.zeros_like(dwp_ref)
            dsc_ref[...] = jnp.zeros_like(dsc_ref)

        cur = z_ref[...]
        halo = jnp.where(i == 0, 0.0, zp_ref[...])
        ds = _pool_d(jnp.concatenate([halo, cur], axis=0), cur, i * tr, tr)
        dy_cur = dy_ref[...]
        dy_next = jnp.where(i == pl.num_programs(0) - 1, 0.0, dyn_ref[...])
        dy_ext = jnp.concatenate([dy_cur, dy_next], axis=0)
        n_ext = tr + POOL_HALO
        t_ext = i * tr + lax.broadcasted_iota(jnp.int32, (n_ext, 1), 0)
        for gi, w in enumerate(POOL_WINDOWS):
            cols = slice(gi * LANES, (gi + 1) * LANES)
            d_b = ds[gi].astype(BF16)
            y_pre = jnp.dot(d_b, wp_ref[gi], preferred_element_type=F32)
            dsc_ref[:, cols] += jnp.sum(dy_cur[:, cols] * y_pre, axis=0, keepdims=True)
            dys = (dy_ext[:, cols] * sc_ref[:, cols]).astype(BF16)
            dwp_ref[gi] += lax.dot_general(d_b, dys[:tr], (((0,), (0,)), ((), ())), preferred_element_type=F32)
            dd = lax.dot_general(dys, wp_ref[gi], (((1,), (1,)), ((), ())), preferred_element_type=F32)
            s = dd / jnp.minimum(t_ext + 1, w).astype(F32)
            sh = 1
            while sh < w:
                s = s + pltpu.roll(s, n_ext - sh, 0)
                sh *= 2
            dz_ref[:, cols] = (s[:tr] - dd[:tr]).astype(BF16)

    return pl.pallas_call(
        body, name="pool_bwd", grid=(S // tr,),
        in_specs=[_rows(tr, POOL_WIDTH, Z_POOL // POOL_WIDTH),
                  pl.BlockSpec((POOL_HALO, POOL_WIDTH), lambda i: (jnp.maximum(i * rb - 1, 0), 0)),
                  _rows(tr, POOL_WIDTH, dy_cb),
                  pl.BlockSpec((POOL_HALO, POOL_WIDTH), lambda i: (jnp.minimum((i + 1) * rb, nhb - 1), dy_cb)),
                  _full(wp.shape), _full((1, POOL_WIDTH))],
        out_specs=[_rows(tr, POOL_WIDTH), _full((ng, LANES, LANES)), _full((1, POOL_WIDTH))],
        out_shape=[jax.ShapeDtypeStruct((S, POOL_WIDTH), BF16), jax.ShapeDtypeStruct((ng, LANES, LANES), F32),
                   jax.ShapeDtypeStruct((1, POOL_WIDTH), F32)],
        compiler_params=_params("arbitrary"),
    )(z, z, dyc, dyc, wp, scale)


def _rope_tables(S):
    half = MLA_ROPE // 2
    inv_freq = 1.0 / (ROPE_THETA ** (jnp.arange(half, dtype=F32) / half))
    ang = jnp.arange(S).astype(F32)[:, None] * inv_freq[None, :]
    cos, sin = jnp.cos(ang), jnp.sin(ang)
    z32 = jnp.zeros((S, half), F32)
    z64 = jnp.zeros((S, LANES - MLA_ROPE), F32)
    cos_t = jnp.concatenate([cos, cos, z64], axis=1)
    sin_a = jnp.concatenate([-sin, z32, z64], axis=1)
    sin_b = jnp.concatenate([z32, sin, z64], axis=1)
    return cos_t, sin_a, sin_b


def _rope(x, cos_t, sin_a, sin_b):
    return x * cos_t + pltpu.roll(x, LANES - 32, 1) * sin_a + pltpu.roll(x, 32, 1) * sin_b


def _rope_t(d, cos_t, sin_a, sin_b):
    return d * cos_t + pltpu.roll(d * sin_a, 32, 1) + pltpu.roll(d * sin_b, LANES - 32, 1)


def _mla_prep(q_raw, kv_raw, z, tabs, gq, gk, shards, kinds, tr=256):
    S = z.shape[0]
    tr = _tile(S, tr)
    n_steps = S // tr
    scale = 1.0 / math.sqrt(MLA_QK)
    W = MLA_HEADS * MLA_QK_PAD
    nw = len(shards)

    def body(q_ref, kv_ref, kr_ref, c_ref, sa_ref, sb_ref, gq_ref, gk_ref, *rest):
        x_refs, (qo_ref, ko_ref, vo_ref), g_refs = rest[:nw], rest[nw:nw + 3], rest[nw + 3:2 * nw + 3]
        gather = _Gather(x_refs, g_refs, kinds, *rest[2 * nw + 3:])
        i = pl.program_id(0)
        pl.when(i == 0)(gather.start)
        pl.when(i == n_steps // 2)(gather.forward)
        tab = (c_ref[...], sa_ref[...], sb_ref[...])
        kr = kr_ref[...]
        kr_ss = jnp.sum(kr * kr, axis=-1, keepdims=True)
        gqn, gqr = gq_ref[:, :LANES], gq_ref[:, LANES:]
        gkn, gkr = gk_ref[:, :LANES], gk_ref[:, LANES:]
        for h in range(MLA_HEADS):
            c0 = h * MLA_QK_PAD
            qn = q_ref[:, c0:c0 + LANES].astype(F32)
            qr = q_ref[:, c0 + LANES:c0 + 2 * LANES].astype(F32)
            r = lax.rsqrt((jnp.sum(qn * qn, -1, keepdims=True) + jnp.sum(qr * qr, -1, keepdims=True)) / MLA_QK
                          + NORM_EPS)
            qo_ref[h, :, :LANES] = (qn * r * gqn * scale).astype(BF16)
            qo_ref[h, :, LANES:] = (_rope(qr * r * gqr, *tab) * scale).astype(BF16)
            kn = kv_ref[:, c0:c0 + LANES].astype(F32)
            r = lax.rsqrt((jnp.sum(kn * kn, -1, keepdims=True) + kr_ss) / MLA_QK + NORM_EPS)
            ko_ref[h, :, :LANES] = (kn * r * gkn).astype(BF16)
            ko_ref[h, :, LANES:] = _rope(kr * r * gkr, *tab).astype(BF16)
            vo_ref[h] = kv_ref[:, c0 + LANES:c0 + 2 * LANES].astype(BF16)
        pl.when(i == n_steps - 1)(gather.finish)

    hs = lambda w: pl.BlockSpec((MLA_HEADS, tr, w), lambda i: (0, i, 0))
    return pl.pallas_call(
        body, name="mla_prep", grid=(n_steps,),
        in_specs=[_rows(tr, W), _rows(tr, W), _rows(tr, LANES, Z_KR // LANES),
                  _rows(tr, LANES), _rows(tr, LANES), _rows(tr, LANES),
                  _full((1, MLA_QK_PAD)), _full((1, MLA_QK_PAD))] + [_ANY] * nw,
        out_specs=[hs(MLA_QK_PAD), hs(MLA_QK_PAD), hs(MLA_V)] + [_ANY] * nw,
        out_shape=[jax.ShapeDtypeStruct((MLA_HEADS, S, MLA_QK_PAD), BF16),
                   jax.ShapeDtypeStruct((MLA_HEADS, S, MLA_QK_PAD), BF16),
                   jax.ShapeDtypeStruct((MLA_HEADS, S, MLA_V), BF16)]
        + [jax.ShapeDtypeStruct(_gather_out_shape(kd, sd.shape), sd.dtype) for kd, sd in zip(kinds, shards)],
        scratch_shapes=[pltpu.SemaphoreType.DMA((nw, 7)), pltpu.SemaphoreType.DMA((nw, 7)),
                        pltpu.SemaphoreType.DMA((nw,))],
        compiler_params=_params("arbitrary"),
    )(q_raw, kv_raw, z, *tabs, gq, gk, *shards)


def _mla_prep_bwd(dq, dk, dv, q_raw, kv_raw, z, tabs, gq, gk, tr=256):
    S = z.shape[0]
    tr = _tile(S, tr)
    scale = 1.0 / math.sqrt(MLA_QK)
    W = MLA_HEADS * MLA_QK_PAD

    def body(dq_ref, dk_ref, dv_ref, q_ref, kv_ref, kr_ref, c_ref, sa_ref, sb_ref, gq_ref, gk_ref,
             dqr_ref, dkvr_ref, dkr_ref, dgq_ref, dgk_ref):
        @pl.when(pl.program_id(0) == 0)
        def _():
            dgq_ref[...] = jnp.zeros_like(dgq_ref)
            dgk_ref[...] = jnp.zeros_like(dgk_ref)

        tab = (c_ref[...], sa_ref[...], sb_ref[...])
        kr = kr_ref[...]
        kr_ss = jnp.sum(kr * kr, axis=-1, keepdims=True)
        gqn, gqr = gq_ref[:, :LANES], gq_ref[:, LANES:]
        gkn, gkr = gk_ref[:, :LANES], gk_ref[:, LANES:]
        dkr_sum = jnp.zeros((tr, LANES), F32)
        dgq_n = jnp.zeros((1, LANES), F32)
        dgq_r = jnp.zeros((1, LANES), F32)
        dgk_n = jnp.zeros((1, LANES), F32)
        dgk_r = jnp.zeros((1, LANES), F32)

        def head_norm_bwd(xn, xr, r, dyn, dyr, gn, gr):
            hn, hr = xn * r, xr * r
            dxn, dxr = dyn * gn, dyr * gr
            mt = (jnp.sum(dxn * hn, -1, keepdims=True) + jnp.sum(dxr * hr, -1, keepdims=True)) / MLA_QK
            return (r * (dxn - hn * mt), r * (dxr - hr * mt),
                    jnp.sum(dyn * hn, axis=0, keepdims=True), jnp.sum(dyr * hr, axis=0, keepdims=True))

        for h in range(MLA_HEADS):
            c0 = h * MLA_QK_PAD
            qn = q_ref[:, c0:c0 + LANES].astype(F32)
            qr = q_ref[:, c0 + LANES:c0 + 2 * LANES].astype(F32)
            r = lax.rsqrt((jnp.sum(qn * qn, -1, keepdims=True) + jnp.sum(qr * qr, -1, keepdims=True)) / MLA_QK
                          + NORM_EPS)
            dyn = dq_ref[h, :, :LANES] * scale
            dyr = _rope_t(dq_ref[h, :, LANES:] * scale, *tab)
            dxn, dxr, gn_, gr_ = head_norm_bwd(qn, qr, r, dyn, dyr, gqn, gqr)
            dgq_n += gn_
            dgq_r += gr_
            dqr_ref[:, c0:c0 + LANES] = dxn.astype(BF16)
            dqr_ref[:, c0 + LANES:c0 + 2 * LANES] = dxr.astype(BF16)

            kn = kv_ref[:, c0:c0 + LANES].astype(F32)
            r = lax.rsqrt((jnp.sum(kn * kn, -1, keepdims=True) + kr_ss) / MLA_QK + NORM_EPS)
            dyn = dk_ref[h, :, :LANES]
            dyr = _rope_t(dk_ref[h, :, LANES:], *tab)
            dxn, dxr, gn_, gr_ = head_norm_bwd(kn, kr, r, dyn, dyr, gkn, gkr)
            dgk_n += gn_
            dgk_r += gr_
            dkr_sum += dxr
            dkvr_ref[:, c0:c0 + LANES] = dxn.astype(BF16)
            dkvr_ref[:, c0 + LANES:c0 + 2 * LANES] = dv_ref[h].astype(BF16)

        dkr_ref[...] = dkr_sum.astype(BF16)
        dgq_ref[:, :LANES] += dgq_n
        dgq_ref[:, LANES:] += dgq_r
        dgk_ref[:, :LANES] += dgk_n
        dgk_ref[:, LANES:] += dgk_r

    hs = lambda w: pl.BlockSpec((MLA_HEADS, tr, w), lambda i: (0, i, 0))
    return pl.pallas_call(
        body, name="mla_prep_bwd", grid=(S // tr,),
        in_specs=[hs(MLA_QK_PAD), hs(MLA_QK_PAD), hs(MLA_V), _rows(tr, W), _rows(tr, W),
                  _rows(tr, LANES, Z_KR // LANES), _rows(tr, LANES), _rows(tr, LANES), _rows(tr, LANES),
                  _full((1, MLA_QK_PAD)), _full((1, MLA_QK_PAD))],
        out_specs=[_rows(tr, W), _rows(tr, W), _rows(tr, LANES), _full((1, MLA_QK_PAD)), _full((1, MLA_QK_PAD))],
        out_shape=[jax.ShapeDtypeStruct((S, W), BF16), jax.ShapeDtypeStruct((S, W), BF16),
                   jax.ShapeDtypeStruct((S, LANES), BF16),
                   jax.ShapeDtypeStruct((1, MLA_QK_PAD), F32), jax.ShapeDtypeStruct((1, MLA_QK_PAD), F32)],
        compiler_params=_params("arbitrary"),
    )(dq, dk, dv, q_raw, kv_raw, z, *tabs, gq, gk)


_NT = (((1,), (1,)), ((), ()))
_TN = (((0,), (0,)), ((), ()))


def _flash_fwd(q, k, v, shards, kinds, tq=1024, tk=1024, ts=512):
    H, S, dq = q.shape
    dv = v.shape[-1]
    tq, tk, ts = _tile(S, tq), _tile(S, tk), _tile(S, ts)
    assert tq % tk == 0 and tq % ts == 0 and dv == LANES
    nq, nd = S // tq, tq // tk
    nw = len(shards)

    def body(q_ref, k_ref, v_ref, *rest):
        x_refs, (o_ref, lse_ref), g_refs = rest[:nw], rest[nw:nw + 2], rest[nw + 2:2 * nw + 2]
        m_sc, l_sc, acc_sc, send_sems, recv_sems, local_sems = rest[2 * nw + 2:]
        h, i = pl.program_id(0), pl.program_id(1)
        gather = _Gather(x_refs, g_refs, kinds, send_sems, recv_sems, local_sems)
        pl.when(jnp.logical_and(h == 0, i == 0))(gather.start)
        pl.when(jnp.logical_and(h == (3 * H) // 4, i == 0))(gather.forward)

        m_sc[...] = jnp.full_like(m_sc, -jnp.inf)
        l_sc[...] = jnp.zeros_like(l_sc)
        acc_sc[...] = jnp.zeros_like(acc_sc)

        def step(off, width, r0, masked):
            rows = slice(r0, tq)
            s = lax.dot_general(q_ref[rows, :], k_ref[pl.ds(off, width), :], _NT, preferred_element_type=F32)
            if masked:
                row = i * tq + r0 + lax.broadcasted_iota(jnp.int32, s.shape, 0)
                col = off + lax.broadcasted_iota(jnp.int32, s.shape, 1)
                s = jnp.where(col <= row, s, -jnp.inf)
            parts = [s[:, c * LANES:(c + 1) * LANES] for c in range(width // LANES)]
            m_cur = parts[0]
            for pt in parts[1:]:
                m_cur = jnp.maximum(m_cur, pt)
            m_prev = m_sc[rows, :]
            m_new = jnp.maximum(m_prev, jnp.max(m_cur, axis=-1, keepdims=True))
            alpha = jnp.exp(m_prev - m_new)
            ps = [jnp.exp(pt - m_new) for pt in parts]
            l_new = alpha * l_sc[rows, :]
            for pc in ps:
                l_new = l_new + pc
            l_sc[rows, :] = l_new
            p = jnp.concatenate(ps, axis=1).astype(BF16)
            acc_sc[rows, :] = alpha * acc_sc[rows, :] + jnp.dot(p, v_ref[pl.ds(off, width), :],
                                                                preferred_element_type=F32)
            m_sc[rows, :] = m_new

        def full_step(j, carry):
            step(pl.multiple_of(j * tk, tk), tk, 0, False)
            return carry

        lax.fori_loop(0, i * nd, full_step, 0)
        for d in range(tq // ts):
            step(pl.multiple_of(i * tq + d * ts, ts), ts, d * ts, True)
        l = jnp.sum(l_sc[...], axis=-1, keepdims=True)
        o_ref[...] = (acc_sc[...] / l).astype(BF16)
        lse_ref[...] = m_sc[...] + jnp.log(l)
        pl.when(jnp.logical_and(h == H - 1, i == nq - 1))(gather.finish)

    return pl.pallas_call(
        body, name="flash_fwd", grid=(H, nq),
        in_specs=[pl.BlockSpec((None, tq, dq), lambda h, i: (h, i, 0)),
                  pl.BlockSpec((None, S, dq), lambda h, i: (h, 0, 0)),
                  pl.BlockSpec((None, S, dv), lambda h, i: (h, 0, 0))] + [_ANY] * nw,
        out_specs=[pl.BlockSpec((tq, dv), lambda h, i: (i, h)),
                   pl.BlockSpec((None, tq, LANES), lambda h, i: (h, i, 0))] + [_ANY] * nw,
        out_shape=[jax.ShapeDtypeStruct((S, H * dv), BF16), jax.ShapeDtypeStruct((H, S, LANES), F32)]
        + [jax.ShapeDtypeStruct(_gather_out_shape(kd, sd.shape), sd.dtype) for kd, sd in zip(kinds, shards)],
        scratch_shapes=[pltpu.VMEM((tq, LANES), F32), pltpu.VMEM((tq, LANES), F32), pltpu.VMEM((tq, dv), F32),
                        pltpu.SemaphoreType.DMA((nw, 7)), pltpu.SemaphoreType.DMA((nw, 7)),
                        pltpu.SemaphoreType.DMA((nw,))],
        compiler_params=_params("arbitrary", "arbitrary"),
    )(q, k, v, *shards)


def _flash_bwd(q, k, v, dyc, cb0, o, lse, sends, tq=1024, tk=1024, ts=512):
    H, S, dq = q.shape
    dv = v.shape[-1]
    tq, tk, ts = _tile(S, tq), _tile(S, tk), _tile(S, ts)
    assert tq % tk == 0 and tq % ts == 0
    nq, nd = S // tq, tq // tk
    nw = len(sends)

    def body(q_ref, k_ref, v_ref, dy_ref, o_ref, lse_ref, *rest):
        s_refs, (dq_ref, dk_ref, dv_ref), r_refs = rest[:nw], rest[nw:nw + 3], rest[nw + 3:2 * nw + 3]
        do_sc, dl_sc, send_sems, recv_sems = rest[2 * nw + 3:]
        h, i = pl.program_id(0), pl.program_id(1)
        scatter = _Scatter(s_refs, r_refs, send_sems, recv_sems)
        pl.when(jnp.logical_and(h == 0, i == 0))(scatter.start)

        @pl.when(i == 0)
        def _():
            dk_ref[...] = jnp.zeros_like(dk_ref)
            dv_ref[...] = jnp.zeros_like(dv_ref)

        dq_ref[...] = jnp.zeros_like(dq_ref)
        dy = dy_ref[...]
        do_sc[...] = dy.astype(BF16)
        dl_sc[...] = jnp.broadcast_to(jnp.sum(dy * o_ref[...].astype(F32), axis=-1, keepdims=True), (tq, LANES))

        def step(off, width, r0, masked):
            rows = slice(r0, tq)
            qv = q_ref[rows, :]
            dov = do_sc[rows, :]
            kb = k_ref[pl.ds(off, width), :]
            s = lax.dot_general(qv, kb, _NT, preferred_element_type=F32)
            if masked:
                row = i * tq + r0 + lax.broadcasted_iota(jnp.int32, s.shape, 0)
                col = off + lax.broadcasted_iota(jnp.int32, s.shape, 1)
                s = jnp.where(col <= row, s, -jnp.inf)
            p = jnp.exp(s - lse_ref[rows, :1])
            dv_ref[pl.ds(off, width), :] += lax.dot_general(p.astype(BF16), dov, _TN, preferred_element_type=F32)
            dp = lax.dot_general(dov, v_ref[pl.ds(off, width), :], _NT, preferred_element_type=F32)
            ds = (p * (dp - dl_sc[rows, :1])).astype(BF16)
            dk_ref[pl.ds(off, width), :] += lax.dot_general(ds, qv, _TN, preferred_element_type=F32)
            dq_ref[rows, :] += jnp.dot(ds, kb, preferred_element_type=F32)

        def full_step(j, carry):
            step(pl.multiple_of(j * tk, tk), tk, 0, False)
            return carry

        lax.fori_loop(0, i * nd, full_step, 0)
        for d in range(tq // ts):
            step(pl.multiple_of(i * tq + d * ts, ts), ts, d * ts, True)
        pl.when(jnp.logical_and(h == H - 1, i == nq - 1))(scatter.finish)

    return pl.pallas_call(
        body, name="flash_bwd", grid=(H, nq),
        in_specs=[pl.BlockSpec((None, tq, dq), lambda h, i: (h, i, 0)),
                  pl.BlockSpec((None, S, dq), lambda h, i: (h, 0, 0), pipeline_mode=pl.Buffered(1)),
                  pl.BlockSpec((None, S, dv), lambda h, i: (h, 0, 0), pipeline_mode=pl.Buffered(1)),
                  pl.BlockSpec((tq, dv), lambda h, i: (i, h + cb0)),
                  pl.BlockSpec((tq, dv), lambda h, i: (i, h)),
                  pl.BlockSpec((None, tq, LANES), lambda h, i: (h, i, 0))] + [_ANY] * nw,
        out_specs=[pl.BlockSpec((None, tq, dq), lambda h, i: (h, i, 0)),
                   pl.BlockSpec((None, S, dq), lambda h, i: (h, 0, 0)),
                   pl.BlockSpec((None, S, dv), lambda h, i: (h, 0, 0))] + [_ANY] * nw,
        out_shape=[jax.ShapeDtypeStruct((H, S, dq), F32), jax.ShapeDtypeStruct((H, S, dq), F32),
                   jax.ShapeDtypeStruct((H, S, dv), F32)]
        + [jax.ShapeDtypeStruct((N_DEV - 1,) + sd.shape[1:], sd.dtype) for sd in sends],
        scratch_shapes=[pltpu.VMEM((tq, dv), BF16), pltpu.VMEM((tq, LANES), F32),
                        pltpu.SemaphoreType.DMA((nw, N_DEV - 1)), pltpu.SemaphoreType.DMA((nw, N_DEV - 1))],
        compiler_params=_params("arbitrary", "arbitrary"),
    )(q, k, v, dyc, o, lse, *sends)


def _mem_norm(mem, g):
    n = mem.shape[1]

    def body(m_ref, g_ref, o_ref):
        xhat, _ = _rms_hat(m_ref[...], n)
        o_ref[...] = (xhat * g_ref[...]).astype(BF16)

    return pl.pallas_call(body, name="mem_norm", out_shape=jax.ShapeDtypeStruct(mem.shape, BF16),
                          compiler_params=pltpu.CompilerParams(vmem_limit_bytes=VMEM_LIMIT))(mem, g)


def _memkv_prep(mkv, gk):
    M = mkv.shape[0]

    def body(mkv_ref, g_ref, k_ref, v_ref):
        for h in range(X_HEADS):
            cols = slice(h * X_HEAD_DIM, (h + 1) * X_HEAD_DIM)
            xhat, _ = _rms_hat(mkv_ref[:, cols], X_HEAD_DIM)
            k_ref[:, cols] = (xhat * g_ref[...]).astype(BF16)
        v_ref[...] = mkv_ref[:, X_WIDTH:].astype(BF16)

    return pl.pallas_call(
        body, name="memkv_prep",
        out_shape=[jax.ShapeDtypeStruct((M, X_WIDTH), BF16), jax.ShapeDtypeStruct((M, X_WIDTH), BF16)],
    )(mkv, gk)


def _memkv_bwd(dk, dv, mkv, gk):
    M = mkv.shape[0]

    def body(dk_ref, dv_ref, mkv_ref, g_ref, dmkv_ref, dg_ref):
        dg = jnp.zeros((1, X_HEAD_DIM), F32)
        for h in range(X_HEADS):
            cols = slice(h * X_HEAD_DIM, (h + 1) * X_HEAD_DIM)
            xhat, r = _rms_hat(mkv_ref[:, cols], X_HEAD_DIM)
            dx, dgh = _rms_bwd(dk_ref[:, cols], xhat, r, g_ref[...], X_HEAD_DIM)
            dmkv_ref[:, cols] = dx.astype(BF16)
            dg += dgh
        dmkv_ref[:, X_WIDTH:] = dv_ref[...].astype(BF16)
        dg_ref[...] = dg

    return pl.pallas_call(
        body, name="memkv_bwd",
        out_shape=[jax.ShapeDtypeStruct((M, 2 * X_WIDTH), BF16), jax.ShapeDtypeStruct((1, X_HEAD_DIM), F32)],
    )(dk, dv, mkv, gk)


def _mem_gain_bwd(mem, dmn):
    n = mem.shape[1]

    def body(m_ref, d_ref, dg_ref):
        xhat, _ = _rms_hat(m_ref[...], n)
        dg_ref[...] = jnp.sum(d_ref[...] * xhat, axis=0, keepdims=True)

    return pl.pallas_call(body, name="mem_gain_bwd", out_shape=jax.ShapeDtypeStruct((1, n), F32),
                          compiler_params=pltpu.CompilerParams(vmem_limit_bytes=VMEM_LIMIT))(mem, dmn)


def _memx_scores(zq_ref, g, kx_ref, h):
    cols = slice(h * X_HEAD_DIM, (h + 1) * X_HEAD_DIM)
    xhat, r = _rms_hat(zq_ref[:, cols], X_HEAD_DIM)
    qn = (xhat * g).astype(BF16)
    s = lax.dot_general(qn, kx_ref[:, cols], _NT, preferred_element_type=F32) * (1.0 / math.sqrt(X_HEAD_DIM))
    p = jnp.exp(s - jnp.max(s, axis=-1, keepdims=True))
    return cols, xhat, r, qn, p, jnp.sum(p, axis=-1, keepdims=True)


def _memx_fwd(z, g, kx, vx, tr=512):
    S = z.shape[0]
    tr = _tile(S, tr)
    M = kx.shape[0]

    def body(zq_ref, g_ref, kx_ref, vx_ref, y_ref):
        for h in range(X_HEADS):
            cols, _, _, _, p, l = _memx_scores(zq_ref, g_ref[...], kx_ref, h)
            o = jnp.dot(p.astype(BF16), vx_ref[:, cols], preferred_element_type=F32)
            y_ref[:, cols] = (o / l).astype(BF16)

    return pl.pallas_call(
        body, name="memx_fwd", grid=(S // tr,),
        in_specs=[_rows(tr, X_WIDTH, Z_MQ // X_WIDTH), _full((1, X_HEAD_DIM)), _full((M, X_WIDTH)),
                  _full((M, X_WIDTH))],
        out_specs=_rows(tr, X_WIDTH),
        out_shape=jax.ShapeDtypeStruct((S, X_WIDTH), BF16),
        compiler_params=_params("parallel"),
    )(z, g, kx, vx)


def _memx_bwd(z, g, kx, vx, dyc, dy_cb, tr=512):
    S = z.shape[0]
    tr = _tile(S, tr)
    M = kx.shape[0]
    scale = 1.0 / math.sqrt(X_HEAD_DIM)

    def body(zq_ref, g_ref, kx_ref, vx_ref, dy_ref, dz_ref, dk_ref, dv_ref, dg_ref):
        @pl.when(pl.program_id(0) == 0)
        def _():
            dk_ref[...] = jnp.zeros_like(dk_ref)
            dv_ref[...] = jnp.zeros_like(dv_ref)
            dg_ref[...] = jnp.zeros_like(dg_ref)

        gv = g_ref[...]
        for h in range(X_HEADS):
            cols, xhat, r, qn, p, l = _memx_scores(zq_ref, gv, kx_ref, h)
            p = p / l
            do = dy_ref[:, cols].astype(BF16)
            dv_ref[:, cols] += lax.dot_general(p.astype(BF16), do, _TN, preferred_element_type=F32)
            dp = lax.dot_general(do, vx_ref[:, cols], _NT, preferred_element_type=F32)
            ds = (p * (dp - jnp.sum(dp * p, axis=-1, keepdims=True)) * scale).astype(BF16)
            dqn = jnp.dot(ds, kx_ref[:, cols], preferred_element_type=F32)
            dk_ref[:, cols] += lax.dot_general(ds, qn, _TN, preferred_element_type=F32)
            dx, dg = _rms_bwd(dqn, xhat, r, gv, X_HEAD_DIM)
            dz_ref[:, cols] = dx.astype(BF16)
            dg_ref[...] += dg

    return pl.pallas_call(
        body, name="memx_bwd", grid=(S // tr,),
        in_specs=[_rows(tr, X_WIDTH, Z_MQ // X_WIDTH), _full((1, X_HEAD_DIM)), _full((M, X_WIDTH)),
                  _full((M, X_WIDTH)), _rows(tr, X_WIDTH, dy_cb)],
        out_specs=[_rows(tr, X_WIDTH), _full((M, X_WIDTH)), _full((M, X_WIDTH)), _full((1, X_HEAD_DIM))],
        out_shape=[jax.ShapeDtypeStruct((S, X_WIDTH), BF16), jax.ShapeDtypeStruct((M, X_WIDTH), F32),
                   jax.ShapeDtypeStruct((M, X_WIDTH), F32), jax.ShapeDtypeStruct((1, X_HEAD_DIM), F32)],
        compiler_params=_params("arbitrary"),
    )(z, g, kx, vx, dyc)


def _conv_gc(g_ext, w_ref, b_ref, n_ext):
    g1 = pltpu.roll(g_ext, 1, 0)
    g2 = pltpu.roll(g_ext, 2, 0)
    gc = b_ref[...] + w_ref[0:1, :] * g2
    gc = gc + w_ref[1:2, :] * g1
    gc = gc + w_ref[2:3, :] * g_ext
    return gc, g1, g2


def _convglu_fwd(g, u, cw, cb, tr=512, tc=512):
    S, F = g.shape
    tr, tc = _tile(S, tr), _tile(F, tc)
    rb = tr // CONV_HALO

    def body(g_ref, gp_ref, u_ref, w_ref, b_ref, a_ref):
        i = pl.program_id(1)
        halo = jnp.where(i == 0, 0.0, gp_ref[...].astype(F32))
        g_ext = jnp.concatenate([halo, g_ref[...].astype(F32)], axis=0)
        gc, _, _ = _conv_gc(g_ext, w_ref, b_ref, tr + CONV_HALO)
        gc = gc[CONV_HALO:]
        sig = 1.0 / (1.0 + jnp.exp(-gc))
        a_ref[...] = (gc * sig * u_ref[...].astype(F32)).astype(BF16)

    return pl.pallas_call(
        body, name="convglu_fwd", grid=(F // tc, S // tr),
        in_specs=[pl.BlockSpec((tr, tc), lambda j, i: (i, j)),
                  pl.BlockSpec((CONV_HALO, tc), lambda j, i: (jnp.maximum(i * rb - 1, 0), j)),
                  pl.BlockSpec((tr, tc), lambda j, i: (i, j)),
                  pl.BlockSpec((3, tc), lambda j, i: (0, j)), pl.BlockSpec((1, tc), lambda j, i: (0, j))],
        out_specs=pl.BlockSpec((tr, tc), lambda j, i: (i, j)),
        out_shape=jax.ShapeDtypeStruct((S, F), BF16),
        compiler_params=_params("parallel", "parallel"),
    )(g, g, u, cw, cb)


def _convglu_bwd(g, u, da, cw, cb, tr=512, tc=512):
    S, F = g.shape
    tr, tc = _tile(S, tr), _tile(F, tc)
    rb = tr // CONV_HALO
    nhb = S // CONV_HALO
    H = CONV_HALO

    def body(g_ref, gp_ref, gn_ref, u_ref, un_ref, da_ref, dan_ref, w_ref, b_ref,
             dg_ref, du_ref, dw_ref, db_ref):
        i = pl.program_id(1)
        last = i == pl.num_programs(1) - 1

        @pl.when(i == 0)
        def _():
            dw_ref[...] = jnp.zeros_like(dw_ref)
            db_ref[...] = jnp.zeros_like(db_ref)

        g_prev = jnp.where(i == 0, 0.0, gp_ref[...].astype(F32))
        g_cur = g_ref[...].astype(F32)
        g_ext = jnp.concatenate([g_prev, g_cur, gn_ref[...].astype(F32)], axis=0)
        gc, g1, g2 = _conv_gc(g_ext, w_ref, b_ref, tr + 2 * H)
        gc = gc[H:]
        ux = jnp.concatenate([u_ref[...], un_ref[...]], axis=0).astype(F32)
        dax = jnp.concatenate([da_ref[...].astype(F32), jnp.where(last, 0.0, dan_ref[...].astype(F32))], axis=0)
        sig = 1.0 / (1.0 + jnp.exp(-gc))
        du_ref[...] = (dax[:tr] * (gc[:tr] * sig[:tr])).astype(BF16)
        dgc = dax * ux * (sig * (1.0 + gc * (1.0 - sig)))
        n = tr + H
        d1 = pltpu.roll(dgc, n - 1, 0)[:tr]
        d2 = pltpu.roll(dgc, n - 2, 0)[:tr]
        d0 = dgc[:tr]
        dg_ref[...] = (w_ref[2:3, :] * d0 + w_ref[1:2, :] * d1 + w_ref[0:1, :] * d2).astype(BF16)
        db_ref[...] += jnp.sum(d0, axis=0, keepdims=True)
        dw_ref[0:1, :] += jnp.sum(d0 * g2[H:H + tr], axis=0, keepdims=True)
        dw_ref[1:2, :] += jnp.sum(d0 * g1[H:H + tr], axis=0, keepdims=True)
        dw_ref[2:3, :] += jnp.sum(d0 * g_cur, axis=0, keepdims=True)

    cur = pl.BlockSpec((tr, tc), lambda j, i: (i, j))
    prv = pl.BlockSpec((H, tc), lambda j, i: (jnp.maximum(i * rb - 1, 0), j))
    nxt = pl.BlockSpec((H, tc), lambda j, i: (jnp.minimum((i + 1) * rb, nhb - 1), j))
    return pl.pallas_call(
        body, name="convglu_bwd", grid=(F // tc, S // tr),
        in_specs=[cur, prv, nxt, cur, nxt, cur, nxt,
                  pl.BlockSpec((3, tc), lambda j, i: (0, j)), pl.BlockSpec((1, tc), lambda j, i: (0, j))],
        out_specs=[cur, cur, pl.BlockSpec((3, tc), lambda j, i: (0, j)), pl.BlockSpec((1, tc), lambda j, i: (0, j))],
        out_shape=[jax.ShapeDtypeStruct((S, F), BF16), jax.ShapeDtypeStruct((S, F), BF16),
                   jax.ShapeDtypeStruct((3, F), F32), jax.ShapeDtypeStruct((1, F), F32)],
        compiler_params=_params("parallel", "arbitrary"),
    )(g, g, g, u, u, da, da, cw, cb)


def _mm_down_loss(a, w, x1, target, tm=1024, tn=512):
    S, K = a.shape
    n = w.shape[1]
    tm, tn = _tile(S, tm), _tile(n, tn)

    def body(a_ref, w_ref, x_ref, t_ref, dy_ref, dyb_ref, loss_ref):
        @pl.when(jnp.logical_and(pl.program_id(0) == 0, pl.program_id(1) == 0))
        def _():
            loss_ref[...] = jnp.zeros_like(loss_ref)

        y = jnp.dot(a_ref[...], w_ref[...], preferred_element_type=F32) + x_ref[...]
        err = y - t_ref[...]
        dy = err / n
        dy_ref[...] = dy
        dyb_ref[...] = dy.astype(BF16)
        loss_ref[...] += 0.5 * jnp.sum(jnp.sum(err * err, axis=-1, keepdims=True) / n)

    tile = pl.BlockSpec((tm, tn), lambda i, j: (i, j))
    return pl.pallas_call(
        body, name="mm_down_loss", grid=(S // tm, n // tn),
        in_specs=[pl.BlockSpec((tm, K), lambda i, j: (i, 0)), pl.BlockSpec((K, tn), lambda i, j: (0, j)), tile, tile],
        out_specs=[tile, tile, pl.BlockSpec((8, LANES), lambda i, j: (0, 0))],
        out_shape=[jax.ShapeDtypeStruct((S, n), F32), jax.ShapeDtypeStruct((S, n), BF16),
                   jax.ShapeDtypeStruct((8, LANES), F32)],
        compiler_params=_params("arbitrary", "arbitrary"),
    )(a, w, x1, target)


def _mm_o_rms(ycat, w, x, g, tm=512):
    S, K = ycat.shape
    n = w.shape[1]
    tm = _tile(S, tm)

    def body(y_ref, w_ref, x_ref, g_ref, x1_ref, h2_ref):
        x1 = jnp.dot(y_ref[...], w_ref[...], preferred_element_type=F32) + x_ref[...]
        x1_ref[...] = x1
        xhat, _ = _rms_hat(x1, n)
        h2_ref[...] = (xhat * g_ref[...]).astype(BF16)

    return pl.pallas_call(
        body, name="mm_o_rms", grid=(S // tm,),
        in_specs=[_rows(tm, K), _full((K, n)), _rows(tm, n), _full((1, n))],
        out_specs=[_rows(tm, n), _rows(tm, n)],
        out_shape=[jax.ShapeDtypeStruct((S, n), F32), jax.ShapeDtypeStruct((S, n), BF16)],
        compiler_params=_params("parallel"),
    )(ycat, w, x, g)


def _mm_in_dx_rms(dz, w, x, g, extra, sends, tm=256):
    S, K = dz.shape
    n = w.shape[0]
    tm = _tile(S, tm)
    n_steps = S // tm
    nw = len(sends)

    def body(dz_ref, w_ref, x_ref, g_ref, ex_ref, *rest):
        s_refs, (dx_ref, dg_ref), r_refs = rest[:nw], rest[nw:nw + 2], rest[nw + 2:2 * nw + 2]
        scatter = _Scatter(s_refs, r_refs, *rest[2 * nw + 2:])
        i = pl.program_id(0)
        pl.when(i == 0)(scatter.start)

        @pl.when(i == 0)
        def _():
            dg_ref[...] = jnp.zeros_like(dg_ref)

        dh = lax.dot_general(dz_ref[...], w_ref[...], _NT, preferred_element_type=F32)
        xhat, r = _rms_hat(x_ref[...], n)
        dx, dg = _rms_bwd(dh, xhat, r, g_ref[...], n)
        dx_ref[...] = dx + ex_ref[...]
        dg_ref[...] += dg
        pl.when(i == n_steps - 1)(scatter.finish)

    return pl.pallas_call(
        body, name="mm_in_dx_rms", grid=(n_steps,),
        in_specs=[_rows(tm, K), _full((n, K)), _rows(tm, n), _full((1, n)), _rows(tm, n)] + [_ANY] * nw,
        out_specs=[_rows(tm, n), _full((1, n))] + [_ANY] * nw,
        out_shape=[jax.ShapeDtypeStruct((S, n), F32), jax.ShapeDtypeStruct((1, n), F32)]
        + [jax.ShapeDtypeStruct((N_DEV - 1,) + sd.shape[1:], sd.dtype) for sd in sends],
        scratch_shapes=[pltpu.SemaphoreType.DMA((nw, N_DEV - 1)), pltpu.SemaphoreType.DMA((nw, N_DEV - 1))],
        compiler_params=_params("arbitrary"),
    )(dz, w, x, g, extra, *sends)


def _local_step(x, mem, target, W, early_shards, late_shards):
    S = x.shape[0]
    tabs = _rope_tables(S)
    W = dict(W)
    G, Gb = {}, {}

    h, W["w_in"], W["w_q_up"], W["w_kv_up"] = _rms_fwd("rms1_fwd", x, W["g_mix"], early_shards, EARLY_GATHER_KINDS)
    z = _mm("mm_in", h, W["w_in"], tm=512, tn=Z_COLS, tk=D_MODEL)
    y_pool = _pool_fwd(z, W["w_pool"], W["pool_scale"])
    ql, kvl = _lat_fwd(z, W["g_q_lat"], W["g_kv_lat"])
    q_raw = _mm("mm_q_up", ql, W["w_q_up"], out_dtype=BF16, tm=1024, tn=2048, tk=Q_LORA)
    kv_raw = _mm("mm_kv_up", kvl, W["w_kv_up"], out_dtype=BF16, tm=1024, tn=2048, tk=KV_LORA)
    n_ffn = len(FFN_GATHER_KINDS)
    q, k, v, W["w_o"], W["w_mem_kv"] = _mla_prep(q_raw, kv_raw, z, tabs, W["g_q_mla"], W["g_k_mla"],
                                                 late_shards[n_ffn:], MIXER_GATHER_KINDS)
    o_mla, lse, wg3, wu3, W["w_down"], cw3 = _flash_fwd(q, k, v, late_shards[:n_ffn], FFN_GATHER_KINDS)
    W["w_gate"] = jnp.transpose(wg3, (1, 0, 2)).reshape(D_MODEL, D_FF)
    W["w_up"] = jnp.transpose(wu3, (1, 0, 2)).reshape(D_MODEL, D_FF)
    cw = jnp.sum(cw3.reshape(N_DEV, 3, 3, D_FF // N_DEV).astype(F32), axis=1)
    W["conv_w"] = jnp.transpose(cw, (1, 0, 2)).reshape(3, D_FF)
    mn = _mem_norm(mem, W["g_mem"])
    mkv = _mm("mm_mem_kv", mn, W["w_mem_kv"], tm=256, tn=1024, tk=D_MODEL)
    kx, vx = _memkv_prep(mkv, W["g_k_x"])
    y_mem = _memx_fwd(z, W["g_q_x"], kx, vx)
    ycat = jnp.concatenate([y_pool, o_mla, y_mem], axis=1)
    x1, h2 = _mm_o_rms(ycat, W["w_o"], x, W["g_ffn"])
    g = _mm("mm_gate", h2, W["w_gate"], out_dtype=BF16, tm=1024, tn=1408, tk=D_MODEL)
    u = _mm("mm_up", h2, W["w_up"], out_dtype=BF16, tm=1024, tn=1408, tk=D_MODEL)
    a = _convglu_fwd(g, u, W["conv_w"], W["conv_b"])
    dy, dyb, loss_part = _mm_down_loss(a, W["w_down"], x1, target)

    da = _mm("mm_down_dx", dyb, W["w_down"], tb=True, out_dtype=BF16, tm=1024, tn=1408, tk=D_MODEL)
    G["w_down"], Gb["w_down"] = _mm("mm_down_dw", a, dyb, ta=True, also_bf16=True, tm=512, tn=1024, tk=4096)
    dg, du, G["conv_w"], G["conv_b"] = _convglu_bwd(g, u, da, W["conv_w"], W["conv_b"])
    dh2 = _mm("mm_gate_dx", dg, W["w_gate"], tb=True, tm=1024, tn=512, tk=D_FF)
    G["w_gate"] = _mm("mm_gate_dw", h2, dg, ta=True, tm=512, tn=1408, tk=4096)
    G["w_up"] = _mm("mm_up_dw", h2, du, ta=True, tm=512, tn=1408, tk=4096)
    dx1, dx1b, G["g_ffn"] = _mm_up_dx_rms(du, W["w_up"], dh2, x1, W["g_ffn"], dy)

    dyc = _mm("mm_o_dx", dx1b, W["w_o"], tb=True, tm=512, tn=2048, tk=D_MODEL)
    G["w_o"], Gb["w_o"] = _mm("mm_o_dw", ycat, dx1b, ta=True, also_bf16=True, tm=1024, tn=1024, tk=2048)
    dz_pool, G["w_pool"], G["pool_scale"] = _pool_bwd(z, dyc, W["w_pool"], W["pool_scale"], dy_cb=0)
    dz_mq, dkx, dvx, G["g_q_x"] = _memx_bwd(z, W["g_q_x"], kx, vx, dyc, dy_cb=3)
    dmkv, G["g_k_x"] = _memkv_bwd(dkx, dvx, mkv, W["g_k_x"])
    G["w_mem_kv"], Gb["w_mem_kv"] = _mm("mm_mem_kv_dw", mn, dmkv, ta=True, also_bf16=True, tm=1024, tn=1024,
                                        tk=MEM_LEN)
    dmn = _mm("mm_mem_kv_dx", dmkv, W["w_mem_kv"], tb=True, tm=256, tn=2048, tk=1024)
    G["g_mem"] = _mem_gain_bwd(mem, dmn)
    sends = [_send_blocks(n, Gb.get(n, G[n])) for n in SCATTER_EARLY]
    dq, dk, dv, *got_early = _flash_bwd(q, k, v, dyc, POOL_WIDTH // MLA_V, o_mla, lse, sends)
    dq_raw, dkv_raw, dz_kr, G["g_q_mla"], G["g_k_mla"] = _mla_prep_bwd(
        dq, dk, dv, q_raw, kv_raw, z, tabs, W["g_q_mla"], W["g_k_mla"])
    G["w_q_up"] = _mm("mm_q_up_dw", ql, dq_raw, ta=True, tm=512, tn=2048, tk=1024)
    dql = _mm("mm_q_up_dx", dq_raw, W["w_q_up"], tb=True, tm=1024, tn=512, tk=2048)
    G["w_kv_up"] = _mm("mm_kv_up_dw", kvl, dkv_raw, ta=True, tm=256, tn=2048, tk=1024)
    dkvl = _mm("mm_kv_up_dx", dkv_raw, W["w_kv_up"], tb=True, tm=1024, tn=256, tk=2048)
    dz_q, dz_kv, G["g_q_lat"], G["g_kv_lat"] = _lat_bwd(z, W["g_q_lat"], W["g_kv_lat"], dql, dkvl)
    dz = jnp.concatenate([dz_pool, dz_q, dz_mq, dz_kv, dz_kr], axis=1)
    G["w_in"] = _mm("mm_in_dw", h, dz, ta=True, tm=1024, tn=Z_COLS, tk=1024)
    grad_x, G["g_mix"], *got_late = _mm_in_dx_rms(dz, W["w_in"], x, W["g_mix"], dx1,
                                                  [_send_blocks(n, G[n]) for n in SCATTER_LATE])
    return loss_part, grad_x, G, dict(zip(SCATTER_EARLY + SCATTER_LATE, got_early + got_late))


_ANY = pl.BlockSpec(memory_space=pl.ANY)
_MESH = pl.DeviceIdType.MESH


def _gather_out_shape(kind, shape):
    if kind == "rows":
        return (N_DEV * shape[0],) + tuple(shape[1:])
    if kind == "cols":
        return (shape[0], N_DEV * shape[1])
    return (N_DEV,) + tuple(shape)


def _gather_view(ref, kind, shape, d):
    if kind == "rows":
        return ref.at[pl.ds(pl.multiple_of(d * shape[0], 16), shape[0]), :]
    if kind == "cols":
        return ref.at[:, pl.ds(pl.multiple_of(d * shape[1], math.gcd(shape[1], LANES)), shape[1])]
    return ref.at[d]


class _Gather:
    def __init__(self, x_refs, out_refs, kinds, send_sems, recv_sems, local_sems):
        self.xr, self.outr, self.kinds = x_refs, out_refs, kinds
        self.ss, self.rs, self.ls = send_sems, recv_sems, local_sems
        x, y, c = lax.axis_index("x"), lax.axis_index("y"), lax.axis_index("c")
        self.c = c
        self.me, self.sibling = (x, y, c), (x, y, 1 - c)
        self.chips = [(1 - x, y), (x, 1 - y), (1 - x, 1 - y)]

    def _view(self, w, dev):
        px, py, pc = dev
        return _gather_view(self.outr[w], self.kinds[w], self.xr[w].shape, 4 * px + 2 * py + pc)

    def _copy(self, w, k, block, to, from_shard=False):
        v = self._view(w, block)
        return pltpu.make_async_remote_copy(
            src_ref=self.xr[w] if from_shard else v, dst_ref=v, send_sem=self.ss.at[w, k],
            recv_sem=self.rs.at[w, k], device_id=to, device_id_type=_MESH)

    def _local(self, w):
        return pltpu.make_async_copy(self.xr[w], self._view(w, self.me), self.ls.at[w])

    def start(self):
        for w in range(len(self.xr)):
            self._local(w).start()
            self._copy(w, 0, self.me, self.sibling, True).start()
            for j, chip in enumerate(self.chips):
                self._copy(w, 1 + j, self.me, (*chip, self.c), True).start()

    def forward(self):
        for j, chip in enumerate(self.chips):
            for w in range(len(self.xr)):
                self._copy(w, 1 + j, (*chip, self.c), self.me).wait_recv()
                self._copy(w, 4 + j, (*chip, self.c), self.sibling).start()

    def finish(self):
        for w in range(len(self.xr)):
            self._copy(w, 0, self.sibling, self.me).wait_recv()
            for j, chip in enumerate(self.chips):
                self._copy(w, 4 + j, (*chip, 1 - self.c), self.me).wait_recv()
            self._copy(w, 0, self.me, self.sibling, True).wait_send()
            for j, chip in enumerate(self.chips):
                self._copy(w, 1 + j, self.me, (*chip, self.c), True).wait_send()
                self._copy(w, 4 + j, (*chip, self.c), self.sibling).wait_send()
            self._local(w).wait()


class _Scatter:
    def __init__(self, send_refs, recv_refs, send_sems, recv_sems):
        self.sr, self.rr, self.ss, self.rs = send_refs, recv_refs, send_sems, recv_sems
        self.xyz = lax.axis_index("x"), lax.axis_index("y"), lax.axis_index("c")

    def _copy(self, w, k):
        x, y, c = self.xyz
        px, py, pc = x ^ ((k >> 2) & 1), y ^ ((k >> 1) & 1), c ^ (k & 1)
        return pltpu.make_async_remote_copy(
            src_ref=self.sr[w].at[4 * px + 2 * py + pc], dst_ref=self.rr[w].at[k - 1],
            send_sem=self.ss.at[w, k - 1], recv_sem=self.rs.at[w, k - 1],
            device_id=(px, py, pc), device_id_type=_MESH)

    def _all(self):
        return [self._copy(w, k) for w in range(len(self.sr)) for k in range(1, N_DEV)]

    def start(self):
        for cp in self._all():
            cp.start()

    def finish(self):
        for cp in self._all():
            cp.wait_recv()
        for cp in self._all():
            cp.wait_send()


def _exchange_small(s):
    def body(x_ref, out_ref, x_send, x_recv, local_sem):
        x, y, c = lax.axis_index("x"), lax.axis_index("y"), lax.axis_index("c")
        me = 4 * x + 2 * y + c
        mine = pltpu.make_async_copy(x_ref, out_ref.at[me], local_sem)
        mine.start()

        def copy(k):
            px, py, pc = x ^ ((k >> 2) & 1), y ^ ((k >> 1) & 1), c ^ (k & 1)
            return pltpu.make_async_remote_copy(
                src_ref=x_ref, dst_ref=out_ref.at[me], send_sem=x_send.at[k - 1], recv_sem=x_recv.at[k - 1],
                device_id=(px, py, pc), device_id_type=_MESH)

        cps = [copy(k) for k in range(1, N_DEV)]
        for cp in cps:
            cp.start()
        for cp in cps:
            cp.wait_recv()
        for cp in cps:
            cp.wait_send()
        mine.wait()

    return pl.pallas_call(
        body, name="exchange_small", out_shape=jax.ShapeDtypeStruct((N_DEV,) + s.shape, s.dtype),
        in_specs=[_ANY], out_specs=_ANY,
        scratch_shapes=[pltpu.SemaphoreType.DMA((N_DEV - 1,)), pltpu.SemaphoreType.DMA((N_DEV - 1,)),
                        pltpu.SemaphoreType.DMA],
    )(s)


def _sum_slots(buf, tr=208):
    n, R, _ = buf.shape
    tr = tr if R % tr == 0 else R

    def body(b_ref, o_ref):
        acc = b_ref[0]
        for d in range(1, n):
            acc = acc + b_ref[d]
        o_ref[...] = acc

    return pl.pallas_call(
        body, name="sum_slots", grid=(R // tr,),
        in_specs=[pl.BlockSpec((n, tr, LANES), lambda i: (0, i, 0))],
        out_specs=pl.BlockSpec((tr, LANES), lambda i: (i, 0)),
        out_shape=jax.ShapeDtypeStruct((R, LANES), buf.dtype),
        compiler_params=_params("parallel"),
    )(buf)


def _adamw_math(w, g, m, v):
    m_new = ADAM_B1 * m + (1.0 - ADAM_B1) * g
    v_new = ADAM_B2 * v + (1.0 - ADAM_B2) * (g * g)
    m_hat = m_new / (1.0 - ADAM_B1 ** ADAM_STEP)
    v_hat = v_new / (1.0 - ADAM_B2 ** ADAM_STEP)
    return -ADAM_LR * (m_hat / (jnp.sqrt(v_hat) + ADAM_EPS) + ADAM_WD * w), m_new, v_new


def _adamw(name, w, g, got, m, v, tr=128):
    R, C = w.shape
    tr = max([t for t in range(16, tr + 1, 16) if R % t == 0], default=R) if R > tr else R

    def body(w_ref, g_ref, got_ref, m_ref, v_ref, go_ref, d_ref, mo_ref, vo_ref):
        gv = g_ref[...]
        for k in range(N_DEV - 1):
            gv = gv + got_ref[k].astype(F32)
        go_ref[...] = gv
        d_ref[...], mo_ref[...], vo_ref[...] = _adamw_math(w_ref[...], gv, m_ref[...], v_ref[...])

    spec = pl.BlockSpec((tr, C), lambda i: (i, 0))
    got_spec = pl.BlockSpec((N_DEV - 1, tr, C), lambda i: (0, i, 0))
    sds = jax.ShapeDtypeStruct((R, C), F32)
    return pl.pallas_call(
        body, name=name, grid=(R // tr,), in_specs=[spec, spec, got_spec, spec, spec], out_specs=[spec] * 4,
        out_shape=[sds] * 4, compiler_params=_params("parallel"),
    )(w, g, got, m, v)


def _adamw_small(ws, gs, ms, vs):
    n = len(ws)

    def body(*refs):
        ins, outs = refs[:4 * n], refs[4 * n:]
        for p in range(n):
            w_ref, g_ref, m_ref, v_ref = (ins[q * n + p] for q in range(4))
            res = _adamw_math(w_ref[...], g_ref[...], m_ref[...], v_ref[...])
            for q in range(3):
                outs[q * n + p][...] = res[q]

    sds = [jax.ShapeDtypeStruct(w.shape, F32) for w in ws]
    outs = pl.pallas_call(body, name="adamw_small", out_shape=sds * 3)(*ws, *gs, *ms, *vs)
    return outs[:n], outs[n:2 * n], outs[2 * n:]


def _pad_rows(a, mult):
    r = (-a.shape[0]) % mult
    return a if r == 0 else jnp.concatenate([a, jnp.zeros((r,) + a.shape[1:], a.dtype)], axis=0)


def _as_rows(a, mult):
    flat = a.reshape(-1)
    r = (-flat.shape[0]) % LANES
    if r:
        flat = jnp.concatenate([flat, jnp.zeros((r,), a.dtype)])
    return _pad_rows(flat.reshape(-1, LANES), mult)


def _w_in_to_kernel_cols(w):
    pad = jnp.zeros(w.shape[:-1] + (Z_COLS - IN_COLS,), w.dtype)
    return jnp.concatenate([w[..., :1024], w[..., 1344:1856], w[..., 1024:1344], pad], axis=-1)


def _w_in_from_kernel_cols(w):
    return jnp.concatenate([w[..., :1024], w[..., 1536:1856], w[..., 1024:1536]], axis=-1)


def _dev_blocks(n, g):
    ffs = D_FF // N_DEV
    if n == "w_in":
        return _w_in_from_kernel_cols(g).reshape(N_DEV, D_MODEL // N_DEV, IN_COLS)
    if n == "w_q_up":
        return jnp.transpose(g.reshape(Q_LORA, N_DEV, MLA_QK_PAD)[:, :, :MLA_QK], (1, 0, 2))
    if n == "w_kv_up":
        return jnp.transpose(g.reshape(KV_LORA, N_DEV, MLA_NOPE + MLA_V), (1, 0, 2))
    if n in ("w_mem_kv", "w_o"):
        return g.reshape(N_DEV, D_MODEL // N_DEV, g.shape[1])
    if n in ("w_gate", "w_up"):
        return jnp.transpose(g.reshape(D_MODEL, N_DEV, ffs), (1, 0, 2))
    if n == "conv_w":
        return jnp.transpose(g.reshape(3, N_DEV, ffs), (1, 0, 2))
    assert n == "w_down"
    return g.reshape(N_DEV, ffs, D_MODEL)


def _send_blocks(n, g):
    blocks = _dev_blocks(n, g)
    return blocks if n == "conv_w" else blocks.astype(BF16)


def kernel(x, mem, g_mix, w_in, g_q_lat, w_q_up, g_kv_lat, w_kv_up, g_q_mla, g_k_mla, w_pool, pool_scale, g_mem, w_mem_kv, g_q_x, g_k_x, w_o, g_ffn, w_gate, w_up, conv_w, conv_b, w_down, loss_target, m_g_mix, m_w_in, m_g_q_lat, m_w_q_up, m_g_kv_lat, m_w_kv_up, m_g_q_mla, m_g_k_mla, m_w_pool, m_pool_scale, m_g_mem, m_w_mem_kv, m_g_q_x, m_g_k_x, m_w_o, m_g_ffn, m_w_gate, m_w_up, m_conv_w, m_conv_b, m_w_down, v_g_mix, v_w_in, v_g_q_lat, v_w_q_up, v_g_kv_lat, v_w_kv_up, v_g_q_mla, v_g_k_mla, v_w_pool, v_pool_scale, v_g_mem, v_w_mem_kv, v_g_q_x, v_g_k_x, v_w_o, v_g_ffn, v_w_gate, v_w_up, v_conv_w, v_conv_b, v_w_down):
    given = dict(g_mix=g_mix, w_in=w_in, g_q_lat=g_q_lat, w_q_up=w_q_up, g_kv_lat=g_kv_lat, w_kv_up=w_kv_up,
                 g_q_mla=g_q_mla, g_k_mla=g_k_mla, w_pool=w_pool, pool_scale=pool_scale, g_mem=g_mem,
                 w_mem_kv=w_mem_kv, g_q_x=g_q_x, g_k_x=g_k_x, w_o=w_o, g_ffn=g_ffn, w_gate=w_gate, w_up=w_up,
                 conv_w=conv_w, conv_b=conv_b, w_down=w_down)
    mom_m = dict(g_mix=m_g_mix, w_in=m_w_in, g_q_lat=m_g_q_lat, w_q_up=m_w_q_up, g_kv_lat=m_g_kv_lat,
                 w_kv_up=m_w_kv_up, g_q_mla=m_g_q_mla, g_k_mla=m_g_k_mla, w_pool=m_w_pool,
                 pool_scale=m_pool_scale, g_mem=m_g_mem, w_mem_kv=m_w_mem_kv, g_q_x=m_g_q_x, g_k_x=m_g_k_x,
                 w_o=m_w_o, g_ffn=m_g_ffn, w_gate=m_w_gate, w_up=m_w_up, conv_w=m_conv_w, conv_b=m_conv_b,
                 w_down=m_w_down)
    mom_v = dict(g_mix=v_g_mix, w_in=v_w_in, g_q_lat=v_g_q_lat, w_q_up=v_w_q_up, g_kv_lat=v_g_kv_lat,
                 w_kv_up=v_w_kv_up, g_q_mla=v_g_q_mla, g_k_mla=v_g_k_mla, w_pool=v_w_pool,
                 pool_scale=v_pool_scale, g_mem=v_g_mem, w_mem_kv=v_w_mem_kv, g_q_x=v_g_q_x, g_k_x=v_g_k_x,
                 w_o=v_w_o, g_ffn=v_g_ffn, w_gate=v_w_gate, w_up=v_w_up, conv_w=v_conv_w, conv_b=v_conv_b,
                 w_down=v_w_down)
    drop = lambda a: a[0] if a.ndim > 2 else a
    sh = {n: drop(given[n]) for n in WEIGHTS}
    mom_m = {n: drop(mom_m[n]) for n in WEIGHTS}
    mom_v = {n: drop(mom_v[n]) for n in WEIGHTS}

    cw_hi = sh["conv_w"].astype(BF16)
    cw_r = sh["conv_w"] - cw_hi.astype(F32)
    cw_mid = cw_r.astype(BF16)
    cw_lo = (cw_r - cw_mid.astype(F32)).astype(BF16)
    early_shards = [
        _w_in_to_kernel_cols(sh["w_in"]).astype(BF16),
        jnp.pad(sh["w_q_up"], ((0, 0), (0, MLA_QK_PAD - MLA_QK))).astype(BF16),
        sh["w_kv_up"].astype(BF16),
    ]
    late_shards = [sh["w_gate"].astype(BF16), sh["w_up"].astype(BF16), sh["w_down"].astype(BF16),
                   jnp.concatenate([cw_hi, cw_mid, cw_lo], axis=0), sh["w_o"].astype(BF16),
                   sh["w_mem_kv"].astype(BF16)]
    W = {"w_pool": sh["w_pool"].astype(BF16)}
    for n in ("g_mix", "g_q_lat", "g_kv_lat", "pool_scale", "g_mem", "g_q_x", "g_k_x", "g_ffn", "conv_b"):
        W[n] = sh[n]
    pad_qk = lambda gv: jnp.pad(gv, ((0, 0), (0, MLA_QK_PAD - MLA_QK)))
    W["g_q_mla"], W["g_k_mla"] = pad_qk(sh["g_q_mla"]), pad_qk(sh["g_k_mla"])

    loss_part, grad_x, G, got = _local_step(x[0], mem[0], loss_target[0], W, early_shards, late_shards)

    small = {
        "g_mix": G["g_mix"], "g_q_lat": G["g_q_lat"], "g_kv_lat": G["g_kv_lat"],
        "g_q_mla": G["g_q_mla"][:, :MLA_QK], "g_k_mla": G["g_k_mla"][:, :MLA_QK],
        "w_pool": G["w_pool"], "pool_scale": G["pool_scale"], "g_mem": G["g_mem"],
        "g_q_x": G["g_q_x"], "g_k_x": G["g_k_x"], "g_ffn": G["g_ffn"], "conv_b": G["conv_b"],
    }
    s_offs = {}
    segs = []
    off = 0
    for n in SMALL:
        r = _as_rows(small[n], 8)
        s_offs[n] = off
        off += r.shape[0]
        segs.append(r)
    segs.append(loss_part)
    loss_row = off
    sbuf = jnp.concatenate(segs, axis=0)
    s_sum = _sum_slots(_exchange_small(sbuf))

    def small_take(buf, n):
        shape = sh[n].shape
        cnt = math.prod(shape)
        return buf[s_offs[n]:s_offs[n] + -(-cnt // LANES)].reshape(-1)[:cnt].reshape(shape)

    loss = s_sum[loss_row, 0]

    me = 4 * lax.axis_index("x") + 2 * lax.axis_index("y") + lax.axis_index("c")
    grads, deltas, new_m, new_v = {}, {}, {}, {}
    for n in BIG:
        shape = sh[n].shape
        own = lax.dynamic_index_in_dim(_dev_blocks(n, G[n]), me, 0, keepdims=False)
        flat = lambda a: a.reshape(-1, shape[-1])
        outs = _adamw("adamw_" + n, flat(sh[n]), flat(own), got[n].reshape(N_DEV - 1, -1, shape[-1]),
                      flat(mom_m[n]), flat(mom_v[n]))
        grads[n], deltas[n], new_m[n], new_v[n] = (o.reshape(shape) for o in outs)

    for n in SMALL:
        grads[n] = small_take(s_sum, n)
    d_s, m_s, v_s = _adamw_small([sh[n] for n in SMALL], [grads[n] for n in SMALL],
                                 [mom_m[n] for n in SMALL], [mom_v[n] for n in SMALL])
    for j, n in enumerate(SMALL):
        deltas[n], new_m[n], new_v[n] = d_s[j], m_s[j], v_s[j]

    lead = lambda n, a: a.reshape(given[n].shape)
    return (loss, grad_x[None],
            *[lead(n, grads[n]) for n in WEIGHTS], *[lead(n, deltas[n]) for n in WEIGHTS],
            *[lead(n, new_m[n]) for n in WEIGHTS], *[lead(n, new_v[n]) for n in WEIGHTS])
```

```python
import math

import jax
import jax.numpy as jnp
from jax import lax
from jax.experimental import pallas as pl
from jax.experimental.pallas import tpu as pltpu

F32 = jnp.float32
BF16 = jnp.bfloat16

D_MODEL = 2048
D_FF = 5632
POOL_WIDTH = 512
POOL_WINDOWS = (2, 4, 8, 16)
POOL_HALO = 16
MLA_HEADS = 8
MLA_NOPE = 128
MLA_ROPE = 64
MLA_QK = MLA_NOPE + MLA_ROPE
MLA_QK_PAD = 256
MLA_V = 128
Q_LORA = 512
KV_LORA = 256
X_HEADS = 4
X_HEAD_DIM = 128
X_WIDTH = 512
MEM_LEN = 256
ROPE_THETA = 10000.0
NORM_EPS = 1e-6
CONV_HALO = 16
IN_COLS = 1856
Z_COLS = 1920
Z_POOL, Z_Q, Z_MQ, Z_KV, Z_KR = 0, 512, 1024, 1536, 1792

ADAM_LR = 0.001
ADAM_B1 = 0.9
ADAM_B2 = 0.999
ADAM_EPS = 1e-08
ADAM_WD = 0.01
ADAM_STEP = 10

N_DEV = 8
LANES = 128
VMEM_LIMIT = 56 * 1024 * 1024

BIG = ("w_in", "w_q_up", "w_kv_up", "w_mem_kv", "w_o", "w_gate", "w_up", "conv_w", "w_down")
SCATTER_EARLY = ("w_down", "w_gate", "w_up", "conv_w", "w_o", "w_mem_kv")
SCATTER_LATE = ("w_in", "w_q_up", "w_kv_up")
EARLY_GATHER_KINDS = ("rows", "cols", "cols")
FFN_GATHER_KINDS = ("lead", "lead", "rows", "lead")
MIXER_GATHER_KINDS = ("rows", "rows")
SMALL = ("g_mix", "g_q_lat", "g_kv_lat", "g_q_mla", "g_k_mla", "w_pool", "pool_scale", "g_mem",
         "g_q_x", "g_k_x", "g_ffn", "conv_b")
WEIGHTS = ("g_mix", "w_in", "g_q_lat", "w_q_up", "g_kv_lat", "w_kv_up", "g_q_mla", "g_k_mla", "w_pool",
           "pool_scale", "g_mem", "w_mem_kv", "g_q_x", "g_k_x", "w_o", "g_ffn", "w_gate", "w_up", "conv_w",
           "conv_b", "w_down")


def _tile(n, t):
    if n <= t:
        return n
    for c in range(t - t % LANES, 0, -LANES):
        if n % c == 0:
            return c
    return n


def _params(*sem):
    return pltpu.CompilerParams(dimension_semantics=sem, vmem_limit_bytes=VMEM_LIMIT)


def _full(shape):
    nd = len(shape)
    return pl.BlockSpec(shape, lambda *_: (0,) * nd)


def _rows(tr, w, cb=0):
    return pl.BlockSpec((tr, w), lambda i: (i, cb))


def _rms_hat(x, n):
    r = lax.rsqrt(jnp.sum(x * x, axis=-1, keepdims=True) / n + NORM_EPS)
    return x * r, r


def _rms_bwd(dy, xhat, r, g, n):
    dxh = dy * g
    dx = r * (dxh - xhat * (jnp.sum(dxh * xhat, axis=-1, keepdims=True) / n))
    return dx, jnp.sum(dy * xhat, axis=0, keepdims=True)


def _mm(name, a, b, *, ta=False, tb=False, add=None, out_dtype=F32, also_bf16=False, tm=512, tn=512, tk=512):
    if ta:
        K, M = a.shape
    else:
        M, K = a.shape
    if tb:
        N, K2 = b.shape
    else:
        K2, N = b.shape
    assert K == K2, (name, a.shape, b.shape)
    tm, tn, tk = _tile(M, tm), _tile(N, tn), _tile(K, tk)
    nk = K // tk
    grid = (M // tm, N // tn, nk)
    a_spec = pl.BlockSpec((tk, tm), lambda i, j, k: (k, i)) if ta else pl.BlockSpec((tm, tk), lambda i, j, k: (i, k))
    b_spec = pl.BlockSpec((tn, tk), lambda i, j, k: (j, k)) if tb else pl.BlockSpec((tk, tn), lambda i, j, k: (k, j))
    o_spec = pl.BlockSpec((tm, tn), lambda i, j, k: (i, j))
    dims = (((0,) if ta else (1,), (1,) if tb else (0,)), ((), ()))
    has_add = add is not None

    n_in = 3 if has_add else 2
    n_out = 2 if also_bf16 else 1

    def body(*refs):
        a_ref, b_ref = refs[:2]
        add_ref = refs[2] if has_add else None
        o_refs = refs[n_in:n_in + n_out]

        def write(r):
            if has_add:
                r = r + add_ref[...]
            for o_ref in o_refs:
                o_ref[...] = r.astype(o_ref.dtype)

        part = lax.dot_general(a_ref[...], b_ref[...], dims, preferred_element_type=F32)
        if nk == 1:
            write(part)
            return
        acc = refs[-1]
        k = pl.program_id(2)

        @pl.when(k == 0)
        def _():
            acc[...] = part

        @pl.when(k > 0)
        def _():
            acc[...] += part

        @pl.when(k == nk - 1)
        def _():
            write(acc[...])

    ins = [a, b] + ([add] if has_add else [])
    in_specs = [a_spec, b_spec] + ([o_spec] if has_add else [])
    out_shape = [jax.ShapeDtypeStruct((M, N), out_dtype)] + ([jax.ShapeDtypeStruct((M, N), BF16)] if also_bf16 else [])
    outs = pl.pallas_call(
        body, name=name, grid=grid, in_specs=in_specs, out_specs=[o_spec] * n_out, out_shape=out_shape,
        scratch_shapes=[pltpu.VMEM((tm, tn), F32)] if nk > 1 else [],
        compiler_params=_params("parallel", "parallel", "arbitrary"),
    )(*ins)
    return outs if also_bf16 else outs[0]


def _rms_fwd(name, x, g, shards, kinds, tr=512):
    S, n = x.shape
    tr = _tile(S, tr)
    n_steps = S // tr
    nw = len(shards)

    def body(x_ref, g_ref, *rest):
        x_refs, h_ref, g_refs = rest[:nw], rest[nw], rest[nw + 1:2 * nw + 1]
        gather = _Gather(x_refs, g_refs, kinds, *rest[2 * nw + 1:])
        i = pl.program_id(0)
        pl.when(i == 0)(gather.start)
        pl.when(i == n_steps // 2)(gather.forward)
        xhat, _ = _rms_hat(x_ref[...], n)
        h_ref[...] = (xhat * g_ref[...]).astype(BF16)
        pl.when(i == n_steps - 1)(gather.finish)

    return pl.pallas_call(
        body, name=name, grid=(n_steps,),
        in_specs=[_rows(tr, n), _full((1, n))] + [_ANY] * nw, out_specs=[_rows(tr, n)] + [_ANY] * nw,
        out_shape=[jax.ShapeDtypeStruct((S, n), BF16)]
        + [jax.ShapeDtypeStruct(_gather_out_shape(kd, sd.shape), sd.dtype) for kd, sd in zip(kinds, shards)],
        scratch_shapes=[pltpu.SemaphoreType.DMA((nw, 7)), pltpu.SemaphoreType.DMA((nw, 7)),
                        pltpu.SemaphoreType.DMA((nw,))],
        compiler_params=_params("arbitrary"),
    )(x, g, *shards)


def _mm_up_dx_rms(du, w, part, x, g, extra, tm=256):
    S, K = du.shape
    n = w.shape[0]
    tm = _tile(S, tm)

    def body(du_ref, w_ref, p_ref, x_ref, g_ref, ex_ref, dx_ref, dxb_ref, dg_ref):
        @pl.when(pl.program_id(0) == 0)
        def _():
            dg_ref[...] = jnp.zeros_like(dg_ref)

        dh = lax.dot_general(du_ref[...], w_ref[...], _NT, preferred_element_type=F32) + p_ref[...]
        xhat, r = _rms_hat(x_ref[...], n)
        dx, dg = _rms_bwd(dh, xhat, r, g_ref[...], n)
        dx = dx + ex_ref[...]
        dx_ref[...] = dx
        dxb_ref[...] = dx.astype(BF16)
        dg_ref[...] += dg

    return pl.pallas_call(
        body, name="mm_up_dx_rms", grid=(S // tm,),
        in_specs=[_rows(tm, K), pl.BlockSpec((n, K), lambda i: (0, 0), pipeline_mode=pl.Buffered(1)),
                  _rows(tm, n), _rows(tm, n), _full((1, n)), _rows(tm, n)],
        out_specs=[_rows(tm, n), _rows(tm, n), _full((1, n))],
        out_shape=[jax.ShapeDtypeStruct((S, n), F32), jax.ShapeDtypeStruct((S, n), BF16),
                   jax.ShapeDtypeStruct((1, n), F32)],
        compiler_params=_params("arbitrary"),
    )(du, w, part, x, g, extra)


def _lat_fwd(z, gq, gkv, tr=512):
    S = z.shape[0]
    tr = _tile(S, tr)

    def body(zq_ref, zkv_ref, gq_ref, gkv_ref, ql_ref, kvl_ref):
        xq, _ = _rms_hat(zq_ref[...], Q_LORA)
        ql_ref[...] = (xq * gq_ref[...]).astype(BF16)
        xkv, _ = _rms_hat(zkv_ref[...], KV_LORA)
        kvl_ref[...] = (xkv * gkv_ref[...]).astype(BF16)

    return pl.pallas_call(
        body, name="lat_fwd", grid=(S // tr,),
        in_specs=[_rows(tr, Q_LORA, Z_Q // Q_LORA), _rows(tr, KV_LORA, Z_KV // KV_LORA),
                  _full((1, Q_LORA)), _full((1, KV_LORA))],
        out_specs=[_rows(tr, Q_LORA), _rows(tr, KV_LORA)],
        out_shape=[jax.ShapeDtypeStruct((S, Q_LORA), BF16), jax.ShapeDtypeStruct((S, KV_LORA), BF16)],
        compiler_params=_params("parallel"),
    )(z, z, gq, gkv)


def _lat_bwd(z, gq, gkv, dql, dkvl, tr=512):
    S = z.shape[0]
    tr = _tile(S, tr)

    def body(zq_ref, zkv_ref, gq_ref, gkv_ref, dql_ref, dkvl_ref, dzq_ref, dzkv_ref, dgq_ref, dgkv_ref):
        @pl.when(pl.program_id(0) == 0)
        def _():
            dgq_ref[...] = jnp.zeros_like(dgq_ref)
            dgkv_ref[...] = jnp.zeros_like(dgkv_ref)

        xq, rq = _rms_hat(zq_ref[...], Q_LORA)
        dx, dg = _rms_bwd(dql_ref[...], xq, rq, gq_ref[...], Q_LORA)
        dzq_ref[...] = dx.astype(BF16)
        dgq_ref[...] += dg
        xkv, rkv = _rms_hat(zkv_ref[...], KV_LORA)
        dx, dg = _rms_bwd(dkvl_ref[...], xkv, rkv, gkv_ref[...], KV_LORA)
        dzkv_ref[...] = dx.astype(BF16)
        dgkv_ref[...] += dg

    return pl.pallas_call(
        body, name="lat_bwd", grid=(S // tr,),
        in_specs=[_rows(tr, Q_LORA, Z_Q // Q_LORA), _rows(tr, KV_LORA, Z_KV // KV_LORA),
                  _full((1, Q_LORA)), _full((1, KV_LORA)), _rows(tr, Q_LORA), _rows(tr, KV_LORA)],
        out_specs=[_rows(tr, Q_LORA), _rows(tr, KV_LORA), _full((1, Q_LORA)), _full((1, KV_LORA))],
        out_shape=[jax.ShapeDtypeStruct((S, Q_LORA), BF16), jax.ShapeDtypeStruct((S, KV_LORA), BF16),
                   jax.ShapeDtypeStruct((1, Q_LORA), F32), jax.ShapeDtypeStruct((1, KV_LORA), F32)],
        compiler_params=_params("arbitrary"),
    )(z, z, gq, gkv, dql, dkvl)


def _pool_d(ext, cur, t0, tr):
    t = t0 + lax.broadcasted_iota(jnp.int32, (tr, 1), 0)
    ds = []
    for gi, w in enumerate(POOL_WINDOWS):
        s = ext[:, gi * LANES:(gi + 1) * LANES]
        sh = 1
        while sh < w:
            s = s + pltpu.roll(s, sh, 0)
            sh *= 2
        cnt = jnp.minimum(t + 1, w).astype(F32)
        ds.append(s[POOL_HALO:] / cnt - cur[:, gi * LANES:(gi + 1) * LANES])
    return ds


def _pool_fwd(z, wp, scale, tr=512):
    S = z.shape[0]
    tr = _tile(S, tr)
    rb = tr // POOL_HALO

    def body(z_ref, zp_ref, wp_ref, sc_ref, y_ref):
        i = pl.program_id(0)
        cur = z_ref[...]
        halo = jnp.where(i == 0, 0.0, zp_ref[...])
        ds = _pool_d(jnp.concatenate([halo, cur], axis=0), cur, i * tr, tr)
        for gi in range(len(POOL_WINDOWS)):
            y = jnp.dot(ds[gi].astype(BF16), wp_ref[gi], preferred_element_type=F32)
            y_ref[:, gi * LANES:(gi + 1) * LANES] = (y * sc_ref[:, gi * LANES:(gi + 1) * LANES]).astype(BF16)

    return pl.pallas_call(
        body, name="pool_fwd", grid=(S // tr,),
        in_specs=[_rows(tr, POOL_WIDTH, Z_POOL // POOL_WIDTH),
                  pl.BlockSpec((POOL_HALO, POOL_WIDTH), lambda i: (jnp.maximum(i * rb - 1, 0), 0)),
                  _full(wp.shape), _full((1, POOL_WIDTH))],
        out_specs=_rows(tr, POOL_WIDTH),
        out_shape=jax.ShapeDtypeStruct((S, POOL_WIDTH), BF16),
        compiler_params=_params("parallel"),
    )(z, z, wp, scale)


def _pool_bwd(z, dyc, wp, scale, dy_cb, tr=512):
    S = z.shape[0]
    tr = _tile(S, tr)
    rb = tr // POOL_HALO
    nhb = S // POOL_HALO
    ng = len(POOL_WINDOWS)

    def body(z_ref, zp_ref, dy_ref, dyn_ref, wp_ref, sc_ref, dz_ref, dwp_ref, dsc_ref):
        i = pl.program_id(0)

        @pl.when(i == 0)
        def _():
            dwp_ref[...] = jnp.zeros_like(dwp_ref)
            dsc_ref[...] = jnp.zeros_like(dsc_ref)

        cur = z_ref[...]
        halo = jnp.where(i == 0, 0.0, zp_ref[...])
        ds = _pool_d(jnp.concatenate([halo, cur], axis=0), cur, i * tr, tr)
        dy_cur = dy_ref[...]
        dy_next = jnp.where(i == pl.num_programs(0) - 1, 0.0, dyn_ref[...])
        dy_ext = jnp.concatenate([dy_cur, dy_next], axis=0)
        n_ext = tr + POOL_HALO
        t_ext = i * tr + lax.broadcasted_iota(jnp.int32, (n_ext, 1), 0)
        for gi, w in enumerate(POOL_WINDOWS):
            cols = slice(gi * LANES, (gi + 1) * LANES)
            d_b = ds[gi].astype(BF16)
            y_pre = jnp.dot(d_b, wp_ref[gi], preferred_element_type=F32)
            dsc_ref[:, cols] += jnp.sum(dy_cur[:, cols] * y_pre, axis=0, keepdims=True)
            dys = (dy_ext[:, cols] * sc_ref[:, cols]).astype(BF16)
            dwp_ref[gi] += lax.dot_general(d_b, dys[:tr], (((0,), (0,)), ((), ())), preferred_element_type=F32)
            dd = lax.dot_general(dys, wp_ref[gi], (((1,), (1,)), ((), ())), preferred_element_type=F32)
            s = dd / jnp.minimum(t_ext + 1, w).astype(F32)
            sh = 1
            while sh < w:
                s = s + pltpu.roll(s, n_ext - sh, 0)
                sh *= 2
            dz_ref[:, cols] = (s[:tr] - dd[:tr]).astype(BF16)

    return pl.pallas_call(
        body, name="pool_bwd", grid=(S // tr,),
        in_specs=[_rows(tr, POOL_WIDTH, Z_POOL // POOL_WIDTH),
                  pl.BlockSpec((POOL_HALO, POOL_WIDTH), lambda i: (jnp.maximum(i * rb - 1, 0), 0)),
                  _rows(tr, POOL_WIDTH, dy_cb),
                  pl.BlockSpec((POOL_HALO, POOL_WIDTH), lambda i: (jnp.minimum((i + 1) * rb, nhb - 1), dy_cb)),
                  _full(wp.shape), _full((1, POOL_WIDTH))],
        out_specs=[_rows(tr, POOL_WIDTH), _full((ng, LANES, LANES)), _full((1, POOL_WIDTH))],
        out_shape=[jax.ShapeDtypeStruct((S, POOL_WIDTH), BF16), jax.ShapeDtypeStruct((ng, LANES, LANES), F32),
                   jax.ShapeDtypeStruct((1, POOL_WIDTH), F32)],
        compiler_params=_params("arbitrary"),
    )(z, z, dyc, dyc, wp, scale)


def _rope_tables(S):
    half = MLA_ROPE // 2
    inv_freq = 1.0 / (ROPE_THETA ** (jnp.arange(half, dtype=F32) / half))
    ang = jnp.arange(S).astype(F32)[:, None] * inv_freq[None, :]
    cos, sin = jnp.cos(ang), jnp.sin(ang)
    z32 = jnp.zeros((S, half), F32)
    z64 = jnp.zeros((S, LANES - MLA_ROPE), F32)
    cos_t = jnp.concatenate([cos, cos, z64], axis=1)
    sin_a = jnp.concatenate([-sin, z32, z64], axis=1)
    sin_b = jnp.concatenate([z32, sin, z64], axis=1)
    return cos_t, sin_a, sin_b


def _rope(x, cos_t, sin_a, sin_b):
    return x * cos_t + pltpu.roll(x, LANES - 32, 1) * sin_a + pltpu.roll(x, 32, 1) * sin_b


def _rope_t(d, cos_t, sin_a, sin_b):
    return d * cos_t + pltpu.roll(d * sin_a, 32, 1) + pltpu.roll(d * sin_b, LANES - 32, 1)


def _mla_prep(q_raw, kv_raw, z, tabs, gq, gk, shards, kinds, tr=256):
    S = z.shape[0]
    tr = _tile(S, tr)
    n_steps = S // tr
    scale = 1.0 / math.sqrt(MLA_QK)
    W = MLA_HEADS * MLA_QK_PAD
    nw = len(shards)

    def body(q_ref, kv_ref, kr_ref, c_ref, sa_ref, sb_ref, gq_ref, gk_ref, *rest):
        x_refs, (qo_ref, ko_ref, vo_ref), g_refs = rest[:nw], rest[nw:nw + 3], rest[nw + 3:2 * nw + 3]
        gather = _Gather(x_refs, g_refs, kinds, *rest[2 * nw + 3:])
        i = pl.program_id(0)
        pl.when(i == 0)(gather.start)
        pl.when(i == (3 * n_steps) // 4)(gather.forward)
        tab = (c_ref[...], sa_ref[...], sb_ref[...])
        kr = kr_ref[...]
        kr_ss = jnp.sum(kr * kr, axis=-1, keepdims=True)
        gqn, gqr = gq_ref[:, :LANES], gq_ref[:, LANES:]
        gkn, gkr = gk_ref[:, :LANES], gk_ref[:, LANES:]
        for h in range(MLA_HEADS):
            c0 = h * MLA_QK_PAD
            qn = q_ref[:, c0:c0 + LANES].astype(F32)
            qr = q_ref[:, c0 + LANES:c0 + 2 * LANES].astype(F32)
            r = lax.rsqrt((jnp.sum(qn * qn, -1, keepdims=True) + jnp.sum(qr * qr, -1, keepdims=True)) / MLA_QK
                          + NORM_EPS)
            qo_ref[h, :, :LANES] = (qn * r * gqn * scale).astype(BF16)
            qo_ref[h, :, LANES:] = (_rope(qr * r * gqr, *tab) * scale).astype(BF16)
            kn = kv_ref[:, c0:c0 + LANES].astype(F32)
            r = lax.rsqrt((jnp.sum(kn * kn, -1, keepdims=True) + kr_ss) / MLA_QK + NORM_EPS)
            ko_ref[h, :, :LANES] = (kn * r * gkn).astype(BF16)
            ko_ref[h, :, LANES:] = _rope(kr * r * gkr, *tab).astype(BF16)
            vo_ref[h] = kv_ref[:, c0 + LANES:c0 + 2 * LANES].astype(BF16)
        pl.when(i == n_steps - 1)(gather.finish)

    hs = lambda w: pl.BlockSpec((MLA_HEADS, tr, w), lambda i: (0, i, 0))
    return pl.pallas_call(
        body, name="mla_prep", grid=(n_steps,),
        in_specs=[_rows(tr, W), _rows(tr, W), _rows(tr, LANES, Z_KR // LANES),
                  _rows(tr, LANES), _rows(tr, LANES), _rows(tr, LANES),
                  _full((1, MLA_QK_PAD)), _full((1, MLA_QK_PAD))] + [_ANY] * nw,
        out_specs=[hs(MLA_QK_PAD), hs(MLA_QK_PAD), hs(MLA_V)] + [_ANY] * nw,
        out_shape=[jax.ShapeDtypeStruct((MLA_HEADS, S, MLA_QK_PAD), BF16),
                   jax.ShapeDtypeStruct((MLA_HEADS, S, MLA_QK_PAD), BF16),
                   jax.ShapeDtypeStruct((MLA_HEADS, S, MLA_V), BF16)]
        + [jax.ShapeDtypeStruct(_gather_out_shape(kd, sd.shape), sd.dtype) for kd, sd in zip(kinds, shards)],
        scratch_shapes=[pltpu.SemaphoreType.DMA((nw, 7)), pltpu.SemaphoreType.DMA((nw, 7)),
                        pltpu.SemaphoreType.DMA((nw,))],
        compiler_params=_params("arbitrary"),
    )(q_raw, kv_raw, z, *tabs, gq, gk, *shards)


def _mla_prep_bwd(dq, dk, dv, q_raw, kv_raw, z, tabs, gq, gk, tr=256):
    S = z.shape[0]
    tr = _tile(S, tr)
    scale = 1.0 / math.sqrt(MLA_QK)
    W = MLA_HEADS * MLA_QK_PAD

    def body(dq_ref, dk_ref, dv_ref, q_ref, kv_ref, kr_ref, c_ref, sa_ref, sb_ref, gq_ref, gk_ref,
             dqr_ref, dkvr_ref, dkr_ref, dgq_ref, dgk_ref):
        @pl.when(pl.program_id(0) == 0)
        def _():
            dgq_ref[...] = jnp.zeros_like(dgq_ref)
            dgk_ref[...] = jnp.zeros_like(dgk_ref)

        tab = (c_ref[...], sa_ref[...], sb_ref[...])
        kr = kr_ref[...]
        kr_ss = jnp.sum(kr * kr, axis=-1, keepdims=True)
        gqn, gqr = gq_ref[:, :LANES], gq_ref[:, LANES:]
        gkn, gkr = gk_ref[:, :LANES], gk_ref[:, LANES:]
        dkr_sum = jnp.zeros((tr, LANES), F32)
        dgq_n = jnp.zeros((1, LANES), F32)
        dgq_r = jnp.zeros((1, LANES), F32)
        dgk_n = jnp.zeros((1, LANES), F32)
        dgk_r = jnp.zeros((1, LANES), F32)

        def head_norm_bwd(xn, xr, r, dyn, dyr, gn, gr):
            hn, hr = xn * r, xr * r
            dxn, dxr = dyn * gn, dyr * gr
            mt = (jnp.sum(dxn * hn, -1, keepdims=True) + jnp.sum(dxr * hr, -1, keepdims=True)) / MLA_QK
            return (r * (dxn - hn * mt), r * (dxr - hr * mt),
                    jnp.sum(dyn * hn, axis=0, keepdims=True), jnp.sum(dyr * hr, axis=0, keepdims=True))

        for h in range(MLA_HEADS):
            c0 = h * MLA_QK_PAD
            qn = q_ref[:, c0:c0 + LANES].astype(F32)
            qr = q_ref[:, c0 + LANES:c0 + 2 * LANES].astype(F32)
            r = lax.rsqrt((jnp.sum(qn * qn, -1, keepdims=True) + jnp.sum(qr * qr, -1, keepdims=True)) / MLA_QK
                          + NORM_EPS)
            dyn = dq_ref[h, :, :LANES] * scale
            dyr = _rope_t(dq_ref[h, :, LANES:] * scale, *tab)
            dxn, dxr, gn_, gr_ = head_norm_bwd(qn, qr, r, dyn, dyr, gqn, gqr)
            dgq_n += gn_
            dgq_r += gr_
            dqr_ref[:, c0:c0 + LANES] = dxn.astype(BF16)
            dqr_ref[:, c0 + LANES:c0 + 2 * LANES] = dxr.astype(BF16)

            kn = kv_ref[:, c0:c0 + LANES].astype(F32)
            r = lax.rsqrt((jnp.sum(kn * kn, -1, keepdims=True) + kr_ss) / MLA_QK + NORM_EPS)
            dyn = dk_ref[h, :, :LANES]
            dyr = _rope_t(dk_ref[h, :, LANES:], *tab)
            dxn, dxr, gn_, gr_ = head_norm_bwd(kn, kr, r, dyn, dyr, gkn, gkr)
            dgk_n += gn_
            dgk_r += gr_
            dkr_sum += dxr
            dkvr_ref[:, c0:c0 + LANES] = dxn.astype(BF16)
            dkvr_ref[:, c0 + LANES:c0 + 2 * LANES] = dv_ref[h].astype(BF16)

        dkr_ref[...] = dkr_sum.astype(BF16)
        dgq_ref[:, :LANES] += dgq_n
        dgq_ref[:, LANES:] += dgq_r
        dgk_ref[:, :LANES] += dgk_n
        dgk_ref[:, LANES:] += dgk_r

    hs = lambda w: pl.BlockSpec((MLA_HEADS, tr, w), lambda i: (0, i, 0))
    return pl.pallas_call(
        body, name="mla_prep_bwd", grid=(S // tr,),
        in_specs=[hs(MLA_QK_PAD), hs(MLA_QK_PAD), hs(MLA_V), _rows(tr, W), _rows(tr, W),
                  _rows(tr, LANES, Z_KR // LANES), _rows(tr, LANES), _rows(tr, LANES), _rows(tr, LANES),
                  _full((1, MLA_QK_PAD)), _full((1, MLA_QK_PAD))],
        out_specs=[_rows(tr, W), _rows(tr, W), _rows(tr, LANES), _full((1, MLA_QK_PAD)), _full((1, MLA_QK_PAD))],
        out_shape=[jax.ShapeDtypeStruct((S, W), BF16), jax.ShapeDtypeStruct((S, W), BF16),
                   jax.ShapeDtypeStruct((S, LANES), BF16),
                   jax.ShapeDtypeStruct((1, MLA_QK_PAD), F32), jax.ShapeDtypeStruct((1, MLA_QK_PAD), F32)],
        compiler_params=_params("arbitrary"),
    )(dq, dk, dv, q_raw, kv_raw, z, *tabs, gq, gk)


_NT = (((1,), (1,)), ((), ()))
_TN = (((0,), (0,)), ((), ()))


def _flash_fwd(q, k, v, shards, kinds, tq=1024, tk=1024, ts=512):
    H, S, dq = q.shape
    dv = v.shape[-1]
    tq, tk, ts = _tile(S, tq), _tile(S, tk), _tile(S, ts)
    assert tq % tk == 0 and tq % ts == 0 and dv == LANES
    nq, nd = S // tq, tq // tk
    nw = len(shards)

    def body(q_ref, k_ref, v_ref, *rest):
        x_refs, (o_ref, lse_ref), g_refs = rest[:nw], rest[nw:nw + 2], rest[nw + 2:2 * nw + 2]
        m_sc, l_sc, acc_sc, send_sems, recv_sems, local_sems = rest[2 * nw + 2:]
        h, i = pl.program_id(0), pl.program_id(1)
        gather = _Gather(x_refs, g_refs, kinds, send_sems, recv_sems, local_sems)
        pl.when(jnp.logical_and(h == 0, i == 0))(gather.start)
        pl.when(jnp.logical_and(h == (3 * H) // 4, i == 0))(gather.forward)

        m_sc[...] = jnp.full_like(m_sc, -jnp.inf)
        l_sc[...] = jnp.zeros_like(l_sc)
        acc_sc[...] = jnp.zeros_like(acc_sc)

        def step(off, width, r0, masked):
            rows = slice(r0, tq)
            s = lax.dot_general(q_ref[rows, :], k_ref[pl.ds(off, width), :], _NT, preferred_element_type=F32)
            if masked:
                row = i * tq + r0 + lax.broadcasted_iota(jnp.int32, s.shape, 0)
                col = off + lax.broadcasted_iota(jnp.int32, s.shape, 1)
                s = jnp.where(col <= row, s, -jnp.inf)
            parts = [s[:, c * LANES:(c + 1) * LANES] for c in range(width // LANES)]
            m_cur = parts[0]
            for pt in parts[1:]:
                m_cur = jnp.maximum(m_cur, pt)
            m_prev = m_sc[rows, :]
            m_new = jnp.maximum(m_prev, jnp.max(m_cur, axis=-1, keepdims=True))
            alpha = jnp.exp(m_prev - m_new)
            ps = [jnp.exp(pt - m_new) for pt in parts]
            l_new = alpha * l_sc[rows, :]
            for pc in ps:
                l_new = l_new + pc
            l_sc[rows, :] = l_new
            p = jnp.concatenate(ps, axis=1).astype(BF16)
            acc_sc[rows, :] = alpha * acc_sc[rows, :] + jnp.dot(p, v_ref[pl.ds(off, width), :],
                                                                preferred_element_type=F32)
            m_sc[rows, :] = m_new

        def full_step(j, carry):
            step(pl.multiple_of(j * tk, tk), tk, 0, False)
            return carry

        lax.fori_loop(0, i * nd, full_step, 0)
        for d in range(tq // ts):
            step(pl.multiple_of(i * tq + d * ts, ts), ts, d * ts, True)
        l = jnp.sum(l_sc[...], axis=-1, keepdims=True)
        o_ref[...] = (acc_sc[...] / l).astype(BF16)
        lse_ref[...] = m_sc[...] + jnp.log(l)
        pl.when(jnp.logical_and(h == H - 1, i == nq - 1))(gather.finish)

    return pl.pallas_call(
        body, name="flash_fwd", grid=(H, nq),
        in_specs=[pl.BlockSpec((None, tq, dq), lambda h, i: (h, i, 0)),
                  pl.BlockSpec((None, S, dq), lambda h, i: (h, 0, 0)),
                  pl.BlockSpec((None, S, dv), lambda h, i: (h, 0, 0))] + [_ANY] * nw,
        out_specs=[pl.BlockSpec((tq, dv), lambda h, i: (i, h)),
                   pl.BlockSpec((None, tq, LANES), lambda h, i: (h, i, 0))] + [_ANY] * nw,
        out_shape=[jax.ShapeDtypeStruct((S, H * dv), BF16), jax.ShapeDtypeStruct((H, S, LANES), F32)]
        + [jax.ShapeDtypeStruct(_gather_out_shape(kd, sd.shape), sd.dtype) for kd, sd in zip(kinds, shards)],
        scratch_shapes=[pltpu.VMEM((tq, LANES), F32), pltpu.VMEM((tq, LANES), F32), pltpu.VMEM((tq, dv), F32),
                        pltpu.SemaphoreType.DMA((nw, 7)), pltpu.SemaphoreType.DMA((nw, 7)),
                        pltpu.SemaphoreType.DMA((nw,))],
        compiler_params=_params("arbitrary", "arbitrary"),
    )(q, k, v, *shards)


def _flash_bwd(q, k, v, dyc, cb0, o, lse, sends, tq=1024, tk=1024, ts=512):
    H, S, dq = q.shape
    dv = v.shape[-1]
    tq, tk, ts = _tile(S, tq), _tile(S, tk), _tile(S, ts)
    assert tq % tk == 0 and tq % ts == 0
    nq, nd = S // tq, tq // tk
    nw = len(sends)

    def body(q_ref, k_ref, v_ref, dy_ref, o_ref, lse_ref, *rest):
        s_refs, (dq_ref, dk_ref, dv_ref), r_refs = rest[:nw], rest[nw:nw + 3], rest[nw + 3:2 * nw + 3]
        do_sc, dl_sc, send_sems, recv_sems = rest[2 * nw + 3:]
        h, i = pl.program_id(0), pl.program_id(1)
        scatter = _Scatter(s_refs, r_refs, send_sems, recv_sems)
        pl.when(jnp.logical_and(h == 0, i == 0))(scatter.start)

        @pl.when(i == 0)
        def _():
            dk_ref[...] = jnp.zeros_like(dk_ref)
            dv_ref[...] = jnp.zeros_like(dv_ref)

        dq_ref[...] = jnp.zeros_like(dq_ref)
        dy = dy_ref[...]
        do_sc[...] = dy.astype(BF16)
        dl_sc[...] = jnp.broadcast_to(jnp.sum(dy * o_ref[...].astype(F32), axis=-1, keepdims=True), (tq, LANES))

        def step(off, width, r0, masked):
            rows = slice(r0, tq)
            qv = q_ref[rows, :]
            dov = do_sc[rows, :]
            kb = k_ref[pl.ds(off, width), :]
            s = lax.dot_general(qv, kb, _NT, preferred_element_type=F32)
            if masked:
                row = i * tq + r0 + lax.broadcasted_iota(jnp.int32, s.shape, 0)
                col = off + lax.broadcasted_iota(jnp.int32, s.shape, 1)
                s = jnp.where(col <= row, s, -jnp.inf)
            p = jnp.exp(s - lse_ref[rows, :1])
            dv_ref[pl.ds(off, width), :] += lax.dot_general(p.astype(BF16), dov, _TN, preferred_element_type=F32)
            dp = lax.dot_general(dov, v_ref[pl.ds(off, width), :], _NT, preferred_element_type=F32)
            ds = (p * (dp - dl_sc[rows, :1])).astype(BF16)
            dk_ref[pl.ds(off, width), :] += lax.dot_general(ds, qv, _TN, preferred_element_type=F32)
            dq_ref[rows, :] += jnp.dot(ds, kb, preferred_element_type=F32)

        def full_step(j, carry):
            step(pl.multiple_of(j * tk, tk), tk, 0, False)
            return carry

        lax.fori_loop(0, i * nd, full_step, 0)
        for d in range(tq // ts):
            step(pl.multiple_of(i * tq + d * ts, ts), ts, d * ts, True)
        pl.when(jnp.logical_and(h == H - 1, i == nq - 1))(scatter.finish)

    return pl.pallas_call(
        body, name="flash_bwd", grid=(H, nq),
        in_specs=[pl.BlockSpec((None, tq, dq), lambda h, i: (h, i, 0)),
                  pl.BlockSpec((None, S, dq), lambda h, i: (h, 0, 0), pipeline_mode=pl.Buffered(1)),
                  pl.BlockSpec((None, S, dv), lambda h, i: (h, 0, 0), pipeline_mode=pl.Buffered(1)),
                  pl.BlockSpec((tq, dv), lambda h, i: (i, h + cb0)),
                  pl.BlockSpec((tq, dv), lambda h, i: (i, h)),
                  pl.BlockSpec((None, tq, LANES), lambda h, i: (h, i, 0))] + [_ANY] * nw,
        out_specs=[pl.BlockSpec((None, tq, dq), lambda h, i: (h, i, 0)),
                   pl.BlockSpec((None, S, dq), lambda h, i: (h, 0, 0)),
                   pl.BlockSpec((None, S, dv), lambda h, i: (h, 0, 0))] + [_ANY] * nw,
        out_shape=[jax.ShapeDtypeStruct((H, S, dq), F32), jax.ShapeDtypeStruct((H, S, dq), F32),
                   jax.ShapeDtypeStruct((H, S, dv), F32)]
        + [jax.ShapeDtypeStruct((N_DEV - 1,) + sd.shape[1:], sd.dtype) for sd in sends],
        scratch_shapes=[pltpu.VMEM((tq, dv), BF16), pltpu.VMEM((tq, LANES), F32),
                        pltpu.SemaphoreType.DMA((nw, N_DEV - 1)), pltpu.SemaphoreType.DMA((nw, N_DEV - 1))],
        compiler_params=_params("arbitrary", "arbitrary"),
    )(q, k, v, dyc, o, lse, *sends)


def _mem_norm(mem, g):
    n = mem.shape[1]

    def body(m_ref, g_ref, o_ref):
        xhat, _ = _rms_hat(m_ref[...], n)
        o_ref[...] = (xhat * g_ref[...]).astype(BF16)

    return pl.pallas_call(body, name="mem_norm", out_shape=jax.ShapeDtypeStruct(mem.shape, BF16),
                          compiler_params=pltpu.CompilerParams(vmem_limit_bytes=VMEM_LIMIT))(mem, g)


def _memkv_prep(mkv, gk):
    M = mkv.shape[0]

    def body(mkv_ref, g_ref, k_ref, v_ref):
        for h in range(X_HEADS):
            cols = slice(h * X_HEAD_DIM, (h + 1) * X_HEAD_DIM)
            xhat, _ = _rms_hat(mkv_ref[:, cols], X_HEAD_DIM)
            k_ref[:, cols] = (xhat * g_ref[...]).astype(BF16)
        v_ref[...] = mkv_ref[:, X_WIDTH:].astype(BF16)

    return pl.pallas_call(
        body, name="memkv_prep",
        out_shape=[jax.ShapeDtypeStruct((M, X_WIDTH), BF16), jax.ShapeDtypeStruct((M, X_WIDTH), BF16)],
    )(mkv, gk)


def _memkv_bwd(dk, dv, mkv, gk):
    M = mkv.shape[0]

    def body(dk_ref, dv_ref, mkv_ref, g_ref, dmkv_ref, dg_ref):
        dg = jnp.zeros((1, X_HEAD_DIM), F32)
        for h in range(X_HEADS):
            cols = slice(h * X_HEAD_DIM, (h + 1) * X_HEAD_DIM)
            xhat, r = _rms_hat(mkv_ref[:, cols], X_HEAD_DIM)
            dx, dgh = _rms_bwd(dk_ref[:, cols], xhat, r, g_ref[...], X_HEAD_DIM)
            dmkv_ref[:, cols] = dx.astype(BF16)
            dg += dgh
        dmkv_ref[:, X_WIDTH:] = dv_ref[...].astype(BF16)
        dg_ref[...] = dg

    return pl.pallas_call(
        body, name="memkv_bwd",
        out_shape=[jax.ShapeDtypeStruct((M, 2 * X_WIDTH), BF16), jax.ShapeDtypeStruct((1, X_HEAD_DIM), F32)],
    )(dk, dv, mkv, gk)


def _mem_gain_bwd(mem, dmn):
    n = mem.shape[1]

    def body(m_ref, d_ref, dg_ref):
        xhat, _ = _rms_hat(m_ref[...], n)
        dg_ref[...] = jnp.sum(d_ref[...] * xhat, axis=0, keepdims=True)

    return pl.pallas_call(body, name="mem_gain_bwd", out_shape=jax.ShapeDtypeStruct((1, n), F32),
                          compiler_params=pltpu.CompilerParams(vmem_limit_bytes=VMEM_LIMIT))(mem, dmn)


def _memx_scores(zq_ref, g, kx_ref, h):
    cols = slice(h * X_HEAD_DIM, (h + 1) * X_HEAD_DIM)
    xhat, r = _rms_hat(zq_ref[:, cols], X_HEAD_DIM)
    qn = (xhat * g).astype(BF16)
    s = lax.dot_general(qn, kx_ref[:, cols], _NT, preferred_element_type=F32) * (1.0 / math.sqrt(X_HEAD_DIM))
    p = jnp.exp(s - jnp.max(s, axis=-1, keepdims=True))
    return cols, xhat, r, qn, p, jnp.sum(p, axis=-1, keepdims=True)


def _memx_fwd(z, g, kx, vx, tr=512):
    S = z.shape[0]
    tr = _tile(S, tr)
    M = kx.shape[0]

    def body(zq_ref, g_ref, kx_ref, vx_ref, y_ref):
        for h in range(X_HEADS):
            cols, _, _, _, p, l = _memx_scores(zq_ref, g_ref[...], kx_ref, h)
            o = jnp.dot(p.astype(BF16), vx_ref[:, cols], preferred_element_type=F32)
            y_ref[:, cols] = (o / l).astype(BF16)

    return pl.pallas_call(
        body, name="memx_fwd", grid=(S // tr,),
        in_specs=[_rows(tr, X_WIDTH, Z_MQ // X_WIDTH), _full((1, X_HEAD_DIM)), _full((M, X_WIDTH)),
                  _full((M, X_WIDTH))],
        out_specs=_rows(tr, X_WIDTH),
        out_shape=jax.ShapeDtypeStruct((S, X_WIDTH), BF16),
        compiler_params=_params("parallel"),
    )(z, g, kx, vx)


def _memx_bwd(z, g, kx, vx, dyc, dy_cb, tr=512):
    S = z.shape[0]
    tr = _tile(S, tr)
    M = kx.shape[0]
    scale = 1.0 / math.sqrt(X_HEAD_DIM)

    def body(zq_ref, g_ref, kx_ref, vx_ref, dy_ref, dz_ref, dk_ref, dv_ref, dg_ref):
        @pl.when(pl.program_id(0) == 0)
        def _():
            dk_ref[...] = jnp.zeros_like(dk_ref)
            dv_ref[...] = jnp.zeros_like(dv_ref)
            dg_ref[...] = jnp.zeros_like(dg_ref)

        gv = g_ref[...]
        for h in range(X_HEADS):
            cols, xhat, r, qn, p, l = _memx_scores(zq_ref, gv, kx_ref, h)
            p = p / l
            do = dy_ref[:, cols].astype(BF16)
            dv_ref[:, cols] += lax.dot_general(p.astype(BF16), do, _TN, preferred_element_type=F32)
            dp = lax.dot_general(do, vx_ref[:, cols], _NT, preferred_element_type=F32)
            ds = (p * (dp - jnp.sum(dp * p, axis=-1, keepdims=True)) * scale).astype(BF16)
            dqn = jnp.dot(ds, kx_ref[:, cols], preferred_element_type=F32)
            dk_ref[:, cols] += lax.dot_general(ds, qn, _TN, preferred_element_type=F32)
            dx, dg = _rms_bwd(dqn, xhat, r, gv, X_HEAD_DIM)
            dz_ref[:, cols] = dx.astype(BF16)
            dg_ref[...] += dg

    return pl.pallas_call(
        body, name="memx_bwd", grid=(S // tr,),
        in_specs=[_rows(tr, X_WIDTH, Z_MQ // X_WIDTH), _full((1, X_HEAD_DIM)), _full((M, X_WIDTH)),
                  _full((M, X_WIDTH)), _rows(tr, X_WIDTH, dy_cb)],
        out_specs=[_rows(tr, X_WIDTH), _full((M, X_WIDTH)), _full((M, X_WIDTH)), _full((1, X_HEAD_DIM))],
        out_shape=[jax.ShapeDtypeStruct((S, X_WIDTH), BF16), jax.ShapeDtypeStruct((M, X_WIDTH), F32),
                   jax.ShapeDtypeStruct((M, X_WIDTH), F32), jax.ShapeDtypeStruct((1, X_HEAD_DIM), F32)],
        compiler_params=_params("arbitrary"),
    )(z, g, kx, vx, dyc)


def _conv_gc(g_ext, w_ref, b_ref, n_ext):
    g1 = pltpu.roll(g_ext, 1, 0)
    g2 = pltpu.roll(g_ext, 2, 0)
    gc = b_ref[...] + w_ref[0:1, :] * g2
    gc = gc + w_ref[1:2, :] * g1
    gc = gc + w_ref[2:3, :] * g_ext
    return gc, g1, g2


def _convglu_fwd(g, u, cw, cb, tr=512, tc=512):
    S, F = g.shape
    tr, tc = _tile(S, tr), _tile(F, tc)
    rb = tr // CONV_HALO

    def body(g_ref, gp_ref, u_ref, w_ref, b_ref, a_ref):
        i = pl.program_id(1)
        halo = jnp.where(i == 0, 0.0, gp_ref[...].astype(F32))
        g_ext = jnp.concatenate([halo, g_ref[...].astype(F32)], axis=0)
        gc, _, _ = _conv_gc(g_ext, w_ref, b_ref, tr + CONV_HALO)
        gc = gc[CONV_HALO:]
        sig = 0.5 * jnp.tanh(0.5 * gc) + 0.5
        a_ref[...] = (gc * sig * u_ref[...].astype(F32)).astype(BF16)

    return pl.pallas_call(
        body, name="convglu_fwd", grid=(F // tc, S // tr),
        in_specs=[pl.BlockSpec((tr, tc), lambda j, i: (i, j)),
                  pl.BlockSpec((CONV_HALO, tc), lambda j, i: (jnp.maximum(i * rb - 1, 0), j)),
                  pl.BlockSpec((tr, tc), lambda j, i: (i, j)),
                  pl.BlockSpec((3, tc), lambda j, i: (0, j)), pl.BlockSpec((1, tc), lambda j, i: (0, j))],
        out_specs=pl.BlockSpec((tr, tc), lambda j, i: (i, j)),
        out_shape=jax.ShapeDtypeStruct((S, F), BF16),
        compiler_params=_params("parallel", "parallel"),
    )(g, g, u, cw, cb)


def _convglu_bwd(g, u, da, cw, cb, tr=512, tc=512):
    S, F = g.shape
    tr, tc = _tile(S, tr), _tile(F, tc)
    rb = tr // CONV_HALO
    nhb = S // CONV_HALO
    H = CONV_HALO

    def body(g_ref, gp_ref, gn_ref, u_ref, un_ref, da_ref, dan_ref, w_ref, b_ref,
             dg_ref, du_ref, dw_ref, db_ref):
        i = pl.program_id(1)
        last = i == pl.num_programs(1) - 1

        @pl.when(i == 0)
        def _():
            dw_ref[...] = jnp.zeros_like(dw_ref)
            db_ref[...] = jnp.zeros_like(db_ref)

        g_prev = jnp.where(i == 0, 0.0, gp_ref[...].astype(F32))
        g_cur = g_ref[...].astype(F32)
        g_ext = jnp.concatenate([g_prev, g_cur, gn_ref[...].astype(F32)], axis=0)
        gc, g1, g2 = _conv_gc(g_ext, w_ref, b_ref, tr + 2 * H)
        gc = gc[H:]
        ux = jnp.concatenate([u_ref[...], un_ref[...]], axis=0).astype(F32)
        dax = jnp.concatenate([da_ref[...].astype(F32), jnp.where(last, 0.0, dan_ref[...].astype(F32))], axis=0)
        sig = 0.5 * jnp.tanh(0.5 * gc) + 0.5
        du_ref[...] = (dax[:tr] * (gc[:tr] * sig[:tr])).astype(BF16)
        dgc = dax * ux * (sig * (1.0 + gc * (1.0 - sig)))
        n = tr + H
        d1 = pltpu.roll(dgc, n - 1, 0)[:tr]
        d2 = pltpu.roll(dgc, n - 2, 0)[:tr]
        d0 = dgc[:tr]
        dg_ref[...] = (w_ref[2:3, :] * d0 + w_ref[1:2, :] * d1 + w_ref[0:1, :] * d2).astype(BF16)
        db_ref[...] += jnp.sum(d0, axis=0, keepdims=True)
        dw_ref[0:1, :] += jnp.sum(d0 * g2[H:H + tr], axis=0, keepdims=True)
        dw_ref[1:2, :] += jnp.sum(d0 * g1[H:H + tr], axis=0, keepdims=True)
        dw_ref[2:3, :] += jnp.sum(d0 * g_cur, axis=0, keepdims=True)

    cur = pl.BlockSpec((tr, tc), lambda j, i: (i, j))
    prv = pl.BlockSpec((H, tc), lambda j, i: (jnp.maximum(i * rb - 1, 0), j))
    nxt = pl.BlockSpec((H, tc), lambda j, i: (jnp.minimum((i + 1) * rb, nhb - 1), j))
    return pl.pallas_call(
        body, name="convglu_bwd", grid=(F // tc, S // tr),
        in_specs=[cur, prv, nxt, cur, nxt, cur, nxt,
                  pl.BlockSpec((3, tc), lambda j, i: (0, j)), pl.BlockSpec((1, tc), lambda j, i: (0, j))],
        out_specs=[cur, cur, pl.BlockSpec((3, tc), lambda j, i: (0, j)), pl.BlockSpec((1, tc), lambda j, i: (0, j))],
        out_shape=[jax.ShapeDtypeStruct((S, F), BF16), jax.ShapeDtypeStruct((S, F), BF16),
                   jax.ShapeDtypeStruct((3, F), F32), jax.ShapeDtypeStruct((1, F), F32)],
        compiler_params=_params("parallel", "arbitrary"),
    )(g, g, g, u, u, da, da, cw, cb)


def _mm_down_loss(a, w, x1, target, tm=1024, tn=512):
    S, K = a.shape
    n = w.shape[1]
    tm, tn = _tile(S, tm), _tile(n, tn)

    def body(a_ref, w_ref, x_ref, t_ref, dy_ref, dyb_ref, loss_ref):
        @pl.when(jnp.logical_and(pl.program_id(0) == 0, pl.program_id(1) == 0))
        def _():
            loss_ref[...] = jnp.zeros_like(loss_ref)

        y = jnp.dot(a_ref[...], w_ref[...], preferred_element_type=F32) + x_ref[...]
        err = y - t_ref[...]
        dy = err / n
        dy_ref[...] = dy
        dyb_ref[...] = dy.astype(BF16)
        loss_ref[...] += 0.5 * jnp.sum(jnp.sum(err * err, axis=-1, keepdims=True) / n)

    tile = pl.BlockSpec((tm, tn), lambda i, j: (i, j))
    return pl.pallas_call(
        body, name="mm_down_loss", grid=(S // tm, n // tn),
        in_specs=[pl.BlockSpec((tm, K), lambda i, j: (i, 0)), pl.BlockSpec((K, tn), lambda i, j: (0, j)), tile, tile],
        out_specs=[tile, tile, pl.BlockSpec((8, LANES), lambda i, j: (0, 0))],
        out_shape=[jax.ShapeDtypeStruct((S, n), F32), jax.ShapeDtypeStruct((S, n), BF16),
                   jax.ShapeDtypeStruct((8, LANES), F32)],
        compiler_params=_params("arbitrary", "arbitrary"),
    )(a, w, x1, target)


def _mm_o_rms(ycat, w, x, g, tm=512):
    S, K = ycat.shape
    n = w.shape[1]
    tm = _tile(S, tm)

    def body(y_ref, w_ref, x_ref, g_ref, x1_ref, h2_ref):
        x1 = jnp.dot(y_ref[...], w_ref[...], preferred_element_type=F32) + x_ref[...]
        x1_ref[...] = x1
        xhat, _ = _rms_hat(x1, n)
        h2_ref[...] = (xhat * g_ref[...]).astype(BF16)

    return pl.pallas_call(
        body, name="mm_o_rms", grid=(S // tm,),
        in_specs=[_rows(tm, K), _full((K, n)), _rows(tm, n), _full((1, n))],
        out_specs=[_rows(tm, n), _rows(tm, n)],
        out_shape=[jax.ShapeDtypeStruct((S, n), F32), jax.ShapeDtypeStruct((S, n), BF16)],
        compiler_params=_params("parallel"),
    )(ycat, w, x, g)


def _mm_in_dx_rms(dz, w, x, g, extra, sends, tm=256):
    S, K = dz.shape
    n = w.shape[0]
    tm = _tile(S, tm)
    n_steps = S // tm
    nw = len(sends)

    def body(dz_ref, w_ref, x_ref, g_ref, ex_ref, *rest):
        s_refs, (dx_ref, dg_ref), r_refs = rest[:nw], rest[nw:nw + 2], rest[nw + 2:2 * nw + 2]
        scatter = _Scatter(s_refs, r_refs, *rest[2 * nw + 2:])
        i = pl.program_id(0)
        pl.when(i == 0)(scatter.start)

        @pl.when(i == 0)
        def _():
            dg_ref[...] = jnp.zeros_like(dg_ref)

        dh = lax.dot_general(dz_ref[...], w_ref[...], _NT, preferred_element_type=F32)
        xhat, r = _rms_hat(x_ref[...], n)
        dx, dg = _rms_bwd(dh, xhat, r, g_ref[...], n)
        dx_ref[...] = dx + ex_ref[...]
        dg_ref[...] += dg
        pl.when(i == n_steps - 1)(scatter.finish)

    return pl.pallas_call(
        body, name="mm_in_dx_rms", grid=(n_steps,),
        in_specs=[_rows(tm, K), _full((n, K)), _rows(tm, n), _full((1, n)), _rows(tm, n)] + [_ANY] * nw,
        out_specs=[_rows(tm, n), _full((1, n))] + [_ANY] * nw,
        out_shape=[jax.ShapeDtypeStruct((S, n), F32), jax.ShapeDtypeStruct((1, n), F32)]
        + [jax.ShapeDtypeStruct((N_DEV - 1,) + sd.shape[1:], sd.dtype) for sd in sends],
        scratch_shapes=[pltpu.SemaphoreType.DMA((nw, N_DEV - 1)), pltpu.SemaphoreType.DMA((nw, N_DEV - 1))],
        compiler_params=_params("arbitrary"),
    )(dz, w, x, g, extra, *sends)


def _local_step(x, mem, target, W, early_shards, late_shards):
    S = x.shape[0]
    tabs = _rope_tables(S)
    W = dict(W)
    G, Gb = {}, {}

    h, W["w_in"], W["w_q_up"], W["w_kv_up"] = _rms_fwd("rms1_fwd", x, W["g_mix"], early_shards, EARLY_GATHER_KINDS)
    z = _mm("mm_in", h, W["w_in"], tm=512, tn=Z_COLS, tk=D_MODEL)
    y_pool = _pool_fwd(z, W["w_pool"], W["pool_scale"])
    ql, kvl = _lat_fwd(z, W["g_q_lat"], W["g_kv_lat"])
    q_raw = _mm("mm_q_up", ql, W["w_q_up"], out_dtype=BF16, tm=1024, tn=2048, tk=Q_LORA)
    kv_raw = _mm("mm_kv_up", kvl, W["w_kv_up"], out_dtype=BF16, tm=1024, tn=2048, tk=KV_LORA)
    n_ffn = len(FFN_GATHER_KINDS)
    q, k, v, W["w_o"], W["w_mem_kv"] = _mla_prep(q_raw, kv_raw, z, tabs, W["g_q_mla"], W["g_k_mla"],
                                                 late_shards[n_ffn:], MIXER_GATHER_KINDS)
    o_mla, lse, wg3, wu3, W["w_down"], cw3 = _flash_fwd(q, k, v, late_shards[:n_ffn], FFN_GATHER_KINDS)
    W["w_gate"] = jnp.transpose(wg3, (1, 0, 2)).reshape(D_MODEL, D_FF)
    W["w_up"] = jnp.transpose(wu3, (1, 0, 2)).reshape(D_MODEL, D_FF)
    cw = jnp.sum(cw3.reshape(N_DEV, 3, 3, D_FF // N_DEV).astype(F32), axis=1)
    W["conv_w"] = jnp.transpose(cw, (1, 0, 2)).reshape(3, D_FF)
    mn = _mem_norm(mem, W["g_mem"])
    mkv = _mm("mm_mem_kv", mn, W["w_mem_kv"], tm=256, tn=1024, tk=D_MODEL)
    kx, vx = _memkv_prep(mkv, W["g_k_x"])
    y_mem = _memx_fwd(z, W["g_q_x"], kx, vx)
    ycat = jnp.concatenate([y_pool, o_mla, y_mem], axis=1)
    x1, h2 = _mm_o_rms(ycat, W["w_o"], x, W["g_ffn"])
    g = _mm("mm_gate", h2, W["w_gate"], out_dtype=BF16, tm=1024, tn=1408, tk=D_MODEL)
    u = _mm("mm_up", h2, W["w_up"], out_dtype=BF16, tm=1024, tn=1408, tk=D_MODEL)
    a = _convglu_fwd(g, u, W["conv_w"], W["conv_b"])
    dy, dyb, loss_part = _mm_down_loss(a, W["w_down"], x1, target)

    da = _mm("mm_down_dx", dyb, W["w_down"], tb=True, out_dtype=BF16, tm=1024, tn=1408, tk=D_MODEL)
    G["w_down"], Gb["w_down"] = _mm("mm_down_dw", a, dyb, ta=True, also_bf16=True, tm=512, tn=1024, tk=4096)
    dg, du, G["conv_w"], G["conv_b"] = _convglu_bwd(g, u, da, W["conv_w"], W["conv_b"])
    dh2 = _mm("mm_gate_dx", dg, W["w_gate"], tb=True, tm=1024, tn=512, tk=D_FF)
    G["w_gate"] = _mm("mm_gate_dw", h2, dg, ta=True, tm=512, tn=1408, tk=4096)
    G["w_up"] = _mm("mm_up_dw", h2, du, ta=True, tm=512, tn=1408, tk=4096)
    dx1, dx1b, G["g_ffn"] = _mm_up_dx_rms(du, W["w_up"], dh2, x1, W["g_ffn"], dy)

    dyc = _mm("mm_o_dx", dx1b, W["w_o"], tb=True, tm=512, tn=2048, tk=D_MODEL)
    G["w_o"], Gb["w_o"] = _mm("mm_o_dw", ycat, dx1b, ta=True, also_bf16=True, tm=1024, tn=1024, tk=2048)
    dz_pool, G["w_pool"], G["pool_scale"] = _pool_bwd(z, dyc, W["w_pool"], W["pool_scale"], dy_cb=0)
    dz_mq, dkx, dvx, G["g_q_x"] = _memx_bwd(z, W["g_q_x"], kx, vx, dyc, dy_cb=3)
    dmkv, G["g_k_x"] = _memkv_bwd(dkx, dvx, mkv, W["g_k_x"])
    G["w_mem_kv"], Gb["w_mem_kv"] = _mm("mm_mem_kv_dw", mn, dmkv, ta=True, also_bf16=True, tm=1024, tn=1024,
                                        tk=MEM_LEN)
    dmn = _mm("mm_mem_kv_dx", dmkv, W["w_mem_kv"], tb=True, tm=256, tn=2048, tk=1024)
    G["g_mem"] = _mem_gain_bwd(mem, dmn)
    sends = [_send_blocks(n, Gb.get(n, G[n])) for n in SCATTER_EARLY]
    dq, dk, dv, *got_early = _flash_bwd(q, k, v, dyc, POOL_WIDTH // MLA_V, o_mla, lse, sends)
    dq_raw, dkv_raw, dz_kr, G["g_q_mla"], G["g_k_mla"] = _mla_prep_bwd(
        dq, dk, dv, q_raw, kv_raw, z, tabs, W["g_q_mla"], W["g_k_mla"])
    G["w_q_up"] = _mm("mm_q_up_dw", ql, dq_raw, ta=True, tm=512, tn=2048, tk=1024)
    dql = _mm("mm_q_up_dx", dq_raw, W["w_q_up"], tb=True, tm=1024, tn=512, tk=2048)
    G["w_kv_up"] = _mm("mm_kv_up_dw", kvl, dkv_raw, ta=True, tm=256, tn=2048, tk=1024)
    dkvl = _mm("mm_kv_up_dx", dkv_raw, W["w_kv_up"], tb=True, tm=1024, tn=256, tk=2048)
    dz_q, dz_kv, G["g_q_lat"], G["g_kv_lat"] = _lat_bwd(z, W["g_q_lat"], W["g_kv_lat"], dql, dkvl)
    dz = jnp.concatenate([dz_pool, dz_q, dz_mq, dz_kv, dz_kr], axis=1)
    G["w_in"] = _mm("mm_in_dw", h, dz, ta=True, tm=1024, tn=Z_COLS, tk=1024)
    grad_x, G["g_mix"], *got_late = _mm_in_dx_rms(dz, W["w_in"], x, W["g_mix"], dx1,
                                                  [_send_blocks(n, G[n]) for n in SCATTER_LATE])
    return loss_part, grad_x, G, dict(zip(SCATTER_EARLY + SCATTER_LATE, got_early + got_late))


_ANY = pl.BlockSpec(memory_space=pl.ANY)
_MESH = pl.DeviceIdType.MESH


def _gather_out_shape(kind, shape):
    if kind == "rows":
        return (N_DEV * shape[0],) + tuple(shape[1:])
    if kind == "cols":
        return (shape[0], N_DEV * shape[1])
    return (N_DEV,) + tuple(shape)


def _gather_view(ref, kind, shape, d):
    if kind == "rows":
        return ref.at[pl.ds(pl.multiple_of(d * shape[0], 16), shape[0]), :]
    if kind == "cols":
        return ref.at[:, pl.ds(pl.multiple_of(d * shape[1], math.gcd(shape[1], LANES)), shape[1])]
    return ref.at[d]


class _Gather:
    def __init__(self, x_refs, out_refs, kinds, send_sems, recv_sems, local_sems):
        self.xr, self.outr, self.kinds = x_refs, out_refs, kinds
        self.ss, self.rs, self.ls = send_sems, recv_sems, local_sems
        x, y, c = lax.axis_index("x"), lax.axis_index("y"), lax.axis_index("c")
        self.c = c
        self.me, self.sibling = (x, y, c), (x, y, 1 - c)
        self.chips = [(1 - x, y), (x, 1 - y), (1 - x, 1 - y)]

    def _view(self, w, dev):
        px, py, pc = dev
        return _gather_view(self.outr[w], self.kinds[w], self.xr[w].shape, 4 * px + 2 * py + pc)

    def _copy(self, w, k, block, to, from_shard=False):
        v = self._view(w, block)
        return pltpu.make_async_remote_copy(
            src_ref=self.xr[w] if from_shard else v, dst_ref=v, send_sem=self.ss.at[w, k],
            recv_sem=self.rs.at[w, k], device_id=to, device_id_type=_MESH)

    def _local(self, w):
        return pltpu.make_async_copy(self.xr[w], self._view(w, self.me), self.ls.at[w])

    def start(self):
        for w in range(len(self.xr)):
            self._local(w).start()
            self._copy(w, 0, self.me, self.sibling, True).start()
            for j, chip in enumerate(self.chips):
                self._copy(w, 1 + j, self.me, (*chip, self.c), True).start()

    def forward(self):
        for j, chip in enumerate(self.chips):
            for w in range(len(self.xr)):
                self._copy(w, 1 + j, (*chip, self.c), self.me).wait_recv()
                self._copy(w, 4 + j, (*chip, self.c), self.sibling).start()

    def finish(self):
        for w in range(len(self.xr)):
            self._copy(w, 0, self.sibling, self.me).wait_recv()
            for j, chip in enumerate(self.chips):
                self._copy(w, 4 + j, (*chip, 1 - self.c), self.me).wait_recv()
            self._copy(w, 0, self.me, self.sibling, True).wait_send()
            for j, chip in enumerate(self.chips):
                self._copy(w, 1 + j, self.me, (*chip, self.c), True).wait_send()
                self._copy(w, 4 + j, (*chip, self.c), self.sibling).wait_send()
            self._local(w).wait()


class _Scatter:
    def __init__(self, send_refs, recv_refs, send_sems, recv_sems):
        self.sr, self.rr, self.ss, self.rs = send_refs, recv_refs, send_sems, recv_sems
        self.xyz = lax.axis_index("x"), lax.axis_index("y"), lax.axis_index("c")

    def _copy(self, w, k):
        x, y, c = self.xyz
        px, py, pc = x ^ ((k >> 2) & 1), y ^ ((k >> 1) & 1), c ^ (k & 1)
        return pltpu.make_async_remote_copy(
            src_ref=self.sr[w].at[4 * px + 2 * py + pc], dst_ref=self.rr[w].at[k - 1],
            send_sem=self.ss.at[w, k - 1], recv_sem=self.rs.at[w, k - 1],
            device_id=(px, py, pc), device_id_type=_MESH)

    def _all(self):
        return [self._copy(w, k) for w in range(len(self.sr)) for k in range(1, N_DEV)]

    def start(self):
        for cp in self._all():
            cp.start()

    def finish(self):
        for cp in self._all():
            cp.wait_recv()
        for cp in self._all():
            cp.wait_send()


def _exchange_small(s):
    def body(x_ref, out_ref, x_send, x_recv, local_sem):
        x, y, c = lax.axis_index("x"), lax.axis_index("y"), lax.axis_index("c")
        me = 4 * x + 2 * y + c
        mine = pltpu.make_async_copy(x_ref, out_ref.at[me], local_sem)
        mine.start()

        def copy(k):
            px, py, pc = x ^ ((k >> 2) & 1), y ^ ((k >> 1) & 1), c ^ (k & 1)
            return pltpu.make_async_remote_copy(
                src_ref=x_ref, dst_ref=out_ref.at[me], send_sem=x_send.at[k - 1], recv_sem=x_recv.at[k - 1],
                device_id=(px, py, pc), device_id_type=_MESH)

        cps = [copy(k) for k in range(1, N_DEV)]
        for cp in cps:
            cp.start()
        for cp in cps:
            cp.wait_recv()
        for cp in cps:
            cp.wait_send()
        mine.wait()

    return pl.pallas_call(
        body, name="exchange_small", out_shape=jax.ShapeDtypeStruct((N_DEV,) + s.shape, s.dtype),
        in_specs=[_ANY], out_specs=_ANY,
        scratch_shapes=[pltpu.SemaphoreType.DMA((N_DEV - 1,)), pltpu.SemaphoreType.DMA((N_DEV - 1,)),
                        pltpu.SemaphoreType.DMA],
    )(s)


def _sum_slots(buf, tr=208):
    n, R, _ = buf.shape
    tr = tr if R % tr == 0 else R

    def body(b_ref, o_ref):
        acc = b_ref[0]
        for d in range(1, n):
            acc = acc + b_ref[d]
        o_ref[...] = acc

    return pl.pallas_call(
        body, name="sum_slots", grid=(R // tr,),
        in_specs=[pl.BlockSpec((n, tr, LANES), lambda i: (0, i, 0))],
        out_specs=pl.BlockSpec((tr, LANES), lambda i: (i, 0)),
        out_shape=jax.ShapeDtypeStruct((R, LANES), buf.dtype),
        compiler_params=_params("parallel"),
    )(buf)


def _adamw_math(w, g, m, v):
    m_new = ADAM_B1 * m + (1.0 - ADAM_B1) * g
    v_new = ADAM_B2 * v + (1.0 - ADAM_B2) * (g * g)
    m_hat = m_new / (1.0 - ADAM_B1 ** ADAM_STEP)
    v_hat = v_new / (1.0 - ADAM_B2 ** ADAM_STEP)
    return -ADAM_LR * (m_hat / (jnp.sqrt(v_hat) + ADAM_EPS) + ADAM_WD * w), m_new, v_new


def _adamw(name, w, g, got, m, v, tr=128):
    R, C = w.shape
    tr = max([t for t in range(16, tr + 1, 16) if R % t == 0], default=R) if R > tr else R

    def body(w_ref, g_ref, got_ref, m_ref, v_ref, go_ref, d_ref, mo_ref, vo_ref):
        gv = g_ref[...]
        for k in range(N_DEV - 1):
            gv = gv + got_ref[k].astype(F32)
        go_ref[...] = gv
        d_ref[...], mo_ref[...], vo_ref[...] = _adamw_math(w_ref[...], gv, m_ref[...], v_ref[...])

    spec = pl.BlockSpec((tr, C), lambda i: (i, 0))
    got_spec = pl.BlockSpec((N_DEV - 1, tr, C), lambda i: (0, i, 0))
    sds = jax.ShapeDtypeStruct((R, C), F32)
    return pl.pallas_call(
        body, name=name, grid=(R // tr,), in_specs=[spec, spec, got_spec, spec, spec], out_specs=[spec] * 4,
        out_shape=[sds] * 4, compiler_params=_params("parallel"),
    )(w, g, got, m, v)


def _adamw_small(ws, gs, ms, vs):
    n = len(ws)

    def body(*refs):
        ins, outs = refs[:4 * n], refs[4 * n:]
        for p in range(n):
            w_ref, g_ref, m_ref, v_ref = (ins[q * n + p] for q in range(4))
            res = _adamw_math(w_ref[...], g_ref[...], m_ref[...], v_ref[...])
            for q in range(3):
                outs[q * n + p][...] = res[q]

    sds = [jax.ShapeDtypeStruct(w.shape, F32) for w in ws]
    outs = pl.pallas_call(body, name="adamw_small", out_shape=sds * 3)(*ws, *gs, *ms, *vs)
    return outs[:n], outs[n:2 * n], outs[2 * n:]


def _pad_rows(a, mult):
    r = (-a.shape[0]) % mult
    return a if r == 0 else jnp.concatenate([a, jnp.zeros((r,) + a.shape[1:], a.dtype)], axis=0)


def _as_rows(a, mult):
    flat = a.reshape(-1)
    r = (-flat.shape[0]) % LANES
    if r:
        flat = jnp.concatenate([flat, jnp.zeros((r,), a.dtype)])
    return _pad_rows(flat.reshape(-1, LANES), mult)


def _w_in_to_kernel_cols(w):
    pad = jnp.zeros(w.shape[:-1] + (Z_COLS - IN_COLS,), w.dtype)
    return jnp.concatenate([w[..., :1024], w[..., 1344:1856], w[..., 1024:1344], pad], axis=-1)


def _w_in_from_kernel_cols(w):
    return jnp.concatenate([w[..., :1024], w[..., 1536:1856], w[..., 1024:1536]], axis=-1)


def _dev_blocks(n, g):
    ffs = D_FF // N_DEV
    if n == "w_in":
        return _w_in_from_kernel_cols(g).reshape(N_DEV, D_MODEL // N_DEV, IN_COLS)
    if n == "w_q_up":
        return jnp.transpose(g.reshape(Q_LORA, N_DEV, MLA_QK_PAD)[:, :, :MLA_QK], (1, 0, 2))
    if n == "w_kv_up":
        return jnp.transpose(g.reshape(KV_LORA, N_DEV, MLA_NOPE + MLA_V), (1, 0, 2))
    if n in ("w_mem_kv", "w_o"):
        return g.reshape(N_DEV, D_MODEL // N_DEV, g.shape[1])
    if n in ("w_gate", "w_up"):
        return jnp.transpose(g.reshape(D_MODEL, N_DEV, ffs), (1, 0, 2))
    if n == "conv_w":
        return jnp.transpose(g.reshape(3, N_DEV, ffs), (1, 0, 2))
    assert n == "w_down"
    return g.reshape(N_DEV, ffs, D_MODEL)


def _send_blocks(n, g):
    blocks = _dev_blocks(n, g)
    return blocks if n == "conv_w" else blocks.astype(BF16)


def kernel(x, mem, g_mix, w_in, g_q_lat, w_q_up, g_kv_lat, w_kv_up, g_q_mla, g_k_mla, w_pool, pool_scale, g_mem, w_mem_kv, g_q_x, g_k_x, w_o, g_ffn, w_gate, w_up, conv_w, conv_b, w_down, loss_target, m_g_mix, m_w_in, m_g_q_lat, m_w_q_up, m_g_kv_lat, m_w_kv_up, m_g_q_mla, m_g_k_mla, m_w_pool, m_pool_scale, m_g_mem, m_w_mem_kv, m_g_q_x, m_g_k_x, m_w_o, m_g_ffn, m_w_gate, m_w_up, m_conv_w, m_conv_b, m_w_down, v_g_mix, v_w_in, v_g_q_lat, v_w_q_up, v_g_kv_lat, v_w_kv_up, v_g_q_mla, v_g_k_mla, v_w_pool, v_pool_scale, v_g_mem, v_w_mem_kv, v_g_q_x, v_g_k_x, v_w_o, v_g_ffn, v_w_gate, v_w_up, v_conv_w, v_conv_b, v_w_down):
    given = dict(g_mix=g_mix, w_in=w_in, g_q_lat=g_q_lat, w_q_up=w_q_up, g_kv_lat=g_kv_lat, w_kv_up=w_kv_up,
                 g_q_mla=g_q_mla, g_k_mla=g_k_mla, w_pool=w_pool, pool_scale=pool_scale, g_mem=g_mem,
                 w_mem_kv=w_mem_kv, g_q_x=g_q_x, g_k_x=g_k_x, w_o=w_o, g_ffn=g_ffn, w_gate=w_gate, w_up=w_up,
                 conv_w=conv_w, conv_b=conv_b, w_down=w_down)
    mom_m = dict(g_mix=m_g_mix, w_in=m_w_in, g_q_lat=m_g_q_lat, w_q_up=m_w_q_up, g_kv_lat=m_g_kv_lat,
                 w_kv_up=m_w_kv_up, g_q_mla=m_g_q_mla, g_k_mla=m_g_k_mla, w_pool=m_w_pool,
                 pool_scale=m_pool_scale, g_mem=m_g_mem, w_mem_kv=m_w_mem_kv, g_q_x=m_g_q_x, g_k_x=m_g_k_x,
                 w_o=m_w_o, g_ffn=m_g_ffn, w_gate=m_w_gate, w_up=m_w_up, conv_w=m_conv_w, conv_b=m_conv_b,
                 w_down=m_w_down)
    mom_v = dict(g_mix=v_g_mix, w_in=v_w_in, g_q_lat=v_g_q_lat, w_q_up=v_w_q_up, g_kv_lat=v_g_kv_lat,
                 w_kv_up=v_w_kv_up, g_q_mla=v_g_q_mla, g_k_mla=v_g_k_mla, w_pool=v_w_pool,
                 pool_scale=v_pool_scale, g_mem=v_g_mem, w_mem_kv=v_w_mem_kv, g_q_x=v_g_q_x, g_k_x=v_g_k_x,
                 w_o=v_w_o, g_ffn=v_g_ffn, w_gate=v_w_gate, w_up=v_w_up, conv_w=v_conv_w, conv_b=v_conv_b,
                 w_down=v_w_down)
    drop = lambda a: a[0] if a.ndim > 2 else a
    sh = {n: drop(given[n]) for n in WEIGHTS}
    mom_m = {n: drop(mom_m[n]) for n in WEIGHTS}
    mom_v = {n: drop(mom_v[n]) for n in WEIGHTS}

    cw_hi = sh["conv_w"].astype(BF16)
    cw_r = sh["conv_w"] - cw_hi.astype(F32)
    cw_mid = cw_r.astype(BF16)
    cw_lo = (cw_r - cw_mid.astype(F32)).astype(BF16)
    early_shards = [
        _w_in_to_kernel_cols(sh["w_in"]).astype(BF16),
        jnp.pad(sh["w_q_up"], ((0, 0), (0, MLA_QK_PAD - MLA_QK))).astype(BF16),
        sh["w_kv_up"].astype(BF16),
    ]
    late_shards = [sh["w_gate"].astype(BF16), sh["w_up"].astype(BF16), sh["w_down"].astype(BF16),
                   jnp.concatenate([cw_hi, cw_mid, cw_lo], axis=0), sh["w_o"].astype(BF16),
                   sh["w_mem_kv"].astype(BF16)]
    W = {"w_pool": sh["w_pool"].astype(BF16)}
    for n in ("g_mix", "g_q_lat", "g_kv_lat", "pool_scale", "g_mem", "g_q_x", "g_k_x", "g_ffn", "conv_b"):
        W[n] = sh[n]
    pad_qk = lambda gv: jnp.pad(gv, ((0, 0), (0, MLA_QK_PAD - MLA_QK)))
    W["g_q_mla"], W["g_k_mla"] = pad_qk(sh["g_q_mla"]), pad_qk(sh["g_k_mla"])

    loss_part, grad_x, G, got = _local_step(x[0], mem[0], loss_target[0], W, early_shards, late_shards)

    small = {
        "g_mix": G["g_mix"], "g_q_lat": G["g_q_lat"], "g_kv_lat": G["g_kv_lat"],
        "g_q_mla": G["g_q_mla"][:, :MLA_QK], "g_k_mla": G["g_k_mla"][:, :MLA_QK],
        "w_pool": G["w_pool"], "pool_scale": G["pool_scale"], "g_mem": G["g_mem"],
        "g_q_x": G["g_q_x"], "g_k_x": G["g_k_x"], "g_ffn": G["g_ffn"], "conv_b": G["conv_b"],
    }
    s_offs = {}
    segs = []
    off = 0
    for n in SMALL:
        r = _as_rows(small[n], 8)
        s_offs[n] = off
        off += r.shape[0]
        segs.append(r)
    segs.append(loss_part)
    loss_row = off
    sbuf = jnp.concatenate(segs, axis=0)
    s_sum = _sum_slots(_exchange_small(sbuf))

    def small_take(buf, n):
        shape = sh[n].shape
        cnt = math.prod(shape)
        return buf[s_offs[n]:s_offs[n] + -(-cnt // LANES)].reshape(-1)[:cnt].reshape(shape)

    loss = s_sum[loss_row, 0]

    me = 4 * lax.axis_index("x") + 2 * lax.axis_index("y") + lax.axis_index("c")
    grads, deltas, new_m, new_v = {}, {}, {}, {}
    for n in BIG:
        shape = sh[n].shape
        own = lax.dynamic_index_in_dim(_dev_blocks(n, G[n]), me, 0, keepdims=False)
        flat = lambda a: a.reshape(-1, shape[-1])
        outs = _adamw("adamw_" + n, flat(sh[n]), flat(own), got[n].reshape(N_DEV - 1, -1, shape[-1]),
                      flat(mom_m[n]), flat(mom_v[n]))
        grads[n], deltas[n], new_m[n], new_v[n] = (o.reshape(shape) for o in outs)

    for n in SMALL:
        grads[n] = small_take(s_sum, n)
    d_s, m_s, v_s = _adamw_small([sh[n] for n in SMALL], [grads[n] for n in SMALL],
                                 [mom_m[n] for n in SMALL], [mom_v[n] for n in SMALL])
    for j, n in enumerate(SMALL):
        deltas[n], new_m[n], new_v[n] = d_s[j], m_s[j], v_s[j]

    lead = lambda n, a: a.reshape(given[n].shape)
    return (loss, grad_x[None],
            *[lead(n, grads[n]) for n in WEIGHTS], *[lead(n, deltas[n]) for n in WEIGHTS],
            *[lead(n, new_m[n]) for n in WEIGHTS], *[lead(n, new_v[n]) for n in WEIGHTS])
```

```python
import math

import jax
import jax.numpy as jnp
from jax import lax
from jax.experimental import pallas as pl
from jax.experimental.pallas import tpu as pltpu

F32 = jnp.float32
BF16 = jnp.bfloat16

D_MODEL = 2048
D_FF = 5632
POOL_WIDTH = 512
POOL_WINDOWS = (2, 4, 8, 16)
POOL_HALO = 16
MLA_HEADS = 8
MLA_NOPE = 128
MLA_ROPE = 64
MLA_QK = MLA_NOPE + MLA_ROPE
MLA_QK_PAD = 256
MLA_V = 128
Q_LORA = 512
KV_LORA = 256
X_HEADS = 4
X_HEAD_DIM = 128
X_WIDTH = 512
MEM_LEN = 256
ROPE_THETA = 10000.0
NORM_EPS = 1e-6
CONV_HALO = 16
IN_COLS = 1856
Z_COLS = 1920
Z_POOL, Z_Q, Z_MQ, Z_KV, Z_KR = 0, 512, 1024, 1536, 1792

ADAM_LR = 0.001
ADAM_B1 = 0.9
ADAM_B2 = 0.999
ADAM_EPS = 1e-08
ADAM_WD = 0.01
ADAM_STEP = 10

N_DEV = 8
LANES = 128
VMEM_LIMIT = 56 * 1024 * 1024

BIG = ("w_in", "w_q_up", "w_kv_up", "w_mem_kv", "w_o", "w_gate", "w_up", "conv_w", "w_down")
SCATTER_EARLY = ("w_down", "w_gate", "w_up", "conv_w", "w_o", "w_mem_kv")
SCATTER_LATE = ("w_in", "w_q_up", "w_kv_up")
EARLY_GATHER_KINDS = ("rows", "cols", "cols")
FFN_GATHER_KINDS = ("lead", "lead", "rows", "lead")
MIXER_GATHER_KINDS = ("rows", "rows")
SMALL = ("g_mix", "g_q_lat", "g_kv_lat", "g_q_mla", "g_k_mla", "w_pool", "pool_scale", "g_mem",
         "g_q_x", "g_k_x", "g_ffn", "conv_b")
WEIGHTS = ("g_mix", "w_in", "g_q_lat", "w_q_up", "g_kv_lat", "w_kv_up", "g_q_mla", "g_k_mla", "w_pool",
           "pool_scale", "g_mem", "w_mem_kv", "g_q_x", "g_k_x", "w_o", "g_ffn", "w_gate", "w_up", "conv_w",
           "conv_b", "w_down")


def _tile(n, t):
    if n <= t:
        return n
    for c in range(t - t % LANES, 0, -LANES):
        if n % c == 0:
            return c
    return n


def _params(*sem):
    return pltpu.CompilerParams(dimension_semantics=sem, vmem_limit_bytes=VMEM_LIMIT)


def _full(shape):
    nd = len(shape)
    return pl.BlockSpec(shape, lambda *_: (0,) * nd)


def _rows(tr, w, cb=0):
    return pl.BlockSpec((tr, w), lambda i: (i, cb))


def _rms_hat(x, n):
    r = lax.rsqrt(jnp.sum(x * x, axis=-1, keepdims=True) / n + NORM_EPS)
    return x * r, r


def _rms_bwd(dy, xhat, r, g, n):
    dxh = dy * g
    dx = r * (dxh - xhat * (jnp.sum(dxh * xhat, axis=-1, keepdims=True) / n))
    return dx, jnp.sum(dy * xhat, axis=0, keepdims=True)


def _mm(name, a, b, *, ta=False, tb=False, add=None, out_dtype=F32, also_bf16=False, tm=512, tn=512, tk=512):
    if ta:
        K, M = a.shape
    else:
        M, K = a.shape
    if tb:
        N, K2 = b.shape
    else:
        K2, N = b.shape
    assert K == K2, (name, a.shape, b.shape)
    tm, tn, tk = _tile(M, tm), _tile(N, tn), _tile(K, tk)
    nk = K // tk
    grid = (M // tm, N // tn, nk)
    a_spec = pl.BlockSpec((tk, tm), lambda i, j, k: (k, i)) if ta else pl.BlockSpec((tm, tk), lambda i, j, k: (i, k))
    b_spec = pl.BlockSpec((tn, tk), lambda i, j, k: (j, k)) if tb else pl.BlockSpec((tk, tn), lambda i, j, k: (k, j))
    o_spec = pl.BlockSpec((tm, tn), lambda i, j, k: (i, j))
    dims = (((0,) if ta else (1,), (1,) if tb else (0,)), ((), ()))
    has_add = add is not None

    n_in = 3 if has_add else 2
    n_out = 2 if also_bf16 else 1

    def body(*refs):
        a_ref, b_ref = refs[:2]
        add_ref = refs[2] if has_add else None
        o_refs = refs[n_in:n_in + n_out]

        def write(r):
            if has_add:
                r = r + add_ref[...]
            for o_ref in o_refs:
                o_ref[...] = r.astype(o_ref.dtype)

        part = lax.dot_general(a_ref[...], b_ref[...], dims, preferred_element_type=F32)
        if nk == 1:
            write(part)
            return
        acc = refs[-1]
        k = pl.program_id(2)

        @pl.when(k == 0)
        def _():
            acc[...] = part

        @pl.when(k > 0)
        def _():
            acc[...] += part

        @pl.when(k == nk - 1)
        def _():
            write(acc[...])

    ins = [a, b] + ([add] if has_add else [])
    in_specs = [a_spec, b_spec] + ([o_spec] if has_add else [])
    out_shape = [jax.ShapeDtypeStruct((M, N), out_dtype)] + ([jax.ShapeDtypeStruct((M, N), BF16)] if also_bf16 else [])
    outs = pl.pallas_call(
        body, name=name, grid=grid, in_specs=in_specs, out_specs=[o_spec] * n_out, out_shape=out_shape,
        scratch_shapes=[pltpu.VMEM((tm, tn), F32)] if nk > 1 else [],
        compiler_params=_params("parallel", "parallel", "arbitrary"),
    )(*ins)
    return outs if also_bf16 else outs[0]


def _rms_fwd(name, x, g, shards, kinds, tr=512):
    S, n = x.shape
    tr = _tile(S, tr)
    n_steps = S // tr
    nw = len(shards)

    def body(x_ref, g_ref, *rest):
        x_refs, h_ref, g_refs = rest[:nw], rest[nw], rest[nw + 1:2 * nw + 1]
        gather = _Gather(x_refs, g_refs, kinds, *rest[2 * nw + 1:])
        i = pl.program_id(0)
        pl.when(i == 0)(gather.start)
        pl.when(i == n_steps // 2)(gather.forward)
        xhat, _ = _rms_hat(x_ref[...], n)
        h_ref[...] = (xhat * g_ref[...]).astype(BF16)
        pl.when(i == n_steps - 1)(gather.finish)

    return pl.pallas_call(
        body, name=name, grid=(n_steps,),
        in_specs=[_rows(tr, n), _full((1, n))] + [_ANY] * nw, out_specs=[_rows(tr, n)] + [_ANY] * nw,
        out_shape=[jax.ShapeDtypeStruct((S, n), BF16)]
        + [jax.ShapeDtypeStruct(_gather_out_shape(kd, sd.shape), sd.dtype) for kd, sd in zip(kinds, shards)],
        scratch_shapes=[pltpu.SemaphoreType.DMA((nw, 7)), pltpu.SemaphoreType.DMA((nw, 7)),
                        pltpu.SemaphoreType.DMA((nw,))],
        compiler_params=_params("arbitrary"),
    )(x, g, *shards)


def _mm_up_dx_rms(du, w, part, x, g, extra, tm=256):
    S, K = du.shape
    n = w.shape[0]
    tm = _tile(S, tm)

    def body(du_ref, w_ref, p_ref, x_ref, g_ref, ex_ref, dx_ref, dxb_ref, dg_ref):
        @pl.when(pl.program_id(0) == 0)
        def _():
            dg_ref[...] = jnp.zeros_like(dg_ref)

        dh = lax.dot_general(du_ref[...], w_ref[...], _NT, preferred_element_type=F32) + p_ref[...]
        xhat, r = _rms_hat(x_ref[...], n)
        dx, dg = _rms_bwd(dh, xhat, r, g_ref[...], n)
        dx = dx + ex_ref[...]
        dx_ref[...] = dx
        dxb_ref[...] = dx.astype(BF16)
        dg_ref[...] += dg

    return pl.pallas_call(
        body, name="mm_up_dx_rms", grid=(S // tm,),
        in_specs=[_rows(tm, K), pl.BlockSpec((n, K), lambda i: (0, 0), pipeline_mode=pl.Buffered(1)),
                  _rows(tm, n), _rows(tm, n), _full((1, n)), _rows(tm, n)],
        out_specs=[_rows(tm, n), _rows(tm, n), _full((1, n))],
        out_shape=[jax.ShapeDtypeStruct((S, n), F32), jax.ShapeDtypeStruct((S, n), BF16),
                   jax.ShapeDtypeStruct((1, n), F32)],
        compiler_params=_params("arbitrary"),
    )(du, w, part, x, g, extra)


def _lat_fwd(z, gq, gkv, tr=512):
    S = z.shape[0]
    tr = _tile(S, tr)

    def body(zq_ref, zkv_ref, gq_ref, gkv_ref, ql_ref, kvl_ref):
        xq, _ = _rms_hat(zq_ref[...], Q_LORA)
        ql_ref[...] = (xq * gq_ref[...]).astype(BF16)
        xkv, _ = _rms_hat(zkv_ref[...], KV_LORA)
        kvl_ref[...] = (xkv * gkv_ref[...]).astype(BF16)

    return pl.pallas_call(
        body, name="lat_fwd", grid=(S // tr,),
        in_specs=[_rows(tr, Q_LORA, Z_Q // Q_LORA), _rows(tr, KV_LORA, Z_KV // KV_LORA),
                  _full((1, Q_LORA)), _full((1, KV_LORA))],
        out_specs=[_rows(tr, Q_LORA), _rows(tr, KV_LORA)],
        out_shape=[jax.ShapeDtypeStruct((S, Q_LORA), BF16), jax.ShapeDtypeStruct((S, KV_LORA), BF16)],
        compiler_params=_params("parallel"),
    )(z, z, gq, gkv)


def _lat_bwd(z, gq, gkv, dql, dkvl, tr=512):
    S = z.shape[0]
    tr = _tile(S, tr)

    def body(zq_ref, zkv_ref, gq_ref, gkv_ref, dql_ref, dkvl_ref, dzq_ref, dzkv_ref, dgq_ref, dgkv_ref):
        @pl.when(pl.program_id(0) == 0)
        def _():
            dgq_ref[...] = jnp.zeros_like(dgq_ref)
            dgkv_ref[...] = jnp.zeros_like(dgkv_ref)

        xq, rq = _rms_hat(zq_ref[...], Q_LORA)
        dx, dg = _rms_bwd(dql_ref[...], xq, rq, gq_ref[...], Q_LORA)
        dzq_ref[...] = dx.astype(BF16)
        dgq_ref[...] += dg
        xkv, rkv = _rms_hat(zkv_ref[...], KV_LORA)
        dx, dg = _rms_bwd(dkvl_ref[...], xkv, rkv, gkv_ref[...], KV_LORA)
        dzkv_ref[...] = dx.astype(BF16)
        dgkv_ref[...] += dg

    return pl.pallas_call(
        body, name="lat_bwd", grid=(S // tr,),
        in_specs=[_rows(tr, Q_LORA, Z_Q // Q_LORA), _rows(tr, KV_LORA, Z_KV // KV_LORA),
                  _full((1, Q_LORA)), _full((1, KV_LORA)), _rows(tr, Q_LORA), _rows(tr, KV_LORA)],
        out_specs=[_rows(tr, Q_LORA), _rows(tr, KV_LORA), _full((1, Q_LORA)), _full((1, KV_LORA))],
        out_shape=[jax.ShapeDtypeStruct((S, Q_LORA), BF16), jax.ShapeDtypeStruct((S, KV_LORA), BF16),
                   jax.ShapeDtypeStruct((1, Q_LORA), F32), jax.ShapeDtypeStruct((1, KV_LORA), F32)],
        compiler_params=_params("arbitrary"),
    )(z, z, gq, gkv, dql, dkvl)


def _pool_d(ext, cur, t0, tr):
    t = t0 + lax.broadcasted_iota(jnp.int32, (tr, 1), 0)
    ds = []
    for gi, w in enumerate(POOL_WINDOWS):
        s = ext[:, gi * LANES:(gi + 1) * LANES]
        sh = 1
        while sh < w:
            s = s + pltpu.roll(s, sh, 0)
            sh *= 2
        cnt = jnp.minimum(t + 1, w).astype(F32)
        ds.append(s[POOL_HALO:] / cnt - cur[:, gi * LANES:(gi + 1) * LANES])
    return ds


def _pool_fwd(z, wp, scale, tr=512):
    S = z.shape[0]
    tr = _tile(S, tr)
    rb = tr // POOL_HALO

    def body(z_ref, zp_ref, wp_ref, sc_ref, y_ref):
        i = pl.program_id(0)
        cur = z_ref[...]
        halo = jnp.where(i == 0, 0.0, zp_ref[...])
        ds = _pool_d(jnp.concatenate([halo, cur], axis=0), cur, i * tr, tr)
        for gi in range(len(POOL_WINDOWS)):
            y = jnp.dot(ds[gi].astype(BF16), wp_ref[gi], preferred_element_type=F32)
            y_ref[:, gi * LANES:(gi + 1) * LANES] = (y * sc_ref[:, gi * LANES:(gi + 1) * LANES]).astype(BF16)

    return pl.pallas_call(
        body, name="pool_fwd", grid=(S // tr,),
        in_specs=[_rows(tr, POOL_WIDTH, Z_POOL // POOL_WIDTH),
                  pl.BlockSpec((POOL_HALO, POOL_WIDTH), lambda i: (jnp.maximum(i * rb - 1, 0), 0)),
                  _full(wp.shape), _full((1, POOL_WIDTH))],
        out_specs=_rows(tr, POOL_WIDTH),
        out_shape=jax.ShapeDtypeStruct((S, POOL_WIDTH), BF16),
        compiler_params=_params("parallel"),
    )(z, z, wp, scale)


def _pool_bwd(z, dyc, wp, scale, dy_cb, tr=512):
    S = z.shape[0]
    tr = _tile(S, tr)
    rb = tr // POOL_HALO
    nhb = S // POOL_HALO
    ng = len(POOL_WINDOWS)

    def body(z_ref, zp_ref, dy_ref, dyn_ref, wp_ref, sc_ref, dz_ref, dwp_ref, dsc_ref):
        i = pl.program_id(0)

        @pl.when(i == 0)
        def _():
            dwp_ref[...] = jnp.zeros_like(dwp_ref)
            dsc_ref[...] = jnp.zeros_like(dsc_ref)

        cur = z_ref[...]
        halo = jnp.where(i == 0, 0.0, zp_ref[...])
        ds = _pool_d(jnp.concatenate([halo, cur], axis=0), cur, i * tr, tr)
        dy_cur = dy_ref[...]
        dy_next = jnp.where(i == pl.num_programs(0) - 1, 0.0, dyn_ref[...])
        dy_ext = jnp.concatenate([dy_cur, dy_next], axis=0)
        n_ext = tr + POOL_HALO
        t_ext = i * tr + lax.broadcasted_iota(jnp.int32, (n_ext, 1), 0)
        for gi, w in enumerate(POOL_WINDOWS):
            cols = slice(gi * LANES, (gi + 1) * LANES)
            d_b = ds[gi].astype(BF16)
            y_pre = jnp.dot(d_b, wp_ref[gi], preferred_element_type=F32)
            dsc_ref[:, cols] += jnp.sum(dy_cur[:, cols] * y_pre, axis=0, keepdims=True)
            dys = (dy_ext[:, cols] * sc_ref[:, cols]).astype(BF16)
            dwp_ref[gi] += lax.dot_general(d_b, dys[:tr], (((0,), (0,)), ((), ())), preferred_element_type=F32)
            dd = lax.dot_general(dys, wp_ref[gi], (((1,), (1,)), ((), ())), preferred_element_type=F32)
            s = dd / jnp.minimum(t_ext + 1, w).astype(F32)
            sh = 1
            while sh < w:
                s = s + pltpu.roll(s, n_ext - sh, 0)
                sh *= 2
            dz_ref[:, cols] = (s[:tr] - dd[:tr]).astype(BF16)

    return pl.pallas_call(
        body, name="pool_bwd", grid=(S // tr,),
        in_specs=[_rows(tr, POOL_WIDTH, Z_POOL // POOL_WIDTH),
                  pl.BlockSpec((POOL_HALO, POOL_WIDTH), lambda i: (jnp.maximum(i * rb - 1, 0), 0)),
                  _rows(tr, POOL_WIDTH, dy_cb),
                  pl.BlockSpec((POOL_HALO, POOL_WIDTH), lambda i: (jnp.minimum((i + 1) * rb, nhb - 1), dy_cb)),
                  _full(wp.shape), _full((1, POOL_WIDTH))],
        out_specs=[_rows(tr, POOL_WIDTH), _full((ng, LANES, LANES)), _full((1, POOL_WIDTH))],
        out_shape=[jax.ShapeDtypeStruct((S, POOL_WIDTH), BF16), jax.ShapeDtypeStruct((ng, LANES, LANES), F32),
                   jax.ShapeDtypeStruct((1, POOL_WIDTH), F32)],
        compiler_params=_params("arbitrary"),
    )(z, z, dyc, dyc, wp, scale)


def _rope_tables(S):
    half = MLA_ROPE // 2
    inv_freq = 1.0 / (ROPE_THETA ** (jnp.arange(half, dtype=F32) / half))
    ang = jnp.arange(S).astype(F32)[:, None] * inv_freq[None, :]
    cos, sin = jnp.cos(ang), jnp.sin(ang)
    z32 = jnp.zeros((S, half), F32)
    z64 = jnp.zeros((S, LANES - MLA_ROPE), F32)
    cos_t = jnp.concatenate([cos, cos, z64], axis=1)
    sin_a = jnp.concatenate([-sin, z32, z64], axis=1)
    sin_b = jnp.concatenate([z32, sin, z64], axis=1)
    return cos_t, sin_a, sin_b


def _rope(x, cos_t, sin_a, sin_b):
    return x * cos_t + pltpu.roll(x, LANES - 32, 1) * sin_a + pltpu.roll(x, 32, 1) * sin_b


def _rope_t(d, cos_t, sin_a, sin_b):
    return d * cos_t + pltpu.roll(d * sin_a, 32, 1) + pltpu.roll(d * sin_b, LANES - 32, 1)


def _mla_prep(q_raw, kv_raw, z, tabs, gq, gk, shards, kinds, tr=256):
    S = z.shape[0]
    tr = _tile(S, tr)
    n_steps = S // tr
    scale = 1.0 / math.sqrt(MLA_QK)
    W = MLA_HEADS * MLA_QK_PAD
    nw = len(shards)

    def body(q_ref, kv_ref, kr_ref, c_ref, sa_ref, sb_ref, gq_ref, gk_ref, *rest):
        x_refs, (qo_ref, ko_ref, vo_ref), g_refs = rest[:nw], rest[nw:nw + 3], rest[nw + 3:2 * nw + 3]
        gather = _Gather(x_refs, g_refs, kinds, *rest[2 * nw + 3:])
        i = pl.program_id(0)
        pl.when(i == 0)(gather.start)
        pl.when(i == (3 * n_steps) // 4)(gather.forward)
        tab = (c_ref[...], sa_ref[...], sb_ref[...])
        kr = kr_ref[...]
        kr_ss = jnp.sum(kr * kr, axis=-1, keepdims=True)
        gqn, gqr = gq_ref[:, :LANES], gq_ref[:, LANES:]
        gkn, gkr = gk_ref[:, :LANES], gk_ref[:, LANES:]
        for h in range(MLA_HEADS):
            c0 = h * MLA_QK_PAD
            qn = q_ref[:, c0:c0 + LANES].astype(F32)
            qr = q_ref[:, c0 + LANES:c0 + 2 * LANES].astype(F32)
            r = lax.rsqrt((jnp.sum(qn * qn, -1, keepdims=True) + jnp.sum(qr * qr, -1, keepdims=True)) / MLA_QK
                          + NORM_EPS)
            qo_ref[h, :, :LANES] = (qn * r * gqn * scale).astype(BF16)
            qo_ref[h, :, LANES:] = (_rope(qr * r * gqr, *tab) * scale).astype(BF16)
            kn = kv_ref[:, c0:c0 + LANES].astype(F32)
            r = lax.rsqrt((jnp.sum(kn * kn, -1, keepdims=True) + kr_ss) / MLA_QK + NORM_EPS)
            ko_ref[h, :, :LANES] = (kn * r * gkn).astype(BF16)
            ko_ref[h, :, LANES:] = _rope(kr * r * gkr, *tab).astype(BF16)
            vo_ref[h] = kv_ref[:, c0 + LANES:c0 + 2 * LANES].astype(BF16)
        pl.when(i == n_steps - 1)(gather.finish)

    hs = lambda w: pl.BlockSpec((MLA_HEADS, tr, w), lambda i: (0, i, 0))
    return pl.pallas_call(
        body, name="mla_prep", grid=(n_steps,),
        in_specs=[_rows(tr, W), _rows(tr, W), _rows(tr, LANES, Z_KR // LANES),
                  _rows(tr, LANES), _rows(tr, LANES), _rows(tr, LANES),
                  _full((1, MLA_QK_PAD)), _full((1, MLA_QK_PAD))] + [_ANY] * nw,
        out_specs=[hs(MLA_QK_PAD), hs(MLA_QK_PAD), hs(MLA_V)] + [_ANY] * nw,
        out_shape=[jax.ShapeDtypeStruct((MLA_HEADS, S, MLA_QK_PAD), BF16),
                   jax.ShapeDtypeStruct((MLA_HEADS, S, MLA_QK_PAD), BF16),
                   jax.ShapeDtypeStruct((MLA_HEADS, S, MLA_V), BF16)]
        + [jax.ShapeDtypeStruct(_gather_out_shape(kd, sd.shape), sd.dtype) for kd, sd in zip(kinds, shards)],
        scratch_shapes=[pltpu.SemaphoreType.DMA((nw, 7)), pltpu.SemaphoreType.DMA((nw, 7)),
                        pltpu.SemaphoreType.DMA((nw,))],
        compiler_params=_params("arbitrary"),
    )(q_raw, kv_raw, z, *tabs, gq, gk, *shards)


def _mla_prep_bwd(dq, dk, dv, q_raw, kv_raw, z, tabs, gq, gk, tr=256):
    S = z.shape[0]
    tr = _tile(S, tr)
    scale = 1.0 / math.sqrt(MLA_QK)
    W = MLA_HEADS * MLA_QK_PAD

    def body(dq_ref, dk_ref, dv_ref, q_ref, kv_ref, kr_ref, c_ref, sa_ref, sb_ref, gq_ref, gk_ref,
             dqr_ref, dkvr_ref, dkr_ref, dgq_ref, dgk_ref):
        @pl.when(pl.program_id(0) == 0)
        def _():
            dgq_ref[...] = jnp.zeros_like(dgq_ref)
            dgk_ref[...] = jnp.zeros_like(dgk_ref)

        tab = (c_ref[...], sa_ref[...], sb_ref[...])
        kr = kr_ref[...]
        kr_ss = jnp.sum(kr * kr, axis=-1, keepdims=True)
        gqn, gqr = gq_ref[:, :LANES], gq_ref[:, LANES:]
        gkn, gkr = gk_ref[:, :LANES], gk_ref[:, LANES:]
        dkr_sum = jnp.zeros((tr, LANES), F32)
        dgq_n = jnp.zeros((1, LANES), F32)
        dgq_r = jnp.zeros((1, LANES), F32)
        dgk_n = jnp.zeros((1, LANES), F32)
        dgk_r = jnp.zeros((1, LANES), F32)

        def head_norm_bwd(xn, xr, r, dyn, dyr, gn, gr):
            hn, hr = xn * r, xr * r
            dxn, dxr = dyn * gn, dyr * gr
            mt = (jnp.sum(dxn * hn, -1, keepdims=True) + jnp.sum(dxr * hr, -1, keepdims=True)) / MLA_QK
            return (r * (dxn - hn * mt), r * (dxr - hr * mt),
                    jnp.sum(dyn * hn, axis=0, keepdims=True), jnp.sum(dyr * hr, axis=0, keepdims=True))

        for h in range(MLA_HEADS):
            c0 = h * MLA_QK_PAD
            qn = q_ref[:, c0:c0 + LANES].astype(F32)
            qr = q_ref[:, c0 + LANES:c0 + 2 * LANES].astype(F32)
            r = lax.rsqrt((jnp.sum(qn * qn, -1, keepdims=True) + jnp.sum(qr * qr, -1, keepdims=True)) / MLA_QK
                          + NORM_EPS)
            dyn = dq_ref[h, :, :LANES] * scale
            dyr = _rope_t(dq_ref[h, :, LANES:] * scale, *tab)
            dxn, dxr, gn_, gr_ = head_norm_bwd(qn, qr, r, dyn, dyr, gqn, gqr)
            dgq_n += gn_
            dgq_r += gr_
            dqr_ref[:, c0:c0 + LANES] = dxn.astype(BF16)
            dqr_ref[:, c0 + LANES:c0 + 2 * LANES] = dxr.astype(BF16)

            kn = kv_ref[:, c0:c0 + LANES].astype(F32)
            r = lax.rsqrt((jnp.sum(kn * kn, -1, keepdims=True) + kr_ss) / MLA_QK + NORM_EPS)
            dyn = dk_ref[h, :, :LANES]
            dyr = _rope_t(dk_ref[h, :, LANES:], *tab)
            dxn, dxr, gn_, gr_ = head_norm_bwd(kn, kr, r, dyn, dyr, gkn, gkr)
            dgk_n += gn_
            dgk_r += gr_
            dkr_sum += dxr
            dkvr_ref[:, c0:c0 + LANES] = dxn.astype(BF16)
            dkvr_ref[:, c0 + LANES:c0 + 2 * LANES] = dv_ref[h].astype(BF16)

        dkr_ref[...] = dkr_sum.astype(BF16)
        dgq_ref[:, :LANES] += dgq_n
        dgq_ref[:, LANES:] += dgq_r
        dgk_ref[:, :LANES] += dgk_n
        dgk_ref[:, LANES:] += dgk_r

    hs = lambda w: pl.BlockSpec((MLA_HEADS, tr, w), lambda i: (0, i, 0))
    return pl.pallas_call(
        body, name="mla_prep_bwd", grid=(S // tr,),
        in_specs=[hs(MLA_QK_PAD), hs(MLA_QK_PAD), hs(MLA_V), _rows(tr, W), _rows(tr, W),
                  _rows(tr, LANES, Z_KR // LANES), _rows(tr, LANES), _rows(tr, LANES), _rows(tr, LANES),
                  _full((1, MLA_QK_PAD)), _full((1, MLA_QK_PAD))],
        out_specs=[_rows(tr, W), _rows(tr, W), _rows(tr, LANES), _full((1, MLA_QK_PAD)), _full((1, MLA_QK_PAD))],
        out_shape=[jax.ShapeDtypeStruct((S, W), BF16), jax.ShapeDtypeStruct((S, W), BF16),
                   jax.ShapeDtypeStruct((S, LANES), BF16),
                   jax.ShapeDtypeStruct((1, MLA_QK_PAD), F32), jax.ShapeDtypeStruct((1, MLA_QK_PAD), F32)],
        compiler_params=_params("arbitrary"),
    )(dq, dk, dv, q_raw, kv_raw, z, *tabs, gq, gk)


_NT = (((1,), (1,)), ((), ()))
_TN = (((0,), (0,)), ((), ()))


def _flash_fwd(q, k, v, shards, kinds, tq=1024, tk=1024, ts=512):
    H, S, dq = q.shape
    dv = v.shape[-1]
    tq, tk, ts = _tile(S, tq), _tile(S, tk), _tile(S, ts)
    assert tq % tk == 0 and tq % ts == 0 and dv == LANES
    nq, nd = S // tq, tq // tk
    nw = len(shards)

    def body(q_ref, k_ref, v_ref, *rest):
        x_refs, (o_ref, lse_ref), g_refs = rest[:nw], rest[nw:nw + 2], rest[nw + 2:2 * nw + 2]
        m_sc, l_sc, acc_sc, send_sems, recv_sems, local_sems = rest[2 * nw + 2:]
        h, i = pl.program_id(0), pl.program_id(1)
        gather = _Gather(x_refs, g_refs, kinds, send_sems, recv_sems, local_sems)
        pl.when(jnp.logical_and(h == 0, i == 0))(gather.start)
        pl.when(jnp.logical_and(h == (7 * H) // 8, i == 0))(gather.forward)

        m_sc[...] = jnp.full_like(m_sc, -jnp.inf)
        l_sc[...] = jnp.zeros_like(l_sc)
        acc_sc[...] = jnp.zeros_like(acc_sc)

        def step(off, width, r0, masked):
            rows = slice(r0, tq)
            s = lax.dot_general(q_ref[rows, :], k_ref[pl.ds(off, width), :], _NT, preferred_element_type=F32)
            if masked:
                row = i * tq + r0 + lax.broadcasted_iota(jnp.int32, s.shape, 0)
                col = off + lax.broadcasted_iota(jnp.int32, s.shape, 1)
                s = jnp.where(col <= row, s, -jnp.inf)
            parts = [s[:, c * LANES:(c + 1) * LANES] for c in range(width // LANES)]
            m_cur = parts[0]
            for pt in parts[1:]:
                m_cur = jnp.maximum(m_cur, pt)
            m_prev = m_sc[rows, :]
            m_new = jnp.maximum(m_prev, jnp.max(m_cur, axis=-1, keepdims=True))
            alpha = jnp.exp(m_prev - m_new)
            ps = [jnp.exp(pt - m_new) for pt in parts]
            l_new = alpha * l_sc[rows, :]
            for pc in ps:
                l_new = l_new + pc
            l_sc[rows, :] = l_new
            p = jnp.concatenate(ps, axis=1).astype(BF16)
            acc_sc[rows, :] = alpha * acc_sc[rows, :] + jnp.dot(p, v_ref[pl.ds(off, width), :],
                                                                preferred_element_type=F32)
            m_sc[rows, :] = m_new

        def full_step(j, carry):
            step(pl.multiple_of(j * tk, tk), tk, 0, False)
            return carry

        lax.fori_loop(0, i * nd, full_step, 0)
        for d in range(tq // ts):
            step(pl.multiple_of(i * tq + d * ts, ts), ts, d * ts, True)
        l = jnp.sum(l_sc[...], axis=-1, keepdims=True)
        o_ref[...] = (acc_sc[...] / l).astype(BF16)
        lse_ref[...] = m_sc[...] + jnp.log(l)
        pl.when(jnp.logical_and(h == H - 1, i == nq - 1))(gather.finish)

    return pl.pallas_call(
        body, name="flash_fwd", grid=(H, nq),
        in_specs=[pl.BlockSpec((None, tq, dq), lambda h, i: (h, i, 0)),
                  pl.BlockSpec((None, S, dq), lambda h, i: (h, 0, 0)),
                  pl.BlockSpec((None, S, dv), lambda h, i: (h, 0, 0))] + [_ANY] * nw,
        out_specs=[pl.BlockSpec((tq, dv), lambda h, i: (i, h)),
                   pl.BlockSpec((None, tq, LANES), lambda h, i: (h, i, 0))] + [_ANY] * nw,
        out_shape=[jax.ShapeDtypeStruct((S, H * dv), BF16), jax.ShapeDtypeStruct((H, S, LANES), F32)]
        + [jax.ShapeDtypeStruct(_gather_out_shape(kd, sd.shape), sd.dtype) for kd, sd in zip(kinds, shards)],
        scratch_shapes=[pltpu.VMEM((tq, LANES), F32), pltpu.VMEM((tq, LANES), F32), pltpu.VMEM((tq, dv), F32),
                        pltpu.SemaphoreType.DMA((nw, 7)), pltpu.SemaphoreType.DMA((nw, 7)),
                        pltpu.SemaphoreType.DMA((nw,))],
        compiler_params=_params("arbitrary", "arbitrary"),
    )(q, k, v, *shards)


def _flash_bwd(q, k, v, dyc, cb0, o, lse, sends, tq=1024, tk=1024, ts=512):
    H, S, dq = q.shape
    dv = v.shape[-1]
    tq, tk, ts = _tile(S, tq), _tile(S, tk), _tile(S, ts)
    assert tq % tk == 0 and tq % ts == 0
    nq, nd = S // tq, tq // tk
    nw = len(sends)

    def body(q_ref, k_ref, v_ref, dy_ref, o_ref, lse_ref, *rest):
        s_refs, (dq_ref, dk_ref, dv_ref), r_refs = rest[:nw], rest[nw:nw + 3], rest[nw + 3:2 * nw + 3]
        do_sc, dl_sc, send_sems, recv_sems = rest[2 * nw + 3:]
        h, i = pl.program_id(0), pl.program_id(1)
        scatter = _Scatter(s_refs, r_refs, send_sems, recv_sems)
        pl.when(jnp.logical_and(h == 0, i == 0))(scatter.start)

        @pl.when(i == 0)
        def _():
            dk_ref[...] = jnp.zeros_like(dk_ref)
            dv_ref[...] = jnp.zeros_like(dv_ref)

        dq_ref[...] = jnp.zeros_like(dq_ref)
        dy = dy_ref[...]
        do_sc[...] = dy.astype(BF16)
        dl_sc[...] = jnp.broadcast_to(jnp.sum(dy * o_ref[...].astype(F32), axis=-1, keepdims=True), (tq, LANES))

        def step(off, width, r0, masked):
            rows = slice(r0, tq)
            qv = q_ref[rows, :]
            dov = do_sc[rows, :]
            kb = k_ref[pl.ds(off, width), :]
            s = lax.dot_general(qv, kb, _NT, preferred_element_type=F32)
            if masked:
                row = i * tq + r0 + lax.broadcasted_iota(jnp.int32, s.shape, 0)
                col = off + lax.broadcasted_iota(jnp.int32, s.shape, 1)
                s = jnp.where(col <= row, s, -jnp.inf)
            p = jnp.exp(s - lse_ref[rows, :1])
            dv_ref[pl.ds(off, width), :] += lax.dot_general(p.astype(BF16), dov, _TN, preferred_element_type=F32)
            dp = lax.dot_general(dov, v_ref[pl.ds(off, width), :], _NT, preferred_element_type=F32)
            ds = (p * (dp - dl_sc[rows, :1])).astype(BF16)
            dk_ref[pl.ds(off, width), :] += lax.dot_general(ds, qv, _TN, preferred_element_type=F32)
            dq_ref[rows, :] += jnp.dot(ds, kb, preferred_element_type=F32)

        def full_step(j, carry):
            step(pl.multiple_of(j * tk, tk), tk, 0, False)
            return carry

        lax.fori_loop(0, i * nd, full_step, 0)
        for d in range(tq // ts):
            step(pl.multiple_of(i * tq + d * ts, ts), ts, d * ts, True)
        pl.when(jnp.logical_and(h == H - 1, i == nq - 1))(scatter.finish)

    return pl.pallas_call(
        body, name="flash_bwd", grid=(H, nq),
        in_specs=[pl.BlockSpec((None, tq, dq), lambda h, i: (h, i, 0)),
                  pl.BlockSpec((None, S, dq), lambda h, i: (h, 0, 0), pipeline_mode=pl.Buffered(1)),
                  pl.BlockSpec((None, S, dv), lambda h, i: (h, 0, 0), pipeline_mode=pl.Buffered(1)),
                  pl.BlockSpec((tq, dv), lambda h, i: (i, h + cb0)),
                  pl.BlockSpec((tq, dv), lambda h, i: (i, h)),
                  pl.BlockSpec((None, tq, LANES), lambda h, i: (h, i, 0))] + [_ANY] * nw,
        out_specs=[pl.BlockSpec((None, tq, dq), lambda h, i: (h, i, 0)),
                   pl.BlockSpec((None, S, dq), lambda h, i: (h, 0, 0)),
                   pl.BlockSpec((None, S, dv), lambda h, i: (h, 0, 0))] + [_ANY] * nw,
        out_shape=[jax.ShapeDtypeStruct((H, S, dq), F32), jax.ShapeDtypeStruct((H, S, dq), F32),
                   jax.ShapeDtypeStruct((H, S, dv), F32)]
        + [jax.ShapeDtypeStruct((N_DEV - 1,) + sd.shape[1:], sd.dtype) for sd in sends],
        scratch_shapes=[pltpu.VMEM((tq, dv), BF16), pltpu.VMEM((tq, LANES), F32),
                        pltpu.SemaphoreType.DMA((nw, N_DEV - 1)), pltpu.SemaphoreType.DMA((nw, N_DEV - 1))],
        compiler_params=_params("arbitrary", "arbitrary"),
    )(q, k, v, dyc, o, lse, *sends)


def _mem_norm(mem, g):
    n = mem.shape[1]

    def body(m_ref, g_ref, o_ref):
        xhat, _ = _rms_hat(m_ref[...], n)
        o_ref[...] = (xhat * g_ref[...]).astype(BF16)

    return pl.pallas_call(body, name="mem_norm", out_shape=jax.ShapeDtypeStruct(mem.shape, BF16),
                          compiler_params=pltpu.CompilerParams(vmem_limit_bytes=VMEM_LIMIT))(mem, g)


def _memkv_prep(mkv, gk):
    M = mkv.shape[0]

    def body(mkv_ref, g_ref, k_ref, v_ref):
        for h in range(X_HEADS):
            cols = slice(h * X_HEAD_DIM, (h + 1) * X_HEAD_DIM)
            xhat, _ = _rms_hat(mkv_ref[:, cols], X_HEAD_DIM)
            k_ref[:, cols] = (xhat * g_ref[...]).astype(BF16)
        v_ref[...] = mkv_ref[:, X_WIDTH:].astype(BF16)

    return pl.pallas_call(
        body, name="memkv_prep",
        out_shape=[jax.ShapeDtypeStruct((M, X_WIDTH), BF16), jax.ShapeDtypeStruct((M, X_WIDTH), BF16)],
    )(mkv, gk)


def _memkv_bwd(dk, dv, mkv, gk):
    M = mkv.shape[0]

    def body(dk_ref, dv_ref, mkv_ref, g_ref, dmkv_ref, dg_ref):
        dg = jnp.zeros((1, X_HEAD_DIM), F32)
        for h in range(X_HEADS):
            cols = slice(h * X_HEAD_DIM, (h + 1) * X_HEAD_DIM)
            xhat, r = _rms_hat(mkv_ref[:, cols], X_HEAD_DIM)
            dx, dgh = _rms_bwd(dk_ref[:, cols], xhat, r, g_ref[...], X_HEAD_DIM)
            dmkv_ref[:, cols] = dx.astype(BF16)
            dg += dgh
        dmkv_ref[:, X_WIDTH:] = dv_ref[...].astype(BF16)
        dg_ref[...] = dg

    return pl.pallas_call(
        body, name="memkv_bwd",
        out_shape=[jax.ShapeDtypeStruct((M, 2 * X_WIDTH), BF16), jax.ShapeDtypeStruct((1, X_HEAD_DIM), F32)],
    )(dk, dv, mkv, gk)


def _mem_gain_bwd(mem, dmn):
    n = mem.shape[1]

    def body(m_ref, d_ref, dg_ref):
        xhat, _ = _rms_hat(m_ref[...], n)
        dg_ref[...] = jnp.sum(d_ref[...] * xhat, axis=0, keepdims=True)

    return pl.pallas_call(body, name="mem_gain_bwd", out_shape=jax.ShapeDtypeStruct((1, n), F32),
                          compiler_params=pltpu.CompilerParams(vmem_limit_bytes=VMEM_LIMIT))(mem, dmn)


def _memx_scores(zq_ref, g, kx_ref, h):
    cols = slice(h * X_HEAD_DIM, (h + 1) * X_HEAD_DIM)
    xhat, r = _rms_hat(zq_ref[:, cols], X_HEAD_DIM)
    qn = (xhat * g).astype(BF16)
    s = lax.dot_general(qn, kx_ref[:, cols], _NT, preferred_element_type=F32) * (1.0 / math.sqrt(X_HEAD_DIM))
    p = jnp.exp(s - jnp.max(s, axis=-1, keepdims=True))
    return cols, xhat, r, qn, p, jnp.sum(p, axis=-1, keepdims=True)


def _memx_fwd(z, g, kx, vx, tr=512):
    S = z.shape[0]
    tr = _tile(S, tr)
    M = kx.shape[0]

    def body(zq_ref, g_ref, kx_ref, vx_ref, y_ref):
        for h in range(X_HEADS):
            cols, _, _, _, p, l = _memx_scores(zq_ref, g_ref[...], kx_ref, h)
            o = jnp.dot(p.astype(BF16), vx_ref[:, cols], preferred_element_type=F32)
            y_ref[:, cols] = (o / l).astype(BF16)

    return pl.pallas_call(
        body, name="memx_fwd", grid=(S // tr,),
        in_specs=[_rows(tr, X_WIDTH, Z_MQ // X_WIDTH), _full((1, X_HEAD_DIM)), _full((M, X_WIDTH)),
                  _full((M, X_WIDTH))],
        out_specs=_rows(tr, X_WIDTH),
        out_shape=jax.ShapeDtypeStruct((S, X_WIDTH), BF16),
        compiler_params=_params("parallel"),
    )(z, g, kx, vx)


def _memx_bwd(z, g, kx, vx, dyc, dy_cb, tr=512):
    S = z.shape[0]
    tr = _tile(S, tr)
    M = kx.shape[0]
    scale = 1.0 / math.sqrt(X_HEAD_DIM)

    def body(zq_ref, g_ref, kx_ref, vx_ref, dy_ref, dz_ref, dk_ref, dv_ref, dg_ref):
        @pl.when(pl.program_id(0) == 0)
        def _():
            dk_ref[...] = jnp.zeros_like(dk_ref)
            dv_ref[...] = jnp.zeros_like(dv_ref)
            dg_ref[...] = jnp.zeros_like(dg_ref)

        gv = g_ref[...]
        for h in range(X_HEADS):
            cols, xhat, r, qn, p, l = _memx_scores(zq_ref, gv, kx_ref, h)
            p = p / l
            do = dy_ref[:, cols].astype(BF16)
            dv_ref[:, cols] += lax.dot_general(p.astype(BF16), do, _TN, preferred_element_type=F32)
            dp = lax.dot_general(do, vx_ref[:, cols], _NT, preferred_element_type=F32)
            ds = (p * (dp - jnp.sum(dp * p, axis=-1, keepdims=True)) * scale).astype(BF16)
            dqn = jnp.dot(ds, kx_ref[:, cols], preferred_element_type=F32)
            dk_ref[:, cols] += lax.dot_general(ds, qn, _TN, preferred_element_type=F32)
            dx, dg = _rms_bwd(dqn, xhat, r, gv, X_HEAD_DIM)
            dz_ref[:, cols] = dx.astype(BF16)
            dg_ref[...] += dg

    return pl.pallas_call(
        body, name="memx_bwd", grid=(S // tr,),
        in_specs=[_rows(tr, X_WIDTH, Z_MQ // X_WIDTH), _full((1, X_HEAD_DIM)), _full((M, X_WIDTH)),
                  _full((M, X_WIDTH)), _rows(tr, X_WIDTH, dy_cb)],
        out_specs=[_rows(tr, X_WIDTH), _full((M, X_WIDTH)), _full((M, X_WIDTH)), _full((1, X_HEAD_DIM))],
        out_shape=[jax.ShapeDtypeStruct((S, X_WIDTH), BF16), jax.ShapeDtypeStruct((M, X_WIDTH), F32),
                   jax.ShapeDtypeStruct((M, X_WIDTH), F32), jax.ShapeDtypeStruct((1, X_HEAD_DIM), F32)],
        compiler_params=_params("arbitrary"),
    )(z, g, kx, vx, dyc)


def _conv_gc(g_ext, w_ref, b_ref, n_ext):
    g1 = pltpu.roll(g_ext, 1, 0)
    g2 = pltpu.roll(g_ext, 2, 0)
    gc = b_ref[...] + w_ref[0:1, :] * g2
    gc = gc + w_ref[1:2, :] * g1
    gc = gc + w_ref[2:3, :] * g_ext
    return gc, g1, g2


def _convglu_fwd(g, u, cw, cb, tr=1024, tc=512):
    S, F = g.shape
    tr, tc = _tile(S, tr), _tile(F, tc)
    rb = tr // CONV_HALO

    def body(g_ref, gp_ref, u_ref, w_ref, b_ref, a_ref):
        i = pl.program_id(1)
        halo = jnp.where(i == 0, 0.0, gp_ref[...].astype(F32))
        g_ext = jnp.concatenate([halo, g_ref[...].astype(F32)], axis=0)
        gc, _, _ = _conv_gc(g_ext, w_ref, b_ref, tr + CONV_HALO)
        gc = gc[CONV_HALO:]
        sig = 0.5 * jnp.tanh(0.5 * gc) + 0.5
        a_ref[...] = (gc * sig * u_ref[...].astype(F32)).astype(BF16)

    return pl.pallas_call(
        body, name="convglu_fwd", grid=(F // tc, S // tr),
        in_specs=[pl.BlockSpec((tr, tc), lambda j, i: (i, j)),
                  pl.BlockSpec((CONV_HALO, tc), lambda j, i: (jnp.maximum(i * rb - 1, 0), j)),
                  pl.BlockSpec((tr, tc), lambda j, i: (i, j)),
                  pl.BlockSpec((3, tc), lambda j, i: (0, j)), pl.BlockSpec((1, tc), lambda j, i: (0, j))],
        out_specs=pl.BlockSpec((tr, tc), lambda j, i: (i, j)),
        out_shape=jax.ShapeDtypeStruct((S, F), BF16),
        compiler_params=_params("parallel", "parallel"),
    )(g, g, u, cw, cb)


def _convglu_bwd(g, u, da, cw, cb, tr=1024, tc=512):
    S, F = g.shape
    tr, tc = _tile(S, tr), _tile(F, tc)
    rb = tr // CONV_HALO
    nhb = S // CONV_HALO
    H = CONV_HALO

    def body(g_ref, gp_ref, gn_ref, u_ref, un_ref, da_ref, dan_ref, w_ref, b_ref,
             dg_ref, du_ref, dw_ref, db_ref):
        i = pl.program_id(1)
        last = i == pl.num_programs(1) - 1

        @pl.when(i == 0)
        def _():
            dw_ref[...] = jnp.zeros_like(dw_ref)
            db_ref[...] = jnp.zeros_like(db_ref)

        g_prev = jnp.where(i == 0, 0.0, gp_ref[...].astype(F32))
        g_cur = g_ref[...].astype(F32)
        g_ext = jnp.concatenate([g_prev, g_cur, gn_ref[...].astype(F32)], axis=0)
        gc, g1, g2 = _conv_gc(g_ext, w_ref, b_ref, tr + 2 * H)
        gc = gc[H:]
        ux = jnp.concatenate([u_ref[...], un_ref[...]], axis=0).astype(F32)
        dax = jnp.concatenate([da_ref[...].astype(F32), jnp.where(last, 0.0, dan_ref[...].astype(F32))], axis=0)
        sig = 0.5 * jnp.tanh(0.5 * gc) + 0.5
        du_ref[...] = (dax[:tr] * (gc[:tr] * sig[:tr])).astype(BF16)
        dgc = dax * ux * (sig * (1.0 + gc * (1.0 - sig)))
        n = tr + H
        d1 = pltpu.roll(dgc, n - 1, 0)[:tr]
        d2 = pltpu.roll(dgc, n - 2, 0)[:tr]
        d0 = dgc[:tr]
        dg_ref[...] = (w_ref[2:3, :] * d0 + w_ref[1:2, :] * d1 + w_ref[0:1, :] * d2).astype(BF16)
        db_ref[...] += jnp.sum(d0, axis=0, keepdims=True)
        dw_ref[0:1, :] += jnp.sum(d0 * g2[H:H + tr], axis=0, keepdims=True)
        dw_ref[1:2, :] += jnp.sum(d0 * g1[H:H + tr], axis=0, keepdims=True)
        dw_ref[2:3, :] += jnp.sum(d0 * g_cur, axis=0, keepdims=True)

    cur = pl.BlockSpec((tr, tc), lambda j, i: (i, j))
    prv = pl.BlockSpec((H, tc), lambda j, i: (jnp.maximum(i * rb - 1, 0), j))
    nxt = pl.BlockSpec((H, tc), lambda j, i: (jnp.minimum((i + 1) * rb, nhb - 1), j))
    return pl.pallas_call(
        body, name="convglu_bwd", grid=(F // tc, S // tr),
        in_specs=[cur, prv, nxt, cur, nxt, cur, nxt,
                  pl.BlockSpec((3, tc), lambda j, i: (0, j)), pl.BlockSpec((1, tc), lambda j, i: (0, j))],
        out_specs=[cur, cur, pl.BlockSpec((3, tc), lambda j, i: (0, j)), pl.BlockSpec((1, tc), lambda j, i: (0, j))],
        out_shape=[jax.ShapeDtypeStruct((S, F), BF16), jax.ShapeDtypeStruct((S, F), BF16),
                   jax.ShapeDtypeStruct((3, F), F32), jax.ShapeDtypeStruct((1, F), F32)],
        compiler_params=_params("parallel", "arbitrary"),
    )(g, g, g, u, u, da, da, cw, cb)


def _mm_down_loss(a, w, x1, target, tm=1024, tn=512):
    S, K = a.shape
    n = w.shape[1]
    tm, tn = _tile(S, tm), _tile(n, tn)

    def body(a_ref, w_ref, x_ref, t_ref, dy_ref, dyb_ref, loss_ref):
        @pl.when(jnp.logical_and(pl.program_id(0) == 0, pl.program_id(1) == 0))
        def _():
            loss_ref[...] = jnp.zeros_like(loss_ref)

        y = jnp.dot(a_ref[...], w_ref[...], preferred_element_type=F32) + x_ref[...]
        err = y - t_ref[...]
        dy = err / n
        dy_ref[...] = dy
        dyb_ref[...] = dy.astype(BF16)
        loss_ref[...] += 0.5 * jnp.sum(jnp.sum(err * err, axis=-1, keepdims=True) / n)

    tile = pl.BlockSpec((tm, tn), lambda i, j: (i, j))
    return pl.pallas_call(
        body, name="mm_down_loss", grid=(S // tm, n // tn),
        in_specs=[pl.BlockSpec((tm, K), lambda i, j: (i, 0)), pl.BlockSpec((K, tn), lambda i, j: (0, j)), tile, tile],
        out_specs=[tile, tile, pl.BlockSpec((8, LANES), lambda i, j: (0, 0))],
        out_shape=[jax.ShapeDtypeStruct((S, n), F32), jax.ShapeDtypeStruct((S, n), BF16),
                   jax.ShapeDtypeStruct((8, LANES), F32)],
        compiler_params=_params("arbitrary", "arbitrary"),
    )(a, w, x1, target)


def _mm_o_rms(ycat, w, x, g, tm=512):
    S, K = ycat.shape
    n = w.shape[1]
    tm = _tile(S, tm)

    def body(y_ref, w_ref, x_ref, g_ref, x1_ref, h2_ref):
        x1 = jnp.dot(y_ref[...], w_ref[...], preferred_element_type=F32) + x_ref[...]
        x1_ref[...] = x1
        xhat, _ = _rms_hat(x1, n)
        h2_ref[...] = (xhat * g_ref[...]).astype(BF16)

    return pl.pallas_call(
        body, name="mm_o_rms", grid=(S // tm,),
        in_specs=[_rows(tm, K), _full((K, n)), _rows(tm, n), _full((1, n))],
        out_specs=[_rows(tm, n), _rows(tm, n)],
        out_shape=[jax.ShapeDtypeStruct((S, n), F32), jax.ShapeDtypeStruct((S, n), BF16)],
        compiler_params=_params("parallel"),
    )(ycat, w, x, g)


def _mm_in_dx_rms(dz, w, x, g, extra, sends, tm=256):
    S, K = dz.shape
    n = w.shape[0]
    tm = _tile(S, tm)
    n_steps = S // tm
    nw = len(sends)

    def body(dz_ref, w_ref, x_ref, g_ref, ex_ref, *rest):
        s_refs, (dx_ref, dg_ref), r_refs = rest[:nw], rest[nw:nw + 2], rest[nw + 2:2 * nw + 2]
        scatter = _Scatter(s_refs, r_refs, *rest[2 * nw + 2:])
        i = pl.program_id(0)
        pl.when(i == 0)(scatter.start)

        @pl.when(i == 0)
        def _():
            dg_ref[...] = jnp.zeros_like(dg_ref)

        dh = lax.dot_general(dz_ref[...], w_ref[...], _NT, preferred_element_type=F32)
        xhat, r = _rms_hat(x_ref[...], n)
        dx, dg = _rms_bwd(dh, xhat, r, g_ref[...], n)
        dx_ref[...] = dx + ex_ref[...]
        dg_ref[...] += dg
        pl.when(i == n_steps - 1)(scatter.finish)

    return pl.pallas_call(
        body, name="mm_in_dx_rms", grid=(n_steps,),
        in_specs=[_rows(tm, K), _full((n, K)), _rows(tm, n), _full((1, n)), _rows(tm, n)] + [_ANY] * nw,
        out_specs=[_rows(tm, n), _full((1, n))] + [_ANY] * nw,
        out_shape=[jax.ShapeDtypeStruct((S, n), F32), jax.ShapeDtypeStruct((1, n), F32)]
        + [jax.ShapeDtypeStruct((N_DEV - 1,) + sd.shape[1:], sd.dtype) for sd in sends],
        scratch_shapes=[pltpu.SemaphoreType.DMA((nw, N_DEV - 1)), pltpu.SemaphoreType.DMA((nw, N_DEV - 1))],
        compiler_params=_params("arbitrary"),
    )(dz, w, x, g, extra, *sends)


def _local_step(x, mem, target, W, early_shards, late_shards):
    S = x.shape[0]
    tabs = _rope_tables(S)
    W = dict(W)
    G, Gb = {}, {}

    h, W["w_in"], W["w_q_up"], W["w_kv_up"] = _rms_fwd("rms1_fwd", x, W["g_mix"], early_shards, EARLY_GATHER_KINDS)
    z = _mm("mm_in", h, W["w_in"], tm=512, tn=Z_COLS, tk=D_MODEL)
    y_pool = _pool_fwd(z, W["w_pool"], W["pool_scale"])
    ql, kvl = _lat_fwd(z, W["g_q_lat"], W["g_kv_lat"])
    q_raw = _mm("mm_q_up", ql, W["w_q_up"], out_dtype=BF16, tm=1024, tn=2048, tk=Q_LORA)
    kv_raw = _mm("mm_kv_up", kvl, W["w_kv_up"], out_dtype=BF16, tm=1024, tn=2048, tk=KV_LORA)
    n_ffn = len(FFN_GATHER_KINDS)
    q, k, v, W["w_o"], W["w_mem_kv"] = _mla_prep(q_raw, kv_raw, z, tabs, W["g_q_mla"], W["g_k_mla"],
                                                 late_shards[n_ffn:], MIXER_GATHER_KINDS)
    o_mla, lse, wg3, wu3, W["w_down"], cw3 = _flash_fwd(q, k, v, late_shards[:n_ffn], FFN_GATHER_KINDS)
    W["w_gate"] = jnp.transpose(wg3, (1, 0, 2)).reshape(D_MODEL, D_FF)
    W["w_up"] = jnp.transpose(wu3, (1, 0, 2)).reshape(D_MODEL, D_FF)
    cw = jnp.sum(cw3.reshape(N_DEV, 3, 3, D_FF // N_DEV).astype(F32), axis=1)
    W["conv_w"] = jnp.transpose(cw, (1, 0, 2)).reshape(3, D_FF)
    mn = _mem_norm(mem, W["g_mem"])
    mkv = _mm("mm_mem_kv", mn, W["w_mem_kv"], tm=256, tn=1024, tk=D_MODEL)
    kx, vx = _memkv_prep(mkv, W["g_k_x"])
    y_mem = _memx_fwd(z, W["g_q_x"], kx, vx)
    ycat = jnp.concatenate([y_pool, o_mla, y_mem], axis=1)
    x1, h2 = _mm_o_rms(ycat, W["w_o"], x, W["g_ffn"])
    g = _mm("mm_gate", h2, W["w_gate"], out_dtype=BF16, tm=1024, tn=1408, tk=D_MODEL)
    u = _mm("mm_up", h2, W["w_up"], out_dtype=BF16, tm=1024, tn=1408, tk=D_MODEL)
    a = _convglu_fwd(g, u, W["conv_w"], W["conv_b"])
    dy, dyb, loss_part = _mm_down_loss(a, W["w_down"], x1, target)

    da = _mm("mm_down_dx", dyb, W["w_down"], tb=True, out_dtype=BF16, tm=1024, tn=1408, tk=D_MODEL)
    G["w_down"], Gb["w_down"] = _mm("mm_down_dw", a, dyb, ta=True, also_bf16=True, tm=512, tn=1024, tk=4096)
    dg, du, G["conv_w"], G["conv_b"] = _convglu_bwd(g, u, da, W["conv_w"], W["conv_b"])
    dh2 = _mm("mm_gate_dx", dg, W["w_gate"], tb=True, tm=1024, tn=512, tk=D_FF)
    G["w_gate"] = _mm("mm_gate_dw", h2, dg, ta=True, tm=512, tn=1408, tk=4096)
    G["w_up"] = _mm("mm_up_dw", h2, du, ta=True, tm=512, tn=1408, tk=4096)
    dx1, dx1b, G["g_ffn"] = _mm_up_dx_rms(du, W["w_up"], dh2, x1, W["g_ffn"], dy)

    dyc = _mm("mm_o_dx", dx1b, W["w_o"], tb=True, tm=512, tn=2048, tk=D_MODEL)
    G["w_o"], Gb["w_o"] = _mm("mm_o_dw", ycat, dx1b, ta=True, also_bf16=True, tm=1024, tn=1024, tk=2048)
    dz_pool, G["w_pool"], G["pool_scale"] = _pool_bwd(z, dyc, W["w_pool"], W["pool_scale"], dy_cb=0)
    dz_mq, dkx, dvx, G["g_q_x"] = _memx_bwd(z, W["g_q_x"], kx, vx, dyc, dy_cb=3)
    dmkv, G["g_k_x"] = _memkv_bwd(dkx, dvx, mkv, W["g_k_x"])
    G["w_mem_kv"], Gb["w_mem_kv"] = _mm("mm_mem_kv_dw", mn, dmkv, ta=True, also_bf16=True, tm=1024, tn=1024,
                                        tk=MEM_LEN)
    dmn = _mm("mm_mem_kv_dx", dmkv, W["w_mem_kv"], tb=True, tm=256, tn=2048, tk=1024)
    G["g_mem"] = _mem_gain_bwd(mem, dmn)
    sends = [_send_blocks(n, Gb.get(n, G[n])) for n in SCATTER_EARLY]
    dq, dk, dv, *got_early = _flash_bwd(q, k, v, dyc, POOL_WIDTH // MLA_V, o_mla, lse, sends)
    dq_raw, dkv_raw, dz_kr, G["g_q_mla"], G["g_k_mla"] = _mla_prep_bwd(
        dq, dk, dv, q_raw, kv_raw, z, tabs, W["g_q_mla"], W["g_k_mla"])
    G["w_q_up"] = _mm("mm_q_up_dw", ql, dq_raw, ta=True, tm=512, tn=2048, tk=1024)
    dql = _mm("mm_q_up_dx", dq_raw, W["w_q_up"], tb=True, tm=1024, tn=512, tk=2048)
    G["w_kv_up"] = _mm("mm_kv_up_dw", kvl, dkv_raw, ta=True, tm=256, tn=2048, tk=1024)
    dkvl = _mm("mm_kv_up_dx", dkv_raw, W["w_kv_up"], tb=True, tm=1024, tn=256, tk=2048)
    dz_q, dz_kv, G["g_q_lat"], G["g_kv_lat"] = _lat_bwd(z, W["g_q_lat"], W["g_kv_lat"], dql, dkvl)
    dz = jnp.concatenate([dz_pool, dz_q, dz_mq, dz_kv, dz_kr], axis=1)
    G["w_in"] = _mm("mm_in_dw", h, dz, ta=True, tm=1024, tn=Z_COLS, tk=1024)
    grad_x, G["g_mix"], *got_late = _mm_in_dx_rms(dz, W["w_in"], x, W["g_mix"], dx1,
                                                  [_send_blocks(n, G[n]) for n in SCATTER_LATE])
    return loss_part, grad_x, G, dict(zip(SCATTER_EARLY + SCATTER_LATE, got_early + got_late))


_ANY = pl.BlockSpec(memory_space=pl.ANY)
_MESH = pl.DeviceIdType.MESH


def _gather_out_shape(kind, shape):
    if kind == "rows":
        return (N_DEV * shape[0],) + tuple(shape[1:])
    if kind == "cols":
        return (shape[0], N_DEV * shape[1])
    return (N_DEV,) + tuple(shape)


def _gather_view(ref, kind, shape, d):
    if kind == "rows":
        return ref.at[pl.ds(pl.multiple_of(d * shape[0], 16), shape[0]), :]
    if kind == "cols":
        return ref.at[:, pl.ds(pl.multiple_of(d * shape[1], math.gcd(shape[1], LANES)), shape[1])]
    return ref.at[d]


class _Gather:
    def __init__(self, x_refs, out_refs, kinds, send_sems, recv_sems, local_sems):
        self.xr, self.outr, self.kinds = x_refs, out_refs, kinds
        self.ss, self.rs, self.ls = send_sems, recv_sems, local_sems
        x, y, c = lax.axis_index("x"), lax.axis_index("y"), lax.axis_index("c")
        self.c = c
        self.me, self.sibling = (x, y, c), (x, y, 1 - c)
        self.chips = [(1 - x, y), (x, 1 - y), (1 - x, 1 - y)]

    def _view(self, w, dev):
        px, py, pc = dev
        return _gather_view(self.outr[w], self.kinds[w], self.xr[w].shape, 4 * px + 2 * py + pc)

    def _copy(self, w, k, block, to, from_shard=False):
        v = self._view(w, block)
        return pltpu.make_async_remote_copy(
            src_ref=self.xr[w] if from_shard else v, dst_ref=v, send_sem=self.ss.at[w, k],
            recv_sem=self.rs.at[w, k], device_id=to, device_id_type=_MESH)

    def _local(self, w):
        return pltpu.make_async_copy(self.xr[w], self._view(w, self.me), self.ls.at[w])

    def start(self):
        for w in range(len(self.xr)):
            self._local(w).start()
            self._copy(w, 0, self.me, self.sibling, True).start()
            for j, chip in enumerate(self.chips):
                self._copy(w, 1 + j, self.me, (*chip, self.c), True).start()

    def forward(self):
        for j, chip in enumerate(self.chips):
            for w in range(len(self.xr)):
                self._copy(w, 1 + j, (*chip, self.c), self.me).wait_recv()
                self._copy(w, 4 + j, (*chip, self.c), self.sibling).start()

    def finish(self):
        for w in range(len(self.xr)):
            self._copy(w, 0, self.sibling, self.me).wait_recv()
            for j, chip in enumerate(self.chips):
                self._copy(w, 4 + j, (*chip, 1 - self.c), self.me).wait_recv()
            self._copy(w, 0, self.me, self.sibling, True).wait_send()
            for j, chip in enumerate(self.chips):
                self._copy(w, 1 + j, self.me, (*chip, self.c), True).wait_send()
                self._copy(w, 4 + j, (*chip, self.c), self.sibling).wait_send()
            self._local(w).wait()


class _Scatter:
    def __init__(self, send_refs, recv_refs, send_sems, recv_sems):
        self.sr, self.rr, self.ss, self.rs = send_refs, recv_refs, send_sems, recv_sems
        self.xyz = lax.axis_index("x"), lax.axis_index("y"), lax.axis_index("c")

    def _copy(self, w, k):
        x, y, c = self.xyz
        px, py, pc = x ^ ((k >> 2) & 1), y ^ ((k >> 1) & 1), c ^ (k & 1)
        return pltpu.make_async_remote_copy(
            src_ref=self.sr[w].at[4 * px + 2 * py + pc], dst_ref=self.rr[w].at[k - 1],
            send_sem=self.ss.at[w, k - 1], recv_sem=self.rs.at[w, k - 1],
            device_id=(px, py, pc), device_id_type=_MESH)

    def _all(self):
        return [self._copy(w, k) for w in range(len(self.sr)) for k in range(1, N_DEV)]

    def start(self):
        for cp in self._all():
            cp.start()

    def finish(self):
        for cp in self._all():
            cp.wait_recv()
        for cp in self._all():
            cp.wait_send()


def _exchange_small(s):
    def body(x_ref, out_ref, x_send, x_recv, local_sem):
        x, y, c = lax.axis_index("x"), lax.axis_index("y"), lax.axis_index("c")
        me = 4 * x + 2 * y + c
        mine = pltpu.make_async_copy(x_ref, out_ref.at[me], local_sem)
        mine.start()

        def copy(k):
            px, py, pc = x ^ ((k >> 2) & 1), y ^ ((k >> 1) & 1), c ^ (k & 1)
            return pltpu.make_async_remote_copy(
                src_ref=x_ref, dst_ref=out_ref.at[me], send_sem=x_send.at[k - 1], recv_sem=x_recv.at[k - 1],
                device_id=(px, py, pc), device_id_type=_MESH)

        cps = [copy(k) for k in range(1, N_DEV)]
        for cp in cps:
            cp.start()
        for cp in cps:
            cp.wait_recv()
        for cp in cps:
            cp.wait_send()
        mine.wait()

    return pl.pallas_call(
        body, name="exchange_small", out_shape=jax.ShapeDtypeStruct((N_DEV,) + s.shape, s.dtype),
        in_specs=[_ANY], out_specs=_ANY,
        scratch_shapes=[pltpu.SemaphoreType.DMA((N_DEV - 1,)), pltpu.SemaphoreType.DMA((N_DEV - 1,)),
                        pltpu.SemaphoreType.DMA],
    )(s)


def _sum_slots(buf, tr=208):
    n, R, _ = buf.shape
    tr = tr if R % tr == 0 else R

    def body(b_ref, o_ref):
        acc = b_ref[0]
        for d in range(1, n):
            acc = acc + b_ref[d]
        o_ref[...] = acc

    return pl.pallas_call(
        body, name="sum_slots", grid=(R // tr,),
        in_specs=[pl.BlockSpec((n, tr, LANES), lambda i: (0, i, 0))],
        out_specs=pl.BlockSpec((tr, LANES), lambda i: (i, 0)),
        out_shape=jax.ShapeDtypeStruct((R, LANES), buf.dtype),
        compiler_params=_params("parallel"),
    )(buf)


def _adamw_math(w, g, m, v):
    m_new = ADAM_B1 * m + (1.0 - ADAM_B1) * g
    v_new = ADAM_B2 * v + (1.0 - ADAM_B2) * (g * g)
    m_hat = m_new / (1.0 - ADAM_B1 ** ADAM_STEP)
    v_hat = v_new / (1.0 - ADAM_B2 ** ADAM_STEP)
    return -ADAM_LR * (m_hat / (jnp.sqrt(v_hat) + ADAM_EPS) + ADAM_WD * w), m_new, v_new


def _adamw(name, w, g, got, m, v, tr=128):
    R, C = w.shape
    tr = max([t for t in range(16, tr + 1, 16) if R % t == 0], default=R) if R > tr else R

    def body(w_ref, g_ref, got_ref, m_ref, v_ref, go_ref, d_ref, mo_ref, vo_ref):
        gv = g_ref[...]
        for k in range(N_DEV - 1):
            gv = gv + got_ref[k].astype(F32)
        go_ref[...] = gv
        d_ref[...], mo_ref[...], vo_ref[...] = _adamw_math(w_ref[...], gv, m_ref[...], v_ref[...])

    spec = pl.BlockSpec((tr, C), lambda i: (i, 0))
    got_spec = pl.BlockSpec((N_DEV - 1, tr, C), lambda i: (0, i, 0))
    sds = jax.ShapeDtypeStruct((R, C), F32)
    return pl.pallas_call(
        body, name=name, grid=(R // tr,), in_specs=[spec, spec, got_spec, spec, spec], out_specs=[spec] * 4,
        out_shape=[sds] * 4, compiler_params=_params("parallel"),
    )(w, g, got, m, v)


def _adamw_small(ws, gs, ms, vs):
    n = len(ws)

    def body(*refs):
        ins, outs = refs[:4 * n], refs[4 * n:]
        for p in range(n):
            w_ref, g_ref, m_ref, v_ref = (ins[q * n + p] for q in range(4))
            res = _adamw_math(w_ref[...], g_ref[...], m_ref[...], v_ref[...])
            for q in range(3):
                outs[q * n + p][...] = res[q]

    sds = [jax.ShapeDtypeStruct(w.shape, F32) for w in ws]
    outs = pl.pallas_call(body, name="adamw_small", out_shape=sds * 3)(*ws, *gs, *ms, *vs)
    return outs[:n], outs[n:2 * n], outs[2 * n:]


def _pad_rows(a, mult):
    r = (-a.shape[0]) % mult
    return a if r == 0 else jnp.concatenate([a, jnp.zeros((r,) + a.shape[1:], a.dtype)], axis=0)


def _as_rows(a, mult):
    flat = a.reshape(-1)
    r = (-flat.shape[0]) % LANES
    if r:
        flat = jnp.concatenate([flat, jnp.zeros((r,), a.dtype)])
    return _pad_rows(flat.reshape(-1, LANES), mult)


def _w_in_to_kernel_cols(w):
    pad = jnp.zeros(w.shape[:-1] + (Z_COLS - IN_COLS,), w.dtype)
    return jnp.concatenate([w[..., :1024], w[..., 1344:1856], w[..., 1024:1344], pad], axis=-1)


def _w_in_from_kernel_cols(w):
    return jnp.concatenate([w[..., :1024], w[..., 1536:1856], w[..., 1024:1536]], axis=-1)


def _dev_blocks(n, g):
    ffs = D_FF // N_DEV
    if n == "w_in":
        return _w_in_from_kernel_cols(g).reshape(N_DEV, D_MODEL // N_DEV, IN_COLS)
    if n == "w_q_up":
        return jnp.transpose(g.reshape(Q_LORA, N_DEV, MLA_QK_PAD)[:, :, :MLA_QK], (1, 0, 2))
    if n == "w_kv_up":
        return jnp.transpose(g.reshape(KV_LORA, N_DEV, MLA_NOPE + MLA_V), (1, 0, 2))
    if n in ("w_mem_kv", "w_o"):
        return g.reshape(N_DEV, D_MODEL // N_DEV, g.shape[1])
    if n in ("w_gate", "w_up"):
        return jnp.transpose(g.reshape(D_MODEL, N_DEV, ffs), (1, 0, 2))
    if n == "conv_w":
        return jnp.transpose(g.reshape(3, N_DEV, ffs), (1, 0, 2))
    assert n == "w_down"
    return g.reshape(N_DEV, ffs, D_MODEL)


def _send_blocks(n, g):
    blocks = _dev_blocks(n, g)
    return blocks if n == "conv_w" else blocks.astype(BF16)


def kernel(x, mem, g_mix, w_in, g_q_lat, w_q_up, g_kv_lat, w_kv_up, g_q_mla, g_k_mla, w_pool, pool_scale, g_mem, w_mem_kv, g_q_x, g_k_x, w_o, g_ffn, w_gate, w_up, conv_w, conv_b, w_down, loss_target, m_g_mix, m_w_in, m_g_q_lat, m_w_q_up, m_g_kv_lat, m_w_kv_up, m_g_q_mla, m_g_k_mla, m_w_pool, m_pool_scale, m_g_mem, m_w_mem_kv, m_g_q_x, m_g_k_x, m_w_o, m_g_ffn, m_w_gate, m_w_up, m_conv_w, m_conv_b, m_w_down, v_g_mix, v_w_in, v_g_q_lat, v_w_q_up, v_g_kv_lat, v_w_kv_up, v_g_q_mla, v_g_k_mla, v_w_pool, v_pool_scale, v_g_mem, v_w_mem_kv, v_g_q_x, v_g_k_x, v_w_o, v_g_ffn, v_w_gate, v_w_up, v_conv_w, v_conv_b, v_w_down):
    given = dict(g_mix=g_mix, w_in=w_in, g_q_lat=g_q_lat, w_q_up=w_q_up, g_kv_lat=g_kv_lat, w_kv_up=w_kv_up,
                 g_q_mla=g_q_mla, g_k_mla=g_k_mla, w_pool=w_pool, pool_scale=pool_scale, g_mem=g_mem,
                 w_mem_kv=w_mem_kv, g_q_x=g_q_x, g_k_x=g_k_x, w_o=w_o, g_ffn=g_ffn, w_gate=w_gate, w_up=w_up,
                 conv_w=conv_w, conv_b=conv_b, w_down=w_down)
    mom_m = dict(g_mix=m_g_mix, w_in=m_w_in, g_q_lat=m_g_q_lat, w_q_up=m_w_q_up, g_kv_lat=m_g_kv_lat,
                 w_kv_up=m_w_kv_up, g_q_mla=m_g_q_mla, g_k_mla=m_g_k_mla, w_pool=m_w_pool,
                 pool_scale=m_pool_scale, g_mem=m_g_mem, w_mem_kv=m_w_mem_kv, g_q_x=m_g_q_x, g_k_x=m_g_k_x,
                 w_o=m_w_o, g_ffn=m_g_ffn, w_gate=m_w_gate, w_up=m_w_up, conv_w=m_conv_w, conv_b=m_conv_b,
                 w_down=m_w_down)
    mom_v = dict(g_mix=v_g_mix, w_in=v_w_in, g_q_lat=v_g_q_lat, w_q_up=v_w_q_up, g_kv_lat=v_g_kv_lat,
                 w_kv_up=v_w_kv_up, g_q_mla=v_g_q_mla, g_k_mla=v_g_k_mla, w_pool=v_w_pool,
                 pool_scale=v_pool_scale, g_mem=v_g_mem, w_mem_kv=v_w_mem_kv, g_q_x=v_g_q_x, g_k_x=v_g_k_x,
                 w_o=v_w_o, g_ffn=v_g_ffn, w_gate=v_w_gate, w_up=v_w_up, conv_w=v_conv_w, conv_b=v_conv_b,
                 w_down=v_w_down)
    drop = lambda a: a[0] if a.ndim > 2 else a
    sh = {n: drop(given[n]) for n in WEIGHTS}
    mom_m = {n: drop(mom_m[n]) for n in WEIGHTS}
    mom_v = {n: drop(mom_v[n]) for n in WEIGHTS}

    cw_hi = sh["conv_w"].astype(BF16)
    cw_r = sh["conv_w"] - cw_hi.astype(F32)
    cw_mid = cw_r.astype(BF16)
    cw_lo = (cw_r - cw_mid.astype(F32)).astype(BF16)
    early_shards = [
        _w_in_to_kernel_cols(sh["w_in"]).astype(BF16),
        jnp.pad(sh["w_q_up"], ((0, 0), (0, MLA_QK_PAD - MLA_QK))).astype(BF16),
        sh["w_kv_up"].astype(BF16),
    ]
    late_shards = [sh["w_gate"].astype(BF16), sh["w_up"].astype(BF16), sh["w_down"].astype(BF16),
                   jnp.concatenate([cw_hi, cw_mid, cw_lo], axis=0), sh["w_o"].astype(BF16),
                   sh["w_mem_kv"].astype(BF16)]
    W = {"w_pool": sh["w_pool"].astype(BF16)}
    for n in ("g_mix", "g_q_lat", "g_kv_lat", "pool_scale", "g_mem", "g_q_x", "g_k_x", "g_ffn", "conv_b"):
        W[n] = sh[n]
    pad_qk = lambda gv: jnp.pad(gv, ((0, 0), (0, MLA_QK_PAD - MLA_QK)))
    W["g_q_mla"], W["g_k_mla"] = pad_qk(sh["g_q_mla"]), pad_qk(sh["g_k_mla"])

    loss_part, grad_x, G, got = _local_step(x[0], mem[0], loss_target[0], W, early_shards, late_shards)

    small = {
        "g_mix": G["g_mix"], "g_q_lat": G["g_q_lat"], "g_kv_lat": G["g_kv_lat"],
        "g_q_mla": G["g_q_mla"][:, :MLA_QK], "g_k_mla": G["g_k_mla"][:, :MLA_QK],
        "w_pool": G["w_pool"], "pool_scale": G["pool_scale"], "g_mem": G["g_mem"],
        "g_q_x": G["g_q_x"], "g_k_x": G["g_k_x"], "g_ffn": G["g_ffn"], "conv_b": G["conv_b"],
    }
    s_offs = {}
    segs = []
    off = 0
    for n in SMALL:
        r = _as_rows(small[n], 8)
        s_offs[n] = off
        off += r.shape[0]
        segs.append(r)
    segs.append(loss_part)
    loss_row = off
    sbuf = jnp.concatenate(segs, axis=0)
    s_sum = _sum_slots(_exchange_small(sbuf))

    def small_take(buf, n):
        shape = sh[n].shape
        cnt = math.prod(shape)
        return buf[s_offs[n]:s_offs[n] + -(-cnt // LANES)].reshape(-1)[:cnt].reshape(shape)

    loss = s_sum[loss_row, 0]

    me = 4 * lax.axis_index("x") + 2 * lax.axis_index("y") + lax.axis_index("c")
    grads, deltas, new_m, new_v = {}, {}, {}, {}
    for n in BIG:
        shape = sh[n].shape
        own = lax.dynamic_index_in_dim(_dev_blocks(n, G[n]), me, 0, keepdims=False)
        flat = lambda a: a.reshape(-1, shape[-1])
        outs = _adamw("adamw_" + n, flat(sh[n]), flat(own), got[n].reshape(N_DEV - 1, -1, shape[-1]),
                      flat(mom_m[n]), flat(mom_v[n]))
        grads[n], deltas[n], new_m[n], new_v[n] = (o.reshape(shape) for o in outs)

    for n in SMALL:
        grads[n] = small_take(s_sum, n)
    d_s, m_s, v_s = _adamw_small([sh[n] for n in SMALL], [grads[n] for n in SMALL],
                                 [mom_m[n] for n in SMALL], [mom_v[n] for n in SMALL])
    for j, n in enumerate(SMALL):
        deltas[n], new_m[n], new_v[n] = d_s[j], m_s[j], v_s[j]

    lead = lambda n, a: a.reshape(given[n].shape)
    return (loss, grad_x[None],
            *[lead(n, grads[n]) for n in WEIGHTS], *[lead(n, deltas[n]) for n in WEIGHTS],
            *[lead(n, new_m[n]) for n in WEIGHTS], *[lead(n, new_v[n]) for n in WEIGHTS])
```

```python
import math

import jax
import jax.numpy as jnp
from jax import lax
from jax.experimental import pallas as pl
from jax.experimental.pallas import tpu as pltpu

F32 = jnp.float32
BF16 = jnp.bfloat16

D_MODEL = 2048
D_FF = 5632
POOL_WIDTH = 512
POOL_WINDOWS = (2, 4, 8, 16)
POOL_HALO = 16
MLA_HEADS = 8
MLA_NOPE = 128
MLA_ROPE = 64
MLA_QK = MLA_NOPE + MLA_ROPE
MLA_QK_PAD = 256
MLA_V = 128
Q_LORA = 512
KV_LORA = 256
X_HEADS = 4
X_HEAD_DIM = 128
X_WIDTH = 512
MEM_LEN = 256
ROPE_THETA = 10000.0
NORM_EPS = 1e-6
CONV_HALO = 16
IN_COLS = 1856
Z_COLS = 1920
Z_POOL, Z_Q, Z_MQ, Z_KV, Z_KR = 0, 512, 1024, 1536, 1792

ADAM_LR = 0.001
ADAM_B1 = 0.9
ADAM_B2 = 0.999
ADAM_EPS = 1e-08
ADAM_WD = 0.01
ADAM_STEP = 10

N_DEV = 8
LANES = 128
VMEM_LIMIT = 56 * 1024 * 1024

BIG = ("w_in", "w_q_up", "w_kv_up", "w_mem_kv", "w_o", "w_gate", "w_up", "conv_w", "w_down")
SCATTER_EARLY = ("w_down", "w_gate", "w_up", "conv_w", "w_o", "w_mem_kv")
SCATTER_LATE = ("w_in", "w_q_up", "w_kv_up")
EARLY_GATHER_KINDS = ("rows", "cols", "cols")
FFN_GATHER_KINDS = ("lead", "lead", "rows", "lead")
MIXER_GATHER_KINDS = ("rows", "rows")
SMALL = ("g_mix", "g_q_lat", "g_kv_lat", "g_q_mla", "g_k_mla", "w_pool", "pool_scale", "g_mem",
         "g_q_x", "g_k_x", "g_ffn", "conv_b")
WEIGHTS = ("g_mix", "w_in", "g_q_lat", "w_q_up", "g_kv_lat", "w_kv_up", "g_q_mla", "g_k_mla", "w_pool",
           "pool_scale", "g_mem", "w_mem_kv", "g_q_x", "g_k_x", "w_o", "g_ffn", "w_gate", "w_up", "conv_w",
           "conv_b", "w_down")


def _tile(n, t):
    if n <= t:
        return n
    for c in range(t - t % LANES, 0, -LANES):
        if n % c == 0:
            return c
    return n


def _params(*sem):
    return pltpu.CompilerParams(dimension_semantics=sem, vmem_limit_bytes=VMEM_LIMIT)


def _full(shape):
    nd = len(shape)
    return pl.BlockSpec(shape, lambda *_: (0,) * nd)


def _rows(tr, w, cb=0):
    return pl.BlockSpec((tr, w), lambda i: (i, cb))


def _rms_hat(x, n):
    r = lax.rsqrt(jnp.sum(x * x, axis=-1, keepdims=True) / n + NORM_EPS)
    return x * r, r


def _rms_bwd(dy, xhat, r, g, n):
    dxh = dy * g
    dx = r * (dxh - xhat * (jnp.sum(dxh * xhat, axis=-1, keepdims=True) / n))
    return dx, jnp.sum(dy * xhat, axis=0, keepdims=True)


def _mm(name, a, b, *, ta=False, tb=False, add=None, out_dtype=F32, also_bf16=False, tm=512, tn=512, tk=512):
    if ta:
        K, M = a.shape
    else:
        M, K = a.shape
    if tb:
        N, K2 = b.shape
    else:
        K2, N = b.shape
    assert K == K2, (name, a.shape, b.shape)
    tm, tn, tk = _tile(M, tm), _tile(N, tn), _tile(K, tk)
    nk = K // tk
    grid = (M // tm, N // tn, nk)
    a_spec = pl.BlockSpec((tk, tm), lambda i, j, k: (k, i)) if ta else pl.BlockSpec((tm, tk), lambda i, j, k: (i, k))
    b_spec = pl.BlockSpec((tn, tk), lambda i, j, k: (j, k)) if tb else pl.BlockSpec((tk, tn), lambda i, j, k: (k, j))
    o_spec = pl.BlockSpec((tm, tn), lambda i, j, k: (i, j))
    dims = (((0,) if ta else (1,), (1,) if tb else (0,)), ((), ()))
    has_add = add is not None

    n_in = 3 if has_add else 2
    n_out = 2 if also_bf16 else 1

    def body(*refs):
        a_ref, b_ref = refs[:2]
        add_ref = refs[2] if has_add else None
        o_refs = refs[n_in:n_in + n_out]

        def write(r):
            if has_add:
                r = r + add_ref[...]
            for o_ref in o_refs:
                o_ref[...] = r.astype(o_ref.dtype)

        part = lax.dot_general(a_ref[...], b_ref[...], dims, preferred_element_type=F32)
        if nk == 1:
            write(part)
            return
        acc = refs[-1]
        k = pl.program_id(2)

        @pl.when(k == 0)
        def _():
            acc[...] = part

        @pl.when(k > 0)
        def _():
            acc[...] += part

        @pl.when(k == nk - 1)
        def _():
            write(acc[...])

    ins = [a, b] + ([add] if has_add else [])
    in_specs = [a_spec, b_spec] + ([o_spec] if has_add else [])
    out_shape = [jax.ShapeDtypeStruct((M, N), out_dtype)] + ([jax.ShapeDtypeStruct((M, N), BF16)] if also_bf16 else [])
    outs = pl.pallas_call(
        body, name=name, grid=grid, in_specs=in_specs, out_specs=[o_spec] * n_out, out_shape=out_shape,
        scratch_shapes=[pltpu.VMEM((tm, tn), F32)] if nk > 1 else [],
        compiler_params=_params("parallel", "parallel", "arbitrary"),
    )(*ins)
    return outs if also_bf16 else outs[0]


def _rms_fwd(name, x, g, shards, kinds, tr=512):
    S, n = x.shape
    tr = _tile(S, tr)
    n_steps = S // tr
    nw = len(shards)

    def body(x_ref, g_ref, *rest):
        x_refs, h_ref, g_refs = rest[:nw], rest[nw], rest[nw + 1:2 * nw + 1]
        gather = _Gather(x_refs, g_refs, kinds, *rest[2 * nw + 1:])
        i = pl.program_id(0)
        pl.when(i == 0)(gather.start)
        pl.when(i == (3 * n_steps) // 4)(gather.forward)
        xhat, _ = _rms_hat(x_ref[...], n)
        h_ref[...] = (xhat * g_ref[...]).astype(BF16)
        pl.when(i == n_steps - 1)(gather.finish)

    return pl.pallas_call(
        body, name=name, grid=(n_steps,),
        in_specs=[_rows(tr, n), _full((1, n))] + [_ANY] * nw, out_specs=[_rows(tr, n)] + [_ANY] * nw,
        out_shape=[jax.ShapeDtypeStruct((S, n), BF16)]
        + [jax.ShapeDtypeStruct(_gather_out_shape(kd, sd.shape), sd.dtype) for kd, sd in zip(kinds, shards)],
        scratch_shapes=[pltpu.SemaphoreType.DMA((nw, 7)), pltpu.SemaphoreType.DMA((nw, 7)),
                        pltpu.SemaphoreType.DMA((nw,))],
        compiler_params=_params("arbitrary"),
    )(x, g, *shards)


def _mm_up_dx_rms(du, w, part, x, g, extra, tm=256):
    S, K = du.shape
    n = w.shape[0]
    tm = _tile(S, tm)

    def body(du_ref, w_ref, p_ref, x_ref, g_ref, ex_ref, dx_ref, dxb_ref, dg_ref):
        @pl.when(pl.program_id(0) == 0)
        def _():
            dg_ref[...] = jnp.zeros_like(dg_ref)

        dh = lax.dot_general(du_ref[...], w_ref[...], _NT, preferred_element_type=F32) + p_ref[...]
        xhat, r = _rms_hat(x_ref[...], n)
        dx, dg = _rms_bwd(dh, xhat, r, g_ref[...], n)
        dx = dx + ex_ref[...]
        dx_ref[...] = dx
        dxb_ref[...] = dx.astype(BF16)
        dg_ref[...] += dg

    return pl.pallas_call(
        body, name="mm_up_dx_rms", grid=(S // tm,),
        in_specs=[_rows(tm, K), pl.BlockSpec((n, K), lambda i: (0, 0), pipeline_mode=pl.Buffered(1)),
                  _rows(tm, n), _rows(tm, n), _full((1, n)), _rows(tm, n)],
        out_specs=[_rows(tm, n), _rows(tm, n), _full((1, n))],
        out_shape=[jax.ShapeDtypeStruct((S, n), F32), jax.ShapeDtypeStruct((S, n), BF16),
                   jax.ShapeDtypeStruct((1, n), F32)],
        compiler_params=_params("arbitrary"),
    )(du, w, part, x, g, extra)


def _lat_fwd(z, gq, gkv, tr=512):
    S = z.shape[0]
    tr = _tile(S, tr)

    def body(zq_ref, zkv_ref, gq_ref, gkv_ref, ql_ref, kvl_ref):
        xq, _ = _rms_hat(zq_ref[...], Q_LORA)
        ql_ref[...] = (xq * gq_ref[...]).astype(BF16)
        xkv, _ = _rms_hat(zkv_ref[...], KV_LORA)
        kvl_ref[...] = (xkv * gkv_ref[...]).astype(BF16)

    return pl.pallas_call(
        body, name="lat_fwd", grid=(S // tr,),
        in_specs=[_rows(tr, Q_LORA, Z_Q // Q_LORA), _rows(tr, KV_LORA, Z_KV // KV_LORA),
                  _full((1, Q_LORA)), _full((1, KV_LORA))],
        out_specs=[_rows(tr, Q_LORA), _rows(tr, KV_LORA)],
        out_shape=[jax.ShapeDtypeStruct((S, Q_LORA), BF16), jax.ShapeDtypeStruct((S, KV_LORA), BF16)],
        compiler_params=_params("parallel"),
    )(z, z, gq, gkv)


def _lat_bwd(z, gq, gkv, dql, dkvl, tr=512):
    S = z.shape[0]
    tr = _tile(S, tr)

    def body(zq_ref, zkv_ref, gq_ref, gkv_ref, dql_ref, dkvl_ref, dzq_ref, dzkv_ref, dgq_ref, dgkv_ref):
        @pl.when(pl.program_id(0) == 0)
        def _():
            dgq_ref[...] = jnp.zeros_like(dgq_ref)
            dgkv_ref[...] = jnp.zeros_like(dgkv_ref)

        xq, rq = _rms_hat(zq_ref[...], Q_LORA)
        dx, dg = _rms_bwd(dql_ref[...], xq, rq, gq_ref[...], Q_LORA)
        dzq_ref[...] = dx.astype(BF16)
        dgq_ref[...] += dg
        xkv, rkv = _rms_hat(zkv_ref[...], KV_LORA)
        dx, dg = _rms_bwd(dkvl_ref[...], xkv, rkv, gkv_ref[...], KV_LORA)
        dzkv_ref[...] = dx.astype(BF16)
        dgkv_ref[...] += dg

    return pl.pallas_call(
        body, name="lat_bwd", grid=(S // tr,),
        in_specs=[_rows(tr, Q_LORA, Z_Q // Q_LORA), _rows(tr, KV_LORA, Z_KV // KV_LORA),
                  _full((1, Q_LORA)), _full((1, KV_LORA)), _rows(tr, Q_LORA), _rows(tr, KV_LORA)],
        out_specs=[_rows(tr, Q_LORA), _rows(tr, KV_LORA), _full((1, Q_LORA)), _full((1, KV_LORA))],
        out_shape=[jax.ShapeDtypeStruct((S, Q_LORA), BF16), jax.ShapeDtypeStruct((S, KV_LORA), BF16),
                   jax.ShapeDtypeStruct((1, Q_LORA), F32), jax.ShapeDtypeStruct((1, KV_LORA), F32)],
        compiler_params=_params("arbitrary"),
    )(z, z, gq, gkv, dql, dkvl)


def _pool_d(ext, cur, t0, tr):
    t = t0 + lax.broadcasted_iota(jnp.int32, (tr, 1), 0)
    ds = []
    for gi, w in enumerate(POOL_WINDOWS):
        s = ext[:, gi * LANES:(gi + 1) * LANES]
        sh = 1
        while sh < w:
            s = s + pltpu.roll(s, sh, 0)
            sh *= 2
        cnt = jnp.minimum(t + 1, w).astype(F32)
        ds.append(s[POOL_HALO:] / cnt - cur[:, gi * LANES:(gi + 1) * LANES])
    return ds


def _pool_fwd(z, wp, scale, tr=512):
    S = z.shape[0]
    tr = _tile(S, tr)
    rb = tr // POOL_HALO

    def body(z_ref, zp_ref, wp_ref, sc_ref, y_ref):
        i = pl.program_id(0)
        cur = z_ref[...]
        halo = jnp.where(i == 0, 0.0, zp_ref[...])
        ds = _pool_d(jnp.concatenate([halo, cur], axis=0), cur, i * tr, tr)
        for gi in range(len(POOL_WINDOWS)):
            y = jnp.dot(ds[gi].astype(BF16), wp_ref[gi], preferred_element_type=F32)
            y_ref[:, gi * LANES:(gi + 1) * LANES] = (y * sc_ref[:, gi * LANES:(gi + 1) * LANES]).astype(BF16)

    return pl.pallas_call(
        body, name="pool_fwd", grid=(S // tr,),
        in_specs=[_rows(tr, POOL_WIDTH, Z_POOL // POOL_WIDTH),
                  pl.BlockSpec((POOL_HALO, POOL_WIDTH), lambda i: (jnp.maximum(i * rb - 1, 0), 0)),
                  _full(wp.shape), _full((1, POOL_WIDTH))],
        out_specs=_rows(tr, POOL_WIDTH),
        out_shape=jax.ShapeDtypeStruct((S, POOL_WIDTH), BF16),
        compiler_params=_params("parallel"),
    )(z, z, wp, scale)


def _pool_bwd(z, dyc, wp, scale, dy_cb, tr=512):
    S = z.shape[0]
    tr = _tile(S, tr)
    rb = tr // POOL_HALO
    nhb = S // POOL_HALO
    ng = len(POOL_WINDOWS)

    def body(z_ref, zp_ref, dy_ref, dyn_ref, wp_ref, sc_ref, dz_ref, dwp_ref, dsc_ref):
        i = pl.program_id(0)

        @pl.when(i == 0)
        def _():
            dwp_ref[...] = jnp.zeros_like(dwp_ref)
            dsc_ref[...] = jnp.zeros_like(dsc_ref)

        cur = z_ref[...]
        halo = jnp.where(i == 0, 0.0, zp_ref[...])
        ds = _pool_d(jnp.concatenate([halo, cur], axis=0), cur, i * tr, tr)
        dy_cur = dy_ref[...]
        dy_next = jnp.where(i == pl.num_programs(0) - 1, 0.0, dyn_ref[...])
        dy_ext = jnp.concatenate([dy_cur, dy_next], axis=0)
        n_ext = tr + POOL_HALO
        t_ext = i * tr + lax.broadcasted_iota(jnp.int32, (n_ext, 1), 0)
        for gi, w in enumerate(POOL_WINDOWS):
            cols = slice(gi * LANES, (gi + 1) * LANES)
            d_b = ds[gi].astype(BF16)
            y_pre = jnp.dot(d_b, wp_ref[gi], preferred_element_type=F32)
            dsc_ref[:, cols] += jnp.sum(dy_cur[:, cols] * y_pre, axis=0, keepdims=True)
            dys = (dy_ext[:, cols] * sc_ref[:, cols]).astype(BF16)
            dwp_ref[gi] += lax.dot_general(d_b, dys[:tr], (((0,), (0,)), ((), ())), preferred_element_type=F32)
            dd = lax.dot_general(dys, wp_ref[gi], (((1,), (1,)), ((), ())), preferred_element_type=F32)
            s = dd / jnp.minimum(t_ext + 1, w).astype(F32)
            sh = 1
            while sh < w:
                s = s + pltpu.roll(s, n_ext - sh, 0)
                sh *= 2
            dz_ref[:, cols] = (s[:tr] - dd[:tr]).astype(BF16)

    return pl.pallas_call(
        body, name="pool_bwd", grid=(S // tr,),
        in_specs=[_rows(tr, POOL_WIDTH, Z_POOL // POOL_WIDTH),
                  pl.BlockSpec((POOL_HALO, POOL_WIDTH), lambda i: (jnp.maximum(i * rb - 1, 0), 0)),
                  _rows(tr, POOL_WIDTH, dy_cb),
                  pl.BlockSpec((POOL_HALO, POOL_WIDTH), lambda i: (jnp.minimum((i + 1) * rb, nhb - 1), dy_cb)),
                  _full(wp.shape), _full((1, POOL_WIDTH))],
        out_specs=[_rows(tr, POOL_WIDTH), _full((ng, LANES, LANES)), _full((1, POOL_WIDTH))],
        out_shape=[jax.ShapeDtypeStruct((S, POOL_WIDTH), BF16), jax.ShapeDtypeStruct((ng, LANES, LANES), F32),
                   jax.ShapeDtypeStruct((1, POOL_WIDTH), F32)],
        compiler_params=_params("arbitrary"),
    )(z, z, dyc, dyc, wp, scale)


def _rope_tables(S):
    half = MLA_ROPE // 2
    inv_freq = 1.0 / (ROPE_THETA ** (jnp.arange(half, dtype=F32) / half))
    ang = jnp.arange(S).astype(F32)[:, None] * inv_freq[None, :]
    cos, sin = jnp.cos(ang), jnp.sin(ang)
    z32 = jnp.zeros((S, half), F32)
    z64 = jnp.zeros((S, LANES - MLA_ROPE), F32)
    cos_t = jnp.concatenate([cos, cos, z64], axis=1)
    sin_a = jnp.concatenate([-sin, z32, z64], axis=1)
    sin_b = jnp.concatenate([z32, sin, z64], axis=1)
    return cos_t, sin_a, sin_b


def _rope(x, cos_t, sin_a, sin_b):
    return x * cos_t + pltpu.roll(x, LANES - 32, 1) * sin_a + pltpu.roll(x, 32, 1) * sin_b


def _rope_t(d, cos_t, sin_a, sin_b):
    return d * cos_t + pltpu.roll(d * sin_a, 32, 1) + pltpu.roll(d * sin_b, LANES - 32, 1)


def _mla_prep(q_raw, kv_raw, z, tabs, gq, gk, shards, kinds, tr=256):
    S = z.shape[0]
    tr = _tile(S, tr)
    n_steps = S // tr
    scale = 1.0 / math.sqrt(MLA_QK)
    W = MLA_HEADS * MLA_QK_PAD
    nw = len(shards)

    def body(q_ref, kv_ref, kr_ref, c_ref, sa_ref, sb_ref, gq_ref, gk_ref, *rest):
        x_refs, (qo_ref, ko_ref, vo_ref), g_refs = rest[:nw], rest[nw:nw + 3], rest[nw + 3:2 * nw + 3]
        gather = _Gather(x_refs, g_refs, kinds, *rest[2 * nw + 3:])
        i = pl.program_id(0)
        pl.when(i == 0)(gather.start)
        pl.when(i == (3 * n_steps) // 4)(gather.forward)
        tab = (c_ref[...], sa_ref[...], sb_ref[...])
        kr = kr_ref[...]
        kr_ss = jnp.sum(kr * kr, axis=-1, keepdims=True)
        gqn, gqr = gq_ref[:, :LANES], gq_ref[:, LANES:]
        gkn, gkr = gk_ref[:, :LANES], gk_ref[:, LANES:]
        for h in range(MLA_HEADS):
            c0 = h * MLA_QK_PAD
            qn = q_ref[:, c0:c0 + LANES].astype(F32)
            qr = q_ref[:, c0 + LANES:c0 + 2 * LANES].astype(F32)
            r = lax.rsqrt((jnp.sum(qn * qn, -1, keepdims=True) + jnp.sum(qr * qr, -1, keepdims=True)) / MLA_QK
                          + NORM_EPS)
            qo_ref[h, :, :LANES] = (qn * r * gqn * scale).astype(BF16)
            qo_ref[h, :, LANES:] = (_rope(qr * r * gqr, *tab) * scale).astype(BF16)
            kn = kv_ref[:, c0:c0 + LANES].astype(F32)
            r = lax.rsqrt((jnp.sum(kn * kn, -1, keepdims=True) + kr_ss) / MLA_QK + NORM_EPS)
            ko_ref[h, :, :LANES] = (kn * r * gkn).astype(BF16)
            ko_ref[h, :, LANES:] = _rope(kr * r * gkr, *tab).astype(BF16)
            vo_ref[h] = kv_ref[:, c0 + LANES:c0 + 2 * LANES].astype(BF16)
        pl.when(i == n_steps - 1)(gather.finish)

    hs = lambda w: pl.BlockSpec((MLA_HEADS, tr, w), lambda i: (0, i, 0))
    return pl.pallas_call(
        body, name="mla_prep", grid=(n_steps,),
        in_specs=[_rows(tr, W), _rows(tr, W), _rows(tr, LANES, Z_KR // LANES),
                  _rows(tr, LANES), _rows(tr, LANES), _rows(tr, LANES),
                  _full((1, MLA_QK_PAD)), _full((1, MLA_QK_PAD))] + [_ANY] * nw,
        out_specs=[hs(MLA_QK_PAD), hs(MLA_QK_PAD), hs(MLA_V)] + [_ANY] * nw,
        out_shape=[jax.ShapeDtypeStruct((MLA_HEADS, S, MLA_QK_PAD), BF16),
                   jax.ShapeDtypeStruct((MLA_HEADS, S, MLA_QK_PAD), BF16),
                   jax.ShapeDtypeStruct((MLA_HEADS, S, MLA_V), BF16)]
        + [jax.ShapeDtypeStruct(_gather_out_shape(kd, sd.shape), sd.dtype) for kd, sd in zip(kinds, shards)],
        scratch_shapes=[pltpu.SemaphoreType.DMA((nw, 7)), pltpu.SemaphoreType.DMA((nw, 7)),
                        pltpu.SemaphoreType.DMA((nw,))],
        compiler_params=_params("arbitrary"),
    )(q_raw, kv_raw, z, *tabs, gq, gk, *shards)


def _mla_prep_bwd(dq, dk, dv, q_raw, kv_raw, z, tabs, gq, gk, tr=256):
    S = z.shape[0]
    tr = _tile(S, tr)
    scale = 1.0 / math.sqrt(MLA_QK)
    W = MLA_HEADS * MLA_QK_PAD

    def body(dq_ref, dk_ref, dv_ref, q_ref, kv_ref, kr_ref, c_ref, sa_ref, sb_ref, gq_ref, gk_ref,
             dqr_ref, dkvr_ref, dkr_ref, dgq_ref, dgk_ref):
        @pl.when(pl.program_id(0) == 0)
        def _():
            dgq_ref[...] = jnp.zeros_like(dgq_ref)
            dgk_ref[...] = jnp.zeros_like(dgk_ref)

        tab = (c_ref[...], sa_ref[...], sb_ref[...])
        kr = kr_ref[...]
        kr_ss = jnp.sum(kr * kr, axis=-1, keepdims=True)
        gqn, gqr = gq_ref[:, :LANES], gq_ref[:, LANES:]
        gkn, gkr = gk_ref[:, :LANES], gk_ref[:, LANES:]
        dkr_sum = jnp.zeros((tr, LANES), F32)
        dgq_n = jnp.zeros((1, LANES), F32)
        dgq_r = jnp.zeros((1, LANES), F32)
        dgk_n = jnp.zeros((1, LANES), F32)
        dgk_r = jnp.zeros((1, LANES), F32)

        def head_norm_bwd(xn, xr, r, dyn, dyr, gn, gr):
            hn, hr = xn * r, xr * r
            dxn, dxr = dyn * gn, dyr * gr
            mt = (jnp.sum(dxn * hn, -1, keepdims=True) + jnp.sum(dxr * hr, -1, keepdims=True)) / MLA_QK
            return (r * (dxn - hn * mt), r * (dxr - hr * mt),
                    jnp.sum(dyn * hn, axis=0, keepdims=True), jnp.sum(dyr * hr, axis=0, keepdims=True))

        for h in range(MLA_HEADS):
            c0 = h * MLA_QK_PAD
            qn = q_ref[:, c0:c0 + LANES].astype(F32)
            qr = q_ref[:, c0 + LANES:c0 + 2 * LANES].astype(F32)
            r = lax.rsqrt((jnp.sum(qn * qn, -1, keepdims=True) + jnp.sum(qr * qr, -1, keepdims=True)) / MLA_QK
                          + NORM_EPS)
            dyn = dq_ref[h, :, :LANES] * scale
            dyr = _rope_t(dq_ref[h, :, LANES:] * scale, *tab)
            dxn, dxr, gn_, gr_ = head_norm_bwd(qn, qr, r, dyn, dyr, gqn, gqr)
            dgq_n += gn_
            dgq_r += gr_
            dqr_ref[:, c0:c0 + LANES] = dxn.astype(BF16)
            dqr_ref[:, c0 + LANES:c0 + 2 * LANES] = dxr.astype(BF16)

            kn = kv_ref[:, c0:c0 + LANES].astype(F32)
            r = lax.rsqrt((jnp.sum(kn * kn, -1, keepdims=True) + kr_ss) / MLA_QK + NORM_EPS)
            dyn = dk_ref[h, :, :LANES]
            dyr = _rope_t(dk_ref[h, :, LANES:], *tab)
            dxn, dxr, gn_, gr_ = head_norm_bwd(kn, kr, r, dyn, dyr, gkn, gkr)
            dgk_n += gn_
            dgk_r += gr_
            dkr_sum += dxr
            dkvr_ref[:, c0:c0 + LANES] = dxn.astype(BF16)
            dkvr_ref[:, c0 + LANES:c0 + 2 * LANES] = dv_ref[h].astype(BF16)

        dkr_ref[...] = dkr_sum.astype(BF16)
        dgq_ref[:, :LANES] += dgq_n
        dgq_ref[:, LANES:] += dgq_r
        dgk_ref[:, :LANES] += dgk_n
        dgk_ref[:, LANES:] += dgk_r

    hs = lambda w: pl.BlockSpec((MLA_HEADS, tr, w), lambda i: (0, i, 0))
    return pl.pallas_call(
        body, name="mla_prep_bwd", grid=(S // tr,),
        in_specs=[hs(MLA_QK_PAD), hs(MLA_QK_PAD), hs(MLA_V), _rows(tr, W), _rows(tr, W),
                  _rows(tr, LANES, Z_KR // LANES), _rows(tr, LANES), _rows(tr, LANES), _rows(tr, LANES),
                  _full((1, MLA_QK_PAD)), _full((1, MLA_QK_PAD))],
        out_specs=[_rows(tr, W), _rows(tr, W), _rows(tr, LANES), _full((1, MLA_QK_PAD)), _full((1, MLA_QK_PAD))],
        out_shape=[jax.ShapeDtypeStruct((S, W), BF16), jax.ShapeDtypeStruct((S, W), BF16),
                   jax.ShapeDtypeStruct((S, LANES), BF16),
                   jax.ShapeDtypeStruct((1, MLA_QK_PAD), F32), jax.ShapeDtypeStruct((1, MLA_QK_PAD), F32)],
        compiler_params=_params("arbitrary"),
    )(dq, dk, dv, q_raw, kv_raw, z, *tabs, gq, gk)


_NT = (((1,), (1,)), ((), ()))
_TN = (((0,), (0,)), ((), ()))


def _flash_fwd(q, k, v, shards, kinds, tq=1024, tk=1024, ts=512):
    H, S, dq = q.shape
    dv = v.shape[-1]
    tq, tk, ts = _tile(S, tq), _tile(S, tk), _tile(S, ts)
    assert tq % tk == 0 and tq % ts == 0 and dv == LANES
    nq, nd = S // tq, tq // tk
    nw = len(shards)

    def body(q_ref, k_ref, v_ref, *rest):
        x_refs, (o_ref, lse_ref), g_refs = rest[:nw], rest[nw:nw + 2], rest[nw + 2:2 * nw + 2]
        m_sc, l_sc, acc_sc, send_sems, recv_sems, local_sems = rest[2 * nw + 2:]
        h, i = pl.program_id(0), pl.program_id(1)
        gather = _Gather(x_refs, g_refs, kinds, send_sems, recv_sems, local_sems)
        pl.when(jnp.logical_and(h == 0, i == 0))(gather.start)
        pl.when(jnp.logical_and(h == (7 * H) // 8, i == 0))(gather.forward)

        m_sc[...] = jnp.full_like(m_sc, -jnp.inf)
        l_sc[...] = jnp.zeros_like(l_sc)
        acc_sc[...] = jnp.zeros_like(acc_sc)

        def step(off, width, r0, masked):
            rows = slice(r0, tq)
            s = lax.dot_general(q_ref[rows, :], k_ref[pl.ds(off, width), :], _NT, preferred_element_type=F32)
            if masked:
                row = i * tq + r0 + lax.broadcasted_iota(jnp.int32, s.shape, 0)
                col = off + lax.broadcasted_iota(jnp.int32, s.shape, 1)
                s = jnp.where(col <= row, s, -jnp.inf)
            parts = [s[:, c * LANES:(c + 1) * LANES] for c in range(width // LANES)]
            m_cur = parts[0]
            for pt in parts[1:]:
                m_cur = jnp.maximum(m_cur, pt)
            m_prev = m_sc[rows, :]
            m_new = jnp.maximum(m_prev, jnp.max(m_cur, axis=-1, keepdims=True))
            alpha = jnp.exp(m_prev - m_new)
            ps = [jnp.exp(pt - m_new) for pt in parts]
            l_new = alpha * l_sc[rows, :]
            for pc in ps:
                l_new = l_new + pc
            l_sc[rows, :] = l_new
            p = jnp.concatenate(ps, axis=1).astype(BF16)
            acc_sc[rows, :] = alpha * acc_sc[rows, :] + jnp.dot(p, v_ref[pl.ds(off, width), :],
                                                                preferred_element_type=F32)
            m_sc[rows, :] = m_new

        def full_step(j, carry):
            step(pl.multiple_of(j * tk, tk), tk, 0, False)
            return carry

        lax.fori_loop(0, i * nd, full_step, 0)
        for d in range(tq // ts):
            step(pl.multiple_of(i * tq + d * ts, ts), ts, d * ts, True)
        l = jnp.sum(l_sc[...], axis=-1, keepdims=True)
        o_ref[...] = (acc_sc[...] / l).astype(BF16)
        lse_ref[...] = m_sc[...] + jnp.log(l)
        pl.when(jnp.logical_and(h == H - 1, i == nq - 1))(gather.finish)

    return pl.pallas_call(
        body, name="flash_fwd", grid=(H, nq),
        in_specs=[pl.BlockSpec((None, tq, dq), lambda h, i: (h, i, 0)),
                  pl.BlockSpec((None, S, dq), lambda h, i: (h, 0, 0)),
                  pl.BlockSpec((None, S, dv), lambda h, i: (h, 0, 0))] + [_ANY] * nw,
        out_specs=[pl.BlockSpec((tq, dv), lambda h, i: (i, h)),
                   pl.BlockSpec((None, tq, LANES), lambda h, i: (h, i, 0))] + [_ANY] * nw,
        out_shape=[jax.ShapeDtypeStruct((S, H * dv), BF16), jax.ShapeDtypeStruct((H, S, LANES), F32)]
        + [jax.ShapeDtypeStruct(_gather_out_shape(kd, sd.shape), sd.dtype) for kd, sd in zip(kinds, shards)],
        scratch_shapes=[pltpu.VMEM((tq, LANES), F32), pltpu.VMEM((tq, LANES), F32), pltpu.VMEM((tq, dv), F32),
                        pltpu.SemaphoreType.DMA((nw, 7)), pltpu.SemaphoreType.DMA((nw, 7)),
                        pltpu.SemaphoreType.DMA((nw,))],
        compiler_params=_params("arbitrary", "arbitrary"),
    )(q, k, v, *shards)


def _flash_bwd(q, k, v, dyc, cb0, o, lse, sends, tq=1024, tk=1024, ts=512):
    H, S, dq = q.shape
    dv = v.shape[-1]
    tq, tk, ts = _tile(S, tq), _tile(S, tk), _tile(S, ts)
    assert tq % tk == 0 and tq % ts == 0
    nq, nd = S // tq, tq // tk
    nw = len(sends)

    def body(q_ref, k_ref, v_ref, dy_ref, o_ref, lse_ref, *rest):
        s_refs, (dq_ref, dk_ref, dv_ref), r_refs = rest[:nw], rest[nw:nw + 3], rest[nw + 3:2 * nw + 3]
        do_sc, dl_sc, send_sems, recv_sems = rest[2 * nw + 3:]
        h, i = pl.program_id(0), pl.program_id(1)
        scatter = _Scatter(s_refs, r_refs, send_sems, recv_sems)
        pl.when(jnp.logical_and(h == 0, i == 0))(scatter.start)

        @pl.when(i == 0)
        def _():
            dk_ref[...] = jnp.zeros_like(dk_ref)
            dv_ref[...] = jnp.zeros_like(dv_ref)

        dq_ref[...] = jnp.zeros_like(dq_ref)
        dy = dy_ref[...]
        do_sc[...] = dy.astype(BF16)
        dl_sc[...] = jnp.broadcast_to(jnp.sum(dy * o_ref[...].astype(F32), axis=-1, keepdims=True), (tq, LANES))

        def step(off, width, r0, masked):
            rows = slice(r0, tq)
            qv = q_ref[rows, :]
            dov = do_sc[rows, :]
            kb = k_ref[pl.ds(off, width), :]
            s = lax.dot_general(qv, kb, _NT, preferred_element_type=F32)
            if masked:
                row = i * tq + r0 + lax.broadcasted_iota(jnp.int32, s.shape, 0)
                col = off + lax.broadcasted_iota(jnp.int32, s.shape, 1)
                s = jnp.where(col <= row, s, -jnp.inf)
            p = jnp.exp(s - lse_ref[rows, :1])
            dv_ref[pl.ds(off, width), :] += lax.dot_general(p.astype(BF16), dov, _TN, preferred_element_type=F32)
            dp = lax.dot_general(dov, v_ref[pl.ds(off, width), :], _NT, preferred_element_type=F32)
            ds = (p * (dp - dl_sc[rows, :1])).astype(BF16)
            dk_ref[pl.ds(off, width), :] += lax.dot_general(ds, qv, _TN, preferred_element_type=F32)
            dq_ref[rows, :] += jnp.dot(ds, kb, preferred_element_type=F32)

        def full_step(j, carry):
            step(pl.multiple_of(j * tk, tk), tk, 0, False)
            return carry

        lax.fori_loop(0, i * nd, full_step, 0)
        for d in range(tq // ts):
            step(pl.multiple_of(i * tq + d * ts, ts), ts, d * ts, True)
        pl.when(jnp.logical_and(h == H - 1, i == nq - 1))(scatter.finish)

    return pl.pallas_call(
        body, name="flash_bwd", grid=(H, nq),
        in_specs=[pl.BlockSpec((None, tq, dq), lambda h, i: (h, i, 0)),
                  pl.BlockSpec((None, S, dq), lambda h, i: (h, 0, 0), pipeline_mode=pl.Buffered(1)),
                  pl.BlockSpec((None, S, dv), lambda h, i: (h, 0, 0), pipeline_mode=pl.Buffered(1)),
                  pl.BlockSpec((tq, dv), lambda h, i: (i, h + cb0)),
                  pl.BlockSpec((tq, dv), lambda h, i: (i, h)),
                  pl.BlockSpec((None, tq, LANES), lambda h, i: (h, i, 0))] + [_ANY] * nw,
        out_specs=[pl.BlockSpec((None, tq, dq), lambda h, i: (h, i, 0)),
                   pl.BlockSpec((None, S, dq), lambda h, i: (h, 0, 0)),
                   pl.BlockSpec((None, S, dv), lambda h, i: (h, 0, 0))] + [_ANY] * nw,
        out_shape=[jax.ShapeDtypeStruct((H, S, dq), F32), jax.ShapeDtypeStruct((H, S, dq), F32),
                   jax.ShapeDtypeStruct((H, S, dv), F32)]
        + [jax.ShapeDtypeStruct((N_DEV - 1,) + sd.shape[1:], sd.dtype) for sd in sends],
        scratch_shapes=[pltpu.VMEM((tq, dv), BF16), pltpu.VMEM((tq, LANES), F32),
                        pltpu.SemaphoreType.DMA((nw, N_DEV - 1)), pltpu.SemaphoreType.DMA((nw, N_DEV - 1))],
        compiler_params=_params("arbitrary", "arbitrary"),
    )(q, k, v, dyc, o, lse, *sends)


def _mem_norm(mem, g):
    n = mem.shape[1]

    def body(m_ref, g_ref, o_ref):
        xhat, _ = _rms_hat(m_ref[...], n)
        o_ref[...] = (xhat * g_ref[...]).astype(BF16)

    return pl.pallas_call(body, name="mem_norm", out_shape=jax.ShapeDtypeStruct(mem.shape, BF16),
                          compiler_params=pltpu.CompilerParams(vmem_limit_bytes=VMEM_LIMIT))(mem, g)


def _memkv_prep(mkv, gk):
    M = mkv.shape[0]

    def body(mkv_ref, g_ref, k_ref, v_ref):
        for h in range(X_HEADS):
            cols = slice(h * X_HEAD_DIM, (h + 1) * X_HEAD_DIM)
            xhat, _ = _rms_hat(mkv_ref[:, cols], X_HEAD_DIM)
            k_ref[:, cols] = (xhat * g_ref[...]).astype(BF16)
        v_ref[...] = mkv_ref[:, X_WIDTH:].astype(BF16)

    return pl.pallas_call(
        body, name="memkv_prep",
        out_shape=[jax.ShapeDtypeStruct((M, X_WIDTH), BF16), jax.ShapeDtypeStruct((M, X_WIDTH), BF16)],
    )(mkv, gk)


def _memkv_bwd(dk, dv, mkv, gk):
    M = mkv.shape[0]

    def body(dk_ref, dv_ref, mkv_ref, g_ref, dmkv_ref, dg_ref):
        dg = jnp.zeros((1, X_HEAD_DIM), F32)
        for h in range(X_HEADS):
            cols = slice(h * X_HEAD_DIM, (h + 1) * X_HEAD_DIM)
            xhat, r = _rms_hat(mkv_ref[:, cols], X_HEAD_DIM)
            dx, dgh = _rms_bwd(dk_ref[:, cols], xhat, r, g_ref[...], X_HEAD_DIM)
            dmkv_ref[:, cols] = dx.astype(BF16)
            dg += dgh
        dmkv_ref[:, X_WIDTH:] = dv_ref[...].astype(BF16)
        dg_ref[...] = dg

    return pl.pallas_call(
        body, name="memkv_bwd",
        out_shape=[jax.ShapeDtypeStruct((M, 2 * X_WIDTH), BF16), jax.ShapeDtypeStruct((1, X_HEAD_DIM), F32)],
    )(dk, dv, mkv, gk)


def _mem_gain_bwd(mem, dmn):
    n = mem.shape[1]

    def body(m_ref, d_ref, dg_ref):
        xhat, _ = _rms_hat(m_ref[...], n)
        dg_ref[...] = jnp.sum(d_ref[...] * xhat, axis=0, keepdims=True)

    return pl.pallas_call(body, name="mem_gain_bwd", out_shape=jax.ShapeDtypeStruct((1, n), F32),
                          compiler_params=pltpu.CompilerParams(vmem_limit_bytes=VMEM_LIMIT))(mem, dmn)


def _memx_scores(zq_ref, g, kx_ref, h):
    cols = slice(h * X_HEAD_DIM, (h + 1) * X_HEAD_DIM)
    xhat, r = _rms_hat(zq_ref[:, cols], X_HEAD_DIM)
    qn = (xhat * g).astype(BF16)
    s = lax.dot_general(qn, kx_ref[:, cols], _NT, preferred_element_type=F32) * (1.0 / math.sqrt(X_HEAD_DIM))
    p = jnp.exp(s - jnp.max(s, axis=-1, keepdims=True))
    return cols, xhat, r, qn, p, jnp.sum(p, axis=-1, keepdims=True)


def _memx_fwd(z, g, kx, vx, tr=512):
    S = z.shape[0]
    tr = _tile(S, tr)
    M = kx.shape[0]

    def body(zq_ref, g_ref, kx_ref, vx_ref, y_ref):
        for h in range(X_HEADS):
            cols, _, _, _, p, l = _memx_scores(zq_ref, g_ref[...], kx_ref, h)
            o = jnp.dot(p.astype(BF16), vx_ref[:, cols], preferred_element_type=F32)
            y_ref[:, cols] = (o / l).astype(BF16)

    return pl.pallas_call(
        body, name="memx_fwd", grid=(S // tr,),
        in_specs=[_rows(tr, X_WIDTH, Z_MQ // X_WIDTH), _full((1, X_HEAD_DIM)), _full((M, X_WIDTH)),
                  _full((M, X_WIDTH))],
        out_specs=_rows(tr, X_WIDTH),
        out_shape=jax.ShapeDtypeStruct((S, X_WIDTH), BF16),
        compiler_params=_params("parallel"),
    )(z, g, kx, vx)


def _memx_bwd(z, g, kx, vx, dyc, dy_cb, tr=512):
    S = z.shape[0]
    tr = _tile(S, tr)
    M = kx.shape[0]
    scale = 1.0 / math.sqrt(X_HEAD_DIM)

    def body(zq_ref, g_ref, kx_ref, vx_ref, dy_ref, dz_ref, dk_ref, dv_ref, dg_ref):
        @pl.when(pl.program_id(0) == 0)
        def _():
            dk_ref[...] = jnp.zeros_like(dk_ref)
            dv_ref[...] = jnp.zeros_like(dv_ref)
            dg_ref[...] = jnp.zeros_like(dg_ref)

        gv = g_ref[...]
        for h in range(X_HEADS):
            cols, xhat, r, qn, p, l = _memx_scores(zq_ref, gv, kx_ref, h)
            p = p / l
            do = dy_ref[:, cols].astype(BF16)
            dv_ref[:, cols] += lax.dot_general(p.astype(BF16), do, _TN, preferred_element_type=F32)
            dp = lax.dot_general(do, vx_ref[:, cols], _NT, preferred_element_type=F32)
            ds = (p * (dp - jnp.sum(dp * p, axis=-1, keepdims=True)) * scale).astype(BF16)
            dqn = jnp.dot(ds, kx_ref[:, cols], preferred_element_type=F32)
            dk_ref[:, cols] += lax.dot_general(ds, qn, _TN, preferred_element_type=F32)
            dx, dg = _rms_bwd(dqn, xhat, r, gv, X_HEAD_DIM)
            dz_ref[:, cols] = dx.astype(BF16)
            dg_ref[...] += dg

    return pl.pallas_call(
        body, name="memx_bwd", grid=(S // tr,),
        in_specs=[_rows(tr, X_WIDTH, Z_MQ // X_WIDTH), _full((1, X_HEAD_DIM)), _full((M, X_WIDTH)),
                  _full((M, X_WIDTH)), _rows(tr, X_WIDTH, dy_cb)],
        out_specs=[_rows(tr, X_WIDTH), _full((M, X_WIDTH)), _full((M, X_WIDTH)), _full((1, X_HEAD_DIM))],
        out_shape=[jax.ShapeDtypeStruct((S, X_WIDTH), BF16), jax.ShapeDtypeStruct((M, X_WIDTH), F32),
                   jax.ShapeDtypeStruct((M, X_WIDTH), F32), jax.ShapeDtypeStruct((1, X_HEAD_DIM), F32)],
        compiler_params=_params("arbitrary"),
    )(z, g, kx, vx, dyc)


def _conv_gc(g_ext, w_ref, b_ref, n_ext):
    g1 = pltpu.roll(g_ext, 1, 0)
    g2 = pltpu.roll(g_ext, 2, 0)
    gc = b_ref[...] + w_ref[0:1, :] * g2
    gc = gc + w_ref[1:2, :] * g1
    gc = gc + w_ref[2:3, :] * g_ext
    return gc, g1, g2


def _convglu_fwd(g, u, cw, cb, tr=2048, tc=512):
    S, F = g.shape
    tr, tc = _tile(S, tr), _tile(F, tc)
    rb = tr // CONV_HALO

    def body(g_ref, gp_ref, u_ref, w_ref, b_ref, a_ref):
        i = pl.program_id(1)
        halo = jnp.where(i == 0, 0.0, gp_ref[...].astype(F32))
        g_ext = jnp.concatenate([halo, g_ref[...].astype(F32)], axis=0)
        gc, _, _ = _conv_gc(g_ext, w_ref, b_ref, tr + CONV_HALO)
        gc = gc[CONV_HALO:]
        sig = 0.5 * jnp.tanh(0.5 * gc) + 0.5
        a_ref[...] = (gc * sig * u_ref[...].astype(F32)).astype(BF16)

    return pl.pallas_call(
        body, name="convglu_fwd", grid=(F // tc, S // tr),
        in_specs=[pl.BlockSpec((tr, tc), lambda j, i: (i, j)),
                  pl.BlockSpec((CONV_HALO, tc), lambda j, i: (jnp.maximum(i * rb - 1, 0), j)),
                  pl.BlockSpec((tr, tc), lambda j, i: (i, j)),
                  pl.BlockSpec((3, tc), lambda j, i: (0, j)), pl.BlockSpec((1, tc), lambda j, i: (0, j))],
        out_specs=pl.BlockSpec((tr, tc), lambda j, i: (i, j)),
        out_shape=jax.ShapeDtypeStruct((S, F), BF16),
        compiler_params=_params("parallel", "parallel"),
    )(g, g, u, cw, cb)


def _convglu_bwd(g, u, da, cw, cb, tr=1024, tc=512):
    S, F = g.shape
    tr, tc = _tile(S, tr), _tile(F, tc)
    rb = tr // CONV_HALO
    nhb = S // CONV_HALO
    H = CONV_HALO

    def body(g_ref, gp_ref, gn_ref, u_ref, un_ref, da_ref, dan_ref, w_ref, b_ref,
             dg_ref, du_ref, dw_ref, db_ref):
        i = pl.program_id(1)
        last = i == pl.num_programs(1) - 1

        @pl.when(i == 0)
        def _():
            dw_ref[...] = jnp.zeros_like(dw_ref)
            db_ref[...] = jnp.zeros_like(db_ref)

        g_prev = jnp.where(i == 0, 0.0, gp_ref[...].astype(F32))
        g_cur = g_ref[...].astype(F32)
        g_ext = jnp.concatenate([g_prev, g_cur, gn_ref[...].astype(F32)], axis=0)
        gc, g1, g2 = _conv_gc(g_ext, w_ref, b_ref, tr + 2 * H)
        gc = gc[H:]
        ux = jnp.concatenate([u_ref[...], un_ref[...]], axis=0).astype(F32)
        dax = jnp.concatenate([da_ref[...].astype(F32), jnp.where(last, 0.0, dan_ref[...].astype(F32))], axis=0)
        sig = 0.5 * jnp.tanh(0.5 * gc) + 0.5
        du_ref[...] = (dax[:tr] * (gc[:tr] * sig[:tr])).astype(BF16)
        dgc = dax * ux * (sig * (1.0 + gc * (1.0 - sig)))
        n = tr + H
        d1 = pltpu.roll(dgc, n - 1, 0)[:tr]
        d2 = pltpu.roll(dgc, n - 2, 0)[:tr]
        d0 = dgc[:tr]
        dg_ref[...] = (w_ref[2:3, :] * d0 + w_ref[1:2, :] * d1 + w_ref[0:1, :] * d2).astype(BF16)
        db_ref[...] += jnp.sum(d0, axis=0, keepdims=True)
        dw_ref[0:1, :] += jnp.sum(d0 * g2[H:H + tr], axis=0, keepdims=True)
        dw_ref[1:2, :] += jnp.sum(d0 * g1[H:H + tr], axis=0, keepdims=True)
        dw_ref[2:3, :] += jnp.sum(d0 * g_cur, axis=0, keepdims=True)

    cur = pl.BlockSpec((tr, tc), lambda j, i: (i, j))
    prv = pl.BlockSpec((H, tc), lambda j, i: (jnp.maximum(i * rb - 1, 0), j))
    nxt = pl.BlockSpec((H, tc), lambda j, i: (jnp.minimum((i + 1) * rb, nhb - 1), j))
    return pl.pallas_call(
        body, name="convglu_bwd", grid=(F // tc, S // tr),
        in_specs=[cur, prv, nxt, cur, nxt, cur, nxt,
                  pl.BlockSpec((3, tc), lambda j, i: (0, j)), pl.BlockSpec((1, tc), lambda j, i: (0, j))],
        out_specs=[cur, cur, pl.BlockSpec((3, tc), lambda j, i: (0, j)), pl.BlockSpec((1, tc), lambda j, i: (0, j))],
        out_shape=[jax.ShapeDtypeStruct((S, F), BF16), jax.ShapeDtypeStruct((S, F), BF16),
                   jax.ShapeDtypeStruct((3, F), F32), jax.ShapeDtypeStruct((1, F), F32)],
        compiler_params=_params("parallel", "arbitrary"),
    )(g, g, g, u, u, da, da, cw, cb)


def _mm_down_loss(a, w, x1, target, tm=1024, tn=512):
    S, K = a.shape
    n = w.shape[1]
    tm, tn = _tile(S, tm), _tile(n, tn)

    def body(a_ref, w_ref, x_ref, t_ref, dy_ref, dyb_ref, loss_ref):
        @pl.when(jnp.logical_and(pl.program_id(0) == 0, pl.program_id(1) == 0))
        def _():
            loss_ref[...] = jnp.zeros_like(loss_ref)

        y = jnp.dot(a_ref[...], w_ref[...], preferred_element_type=F32) + x_ref[...]
        err = y - t_ref[...]
        dy = err / n
        dy_ref[...] = dy
        dyb_ref[...] = dy.astype(BF16)
        loss_ref[...] += 0.5 * jnp.sum(jnp.sum(err * err, axis=-1, keepdims=True) / n)

    tile = pl.BlockSpec((tm, tn), lambda i, j: (i, j))
    return pl.pallas_call(
        body, name="mm_down_loss", grid=(S // tm, n // tn),
        in_specs=[pl.BlockSpec((tm, K), lambda i, j: (i, 0)), pl.BlockSpec((K, tn), lambda i, j: (0, j)), tile, tile],
        out_specs=[tile, tile, pl.BlockSpec((8, LANES), lambda i, j: (0, 0))],
        out_shape=[jax.ShapeDtypeStruct((S, n), F32), jax.ShapeDtypeStruct((S, n), BF16),
                   jax.ShapeDtypeStruct((8, LANES), F32)],
        compiler_params=_params("arbitrary", "arbitrary"),
    )(a, w, x1, target)


def _mm_o_rms(ycat, w, x, g, tm=512):
    S, K = ycat.shape
    n = w.shape[1]
    tm = _tile(S, tm)

    def body(y_ref, w_ref, x_ref, g_ref, x1_ref, h2_ref):
        x1 = jnp.dot(y_ref[...], w_ref[...], preferred_element_type=F32) + x_ref[...]
        x1_ref[...] = x1
        xhat, _ = _rms_hat(x1, n)
        h2_ref[...] = (xhat * g_ref[...]).astype(BF16)

    return pl.pallas_call(
        body, name="mm_o_rms", grid=(S // tm,),
        in_specs=[_rows(tm, K), _full((K, n)), _rows(tm, n), _full((1, n))],
        out_specs=[_rows(tm, n), _rows(tm, n)],
        out_shape=[jax.ShapeDtypeStruct((S, n), F32), jax.ShapeDtypeStruct((S, n), BF16)],
        compiler_params=_params("parallel"),
    )(ycat, w, x, g)


def _mm_in_dx_rms(dz, w, x, g, extra, sends, tm=256):
    S, K = dz.shape
    n = w.shape[0]
    tm = _tile(S, tm)
    n_steps = S // tm
    nw = len(sends)

    def body(dz_ref, w_ref, x_ref, g_ref, ex_ref, *rest):
        s_refs, (dx_ref, dg_ref), r_refs = rest[:nw], rest[nw:nw + 2], rest[nw + 2:2 * nw + 2]
        scatter = _Scatter(s_refs, r_refs, *rest[2 * nw + 2:])
        i = pl.program_id(0)
        pl.when(i == 0)(scatter.start)

        @pl.when(i == 0)
        def _():
            dg_ref[...] = jnp.zeros_like(dg_ref)

        dh = lax.dot_general(dz_ref[...], w_ref[...], _NT, preferred_element_type=F32)
        xhat, r = _rms_hat(x_ref[...], n)
        dx, dg = _rms_bwd(dh, xhat, r, g_ref[...], n)
        dx_ref[...] = dx + ex_ref[...]
        dg_ref[...] += dg
        pl.when(i == n_steps - 1)(scatter.finish)

    return pl.pallas_call(
        body, name="mm_in_dx_rms", grid=(n_steps,),
        in_specs=[_rows(tm, K), _full((n, K)), _rows(tm, n), _full((1, n)), _rows(tm, n)] + [_ANY] * nw,
        out_specs=[_rows(tm, n), _full((1, n))] + [_ANY] * nw,
        out_shape=[jax.ShapeDtypeStruct((S, n), F32), jax.ShapeDtypeStruct((1, n), F32)]
        + [jax.ShapeDtypeStruct((N_DEV - 1,) + sd.shape[1:], sd.dtype) for sd in sends],
        scratch_shapes=[pltpu.SemaphoreType.DMA((nw, N_DEV - 1)), pltpu.SemaphoreType.DMA((nw, N_DEV - 1))],
        compiler_params=_params("arbitrary"),
    )(dz, w, x, g, extra, *sends)


def _local_step(x, mem, target, W, early_shards, late_shards):
    S = x.shape[0]
    tabs = _rope_tables(S)
    W = dict(W)
    G, Gb = {}, {}

    h, W["w_in"], W["w_q_up"], W["w_kv_up"] = _rms_fwd("rms1_fwd", x, W["g_mix"], early_shards, EARLY_GATHER_KINDS)
    z = _mm("mm_in", h, W["w_in"], tm=512, tn=Z_COLS, tk=D_MODEL)
    y_pool = _pool_fwd(z, W["w_pool"], W["pool_scale"])
    ql, kvl = _lat_fwd(z, W["g_q_lat"], W["g_kv_lat"])
    q_raw = _mm("mm_q_up", ql, W["w_q_up"], out_dtype=BF16, tm=1024, tn=2048, tk=Q_LORA)
    kv_raw = _mm("mm_kv_up", kvl, W["w_kv_up"], out_dtype=BF16, tm=1024, tn=2048, tk=KV_LORA)
    n_ffn = len(FFN_GATHER_KINDS)
    q, k, v, W["w_o"], W["w_mem_kv"] = _mla_prep(q_raw, kv_raw, z, tabs, W["g_q_mla"], W["g_k_mla"],
                                                 late_shards[n_ffn:], MIXER_GATHER_KINDS)
    o_mla, lse, wg3, wu3, W["w_down"], cw3 = _flash_fwd(q, k, v, late_shards[:n_ffn], FFN_GATHER_KINDS)
    W["w_gate"] = jnp.transpose(wg3, (1, 0, 2)).reshape(D_MODEL, D_FF)
    W["w_up"] = jnp.transpose(wu3, (1, 0, 2)).reshape(D_MODEL, D_FF)
    cw = jnp.sum(cw3.reshape(N_DEV, 3, 3, D_FF // N_DEV).astype(F32), axis=1)
    W["conv_w"] = jnp.transpose(cw, (1, 0, 2)).reshape(3, D_FF)
    mn = _mem_norm(mem, W["g_mem"])
    mkv = _mm("mm_mem_kv", mn, W["w_mem_kv"], tm=256, tn=1024, tk=D_MODEL)
    kx, vx = _memkv_prep(mkv, W["g_k_x"])
    y_mem = _memx_fwd(z, W["g_q_x"], kx, vx)
    ycat = jnp.concatenate([y_pool, o_mla, y_mem], axis=1)
    x1, h2 = _mm_o_rms(ycat, W["w_o"], x, W["g_ffn"])
    g = _mm("mm_gate", h2, W["w_gate"], out_dtype=BF16, tm=1024, tn=1408, tk=D_MODEL)
    u = _mm("mm_up", h2, W["w_up"], out_dtype=BF16, tm=1024, tn=1408, tk=D_MODEL)
    a = _convglu_fwd(g, u, W["conv_w"], W["conv_b"])
    dy, dyb, loss_part = _mm_down_loss(a, W["w_down"], x1, target)

    da = _mm("mm_down_dx", dyb, W["w_down"], tb=True, out_dtype=BF16, tm=1024, tn=1408, tk=D_MODEL)
    G["w_down"], Gb["w_down"] = _mm("mm_down_dw", a, dyb, ta=True, also_bf16=True, tm=512, tn=1024, tk=4096)
    dg, du, G["conv_w"], G["conv_b"] = _convglu_bwd(g, u, da, W["conv_w"], W["conv_b"])
    dh2 = _mm("mm_gate_dx", dg, W["w_gate"], tb=True, tm=1024, tn=512, tk=D_FF)
    G["w_gate"] = _mm("mm_gate_dw", h2, dg, ta=True, tm=512, tn=1408, tk=4096)
    G["w_up"] = _mm("mm_up_dw", h2, du, ta=True, tm=512, tn=1408, tk=4096)
    dx1, dx1b, G["g_ffn"] = _mm_up_dx_rms(du, W["w_up"], dh2, x1, W["g_ffn"], dy)

    dyc = _mm("mm_o_dx", dx1b, W["w_o"], tb=True, tm=512, tn=2048, tk=D_MODEL)
    G["w_o"], Gb["w_o"] = _mm("mm_o_dw", ycat, dx1b, ta=True, also_bf16=True, tm=1024, tn=1024, tk=2048)
    dz_pool, G["w_pool"], G["pool_scale"] = _pool_bwd(z, dyc, W["w_pool"], W["pool_scale"], dy_cb=0)
    dz_mq, dkx, dvx, G["g_q_x"] = _memx_bwd(z, W["g_q_x"], kx, vx, dyc, dy_cb=3)
    dmkv, G["g_k_x"] = _memkv_bwd(dkx, dvx, mkv, W["g_k_x"])
    G["w_mem_kv"], Gb["w_mem_kv"] = _mm("mm_mem_kv_dw", mn, dmkv, ta=True, also_bf16=True, tm=1024, tn=1024,
                                        tk=MEM_LEN)
    dmn = _mm("mm_mem_kv_dx", dmkv, W["w_mem_kv"], tb=True, tm=256, tn=2048, tk=1024)
    G["g_mem"] = _mem_gain_bwd(mem, dmn)
    sends = [_send_blocks(n, Gb.get(n, G[n])) for n in SCATTER_EARLY]
    dq, dk, dv, *got_early = _flash_bwd(q, k, v, dyc, POOL_WIDTH // MLA_V, o_mla, lse, sends)
    dq_raw, dkv_raw, dz_kr, G["g_q_mla"], G["g_k_mla"] = _mla_prep_bwd(
        dq, dk, dv, q_raw, kv_raw, z, tabs, W["g_q_mla"], W["g_k_mla"])
    G["w_q_up"] = _mm("mm_q_up_dw", ql, dq_raw, ta=True, tm=512, tn=2048, tk=1024)
    dql = _mm("mm_q_up_dx", dq_raw, W["w_q_up"], tb=True, tm=1024, tn=512, tk=2048)
    G["w_kv_up"] = _mm("mm_kv_up_dw", kvl, dkv_raw, ta=True, tm=256, tn=2048, tk=1024)
    dkvl = _mm("mm_kv_up_dx", dkv_raw, W["w_kv_up"], tb=True, tm=1024, tn=256, tk=2048)
    dz_q, dz_kv, G["g_q_lat"], G["g_kv_lat"] = _lat_bwd(z, W["g_q_lat"], W["g_kv_lat"], dql, dkvl)
    dz = jnp.concatenate([dz_pool, dz_q, dz_mq, dz_kv, dz_kr], axis=1)
    G["w_in"] = _mm("mm_in_dw", h, dz, ta=True, tm=1024, tn=Z_COLS, tk=1024)
    grad_x, G["g_mix"], *got_late = _mm_in_dx_rms(dz, W["w_in"], x, W["g_mix"], dx1,
                                                  [_send_blocks(n, G[n]) for n in SCATTER_LATE])
    return loss_part, grad_x, G, dict(zip(SCATTER_EARLY + SCATTER_LATE, got_early + got_late))


_ANY = pl.BlockSpec(memory_space=pl.ANY)
_MESH = pl.DeviceIdType.MESH


def _gather_out_shape(kind, shape):
    if kind == "rows":
        return (N_DEV * shape[0],) + tuple(shape[1:])
    if kind == "cols":
        return (shape[0], N_DEV * shape[1])
    return (N_DEV,) + tuple(shape)


def _gather_view(ref, kind, shape, d):
    if kind == "rows":
        return ref.at[pl.ds(pl.multiple_of(d * shape[0], 16), shape[0]), :]
    if kind == "cols":
        return ref.at[:, pl.ds(pl.multiple_of(d * shape[1], math.gcd(shape[1], LANES)), shape[1])]
    return ref.at[d]


class _Gather:
    def __init__(self, x_refs, out_refs, kinds, send_sems, recv_sems, local_sems):
        self.xr, self.outr, self.kinds = x_refs, out_refs, kinds
        self.ss, self.rs, self.ls = send_sems, recv_sems, local_sems
        x, y, c = lax.axis_index("x"), lax.axis_index("y"), lax.axis_index("c")
        self.c = c
        self.me, self.sibling = (x, y, c), (x, y, 1 - c)
        self.chips = [(1 - x, y), (x, 1 - y), (1 - x, 1 - y)]

    def _view(self, w, dev):
        px, py, pc = dev
        return _gather_view(self.outr[w], self.kinds[w], self.xr[w].shape, 4 * px + 2 * py + pc)

    def _copy(self, w, k, block, to, from_shard=False):
        v = self._view(w, block)
        return pltpu.make_async_remote_copy(
            src_ref=self.xr[w] if from_shard else v, dst_ref=v, send_sem=self.ss.at[w, k],
            recv_sem=self.rs.at[w, k], device_id=to, device_id_type=_MESH)

    def _local(self, w):
        return pltpu.make_async_copy(self.xr[w], self._view(w, self.me), self.ls.at[w])

    def start(self):
        for w in range(len(self.xr)):
            self._local(w).start()
            self._copy(w, 0, self.me, self.sibling, True).start()
            for j, chip in enumerate(self.chips):
                self._copy(w, 1 + j, self.me, (*chip, self.c), True).start()

    def forward(self):
        for j, chip in enumerate(self.chips):
            for w in range(len(self.xr)):
                self._copy(w, 1 + j, (*chip, self.c), self.me).wait_recv()
                self._copy(w, 4 + j, (*chip, self.c), self.sibling).start()

    def finish(self):
        for w in range(len(self.xr)):
            self._copy(w, 0, self.sibling, self.me).wait_recv()
            for j, chip in enumerate(self.chips):
                self._copy(w, 4 + j, (*chip, 1 - self.c), self.me).wait_recv()
            self._copy(w, 0, self.me, self.sibling, True).wait_send()
            for j, chip in enumerate(self.chips):
                self._copy(w, 1 + j, self.me, (*chip, self.c), True).wait_send()
                self._copy(w, 4 + j, (*chip, self.c), self.sibling).wait_send()
            self._local(w).wait()


class _Scatter:
    def __init__(self, send_refs, recv_refs, send_sems, recv_sems):
        self.sr, self.rr, self.ss, self.rs = send_refs, recv_refs, send_sems, recv_sems
        self.xyz = lax.axis_index("x"), lax.axis_index("y"), lax.axis_index("c")

    def _copy(self, w, k):
        x, y, c = self.xyz
        px, py, pc = x ^ ((k >> 2) & 1), y ^ ((k >> 1) & 1), c ^ (k & 1)
        return pltpu.make_async_remote_copy(
            src_ref=self.sr[w].at[4 * px + 2 * py + pc], dst_ref=self.rr[w].at[k - 1],
            send_sem=self.ss.at[w, k - 1], recv_sem=self.rs.at[w, k - 1],
            device_id=(px, py, pc), device_id_type=_MESH)

    def _all(self):
        return [self._copy(w, k) for w in range(len(self.sr)) for k in range(1, N_DEV)]

    def start(self):
        for cp in self._all():
            cp.start()

    def finish(self):
        for cp in self._all():
            cp.wait_recv()
        for cp in self._all():
            cp.wait_send()


def _exchange_small(s):
    def body(x_ref, out_ref, x_send, x_recv, local_sem):
        x, y, c = lax.axis_index("x"), lax.axis_index("y"), lax.axis_index("c")
        me = 4 * x + 2 * y + c
        mine = pltpu.make_async_copy(x_ref, out_ref.at[me], local_sem)
        mine.start()

        def copy(k):
            px, py, pc = x ^ ((k >> 2) & 1), y ^ ((k >> 1) & 1), c ^ (k & 1)
            return pltpu.make_async_remote_copy(
                src_ref=x_ref, dst_ref=out_ref.at[me], send_sem=x_send.at[k - 1], recv_sem=x_recv.at[k - 1],
                device_id=(px, py, pc), device_id_type=_MESH)

        cps = [copy(k) for k in range(1, N_DEV)]
        for cp in cps:
            cp.start()
        for cp in cps:
            cp.wait_recv()
        for cp in cps:
            cp.wait_send()
        mine.wait()

    return pl.pallas_call(
        body, name="exchange_small", out_shape=jax.ShapeDtypeStruct((N_DEV,) + s.shape, s.dtype),
        in_specs=[_ANY], out_specs=_ANY,
        scratch_shapes=[pltpu.SemaphoreType.DMA((N_DEV - 1,)), pltpu.SemaphoreType.DMA((N_DEV - 1,)),
                        pltpu.SemaphoreType.DMA],
    )(s)


def _sum_slots(buf, tr=208):
    n, R, _ = buf.shape
    tr = tr if R % tr == 0 else R

    def body(b_ref, o_ref):
        acc = b_ref[0]
        for d in range(1, n):
            acc = acc + b_ref[d]
        o_ref[...] = acc

    return pl.pallas_call(
        body, name="sum_slots", grid=(R // tr,),
        in_specs=[pl.BlockSpec((n, tr, LANES), lambda i: (0, i, 0))],
        out_specs=pl.BlockSpec((tr, LANES), lambda i: (i, 0)),
        out_shape=jax.ShapeDtypeStruct((R, LANES), buf.dtype),
        compiler_params=_params("parallel"),
    )(buf)


def _adamw_math(w, g, m, v):
    m_new = ADAM_B1 * m + (1.0 - ADAM_B1) * g
    v_new = ADAM_B2 * v + (1.0 - ADAM_B2) * (g * g)
    m_hat = m_new / (1.0 - ADAM_B1 ** ADAM_STEP)
    v_hat = v_new / (1.0 - ADAM_B2 ** ADAM_STEP)
    return -ADAM_LR * (m_hat / (jnp.sqrt(v_hat) + ADAM_EPS) + ADAM_WD * w), m_new, v_new


def _adamw(name, w, g, got, m, v, tr=128):
    R, C = w.shape
    tr = max([t for t in range(16, tr + 1, 16) if R % t == 0], default=R) if R > tr else R

    def body(w_ref, g_ref, got_ref, m_ref, v_ref, go_ref, d_ref, mo_ref, vo_ref):
        gv = g_ref[...]
        for k in range(N_DEV - 1):
            gv = gv + got_ref[k].astype(F32)
        go_ref[...] = gv
        d_ref[...], mo_ref[...], vo_ref[...] = _adamw_math(w_ref[...], gv, m_ref[...], v_ref[...])

    spec = pl.BlockSpec((tr, C), lambda i: (i, 0))
    got_spec = pl.BlockSpec((N_DEV - 1, tr, C), lambda i: (0, i, 0))
    sds = jax.ShapeDtypeStruct((R, C), F32)
    return pl.pallas_call(
        body, name=name, grid=(R // tr,), in_specs=[spec, spec, got_spec, spec, spec], out_specs=[spec] * 4,
        out_shape=[sds] * 4, compiler_params=_params("parallel"),
    )(w, g, got, m, v)


def _adamw_small(ws, gs, ms, vs):
    n = len(ws)

    def body(*refs):
        ins, outs = refs[:4 * n], refs[4 * n:]
        for p in range(n):
            w_ref, g_ref, m_ref, v_ref = (ins[q * n + p] for q in range(4))
            res = _adamw_math(w_ref[...], g_ref[...], m_ref[...], v_ref[...])
            for q in range(3):
                outs[q * n + p][...] = res[q]

    sds = [jax.ShapeDtypeStruct(w.shape, F32) for w in ws]
    outs = pl.pallas_call(body, name="adamw_small", out_shape=sds * 3)(*ws, *gs, *ms, *vs)
    return outs[:n], outs[n:2 * n], outs[2 * n:]


def _pad_rows(a, mult):
    r = (-a.shape[0]) % mult
    return a if r == 0 else jnp.concatenate([a, jnp.zeros((r,) + a.shape[1:], a.dtype)], axis=0)


def _as_rows(a, mult):
    flat = a.reshape(-1)
    r = (-flat.shape[0]) % LANES
    if r:
        flat = jnp.concatenate([flat, jnp.zeros((r,), a.dtype)])
    return _pad_rows(flat.reshape(-1, LANES), mult)


def _w_in_to_kernel_cols(w):
    pad = jnp.zeros(w.shape[:-1] + (Z_COLS - IN_COLS,), w.dtype)
    return jnp.concatenate([w[..., :1024], w[..., 1344:1856], w[..., 1024:1344], pad], axis=-1)


def _w_in_from_kernel_cols(w):
    return jnp.concatenate([w[..., :1024], w[..., 1536:1856], w[..., 1024:1536]], axis=-1)


def _dev_blocks(n, g):
    ffs = D_FF // N_DEV
    if n == "w_in":
        return _w_in_from_kernel_cols(g).reshape(N_DEV, D_MODEL // N_DEV, IN_COLS)
    if n == "w_q_up":
        return jnp.transpose(g.reshape(Q_LORA, N_DEV, MLA_QK_PAD)[:, :, :MLA_QK], (1, 0, 2))
    if n == "w_kv_up":
        return jnp.transpose(g.reshape(KV_LORA, N_DEV, MLA_NOPE + MLA_V), (1, 0, 2))
    if n in ("w_mem_kv", "w_o"):
        return g.reshape(N_DEV, D_MODEL // N_DEV, g.shape[1])
    if n in ("w_gate", "w_up"):
        return jnp.transpose(g.reshape(D_MODEL, N_DEV, ffs), (1, 0, 2))
    if n == "conv_w":
        return jnp.transpose(g.reshape(3, N_DEV, ffs), (1, 0, 2))
    assert n == "w_down"
    return g.reshape(N_DEV, ffs, D_MODEL)


def _send_blocks(n, g):
    blocks = _dev_blocks(n, g)
    return blocks if n == "conv_w" else blocks.astype(BF16)


def kernel(x, mem, g_mix, w_in, g_q_lat, w_q_up, g_kv_lat, w_kv_up, g_q_mla, g_k_mla, w_pool, pool_scale, g_mem, w_mem_kv, g_q_x, g_k_x, w_o, g_ffn, w_gate, w_up, conv_w, conv_b, w_down, loss_target, m_g_mix, m_w_in, m_g_q_lat, m_w_q_up, m_g_kv_lat, m_w_kv_up, m_g_q_mla, m_g_k_mla, m_w_pool, m_pool_scale, m_g_mem, m_w_mem_kv, m_g_q_x, m_g_k_x, m_w_o, m_g_ffn, m_w_gate, m_w_up, m_conv_w, m_conv_b, m_w_down, v_g_mix, v_w_in, v_g_q_lat, v_w_q_up, v_g_kv_lat, v_w_kv_up, v_g_q_mla, v_g_k_mla, v_w_pool, v_pool_scale, v_g_mem, v_w_mem_kv, v_g_q_x, v_g_k_x, v_w_o, v_g_ffn, v_w_gate, v_w_up, v_conv_w, v_conv_b, v_w_down):
    given = dict(g_mix=g_mix, w_in=w_in, g_q_lat=g_q_lat, w_q_up=w_q_up, g_kv_lat=g_kv_lat, w_kv_up=w_kv_up,
                 g_q_mla=g_q_mla, g_k_mla=g_k_mla, w_pool=w_pool, pool_scale=pool_scale, g_mem=g_mem,
                 w_mem_kv=w_mem_kv, g_q_x=g_q_x, g_k_x=g_k_x, w_o=w_o, g_ffn=g_ffn, w_gate=w_gate, w_up=w_up,
                 conv_w=conv_w, conv_b=conv_b, w_down=w_down)
    mom_m = dict(g_mix=m_g_mix, w_in=m_w_in, g_q_lat=m_g_q_lat, w_q_up=m_w_q_up, g_kv_lat=m_g_kv_lat,
                 w_kv_up=m_w_kv_up, g_q_mla=m_g_q_mla, g_k_mla=m_g_k_mla, w_pool=m_w_pool,
                 pool_scale=m_pool_scale, g_mem=m_g_mem, w_mem_kv=m_w_mem_kv, g_q_x=m_g_q_x, g_k_x=m_g_k_x,
                 w_o=m_w_o, g_ffn=m_g_ffn, w_gate=m_w_gate, w_up=m_w_up, conv_w=m_conv_w, conv_b=m_conv_b,
                 w_down=m_w_down)
    mom_v = dict(g_mix=v_g_mix, w_in=v_w_in, g_q_lat=v_g_q_lat, w_q_up=v_w_q_up, g_kv_lat=v_g_kv_lat,
                 w_kv_up=v_w_kv_up, g_q_mla=v_g_q_mla, g_k_mla=v_g_k_mla, w_pool=v_w_pool,
                 pool_scale=v_pool_scale, g_mem=v_g_mem, w_mem_kv=v_w_mem_kv, g_q_x=v_g_q_x, g_k_x=v_g_k_x,
                 w_o=v_w_o, g_ffn=v_g_ffn, w_gate=v_w_gate, w_up=v_w_up, conv_w=v_conv_w, conv_b=v_conv_b,
                 w_down=v_w_down)
    drop = lambda a: a[0] if a.ndim > 2 else a
    sh = {n: drop(given[n]) for n in WEIGHTS}
    mom_m = {n: drop(mom_m[n]) for n in WEIGHTS}
    mom_v = {n: drop(mom_v[n]) for n in WEIGHTS}

    cw_hi = sh["conv_w"].astype(BF16)
    cw_r = sh["conv_w"] - cw_hi.astype(F32)
    cw_mid = cw_r.astype(BF16)
    cw_lo = (cw_r - cw_mid.astype(F32)).astype(BF16)
    early_shards = [
        _w_in_to_kernel_cols(sh["w_in"]).astype(BF16),
        jnp.pad(sh["w_q_up"], ((0, 0), (0, MLA_QK_PAD - MLA_QK))).astype(BF16),
        sh["w_kv_up"].astype(BF16),
    ]
    late_shards = [sh["w_gate"].astype(BF16), sh["w_up"].astype(BF16), sh["w_down"].astype(BF16),
                   jnp.concatenate([cw_hi, cw_mid, cw_lo], axis=0), sh["w_o"].astype(BF16),
                   sh["w_mem_kv"].astype(BF16)]
    W = {"w_pool": sh["w_pool"].astype(BF16)}
    for n in ("g_mix", "g_q_lat", "g_kv_lat", "pool_scale", "g_mem", "g_q_x", "g_k_x", "g_ffn", "conv_b"):
        W[n] = sh[n]
    pad_qk = lambda gv: jnp.pad(gv, ((0, 0), (0, MLA_QK_PAD - MLA_QK)))
    W["g_q_mla"], W["g_k_mla"] = pad_qk(sh["g_q_mla"]), pad_qk(sh["g_k_mla"])

    loss_part, grad_x, G, got = _local_step(x[0], mem[0], loss_target[0], W, early_shards, late_shards)

    small = {
        "g_mix": G["g_mix"], "g_q_lat": G["g_q_lat"], "g_kv_lat": G["g_kv_lat"],
        "g_q_mla": G["g_q_mla"][:, :MLA_QK], "g_k_mla": G["g_k_mla"][:, :MLA_QK],
        "w_pool": G["w_pool"], "pool_scale": G["pool_scale"], "g_mem": G["g_mem"],
        "g_q_x": G["g_q_x"], "g_k_x": G["g_k_x"], "g_ffn": G["g_ffn"], "conv_b": G["conv_b"],
    }
    s_offs = {}
    segs = []
    off = 0
    for n in SMALL:
        r = _as_rows(small[n], 8)
        s_offs[n] = off
        off += r.shape[0]
        segs.append(r)
    segs.append(loss_part)
    loss_row = off
    sbuf = jnp.concatenate(segs, axis=0)
    s_sum = _sum_slots(_exchange_small(sbuf))

    def small_take(buf, n):
        shape = sh[n].shape
        cnt = math.prod(shape)
        return buf[s_offs[n]:s_offs[n] + -(-cnt // LANES)].reshape(-1)[:cnt].reshape(shape)

    loss = s_sum[loss_row, 0]

    me = 4 * lax.axis_index("x") + 2 * lax.axis_index("y") + lax.axis_index("c")
    grads, deltas, new_m, new_v = {}, {}, {}, {}
    for n in BIG:
        shape = sh[n].shape
        own = lax.dynamic_index_in_dim(_dev_blocks(n, G[n]), me, 0, keepdims=False)
        flat = lambda a: a.reshape(-1, shape[-1])
        outs = _adamw("adamw_" + n, flat(sh[n]), flat(own), got[n].reshape(N_DEV - 1, -1, shape[-1]),
                      flat(mom_m[n]), flat(mom_v[n]))
        grads[n], deltas[n], new_m[n], new_v[n] = (o.reshape(shape) for o in outs)

    for n in SMALL:
        grads[n] = small_take(s_sum, n)
    d_s, m_s, v_s = _adamw_small([sh[n] for n in SMALL], [grads[n] for n in SMALL],
                                 [mom_m[n] for n in SMALL], [mom_v[n] for n in SMALL])
    for j, n in enumerate(SMALL):
        deltas[n], new_m[n], new_v[n] = d_s[j], m_s[j], v_s[j]

    lead = lambda n, a: a.reshape(given[n].shape)
    return (loss, grad_x[None],
            *[lead(n, grads[n]) for n in WEIGHTS], *[lead(n, deltas[n]) for n in WEIGHTS],
            *[lead(n, new_m[n]) for n in WEIGHTS], *[lead(n, new_v[n]) for n in WEIGHTS])
```
